```python
import jax
import jax.numpy as jnp
from jax import lax
import numpy as np

D_MODEL = 1024
BATCH = 8
SEQ = 2048
DEPTH = 1
DEC_BATCH = 128
DEC_SEQ = 4
PAST_LEN = 16384
PAGE_SIZE = 128

HEAD_DIM = 64
N_ATT_HEADS = 8
N_KV_HEADS = 2
KV_GROUP = N_ATT_HEADS // N_KV_HEADS
WINDOW = 128
ATT_BLOCK = WINDOW
N_RET_HEADS = 4
RET_QK_DIM = 64
RET_V_DIM = 128
RET_CHUNK = 128
N_MEM = 256
N_X_HEADS = 4
X_HEAD_DIM = D_MODEL // N_X_HEADS
X_WIDTH = N_X_HEADS * X_HEAD_DIM
D_FF = 4 * D_MODEL

RMS_EPS = 1e-6
GN_EPS = 1e-5

ATT_Q_W = N_ATT_HEADS * HEAD_DIM
ATT_KV_W = N_KV_HEADS * HEAD_DIM
RET_QK_W = N_RET_HEADS * RET_QK_DIM
RET_V_W = N_RET_HEADS * RET_V_DIM
MIX_OUT = ATT_Q_W + RET_V_W
IN_SIZES = (ATT_Q_W, ATT_KV_W, ATT_KV_W, RET_QK_W, RET_QK_W, RET_V_W, RET_V_W)
D_IN = sum(IN_SIZES)

kernel_name = "hymba_swa_sink_retention_step"

F32 = jnp.float32


def rms_norm(x, g):
    xf = x.astype(F32)
    y = xf * lax.rsqrt(jnp.mean(xf * xf, axis=-1, keepdims=True) + RMS_EPS)
    return (y * g.astype(F32)).astype(x.dtype)


def alibi_slopes():
    s = 2.0 ** (-8.0 * np.arange(1, N_ATT_HEADS + 1) / N_ATT_HEADS)
    return jnp.asarray(s.astype(np.float32)).reshape(N_KV_HEADS, KV_GROUP)


def retention_log_decay():
    g = 1.0 - 2.0 ** (-5.0 - np.arange(N_RET_HEADS))
    return jnp.asarray(np.log(g).astype(np.float32))


def split_projection(z):
    offs = [int(o) for o in np.cumsum(IN_SIZES)[:-1]]
    return jnp.split(z, offs, axis=-1)


def mixer_project(xn, w_in):
    B, L = xn.shape[:2]
    qa, ka, va, qr, kr, vr, gr = split_projection(xn @ w_in)
    ka = ka.reshape(B, L, N_KV_HEADS, HEAD_DIM)
    va = va.reshape(B, L, N_KV_HEADS, HEAD_DIM)
    qr = qr.reshape(B, L, N_RET_HEADS, RET_QK_DIM).astype(F32)
    kr = kr.reshape(B, L, N_RET_HEADS, RET_QK_DIM).astype(F32) * (RET_QK_DIM ** -0.5)
    vr = vr.reshape(B, L, N_RET_HEADS, RET_V_DIM).astype(F32)
    return qa, ka, va, qr, kr, vr, gr


def sink_attention(q, k, v, dist, valid, sinks):
    s = jnp.einsum('bnqhgd,bnkhd->bnhgqk', q, k).astype(F32) * (HEAD_DIM ** -0.5)
    s = s - alibi_slopes()[None, None, :, :, None, None] * dist.astype(F32)[None, None, None, None]
    s = jnp.where(valid[None, :, None, None], s, -jnp.inf)
    sink = sinks.astype(F32).reshape(N_KV_HEADS, KV_GROUP)[None, None, :, :, None, None]
    m = jnp.maximum(jnp.max(s, axis=-1, keepdims=True), sink)
    p = jnp.exp(s - m)
    p = p / (jnp.sum(p, axis=-1, keepdims=True) + jnp.exp(sink - m))
    return jnp.einsum('bnhgqk,bnkhd->bnqhgd', p.astype(v.dtype), v)


def window_attention_prompt(q, k, v, sinks):
    B, S = q.shape[:2]
    nb = S // ATT_BLOCK
    qb = q.reshape(B, nb, ATT_BLOCK, N_KV_HEADS, KV_GROUP, HEAD_DIM)
    kb = k.reshape(B, nb, ATT_BLOCK, N_KV_HEADS, HEAD_DIM)
    vb = v.reshape(B, nb, ATT_BLOCK, N_KV_HEADS, HEAD_DIM)
    pad = ((0, 0), (1, 0), (0, 0), (0, 0), (0, 0))
    kk = jnp.concatenate([jnp.pad(kb[:, :-1], pad), kb], axis=2)
    vv = jnp.concatenate([jnp.pad(vb[:, :-1], pad), vb], axis=2)
    qi = jnp.arange(ATT_BLOCK)[:, None]
    kj = jnp.arange(2 * ATT_BLOCK)[None, :]
    dist = qi + ATT_BLOCK - kj
    key_pos = jnp.arange(nb)[:, None, None] * ATT_BLOCK - ATT_BLOCK + kj[None]
    valid = (dist >= 0)[None] & (dist < WINDOW)[None] & (key_pos >= 0)
    o = sink_attention(qb, kk, vv, dist, valid, sinks)
    return o.reshape(B, S, ATT_Q_W), k[:, -WINDOW:], v[:, -WINDOW:]


def window_attention_sample(q, k, v, buf_k, buf_v, sinks):
    B, L = q.shape[:2]
    kk = jnp.concatenate([buf_k.astype(k.dtype), k], axis=1)
    vv = jnp.concatenate([buf_v.astype(v.dtype), v], axis=1)
    qb = q.reshape(B, 1, L, N_KV_HEADS, KV_GROUP, HEAD_DIM)
    dist = jnp.arange(L)[:, None] + WINDOW - jnp.arange(WINDOW + L)[None, :]
    valid = ((dist >= 0) & (dist < WINDOW))[None]
    o = sink_attention(qb, kk[:, None], vv[:, None], dist, valid, sinks)
    return o.reshape(B, L, ATT_Q_W), kk[:, -WINDOW:], vv[:, -WINDOW:]


def retention_chunk(state, q, k, v, log_g):
    L = q.shape[1]
    idx = jnp.arange(L, dtype=F32)
    diff = idx[:, None] - idx[None, :]
    decay = jnp.where(diff >= 0, jnp.exp(log_g[:, None, None] * jnp.maximum(diff, 0.0)), 0.0)
    inner = jnp.einsum('blhd,bshd->bhls', q, k) * decay[None]
    o = jnp.einsum('bhls,bshe->blhe', inner, v)
    xi = jnp.exp((idx[:, None] + 1.0) * log_g[None, :])
    o = o + jnp.einsum('blhd,bhde->blhe', q, state) * xi[None, :, :, None]
    zeta = jnp.exp((L - 1.0 - idx)[:, None] * log_g[None, :])
    new_state = jnp.exp(log_g * L)[None, :, None, None] * state + jnp.einsum('bshd,sh,bshe->bhde', k, zeta, v)
    return o, new_state


def retention_prompt(q, k, v):
    B, S = q.shape[:2]
    nc = S // RET_CHUNK
    log_g = retention_log_decay()

    def to_chunks(t):
        return jnp.moveaxis(t.reshape(B, nc, RET_CHUNK, *t.shape[2:]), 1, 0)

    def step(state, qkv):
        o, new_state = retention_chunk(state, qkv[0], qkv[1], qkv[2], log_g)
        return new_state, o

    s0 = jnp.zeros((B, N_RET_HEADS, RET_QK_DIM, RET_V_DIM), F32)
    final, o = lax.scan(step, s0, (to_chunks(q), to_chunks(k), to_chunks(v)))
    return jnp.moveaxis(o, 0, 1).reshape(B, S, N_RET_HEADS, RET_V_DIM), final


def mixer_merge(att_o, ret_o, gate, gn_g, gn_b, w_out):
    B, L = att_o.shape[:2]
    mu = jnp.mean(ret_o, axis=-1, keepdims=True)
    var = jnp.mean(jnp.square(ret_o - mu), axis=-1, keepdims=True)
    r = (ret_o - mu) * lax.rsqrt(var + GN_EPS) * gn_g.astype(F32) + gn_b.astype(F32)
    r = r.reshape(B, L, RET_V_W).astype(gate.dtype) * jax.nn.silu(gate)
    return jnp.concatenate([att_o, r.astype(att_o.dtype)], axis=-1) @ w_out


def memory_kv(mem, g_mem, w_xk, w_xv):
    B = mem.shape[0]
    mn = rms_norm(mem, g_mem)
    mk = (mn @ w_xk).reshape(B, N_MEM, N_X_HEADS, X_HEAD_DIM)
    mv = (mn @ w_xv).reshape(B, N_MEM, N_X_HEADS, X_HEAD_DIM)
    return mk, mv


def cross_attend(xn, mk, mv, w_xq, w_xo):
    B, L = xn.shape[:2]
    q = (xn @ w_xq).reshape(B, L, N_X_HEADS, X_HEAD_DIM)
    s = jnp.einsum('bqhd,bkhd->bhqk', q, mk.astype(q.dtype)).astype(F32) * (X_HEAD_DIM ** -0.5)
    p = jax.nn.softmax(s, axis=-1).astype(q.dtype)
    o = jnp.einsum('bhqk,bkhd->bqhd', p, mv.astype(q.dtype))
    return o.reshape(B, L, X_WIDTH) @ w_xo


def sqrelu_mlp(xn, w_up, w_down):
    return jnp.square(jax.nn.relu(xn @ w_up)) @ w_down


def setup_inputs(seed: int = 0) -> dict:
    key = jax.random.key(seed)
    ks = jax.random.split(key, 32)

    def nrm(k, shape, scale):
        return jax.random.normal(k, shape, F32) * scale

    d = DEPTH
    return {
        'x_prompt': nrm(ks[0], (BATCH, SEQ, D_MODEL), 1.0),
        'x_sample': nrm(ks[1], (DEC_BATCH, DEC_SEQ, D_MODEL), 1.0),
        'mem_prompt': nrm(ks[2], (BATCH, N_MEM, D_MODEL), 1.0),
        'cache_win_k': nrm(ks[3], (d, DEC_BATCH, WINDOW, N_KV_HEADS, HEAD_DIM), 1.0),
        'cache_win_v': nrm(ks[4], (d, DEC_BATCH, WINDOW, N_KV_HEADS, HEAD_DIM), 1.0),
        'state_ret': nrm(ks[5], (d, DEC_BATCH, N_RET_HEADS, RET_QK_DIM, RET_V_DIM), 1.0),
        'cache_mem_k': nrm(ks[6], (d, DEC_BATCH, N_MEM, N_X_HEADS, X_HEAD_DIM), 1.0),
        'cache_mem_v': nrm(ks[7], (d, DEC_BATCH, N_MEM, N_X_HEADS, X_HEAD_DIM), 1.0),
        'g_mix': 1.0 + nrm(ks[8], (d, D_MODEL), 0.05),
        'w_in': nrm(ks[9], (d, D_MODEL, D_IN), D_MODEL ** -0.5),
        'attn_sinks': nrm(ks[10], (d, N_ATT_HEADS), 1.0),
        'ret_gn_g': 1.0 + nrm(ks[11], (d, N_RET_HEADS, RET_V_DIM), 0.05),
        'ret_gn_b': nrm(ks[12], (d, N_RET_HEADS, RET_V_DIM), 0.02),
        'w_out': nrm(ks[13], (d, MIX_OUT, D_MODEL), MIX_OUT ** -0.5),
        'g_xattn': 1.0 + nrm(ks[14], (d, D_MODEL), 0.05),
        'g_mem': 1.0 + nrm(ks[15], (d, D_MODEL), 0.05),
        'w_xq': nrm(ks[16], (d, D_MODEL, X_WIDTH), D_MODEL ** -0.5),
        'w_xk': nrm(ks[17], (d, D_MODEL, X_WIDTH), D_MODEL ** -0.5),
        'w_xv': nrm(ks[18], (d, D_MODEL, X_WIDTH), D_MODEL ** -0.5),
        'w_xo': nrm(ks[19], (d, X_WIDTH, D_MODEL), X_WIDTH ** -0.5),
        'g_mlp': 1.0 + nrm(ks[20], (d, D_MODEL), 0.05),
        'w_up': nrm(ks[21], (d, D_MODEL, D_FF), D_MODEL ** -0.5),
        'w_down': nrm(ks[22], (d, D_FF, D_MODEL), D_FF ** -0.5),
        'g_final': 1.0 + nrm(ks[23], (D_MODEL,), 0.05),
    }


def reference(x_prompt, x_sample, mem_prompt, cache_win_k, cache_win_v, state_ret, cache_mem_k, cache_mem_v,
              g_mix, w_in, attn_sinks, ret_gn_g, ret_gn_b, w_out, g_xattn, g_mem, w_xq, w_xk, w_xv, w_xo,
              g_mlp, w_up, w_down, g_final):
    hp, hs = x_prompt, x_sample
    p_wk, p_wv, p_rs, p_mk, p_mv, s_wk, s_wv, s_rs = [], [], [], [], [], [], [], []
    log_g = retention_log_decay()
    for l in range(DEPTH):
        qa, ka, va, qr, kr, vr, gr = mixer_project(rms_norm(hp, g_mix[l]), w_in[l])
        att_o, wk, wv = window_attention_prompt(qa, ka, va, attn_sinks[l])
        ret_o, rs = retention_prompt(qr, kr, vr)
        hp = hp + mixer_merge(att_o, ret_o, gr, ret_gn_g[l], ret_gn_b[l], w_out[l])
        mk, mv = memory_kv(mem_prompt, g_mem[l], w_xk[l], w_xv[l])
        hp = hp + cross_attend(rms_norm(hp, g_xattn[l]), mk, mv, w_xq[l], w_xo[l])
        hp = hp + sqrelu_mlp(rms_norm(hp, g_mlp[l]), w_up[l], w_down[l])
        p_wk.append(wk); p_wv.append(wv); p_rs.append(rs); p_mk.append(mk); p_mv.append(mv)

        qa, ka, va, qr, kr, vr, gr = mixer_project(rms_norm(hs, g_mix[l]), w_in[l])
        att_o, wk, wv = window_attention_sample(qa, ka, va, cache_win_k[l], cache_win_v[l], attn_sinks[l])
        ret_o, rs = retention_chunk(state_ret[l].astype(F32), qr, kr, vr, log_g)
        hs = hs + mixer_merge(att_o, ret_o, gr, ret_gn_g[l], ret_gn_b[l], w_out[l])
        hs = hs + cross_attend(rms_norm(hs, g_xattn[l]), cache_mem_k[l], cache_mem_v[l], w_xq[l], w_xo[l])
        hs = hs + sqrelu_mlp(rms_norm(hs, g_mlp[l]), w_up[l], w_down[l])
        s_wk.append(wk); s_wv.append(wv); s_rs.append(rs)

    y_prompt = rms_norm(hp, g_final)
    y_sample = rms_norm(hs, g_final)
    return (y_prompt, y_sample,
            jnp.stack(p_wk), jnp.stack(p_wv), jnp.stack(p_rs), jnp.stack(p_mk), jnp.stack(p_mv),
            jnp.stack(s_wk), jnp.stack(s_wv), jnp.stack(s_rs))
```

```python
import functools

import jax
import jax.numpy as jnp
import numpy as np
from jax import lax
from jax.experimental import pallas as pl
from jax.experimental.pallas import tpu as pltpu

F32 = jnp.float32
BF16 = jnp.bfloat16

D_MODEL = 1024
BATCH = 8
SEQ = 2048
DEC_BATCH = 128
DEC_SEQ = 4
HEAD_DIM = 64
N_ATT_HEADS = 8
N_KV_HEADS = 2
KV_GROUP = N_ATT_HEADS // N_KV_HEADS
WINDOW = 128
BLK = 128
N_RET_HEADS = 4
RET_QK_DIM = 64
RET_V_DIM = 128
N_MEM = 256
N_X_HEADS = 4
X_HEAD_DIM = D_MODEL // N_X_HEADS
D_FF = 4 * D_MODEL
RMS_EPS = 1e-6
GN_EPS = 1e-5

ATT_Q_W = N_ATT_HEADS * HEAD_DIM
ATT_KV_W = N_KV_HEADS * HEAD_DIM
RET_QK_W = N_RET_HEADS * RET_QK_DIM
RET_V_W = N_RET_HEADS * RET_V_DIM
MIX_OUT = ATT_Q_W + RET_V_W
D_IN = ATT_Q_W + 2 * ATT_KV_W + 2 * RET_QK_W + 2 * RET_V_W
C_QA, C_KV, C_QKR, C_VR, C_GR = 0, 512, 768, 1280, 1792

LANES = 128
SUBLANES = 8
HALF = LANES // 2
NEG = -1e30
VMEM_LIMIT = 52 * 1024 * 1024

TM_MIX = 512
TM_X = 512
TM_MLP = 512
FF_CHUNK = 1024
BB_MIX = 16
BB_X = 4

NEG_SLOPES = [-(2.0 ** (-8.0 * (i + 1) / N_ATT_HEADS)) for i in range(N_ATT_HEADS)]
_LOG_G = np.log(1.0 - 2.0 ** (-5.0 - np.arange(N_RET_HEADS))).astype(np.float32).astype(np.float64)


def _prompt_tables():
    qi = np.arange(BLK)[:, None]
    kj = np.arange(2 * BLK)[None, :]
    dist = (qi + BLK - kj).astype(np.float64)
    mask = np.where((dist >= 0) & (dist < WINDOW), 0.0, NEG)
    l = np.arange(BLK, dtype=np.float64)
    diff = l[:, None] - l[None, :]
    decay = np.where(diff >= 0, np.exp(_LOG_G[:, None, None] * np.maximum(diff, 0.0)), 0.0)
    xi = np.exp((l[:, None] + 1.0) * _LOG_G[None, :])
    zeta = np.exp((BLK - 1.0 - l)[:, None] * _LOG_G[None, :])
    xi_t = np.repeat(xi, RET_V_DIM, axis=1)
    zeta_t = np.repeat(zeta, RET_QK_DIM, axis=1)
    f = lambda a: np.asarray(a, np.float32)
    return f(dist), f(mask), f(decay), f(xi_t), f(zeta_t)


def _sample_tables():
    slopes = -np.asarray(NEG_SLOPES)
    bias = np.full((2, N_ATT_HEADS * SUBLANES, 2 * BLK), NEG, np.float64)
    dec = np.zeros((2, N_RET_HEADS * SUBLANES, BLK), np.float64)
    xi = np.zeros((2, N_RET_HEADS * SUBLANES, RET_V_DIM), np.float64)
    zeta = np.zeros((2, SUBLANES, RET_QK_W), np.float64)
    for par in range(2):
        for r in range(SUBLANES):
            own = DEC_SEQ * par <= r < DEC_SEQ * (par + 1)
            t = r - DEC_SEQ * par if own else r % DEC_SEQ
            for h in range(N_ATT_HEADS):
                row = h * SUBLANES + r
                for j in range(WINDOW):
                    d = t + WINDOW - j
                    if 0 <= d < WINDOW:
                        bias[par, row, j] = -slopes[h] * d
                for c in range(DEC_SEQ):
                    d = t - c
                    if d >= 0:
                        bias[par, row, WINDOW + DEC_SEQ * par + c] = -slopes[h] * d
            for h in range(N_RET_HEADS):
                row = h * SUBLANES + r
                if own:
                    xi[par, row, :] = np.exp((t + 1.0) * _LOG_G[h])
                    zeta[par, r, h * RET_QK_DIM:(h + 1) * RET_QK_DIM] = np.exp((DEC_SEQ - 1.0 - t) * _LOG_G[h])
                    for c in range(t + 1):
                        dec[par, row, DEC_SEQ * par + c] = np.exp(_LOG_G[h] * (t - c))
    f = lambda a: np.asarray(a, np.float32)
    return f(bias), f(dec), f(xi), f(zeta)


_P_DIST, _P_MASK, _P_DECAY, _P_XI, _P_ZETA = _prompt_tables()
_S_BIAS, _S_DEC, _S_XI, _S_ZETA = _sample_tables()
_GL_PROMPT = [float(np.exp(_LOG_G[h] * BLK)) for h in range(N_RET_HEADS)]
_GL_SAMPLE = [float(np.exp(_LOG_G[h] * DEC_SEQ)) for h in range(N_RET_HEADS)]


def _rms(x, g):
    return x * lax.rsqrt(jnp.mean(x * x, axis=-1, keepdims=True) + RMS_EPS) * g


def _dot(a, b):
    return jnp.dot(a, b, preferred_element_type=F32)


def _dot_nt(a, b):
    return lax.dot_general(a, b, (((1,), (1,)), ((), ())), preferred_element_type=F32)


def _dot_tn(a, b):
    return lax.dot_general(a, b, (((0,), (0,)), ((), ())), preferred_element_type=F32)


def _silu(g):
    return g * (1.0 / (1.0 + jnp.exp(-g)))


def _half_masks(width):
    lane = lax.broadcasted_iota(jnp.int32, (1, width), 1)
    lo = ((lane & (LANES - 1)) < HALF).astype(F32)
    return lo, 1.0 - lo


def _sink_softmax(s, sink):
    m = jnp.maximum(jnp.max(s, axis=-1, keepdims=True), sink)
    p = jnp.exp(s - m)
    den = jnp.sum(p, axis=-1, keepdims=True) + jnp.exp(sink - m)
    return p * (1.0 / den)


def _group_norm(o, g, b):
    mu = jnp.mean(o, axis=-1, keepdims=True)
    d = o - mu
    var = jnp.mean(d * d, axis=-1, keepdims=True)
    return d * lax.rsqrt(var + GN_EPS) * g + b


def _prompt_mixer_kernel(sinks_ref, x_ref, gmix_ref, win_ref, wout_ref, gng_ref, gnb_ref,
                         dist_ref, mask_ref, decay_ref, xi_ref, zeta_ref,
                         h_ref, wk_ref, wv_ref, st_ref,
                         qlo_s, qhi_s, kd0_s, kd1_s, vd0_s, vd1_s,
                         qrlo_s, qrhi_s, kr_s, vr_s, gate_s, mix_s):
    t = pl.program_id(1)
    nt = pl.num_programs(1)
    tm = x_ref.shape[1]
    nblk = tm // BLK

    @pl.when(t == 0)
    def _():
        zero = jnp.zeros((BLK, LANES), BF16)
        kd0_s[0:BLK, :] = zero
        kd1_s[0:BLK, :] = zero
        vd0_s[0:BLK, :] = zero
        vd1_s[0:BLK, :] = zero
        st_ref[...] = jnp.zeros_like(st_ref)

    x = x_ref[0]
    xn = _rms(x, gmix_ref[...]).astype(BF16)

    lo512, hi512 = _half_masks(ATT_Q_W)
    q = _dot(xn, win_ref[:, C_QA:C_QA + ATT_Q_W]) * (HEAD_DIM ** -0.5)
    qlo_s[...] = (q * lo512).astype(BF16)
    qhi_s[...] = (q * hi512).astype(BF16)

    lo128, hi128 = _half_masks(LANES)
    kv = _dot(xn, win_ref[:, C_KV:C_KV + 2 * ATT_KV_W])
    k = kv[:, :ATT_KV_W]
    v = kv[:, ATT_KV_W:]
    k_r = pltpu.roll(k, HALF, axis=1)
    v_r = pltpu.roll(v, HALF, axis=1)
    kd0_s[BLK:BLK + tm, :] = (k * lo128 + k_r * hi128).astype(BF16)
    kd1_s[BLK:BLK + tm, :] = (k_r * lo128 + k * hi128).astype(BF16)
    vd0_s[BLK:BLK + tm, :] = (v * lo128 + v_r * hi128).astype(BF16)
    vd1_s[BLK:BLK + tm, :] = (v_r * lo128 + v * hi128).astype(BF16)

    @pl.when(t == nt - 1)
    def _():
        wk_ref[0] = k[tm - WINDOW:, :]
        wv_ref[0] = v[tm - WINDOW:, :]

    lo256, hi256 = _half_masks(RET_QK_W)
    qkr = _dot(xn, win_ref[:, C_QKR:C_QKR + 2 * RET_QK_W])
    qr = qkr[:, :RET_QK_W]
    qrlo_s[...] = (qr * lo256).astype(BF16)
    qrhi_s[...] = (qr * hi256).astype(BF16)
    kr_s[...] = qkr[:, RET_QK_W:] * (RET_QK_DIM ** -0.5)
    vr_s[...] = _dot(xn, win_ref[:, C_VR:C_VR + RET_V_W]).astype(BF16)
    gate_s[...] = _silu(_dot(xn, win_ref[:, C_GR:C_GR + RET_V_W]))

    lane = lax.broadcasted_iota(jnp.int32, (BLK, LANES), 1)
    col = lax.broadcasted_iota(jnp.int32, (BLK, 2 * BLK), 1)
    kd_refs = (kd0_s, kd1_s)
    vd_refs = (vd0_s, vd1_s)

    def block(j, carry):
        r0 = pl.multiple_of(j * BLK, BLK)
        rows = pl.ds(r0, BLK)
        krows = pl.ds(r0, 2 * BLK)
        first = jnp.where(t * nblk + j == 0, NEG, 0.0)
        mask_n = mask_ref[...] + jnp.where(col < BLK, first, 0.0)
        dist = dist_ref[...]

        for kvh in range(N_KV_HEADS):
            kd = kd_refs[kvh][krows, :]
            vd = vd_refs[kvh][krows, :]
            c0 = kvh * KV_GROUP * HEAD_DIM
            qst = jnp.concatenate([qlo_s[rows, c0:c0 + LANES], qhi_s[rows, c0:c0 + LANES],
                                   qlo_s[rows, c0 + LANES:c0 + 2 * LANES],
                                   qhi_s[rows, c0 + LANES:c0 + 2 * LANES]], axis=0)
            s = _dot_nt(qst, kd)
            ps = []
            for g in range(KV_GROUP):
                h = kvh * KV_GROUP + g
                sg = s[g * BLK:(g + 1) * BLK] + (NEG_SLOPES[h] * dist + mask_n)
                ps.append(_sink_softmax(sg, sinks_ref[h]).astype(BF16))
            o = _dot(jnp.concatenate(ps, axis=0), vd)
            oa = jnp.where(lane < HALF, o[0:BLK], o[BLK:2 * BLK])
            ob = jnp.where(lane < HALF, o[2 * BLK:3 * BLK], o[3 * BLK:4 * BLK])
            mix_s[rows, c0:c0 + LANES] = oa.astype(BF16)
            mix_s[rows, c0 + LANES:c0 + 2 * LANES] = ob.astype(BF16)

        for i in range(N_RET_HEADS // 2):
            lsl = slice(i * LANES, (i + 1) * LANES)
            kp = kr_s[rows, lsl]
            kp_b = kp.astype(BF16)
            sp = st_ref[0, lsl, :]
            sp_b = sp.astype(BF16)
            vpair = vr_s[rows, 2 * i * RET_V_DIM:(2 * i + 2) * RET_V_DIM]
            for half, q_s in enumerate((qrlo_s, qrhi_s)):
                h = 2 * i + half
                vsl = slice(h * RET_V_DIM, (h + 1) * RET_V_DIM)
                qm = q_s[rows, lsl]
                inner = (_dot_nt(qm, kp_b) * decay_ref[h]).astype(BF16)
                o = _dot(inner, vpair[:, half * RET_V_DIM:(half + 1) * RET_V_DIM])
                o = o + _dot(qm, sp_b) * xi_ref[:, vsl]
                r = _group_norm(o, gng_ref[:, vsl], gnb_ref[:, vsl]) * gate_s[rows, vsl]
                mix_s[rows, ATT_Q_W + h * RET_V_DIM:ATT_Q_W + (h + 1) * RET_V_DIM] = r.astype(BF16)
            kz = (kp * zeta_ref[:, lsl]).astype(BF16)
            u = _dot_tn(kz, vpair)
            st_ref[0, i * LANES:i * LANES + RET_QK_DIM, :] = (
                _GL_PROMPT[2 * i] * sp[:RET_QK_DIM] + u[:RET_QK_DIM, :RET_V_DIM])
            st_ref[0, i * LANES + RET_QK_DIM:(i + 1) * LANES, :] = (
                _GL_PROMPT[2 * i + 1] * sp[RET_QK_DIM:] + u[RET_QK_DIM:, RET_V_DIM:])
        return carry

    lax.fori_loop(0, nblk, block, 0)

    kd0_s[0:BLK, :] = kd0_s[tm:tm + BLK, :]
    kd1_s[0:BLK, :] = kd1_s[tm:tm + BLK, :]
    vd0_s[0:BLK, :] = vd0_s[tm:tm + BLK, :]
    vd1_s[0:BLK, :] = vd1_s[tm:tm + BLK, :]

    h_ref[0] = x + _dot(mix_s[...], wout_ref[...])


def _prompt_mixer(x, g_mix, w_in, w_out, sinks, gn_g, gn_b):
    b, s, d = x.shape
    tm = TM_MIX
    const = lambda shape: pl.BlockSpec(shape, lambda i, j: (0,) * len(shape))
    return pl.pallas_call(
        _prompt_mixer_kernel,
        grid=(b, s // tm),
        in_specs=[
            pl.BlockSpec(memory_space=pltpu.SMEM),
            pl.BlockSpec((1, tm, d), lambda i, j: (i, j, 0)),
            const((1, d)), const((d, D_IN)), const((MIX_OUT, d)),
            const((1, RET_V_W)), const((1, RET_V_W)),
            const((BLK, 2 * BLK)), const((BLK, 2 * BLK)),
            const((N_RET_HEADS, BLK, BLK)), const((BLK, RET_V_W)), const((BLK, RET_QK_W)),
        ],
        out_specs=[
            pl.BlockSpec((1, tm, d), lambda i, j: (i, j, 0)),
            pl.BlockSpec((1, WINDOW, ATT_KV_W), lambda i, j: (i, 0, 0)),
            pl.BlockSpec((1, WINDOW, ATT_KV_W), lambda i, j: (i, 0, 0)),
            pl.BlockSpec((1, RET_QK_W, RET_V_DIM), lambda i, j: (i, 0, 0)),
        ],
        out_shape=[
            jax.ShapeDtypeStruct((b, s, d), F32),
            jax.ShapeDtypeStruct((b, WINDOW, ATT_KV_W), F32),
            jax.ShapeDtypeStruct((b, WINDOW, ATT_KV_W), F32),
            jax.ShapeDtypeStruct((b, RET_QK_W, RET_V_DIM), F32),
        ],
        scratch_shapes=[
            pltpu.VMEM((tm, ATT_Q_W), BF16), pltpu.VMEM((tm, ATT_Q_W), BF16),
            pltpu.VMEM((tm + BLK, LANES), BF16), pltpu.VMEM((tm + BLK, LANES), BF16),
            pltpu.VMEM((tm + BLK, LANES), BF16), pltpu.VMEM((tm + BLK, LANES), BF16),
            pltpu.VMEM((tm, RET_QK_W), BF16), pltpu.VMEM((tm, RET_QK_W), BF16),
            pltpu.VMEM((tm, RET_QK_W), F32), pltpu.VMEM((tm, RET_V_W), BF16),
            pltpu.VMEM((tm, RET_V_W), F32), pltpu.VMEM((tm, MIX_OUT), BF16),
        ],
        compiler_params=pltpu.CompilerParams(
            dimension_semantics=("arbitrary", "arbitrary"), vmem_limit_bytes=VMEM_LIMIT),
        name="prompt_mixer",
    )(sinks, x, g_mix, w_in, w_out, gn_g, gn_b,
      jnp.asarray(_P_DIST), jnp.asarray(_P_MASK), jnp.asarray(_P_DECAY), jnp.asarray(_P_XI),
      jnp.asarray(_P_ZETA))


def _memkv_kernel(mem_ref, g_ref, wk_ref, wv_ref, mk_ref, mv_ref, mkb_ref, mvb_ref):
    mn = _rms(mem_ref[...], g_ref[...]).astype(BF16)
    mk = _dot(mn, wk_ref[...])
    mv = _dot(mn, wv_ref[...])
    mk_ref[...] = mk
    mv_ref[...] = mv
    mkb_ref[...] = mk.astype(BF16)
    mvb_ref[...] = mv.astype(BF16)


def _memory_kv(mem2d, g_mem, w_xk, w_xv):
    n, d = mem2d.shape
    tm = 512
    row = pl.BlockSpec((tm, d), lambda i: (i, 0))
    const = lambda shape: pl.BlockSpec(shape, lambda i: (0,) * len(shape))
    return pl.pallas_call(
        _memkv_kernel,
        grid=(n // tm,),
        in_specs=[row, const((1, d)), const((d, d)), const((d, d))],
        out_specs=[row, row, row, row],
        out_shape=[jax.ShapeDtypeStruct((n, d), F32), jax.ShapeDtypeStruct((n, d), F32),
                   jax.ShapeDtypeStruct((n, d), BF16), jax.ShapeDtypeStruct((n, d), BF16)],
        compiler_params=pltpu.CompilerParams(
            dimension_semantics=("arbitrary",), vmem_limit_bytes=VMEM_LIMIT),
        name="memory_kv",
    )(mem2d, g_mem, w_xk, w_xv)


def _prompt_xattn_kernel(h_ref, g_ref, wq_ref, wo_ref, mk_ref, mv_ref, out_ref, o_s):
    h = h_ref[0]
    xn = _rms(h, g_ref[...]).astype(BF16)
    q = (_dot(xn, wq_ref[...]) * (X_HEAD_DIM ** -0.5)).astype(BF16)
    for hd in range(N_X_HEADS):
        sl = slice(hd * X_HEAD_DIM, (hd + 1) * X_HEAD_DIM)
        s = _dot_nt(q[:, sl], mk_ref[0, :, sl])
        m = jnp.max(s, axis=-1, keepdims=True)
        p = jnp.exp(s - m)
        p = p * (1.0 / jnp.sum(p, axis=-1, keepdims=True))
        o_s[:, sl] = _dot(p.astype(BF16), mv_ref[0, :, sl]).astype(BF16)
    out_ref[0] = h + _dot(o_s[...], wo_ref[...])


def _prompt_xattn(h, g, w_xq, w_xo, mkb, mvb):
    b, s, d = h.shape
    tm = TM_X
    const = lambda shape: pl.BlockSpec(shape, lambda i, j: (0,) * len(shape))
    tok = pl.BlockSpec((1, tm, d), lambda i, j: (i, j, 0))
    mem = pl.BlockSpec((1, N_MEM, d), lambda i, j: (i, 0, 0))
    return pl.pallas_call(
        _prompt_xattn_kernel,
        grid=(b, s // tm),
        in_specs=[tok, const((1, d)), const((d, d)), const((d, d)), mem, mem],
        out_specs=tok,
        out_shape=jax.ShapeDtypeStruct((b, s, d), F32),
        scratch_shapes=[pltpu.VMEM((tm, d), BF16)],
        compiler_params=pltpu.CompilerParams(
            dimension_semantics=("arbitrary", "arbitrary"), vmem_limit_bytes=VMEM_LIMIT),
        name="prompt_xattn",
    )(h, g, w_xq, w_xo, mkb, mvb)


def _mlp_kernel(h_ref, g_ref, wup_ref, wdn_ref, gf_ref, y_ref):
    h = h_ref[...]
    xn = _rms(h, g_ref[...]).astype(BF16)
    acc = h
    for c in range(D_FF // FF_CHUNK):
        sl = slice(c * FF_CHUNK, (c + 1) * FF_CHUNK)
        u = jnp.maximum(_dot(xn, wup_ref[:, sl]), 0.0)
        acc = acc + _dot((u * u).astype(BF16), wdn_ref[sl, :])
    y_ref[...] = _rms(acc, gf_ref[...])


def _mlp_final(h2d, g_mlp, w_up, w_down, g_final):
    n, d = h2d.shape
    tm = TM_MLP
    row = pl.BlockSpec((tm, d), lambda i: (i, 0))
    const = lambda shape: pl.BlockSpec(shape, lambda i: (0,) * len(shape))
    return pl.pallas_call(
        _mlp_kernel,
        grid=(n // tm,),
        in_specs=[row, const((1, d)), const((d, D_FF)), const((D_FF, d)), const((1, d))],
        out_specs=row,
        out_shape=jax.ShapeDtypeStruct((n, d), F32),
        compiler_params=pltpu.CompilerParams(
            dimension_semantics=("arbitrary",), vmem_limit_bytes=VMEM_LIMIT),
        name="mlp_final",
    )(h2d, g_mlp, w_up, w_down, g_final)


def _sample_mixer_kernel(sinks_ref, x_ref, gmix_ref, win_ref, wout_ref, gng_ref, gnb_ref,
                         ck_ref, cv_ref, st_ref, bias_ref, dec_ref, xi_ref, zeta_ref,
                         h_ref, swk_ref, swv_ref, sst_ref):
    bb = ck_ref.shape[0]
    nt = bb // 2
    x = x_ref[...]
    xn = _rms(x, gmix_ref[...]).astype(BF16)
    tile3 = lambda a: a.reshape(nt, SUBLANES, a.shape[-1])

    q = _dot(xn, win_ref[:, C_QA:C_QA + ATT_Q_W]) * (HEAD_DIM ** -0.5)
    kv = _dot(xn, win_ref[:, C_KV:C_KV + 2 * ATT_KV_W])
    qkr = _dot(xn, win_ref[:, C_QKR:C_QKR + 2 * RET_QK_W])
    vr = _dot(xn, win_ref[:, C_VR:C_VR + RET_V_W])
    gate3 = tile3(_silu(_dot(xn, win_ref[:, C_GR:C_GR + RET_V_W])))

    lo512, hi512 = _half_masks(ATT_Q_W)
    q_r = pltpu.roll(q, HALF, axis=1)
    q_nat3 = tile3(q)
    q_rot3 = tile3(q_r)
    lo3 = lo512.reshape(1, 1, ATT_Q_W)
    hi3 = hi512.reshape(1, 1, ATT_Q_W)
    qa3 = (q_nat3 * lo3).astype(BF16)
    qb3 = (q_rot3 * lo3).astype(BF16)
    qc3 = (q_rot3 * hi3).astype(BF16)
    qd3 = (q_nat3 * hi3).astype(BF16)
    t128 = lambda a, i: a[:, :, i * LANES:(i + 1) * LANES]
    qs = jnp.concatenate([t128(qa3, 0), t128(qb3, 1), t128(qa3, 1), t128(qb3, 2),
                          t128(qc3, 2), t128(qd3, 2), t128(qc3, 3), t128(qd3, 3)], axis=1)

    k3 = tile3(kv[:, :ATT_KV_W])
    v3 = tile3(kv[:, ATT_KV_W:])
    pad_kv = jnp.zeros((nt, BLK - SUBLANES, LANES), BF16)

    lo256, _ = _half_masks(RET_QK_W)
    qr3 = tile3(qkr[:, :RET_QK_W])
    kr3 = tile3(qkr[:, RET_QK_W:] * (RET_QK_DIM ** -0.5))
    vr3 = tile3(vr)
    lane256 = lax.broadcasted_iota(jnp.int32, (1, 1, RET_QK_W), 2)
    qrs = jnp.concatenate(
        [(qr3 * ((lane256 >= h * RET_QK_DIM) & (lane256 < (h + 1) * RET_QK_DIM)).astype(F32)).astype(BF16)
         for h in range(N_RET_HEADS)],
        axis=1)
    kr_pad = jnp.concatenate([kr3.astype(BF16), jnp.zeros((nt, BLK - SUBLANES, RET_QK_W), BF16)], axis=1)
    vr_pad = jnp.concatenate([vr3.astype(BF16), jnp.zeros((nt, BLK - SUBLANES, RET_V_W), BF16)], axis=1)

    lane = lax.broadcasted_iota(jnp.int32, (1, 1, LANES), 2)
    row8 = lax.broadcasted_iota(jnp.int32, (1, SUBLANES, 1), 1)
    bmm_nt = lambda a, b: jnp.einsum('bqd,bkd->bqk', a, b, preferred_element_type=F32)
    bmm = lambda a, b: jnp.einsum('bqk,bkd->bqd', a, b, preferred_element_type=F32)

    att_par, ret_par = [], []
    for par in range(2):
        bsl = pl.ds(par, nt, stride=2)
        ck = ck_ref[bsl]
        cv = cv_ref[bsl]
        swk_ref[bsl, 0:WINDOW - DEC_SEQ, :] = ck[:, DEC_SEQ:, :]
        swv_ref[bsl, 0:WINDOW - DEC_SEQ, :] = cv[:, DEC_SEQ:, :]
        swk_ref[bsl, WINDOW - DEC_SEQ:WINDOW, :] = k3[:, DEC_SEQ * par:DEC_SEQ * (par + 1), :]
        swv_ref[bsl, WINDOW - DEC_SEQ:WINDOW, :] = v3[:, DEC_SEQ * par:DEC_SEQ * (par + 1), :]

        kfull = jnp.concatenate([ck.astype(BF16), k3.astype(BF16), pad_kv], axis=1)
        vfull = jnp.concatenate([cv.astype(BF16), v3.astype(BF16), pad_kv], axis=1)
        s = bmm_nt(qs, kfull) + bias_ref[par]
        ps = []
        for h in range(N_ATT_HEADS):
            ps.append(_sink_softmax(s[:, h * SUBLANES:(h + 1) * SUBLANES, :], sinks_ref[h]).astype(BF16))
        o = bmm(jnp.concatenate(ps, axis=1), vfull)
        o_r = pltpu.roll(o.reshape(nt * N_ATT_HEADS * SUBLANES, LANES), HALF, axis=1).reshape(o.shape)
        hr = lambda a, h: a[:, h * SUBLANES:(h + 1) * SUBLANES, :]
        low = lane < HALF
        att_par.append(jnp.concatenate([
            jnp.where(low, hr(o, 0), hr(o_r, 1)), jnp.where(low, hr(o, 2), hr(o_r, 3)),
            jnp.where(low, hr(o_r, 4), hr(o, 5)), jnp.where(low, hr(o_r, 6), hr(o, 7))], axis=2))

        st = st_ref[bsl]
        oc = bmm(qrs, st.astype(BF16))
        inner = (bmm_nt(qrs, kr_pad) * dec_ref[par]).astype(BF16)
        oi = bmm(inner, vr_pad)
        rs = []
        for h in range(N_RET_HEADS):
            vsl = slice(h * RET_V_DIM, (h + 1) * RET_V_DIM)
            rsl = slice(h * SUBLANES, (h + 1) * SUBLANES)
            o_h = oi[:, rsl, vsl] + oc[:, rsl, :] * xi_ref[par, rsl, :]
            rs.append(_group_norm(o_h, gng_ref[:, vsl], gnb_ref[:, vsl]) * gate3[:, :, vsl])
        ret_par.append(jnp.concatenate(rs, axis=2))

        kz3 = (kr3 * zeta_ref[par]).astype(BF16)
        vr3_b = vr3.astype(BF16)
        for p in range(nt):
            u = _dot_tn(kz3[p], vr3_b[p])
            for h in range(N_RET_HEADS):
                dsl = slice(h * RET_QK_DIM, (h + 1) * RET_QK_DIM)
                sst_ref[2 * p + par, dsl, :] = (
                    _GL_SAMPLE[h] * st[p, dsl, :] + u[dsl, h * RET_V_DIM:(h + 1) * RET_V_DIM])

    own0 = row8 < DEC_SEQ
    att3 = jnp.where(own0, att_par[0], att_par[1])
    ret3 = jnp.where(own0, ret_par[0], ret_par[1])
    mix = jnp.concatenate([att3, ret3], axis=2).reshape(2 * nt * DEC_SEQ, MIX_OUT).astype(BF16)
    h_ref[...] = x + _dot(mix, wout_ref[...])


def _sample_mixer(x2d, g_mix, w_in, w_out, sinks, gn_g, gn_b, ck, cv, st):
    n, d = x2d.shape
    nb = ck.shape[0]
    bb = BB_MIX
    r = bb * DEC_SEQ
    const = lambda shape: pl.BlockSpec(shape, lambda i: (0,) * len(shape))
    row = pl.BlockSpec((r, d), lambda i: (i, 0))
    win = pl.BlockSpec((bb, WINDOW, ATT_KV_W), lambda i: (i, 0, 0))
    state = pl.BlockSpec((bb, RET_QK_W, RET_V_DIM), lambda i: (i, 0, 0))
    return pl.pallas_call(
        _sample_mixer_kernel,
        grid=(nb // bb,),
        in_specs=[
            pl.BlockSpec(memory_space=pltpu.SMEM),
            row, const((1, d)), const((d, D_IN)), const((MIX_OUT, d)),
            const((1, RET_V_W)), const((1, RET_V_W)),
            win, win, state,
            const(_S_BIAS.shape), const(_S_DEC.shape), const(_S_XI.shape), const(_S_ZETA.shape),
        ],
        out_specs=[row, win, win, state],
        out_shape=[
            jax.ShapeDtypeStruct((n, d), F32),
            jax.ShapeDtypeStruct((nb, WINDOW, ATT_KV_W), F32),
            jax.ShapeDtypeStruct((nb, WINDOW, ATT_KV_W), F32),
            jax.ShapeDtypeStruct((nb, RET_QK_W, RET_V_DIM), F32),
        ],
        compiler_params=pltpu.CompilerParams(
            dimension_semantics=("arbitrary",), vmem_limit_bytes=VMEM_LIMIT),
        name="sample_mixer",
    )(sinks, x2d, g_mix, w_in, w_out, gn_g, gn_b, ck, cv, st,
      jnp.asarray(_S_BIAS), jnp.asarray(_S_DEC), jnp.asarray(_S_XI), jnp.asarray(_S_ZETA))


def _sample_xattn_kernel(h_ref, g_ref, wq_ref, wo_ref, mk_ref, mv_ref, out_ref):
    bb = mk_ref.shape[0]
    nt = bb // 2
    h = h_ref[...]
    xn = _rms(h, g_ref[...]).astype(BF16)
    q = _dot(xn, wq_ref[...]) * (X_HEAD_DIM ** -0.5)
    own0 = lax.broadcasted_iota(jnp.int32, (SUBLANES, 1), 0) < DEC_SEQ
    o_tiles = []
    for t in range(nt):
        qt = q[t * SUBLANES:(t + 1) * SUBLANES].astype(BF16)
        o_par = []
        for par in range(2):
            b = 2 * t + par
            os_ = []
            for hd in range(N_X_HEADS):
                sl = slice(hd * X_HEAD_DIM, (hd + 1) * X_HEAD_DIM)
                s = _dot_nt(qt[:, sl], mk_ref[b, :, sl].astype(BF16))
                m = jnp.max(s, axis=-1, keepdims=True)
                p = jnp.exp(s - m)
                p = (p * (1.0 / jnp.sum(p, axis=-1, keepdims=True))).astype(BF16)
                os_.append(_dot(p, mv_ref[b, :, sl].astype(BF16)))
            o_par.append(jnp.concatenate(os_, axis=1))
        o_tiles.append(jnp.where(own0, o_par[0], o_par[1]))
    o = jnp.concatenate(o_tiles, axis=0).astype(BF16)
    out_ref[...] = h + _dot(o, wo_ref[...])


def _sample_xattn(h2d, g, w_xq, w_xo, mk, mv):
    n, d = h2d.shape
    nb = mk.shape[0]
    bb = BB_X
    r = bb * DEC_SEQ
    const = lambda shape: pl.BlockSpec(shape, lambda i: (0,) * len(shape))
    row = pl.BlockSpec((r, d), lambda i: (i, 0))
    mem = pl.BlockSpec((bb, N_MEM, d), lambda i: (i, 0, 0))
    return pl.pallas_call(
        _sample_xattn_kernel,
        grid=(nb // bb,),
        in_specs=[row, const((1, d)), const((d, d)), const((d, d)), mem, mem],
        out_specs=row,
        out_shape=jax.ShapeDtypeStruct((n, d), F32),
        compiler_params=pltpu.CompilerParams(
            dimension_semantics=("arbitrary",), vmem_limit_bytes=VMEM_LIMIT),
        name="sample_xattn",
    )(h2d, g, w_xq, w_xo, mk, mv)


def kernel(x_prompt, x_sample, mem_prompt, cache_win_k, cache_win_v, state_ret, cache_mem_k, cache_mem_v,
           g_mix, w_in, attn_sinks, ret_gn_g, ret_gn_b, w_out, g_xattn, g_mem, w_xq, w_xk, w_xv, w_xo,
           g_mlp, w_up, w_down, g_final):
    depth = w_in.shape[0]
    assert depth == 1, "single-layer trunk"
    b, s, d = x_prompt.shape
    nb, ls, _ = x_sample.shape
    row = lambda a: a.reshape(1, -1)
    bf = lambda a: a.astype(BF16)

    w_in_b, w_out_b = bf(w_in[0]), bf(w_out[0])
    w_xq_b, w_xk_b, w_xv_b, w_xo_b = bf(w_xq[0]), bf(w_xk[0]), bf(w_xv[0]), bf(w_xo[0])
    w_up_b, w_dn_b = bf(w_up[0]), bf(w_down[0])
    sinks = attn_sinks[0]
    gn_g, gn_b = row(ret_gn_g[0]), row(ret_gn_b[0])
    g_fin = row(g_final)

    hp, p_wk, p_wv, p_rs = _prompt_mixer(x_prompt, row(g_mix[0]), w_in_b, w_out_b, sinks, gn_g, gn_b)
    mk, mv, mkb, mvb = _memory_kv(mem_prompt.reshape(b * N_MEM, d), row(g_mem[0]), w_xk_b, w_xv_b)
    hp = _prompt_xattn(hp, row(g_xattn[0]), w_xq_b, w_xo_b,
                       mkb.reshape(b, N_MEM, d), mvb.reshape(b, N_MEM, d))
    y_prompt = _mlp_final(hp.reshape(b * s, d), row(g_mlp[0]), w_up_b, w_dn_b, g_fin).reshape(b, s, d)

    hs, s_wk, s_wv, s_rs = _sample_mixer(
        x_sample.reshape(nb * ls, d), row(g_mix[0]), w_in_b, w_out_b, sinks, gn_g, gn_b,
        cache_win_k[0].reshape(nb, WINDOW, ATT_KV_W), cache_win_v[0].reshape(nb, WINDOW, ATT_KV_W),
        state_ret[0].reshape(nb, RET_QK_W, RET_V_DIM))
    hs = _sample_xattn(hs, row(g_xattn[0]), w_xq_b, w_xo_b,
                       cache_mem_k[0].reshape(nb, N_MEM, d), cache_mem_v[0].reshape(nb, N_MEM, d))
    y_sample = _mlp_final(hs, row(g_mlp[0]), w_up_b, w_dn_b, g_fin).reshape(nb, ls, d)

    win5 = lambda a, n: a.reshape(1, n, WINDOW, N_KV_HEADS, HEAD_DIM)
    ret5 = lambda a, n: a.reshape(1, n, N_RET_HEADS, RET_QK_DIM, RET_V_DIM)
    mem5 = lambda a: a.reshape(1, b, N_MEM, N_X_HEADS, X_HEAD_DIM)
    return (y_prompt, y_sample,
            win5(p_wk, b), win5(p_wv, b), ret5(p_rs, b), mem5(mk), mem5(mv),
            win5(s_wk, nb), win5(s_wv, nb), ret5(s_rs, nb))
```

```python
import functools

import jax
import jax.numpy as jnp
import numpy as np
from jax import lax
from jax.experimental import pallas as pl
from jax.experimental.pallas import tpu as pltpu

F32 = jnp.float32
BF16 = jnp.bfloat16

D_MODEL = 1024
BATCH = 8
SEQ = 2048
DEC_BATCH = 128
DEC_SEQ = 4
HEAD_DIM = 64
N_ATT_HEADS = 8
N_KV_HEADS = 2
KV_GROUP = N_ATT_HEADS // N_KV_HEADS
WINDOW = 128
BLK = 128
N_RET_HEADS = 4
RET_QK_DIM = 64
RET_V_DIM = 128
N_MEM = 256
N_X_HEADS = 4
X_HEAD_DIM = D_MODEL // N_X_HEADS
D_FF = 4 * D_MODEL
RMS_EPS = 1e-6
GN_EPS = 1e-5

ATT_Q_W = N_ATT_HEADS * HEAD_DIM
ATT_KV_W = N_KV_HEADS * HEAD_DIM
RET_QK_W = N_RET_HEADS * RET_QK_DIM
RET_V_W = N_RET_HEADS * RET_V_DIM
MIX_OUT = ATT_Q_W + RET_V_W
D_IN = ATT_Q_W + 2 * ATT_KV_W + 2 * RET_QK_W + 2 * RET_V_W
C_QA, C_KV, C_QKR, C_VR, C_GR = 0, 512, 768, 1280, 1792

LANES = 128
SUBLANES = 8
HALF = LANES // 2
X_D_HALVES = X_HEAD_DIM // LANES
NEG = -1e30
VMEM_LIMIT = 52 * 1024 * 1024

TM_MIX = 512
TM_X = 512
TM_MLP = 512
FF_CHUNK = 1024
BB_MIX = 16
BB_X = 4

NEG_SLOPES = [-(2.0 ** (-8.0 * (i + 1) / N_ATT_HEADS)) for i in range(N_ATT_HEADS)]
_LOG_G = np.log(1.0 - 2.0 ** (-5.0 - np.arange(N_RET_HEADS))).astype(np.float32).astype(np.float64)


def _prompt_tables():
    qi = np.arange(BLK)[:, None]
    kj = np.arange(2 * BLK)[None, :]
    dist = (qi + BLK - kj).astype(np.float64)
    mask = np.where((dist >= 0) & (dist < WINDOW), 0.0, NEG)
    l = np.arange(BLK, dtype=np.float64)
    diff = l[:, None] - l[None, :]
    decay = np.where(diff >= 0, np.exp(_LOG_G[:, None, None] * np.maximum(diff, 0.0)), 0.0)
    xi = np.exp((l[:, None] + 1.0) * _LOG_G[None, :])
    zeta = np.exp((BLK - 1.0 - l)[:, None] * _LOG_G[None, :])
    xi_t = np.repeat(xi, RET_V_DIM, axis=1)
    zeta_t = np.repeat(zeta, RET_QK_DIM, axis=1)
    f = lambda a: np.asarray(a, np.float32)
    return f(dist), f(mask), f(decay), f(xi_t), f(zeta_t)


def _sample_tables():
    slopes = -np.asarray(NEG_SLOPES)
    bias = np.full((2, N_ATT_HEADS * SUBLANES, 2 * BLK), NEG, np.float64)
    dec = np.zeros((2, N_RET_HEADS * SUBLANES, BLK), np.float64)
    xi = np.zeros((2, N_RET_HEADS * SUBLANES, RET_V_DIM), np.float64)
    zeta = np.zeros((2, SUBLANES, RET_QK_W), np.float64)
    for par in range(2):
        for r in range(SUBLANES):
            own = DEC_SEQ * par <= r < DEC_SEQ * (par + 1)
            t = r - DEC_SEQ * par if own else r % DEC_SEQ
            for h in range(N_ATT_HEADS):
                row = h * SUBLANES + r
                for j in range(WINDOW):
                    d = t + WINDOW - j
                    if 0 <= d < WINDOW:
                        bias[par, row, j] = -slopes[h] * d
                for c in range(DEC_SEQ):
                    d = t - c
                    if d >= 0:
                        bias[par, row, WINDOW + DEC_SEQ * par + c] = -slopes[h] * d
            for h in range(N_RET_HEADS):
                row = h * SUBLANES + r
                if own:
                    xi[par, row, :] = np.exp((t + 1.0) * _LOG_G[h])
                    zeta[par, r, h * RET_QK_DIM:(h + 1) * RET_QK_DIM] = np.exp((DEC_SEQ - 1.0 - t) * _LOG_G[h])
                    for c in range(t + 1):
                        dec[par, row, DEC_SEQ * par + c] = np.exp(_LOG_G[h] * (t - c))
    f = lambda a: np.asarray(a, np.float32)
    return f(bias), f(dec), f(xi), f(zeta)


_P_DIST, _P_MASK, _P_DECAY, _P_XI, _P_ZETA = _prompt_tables()
_S_BIAS, _S_DEC, _S_XI, _S_ZETA = _sample_tables()
_GL_PROMPT = [float(np.exp(_LOG_G[h] * BLK)) for h in range(N_RET_HEADS)]
_GL_SAMPLE = [float(np.exp(_LOG_G[h] * DEC_SEQ)) for h in range(N_RET_HEADS)]


def _rms(x, g):
    return x * lax.rsqrt(jnp.mean(x * x, axis=-1, keepdims=True) + RMS_EPS) * g


def _dot(a, b):
    return jnp.dot(a, b, preferred_element_type=F32)


def _dot_nt(a, b):
    return lax.dot_general(a, b, (((1,), (1,)), ((), ())), preferred_element_type=F32)


def _dot_tn(a, b):
    return lax.dot_general(a, b, (((0,), (0,)), ((), ())), preferred_element_type=F32)


def _silu(g):
    return g * (1.0 / (1.0 + jnp.exp(-g)))


def _half_masks(width):
    lane = lax.broadcasted_iota(jnp.int32, (1, width), 1)
    lo = ((lane & (LANES - 1)) < HALF).astype(F32)
    return lo, 1.0 - lo


def _sink_softmax(s, sink):
    m = jnp.maximum(jnp.max(s, axis=-1, keepdims=True), sink)
    p = jnp.exp(s - m)
    den = jnp.sum(p, axis=-1, keepdims=True) + jnp.exp(sink - m)
    return p * (1.0 / den)


def _group_norm(o, g, b):
    mu = jnp.mean(o, axis=-1, keepdims=True)
    d = o - mu
    var = jnp.mean(d * d, axis=-1, keepdims=True)
    return d * lax.rsqrt(var + GN_EPS) * g + b


def _prompt_mixer_kernel(sinks_ref, x_ref, gmix_ref, win_ref, wout_ref, gng_ref, gnb_ref,
                         dist_ref, mask_ref, decay_ref, xi_ref, zeta_ref,
                         h_ref, wk_ref, wv_ref, st_ref,
                         qlo_s, qhi_s, kd0_s, kd1_s, vd0_s, vd1_s,
                         qrlo_s, qrhi_s, kr_s, vr_s, gate_s, mix_s):
    t = pl.program_id(1)
    nt = pl.num_programs(1)
    tm = x_ref.shape[1]
    nblk = tm // BLK

    @pl.when(t == 0)
    def _():
        zero = jnp.zeros((BLK, LANES), BF16)
        kd0_s[0:BLK, :] = zero
        kd1_s[0:BLK, :] = zero
        vd0_s[0:BLK, :] = zero
        vd1_s[0:BLK, :] = zero
        st_ref[...] = jnp.zeros_like(st_ref)

    x = x_ref[0]
    xn = _rms(x, gmix_ref[...]).astype(BF16)

    lo512, hi512 = _half_masks(ATT_Q_W)
    q = _dot(xn, win_ref[:, C_QA:C_QA + ATT_Q_W]) * (HEAD_DIM ** -0.5)
    qlo_s[...] = (q * lo512).astype(BF16)
    qhi_s[...] = (q * hi512).astype(BF16)

    lo128, hi128 = _half_masks(LANES)
    kv = _dot(xn, win_ref[:, C_KV:C_KV + 2 * ATT_KV_W])
    k = kv[:, :ATT_KV_W]
    v = kv[:, ATT_KV_W:]
    k_r = pltpu.roll(k, HALF, axis=1)
    v_r = pltpu.roll(v, HALF, axis=1)
    kd0_s[BLK:BLK + tm, :] = (k * lo128 + k_r * hi128).astype(BF16)
    kd1_s[BLK:BLK + tm, :] = (k_r * lo128 + k * hi128).astype(BF16)
    vd0_s[BLK:BLK + tm, :] = (v * lo128 + v_r * hi128).astype(BF16)
    vd1_s[BLK:BLK + tm, :] = (v_r * lo128 + v * hi128).astype(BF16)

    @pl.when(t == nt - 1)
    def _():
        wk_ref[0] = k[tm - WINDOW:, :]
        wv_ref[0] = v[tm - WINDOW:, :]

    lo256, hi256 = _half_masks(RET_QK_W)
    qkr = _dot(xn, win_ref[:, C_QKR:C_QKR + 2 * RET_QK_W])
    qr = qkr[:, :RET_QK_W]
    qrlo_s[...] = (qr * lo256).astype(BF16)
    qrhi_s[...] = (qr * hi256).astype(BF16)
    kr_s[...] = qkr[:, RET_QK_W:] * (RET_QK_DIM ** -0.5)
    vr_s[...] = _dot(xn, win_ref[:, C_VR:C_VR + RET_V_W]).astype(BF16)
    gate_s[...] = _silu(_dot(xn, win_ref[:, C_GR:C_GR + RET_V_W]))

    lane = lax.broadcasted_iota(jnp.int32, (BLK, LANES), 1)
    col = lax.broadcasted_iota(jnp.int32, (BLK, 2 * BLK), 1)
    kd_refs = (kd0_s, kd1_s)
    vd_refs = (vd0_s, vd1_s)

    def block(j, carry):
        r0 = pl.multiple_of(j * BLK, BLK)
        rows = pl.ds(r0, BLK)
        krows = pl.ds(r0, 2 * BLK)
        first = jnp.where(t * nblk + j == 0, NEG, 0.0)
        mask_n = mask_ref[...] + jnp.where(col < BLK, first, 0.0)
        dist = dist_ref[...]

        for kvh in range(N_KV_HEADS):
            kd = kd_refs[kvh][krows, :]
            vd = vd_refs[kvh][krows, :]
            c0 = kvh * KV_GROUP * HEAD_DIM
            qst = jnp.concatenate([qlo_s[rows, c0:c0 + LANES], qhi_s[rows, c0:c0 + LANES],
                                   qlo_s[rows, c0 + LANES:c0 + 2 * LANES],
                                   qhi_s[rows, c0 + LANES:c0 + 2 * LANES]], axis=0)
            s = _dot_nt(qst, kd)
            ps = []
            for g in range(KV_GROUP):
                h = kvh * KV_GROUP + g
                sg = s[g * BLK:(g + 1) * BLK] + (NEG_SLOPES[h] * dist + mask_n)
                ps.append(_sink_softmax(sg, sinks_ref[h]).astype(BF16))
            o = _dot(jnp.concatenate(ps, axis=0), vd)
            oa = jnp.where(lane < HALF, o[0:BLK], o[BLK:2 * BLK])
            ob = jnp.where(lane < HALF, o[2 * BLK:3 * BLK], o[3 * BLK:4 * BLK])
            mix_s[rows, c0:c0 + LANES] = oa.astype(BF16)
            mix_s[rows, c0 + LANES:c0 + 2 * LANES] = ob.astype(BF16)

        for i in range(N_RET_HEADS // 2):
            lsl = slice(i * LANES, (i + 1) * LANES)
            kp = kr_s[rows, lsl]
            kp_b = kp.astype(BF16)
            sp = st_ref[0, lsl, :]
            sp_b = sp.astype(BF16)
            vpair = vr_s[rows, 2 * i * RET_V_DIM:(2 * i + 2) * RET_V_DIM]
            for half, q_s in enumerate((qrlo_s, qrhi_s)):
                h = 2 * i + half
                vsl = slice(h * RET_V_DIM, (h + 1) * RET_V_DIM)
                qm = q_s[rows, lsl]
                inner = (_dot_nt(qm, kp_b) * decay_ref[h]).astype(BF16)
                o = _dot(inner, vpair[:, half * RET_V_DIM:(half + 1) * RET_V_DIM])
                o = o + _dot(qm, sp_b) * xi_ref[:, vsl]
                r = _group_norm(o, gng_ref[:, vsl], gnb_ref[:, vsl]) * gate_s[rows, vsl]
                mix_s[rows, ATT_Q_W + h * RET_V_DIM:ATT_Q_W + (h + 1) * RET_V_DIM] = r.astype(BF16)
            kz = (kp * zeta_ref[:, lsl]).astype(BF16)
            u = _dot_tn(kz, vpair)
            st_ref[0, i * LANES:i * LANES + RET_QK_DIM, :] = (
                _GL_PROMPT[2 * i] * sp[:RET_QK_DIM] + u[:RET_QK_DIM, :RET_V_DIM])
            st_ref[0, i * LANES + RET_QK_DIM:(i + 1) * LANES, :] = (
                _GL_PROMPT[2 * i + 1] * sp[RET_QK_DIM:] + u[RET_QK_DIM:, RET_V_DIM:])
        return carry

    lax.fori_loop(0, nblk, block, 0)

    kd0_s[0:BLK, :] = kd0_s[tm:tm + BLK, :]
    kd1_s[0:BLK, :] = kd1_s[tm:tm + BLK, :]
    vd0_s[0:BLK, :] = vd0_s[tm:tm + BLK, :]
    vd1_s[0:BLK, :] = vd1_s[tm:tm + BLK, :]

    h_ref[0] = x + _dot(mix_s[...], wout_ref[...])


def _prompt_mixer(x, g_mix, w_in, w_out, sinks, gn_g, gn_b):
    b, s, d = x.shape
    tm = TM_MIX
    const = lambda shape: pl.BlockSpec(shape, lambda i, j: (0,) * len(shape))
    return pl.pallas_call(
        _prompt_mixer_kernel,
        grid=(b, s // tm),
        in_specs=[
            pl.BlockSpec(memory_space=pltpu.SMEM),
            pl.BlockSpec((1, tm, d), lambda i, j: (i, j, 0)),
            const((1, d)), const((d, D_IN)), const((MIX_OUT, d)),
            const((1, RET_V_W)), const((1, RET_V_W)),
            const((BLK, 2 * BLK)), const((BLK, 2 * BLK)),
            const((N_RET_HEADS, BLK, BLK)), const((BLK, RET_V_W)), const((BLK, RET_QK_W)),
        ],
        out_specs=[
            pl.BlockSpec((1, tm, d), lambda i, j: (i, j, 0)),
            pl.BlockSpec((1, WINDOW, ATT_KV_W), lambda i, j: (i, 0, 0)),
            pl.BlockSpec((1, WINDOW, ATT_KV_W), lambda i, j: (i, 0, 0)),
            pl.BlockSpec((1, RET_QK_W, RET_V_DIM), lambda i, j: (i, 0, 0)),
        ],
        out_shape=[
            jax.ShapeDtypeStruct((b, s, d), F32),
            jax.ShapeDtypeStruct((b, WINDOW, ATT_KV_W), F32),
            jax.ShapeDtypeStruct((b, WINDOW, ATT_KV_W), F32),
            jax.ShapeDtypeStruct((b, RET_QK_W, RET_V_DIM), F32),
        ],
        scratch_shapes=[
            pltpu.VMEM((tm, ATT_Q_W), BF16), pltpu.VMEM((tm, ATT_Q_W), BF16),
            pltpu.VMEM((tm + BLK, LANES), BF16), pltpu.VMEM((tm + BLK, LANES), BF16),
            pltpu.VMEM((tm + BLK, LANES), BF16), pltpu.VMEM((tm + BLK, LANES), BF16),
            pltpu.VMEM((tm, RET_QK_W), BF16), pltpu.VMEM((tm, RET_QK_W), BF16),
            pltpu.VMEM((tm, RET_QK_W), F32), pltpu.VMEM((tm, RET_V_W), BF16),
            pltpu.VMEM((tm, RET_V_W), F32), pltpu.VMEM((tm, MIX_OUT), BF16),
        ],
        compiler_params=pltpu.CompilerParams(
            dimension_semantics=("arbitrary", "arbitrary"), vmem_limit_bytes=VMEM_LIMIT),
        name="prompt_mixer",
    )(sinks, x, g_mix, w_in, w_out, gn_g, gn_b,
      jnp.asarray(_P_DIST), jnp.asarray(_P_MASK), jnp.asarray(_P_DECAY), jnp.asarray(_P_XI),
      jnp.asarray(_P_ZETA))


def _memkv_kernel(mem_ref, g_ref, wk_ref, wv_ref, mk_ref, mv_ref, mkb_ref, mvb_ref):
    mn = _rms(mem_ref[...], g_ref[...]).astype(BF16)
    mk = _dot(mn, wk_ref[...])
    mv = _dot(mn, wv_ref[...])
    tm = mem_ref.shape[0]
    group = X_D_HALVES * N_X_HEADS
    for hd in range(N_X_HEADS):
        for dh in range(X_D_HALVES):
            cols = slice(hd * X_HEAD_DIM + dh * LANES, hd * X_HEAD_DIM + (dh + 1) * LANES)
            rows = pl.ds(dh * N_X_HEADS + hd, tm, stride=group)
            mk_ref[rows, :] = mk[:, cols]
            mv_ref[rows, :] = mv[:, cols]
    mkb_ref[...] = mk.astype(BF16)
    mvb_ref[...] = mv.astype(BF16)


def _memory_kv(mem2d, g_mem, w_xk, w_xv):
    n, d = mem2d.shape
    tm = 512
    row = pl.BlockSpec((tm, d), lambda i: (i, 0))
    rows_out = pl.BlockSpec((tm * d // LANES, LANES), lambda i: (i, 0))
    const = lambda shape: pl.BlockSpec(shape, lambda i: (0,) * len(shape))
    return pl.pallas_call(
        _memkv_kernel,
        grid=(n // tm,),
        in_specs=[row, const((1, d)), const((d, d)), const((d, d))],
        out_specs=[rows_out, rows_out, row, row],
        out_shape=[jax.ShapeDtypeStruct((n * d // LANES, LANES), F32),
                   jax.ShapeDtypeStruct((n * d // LANES, LANES), F32),
                   jax.ShapeDtypeStruct((n, d), BF16), jax.ShapeDtypeStruct((n, d), BF16)],
        compiler_params=pltpu.CompilerParams(
            dimension_semantics=("arbitrary",), vmem_limit_bytes=VMEM_LIMIT),
        name="memory_kv",
    )(mem2d, g_mem, w_xk, w_xv)


def _prompt_xattn_kernel(h_ref, g_ref, wq_ref, wo_ref, mk_ref, mv_ref, out_ref, o_s):
    h = h_ref[0]
    xn = _rms(h, g_ref[...]).astype(BF16)
    q = (_dot(xn, wq_ref[...]) * (X_HEAD_DIM ** -0.5)).astype(BF16)
    for hd in range(N_X_HEADS):
        sl = slice(hd * X_HEAD_DIM, (hd + 1) * X_HEAD_DIM)
        s = _dot_nt(q[:, sl], mk_ref[0, :, sl])
        m = jnp.max(s, axis=-1, keepdims=True)
        p = jnp.exp(s - m)
        p = p * (1.0 / jnp.sum(p, axis=-1, keepdims=True))
        o_s[:, sl] = _dot(p.astype(BF16), mv_ref[0, :, sl]).astype(BF16)
    out_ref[0] = h + _dot(o_s[...], wo_ref[...])


def _prompt_xattn(h, g, w_xq, w_xo, mkb, mvb):
    b, s, d = h.shape
    tm = TM_X
    const = lambda shape: pl.BlockSpec(shape, lambda i, j: (0,) * len(shape))
    tok = pl.BlockSpec((1, tm, d), lambda i, j: (i, j, 0))
    mem = pl.BlockSpec((1, N_MEM, d), lambda i, j: (i, 0, 0))
    return pl.pallas_call(
        _prompt_xattn_kernel,
        grid=(b, s // tm),
        in_specs=[tok, const((1, d)), const((d, d)), const((d, d)), mem, mem],
        out_specs=tok,
        out_shape=jax.ShapeDtypeStruct((b, s, d), F32),
        scratch_shapes=[pltpu.VMEM((tm, d), BF16)],
        compiler_params=pltpu.CompilerParams(
            dimension_semantics=("arbitrary", "arbitrary"), vmem_limit_bytes=VMEM_LIMIT),
        name="prompt_xattn",
    )(h, g, w_xq, w_xo, mkb, mvb)


def _mlp_kernel(h_ref, g_ref, wup_ref, wdn_ref, gf_ref, y_ref):
    h = h_ref[...]
    xn = _rms(h, g_ref[...]).astype(BF16)
    acc = h
    for c in range(D_FF // FF_CHUNK):
        sl = slice(c * FF_CHUNK, (c + 1) * FF_CHUNK)
        u = jnp.maximum(_dot(xn, wup_ref[:, sl]), 0.0)
        acc = acc + _dot((u * u).astype(BF16), wdn_ref[sl, :])
    y_ref[...] = _rms(acc, gf_ref[...])


def _mlp_final(h2d, g_mlp, w_up, w_down, g_final):
    n, d = h2d.shape
    tm = TM_MLP
    row = pl.BlockSpec((tm, d), lambda i: (i, 0))
    const = lambda shape: pl.BlockSpec(shape, lambda i: (0,) * len(shape))
    return pl.pallas_call(
        _mlp_kernel,
        grid=(n // tm,),
        in_specs=[row, const((1, d)), const((d, D_FF)), const((D_FF, d)), const((1, d))],
        out_specs=row,
        out_shape=jax.ShapeDtypeStruct((n, d), F32),
        compiler_params=pltpu.CompilerParams(
            dimension_semantics=("arbitrary",), vmem_limit_bytes=VMEM_LIMIT),
        name="mlp_final",
    )(h2d, g_mlp, w_up, w_down, g_final)


def _sample_mixer_kernel(sinks_ref, x_ref, gmix_ref, win_ref, wout_ref, gng_ref, gnb_ref,
                         ck_ref, cv_ref, st_ref, bias_ref, dec_ref, xi_ref, zeta_ref,
                         h_ref, swk_ref, swv_ref, sst_ref):
    bb = ck_ref.shape[0]
    nt = bb // 2
    x = x_ref[...]
    xn = _rms(x, gmix_ref[...]).astype(BF16)
    tile3 = lambda a: a.reshape(nt, SUBLANES, a.shape[-1])

    q = _dot(xn, win_ref[:, C_QA:C_QA + ATT_Q_W]) * (HEAD_DIM ** -0.5)
    kv = _dot(xn, win_ref[:, C_KV:C_KV + 2 * ATT_KV_W])
    qkr = _dot(xn, win_ref[:, C_QKR:C_QKR + 2 * RET_QK_W])
    vr = _dot(xn, win_ref[:, C_VR:C_VR + RET_V_W])
    gate3 = tile3(_silu(_dot(xn, win_ref[:, C_GR:C_GR + RET_V_W])))

    lo512, hi512 = _half_masks(ATT_Q_W)
    q_r = pltpu.roll(q, HALF, axis=1)
    q_nat3 = tile3(q)
    q_rot3 = tile3(q_r)
    lo3 = lo512.reshape(1, 1, ATT_Q_W)
    hi3 = hi512.reshape(1, 1, ATT_Q_W)
    qa3 = (q_nat3 * lo3).astype(BF16)
    qb3 = (q_rot3 * lo3).astype(BF16)
    qc3 = (q_rot3 * hi3).astype(BF16)
    qd3 = (q_nat3 * hi3).astype(BF16)
    t128 = lambda a, i: a[:, :, i * LANES:(i + 1) * LANES]
    qs = jnp.concatenate([t128(qa3, 0), t128(qb3, 1), t128(qa3, 1), t128(qb3, 2),
                          t128(qc3, 2), t128(qd3, 2), t128(qc3, 3), t128(qd3, 3)], axis=1)

    k3 = tile3(kv[:, :ATT_KV_W])
    v3 = tile3(kv[:, ATT_KV_W:])
    pad_kv = jnp.zeros((nt, BLK - SUBLANES, LANES), BF16)

    lo256, _ = _half_masks(RET_QK_W)
    qr3 = tile3(qkr[:, :RET_QK_W])
    kr3 = tile3(qkr[:, RET_QK_W:] * (RET_QK_DIM ** -0.5))
    vr3 = tile3(vr)
    lane256 = lax.broadcasted_iota(jnp.int32, (1, 1, RET_QK_W), 2)
    qrs = jnp.concatenate(
        [(qr3 * ((lane256 >= h * RET_QK_DIM) & (lane256 < (h + 1) * RET_QK_DIM)).astype(F32)).astype(BF16)
         for h in range(N_RET_HEADS)],
        axis=1)
    kr_pad = jnp.concatenate([kr3.astype(BF16), jnp.zeros((nt, BLK - SUBLANES, RET_QK_W), BF16)], axis=1)
    vr_pad = jnp.concatenate([vr3.astype(BF16), jnp.zeros((nt, BLK - SUBLANES, RET_V_W), BF16)], axis=1)

    lane = lax.broadcasted_iota(jnp.int32, (1, 1, LANES), 2)
    row8 = lax.broadcasted_iota(jnp.int32, (1, SUBLANES, 1), 1)
    bmm_nt = lambda a, b: jnp.einsum('bqd,bkd->bqk', a, b, preferred_element_type=F32)
    bmm = lambda a, b: jnp.einsum('bqk,bkd->bqd', a, b, preferred_element_type=F32)

    att_par, ret_par = [], []
    for par in range(2):
        bsl = pl.ds(par, nt, stride=2)
        ck = ck_ref[bsl]
        cv = cv_ref[bsl]
        swk_ref[bsl, 0:WINDOW - DEC_SEQ, :] = ck[:, DEC_SEQ:, :]
        swv_ref[bsl, 0:WINDOW - DEC_SEQ, :] = cv[:, DEC_SEQ:, :]
        swk_ref[bsl, WINDOW - DEC_SEQ:WINDOW, :] = k3[:, DEC_SEQ * par:DEC_SEQ * (par + 1), :]
        swv_ref[bsl, WINDOW - DEC_SEQ:WINDOW, :] = v3[:, DEC_SEQ * par:DEC_SEQ * (par + 1), :]

        kfull = jnp.concatenate([ck.astype(BF16), k3.astype(BF16), pad_kv], axis=1)
        vfull = jnp.concatenate([cv.astype(BF16), v3.astype(BF16), pad_kv], axis=1)
        s = bmm_nt(qs, kfull) + bias_ref[par]
        ps = []
        for h in range(N_ATT_HEADS):
            ps.append(_sink_softmax(s[:, h * SUBLANES:(h + 1) * SUBLANES, :], sinks_ref[h]).astype(BF16))
        o = bmm(jnp.concatenate(ps, axis=1), vfull)
        o_r = pltpu.roll(o.reshape(nt * N_ATT_HEADS * SUBLANES, LANES), HALF, axis=1).reshape(o.shape)
        hr = lambda a, h: a[:, h * SUBLANES:(h + 1) * SUBLANES, :]
        low = lane < HALF
        att_par.append(jnp.concatenate([
            jnp.where(low, hr(o, 0), hr(o_r, 1)), jnp.where(low, hr(o, 2), hr(o_r, 3)),
            jnp.where(low, hr(o_r, 4), hr(o, 5)), jnp.where(low, hr(o_r, 6), hr(o, 7))], axis=2))

        st = st_ref[bsl]
        oc = bmm(qrs, st.astype(BF16))
        inner = (bmm_nt(qrs, kr_pad) * dec_ref[par]).astype(BF16)
        oi = bmm(inner, vr_pad)
        rs = []
        for h in range(N_RET_HEADS):
            vsl = slice(h * RET_V_DIM, (h + 1) * RET_V_DIM)
            rsl = slice(h * SUBLANES, (h + 1) * SUBLANES)
            o_h = oi[:, rsl, vsl] + oc[:, rsl, :] * xi_ref[par, rsl, :]
            rs.append(_group_norm(o_h, gng_ref[:, vsl], gnb_ref[:, vsl]) * gate3[:, :, vsl])
        ret_par.append(jnp.concatenate(rs, axis=2))

        kz3 = (kr3 * zeta_ref[par]).astype(BF16)
        vr3_b = vr3.astype(BF16)
        for p in range(nt):
            u = _dot_tn(kz3[p], vr3_b[p])
            for h in range(N_RET_HEADS):
                dsl = slice(h * RET_QK_DIM, (h + 1) * RET_QK_DIM)
                sst_ref[2 * p + par, dsl, :] = (
                    _GL_SAMPLE[h] * st[p, dsl, :] + u[dsl, h * RET_V_DIM:(h + 1) * RET_V_DIM])

    own0 = row8 < DEC_SEQ
    att3 = jnp.where(own0, att_par[0], att_par[1])
    ret3 = jnp.where(own0, ret_par[0], ret_par[1])
    mix = jnp.concatenate([att3, ret3], axis=2).reshape(2 * nt * DEC_SEQ, MIX_OUT).astype(BF16)
    h_ref[...] = x + _dot(mix, wout_ref[...])


def _sample_mixer(x2d, g_mix, w_in, w_out, sinks, gn_g, gn_b, ck, cv, st):
    n, d = x2d.shape
    nb = ck.shape[0]
    bb = BB_MIX
    r = bb * DEC_SEQ
    const = lambda shape: pl.BlockSpec(shape, lambda i: (0,) * len(shape))
    row = pl.BlockSpec((r, d), lambda i: (i, 0))
    win = pl.BlockSpec((bb, WINDOW, ATT_KV_W), lambda i: (i, 0, 0))
    state = pl.BlockSpec((bb, RET_QK_W, RET_V_DIM), lambda i: (i, 0, 0))
    return pl.pallas_call(
        _sample_mixer_kernel,
        grid=(nb // bb,),
        in_specs=[
            pl.BlockSpec(memory_space=pltpu.SMEM),
            row, const((1, d)), const((d, D_IN)), const((MIX_OUT, d)),
            const((1, RET_V_W)), const((1, RET_V_W)),
            win, win, state,
            const(_S_BIAS.shape), const(_S_DEC.shape), const(_S_XI.shape), const(_S_ZETA.shape),
        ],
        out_specs=[row, win, win, state],
        out_shape=[
            jax.ShapeDtypeStruct((n, d), F32),
            jax.ShapeDtypeStruct((nb, WINDOW, ATT_KV_W), F32),
            jax.ShapeDtypeStruct((nb, WINDOW, ATT_KV_W), F32),
            jax.ShapeDtypeStruct((nb, RET_QK_W, RET_V_DIM), F32),
        ],
        compiler_params=pltpu.CompilerParams(
            dimension_semantics=("arbitrary",), vmem_limit_bytes=VMEM_LIMIT),
        name="sample_mixer",
    )(sinks, x2d, g_mix, w_in, w_out, gn_g, gn_b, ck, cv, st,
      jnp.asarray(_S_BIAS), jnp.asarray(_S_DEC), jnp.asarray(_S_XI), jnp.asarray(_S_ZETA))


def _head_slab(x_ref, b, hd):
    group = X_D_HALVES * N_X_HEADS
    halves = [x_ref[b, pl.ds(dh * N_X_HEADS + hd, N_MEM, stride=group), :] for dh in range(X_D_HALVES)]
    return jnp.concatenate(halves, axis=1).astype(BF16)


def _sample_xattn_kernel(h_ref, g_ref, wq_ref, wo_ref, xk_ref, xv_ref, out_ref):
    bb = xk_ref.shape[0]
    nt = bb // 2
    h = h_ref[...]
    xn = _rms(h, g_ref[...]).astype(BF16)
    q = _dot(xn, wq_ref[...]) * (X_HEAD_DIM ** -0.5)
    units = [(t, par, hd) for t in range(nt) for par in range(2) for hd in range(N_X_HEADS)]
    qts = [q[t * SUBLANES:(t + 1) * SUBLANES].astype(BF16) for t in range(nt)]
    s = jnp.concatenate(
        [_dot_nt(qts[t][:, hd * X_HEAD_DIM:(hd + 1) * X_HEAD_DIM], _head_slab(xk_ref, 2 * t + par, hd))
         for t, par, hd in units], axis=0)
    m = jnp.max(s, axis=-1, keepdims=True)
    p = jnp.exp(s - m)
    p = p * (1.0 / jnp.sum(p, axis=-1, keepdims=True))
    os_ = {}
    for i, (t, par, hd) in enumerate(units):
        pi = p[i * SUBLANES:(i + 1) * SUBLANES].astype(BF16)
        os_[(t, par, hd)] = _dot(pi, _head_slab(xv_ref, 2 * t + par, hd))
    own0 = lax.broadcasted_iota(jnp.int32, (SUBLANES, 1), 0) < DEC_SEQ
    o_tiles = []
    for t in range(nt):
        o_par = [jnp.concatenate([os_[(t, par, hd)] for hd in range(N_X_HEADS)], axis=1) for par in range(2)]
        o_tiles.append(jnp.where(own0, o_par[0], o_par[1]))
    o = jnp.concatenate(o_tiles, axis=0).astype(BF16)
    out_ref[...] = h + _dot(o, wo_ref[...])


def _mem_rows(c):
    nb = c.shape[0]
    c = c.reshape(nb, N_MEM, N_X_HEADS, X_D_HALVES, LANES)
    return jnp.transpose(c, (0, 1, 3, 2, 4)).reshape(nb, N_MEM * X_D_HALVES * N_X_HEADS, LANES)


def _sample_xattn(h2d, g, w_xq, w_xo, mk, mv):
    n, d = h2d.shape
    nb = mk.shape[0]
    bb = BB_X
    r = bb * DEC_SEQ
    const = lambda shape: pl.BlockSpec(shape, lambda i: (0,) * len(shape))
    row = pl.BlockSpec((r, d), lambda i: (i, 0))
    mem = pl.BlockSpec((bb,) + mk.shape[1:], lambda i: (i, 0, 0))
    return pl.pallas_call(
        _sample_xattn_kernel,
        grid=(nb // bb,),
        in_specs=[row, const((1, d)), const((d, d)), const((d, d)), mem, mem],
        out_specs=row,
        out_shape=jax.ShapeDtypeStruct((n, d), F32),
        compiler_params=pltpu.CompilerParams(
            dimension_semantics=("arbitrary",), vmem_limit_bytes=VMEM_LIMIT),
        name="sample_xattn",
    )(h2d, g, w_xq, w_xo, mk, mv)


def kernel(x_prompt, x_sample, mem_prompt, cache_win_k, cache_win_v, state_ret, cache_mem_k, cache_mem_v,
           g_mix, w_in, attn_sinks, ret_gn_g, ret_gn_b, w_out, g_xattn, g_mem, w_xq, w_xk, w_xv, w_xo,
           g_mlp, w_up, w_down, g_final):
    depth = w_in.shape[0]
    assert depth == 1, "single-layer trunk"
    b, s, d = x_prompt.shape
    nb, ls, _ = x_sample.shape
    row = lambda a: a.reshape(1, -1)
    bf = lambda a: a.astype(BF16)

    w_in_b, w_out_b = bf(w_in[0]), bf(w_out[0])
    w_xq_b, w_xk_b, w_xv_b, w_xo_b = bf(w_xq[0]), bf(w_xk[0]), bf(w_xv[0]), bf(w_xo[0])
    w_up_b, w_dn_b = bf(w_up[0]), bf(w_down[0])
    sinks = attn_sinks[0]
    gn_g, gn_b = row(ret_gn_g[0]), row(ret_gn_b[0])
    g_fin = row(g_final)

    hp, p_wk, p_wv, p_rs = _prompt_mixer(x_prompt, row(g_mix[0]), w_in_b, w_out_b, sinks, gn_g, gn_b)
    mk, mv, mkb, mvb = _memory_kv(mem_prompt.reshape(b * N_MEM, d), row(g_mem[0]), w_xk_b, w_xv_b)
    hp = _prompt_xattn(hp, row(g_xattn[0]), w_xq_b, w_xo_b,
                       mkb.reshape(b, N_MEM, d), mvb.reshape(b, N_MEM, d))
    y_prompt = _mlp_final(hp.reshape(b * s, d), row(g_mlp[0]), w_up_b, w_dn_b, g_fin).reshape(b, s, d)

    hs, s_wk, s_wv, s_rs = _sample_mixer(
        x_sample.reshape(nb * ls, d), row(g_mix[0]), w_in_b, w_out_b, sinks, gn_g, gn_b,
        cache_win_k[0].reshape(nb, WINDOW, ATT_KV_W), cache_win_v[0].reshape(nb, WINDOW, ATT_KV_W),
        state_ret[0].reshape(nb, RET_QK_W, RET_V_DIM))
    hs = _sample_xattn(hs, row(g_xattn[0]), w_xq_b, w_xo_b,
                       _mem_rows(cache_mem_k[0]), _mem_rows(cache_mem_v[0]))
    y_sample = _mlp_final(hs, row(g_mlp[0]), w_up_b, w_dn_b, g_fin).reshape(nb, ls, d)

    win5 = lambda a, n: a.reshape(1, n, WINDOW, N_KV_HEADS, HEAD_DIM)
    ret5 = lambda a, n: a.reshape(1, n, N_RET_HEADS, RET_QK_DIM, RET_V_DIM)
    mem5 = lambda a: jnp.transpose(a.reshape(b, N_MEM, X_D_HALVES, N_X_HEADS, LANES),
                                   (0, 1, 3, 2, 4)).reshape(1, b, N_MEM, N_X_HEADS, X_HEAD_DIM)
    return (y_prompt, y_sample,
            win5(p_wk, b), win5(p_wv, b), ret5(p_rs, b), mem5(mk), mem5(mv),
            win5(s_wk, nb), win5(s_wv, nb), ret5(s_rs, nb))
```

```python
import functools

import jax
import jax.numpy as jnp
import numpy as np
from jax import lax
from jax.experimental import pallas as pl
from jax.experimental.pallas import tpu as pltpu

F32 = jnp.float32
BF16 = jnp.bfloat16

D_MODEL = 1024
BATCH = 8
SEQ = 2048
DEC_BATCH = 128
DEC_SEQ = 4
HEAD_DIM = 64
N_ATT_HEADS = 8
N_KV_HEADS = 2
KV_GROUP = N_ATT_HEADS // N_KV_HEADS
WINDOW = 128
BLK = 128
N_RET_HEADS = 4
RET_QK_DIM = 64
RET_V_DIM = 128
N_MEM = 256
N_X_HEADS = 4
X_HEAD_DIM = D_MODEL // N_X_HEADS
D_FF = 4 * D_MODEL
RMS_EPS = 1e-6
GN_EPS = 1e-5

ATT_Q_W = N_ATT_HEADS * HEAD_DIM
ATT_KV_W = N_KV_HEADS * HEAD_DIM
RET_QK_W = N_RET_HEADS * RET_QK_DIM
RET_V_W = N_RET_HEADS * RET_V_DIM
MIX_OUT = ATT_Q_W + RET_V_W
D_IN = ATT_Q_W + 2 * ATT_KV_W + 2 * RET_QK_W + 2 * RET_V_W
C_QA, C_KV, C_QKR, C_VR, C_GR = 0, 512, 768, 1280, 1792

LANES = 128
SUBLANES = 8
HALF = LANES // 2
X_D_HALVES = X_HEAD_DIM // LANES
NEG = -1e30
VMEM_LIMIT = 52 * 1024 * 1024

TM_MIX = 512
TM_X = 512
TM_MLP = 512
FF_CHUNK = 1024
BB_MIX = 16
BB_X = 4

NEG_SLOPES = [-(2.0 ** (-8.0 * (i + 1) / N_ATT_HEADS)) for i in range(N_ATT_HEADS)]
_LOG_G = np.log(1.0 - 2.0 ** (-5.0 - np.arange(N_RET_HEADS))).astype(np.float32).astype(np.float64)


def _prompt_tables():
    qi = np.arange(BLK)[:, None]
    kj = np.arange(2 * BLK)[None, :]
    dist = (qi + BLK - kj).astype(np.float64)
    mask = np.where((dist >= 0) & (dist < WINDOW), 0.0, NEG)
    l = np.arange(BLK, dtype=np.float64)
    diff = l[:, None] - l[None, :]
    decay = np.where(diff >= 0, np.exp(_LOG_G[:, None, None] * np.maximum(diff, 0.0)), 0.0)
    xi = np.exp((l[:, None] + 1.0) * _LOG_G[None, :])
    zeta = np.exp((BLK - 1.0 - l)[:, None] * _LOG_G[None, :])
    xi_t = np.repeat(xi, RET_V_DIM, axis=1)
    zeta_t = np.repeat(zeta, RET_QK_DIM, axis=1)
    f = lambda a: np.asarray(a, np.float32)
    return f(dist), f(mask), f(decay), f(xi_t), f(zeta_t)


def _sample_tables():
    slopes = -np.asarray(NEG_SLOPES)
    bias = np.full((2, N_ATT_HEADS * SUBLANES, 2 * BLK), NEG, np.float64)
    dec = np.zeros((2, N_RET_HEADS * SUBLANES, BLK), np.float64)
    xi = np.zeros((2, N_RET_HEADS * SUBLANES, RET_V_DIM), np.float64)
    zeta = np.zeros((2, SUBLANES, RET_QK_W), np.float64)
    for par in range(2):
        for r in range(SUBLANES):
            own = DEC_SEQ * par <= r < DEC_SEQ * (par + 1)
            t = r - DEC_SEQ * par if own else r % DEC_SEQ
            for h in range(N_ATT_HEADS):
                row = h * SUBLANES + r
                for j in range(WINDOW):
                    d = t + WINDOW - j
                    if 0 <= d < WINDOW:
                        bias[par, row, j] = -slopes[h] * d
                for c in range(DEC_SEQ):
                    d = t - c
                    if d >= 0:
                        bias[par, row, WINDOW + DEC_SEQ * par + c] = -slopes[h] * d
            for h in range(N_RET_HEADS):
                row = h * SUBLANES + r
                if own:
                    xi[par, row, :] = np.exp((t + 1.0) * _LOG_G[h])
                    zeta[par, r, h * RET_QK_DIM:(h + 1) * RET_QK_DIM] = np.exp((DEC_SEQ - 1.0 - t) * _LOG_G[h])
                    for c in range(t + 1):
                        dec[par, row, DEC_SEQ * par + c] = np.exp(_LOG_G[h] * (t - c))
    f = lambda a: np.asarray(a, np.float32)
    return f(bias), f(dec), f(xi), f(zeta)


_P_DIST, _P_MASK, _P_DECAY, _P_XI, _P_ZETA = _prompt_tables()
_S_BIAS, _S_DEC, _S_XI, _S_ZETA = _sample_tables()
_GL_PROMPT = [float(np.exp(_LOG_G[h] * BLK)) for h in range(N_RET_HEADS)]
_GL_SAMPLE = [float(np.exp(_LOG_G[h] * DEC_SEQ)) for h in range(N_RET_HEADS)]


def _rms(x, g):
    return x * lax.rsqrt(jnp.mean(x * x, axis=-1, keepdims=True) + RMS_EPS) * g


def _dot(a, b):
    return jnp.dot(a, b, preferred_element_type=F32)


def _dot_nt(a, b):
    return lax.dot_general(a, b, (((1,), (1,)), ((), ())), preferred_element_type=F32)


def _dot_tn(a, b):
    return lax.dot_general(a, b, (((0,), (0,)), ((), ())), preferred_element_type=F32)


def _silu(g):
    return g * (1.0 / (1.0 + jnp.exp(-g)))


def _half_masks(width):
    lane = lax.broadcasted_iota(jnp.int32, (1, width), 1)
    lo = ((lane & (LANES - 1)) < HALF).astype(F32)
    return lo, 1.0 - lo


def _sink_softmax(s, sink):
    m = jnp.maximum(jnp.max(s, axis=-1, keepdims=True), sink)
    p = jnp.exp(s - m)
    den = jnp.sum(p, axis=-1, keepdims=True) + jnp.exp(sink - m)
    return p * (1.0 / den)


def _group_norm(o, g, b):
    mu = jnp.mean(o, axis=-1, keepdims=True)
    d = o - mu
    var = jnp.mean(d * d, axis=-1, keepdims=True)
    return d * lax.rsqrt(var + GN_EPS) * g + b


def _prompt_mixer_kernel(sinks_ref, x_ref, gmix_ref, win_ref, wout_ref, gng_ref, gnb_ref,
                         dist_ref, mask_ref, decay_ref, xi_ref, zeta_ref,
                         h_ref, wk_ref, wv_ref, st_ref,
                         qlo_s, qhi_s, kd0_s, kd1_s, vd0_s, vd1_s,
                         qrlo_s, qrhi_s, kr_s, vr_s, gate_s, mix_s, bias_s):
    t = pl.program_id(1)
    nt = pl.num_programs(1)
    tm = x_ref.shape[1]
    nblk = tm // BLK

    @pl.when(t == 0)
    def _():
        kd0_s[0:BLK, :] = jnp.zeros((BLK, LANES), BF16)
        kd1_s[0:BLK, :] = jnp.zeros((BLK, LANES), BF16)
        vd0_s[0:BLK, :] = jnp.zeros((BLK, 2 * LANES), BF16)
        vd1_s[0:BLK, :] = jnp.zeros((BLK, 2 * LANES), BF16)
        st_ref[...] = jnp.zeros_like(st_ref)

    @pl.when(t > 0)
    def _():
        kd0_s[0:BLK, :] = kd0_s[tm:tm + BLK, :]
        kd1_s[0:BLK, :] = kd1_s[tm:tm + BLK, :]
        vd0_s[0:BLK, :] = vd0_s[tm:tm + BLK, :]
        vd1_s[0:BLK, :] = vd1_s[tm:tm + BLK, :]

    @pl.when((t == 0) & (pl.program_id(0) == 0))
    def _():
        for h in range(N_ATT_HEADS):
            bias_s[h] = NEG_SLOPES[h] * dist_ref[...] + mask_ref[...]

    x = x_ref[0]
    xn = _rms(x, gmix_ref[...]).astype(BF16)

    lo512, hi512 = _half_masks(ATT_Q_W)
    q = _dot(xn, win_ref[:, C_QA:C_QA + ATT_Q_W])
    qlo_s[...] = (q * (lo512 * HEAD_DIM ** -0.5)).astype(BF16)
    qhi_s[...] = (q * (hi512 * HEAD_DIM ** -0.5)).astype(BF16)

    low = lax.broadcasted_iota(jnp.int32, (tm, LANES), 1) < HALF
    kv = _dot(xn, win_ref[:, C_KV:C_KV + 2 * ATT_KV_W])
    k = kv[:, :ATT_KV_W]
    v = kv[:, ATT_KV_W:]
    k_r = pltpu.roll(k, HALF, axis=1)
    v_r = pltpu.roll(v, HALF, axis=1)
    kd0_s[BLK:BLK + tm, :] = jnp.where(low, k, k_r).astype(BF16)
    kd1_s[BLK:BLK + tm, :] = jnp.where(low, k_r, k).astype(BF16)
    vd0_s[BLK:BLK + tm, 0:LANES] = jnp.where(low, v, 1.0).astype(BF16)
    vd0_s[BLK:BLK + tm, LANES:2 * LANES] = jnp.where(low, 1.0, v_r).astype(BF16)
    vd1_s[BLK:BLK + tm, 0:LANES] = jnp.where(low, v_r, 1.0).astype(BF16)
    vd1_s[BLK:BLK + tm, LANES:2 * LANES] = jnp.where(low, 1.0, v).astype(BF16)

    @pl.when(t == nt - 1)
    def _():
        wk_ref[0] = k[tm - WINDOW:, :]
        wv_ref[0] = v[tm - WINDOW:, :]

    lo256, hi256 = _half_masks(RET_QK_W)
    qkr = _dot(xn, win_ref[:, C_QKR:C_QKR + 2 * RET_QK_W])
    qr = qkr[:, :RET_QK_W]
    qrlo_s[...] = (qr * lo256).astype(BF16)
    qrhi_s[...] = (qr * hi256).astype(BF16)
    kr_s[...] = qkr[:, RET_QK_W:] * (RET_QK_DIM ** -0.5)
    vr_s[...] = _dot(xn, win_ref[:, C_VR:C_VR + RET_V_W]).astype(BF16)
    gate_s[...] = _silu(_dot(xn, win_ref[:, C_GR:C_GR + RET_V_W]))

    lowb = lax.broadcasted_iota(jnp.int32, (BLK, LANES), 1) < HALF
    col = lax.broadcasted_iota(jnp.int32, (BLK, 2 * BLK), 1)
    first_mask = jnp.where((col < BLK) & (t == 0), NEG, 0.0)
    kd_refs = (kd0_s, kd1_s)
    vd_refs = (vd0_s, vd1_s)
    n_pairs = N_RET_HEADS // 2
    state = [st_ref[0, i * LANES:(i + 1) * LANES, :] for i in range(n_pairs)]

    for j in range(nblk):
        rows = slice(j * BLK, (j + 1) * BLK)
        krows = slice(j * BLK, (j + 2) * BLK)

        for kvh in range(N_KV_HEADS):
            kd = kd_refs[kvh][krows, :]
            vd = vd_refs[kvh][krows, :]
            c0 = kvh * KV_GROUP * HEAD_DIM
            qst = jnp.concatenate([qlo_s[rows, c0:c0 + LANES], qhi_s[rows, c0:c0 + LANES],
                                   qlo_s[rows, c0 + LANES:c0 + 2 * LANES],
                                   qhi_s[rows, c0 + LANES:c0 + 2 * LANES]], axis=0)
            s = _dot_nt(qst, kd)
            es, esink = [], []
            for g in range(KV_GROUP):
                h = kvh * KV_GROUP + g
                sg = s[g * BLK:(g + 1) * BLK] + bias_s[h]
                if j == 0:
                    sg = sg + first_mask
                sink = sinks_ref[h]
                m = jnp.maximum(jnp.max(sg, axis=-1, keepdims=True), sink)
                es.append(jnp.exp(sg - m).astype(BF16))
                esink.append(jnp.exp(sink - m))
            o = _dot(jnp.concatenate(es, axis=0), vd)
            for pair in range(KV_GROUP // 2):
                oe = o[2 * pair * BLK:(2 * pair + 1) * BLK]
                oo = o[(2 * pair + 1) * BLK:(2 * pair + 2) * BLK]
                num = jnp.where(lowb, oe[:, :LANES], oo[:, LANES:])
                den = (jnp.where(lowb, oe[:, LANES:], oo[:, :LANES])
                       + jnp.where(lowb, esink[2 * pair], esink[2 * pair + 1]))
                cs = c0 + pair * LANES
                mix_s[rows, cs:cs + LANES] = (num * (1.0 / den)).astype(BF16)

        for i in range(n_pairs):
            lsl = slice(i * LANES, (i + 1) * LANES)
            kp = kr_s[rows, lsl]
            sp = state[i]
            vpair = vr_s[rows, 2 * i * RET_V_DIM:(2 * i + 2) * RET_V_DIM]
            q2 = jnp.concatenate([qrlo_s[rows, lsl], qrhi_s[rows, lsl]], axis=0)
            a = _dot_nt(q2, kp.astype(BF16))
            inner = jnp.concatenate([a[:BLK] * decay_ref[2 * i], a[BLK:] * decay_ref[2 * i + 1]], axis=0)
            oi = _dot(inner.astype(BF16), vpair)
            oc = _dot(q2, sp.astype(BF16))
            for half in range(2):
                h = 2 * i + half
                vsl = slice(h * RET_V_DIM, (h + 1) * RET_V_DIM)
                hr = slice(half * BLK, (half + 1) * BLK)
                o = oi[hr, half * RET_V_DIM:(half + 1) * RET_V_DIM] + oc[hr] * xi_ref[:, vsl]
                r = _group_norm(o, gng_ref[:, vsl], gnb_ref[:, vsl]) * gate_s[rows, vsl]
                mix_s[rows, ATT_Q_W + h * RET_V_DIM:ATT_Q_W + (h + 1) * RET_V_DIM] = r.astype(BF16)
            kz = (kp * zeta_ref[:, lsl]).astype(BF16)
            u = _dot_tn(kz, vpair)
            state[i] = jnp.concatenate(
                [_GL_PROMPT[2 * i] * sp[:RET_QK_DIM] + u[:RET_QK_DIM, :RET_V_DIM],
                 _GL_PROMPT[2 * i + 1] * sp[RET_QK_DIM:] + u[RET_QK_DIM:, RET_V_DIM:]], axis=0)

    for i in range(n_pairs):
        st_ref[0, i * LANES:(i + 1) * LANES, :] = state[i]

    h_ref[0] = x + _dot(mix_s[...], wout_ref[...])


def _prompt_mixer(x, g_mix, w_in, w_out, sinks, gn_g, gn_b):
    b, s, d = x.shape
    tm = TM_MIX
    const = lambda shape: pl.BlockSpec(shape, lambda i, j: (0,) * len(shape))
    return pl.pallas_call(
        _prompt_mixer_kernel,
        grid=(b, s // tm),
        in_specs=[
            pl.BlockSpec(memory_space=pltpu.SMEM),
            pl.BlockSpec((1, tm, d), lambda i, j: (i, j, 0)),
            const((1, d)), const((d, D_IN)), const((MIX_OUT, d)),
            const((1, RET_V_W)), const((1, RET_V_W)),
            const((BLK, 2 * BLK)), const((BLK, 2 * BLK)),
            const((N_RET_HEADS, BLK, BLK)), const((BLK, RET_V_W)), const((BLK, RET_QK_W)),
        ],
        out_specs=[
            pl.BlockSpec((1, tm, d), lambda i, j: (i, j, 0)),
            pl.BlockSpec((1, WINDOW, ATT_KV_W), lambda i, j: (i, 0, 0)),
            pl.BlockSpec((1, WINDOW, ATT_KV_W), lambda i, j: (i, 0, 0)),
            pl.BlockSpec((1, RET_QK_W, RET_V_DIM), lambda i, j: (i, 0, 0)),
        ],
        out_shape=[
            jax.ShapeDtypeStruct((b, s, d), F32),
            jax.ShapeDtypeStruct((b, WINDOW, ATT_KV_W), F32),
            jax.ShapeDtypeStruct((b, WINDOW, ATT_KV_W), F32),
            jax.ShapeDtypeStruct((b, RET_QK_W, RET_V_DIM), F32),
        ],
        scratch_shapes=[
            pltpu.VMEM((tm, ATT_Q_W), BF16), pltpu.VMEM((tm, ATT_Q_W), BF16),
            pltpu.VMEM((tm + BLK, LANES), BF16), pltpu.VMEM((tm + BLK, LANES), BF16),
            pltpu.VMEM((tm + BLK, 2 * LANES), BF16), pltpu.VMEM((tm + BLK, 2 * LANES), BF16),
            pltpu.VMEM((tm, RET_QK_W), BF16), pltpu.VMEM((tm, RET_QK_W), BF16),
            pltpu.VMEM((tm, RET_QK_W), F32), pltpu.VMEM((tm, RET_V_W), BF16),
            pltpu.VMEM((tm, RET_V_W), F32), pltpu.VMEM((tm, MIX_OUT), BF16),
            pltpu.VMEM((N_ATT_HEADS, BLK, 2 * BLK), F32),
        ],
        compiler_params=pltpu.CompilerParams(
            dimension_semantics=("arbitrary", "arbitrary"), vmem_limit_bytes=VMEM_LIMIT),
        name="prompt_mixer",
    )(sinks, x, g_mix, w_in, w_out, gn_g, gn_b,
      jnp.asarray(_P_DIST), jnp.asarray(_P_MASK), jnp.asarray(_P_DECAY), jnp.asarray(_P_XI),
      jnp.asarray(_P_ZETA))


def _memkv_kernel(mem_ref, g_ref, wk_ref, wv_ref, mk_ref, mv_ref, mkb_ref, mvb_ref):
    mn = _rms(mem_ref[...], g_ref[...]).astype(BF16)
    mk = _dot(mn, wk_ref[...])
    mv = _dot(mn, wv_ref[...])
    tm = mem_ref.shape[0]
    group = X_D_HALVES * N_X_HEADS
    for hd in range(N_X_HEADS):
        for dh in range(X_D_HALVES):
            cols = slice(hd * X_HEAD_DIM + dh * LANES, hd * X_HEAD_DIM + (dh + 1) * LANES)
            rows = pl.ds(dh * N_X_HEADS + hd, tm, stride=group)
            mk_ref[rows, :] = mk[:, cols]
            mv_ref[rows, :] = mv[:, cols]
    mkb_ref[...] = mk.astype(BF16)
    mvb_ref[...] = mv.astype(BF16)


def _memory_kv(mem2d, g_mem, w_xk, w_xv):
    n, d = mem2d.shape
    tm = 512
    row = pl.BlockSpec((tm, d), lambda i: (i, 0))
    rows_out = pl.BlockSpec((tm * d // LANES, LANES), lambda i: (i, 0))
    const = lambda shape: pl.BlockSpec(shape, lambda i: (0,) * len(shape))
    return pl.pallas_call(
        _memkv_kernel,
        grid=(n // tm,),
        in_specs=[row, const((1, d)), const((d, d)), const((d, d))],
        out_specs=[rows_out, rows_out, row, row],
        out_shape=[jax.ShapeDtypeStruct((n * d // LANES, LANES), F32),
                   jax.ShapeDtypeStruct((n * d // LANES, LANES), F32),
                   jax.ShapeDtypeStruct((n, d), BF16), jax.ShapeDtypeStruct((n, d), BF16)],
        compiler_params=pltpu.CompilerParams(
            dimension_semantics=("arbitrary",), vmem_limit_bytes=VMEM_LIMIT),
        name="memory_kv",
    )(mem2d, g_mem, w_xk, w_xv)


def _prompt_xattn_kernel(h_ref, g_ref, wq_ref, wo_ref, mk_ref, mv_ref, out_ref, o_s):
    h = h_ref[0]
    xn = _rms(h, g_ref[...]).astype(BF16)
    q = (_dot(xn, wq_ref[...]) * (X_HEAD_DIM ** -0.5)).astype(BF16)
    for hd in range(N_X_HEADS):
        sl = slice(hd * X_HEAD_DIM, (hd + 1) * X_HEAD_DIM)
        s = _dot_nt(q[:, sl], mk_ref[0, :, sl])
        m = jnp.max(s, axis=-1, keepdims=True)
        p = jnp.exp(s - m)
        p = p * (1.0 / jnp.sum(p, axis=-1, keepdims=True))
        o_s[:, sl] = _dot(p.astype(BF16), mv_ref[0, :, sl]).astype(BF16)
    out_ref[0] = h + _dot(o_s[...], wo_ref[...])


def _prompt_xattn(h, g, w_xq, w_xo, mkb, mvb):
    b, s, d = h.shape
    tm = TM_X
    const = lambda shape: pl.BlockSpec(shape, lambda i, j: (0,) * len(shape))
    tok = pl.BlockSpec((1, tm, d), lambda i, j: (i, j, 0))
    mem = pl.BlockSpec((1, N_MEM, d), lambda i, j: (i, 0, 0))
    return pl.pallas_call(
        _prompt_xattn_kernel,
        grid=(b, s // tm),
        in_specs=[tok, const((1, d)), const((d, d)), const((d, d)), mem, mem],
        out_specs=tok,
        out_shape=jax.ShapeDtypeStruct((b, s, d), F32),
        scratch_shapes=[pltpu.VMEM((tm, d), BF16)],
        compiler_params=pltpu.CompilerParams(
            dimension_semantics=("arbitrary", "arbitrary"), vmem_limit_bytes=VMEM_LIMIT),
        name="prompt_xattn",
    )(h, g, w_xq, w_xo, mkb, mvb)


def _mlp_kernel(h_ref, g_ref, wup_ref, wdn_ref, gf_ref, y_ref):
    h = h_ref[...]
    xn = _rms(h, g_ref[...]).astype(BF16)
    acc = h
    for c in range(D_FF // FF_CHUNK):
        sl = slice(c * FF_CHUNK, (c + 1) * FF_CHUNK)
        u = jnp.maximum(_dot(xn, wup_ref[:, sl]), 0.0)
        acc = acc + _dot((u * u).astype(BF16), wdn_ref[sl, :])
    y_ref[...] = _rms(acc, gf_ref[...])


def _mlp_final(h2d, g_mlp, w_up, w_down, g_final):
    n, d = h2d.shape
    tm = TM_MLP
    row = pl.BlockSpec((tm, d), lambda i: (i, 0))
    const = lambda shape: pl.BlockSpec(shape, lambda i: (0,) * len(shape))
    return pl.pallas_call(
        _mlp_kernel,
        grid=(n // tm,),
        in_specs=[row, const((1, d)), const((d, D_FF)), const((D_FF, d)), const((1, d))],
        out_specs=row,
        out_shape=jax.ShapeDtypeStruct((n, d), F32),
        compiler_params=pltpu.CompilerParams(
            dimension_semantics=("arbitrary",), vmem_limit_bytes=VMEM_LIMIT),
        name="mlp_final",
    )(h2d, g_mlp, w_up, w_down, g_final)


def _sample_mixer_kernel(sinks_ref, x_ref, gmix_ref, win_ref, wout_ref, gng_ref, gnb_ref,
                         ck_ref, cv_ref, st_ref, bias_ref, dec_ref, xi_ref, zeta_ref,
                         h_ref, swk_ref, swv_ref, sst_ref):
    bb = ck_ref.shape[0]
    nt = bb // 2
    x = x_ref[...]
    xn = _rms(x, gmix_ref[...]).astype(BF16)
    tile3 = lambda a: a.reshape(nt, SUBLANES, a.shape[-1])

    q = _dot(xn, win_ref[:, C_QA:C_QA + ATT_Q_W]) * (HEAD_DIM ** -0.5)
    kv = _dot(xn, win_ref[:, C_KV:C_KV + 2 * ATT_KV_W])
    qkr = _dot(xn, win_ref[:, C_QKR:C_QKR + 2 * RET_QK_W])
    vr = _dot(xn, win_ref[:, C_VR:C_VR + RET_V_W])
    gate3 = tile3(_silu(_dot(xn, win_ref[:, C_GR:C_GR + RET_V_W])))

    lo512, hi512 = _half_masks(ATT_Q_W)
    q_r = pltpu.roll(q, HALF, axis=1)
    q_nat3 = tile3(q)
    q_rot3 = tile3(q_r)
    lo3 = lo512.reshape(1, 1, ATT_Q_W)
    hi3 = hi512.reshape(1, 1, ATT_Q_W)
    qa3 = (q_nat3 * lo3).astype(BF16)
    qb3 = (q_rot3 * lo3).astype(BF16)
    qc3 = (q_rot3 * hi3).astype(BF16)
    qd3 = (q_nat3 * hi3).astype(BF16)
    t128 = lambda a, i: a[:, :, i * LANES:(i + 1) * LANES]
    qs = jnp.concatenate([t128(qa3, 0), t128(qb3, 1), t128(qa3, 1), t128(qb3, 2),
                          t128(qc3, 2), t128(qd3, 2), t128(qc3, 3), t128(qd3, 3)], axis=1)

    k3 = tile3(kv[:, :ATT_KV_W])
    v3 = tile3(kv[:, ATT_KV_W:])
    pad_kv = jnp.zeros((nt, BLK - SUBLANES, LANES), BF16)

    lo256, _ = _half_masks(RET_QK_W)
    qr3 = tile3(qkr[:, :RET_QK_W])
    kr3 = tile3(qkr[:, RET_QK_W:] * (RET_QK_DIM ** -0.5))
    vr3 = tile3(vr)
    lane256 = lax.broadcasted_iota(jnp.int32, (1, 1, RET_QK_W), 2)
    qrs = jnp.concatenate(
        [(qr3 * ((lane256 >= h * RET_QK_DIM) & (lane256 < (h + 1) * RET_QK_DIM)).astype(F32)).astype(BF16)
         for h in range(N_RET_HEADS)],
        axis=1)
    kr_pad = jnp.concatenate([kr3.astype(BF16), jnp.zeros((nt, BLK - SUBLANES, RET_QK_W), BF16)], axis=1)
    vr_pad = jnp.concatenate([vr3.astype(BF16), jnp.zeros((nt, BLK - SUBLANES, RET_V_W), BF16)], axis=1)

    lane = lax.broadcasted_iota(jnp.int32, (1, 1, LANES), 2)
    row8 = lax.broadcasted_iota(jnp.int32, (1, SUBLANES, 1), 1)
    bmm_nt = lambda a, b: jnp.einsum('bqd,bkd->bqk', a, b, preferred_element_type=F32)
    bmm = lambda a, b: jnp.einsum('bqk,bkd->bqd', a, b, preferred_element_type=F32)

    att_par, ret_par = [], []
    for par in range(2):
        bsl = pl.ds(par, nt, stride=2)
        ck = ck_ref[bsl]
        cv = cv_ref[bsl]
        swk_ref[bsl, 0:WINDOW - DEC_SEQ, :] = ck[:, DEC_SEQ:, :]
        swv_ref[bsl, 0:WINDOW - DEC_SEQ, :] = cv[:, DEC_SEQ:, :]
        swk_ref[bsl, WINDOW - DEC_SEQ:WINDOW, :] = k3[:, DEC_SEQ * par:DEC_SEQ * (par + 1), :]
        swv_ref[bsl, WINDOW - DEC_SEQ:WINDOW, :] = v3[:, DEC_SEQ * par:DEC_SEQ * (par + 1), :]

        kfull = jnp.concatenate([ck.astype(BF16), k3.astype(BF16), pad_kv], axis=1)
        vfull = jnp.concatenate([cv.astype(BF16), v3.astype(BF16), pad_kv], axis=1)
        s = bmm_nt(qs, kfull) + bias_ref[par]
        ps = []
        for h in range(N_ATT_HEADS):
            ps.append(_sink_softmax(s[:, h * SUBLANES:(h + 1) * SUBLANES, :], sinks_ref[h]).astype(BF16))
        o = bmm(jnp.concatenate(ps, axis=1), vfull)
        o_r = pltpu.roll(o.reshape(nt * N_ATT_HEADS * SUBLANES, LANES), HALF, axis=1).reshape(o.shape)
        hr = lambda a, h: a[:, h * SUBLANES:(h + 1) * SUBLANES, :]
        low = lane < HALF
        att_par.append(jnp.concatenate([
            jnp.where(low, hr(o, 0), hr(o_r, 1)), jnp.where(low, hr(o, 2), hr(o_r, 3)),
            jnp.where(low, hr(o_r, 4), hr(o, 5)), jnp.where(low, hr(o_r, 6), hr(o, 7))], axis=2))

        st = st_ref[bsl]
        oc = bmm(qrs, st.astype(BF16))
        inner = (bmm_nt(qrs, kr_pad) * dec_ref[par]).astype(BF16)
        oi = bmm(inner, vr_pad)
        rs = []
        for h in range(N_RET_HEADS):
            vsl = slice(h * RET_V_DIM, (h + 1) * RET_V_DIM)
            rsl = slice(h * SUBLANES, (h + 1) * SUBLANES)
            o_h = oi[:, rsl, vsl] + oc[:, rsl, :] * xi_ref[par, rsl, :]
            rs.append(_group_norm(o_h, gng_ref[:, vsl], gnb_ref[:, vsl]) * gate3[:, :, vsl])
        ret_par.append(jnp.concatenate(rs, axis=2))

        kz3 = (kr3 * zeta_ref[par]).astype(BF16)
        vr3_b = vr3.astype(BF16)
        for p in range(nt):
            u = _dot_tn(kz3[p], vr3_b[p])
            for h in range(N_RET_HEADS):
                dsl = slice(h * RET_QK_DIM, (h + 1) * RET_QK_DIM)
                sst_ref[2 * p + par, dsl, :] = (
                    _GL_SAMPLE[h] * st[p, dsl, :] + u[dsl, h * RET_V_DIM:(h + 1) * RET_V_DIM])

    own0 = row8 < DEC_SEQ
    att3 = jnp.where(own0, att_par[0], att_par[1])
    ret3 = jnp.where(own0, ret_par[0], ret_par[1])
    mix = jnp.concatenate([att3, ret3], axis=2).reshape(2 * nt * DEC_SEQ, MIX_OUT).astype(BF16)
    h_ref[...] = x + _dot(mix, wout_ref[...])


def _sample_mixer(x2d, g_mix, w_in, w_out, sinks, gn_g, gn_b, ck, cv, st):
    n, d = x2d.shape
    nb = ck.shape[0]
    bb = BB_MIX
    r = bb * DEC_SEQ
    const = lambda shape: pl.BlockSpec(shape, lambda i: (0,) * len(shape))
    row = pl.BlockSpec((r, d), lambda i: (i, 0))
    win = pl.BlockSpec((bb, WINDOW, ATT_KV_W), lambda i: (i, 0, 0))
    state = pl.BlockSpec((bb, RET_QK_W, RET_V_DIM), lambda i: (i, 0, 0))
    return pl.pallas_call(
        _sample_mixer_kernel,
        grid=(nb // bb,),
        in_specs=[
            pl.BlockSpec(memory_space=pltpu.SMEM),
            row, const((1, d)), const((d, D_IN)), const((MIX_OUT, d)),
            const((1, RET_V_W)), const((1, RET_V_W)),
            win, win, state,
            const(_S_BIAS.shape), const(_S_DEC.shape), const(_S_XI.shape), const(_S_ZETA.shape),
        ],
        out_specs=[row, win, win, state],
        out_shape=[
            jax.ShapeDtypeStruct((n, d), F32),
            jax.ShapeDtypeStruct((nb, WINDOW, ATT_KV_W), F32),
            jax.ShapeDtypeStruct((nb, WINDOW, ATT_KV_W), F32),
            jax.ShapeDtypeStruct((nb, RET_QK_W, RET_V_DIM), F32),
        ],
        compiler_params=pltpu.CompilerParams(
            dimension_semantics=("arbitrary",), vmem_limit_bytes=VMEM_LIMIT),
        name="sample_mixer",
    )(sinks, x2d, g_mix, w_in, w_out, gn_g, gn_b, ck, cv, st,
      jnp.asarray(_S_BIAS), jnp.asarray(_S_DEC), jnp.asarray(_S_XI), jnp.asarray(_S_ZETA))


def _head_slab(x_ref, b, hd):
    group = X_D_HALVES * N_X_HEADS
    halves = [x_ref[b, pl.ds(dh * N_X_HEADS + hd, N_MEM, stride=group), :] for dh in range(X_D_HALVES)]
    return jnp.concatenate(halves, axis=1).astype(BF16)


def _sample_xattn_kernel(h_ref, g_ref, wq_ref, wo_ref, xk_ref, xv_ref, out_ref):
    bb = xk_ref.shape[0]
    nt = bb // 2
    h = h_ref[...]
    xn = _rms(h, g_ref[...]).astype(BF16)
    q = _dot(xn, wq_ref[...]) * (X_HEAD_DIM ** -0.5)
    units = [(t, par, hd) for t in range(nt) for par in range(2) for hd in range(N_X_HEADS)]
    qts = [q[t * SUBLANES:(t + 1) * SUBLANES].astype(BF16) for t in range(nt)]
    s = jnp.concatenate(
        [_dot_nt(qts[t][:, hd * X_HEAD_DIM:(hd + 1) * X_HEAD_DIM], _head_slab(xk_ref, 2 * t + par, hd))
         for t, par, hd in units], axis=0)
    m = jnp.max(s, axis=-1, keepdims=True)
    p = jnp.exp(s - m)
    p = p * (1.0 / jnp.sum(p, axis=-1, keepdims=True))
    os_ = {}
    for i, (t, par, hd) in enumerate(units):
        pi = p[i * SUBLANES:(i + 1) * SUBLANES].astype(BF16)
        os_[(t, par, hd)] = _dot(pi, _head_slab(xv_ref, 2 * t + par, hd))
    own0 = lax.broadcasted_iota(jnp.int32, (SUBLANES, 1), 0) < DEC_SEQ
    o_tiles = []
    for t in range(nt):
        o_par = [jnp.concatenate([os_[(t, par, hd)] for hd in range(N_X_HEADS)], axis=1) for par in range(2)]
        o_tiles.append(jnp.where(own0, o_par[0], o_par[1]))
    o = jnp.concatenate(o_tiles, axis=0).astype(BF16)
    out_ref[...] = h + _dot(o, wo_ref[...])


def _mem_rows(c):
    nb = c.shape[0]
    c = c.reshape(nb, N_MEM, N_X_HEADS, X_D_HALVES, LANES)
    return jnp.transpose(c, (0, 1, 3, 2, 4)).reshape(nb, N_MEM * X_D_HALVES * N_X_HEADS, LANES)


def _sample_xattn(h2d, g, w_xq, w_xo, mk, mv):
    n, d = h2d.shape
    nb = mk.shape[0]
    bb = BB_X
    r = bb * DEC_SEQ
    const = lambda shape: pl.BlockSpec(shape, lambda i: (0,) * len(shape))
    row = pl.BlockSpec((r, d), lambda i: (i, 0))
    mem = pl.BlockSpec((bb,) + mk.shape[1:], lambda i: (i, 0, 0))
    return pl.pallas_call(
        _sample_xattn_kernel,
        grid=(nb // bb,),
        in_specs=[row, const((1, d)), const((d, d)), const((d, d)), mem, mem],
        out_specs=row,
        out_shape=jax.ShapeDtypeStruct((n, d), F32),
        compiler_params=pltpu.CompilerParams(
            dimension_semantics=("arbitrary",), vmem_limit_bytes=VMEM_LIMIT),
        name="sample_xattn",
    )(h2d, g, w_xq, w_xo, mk, mv)


def kernel(x_prompt, x_sample, mem_prompt, cache_win_k, cache_win_v, state_ret, cache_mem_k, cache_mem_v,
           g_mix, w_in, attn_sinks, ret_gn_g, ret_gn_b, w_out, g_xattn, g_mem, w_xq, w_xk, w_xv, w_xo,
           g_mlp, w_up, w_down, g_final):
    depth = w_in.shape[0]
    assert depth == 1, "single-layer trunk"
    b, s, d = x_prompt.shape
    nb, ls, _ = x_sample.shape
    row = lambda a: a.reshape(1, -1)
    bf = lambda a: a.astype(BF16)

    w_in_b, w_out_b = bf(w_in[0]), bf(w_out[0])
    w_xq_b, w_xk_b, w_xv_b, w_xo_b = bf(w_xq[0]), bf(w_xk[0]), bf(w_xv[0]), bf(w_xo[0])
    w_up_b, w_dn_b = bf(w_up[0]), bf(w_down[0])
    sinks = attn_sinks[0]
    gn_g, gn_b = row(ret_gn_g[0]), row(ret_gn_b[0])
    g_fin = row(g_final)

    hp, p_wk, p_wv, p_rs = _prompt_mixer(x_prompt, row(g_mix[0]), w_in_b, w_out_b, sinks, gn_g, gn_b)
    mk, mv, mkb, mvb = _memory_kv(mem_prompt.reshape(b * N_MEM, d), row(g_mem[0]), w_xk_b, w_xv_b)
    hp = _prompt_xattn(hp, row(g_xattn[0]), w_xq_b, w_xo_b,
                       mkb.reshape(b, N_MEM, d), mvb.reshape(b, N_MEM, d))
    y_prompt = _mlp_final(hp.reshape(b * s, d), row(g_mlp[0]), w_up_b, w_dn_b, g_fin).reshape(b, s, d)

    hs, s_wk, s_wv, s_rs = _sample_mixer(
        x_sample.reshape(nb * ls, d), row(g_mix[0]), w_in_b, w_out_b, sinks, gn_g, gn_b,
        cache_win_k[0].reshape(nb, WINDOW, ATT_KV_W), cache_win_v[0].reshape(nb, WINDOW, ATT_KV_W),
        state_ret[0].reshape(nb, RET_QK_W, RET_V_DIM))
    hs = _sample_xattn(hs, row(g_xattn[0]), w_xq_b, w_xo_b,
                       _mem_rows(cache_mem_k[0]), _mem_rows(cache_mem_v[0]))
    y_sample = _mlp_final(hs, row(g_mlp[0]), w_up_b, w_dn_b, g_fin).reshape(nb, ls, d)

    win5 = lambda a, n: a.reshape(1, n, WINDOW, N_KV_HEADS, HEAD_DIM)
    ret5 = lambda a, n: a.reshape(1, n, N_RET_HEADS, RET_QK_DIM, RET_V_DIM)
    mem5 = lambda a: jnp.transpose(a.reshape(b, N_MEM, X_D_HALVES, N_X_HEADS, LANES),
                                   (0, 1, 3, 2, 4)).reshape(1, b, N_MEM, N_X_HEADS, X_HEAD_DIM)
    return (y_prompt, y_sample,
            win5(p_wk, b), win5(p_wv, b), ret5(p_rs, b), mem5(mk), mem5(mv),
            win5(s_wk, nb), win5(s_wv, nb), ret5(s_rs, nb))
```

```python
import functools

import jax
import jax.numpy as jnp
import numpy as np
from jax import lax
from jax.experimental import pallas as pl
from jax.experimental.pallas import tpu as pltpu

F32 = jnp.float32
BF16 = jnp.bfloat16

D_MODEL = 1024
BATCH = 8
SEQ = 2048
DEC_BATCH = 128
DEC_SEQ = 4
HEAD_DIM = 64
N_ATT_HEADS = 8
N_KV_HEADS = 2
KV_GROUP = N_ATT_HEADS // N_KV_HEADS
WINDOW = 128
BLK = 128
N_RET_HEADS = 4
RET_QK_DIM = 64
RET_V_DIM = 128
N_MEM = 256
N_X_HEADS = 4
X_HEAD_DIM = D_MODEL // N_X_HEADS
D_FF = 4 * D_MODEL
RMS_EPS = 1e-6
GN_EPS = 1e-5

ATT_Q_W = N_ATT_HEADS * HEAD_DIM
ATT_KV_W = N_KV_HEADS * HEAD_DIM
RET_QK_W = N_RET_HEADS * RET_QK_DIM
RET_V_W = N_RET_HEADS * RET_V_DIM
MIX_OUT = ATT_Q_W + RET_V_W
D_IN = ATT_Q_W + 2 * ATT_KV_W + 2 * RET_QK_W + 2 * RET_V_W
C_QA, C_KV, C_QKR, C_VR, C_GR = 0, 512, 768, 1280, 1792

LANES = 128
SUBLANES = 8
HALF = LANES // 2
X_D_HALVES = X_HEAD_DIM // LANES
NEG = -1e30
VMEM_LIMIT = 52 * 1024 * 1024

TM_MIX = 512
TM_X = 1024
TM_MLP = 1024
SUB_ROWS = 512
FF_CHUNK = 1024
BB_MIX = 16
BB_X = 8

NEG_SLOPES = [-(2.0 ** (-8.0 * (i + 1) / N_ATT_HEADS)) for i in range(N_ATT_HEADS)]
_LOG_G = np.log(1.0 - 2.0 ** (-5.0 - np.arange(N_RET_HEADS))).astype(np.float32).astype(np.float64)


def _prompt_tables():
    qi = np.arange(BLK)[:, None]
    kj = np.arange(2 * BLK)[None, :]
    dist = (qi + BLK - kj).astype(np.float64)
    mask = np.where((dist >= 0) & (dist < WINDOW), 0.0, NEG)
    l = np.arange(BLK, dtype=np.float64)
    diff = l[:, None] - l[None, :]
    decay = np.where(diff >= 0, np.exp(_LOG_G[:, None, None] * np.maximum(diff, 0.0)), 0.0)
    xi = np.exp((l[:, None] + 1.0) * _LOG_G[None, :])
    zeta = np.exp((BLK - 1.0 - l)[:, None] * _LOG_G[None, :])
    xi_t = np.repeat(xi, RET_V_DIM, axis=1)
    zeta_t = np.repeat(zeta, RET_QK_DIM, axis=1)
    f = lambda a: np.asarray(a, np.float32)
    return f(dist), f(mask), f(decay), f(xi_t), f(zeta_t)


def _sample_tables():
    slopes = -np.asarray(NEG_SLOPES)
    bias = np.full((2, N_ATT_HEADS * SUBLANES, 2 * BLK), NEG, np.float64)
    dec = np.zeros((2, N_RET_HEADS * SUBLANES, BLK), np.float64)
    xi = np.zeros((2, N_RET_HEADS * SUBLANES, RET_V_DIM), np.float64)
    zeta = np.zeros((2, SUBLANES, RET_QK_W), np.float64)
    for par in range(2):
        for r in range(SUBLANES):
            own = DEC_SEQ * par <= r < DEC_SEQ * (par + 1)
            t = r - DEC_SEQ * par if own else r % DEC_SEQ
            for h in range(N_ATT_HEADS):
                row = h * SUBLANES + r
                for j in range(WINDOW):
                    d = t + WINDOW - j
                    if 0 <= d < WINDOW:
                        bias[par, row, j] = -slopes[h] * d
                for c in range(DEC_SEQ):
                    d = t - c
                    if d >= 0:
                        bias[par, row, WINDOW + DEC_SEQ * par + c] = -slopes[h] * d
            for h in range(N_RET_HEADS):
                row = h * SUBLANES + r
                if own:
                    xi[par, row, :] = np.exp((t + 1.0) * _LOG_G[h])
                    zeta[par, r, h * RET_QK_DIM:(h + 1) * RET_QK_DIM] = np.exp((DEC_SEQ - 1.0 - t) * _LOG_G[h])
                    for c in range(t + 1):
                        dec[par, row, DEC_SEQ * par + c] = np.exp(_LOG_G[h] * (t - c))
    f = lambda a: np.asarray(a, np.float32)
    return f(bias), f(dec), f(xi), f(zeta)


_P_DIST, _P_MASK, _P_DECAY, _P_XI, _P_ZETA = _prompt_tables()
_S_BIAS, _S_DEC, _S_XI, _S_ZETA = _sample_tables()
_GL_PROMPT = [float(np.exp(_LOG_G[h] * BLK)) for h in range(N_RET_HEADS)]
_GL_SAMPLE = [float(np.exp(_LOG_G[h] * DEC_SEQ)) for h in range(N_RET_HEADS)]


def _rms(x, g):
    return x * lax.rsqrt(jnp.mean(x * x, axis=-1, keepdims=True) + RMS_EPS) * g


def _dot(a, b):
    return jnp.dot(a, b, preferred_element_type=F32)


def _dot_nt(a, b):
    return lax.dot_general(a, b, (((1,), (1,)), ((), ())), preferred_element_type=F32)


def _dot_tn(a, b):
    return lax.dot_general(a, b, (((0,), (0,)), ((), ())), preferred_element_type=F32)


def _silu(g):
    return g * (1.0 / (1.0 + jnp.exp(-g)))


def _half_masks(width):
    lane = lax.broadcasted_iota(jnp.int32, (1, width), 1)
    lo = ((lane & (LANES - 1)) < HALF).astype(F32)
    return lo, 1.0 - lo


def _sink_softmax(s, sink):
    m = jnp.maximum(jnp.max(s, axis=-1, keepdims=True), sink)
    p = jnp.exp(s - m)
    den = jnp.sum(p, axis=-1, keepdims=True) + jnp.exp(sink - m)
    return p * (1.0 / den)


def _group_norm(o, g, b):
    mu = jnp.mean(o, axis=-1, keepdims=True)
    d = o - mu
    var = jnp.mean(d * d, axis=-1, keepdims=True)
    return d * lax.rsqrt(var + GN_EPS) * g + b


def _prompt_mixer_kernel(sinks_ref, x_ref, gmix_ref, win_ref, wout_ref, gng_ref, gnb_ref,
                         dist_ref, mask_ref, decay_ref, xi_ref, zeta_ref,
                         h_ref, wk_ref, wv_ref, st_ref,
                         qlo_s, qhi_s, kd0_s, kd1_s, vd0_s, vd1_s,
                         qrlo_s, qrhi_s, kr_s, vr_s, gate_s, mix_s, bias_s):
    t = pl.program_id(1)
    nt = pl.num_programs(1)
    tm = x_ref.shape[1]
    nblk = tm // BLK

    @pl.when(t == 0)
    def _():
        kd0_s[0:BLK, :] = jnp.zeros((BLK, LANES), BF16)
        kd1_s[0:BLK, :] = jnp.zeros((BLK, LANES), BF16)
        vd0_s[0:BLK, :] = jnp.zeros((BLK, 2 * LANES), BF16)
        vd1_s[0:BLK, :] = jnp.zeros((BLK, 2 * LANES), BF16)
        st_ref[...] = jnp.zeros_like(st_ref)

    @pl.when(t > 0)
    def _():
        kd0_s[0:BLK, :] = kd0_s[tm:tm + BLK, :]
        kd1_s[0:BLK, :] = kd1_s[tm:tm + BLK, :]
        vd0_s[0:BLK, :] = vd0_s[tm:tm + BLK, :]
        vd1_s[0:BLK, :] = vd1_s[tm:tm + BLK, :]

    @pl.when((t == 0) & (pl.program_id(0) == 0))
    def _():
        for h in range(N_ATT_HEADS):
            bias_s[h] = NEG_SLOPES[h] * dist_ref[...] + mask_ref[...]

    x = x_ref[0]
    xn = _rms(x, gmix_ref[...]).astype(BF16)

    z = _dot(xn, win_ref[...])
    lo512, hi512 = _half_masks(ATT_Q_W)
    q = z[:, C_QA:C_QA + ATT_Q_W]
    qlo_s[...] = (q * (lo512 * HEAD_DIM ** -0.5)).astype(BF16)
    qhi_s[...] = (q * (hi512 * HEAD_DIM ** -0.5)).astype(BF16)

    low = lax.broadcasted_iota(jnp.int32, (tm, LANES), 1) < HALF
    kv = z[:, C_KV:C_KV + 2 * ATT_KV_W]
    k = kv[:, :ATT_KV_W]
    v = kv[:, ATT_KV_W:]
    k_r = pltpu.roll(k, HALF, axis=1)
    v_r = pltpu.roll(v, HALF, axis=1)
    kd0_s[BLK:BLK + tm, :] = jnp.where(low, k, k_r).astype(BF16)
    kd1_s[BLK:BLK + tm, :] = jnp.where(low, k_r, k).astype(BF16)
    vd0_s[BLK:BLK + tm, 0:LANES] = jnp.where(low, v, 1.0).astype(BF16)
    vd0_s[BLK:BLK + tm, LANES:2 * LANES] = jnp.where(low, 1.0, v_r).astype(BF16)
    vd1_s[BLK:BLK + tm, 0:LANES] = jnp.where(low, v_r, 1.0).astype(BF16)
    vd1_s[BLK:BLK + tm, LANES:2 * LANES] = jnp.where(low, 1.0, v).astype(BF16)

    @pl.when(t == nt - 1)
    def _():
        wk_ref[0] = k[tm - WINDOW:, :]
        wv_ref[0] = v[tm - WINDOW:, :]

    lo256, hi256 = _half_masks(RET_QK_W)
    qkr = z[:, C_QKR:C_QKR + 2 * RET_QK_W]
    qr = qkr[:, :RET_QK_W]
    qrlo_s[...] = (qr * lo256).astype(BF16)
    qrhi_s[...] = (qr * hi256).astype(BF16)
    kr_s[...] = qkr[:, RET_QK_W:] * (RET_QK_DIM ** -0.5)
    vr_s[...] = z[:, C_VR:C_VR + RET_V_W].astype(BF16)
    gate_s[...] = _silu(z[:, C_GR:C_GR + RET_V_W])

    lowb =lax.broadcasted_iota(jnp.int32, (BLK, LANES), 1) < HALF
    col = lax.broadcasted_iota(jnp.int32, (BLK, 2 * BLK), 1)
    first_mask = jnp.where((col < BLK) & (t == 0), NEG, 0.0)
    kd_refs = (kd0_s, kd1_s)
    vd_refs = (vd0_s, vd1_s)
    n_pairs = N_RET_HEADS // 2
    state = [st_ref[0, i * LANES:(i + 1) * LANES, :] for i in range(n_pairs)]

    for j in range(nblk):
        rows = slice(j * BLK, (j + 1) * BLK)
        krows = slice(j * BLK, (j + 2) * BLK)

        for kvh in range(N_KV_HEADS):
            kd = kd_refs[kvh][krows, :]
            vd = vd_refs[kvh][krows, :]
            c0 = kvh * KV_GROUP * HEAD_DIM
            qst = jnp.concatenate([qlo_s[rows, c0:c0 + LANES], qhi_s[rows, c0:c0 + LANES],
                                   qlo_s[rows, c0 + LANES:c0 + 2 * LANES],
                                   qhi_s[rows, c0 + LANES:c0 + 2 * LANES]], axis=0)
            s = _dot_nt(qst, kd)
            es, esink = [], []
            for g in range(KV_GROUP):
                h = kvh * KV_GROUP + g
                sg = s[g * BLK:(g + 1) * BLK] + bias_s[h]
                if j == 0:
                    sg = sg + first_mask
                sink = sinks_ref[h]
                m = jnp.maximum(jnp.max(sg, axis=-1, keepdims=True), sink)
                es.append(jnp.exp(sg - m).astype(BF16))
                esink.append(jnp.exp(sink - m))
            o = _dot(jnp.concatenate(es, axis=0), vd)
            for pair in range(KV_GROUP // 2):
                oe = o[2 * pair * BLK:(2 * pair + 1) * BLK]
                oo = o[(2 * pair + 1) * BLK:(2 * pair + 2) * BLK]
                num = jnp.where(lowb, oe[:, :LANES], oo[:, LANES:])
                den = (jnp.where(lowb, oe[:, LANES:], oo[:, :LANES])
                       + jnp.where(lowb, esink[2 * pair], esink[2 * pair + 1]))
                cs = c0 + pair * LANES
                mix_s[rows, cs:cs + LANES] = (num * (1.0 / den)).astype(BF16)

        for i in range(n_pairs):
            lsl = slice(i * LANES, (i + 1) * LANES)
            kp = kr_s[rows, lsl]
            sp = state[i]
            vpair = vr_s[rows, 2 * i * RET_V_DIM:(2 * i + 2) * RET_V_DIM]
            q2 = jnp.concatenate([qrlo_s[rows, lsl], qrhi_s[rows, lsl]], axis=0)
            a = _dot_nt(q2, kp.astype(BF16))
            inner = jnp.concatenate([a[:BLK] * decay_ref[2 * i], a[BLK:] * decay_ref[2 * i + 1]], axis=0)
            oi = _dot(inner.astype(BF16), vpair)
            oc = _dot(q2, sp.astype(BF16))
            for half in range(2):
                h = 2 * i + half
                vsl = slice(h * RET_V_DIM, (h + 1) * RET_V_DIM)
                hr = slice(half * BLK, (half + 1) * BLK)
                o = oi[hr, half * RET_V_DIM:(half + 1) * RET_V_DIM] + oc[hr] * xi_ref[:, vsl]
                r = _group_norm(o, gng_ref[:, vsl], gnb_ref[:, vsl]) * gate_s[rows, vsl]
                mix_s[rows, ATT_Q_W + h * RET_V_DIM:ATT_Q_W + (h + 1) * RET_V_DIM] = r.astype(BF16)
            kz = (kp * zeta_ref[:, lsl]).astype(BF16)
            u = _dot_tn(kz, vpair)
            state[i] = jnp.concatenate(
                [_GL_PROMPT[2 * i] * sp[:RET_QK_DIM] + u[:RET_QK_DIM, :RET_V_DIM],
                 _GL_PROMPT[2 * i + 1] * sp[RET_QK_DIM:] + u[RET_QK_DIM:, RET_V_DIM:]], axis=0)

    for i in range(n_pairs):
        st_ref[0, i * LANES:(i + 1) * LANES, :] = state[i]

    h_ref[0] = x + _dot(mix_s[...], wout_ref[...])


def _prompt_mixer(x, g_mix, w_in, w_out, sinks, gn_g, gn_b):
    b, s, d = x.shape
    tm = TM_MIX
    const = lambda shape: pl.BlockSpec(shape, lambda i, j: (0,) * len(shape))
    return pl.pallas_call(
        _prompt_mixer_kernel,
        grid=(b, s // tm),
        in_specs=[
            pl.BlockSpec(memory_space=pltpu.SMEM),
            pl.BlockSpec((1, tm, d), lambda i, j: (i, j, 0)),
            const((1, d)), const((d, D_IN)), const((MIX_OUT, d)),
            const((1, RET_V_W)), const((1, RET_V_W)),
            const((BLK, 2 * BLK)), const((BLK, 2 * BLK)),
            const((N_RET_HEADS, BLK, BLK)), const((BLK, RET_V_W)), const((BLK, RET_QK_W)),
        ],
        out_specs=[
            pl.BlockSpec((1, tm, d), lambda i, j: (i, j, 0)),
            pl.BlockSpec((1, WINDOW, ATT_KV_W), lambda i, j: (i, 0, 0)),
            pl.BlockSpec((1, WINDOW, ATT_KV_W), lambda i, j: (i, 0, 0)),
            pl.BlockSpec((1, RET_QK_W, RET_V_DIM), lambda i, j: (i, 0, 0)),
        ],
        out_shape=[
            jax.ShapeDtypeStruct((b, s, d), F32),
            jax.ShapeDtypeStruct((b, WINDOW, ATT_KV_W), F32),
            jax.ShapeDtypeStruct((b, WINDOW, ATT_KV_W), F32),
            jax.ShapeDtypeStruct((b, RET_QK_W, RET_V_DIM), F32),
        ],
        scratch_shapes=[
            pltpu.VMEM((tm, ATT_Q_W), BF16), pltpu.VMEM((tm, ATT_Q_W), BF16),
            pltpu.VMEM((tm + BLK, LANES), BF16), pltpu.VMEM((tm + BLK, LANES), BF16),
            pltpu.VMEM((tm + BLK, 2 * LANES), BF16), pltpu.VMEM((tm + BLK, 2 * LANES), BF16),
            pltpu.VMEM((tm, RET_QK_W), BF16), pltpu.VMEM((tm, RET_QK_W), BF16),
            pltpu.VMEM((tm, RET_QK_W), F32), pltpu.VMEM((tm, RET_V_W), BF16),
            pltpu.VMEM((tm, RET_V_W), F32), pltpu.VMEM((tm, MIX_OUT), BF16),
            pltpu.VMEM((N_ATT_HEADS, BLK, 2 * BLK), F32),
        ],
        compiler_params=pltpu.CompilerParams(
            dimension_semantics=("arbitrary", "arbitrary"), vmem_limit_bytes=VMEM_LIMIT),
        name="prompt_mixer",
    )(sinks, x, g_mix, w_in, w_out, gn_g, gn_b,
      jnp.asarray(_P_DIST), jnp.asarray(_P_MASK), jnp.asarray(_P_DECAY), jnp.asarray(_P_XI),
      jnp.asarray(_P_ZETA))


def _memkv_kernel(mem_ref, g_ref, wk_ref, wv_ref, mk_ref, mv_ref, mkb_ref, mvb_ref):
    mn = _rms(mem_ref[...], g_ref[...]).astype(BF16)
    mk = _dot(mn, wk_ref[...])
    mv = _dot(mn, wv_ref[...])
    tm = mem_ref.shape[0]
    group = X_D_HALVES * N_X_HEADS
    for hd in range(N_X_HEADS):
        for dh in range(X_D_HALVES):
            cols = slice(hd * X_HEAD_DIM + dh * LANES, hd * X_HEAD_DIM + (dh + 1) * LANES)
            rows = pl.ds(dh * N_X_HEADS + hd, tm, stride=group)
            mk_ref[rows, :] = mk[:, cols]
            mv_ref[rows, :] = mv[:, cols]
    mkb_ref[...] = mk.astype(BF16)
    mvb_ref[...] = mv.astype(BF16)


def _memory_kv(mem2d, g_mem, w_xk, w_xv):
    n, d = mem2d.shape
    tm = 512
    row = pl.BlockSpec((tm, d), lambda i: (i, 0))
    rows_out = pl.BlockSpec((tm * d // LANES, LANES), lambda i: (i, 0))
    const = lambda shape: pl.BlockSpec(shape, lambda i: (0,) * len(shape))
    return pl.pallas_call(
        _memkv_kernel,
        grid=(n // tm,),
        in_specs=[row, const((1, d)), const((d, d)), const((d, d))],
        out_specs=[rows_out, rows_out, row, row],
        out_shape=[jax.ShapeDtypeStruct((n * d // LANES, LANES), F32),
                   jax.ShapeDtypeStruct((n * d // LANES, LANES), F32),
                   jax.ShapeDtypeStruct((n, d), BF16), jax.ShapeDtypeStruct((n, d), BF16)],
        compiler_params=pltpu.CompilerParams(
            dimension_semantics=("arbitrary",), vmem_limit_bytes=VMEM_LIMIT),
        name="memory_kv",
    )(mem2d, g_mem, w_xk, w_xv)


def _prompt_xattn_kernel(h_ref, g_ref, wq_ref, wo_ref, mk_ref, mv_ref, out_ref, o_s):
    for r0 in range(0, h_ref.shape[1], SUB_ROWS):
        rows = slice(r0, r0 + SUB_ROWS)
        h = h_ref[0, rows, :]
        xn = _rms(h, g_ref[...]).astype(BF16)
        q = (_dot(xn, wq_ref[...]) * (X_HEAD_DIM ** -0.5)).astype(BF16)
        for hd in range(N_X_HEADS):
            sl = slice(hd * X_HEAD_DIM, (hd + 1) * X_HEAD_DIM)
            s = _dot_nt(q[:, sl], mk_ref[0, :, sl])
            m = jnp.max(s, axis=-1, keepdims=True)
            p = jnp.exp(s - m)
            p = p * (1.0 / jnp.sum(p, axis=-1, keepdims=True))
            o_s[rows, sl] = _dot(p.astype(BF16), mv_ref[0, :, sl]).astype(BF16)
        out_ref[0, rows, :] = h + _dot(o_s[rows, :], wo_ref[...])


def _prompt_xattn(h, g, w_xq, w_xo, mkb, mvb):
    b, s, d = h.shape
    tm = TM_X
    const = lambda shape: pl.BlockSpec(shape, lambda i, j: (0,) * len(shape))
    tok = pl.BlockSpec((1, tm, d), lambda i, j: (i, j, 0))
    mem = pl.BlockSpec((1, N_MEM, d), lambda i, j: (i, 0, 0))
    return pl.pallas_call(
        _prompt_xattn_kernel,
        grid=(b, s // tm),
        in_specs=[tok, const((1, d)), const((d, d)), const((d, d)), mem, mem],
        out_specs=tok,
        out_shape=jax.ShapeDtypeStruct((b, s, d), F32),
        scratch_shapes=[pltpu.VMEM((tm, d), BF16)],
        compiler_params=pltpu.CompilerParams(
            dimension_semantics=("arbitrary", "arbitrary"), vmem_limit_bytes=VMEM_LIMIT),
        name="prompt_xattn",
    )(h, g, w_xq, w_xo, mkb, mvb)


def _mlp_kernel(h_ref, g_ref, wup_ref, wdn_ref, gf_ref, y_ref):
    for r0 in range(0, h_ref.shape[0], SUB_ROWS):
        rows = slice(r0, r0 + SUB_ROWS)
        h = h_ref[rows, :]
        xn = _rms(h, g_ref[...]).astype(BF16)
        acc = h
        for c in range(D_FF // FF_CHUNK):
            sl = slice(c * FF_CHUNK, (c + 1) * FF_CHUNK)
            u = jnp.maximum(_dot(xn, wup_ref[:, sl]), 0.0)
            acc = acc + _dot((u * u).astype(BF16), wdn_ref[sl, :])
        y_ref[rows, :] = _rms(acc, gf_ref[...])


def _mlp_final(h2d, g_mlp, w_up, w_down, g_final):
    n, d = h2d.shape
    tm = min(TM_MLP, n)
    row = pl.BlockSpec((tm, d), lambda i: (i, 0))
    const = lambda shape: pl.BlockSpec(shape, lambda i: (0,) * len(shape), pipeline_mode=pl.Buffered(1))
    return pl.pallas_call(
        _mlp_kernel,
        grid=(n // tm,),
        in_specs=[row, const((1, d)), const((d, D_FF)), const((D_FF, d)), const((1, d))],
        out_specs=row,
        out_shape=jax.ShapeDtypeStruct((n, d), F32),
        compiler_params=pltpu.CompilerParams(
            dimension_semantics=("arbitrary",), vmem_limit_bytes=VMEM_LIMIT),
        name="mlp_final",
    )(h2d, g_mlp, w_up, w_down, g_final)


def _sample_mixer_kernel(sinks_ref, x_ref, gmix_ref, win_ref, wout_ref, gng_ref, gnb_ref,
                         ck_ref, cv_ref, st_ref, bias_ref, dec_ref, xi_ref, zeta_ref,
                         h_ref, swk_ref, swv_ref, sst_ref):
    bb = ck_ref.shape[0]
    nt = bb // 2
    x = x_ref[...]
    xn = _rms(x, gmix_ref[...]).astype(BF16)
    tile3 = lambda a: a.reshape(nt, SUBLANES, a.shape[-1])

    q = _dot(xn, win_ref[:, C_QA:C_QA + ATT_Q_W]) * (HEAD_DIM ** -0.5)
    kv = _dot(xn, win_ref[:, C_KV:C_KV + 2 * ATT_KV_W])
    qkr = _dot(xn, win_ref[:, C_QKR:C_QKR + 2 * RET_QK_W])
    vr = _dot(xn, win_ref[:, C_VR:C_VR + RET_V_W])
    gate3 = tile3(_silu(_dot(xn, win_ref[:, C_GR:C_GR + RET_V_W])))

    lo512, hi512 = _half_masks(ATT_Q_W)
    q_r = pltpu.roll(q, HALF, axis=1)
    q_nat3 = tile3(q)
    q_rot3 = tile3(q_r)
    lo3 = lo512.reshape(1, 1, ATT_Q_W)
    hi3 = hi512.reshape(1, 1, ATT_Q_W)
    qa3 = (q_nat3 * lo3).astype(BF16)
    qb3 = (q_rot3 * lo3).astype(BF16)
    qc3 = (q_rot3 * hi3).astype(BF16)
    qd3 = (q_nat3 * hi3).astype(BF16)
    t128 = lambda a, i: a[:, :, i * LANES:(i + 1) * LANES]
    qs = jnp.concatenate([t128(qa3, 0), t128(qb3, 1), t128(qa3, 1), t128(qb3, 2),
                          t128(qc3, 2), t128(qd3, 2), t128(qc3, 3), t128(qd3, 3)], axis=1)

    k3 = tile3(kv[:, :ATT_KV_W])
    v3 = tile3(kv[:, ATT_KV_W:])
    pad_kv = jnp.zeros((nt, BLK - SUBLANES, LANES), BF16)

    lo256, _ = _half_masks(RET_QK_W)
    qr3 = tile3(qkr[:, :RET_QK_W])
    kr3 = tile3(qkr[:, RET_QK_W:] * (RET_QK_DIM ** -0.5))
    vr3 = tile3(vr)
    lane256 = lax.broadcasted_iota(jnp.int32, (1, 1, RET_QK_W), 2)
    qrs = jnp.concatenate(
        [(qr3 * ((lane256 >= h * RET_QK_DIM) & (lane256 < (h + 1) * RET_QK_DIM)).astype(F32)).astype(BF16)
         for h in range(N_RET_HEADS)],
        axis=1)
    kr_pad = jnp.concatenate([kr3.astype(BF16), jnp.zeros((nt, BLK - SUBLANES, RET_QK_W), BF16)], axis=1)
    vr_pad = jnp.concatenate([vr3.astype(BF16), jnp.zeros((nt, BLK - SUBLANES, RET_V_W), BF16)], axis=1)

    lane = lax.broadcasted_iota(jnp.int32, (1, 1, LANES), 2)
    row8 = lax.broadcasted_iota(jnp.int32, (1, SUBLANES, 1), 1)
    bmm_nt = lambda a, b: jnp.einsum('bqd,bkd->bqk', a, b, preferred_element_type=F32)
    bmm = lambda a, b: jnp.einsum('bqk,bkd->bqd', a, b, preferred_element_type=F32)

    att_par, ret_par = [], []
    for par in range(2):
        bsl = pl.ds(par, nt, stride=2)
        ck = ck_ref[bsl]
        cv = cv_ref[bsl]
        swk_ref[bsl, 0:WINDOW - DEC_SEQ, :] = ck[:, DEC_SEQ:, :]
        swv_ref[bsl, 0:WINDOW - DEC_SEQ, :] = cv[:, DEC_SEQ:, :]
        swk_ref[bsl, WINDOW - DEC_SEQ:WINDOW, :] = k3[:, DEC_SEQ * par:DEC_SEQ * (par + 1), :]
        swv_ref[bsl, WINDOW - DEC_SEQ:WINDOW, :] = v3[:, DEC_SEQ * par:DEC_SEQ * (par + 1), :]

        kfull = jnp.concatenate([ck.astype(BF16), k3.astype(BF16), pad_kv], axis=1)
        vfull = jnp.concatenate([cv.astype(BF16), v3.astype(BF16), pad_kv], axis=1)
        s = bmm_nt(qs, kfull) + bias_ref[par]
        ps = []
        for h in range(N_ATT_HEADS):
            ps.append(_sink_softmax(s[:, h * SUBLANES:(h + 1) * SUBLANES, :], sinks_ref[h]).astype(BF16))
        o = bmm(jnp.concatenate(ps, axis=1), vfull)
        o_r = pltpu.roll(o.reshape(nt * N_ATT_HEADS * SUBLANES, LANES), HALF, axis=1).reshape(o.shape)
        hr = lambda a, h: a[:, h * SUBLANES:(h + 1) * SUBLANES, :]
        low = lane < HALF
        att_par.append(jnp.concatenate([
            jnp.where(low, hr(o, 0), hr(o_r, 1)), jnp.where(low, hr(o, 2), hr(o_r, 3)),
            jnp.where(low, hr(o_r, 4), hr(o, 5)), jnp.where(low, hr(o_r, 6), hr(o, 7))], axis=2))

        st = st_ref[bsl]
        oc = bmm(qrs, st.astype(BF16))
        inner = (bmm_nt(qrs, kr_pad) * dec_ref[par]).astype(BF16)
        oi = bmm(inner, vr_pad)
        rs = []
        for h in range(N_RET_HEADS):
            vsl = slice(h * RET_V_DIM, (h + 1) * RET_V_DIM)
            rsl = slice(h * SUBLANES, (h + 1) * SUBLANES)
            o_h = oi[:, rsl, vsl] + oc[:, rsl, :] * xi_ref[par, rsl, :]
            rs.append(_group_norm(o_h, gng_ref[:, vsl], gnb_ref[:, vsl]) * gate3[:, :, vsl])
        ret_par.append(jnp.concatenate(rs, axis=2))

        kz3 = (kr3 * zeta_ref[par]).astype(BF16)
        vr3_b = vr3.astype(BF16)
        for p in range(nt):
            u = _dot_tn(kz3[p], vr3_b[p])
            for h in range(N_RET_HEADS):
                dsl = slice(h * RET_QK_DIM, (h + 1) * RET_QK_DIM)
                sst_ref[2 * p + par, dsl, :] = (
                    _GL_SAMPLE[h] * st[p, dsl, :] + u[dsl, h * RET_V_DIM:(h + 1) * RET_V_DIM])

    own0 = row8 < DEC_SEQ
    att3 = jnp.where(own0, att_par[0], att_par[1])
    ret3 = jnp.where(own0, ret_par[0], ret_par[1])
    mix = jnp.concatenate([att3, ret3], axis=2).reshape(2 * nt * DEC_SEQ, MIX_OUT).astype(BF16)
    h_ref[...] = x + _dot(mix, wout_ref[...])


def _sample_mixer(x2d, g_mix, w_in, w_out, sinks, gn_g, gn_b, ck, cv, st):
    n, d = x2d.shape
    nb = ck.shape[0]
    bb = BB_MIX
    r = bb * DEC_SEQ
    const = lambda shape: pl.BlockSpec(shape, lambda i: (0,) * len(shape))
    row = pl.BlockSpec((r, d), lambda i: (i, 0))
    win = pl.BlockSpec((bb, WINDOW, ATT_KV_W), lambda i: (i, 0, 0))
    state = pl.BlockSpec((bb, RET_QK_W, RET_V_DIM), lambda i: (i, 0, 0))
    return pl.pallas_call(
        _sample_mixer_kernel,
        grid=(nb // bb,),
        in_specs=[
            pl.BlockSpec(memory_space=pltpu.SMEM),
            row, const((1, d)), const((d, D_IN)), const((MIX_OUT, d)),
            const((1, RET_V_W)), const((1, RET_V_W)),
            win, win, state,
            const(_S_BIAS.shape), const(_S_DEC.shape), const(_S_XI.shape), const(_S_ZETA.shape),
        ],
        out_specs=[row, win, win, state],
        out_shape=[
            jax.ShapeDtypeStruct((n, d), F32),
            jax.ShapeDtypeStruct((nb, WINDOW, ATT_KV_W), F32),
            jax.ShapeDtypeStruct((nb, WINDOW, ATT_KV_W), F32),
            jax.ShapeDtypeStruct((nb, RET_QK_W, RET_V_DIM), F32),
        ],
        compiler_params=pltpu.CompilerParams(
            dimension_semantics=("arbitrary",), vmem_limit_bytes=VMEM_LIMIT),
        name="sample_mixer",
    )(sinks, x2d, g_mix, w_in, w_out, gn_g, gn_b, ck, cv, st,
      jnp.asarray(_S_BIAS), jnp.asarray(_S_DEC), jnp.asarray(_S_XI), jnp.asarray(_S_ZETA))


def _head_slab(x_ref, b, hd):
    group = X_D_HALVES * N_X_HEADS
    halves = [x_ref[b, pl.ds(dh * N_X_HEADS + hd, N_MEM, stride=group), :] for dh in range(X_D_HALVES)]
    return jnp.concatenate(halves, axis=1).astype(BF16)


def _sample_xattn_kernel(h_ref, g_ref, wq_ref, wo_ref, xk_ref, xv_ref, out_ref):
    bb = xk_ref.shape[0]
    nt = bb // 2
    h = h_ref[...]
    xn = _rms(h, g_ref[...]).astype(BF16)
    q = _dot(xn, wq_ref[...]) * (X_HEAD_DIM ** -0.5)
    units = [(t, par, hd) for t in range(nt) for par in range(2) for hd in range(N_X_HEADS)]
    qts = [q[t * SUBLANES:(t + 1) * SUBLANES].astype(BF16) for t in range(nt)]
    s = jnp.concatenate(
        [_dot_nt(qts[t][:, hd * X_HEAD_DIM:(hd + 1) * X_HEAD_DIM], _head_slab(xk_ref, 2 * t + par, hd))
         for t, par, hd in units], axis=0)
    m = jnp.max(s, axis=-1, keepdims=True)
    p = jnp.exp(s - m)
    p = p * (1.0 / jnp.sum(p, axis=-1, keepdims=True))
    os_ = {}
    for i, (t, par, hd) in enumerate(units):
        pi = p[i * SUBLANES:(i + 1) * SUBLANES].astype(BF16)
        os_[(t, par, hd)] = _dot(pi, _head_slab(xv_ref, 2 * t + par, hd))
    own0 = lax.broadcasted_iota(jnp.int32, (SUBLANES, 1), 0) < DEC_SEQ
    o_tiles = []
    for t in range(nt):
        o_par = [jnp.concatenate([os_[(t, par, hd)] for hd in range(N_X_HEADS)], axis=1) for par in range(2)]
        o_tiles.append(jnp.where(own0, o_par[0], o_par[1]))
    o = jnp.concatenate(o_tiles, axis=0).astype(BF16)
    out_ref[...] = h + _dot(o, wo_ref[...])


def _mem_rows(c):
    nb = c.shape[0]
    c = c.reshape(nb, N_MEM, N_X_HEADS, X_D_HALVES, LANES)
    return jnp.transpose(c, (0, 1, 3, 2, 4)).reshape(nb, N_MEM * X_D_HALVES * N_X_HEADS, LANES)


def _sample_xattn(h2d, g, w_xq, w_xo, mk, mv):
    n, d = h2d.shape
    nb = mk.shape[0]
    bb = BB_X
    r = bb * DEC_SEQ
    const = lambda shape: pl.BlockSpec(shape, lambda i: (0,) * len(shape))
    row = pl.BlockSpec((r, d), lambda i: (i, 0))
    mem = pl.BlockSpec((bb,) + mk.shape[1:], lambda i: (i, 0, 0))
    return pl.pallas_call(
        _sample_xattn_kernel,
        grid=(nb // bb,),
        in_specs=[row, const((1, d)), const((d, d)), const((d, d)), mem, mem],
        out_specs=row,
        out_shape=jax.ShapeDtypeStruct((n, d), F32),
        compiler_params=pltpu.CompilerParams(
            dimension_semantics=("arbitrary",), vmem_limit_bytes=VMEM_LIMIT),
        name="sample_xattn",
    )(h2d, g, w_xq, w_xo, mk, mv)


def kernel(x_prompt, x_sample, mem_prompt, cache_win_k, cache_win_v, state_ret, cache_mem_k, cache_mem_v,
           g_mix, w_in, attn_sinks, ret_gn_g, ret_gn_b, w_out, g_xattn, g_mem, w_xq, w_xk, w_xv, w_xo,
           g_mlp, w_up, w_down, g_final):
    depth = w_in.shape[0]
    assert depth == 1, "single-layer trunk"
    b, s, d = x_prompt.shape
    nb, ls, _ = x_sample.shape
    row = lambda a: a.reshape(1, -1)
    bf = lambda a: a.astype(BF16)

    w_in_b, w_out_b = bf(w_in[0]), bf(w_out[0])
    w_xq_b, w_xk_b, w_xv_b, w_xo_b = bf(w_xq[0]), bf(w_xk[0]), bf(w_xv[0]), bf(w_xo[0])
    w_up_b, w_dn_b = bf(w_up[0]), bf(w_down[0])
    sinks = attn_sinks[0]
    gn_g, gn_b = row(ret_gn_g[0]), row(ret_gn_b[0])
    g_fin = row(g_final)

    hp, p_wk, p_wv, p_rs = _prompt_mixer(x_prompt, row(g_mix[0]), w_in_b, w_out_b, sinks, gn_g, gn_b)
    mk, mv, mkb, mvb = _memory_kv(mem_prompt.reshape(b * N_MEM, d), row(g_mem[0]), w_xk_b, w_xv_b)
    hp = _prompt_xattn(hp, row(g_xattn[0]), w_xq_b, w_xo_b,
                       mkb.reshape(b, N_MEM, d), mvb.reshape(b, N_MEM, d))
    y_prompt = _mlp_final(hp.reshape(b * s, d), row(g_mlp[0]), w_up_b, w_dn_b, g_fin).reshape(b, s, d)

    hs, s_wk, s_wv, s_rs = _sample_mixer(
        x_sample.reshape(nb * ls, d), row(g_mix[0]), w_in_b, w_out_b, sinks, gn_g, gn_b,
        cache_win_k[0].reshape(nb, WINDOW, ATT_KV_W), cache_win_v[0].reshape(nb, WINDOW, ATT_KV_W),
        state_ret[0].reshape(nb, RET_QK_W, RET_V_DIM))
    hs = _sample_xattn(hs, row(g_xattn[0]), w_xq_b, w_xo_b,
                       _mem_rows(cache_mem_k[0]), _mem_rows(cache_mem_v[0]))
    y_sample = _mlp_final(hs, row(g_mlp[0]), w_up_b, w_dn_b, g_fin).reshape(nb, ls, d)

    win5 = lambda a, n: a.reshape(1, n, WINDOW, N_KV_HEADS, HEAD_DIM)
    ret5 = lambda a, n: a.reshape(1, n, N_RET_HEADS, RET_QK_DIM, RET_V_DIM)
    mem5 = lambda a: jnp.transpose(a.reshape(b, N_MEM, X_D_HALVES, N_X_HEADS, LANES),
                                   (0, 1, 3, 2, 4)).reshape(1, b, N_MEM, N_X_HEADS, X_HEAD_DIM)
    return (y_prompt, y_sample,
            win5(p_wk, b), win5(p_wv, b), ret5(p_rs, b), mem5(mk), mem5(mv),
            win5(s_wk, nb), win5(s_wv, nb), ret5(s_rs, nb))
```

```python
import functools

import jax
import jax.numpy as jnp
import numpy as np
from jax import lax
from jax.experimental import pallas as pl
from jax.experimental.pallas import tpu as pltpu

F32 = jnp.float32
BF16 = jnp.bfloat16

D_MODEL = 1024
BATCH = 8
SEQ = 2048
DEC_BATCH = 128
DEC_SEQ = 4
HEAD_DIM = 64
N_ATT_HEADS = 8
N_KV_HEADS = 2
KV_GROUP = N_ATT_HEADS // N_KV_HEADS
WINDOW = 128
BLK = 128
N_RET_HEADS = 4
RET_QK_DIM = 64
RET_V_DIM = 128
N_MEM = 256
N_X_HEADS = 4
X_HEAD_DIM = D_MODEL // N_X_HEADS
D_FF = 4 * D_MODEL
RMS_EPS = 1e-6
GN_EPS = 1e-5

ATT_Q_W = N_ATT_HEADS * HEAD_DIM
ATT_KV_W = N_KV_HEADS * HEAD_DIM
RET_QK_W = N_RET_HEADS * RET_QK_DIM
RET_V_W = N_RET_HEADS * RET_V_DIM
MIX_OUT = ATT_Q_W + RET_V_W
D_IN = ATT_Q_W + 2 * ATT_KV_W + 2 * RET_QK_W + 2 * RET_V_W
C_QA, C_KV, C_QKR, C_VR, C_GR = 0, 512, 768, 1280, 1792

LANES = 128
SUBLANES = 8
HALF = LANES // 2
X_D_HALVES = X_HEAD_DIM // LANES
NEG = -1e30
VMEM_LIMIT = 52 * 1024 * 1024

TM_MIX = 512
TM_X = 1024
TM_MLP = 1024
SUB_ROWS = 512
FF_CHUNK = 1024
BB_MIX = 16
BB_X = 8

NEG_SLOPES = [-(2.0 ** (-8.0 * (i + 1) / N_ATT_HEADS)) for i in range(N_ATT_HEADS)]
_LOG_G = np.log(1.0 - 2.0 ** (-5.0 - np.arange(N_RET_HEADS))).astype(np.float32).astype(np.float64)


def _prompt_tables():
    qi = np.arange(BLK)[:, None]
    kj = np.arange(2 * BLK)[None, :]
    dist = (qi + BLK - kj).astype(np.float64)
    mask = np.where((dist >= 0) & (dist < WINDOW), 0.0, NEG)
    l = np.arange(BLK, dtype=np.float64)
    diff = l[:, None] - l[None, :]
    decay = np.where(diff >= 0, np.exp(_LOG_G[:, None, None] * np.maximum(diff, 0.0)), 0.0)
    xi = np.exp((l[:, None] + 1.0) * _LOG_G[None, :])
    zeta = np.exp((BLK - 1.0 - l)[:, None] * _LOG_G[None, :])
    xi_t = np.repeat(xi, RET_V_DIM, axis=1)
    zeta_t = np.repeat(zeta, RET_QK_DIM, axis=1)
    f = lambda a: np.asarray(a, np.float32)
    return f(dist), f(mask), f(decay), f(xi_t), f(zeta_t)


def _sample_tables():
    slopes = -np.asarray(NEG_SLOPES)
    bias = np.full((2, N_ATT_HEADS * SUBLANES, 2 * BLK), NEG, np.float64)
    dec = np.zeros((2, N_RET_HEADS * SUBLANES, BLK), np.float64)
    xi = np.zeros((2, N_RET_HEADS * SUBLANES, RET_V_DIM), np.float64)
    zeta = np.zeros((2, SUBLANES, RET_QK_W), np.float64)
    for par in range(2):
        for r in range(SUBLANES):
            own = DEC_SEQ * par <= r < DEC_SEQ * (par + 1)
            t = r - DEC_SEQ * par if own else r % DEC_SEQ
            for h in range(N_ATT_HEADS):
                row = h * SUBLANES + r
                for j in range(WINDOW):
                    d = t + WINDOW - j
                    if 0 <= d < WINDOW:
                        bias[par, row, j] = -slopes[h] * d
                for c in range(DEC_SEQ):
                    d = t - c
                    if d >= 0:
                        bias[par, row, WINDOW + DEC_SEQ * par + c] = -slopes[h] * d
            for h in range(N_RET_HEADS):
                row = h * SUBLANES + r
                if own:
                    xi[par, row, :] = np.exp((t + 1.0) * _LOG_G[h])
                    zeta[par, r, h * RET_QK_DIM:(h + 1) * RET_QK_DIM] = np.exp((DEC_SEQ - 1.0 - t) * _LOG_G[h])
                    for c in range(t + 1):
                        dec[par, row, DEC_SEQ * par + c] = np.exp(_LOG_G[h] * (t - c))
    f = lambda a: np.asarray(a, np.float32)
    return f(bias), f(dec), f(xi), f(zeta)


_P_DIST, _P_MASK, _P_DECAY, _P_XI, _P_ZETA = _prompt_tables()
_S_BIAS, _S_DEC, _S_XI, _S_ZETA = _sample_tables()
_GL_PROMPT = [float(np.exp(_LOG_G[h] * BLK)) for h in range(N_RET_HEADS)]
_GL_SAMPLE = [float(np.exp(_LOG_G[h] * DEC_SEQ)) for h in range(N_RET_HEADS)]


def _rms(x, g):
    return x * lax.rsqrt(jnp.mean(x * x, axis=-1, keepdims=True) + RMS_EPS) * g


def _dot(a, b):
    return jnp.dot(a, b, preferred_element_type=F32)


def _dot_nt(a, b):
    return lax.dot_general(a, b, (((1,), (1,)), ((), ())), preferred_element_type=F32)


def _dot_tn(a, b):
    return lax.dot_general(a, b, (((0,), (0,)), ((), ())), preferred_element_type=F32)


def _silu(g):
    return g * (1.0 / (1.0 + jnp.exp(-g)))


def _half_masks(width):
    lane = lax.broadcasted_iota(jnp.int32, (1, width), 1)
    lo = ((lane & (LANES - 1)) < HALF).astype(F32)
    return lo, 1.0 - lo


def _sink_softmax(s, sink):
    m = jnp.maximum(jnp.max(s, axis=-1, keepdims=True), sink)
    p = jnp.exp(s - m)
    den = jnp.sum(p, axis=-1, keepdims=True) + jnp.exp(sink - m)
    return p * (1.0 / den)


def _group_norm(o, g, b):
    mu = jnp.mean(o, axis=-1, keepdims=True)
    d = o - mu
    var = jnp.mean(d * d, axis=-1, keepdims=True)
    return d * lax.rsqrt(var + GN_EPS) * g + b


def _prompt_mixer_kernel(sinks_ref, x_ref, gmix_ref, win_ref, wout_ref, gng_ref, gnb_ref,
                         dist_ref, mask_ref, decay_ref, xi_ref, zeta_ref,
                         h_ref, wk_ref, wv_ref, st_ref,
                         qlo_s, qhi_s, kd0_s, kd1_s, vd0_s, vd1_s,
                         qrlo_s, qrhi_s, kr_s, vr_s, gate_s, mix_s, bias_s):
    t = pl.program_id(1)
    nt = pl.num_programs(1)
    tm = x_ref.shape[1]
    nblk = tm // BLK

    @pl.when(t == 0)
    def _():
        kd0_s[0:BLK, :] = jnp.zeros((BLK, LANES), BF16)
        kd1_s[0:BLK, :] = jnp.zeros((BLK, LANES), BF16)
        vd0_s[0:BLK, :] = jnp.zeros((BLK, 2 * LANES), BF16)
        vd1_s[0:BLK, :] = jnp.zeros((BLK, 2 * LANES), BF16)
        st_ref[...] = jnp.zeros_like(st_ref)

    @pl.when(t > 0)
    def _():
        kd0_s[0:BLK, :] = kd0_s[tm:tm + BLK, :]
        kd1_s[0:BLK, :] = kd1_s[tm:tm + BLK, :]
        vd0_s[0:BLK, :] = vd0_s[tm:tm + BLK, :]
        vd1_s[0:BLK, :] = vd1_s[tm:tm + BLK, :]

    @pl.when((t == 0) & (pl.program_id(0) == 0))
    def _():
        for h in range(N_ATT_HEADS):
            bias_s[h] = NEG_SLOPES[h] * dist_ref[...] + mask_ref[...]

    x = x_ref[0]
    xn = _rms(x, gmix_ref[...]).astype(BF16)

    z = _dot(xn, win_ref[...])
    lo512, hi512 = _half_masks(ATT_Q_W)
    q = z[:, C_QA:C_QA + ATT_Q_W]
    qlo_s[...] = (q * (lo512 * HEAD_DIM ** -0.5)).astype(BF16)
    qhi_s[...] = (q * (hi512 * HEAD_DIM ** -0.5)).astype(BF16)

    low = lax.broadcasted_iota(jnp.int32, (tm, LANES), 1) < HALF
    kv = z[:, C_KV:C_KV + 2 * ATT_KV_W]
    k = kv[:, :ATT_KV_W]
    v = kv[:, ATT_KV_W:]
    k_r = pltpu.roll(k, HALF, axis=1)
    v_r = pltpu.roll(v, HALF, axis=1)
    kd0_s[BLK:BLK + tm, :] = jnp.where(low, k, k_r).astype(BF16)
    kd1_s[BLK:BLK + tm, :] = jnp.where(low, k_r, k).astype(BF16)
    vd0_s[BLK:BLK + tm, 0:LANES] = jnp.where(low, v, 1.0).astype(BF16)
    vd0_s[BLK:BLK + tm, LANES:2 * LANES] = jnp.where(low, 1.0, v_r).astype(BF16)
    vd1_s[BLK:BLK + tm, 0:LANES] = jnp.where(low, v_r, 1.0).astype(BF16)
    vd1_s[BLK:BLK + tm, LANES:2 * LANES] = jnp.where(low, 1.0, v).astype(BF16)

    @pl.when(t == nt - 1)
    def _():
        wk_ref[0] = k[tm - WINDOW:, :]
        wv_ref[0] = v[tm - WINDOW:, :]

    lo256, hi256 = _half_masks(RET_QK_W)
    qkr = z[:, C_QKR:C_QKR + 2 * RET_QK_W]
    qr = qkr[:, :RET_QK_W]
    qrlo_s[...] = (qr * lo256).astype(BF16)
    qrhi_s[...] = (qr * hi256).astype(BF16)
    kr_s[...] = qkr[:, RET_QK_W:] * (RET_QK_DIM ** -0.5)
    vr_s[...] = z[:, C_VR:C_VR + RET_V_W].astype(BF16)
    gate_s[...] = _silu(z[:, C_GR:C_GR + RET_V_W])

    lowb =lax.broadcasted_iota(jnp.int32, (BLK, LANES), 1) < HALF
    col = lax.broadcasted_iota(jnp.int32, (BLK, 2 * BLK), 1)
    first_mask = jnp.where((col < BLK) & (t == 0), NEG, 0.0)
    kd_refs = (kd0_s, kd1_s)
    vd_refs = (vd0_s, vd1_s)
    n_pairs = N_RET_HEADS // 2
    state = [st_ref[0, i * LANES:(i + 1) * LANES, :] for i in range(n_pairs)]

    for j in range(nblk):
        rows = slice(j * BLK, (j + 1) * BLK)
        krows = slice(j * BLK, (j + 2) * BLK)

        for kvh in range(N_KV_HEADS):
            kd = kd_refs[kvh][krows, :]
            vd = vd_refs[kvh][krows, :]
            c0 = kvh * KV_GROUP * HEAD_DIM
            qst = jnp.concatenate([qlo_s[rows, c0:c0 + LANES], qhi_s[rows, c0:c0 + LANES],
                                   qlo_s[rows, c0 + LANES:c0 + 2 * LANES],
                                   qhi_s[rows, c0 + LANES:c0 + 2 * LANES]], axis=0)
            s = _dot_nt(qst, kd)
            es, esink = [], []
            for g in range(KV_GROUP):
                h = kvh * KV_GROUP + g
                sg = s[g * BLK:(g + 1) * BLK] + bias_s[h]
                if j == 0:
                    sg = sg + first_mask
                sink = sinks_ref[h]
                m = jnp.maximum(jnp.max(sg, axis=-1, keepdims=True), sink)
                es.append(jnp.exp(sg - m).astype(BF16))
                esink.append(jnp.exp(sink - m))
            o = _dot(jnp.concatenate(es, axis=0), vd)
            for pair in range(KV_GROUP // 2):
                oe = o[2 * pair * BLK:(2 * pair + 1) * BLK]
                oo = o[(2 * pair + 1) * BLK:(2 * pair + 2) * BLK]
                num = jnp.where(lowb, oe[:, :LANES], oo[:, LANES:])
                den = (jnp.where(lowb, oe[:, LANES:], oo[:, :LANES])
                       + jnp.where(lowb, esink[2 * pair], esink[2 * pair + 1]))
                cs = c0 + pair * LANES
                mix_s[rows, cs:cs + LANES] = (num * (1.0 / den)).astype(BF16)

        for i in range(n_pairs):
            lsl = slice(i * LANES, (i + 1) * LANES)
            kp = kr_s[rows, lsl]
            sp = state[i]
            vpair = vr_s[rows, 2 * i * RET_V_DIM:(2 * i + 2) * RET_V_DIM]
            q2 = jnp.concatenate([qrlo_s[rows, lsl], qrhi_s[rows, lsl]], axis=0)
            a = _dot_nt(q2, kp.astype(BF16))
            inner = jnp.concatenate([a[:BLK] * decay_ref[2 * i], a[BLK:] * decay_ref[2 * i + 1]], axis=0)
            oi = _dot(inner.astype(BF16), vpair)
            oc = _dot(q2, sp.astype(BF16))
            for half in range(2):
                h = 2 * i + half
                vsl = slice(h * RET_V_DIM, (h + 1) * RET_V_DIM)
                hr = slice(half * BLK, (half + 1) * BLK)
                o = oi[hr, half * RET_V_DIM:(half + 1) * RET_V_DIM] + oc[hr] * xi_ref[:, vsl]
                r = _group_norm(o, gng_ref[:, vsl], gnb_ref[:, vsl]) * gate_s[rows, vsl]
                mix_s[rows, ATT_Q_W + h * RET_V_DIM:ATT_Q_W + (h + 1) * RET_V_DIM] = r.astype(BF16)
            kz = (kp * zeta_ref[:, lsl]).astype(BF16)
            u = _dot_tn(kz, vpair)
            state[i] = jnp.concatenate(
                [_GL_PROMPT[2 * i] * sp[:RET_QK_DIM] + u[:RET_QK_DIM, :RET_V_DIM],
                 _GL_PROMPT[2 * i + 1] * sp[RET_QK_DIM:] + u[RET_QK_DIM:, RET_V_DIM:]], axis=0)

    for i in range(n_pairs):
        st_ref[0, i * LANES:(i + 1) * LANES, :] = state[i]

    h_ref[0] = x + _dot(mix_s[...], wout_ref[...])


def _prompt_mixer(x, g_mix, w_in, w_out, sinks, gn_g, gn_b):
    b, s, d = x.shape
    tm = TM_MIX
    const = lambda shape: pl.BlockSpec(shape, lambda i, j: (0,) * len(shape))
    return pl.pallas_call(
        _prompt_mixer_kernel,
        grid=(b, s // tm),
        in_specs=[
            pl.BlockSpec(memory_space=pltpu.SMEM),
            pl.BlockSpec((1, tm, d), lambda i, j: (i, j, 0)),
            const((1, d)), const((d, D_IN)), const((MIX_OUT, d)),
            const((1, RET_V_W)), const((1, RET_V_W)),
            const((BLK, 2 * BLK)), const((BLK, 2 * BLK)),
            const((N_RET_HEADS, BLK, BLK)), const((BLK, RET_V_W)), const((BLK, RET_QK_W)),
        ],
        out_specs=[
            pl.BlockSpec((1, tm, d), lambda i, j: (i, j, 0)),
            pl.BlockSpec((1, WINDOW, ATT_KV_W), lambda i, j: (i, 0, 0)),
            pl.BlockSpec((1, WINDOW, ATT_KV_W), lambda i, j: (i, 0, 0)),
            pl.BlockSpec((1, RET_QK_W, RET_V_DIM), lambda i, j: (i, 0, 0)),
        ],
        out_shape=[
            jax.ShapeDtypeStruct((b, s, d), F32),
            jax.ShapeDtypeStruct((b, WINDOW, ATT_KV_W), F32),
            jax.ShapeDtypeStruct((b, WINDOW, ATT_KV_W), F32),
            jax.ShapeDtypeStruct((b, RET_QK_W, RET_V_DIM), F32),
        ],
        scratch_shapes=[
            pltpu.VMEM((tm, ATT_Q_W), BF16), pltpu.VMEM((tm, ATT_Q_W), BF16),
            pltpu.VMEM((tm + BLK, LANES), BF16), pltpu.VMEM((tm + BLK, LANES), BF16),
            pltpu.VMEM((tm + BLK, 2 * LANES), BF16), pltpu.VMEM((tm + BLK, 2 * LANES), BF16),
            pltpu.VMEM((tm, RET_QK_W), BF16), pltpu.VMEM((tm, RET_QK_W), BF16),
            pltpu.VMEM((tm, RET_QK_W), F32), pltpu.VMEM((tm, RET_V_W), BF16),
            pltpu.VMEM((tm, RET_V_W), F32), pltpu.VMEM((tm, MIX_OUT), BF16),
            pltpu.VMEM((N_ATT_HEADS, BLK, 2 * BLK), F32),
        ],
        compiler_params=pltpu.CompilerParams(
            dimension_semantics=("arbitrary", "arbitrary"), vmem_limit_bytes=VMEM_LIMIT),
        name="prompt_mixer",
    )(sinks, x, g_mix, w_in, w_out, gn_g, gn_b,
      jnp.asarray(_P_DIST), jnp.asarray(_P_MASK), jnp.asarray(_P_DECAY), jnp.asarray(_P_XI),
      jnp.asarray(_P_ZETA))


def _memkv_kernel(mem_ref, g_ref, wk_ref, wv_ref, mk_ref, mv_ref, mkb_ref, mvb_ref):
    mn = _rms(mem_ref[...], g_ref[...]).astype(BF16)
    mk = _dot(mn, wk_ref[...])
    mv = _dot(mn, wv_ref[...])
    tm = mem_ref.shape[0]
    group = X_D_HALVES * N_X_HEADS
    for hd in range(N_X_HEADS):
        for dh in range(X_D_HALVES):
            cols = slice(hd * X_HEAD_DIM + dh * LANES, hd * X_HEAD_DIM + (dh + 1) * LANES)
            rows = pl.ds(dh * N_X_HEADS + hd, tm, stride=group)
            mk_ref[rows, :] = mk[:, cols]
            mv_ref[rows, :] = mv[:, cols]
    mkb_ref[...] = mk.astype(BF16)
    mvb_ref[...] = mv.astype(BF16)


def _memory_kv(mem2d, g_mem, w_xk, w_xv):
    n, d = mem2d.shape
    tm = 512
    row = pl.BlockSpec((tm, d), lambda i: (i, 0))
    rows_out = pl.BlockSpec((tm * d // LANES, LANES), lambda i: (i, 0))
    const = lambda shape: pl.BlockSpec(shape, lambda i: (0,) * len(shape))
    return pl.pallas_call(
        _memkv_kernel,
        grid=(n // tm,),
        in_specs=[row, const((1, d)), const((d, d)), const((d, d))],
        out_specs=[rows_out, rows_out, row, row],
        out_shape=[jax.ShapeDtypeStruct((n * d // LANES, LANES), F32),
                   jax.ShapeDtypeStruct((n * d // LANES, LANES), F32),
                   jax.ShapeDtypeStruct((n, d), BF16), jax.ShapeDtypeStruct((n, d), BF16)],
        compiler_params=pltpu.CompilerParams(
            dimension_semantics=("arbitrary",), vmem_limit_bytes=VMEM_LIMIT),
        name="memory_kv",
    )(mem2d, g_mem, w_xk, w_xv)


def _prompt_xattn_kernel(h_ref, g_ref, wq_ref, wo_ref, mk_ref, mv_ref, out_ref, o_s):
    for r0 in range(0, h_ref.shape[1], SUB_ROWS):
        rows = slice(r0, r0 + SUB_ROWS)
        h = h_ref[0, rows, :]
        xn = _rms(h, g_ref[...]).astype(BF16)
        q = (_dot(xn, wq_ref[...]) * (X_HEAD_DIM ** -0.5)).astype(BF16)
        for hd in range(N_X_HEADS):
            sl = slice(hd * X_HEAD_DIM, (hd + 1) * X_HEAD_DIM)
            s = _dot_nt(q[:, sl], mk_ref[0, :, sl])
            m = jnp.max(s, axis=-1, keepdims=True)
            p = jnp.exp(s - m)
            p = p * (1.0 / jnp.sum(p, axis=-1, keepdims=True))
            o_s[rows, sl] = _dot(p.astype(BF16), mv_ref[0, :, sl]).astype(BF16)
        out_ref[0, rows, :] = h + _dot(o_s[rows, :], wo_ref[...])


def _prompt_xattn(h, g, w_xq, w_xo, mkb, mvb):
    b, s, d = h.shape
    tm = TM_X
    const = lambda shape: pl.BlockSpec(shape, lambda i, j: (0,) * len(shape))
    tok = pl.BlockSpec((1, tm, d), lambda i, j: (i, j, 0))
    mem = pl.BlockSpec((1, N_MEM, d), lambda i, j: (i, 0, 0))
    return pl.pallas_call(
        _prompt_xattn_kernel,
        grid=(b, s // tm),
        in_specs=[tok, const((1, d)), const((d, d)), const((d, d)), mem, mem],
        out_specs=tok,
        out_shape=jax.ShapeDtypeStruct((b, s, d), F32),
        scratch_shapes=[pltpu.VMEM((tm, d), BF16)],
        compiler_params=pltpu.CompilerParams(
            dimension_semantics=("arbitrary", "arbitrary"), vmem_limit_bytes=VMEM_LIMIT),
        name="prompt_xattn",
    )(h, g, w_xq, w_xo, mkb, mvb)


def _mlp_rows(h_ref, y_ref, g_ref, wup_ref, wdn_ref, gf_ref):
    for r0 in range(0, h_ref.shape[0], SUB_ROWS):
        rows = slice(r0, r0 + SUB_ROWS)
        h = h_ref[rows, :]
        xn = _rms(h, g_ref[...]).astype(BF16)
        acc = h
        for c in range(D_FF // FF_CHUNK):
            sl = slice(c * FF_CHUNK, (c + 1) * FF_CHUNK)
            u = jnp.maximum(_dot(xn, wup_ref[:, sl]), 0.0)
            acc = acc + _dot((u * u).astype(BF16), wdn_ref[sl, :])
        y_ref[rows, :] = _rms(acc, gf_ref[...])


def _mlp_kernel(hp_ref, hs_ref, g_ref, wup_ref, wdn_ref, gf_ref, yp_ref, ys_ref):
    last = pl.num_programs(0) - 1

    @pl.when(pl.program_id(0) < last)
    def _():
        _mlp_rows(hp_ref, yp_ref, g_ref, wup_ref, wdn_ref, gf_ref)

    @pl.when(pl.program_id(0) == last)
    def _():
        _mlp_rows(hs_ref, ys_ref, g_ref, wup_ref, wdn_ref, gf_ref)


def _mlp_final(hp2d, hs2d, g_mlp, w_up, w_down, g_final):
    n, d = hp2d.shape
    ns = hs2d.shape[0]
    tm = TM_MLP
    n_tiles = n // tm
    prompt = pl.BlockSpec((tm, d), lambda i: (jnp.minimum(i, n_tiles - 1), 0))
    sample = pl.BlockSpec((ns, d), lambda i: (0, 0))
    const = lambda shape: pl.BlockSpec(shape, lambda i: (0,) * len(shape), pipeline_mode=pl.Buffered(1))
    return pl.pallas_call(
        _mlp_kernel,
        grid=(n_tiles + 1,),
        in_specs=[prompt, sample, const((1, d)), const((d, D_FF)), const((D_FF, d)), const((1, d))],
        out_specs=[prompt, sample],
        out_shape=[jax.ShapeDtypeStruct((n, d), F32), jax.ShapeDtypeStruct((ns, d), F32)],
        compiler_params=pltpu.CompilerParams(
            dimension_semantics=("arbitrary",), vmem_limit_bytes=VMEM_LIMIT),
        name="mlp_final",
    )(hp2d, hs2d, g_mlp, w_up, w_down, g_final)


def _sample_mixer_kernel(sinks_ref, x_ref, gmix_ref, win_ref, wout_ref, gng_ref, gnb_ref,
                         ck_ref, cv_ref, st_ref, bias_ref, dec_ref, xi_ref, zeta_ref,
                         h_ref, swk_ref, swv_ref, sst_ref):
    bb = ck_ref.shape[0]
    nt = bb // 2
    x = x_ref[...]
    xn = _rms(x, gmix_ref[...]).astype(BF16)
    tile3 = lambda a: a.reshape(nt, SUBLANES, a.shape[-1])

    q = _dot(xn, win_ref[:, C_QA:C_QA + ATT_Q_W]) * (HEAD_DIM ** -0.5)
    kv = _dot(xn, win_ref[:, C_KV:C_KV + 2 * ATT_KV_W])
    qkr = _dot(xn, win_ref[:, C_QKR:C_QKR + 2 * RET_QK_W])
    vr = _dot(xn, win_ref[:, C_VR:C_VR + RET_V_W])
    gate3 = tile3(_silu(_dot(xn, win_ref[:, C_GR:C_GR + RET_V_W])))

    lo512, hi512 = _half_masks(ATT_Q_W)
    q_r = pltpu.roll(q, HALF, axis=1)
    q_nat3 = tile3(q)
    q_rot3 = tile3(q_r)
    lo3 = lo512.reshape(1, 1, ATT_Q_W)
    hi3 = hi512.reshape(1, 1, ATT_Q_W)
    qa3 = (q_nat3 * lo3).astype(BF16)
    qb3 = (q_rot3 * lo3).astype(BF16)
    qc3 = (q_rot3 * hi3).astype(BF16)
    qd3 = (q_nat3 * hi3).astype(BF16)
    t128 = lambda a, i: a[:, :, i * LANES:(i + 1) * LANES]
    qs = jnp.concatenate([t128(qa3, 0), t128(qb3, 1), t128(qa3, 1), t128(qb3, 2),
                          t128(qc3, 2), t128(qd3, 2), t128(qc3, 3), t128(qd3, 3)], axis=1)

    k3 = tile3(kv[:, :ATT_KV_W])
    v3 = tile3(kv[:, ATT_KV_W:])
    pad_kv = jnp.zeros((nt, BLK - SUBLANES, LANES), BF16)

    lo256, _ = _half_masks(RET_QK_W)
    qr3 = tile3(qkr[:, :RET_QK_W])
    kr3 = tile3(qkr[:, RET_QK_W:] * (RET_QK_DIM ** -0.5))
    vr3 = tile3(vr)
    lane256 = lax.broadcasted_iota(jnp.int32, (1, 1, RET_QK_W), 2)
    qrs = jnp.concatenate(
        [(qr3 * ((lane256 >= h * RET_QK_DIM) & (lane256 < (h + 1) * RET_QK_DIM)).astype(F32)).astype(BF16)
         for h in range(N_RET_HEADS)],
        axis=1)
    kr_pad = jnp.concatenate([kr3.astype(BF16), jnp.zeros((nt, BLK - SUBLANES, RET_QK_W), BF16)], axis=1)
    vr_pad = jnp.concatenate([vr3.astype(BF16), jnp.zeros((nt, BLK - SUBLANES, RET_V_W), BF16)], axis=1)

    lane = lax.broadcasted_iota(jnp.int32, (1, 1, LANES), 2)
    row8 = lax.broadcasted_iota(jnp.int32, (1, SUBLANES, 1), 1)
    bmm_nt = lambda a, b: jnp.einsum('bqd,bkd->bqk', a, b, preferred_element_type=F32)
    bmm = lambda a, b: jnp.einsum('bqk,bkd->bqd', a, b, preferred_element_type=F32)

    att_par, ret_par = [], []
    for par in range(2):
        bsl = pl.ds(par, nt, stride=2)
        ck = ck_ref[bsl]
        cv = cv_ref[bsl]
        swk_ref[bsl, 0:WINDOW - DEC_SEQ, :] = ck[:, DEC_SEQ:, :]
        swv_ref[bsl, 0:WINDOW - DEC_SEQ, :] = cv[:, DEC_SEQ:, :]
        swk_ref[bsl, WINDOW - DEC_SEQ:WINDOW, :] = k3[:, DEC_SEQ * par:DEC_SEQ * (par + 1), :]
        swv_ref[bsl, WINDOW - DEC_SEQ:WINDOW, :] = v3[:, DEC_SEQ * par:DEC_SEQ * (par + 1), :]

        kfull = jnp.concatenate([ck.astype(BF16), k3.astype(BF16), pad_kv], axis=1)
        vfull = jnp.concatenate([cv.astype(BF16), v3.astype(BF16), pad_kv], axis=1)
        s = bmm_nt(qs, kfull) + bias_ref[par]
        ps = []
        for h in range(N_ATT_HEADS):
            ps.append(_sink_softmax(s[:, h * SUBLANES:(h + 1) * SUBLANES, :], sinks_ref[h]).astype(BF16))
        o = bmm(jnp.concatenate(ps, axis=1), vfull)
        o_r = pltpu.roll(o.reshape(nt * N_ATT_HEADS * SUBLANES, LANES), HALF, axis=1).reshape(o.shape)
        hr = lambda a, h: a[:, h * SUBLANES:(h + 1) * SUBLANES, :]
        low = lane < HALF
        att_par.append(jnp.concatenate([
            jnp.where(low, hr(o, 0), hr(o_r, 1)), jnp.where(low, hr(o, 2), hr(o_r, 3)),
            jnp.where(low, hr(o_r, 4), hr(o, 5)), jnp.where(low, hr(o_r, 6), hr(o, 7))], axis=2))

        st = st_ref[bsl]
        oc = bmm(qrs, st.astype(BF16))
        inner = (bmm_nt(qrs, kr_pad) * dec_ref[par]).astype(BF16)
        oi = bmm(inner, vr_pad)
        rs = []
        for h in range(N_RET_HEADS):
            vsl = slice(h * RET_V_DIM, (h + 1) * RET_V_DIM)
            rsl = slice(h * SUBLANES, (h + 1) * SUBLANES)
            o_h = oi[:, rsl, vsl] + oc[:, rsl, :] * xi_ref[par, rsl, :]
            rs.append(_group_norm(o_h, gng_ref[:, vsl], gnb_ref[:, vsl]) * gate3[:, :, vsl])
        ret_par.append(jnp.concatenate(rs, axis=2))

        kz3 = (kr3 * zeta_ref[par]).astype(BF16)
        vr3_b = vr3.astype(BF16)
        for p in range(nt):
            for i in range(N_RET_HEADS // 2):
                u = _dot_tn(kz3[p][:, i * LANES:(i + 1) * LANES],
                            vr3_b[p][:, 2 * i * RET_V_DIM:(2 * i + 2) * RET_V_DIM])
                for half in range(2):
                    h = 2 * i + half
                    dsl = slice(h * RET_QK_DIM, (h + 1) * RET_QK_DIM)
                    sst_ref[2 * p + par, dsl, :] = (
                        _GL_SAMPLE[h] * st[p, dsl, :]
                        + u[half * RET_QK_DIM:(half + 1) * RET_QK_DIM, half * RET_V_DIM:(half + 1) * RET_V_DIM])

    own0 = row8 < DEC_SEQ
    att3 = jnp.where(own0, att_par[0], att_par[1])
    ret3 = jnp.where(own0, ret_par[0], ret_par[1])
    mix = jnp.concatenate([att3, ret3], axis=2).reshape(2 * nt * DEC_SEQ, MIX_OUT).astype(BF16)
    h_ref[...] = x + _dot(mix, wout_ref[...])


def _sample_mixer(x2d, g_mix, w_in, w_out, sinks, gn_g, gn_b, ck, cv, st):
    n, d = x2d.shape
    nb = ck.shape[0]
    bb = BB_MIX
    r = bb * DEC_SEQ
    const = lambda shape: pl.BlockSpec(shape, lambda i: (0,) * len(shape))
    row = pl.BlockSpec((r, d), lambda i: (i, 0))
    win = pl.BlockSpec((bb, WINDOW, ATT_KV_W), lambda i: (i, 0, 0))
    state = pl.BlockSpec((bb, RET_QK_W, RET_V_DIM), lambda i: (i, 0, 0))
    return pl.pallas_call(
        _sample_mixer_kernel,
        grid=(nb // bb,),
        in_specs=[
            pl.BlockSpec(memory_space=pltpu.SMEM),
            row, const((1, d)), const((d, D_IN)), const((MIX_OUT, d)),
            const((1, RET_V_W)), const((1, RET_V_W)),
            win, win, state,
            const(_S_BIAS.shape), const(_S_DEC.shape), const(_S_XI.shape), const(_S_ZETA.shape),
        ],
        out_specs=[row, win, win, state],
        out_shape=[
            jax.ShapeDtypeStruct((n, d), F32),
            jax.ShapeDtypeStruct((nb, WINDOW, ATT_KV_W), F32),
            jax.ShapeDtypeStruct((nb, WINDOW, ATT_KV_W), F32),
            jax.ShapeDtypeStruct((nb, RET_QK_W, RET_V_DIM), F32),
        ],
        compiler_params=pltpu.CompilerParams(
            dimension_semantics=("arbitrary",), vmem_limit_bytes=VMEM_LIMIT),
        name="sample_mixer",
    )(sinks, x2d, g_mix, w_in, w_out, gn_g, gn_b, ck, cv, st,
      jnp.asarray(_S_BIAS), jnp.asarray(_S_DEC), jnp.asarray(_S_XI), jnp.asarray(_S_ZETA))


def _head_slab(x_ref, b, hd):
    group = X_D_HALVES * N_X_HEADS
    halves = [x_ref[b, pl.ds(dh * N_X_HEADS + hd, N_MEM, stride=group), :] for dh in range(X_D_HALVES)]
    return jnp.concatenate(halves, axis=1).astype(BF16)


def _sample_xattn_kernel(h_ref, g_ref, wq_ref, wo_ref, xk_ref, xv_ref, out_ref):
    bb = xk_ref.shape[0]
    nt = bb // 2
    h = h_ref[...]
    xn = _rms(h, g_ref[...]).astype(BF16)
    q = _dot(xn, wq_ref[...]) * (X_HEAD_DIM ** -0.5)
    units = [(t, par, hd) for t in range(nt) for par in range(2) for hd in range(N_X_HEADS)]
    qts = [q[t * SUBLANES:(t + 1) * SUBLANES].astype(BF16) for t in range(nt)]
    s = jnp.concatenate(
        [_dot_nt(qts[t][:, hd * X_HEAD_DIM:(hd + 1) * X_HEAD_DIM], _head_slab(xk_ref, 2 * t + par, hd))
         for t, par, hd in units], axis=0)
    m = jnp.max(s, axis=-1, keepdims=True)
    p = jnp.exp(s - m)
    p = p * (1.0 / jnp.sum(p, axis=-1, keepdims=True))
    os_ = {}
    for i, (t, par, hd) in enumerate(units):
        pi = p[i * SUBLANES:(i + 1) * SUBLANES].astype(BF16)
        os_[(t, par, hd)] = _dot(pi, _head_slab(xv_ref, 2 * t + par, hd))
    own0 = lax.broadcasted_iota(jnp.int32, (SUBLANES, 1), 0) < DEC_SEQ
    o_tiles = []
    for t in range(nt):
        o_par = [jnp.concatenate([os_[(t, par, hd)] for hd in range(N_X_HEADS)], axis=1) for par in range(2)]
        o_tiles.append(jnp.where(own0, o_par[0], o_par[1]))
    o = jnp.concatenate(o_tiles, axis=0).astype(BF16)
    out_ref[...] = h + _dot(o, wo_ref[...])


def _mem_rows(c):
    nb = c.shape[0]
    c = c.reshape(nb, N_MEM, N_X_HEADS, X_D_HALVES, LANES)
    return jnp.transpose(c, (0, 1, 3, 2, 4)).reshape(nb, N_MEM * X_D_HALVES * N_X_HEADS, LANES)


def _sample_xattn(h2d, g, w_xq, w_xo, mk, mv):
    n, d = h2d.shape
    nb = mk.shape[0]
    bb = BB_X
    r = bb * DEC_SEQ
    const = lambda shape: pl.BlockSpec(shape, lambda i: (0,) * len(shape))
    row = pl.BlockSpec((r, d), lambda i: (i, 0))
    mem = pl.BlockSpec((bb,) + mk.shape[1:], lambda i: (i, 0, 0))
    return pl.pallas_call(
        _sample_xattn_kernel,
        grid=(nb // bb,),
        in_specs=[row, const((1, d)), const((d, d)), const((d, d)), mem, mem],
        out_specs=row,
        out_shape=jax.ShapeDtypeStruct((n, d), F32),
        compiler_params=pltpu.CompilerParams(
            dimension_semantics=("arbitrary",), vmem_limit_bytes=VMEM_LIMIT),
        name="sample_xattn",
    )(h2d, g, w_xq, w_xo, mk, mv)


def kernel(x_prompt, x_sample, mem_prompt, cache_win_k, cache_win_v, state_ret, cache_mem_k, cache_mem_v,
           g_mix, w_in, attn_sinks, ret_gn_g, ret_gn_b, w_out, g_xattn, g_mem, w_xq, w_xk, w_xv, w_xo,
           g_mlp, w_up, w_down, g_final):
    depth = w_in.shape[0]
    assert depth == 1, "single-layer trunk"
    b, s, d = x_prompt.shape
    nb, ls, _ = x_sample.shape
    row = lambda a: a.reshape(1, -1)
    bf = lambda a: a.astype(BF16)

    w_in_b, w_out_b = bf(w_in[0]), bf(w_out[0])
    w_xq_b, w_xk_b, w_xv_b, w_xo_b = bf(w_xq[0]), bf(w_xk[0]), bf(w_xv[0]), bf(w_xo[0])
    w_up_b, w_dn_b = bf(w_up[0]), bf(w_down[0])
    sinks = attn_sinks[0]
    gn_g, gn_b = row(ret_gn_g[0]), row(ret_gn_b[0])
    g_fin = row(g_final)

    hp, p_wk, p_wv, p_rs = _prompt_mixer(x_prompt, row(g_mix[0]), w_in_b, w_out_b, sinks, gn_g, gn_b)
    mk, mv, mkb, mvb = _memory_kv(mem_prompt.reshape(b * N_MEM, d), row(g_mem[0]), w_xk_b, w_xv_b)
    hp = _prompt_xattn(hp, row(g_xattn[0]), w_xq_b, w_xo_b,
                       mkb.reshape(b, N_MEM, d), mvb.reshape(b, N_MEM, d))

    hs, s_wk, s_wv, s_rs = _sample_mixer(
        x_sample.reshape(nb * ls, d), row(g_mix[0]), w_in_b, w_out_b, sinks, gn_g, gn_b,
        cache_win_k[0].reshape(nb, WINDOW, ATT_KV_W), cache_win_v[0].reshape(nb, WINDOW, ATT_KV_W),
        state_ret[0].reshape(nb, RET_QK_W, RET_V_DIM))
    hs = _sample_xattn(hs, row(g_xattn[0]), w_xq_b, w_xo_b,
                       _mem_rows(cache_mem_k[0]), _mem_rows(cache_mem_v[0]))

    y_prompt, y_sample = _mlp_final(hp.reshape(b * s, d), hs, row(g_mlp[0]), w_up_b, w_dn_b, g_fin)
    y_prompt = y_prompt.reshape(b, s, d)
    y_sample = y_sample.reshape(nb, ls, d)

    win5 = lambda a, n: a.reshape(1, n, WINDOW, N_KV_HEADS, HEAD_DIM)
    ret5 = lambda a, n: a.reshape(1, n, N_RET_HEADS, RET_QK_DIM, RET_V_DIM)
    mem5 = lambda a: jnp.transpose(a.reshape(b, N_MEM, X_D_HALVES, N_X_HEADS, LANES),
                                   (0, 1, 3, 2, 4)).reshape(1, b, N_MEM, N_X_HEADS, X_HEAD_DIM)
    return (y_prompt, y_sample,
            win5(p_wk, b), win5(p_wv, b), ret5(p_rs, b), mem5(mk), mem5(mv),
            win5(s_wk, nb), win5(s_wv, nb), ret5(s_rs, nb))
```

```python
import functools

import jax
import jax.numpy as jnp
import numpy as np
from jax import lax
from jax.experimental import pallas as pl
from jax.experimental.pallas import tpu as pltpu

F32 = jnp.float32
BF16 = jnp.bfloat16

D_MODEL = 1024
BATCH = 8
SEQ = 2048
DEC_BATCH = 128
DEC_SEQ = 4
HEAD_DIM = 64
N_ATT_HEADS = 8
N_KV_HEADS = 2
KV_GROUP = N_ATT_HEADS // N_KV_HEADS
WINDOW = 128
BLK = 128
N_RET_HEADS = 4
RET_QK_DIM = 64
RET_V_DIM = 128
N_MEM = 256
N_X_HEADS = 4
X_HEAD_DIM = D_MODEL // N_X_HEADS
D_FF = 4 * D_MODEL
RMS_EPS = 1e-6
GN_EPS = 1e-5

ATT_Q_W = N_ATT_HEADS * HEAD_DIM
ATT_KV_W = N_KV_HEADS * HEAD_DIM
RET_QK_W = N_RET_HEADS * RET_QK_DIM
RET_V_W = N_RET_HEADS * RET_V_DIM
MIX_OUT = ATT_Q_W + RET_V_W
D_IN = ATT_Q_W + 2 * ATT_KV_W + 2 * RET_QK_W + 2 * RET_V_W
C_QA, C_KV, C_QKR, C_VR, C_GR = 0, 512, 768, 1280, 1792

LANES = 128
SUBLANES = 8
HALF = LANES // 2
X_D_HALVES = X_HEAD_DIM // LANES
NEG = -1e30
VMEM_LIMIT = 52 * 1024 * 1024

TM_MIX = 512
TM_X = 1024
TM_MLP = 1024
SUB_ROWS = 512
FF_CHUNK = 1024
BB_MIX = 16
BB_X = 8

NEG_SLOPES = [-(2.0 ** (-8.0 * (i + 1) / N_ATT_HEADS)) for i in range(N_ATT_HEADS)]
_LOG_G = np.log(1.0 - 2.0 ** (-5.0 - np.arange(N_RET_HEADS))).astype(np.float32).astype(np.float64)


def _prompt_tables():
    qi = np.arange(BLK)[:, None]
    kj = np.arange(2 * BLK)[None, :]
    dist = (qi + BLK - kj).astype(np.float64)
    mask = np.where((dist >= 0) & (dist < WINDOW), 0.0, NEG)
    l = np.arange(BLK, dtype=np.float64)
    diff = l[:, None] - l[None, :]
    decay = np.where(diff >= 0, np.exp(_LOG_G[:, None, None] * np.maximum(diff, 0.0)), 0.0)
    xi = np.exp((l[:, None] + 1.0) * _LOG_G[None, :])
    zeta = np.exp((BLK - 1.0 - l)[:, None] * _LOG_G[None, :])
    xi_t = np.repeat(xi, RET_V_DIM, axis=1)
    zeta_t = np.repeat(zeta, RET_QK_DIM, axis=1)
    f = lambda a: np.asarray(a, np.float32)
    return f(dist), f(mask), f(decay), f(xi_t), f(zeta_t)


def _sample_tables():
    slopes = -np.asarray(NEG_SLOPES)
    bias = np.full((2, N_ATT_HEADS * SUBLANES, 2 * BLK), NEG, np.float64)
    dec = np.zeros((2, N_RET_HEADS * SUBLANES, BLK), np.float64)
    xi = np.zeros((2, N_RET_HEADS * SUBLANES, RET_V_DIM), np.float64)
    zeta = np.zeros((2, SUBLANES, RET_QK_W), np.float64)
    for par in range(2):
        for r in range(SUBLANES):
            own = DEC_SEQ * par <= r < DEC_SEQ * (par + 1)
            t = r - DEC_SEQ * par if own else r % DEC_SEQ
            for h in range(N_ATT_HEADS):
                row = h * SUBLANES + r
                for j in range(WINDOW):
                    d = t + WINDOW - j
                    if 0 <= d < WINDOW:
                        bias[par, row, j] = -slopes[h] * d
                for c in range(DEC_SEQ):
                    d = t - c
                    if d >= 0:
                        bias[par, row, WINDOW + DEC_SEQ * par + c] = -slopes[h] * d
            for h in range(N_RET_HEADS):
                row = h * SUBLANES + r
                if own:
                    xi[par, row, :] = np.exp((t + 1.0) * _LOG_G[h])
                    zeta[par, r, h * RET_QK_DIM:(h + 1) * RET_QK_DIM] = np.exp((DEC_SEQ - 1.0 - t) * _LOG_G[h])
                    for c in range(t + 1):
                        dec[par, row, DEC_SEQ * par + c] = np.exp(_LOG_G[h] * (t - c))
    f = lambda a: np.asarray(a, np.float32)
    return f(bias), f(dec), f(xi), f(zeta)


_P_DIST, _P_MASK, _P_DECAY, _P_XI, _P_ZETA = _prompt_tables()
_S_BIAS, _S_DEC, _S_XI, _S_ZETA = _sample_tables()
_GL_PROMPT = [float(np.exp(_LOG_G[h] * BLK)) for h in range(N_RET_HEADS)]
_GL_SAMPLE = [float(np.exp(_LOG_G[h] * DEC_SEQ)) for h in range(N_RET_HEADS)]


def _rms(x, g):
    return x * lax.rsqrt(jnp.mean(x * x, axis=-1, keepdims=True) + RMS_EPS) * g


def _dot(a, b):
    return jnp.dot(a, b, preferred_element_type=F32)


def _dot_nt(a, b):
    return lax.dot_general(a, b, (((1,), (1,)), ((), ())), preferred_element_type=F32)


def _dot_tn(a, b):
    return lax.dot_general(a, b, (((0,), (0,)), ((), ())), preferred_element_type=F32)


def _silu(g):
    return g * (1.0 / (1.0 + jnp.exp(-g)))


def _half_masks(width):
    lane = lax.broadcasted_iota(jnp.int32, (1, width), 1)
    lo = ((lane & (LANES - 1)) < HALF).astype(F32)
    return lo, 1.0 - lo


def _sink_softmax(s, sink):
    m = jnp.maximum(jnp.max(s, axis=-1, keepdims=True), sink)
    p = jnp.exp(s - m)
    den = jnp.sum(p, axis=-1, keepdims=True) + jnp.exp(sink - m)
    return p * (1.0 / den)


def _group_norm(o, g, b):
    mu = jnp.mean(o, axis=-1, keepdims=True)
    d = o - mu
    var = jnp.mean(d * d, axis=-1, keepdims=True)
    return d * lax.rsqrt(var + GN_EPS) * g + b


def _prompt_mixer_kernel(sinks_ref, x_ref, gmix_ref, win_ref, wout_ref, gng_ref, gnb_ref,
                         dist_ref, mask_ref, decay_ref, xi_ref, zeta_ref,
                         h_ref, wk_ref, wv_ref, st_ref,
                         qlo_s, qhi_s, kd0_s, kd1_s, vd0_s, vd1_s,
                         qrlo_s, qrhi_s, kr_s, vr_s, gate_s, mix_s, bias_s):
    t = pl.program_id(1)
    nt = pl.num_programs(1)
    tm = x_ref.shape[1]
    nblk = tm // BLK

    @pl.when(t == 0)
    def _():
        kd0_s[0:BLK, :] = jnp.zeros((BLK, LANES), BF16)
        kd1_s[0:BLK, :] = jnp.zeros((BLK, LANES), BF16)
        vd0_s[0:BLK, :] = jnp.zeros((BLK, 2 * LANES), BF16)
        vd1_s[0:BLK, :] = jnp.zeros((BLK, 2 * LANES), BF16)
        st_ref[...] = jnp.zeros_like(st_ref)

    @pl.when(t > 0)
    def _():
        kd0_s[0:BLK, :] = kd0_s[tm:tm + BLK, :]
        kd1_s[0:BLK, :] = kd1_s[tm:tm + BLK, :]
        vd0_s[0:BLK, :] = vd0_s[tm:tm + BLK, :]
        vd1_s[0:BLK, :] = vd1_s[tm:tm + BLK, :]

    @pl.when((t == 0) & (pl.program_id(0) == 0))
    def _():
        for h in range(N_ATT_HEADS):
            bias_s[h] = NEG_SLOPES[h] * dist_ref[...] + mask_ref[...]

    x = x_ref[0]
    xn = _rms(x, gmix_ref[...]).astype(BF16)

    z = _dot(xn, win_ref[...])
    lo512, hi512 = _half_masks(ATT_Q_W)
    q = z[:, C_QA:C_QA + ATT_Q_W]
    qlo_s[...] = (q * (lo512 * HEAD_DIM ** -0.5)).astype(BF16)
    qhi_s[...] = (q * (hi512 * HEAD_DIM ** -0.5)).astype(BF16)

    low = lax.broadcasted_iota(jnp.int32, (tm, LANES), 1) < HALF
    kv = z[:, C_KV:C_KV + 2 * ATT_KV_W]
    k = kv[:, :ATT_KV_W]
    v = kv[:, ATT_KV_W:]
    k_r = pltpu.roll(k, HALF, axis=1)
    v_r = pltpu.roll(v, HALF, axis=1)
    kd0_s[BLK:BLK + tm, :] = jnp.where(low, k, k_r).astype(BF16)
    kd1_s[BLK:BLK + tm, :] = jnp.where(low, k_r, k).astype(BF16)
    vd0_s[BLK:BLK + tm, 0:LANES] = jnp.where(low, v, 1.0).astype(BF16)
    vd0_s[BLK:BLK + tm, LANES:2 * LANES] = jnp.where(low, 1.0, v_r).astype(BF16)
    vd1_s[BLK:BLK + tm, 0:LANES] = jnp.where(low, v_r, 1.0).astype(BF16)
    vd1_s[BLK:BLK + tm, LANES:2 * LANES] = jnp.where(low, 1.0, v).astype(BF16)

    @pl.when(t == nt - 1)
    def _():
        wk_ref[0] = k[tm - WINDOW:, :]
        wv_ref[0] = v[tm - WINDOW:, :]

    lo256, hi256 = _half_masks(RET_QK_W)
    qkr = z[:, C_QKR:C_QKR + 2 * RET_QK_W]
    qr = qkr[:, :RET_QK_W]
    qrlo_s[...] = (qr * lo256).astype(BF16)
    qrhi_s[...] = (qr * hi256).astype(BF16)
    kr_s[...] = qkr[:, RET_QK_W:] * (RET_QK_DIM ** -0.5)
    vr_s[...] = z[:, C_VR:C_VR + RET_V_W].astype(BF16)
    gate_s[...] = _silu(z[:, C_GR:C_GR + RET_V_W])

    lowb =lax.broadcasted_iota(jnp.int32, (BLK, LANES), 1) < HALF
    col = lax.broadcasted_iota(jnp.int32, (BLK, 2 * BLK), 1)
    first_mask = jnp.where((col < BLK) & (t == 0), NEG, 0.0)
    kd_refs = (kd0_s, kd1_s)
    vd_refs = (vd0_s, vd1_s)
    n_pairs = N_RET_HEADS // 2
    state = [st_ref[0, i * LANES:(i + 1) * LANES, :] for i in range(n_pairs)]

    for j in range(nblk):
        rows = slice(j * BLK, (j + 1) * BLK)
        krows = slice(j * BLK, (j + 2) * BLK)

        for kvh in range(N_KV_HEADS):
            kd = kd_refs[kvh][krows, :]
            vd = vd_refs[kvh][krows, :]
            c0 = kvh * KV_GROUP * HEAD_DIM
            qst = jnp.concatenate([qlo_s[rows, c0:c0 + LANES], qhi_s[rows, c0:c0 + LANES],
                                   qlo_s[rows, c0 + LANES:c0 + 2 * LANES],
                                   qhi_s[rows, c0 + LANES:c0 + 2 * LANES]], axis=0)
            s = _dot_nt(qst, kd)
            es, esink = [], []
            for g in range(KV_GROUP):
                h = kvh * KV_GROUP + g
                sg = s[g * BLK:(g + 1) * BLK] + bias_s[h]
                if j == 0:
                    sg = sg + first_mask
                sink = sinks_ref[h]
                m = jnp.maximum(jnp.max(sg, axis=-1, keepdims=True), sink)
                es.append(jnp.exp(sg - m).astype(BF16))
                esink.append(jnp.exp(sink - m))
            o = _dot(jnp.concatenate(es, axis=0), vd)
            for pair in range(KV_GROUP // 2):
                oe = o[2 * pair * BLK:(2 * pair + 1) * BLK]
                oo = o[(2 * pair + 1) * BLK:(2 * pair + 2) * BLK]
                num = jnp.where(lowb, oe[:, :LANES], oo[:, LANES:])
                den = (jnp.where(lowb, oe[:, LANES:], oo[:, :LANES])
                       + jnp.where(lowb, esink[2 * pair], esink[2 * pair + 1]))
                cs = c0 + pair * LANES
                mix_s[rows, cs:cs + LANES] = (num * (1.0 / den)).astype(BF16)

        for i in range(n_pairs):
            lsl = slice(i * LANES, (i + 1) * LANES)
            kp = kr_s[rows, lsl]
            sp = state[i]
            vpair = vr_s[rows, 2 * i * RET_V_DIM:(2 * i + 2) * RET_V_DIM]
            q2 = jnp.concatenate([qrlo_s[rows, lsl], qrhi_s[rows, lsl]], axis=0)
            a = _dot_nt(q2, kp.astype(BF16))
            inner = jnp.concatenate([a[:BLK] * decay_ref[2 * i], a[BLK:] * decay_ref[2 * i + 1]], axis=0)
            oi = _dot(inner.astype(BF16), vpair)
            oc = _dot(q2, sp.astype(BF16))
            for half in range(2):
                h = 2 * i + half
                vsl = slice(h * RET_V_DIM, (h + 1) * RET_V_DIM)
                hr = slice(half * BLK, (half + 1) * BLK)
                o = oi[hr, half * RET_V_DIM:(half + 1) * RET_V_DIM] + oc[hr] * xi_ref[:, vsl]
                r = _group_norm(o, gng_ref[:, vsl], gnb_ref[:, vsl]) * gate_s[rows, vsl]
                mix_s[rows, ATT_Q_W + h * RET_V_DIM:ATT_Q_W + (h + 1) * RET_V_DIM] = r.astype(BF16)
            kz = (kp * zeta_ref[:, lsl]).astype(BF16)
            u = _dot_tn(kz, vpair)
            state[i] = jnp.concatenate(
                [_GL_PROMPT[2 * i] * sp[:RET_QK_DIM] + u[:RET_QK_DIM, :RET_V_DIM],
                 _GL_PROMPT[2 * i + 1] * sp[RET_QK_DIM:] + u[RET_QK_DIM:, RET_V_DIM:]], axis=0)

    for i in range(n_pairs):
        st_ref[0, i * LANES:(i + 1) * LANES, :] = state[i]

    h_ref[0] = x + _dot(mix_s[...], wout_ref[...])


def _prompt_mixer(x, g_mix, w_in, w_out, sinks, gn_g, gn_b):
    b, s, d = x.shape
    tm = TM_MIX
    const = lambda shape: pl.BlockSpec(shape, lambda i, j: (0,) * len(shape))
    return pl.pallas_call(
        _prompt_mixer_kernel,
        grid=(b, s // tm),
        in_specs=[
            pl.BlockSpec(memory_space=pltpu.SMEM),
            pl.BlockSpec((1, tm, d), lambda i, j: (i, j, 0)),
            const((1, d)), const((d, D_IN)), const((MIX_OUT, d)),
            const((1, RET_V_W)), const((1, RET_V_W)),
            const((BLK, 2 * BLK)), const((BLK, 2 * BLK)),
            const((N_RET_HEADS, BLK, BLK)), const((BLK, RET_V_W)), const((BLK, RET_QK_W)),
        ],
        out_specs=[
            pl.BlockSpec((1, tm, d), lambda i, j: (i, j, 0)),
            pl.BlockSpec((1, WINDOW, ATT_KV_W), lambda i, j: (i, 0, 0)),
            pl.BlockSpec((1, WINDOW, ATT_KV_W), lambda i, j: (i, 0, 0)),
            pl.BlockSpec((1, RET_QK_W, RET_V_DIM), lambda i, j: (i, 0, 0)),
        ],
        out_shape=[
            jax.ShapeDtypeStruct((b, s, d), F32),
            jax.ShapeDtypeStruct((b, WINDOW, ATT_KV_W), F32),
            jax.ShapeDtypeStruct((b, WINDOW, ATT_KV_W), F32),
            jax.ShapeDtypeStruct((b, RET_QK_W, RET_V_DIM), F32),
        ],
        scratch_shapes=[
            pltpu.VMEM((tm, ATT_Q_W), BF16), pltpu.VMEM((tm, ATT_Q_W), BF16),
            pltpu.VMEM((tm + BLK, LANES), BF16), pltpu.VMEM((tm + BLK, LANES), BF16),
            pltpu.VMEM((tm + BLK, 2 * LANES), BF16), pltpu.VMEM((tm + BLK, 2 * LANES), BF16),
            pltpu.VMEM((tm, RET_QK_W), BF16), pltpu.VMEM((tm, RET_QK_W), BF16),
            pltpu.VMEM((tm, RET_QK_W), F32), pltpu.VMEM((tm, RET_V_W), BF16),
            pltpu.VMEM((tm, RET_V_W), F32), pltpu.VMEM((tm, MIX_OUT), BF16),
            pltpu.VMEM((N_ATT_HEADS, BLK, 2 * BLK), F32),
        ],
        compiler_params=pltpu.CompilerParams(
            dimension_semantics=("arbitrary", "arbitrary"), vmem_limit_bytes=VMEM_LIMIT),
        name="prompt_mixer",
    )(sinks, x, g_mix, w_in, w_out, gn_g, gn_b,
      jnp.asarray(_P_DIST), jnp.asarray(_P_MASK), jnp.asarray(_P_DECAY), jnp.asarray(_P_XI),
      jnp.asarray(_P_ZETA))


def _pm_project_stages(x, slot, gmix_ref, win_ref, sc, kv_out=None):
    tm = x.shape[0]
    xn = _rms(x, gmix_ref[...]).astype(BF16)

    def stage_q():
        q = _dot(xn, win_ref[:, C_QA:C_QA + ATT_Q_W])
        lo512, hi512 = _half_masks(ATT_Q_W)
        sc["qlo"][slot] = (q * (lo512 * HEAD_DIM ** -0.5)).astype(BF16)
        sc["qhi"][slot] = (q * (hi512 * HEAD_DIM ** -0.5)).astype(BF16)

    def stage_kv_qkr():
        z = _dot(xn, win_ref[:, C_KV:C_VR])
        low = lax.broadcasted_iota(jnp.int32, (tm, LANES), 1) < HALF
        k = z[:, 0:ATT_KV_W]
        v = z[:, ATT_KV_W:2 * ATT_KV_W]
        if kv_out is not None:
            kv_out[0][0] = k[tm - WINDOW:, :]
            kv_out[1][0] = v[tm - WINDOW:, :]
        k_r = pltpu.roll(k, HALF, axis=1)
        v_r = pltpu.roll(v, HALF, axis=1)
        sc["kd0"][slot] = jnp.where(low, k, k_r).astype(BF16)
        sc["kd1"][slot] = jnp.where(low, k_r, k).astype(BF16)
        sc["vd0"][slot, :, 0:LANES] = jnp.where(low, v, 1.0).astype(BF16)
        sc["vd0"][slot, :, LANES:2 * LANES] = jnp.where(low, 1.0, v_r).astype(BF16)
        sc["vd1"][slot, :, 0:LANES] = jnp.where(low, v_r, 1.0).astype(BF16)
        sc["vd1"][slot, :, LANES:2 * LANES] = jnp.where(low, 1.0, v).astype(BF16)
        lo256, hi256 = _half_masks(RET_QK_W)
        qr = z[:, 2 * ATT_KV_W:2 * ATT_KV_W + RET_QK_W]
        sc["qrlo"][slot] = (qr * lo256).astype(BF16)
        sc["qrhi"][slot] = (qr * hi256).astype(BF16)
        sc["kr"][slot] = z[:, 2 * ATT_KV_W + RET_QK_W:] * (RET_QK_DIM ** -0.5)

    def stage_vr():
        sc["vr"][slot] = _dot(xn, win_ref[:, C_VR:C_VR + RET_V_W]).astype(BF16)

    def stage_gate():
        sc["gate"][slot] = _silu(_dot(xn, win_ref[:, C_GR:C_GR + RET_V_W]))

    return [stage_q, stage_kv_qkr, stage_vr, stage_gate]


def _pm_last_block(slot, tm, sc):
    rows = slice(tm - BLK, tm)
    return ([sc["kd0"][slot, rows, :], sc["kd1"][slot, rows, :]],
            [sc["vd0"][slot, rows, :], sc["vd1"][slot, rows, :]])


def _pm_blocks(slot, prev_kd, prev_vd, is_first, state, x, fillers, tm, sinks_ref, wout_ref, gng_ref, gnb_ref,
               decay_ref, xi_ref, zeta_ref, sc):
    nblk = tm // BLK
    lowb = lax.broadcasted_iota(jnp.int32, (BLK, LANES), 1) < HALF
    col = lax.broadcasted_iota(jnp.int32, (BLK, 2 * BLK), 1)
    first_mask = None if is_first is False else jnp.where((col < BLK) & is_first, NEG, 0.0)
    kd_refs = (sc["kd0"], sc["kd1"])
    vd_refs = (sc["vd0"], sc["vd1"])
    qlo, qhi, mix = sc["qlo"], sc["qhi"], sc["mix"]
    n_pairs = N_RET_HEADS // 2

    for j in range(nblk):
        rows = slice(j * BLK, (j + 1) * BLK)
        for kvh in range(N_KV_HEADS):
            if j == 0:
                kd = jnp.concatenate([prev_kd[kvh], kd_refs[kvh][slot, rows, :]], axis=0)
                vd = jnp.concatenate([prev_vd[kvh], vd_refs[kvh][slot, rows, :]], axis=0)
            else:
                krows = slice((j - 1) * BLK, (j + 1) * BLK)
                kd = kd_refs[kvh][slot, krows, :]
                vd = vd_refs[kvh][slot, krows, :]
            c0 = kvh * KV_GROUP * HEAD_DIM
            qst = jnp.concatenate([qlo[slot, rows, c0:c0 + LANES], qhi[slot, rows, c0:c0 + LANES],
                                   qlo[slot, rows, c0 + LANES:c0 + 2 * LANES],
                                   qhi[slot, rows, c0 + LANES:c0 + 2 * LANES]], axis=0)
            s = _dot_nt(qst, kd)
            es, esink = [], []
            for g in range(KV_GROUP):
                h = kvh * KV_GROUP + g
                sg = s[g * BLK:(g + 1) * BLK] + sc["bias"][h]
                if j == 0 and first_mask is not None:
                    sg = sg + first_mask
                sink = sinks_ref[h]
                m = jnp.maximum(jnp.max(sg, axis=-1, keepdims=True), sink)
                es.append(jnp.exp(sg - m).astype(BF16))
                esink.append(jnp.exp(sink - m))
            o = _dot(jnp.concatenate(es, axis=0), vd)
            for pair in range(KV_GROUP // 2):
                oe = o[2 * pair * BLK:(2 * pair + 1) * BLK]
                oo = o[(2 * pair + 1) * BLK:(2 * pair + 2) * BLK]
                num = jnp.where(lowb, oe[:, :LANES], oo[:, LANES:])
                den = (jnp.where(lowb, oe[:, LANES:], oo[:, :LANES])
                       + jnp.where(lowb, esink[2 * pair], esink[2 * pair + 1]))
                cs = c0 + pair * LANES
                mix[slot, rows, cs:cs + LANES] = (num * (1.0 / den)).astype(BF16)

        for i in range(n_pairs):
            lsl = slice(i * LANES, (i + 1) * LANES)
            kp = sc["kr"][slot, rows, lsl]
            sp = state[i]
            vpair = sc["vr"][slot, rows, 2 * i * RET_V_DIM:(2 * i + 2) * RET_V_DIM]
            q2 = jnp.concatenate([sc["qrlo"][slot, rows, lsl], sc["qrhi"][slot, rows, lsl]], axis=0)
            a = _dot_nt(q2, kp.astype(BF16))
            inner = jnp.concatenate([a[:BLK] * decay_ref[2 * i], a[BLK:] * decay_ref[2 * i + 1]], axis=0)
            oi = _dot(inner.astype(BF16), vpair)
            oc = _dot(q2, sp.astype(BF16))
            for half in range(2):
                h = 2 * i + half
                vsl = slice(h * RET_V_DIM, (h + 1) * RET_V_DIM)
                hr = slice(half * BLK, (half + 1) * BLK)
                o = oi[hr, half * RET_V_DIM:(half + 1) * RET_V_DIM] + oc[hr] * xi_ref[:, vsl]
                r = _group_norm(o, gng_ref[:, vsl], gnb_ref[:, vsl]) * sc["gate"][slot, rows, vsl]
                mix[slot, rows, ATT_Q_W + h * RET_V_DIM:ATT_Q_W + (h + 1) * RET_V_DIM] = r.astype(BF16)
            kz = (kp * zeta_ref[:, lsl]).astype(BF16)
            u = _dot_tn(kz, vpair)
            state[i] = jnp.concatenate(
                [_GL_PROMPT[2 * i] * sp[:RET_QK_DIM] + u[:RET_QK_DIM, :RET_V_DIM],
                 _GL_PROMPT[2 * i + 1] * sp[RET_QK_DIM:] + u[RET_QK_DIM:, RET_V_DIM:]], axis=0)

        fillers[j]()

    return x + _dot(mix[slot], wout_ref[...]), state


def _prompt_mixer_kernel_p(sinks_ref, xpair_ref, xnext_ref, gmix_ref, win_ref, wout_ref, gng_ref, gnb_ref,
                           dist_ref, mask_ref, decay_ref, xi_ref, zeta_ref,
                           h_ref, wk_ref, wv_ref, st_ref,
                           qlo_s, qhi_s, kd0_s, kd1_s, vd0_s, vd1_s,
                           qrlo_s, qrhi_s, kr_s, vr_s, gate_s, mix_s, bias_s, state_s):
    u = pl.program_id(0)
    tm = xnext_ref.shape[0]
    sc = dict(qlo=qlo_s, qhi=qhi_s, kd0=kd0_s, kd1=kd1_s, vd0=vd0_s, vd1=vd1_s, qrlo=qrlo_s, qrhi=qrhi_s,
              kr=kr_s, vr=vr_s, gate=gate_s, mix=mix_s, bias=bias_s)
    n_pairs = N_RET_HEADS // 2
    blocks = functools.partial(_pm_blocks, tm=tm, sinks_ref=sinks_ref, wout_ref=wout_ref, gng_ref=gng_ref,
                               gnb_ref=gnb_ref, decay_ref=decay_ref, xi_ref=xi_ref, zeta_ref=zeta_ref, sc=sc)

    @pl.when(u == 0)
    def _():
        for h in range(N_ATT_HEADS):
            bias_s[h] = NEG_SLOPES[h] * dist_ref[...] + mask_ref[...]
        state_s[...] = jnp.zeros_like(state_s)
        kd0_s[1] = jnp.zeros(kd0_s.shape[1:], BF16)
        kd1_s[1] = jnp.zeros(kd1_s.shape[1:], BF16)
        vd0_s[1] = jnp.zeros(vd0_s.shape[1:], BF16)
        vd1_s[1] = jnp.zeros(vd1_s.shape[1:], BF16)
        for stage in _pm_project_stages(xpair_ref[0:tm, :], 0, gmix_ref, win_ref, sc):
            stage()

    seq_start = (u % 2) == 0
    state = [jnp.where(seq_start, 0.0, state_s[i * LANES:(i + 1) * LANES, :]) for i in range(n_pairs)]

    prev_kd, prev_vd = _pm_last_block(1, tm, sc)
    stages = _pm_project_stages(xpair_ref[tm:2 * tm, :], 1, gmix_ref, win_ref, sc, kv_out=(wk_ref, wv_ref))
    h0, state = blocks(0, prev_kd, prev_vd, seq_start, state, xpair_ref[0:tm, :], stages)
    h_ref[0:tm, :] = h0

    prev_kd, prev_vd = _pm_last_block(0, tm, sc)
    stages = _pm_project_stages(xnext_ref[...], 0, gmix_ref, win_ref, sc)
    h1, state = blocks(1, prev_kd, prev_vd, False, state, xpair_ref[tm:2 * tm, :], stages)
    h_ref[tm:2 * tm, :] = h1

    for i in range(n_pairs):
        state_s[i * LANES:(i + 1) * LANES, :] = state[i]
        st_ref[0, i * LANES:(i + 1) * LANES, :] = state[i]


def _prompt_mixer_p(x, g_mix, w_in, w_out, sinks, gn_g, gn_b):
    b, s, d = x.shape
    tm = TM_MIX
    n_tiles = b * s // tm
    steps = n_tiles // 2
    seq_steps = s // (2 * tm)
    assert s % (2 * tm) == 0 and seq_steps == 2, "kernel assumes 4 tiles per sequence"
    x2d = x.reshape(b * s, d)
    const = lambda shape: pl.BlockSpec(shape, lambda i: (0,) * len(shape), pipeline_mode=pl.Buffered(1))
    slot2 = lambda rows, cols, dt: pltpu.VMEM((2, rows, cols), dt)
    outs = pl.pallas_call(
        _prompt_mixer_kernel_p,
        grid=(steps,),
        in_specs=[
            pl.BlockSpec(memory_space=pltpu.SMEM),
            pl.BlockSpec((2 * tm, d), lambda i: (i, 0)),
            pl.BlockSpec((tm, d), lambda i: (jnp.minimum(2 * i + 2, n_tiles - 1), 0)),
            const((1, d)), const((d, D_IN)), const((MIX_OUT, d)),
            const((1, RET_V_W)), const((1, RET_V_W)),
            const((BLK, 2 * BLK)), const((BLK, 2 * BLK)),
            const((N_RET_HEADS, BLK, BLK)), const((BLK, RET_V_W)), const((BLK, RET_QK_W)),
        ],
        out_specs=[
            pl.BlockSpec((2 * tm, d), lambda i: (i, 0)),
            pl.BlockSpec((1, WINDOW, ATT_KV_W), lambda i: (i // seq_steps, 0, 0)),
            pl.BlockSpec((1, WINDOW, ATT_KV_W), lambda i: (i // seq_steps, 0, 0)),
            pl.BlockSpec((1, RET_QK_W, RET_V_DIM), lambda i: (i // seq_steps, 0, 0)),
        ],
        out_shape=[
            jax.ShapeDtypeStruct((b * s, d), F32),
            jax.ShapeDtypeStruct((b, WINDOW, ATT_KV_W), F32),
            jax.ShapeDtypeStruct((b, WINDOW, ATT_KV_W), F32),
            jax.ShapeDtypeStruct((b, RET_QK_W, RET_V_DIM), F32),
        ],
        scratch_shapes=[
            slot2(tm, ATT_Q_W, BF16), slot2(tm, ATT_Q_W, BF16),
            slot2(tm, LANES, BF16), slot2(tm, LANES, BF16),
            slot2(tm, 2 * LANES, BF16), slot2(tm, 2 * LANES, BF16),
            slot2(tm, RET_QK_W, BF16), slot2(tm, RET_QK_W, BF16),
            slot2(tm, RET_QK_W, F32), slot2(tm, RET_V_W, BF16),
            slot2(tm, RET_V_W, F32), slot2(tm, MIX_OUT, BF16),
            pltpu.VMEM((N_ATT_HEADS, BLK, 2 * BLK), F32),
            pltpu.VMEM((RET_QK_W, RET_V_DIM), F32),
        ],
        compiler_params=pltpu.CompilerParams(
            dimension_semantics=("arbitrary",), vmem_limit_bytes=VMEM_LIMIT),
        name="prompt_mixer",
    )(sinks, x2d, x2d, g_mix, w_in, w_out, gn_g, gn_b,
      jnp.asarray(_P_DIST), jnp.asarray(_P_MASK), jnp.asarray(_P_DECAY), jnp.asarray(_P_XI),
      jnp.asarray(_P_ZETA))
    return (outs[0].reshape(b, s, d),) + tuple(outs[1:])


def _memkv_kernel(mem_ref, g_ref, wk_ref, wv_ref, mk_ref, mv_ref, mkb_ref, mvb_ref):
    mn = _rms(mem_ref[...], g_ref[...]).astype(BF16)
    mk = _dot(mn, wk_ref[...])
    mv = _dot(mn, wv_ref[...])
    tm = mem_ref.shape[0]
    group = X_D_HALVES * N_X_HEADS
    for hd in range(N_X_HEADS):
        for dh in range(X_D_HALVES):
            cols = slice(hd * X_HEAD_DIM + dh * LANES, hd * X_HEAD_DIM + (dh + 1) * LANES)
            rows = pl.ds(dh * N_X_HEADS + hd, tm, stride=group)
            mk_ref[rows, :] = mk[:, cols]
            mv_ref[rows, :] = mv[:, cols]
    mkb_ref[...] = mk.astype(BF16)
    mvb_ref[...] = mv.astype(BF16)


def _memory_kv(mem2d, g_mem, w_xk, w_xv):
    n, d = mem2d.shape
    tm = 512
    row = pl.BlockSpec((tm, d), lambda i: (i, 0))
    rows_out = pl.BlockSpec((tm * d // LANES, LANES), lambda i: (i, 0))
    const = lambda shape: pl.BlockSpec(shape, lambda i: (0,) * len(shape))
    return pl.pallas_call(
        _memkv_kernel,
        grid=(n // tm,),
        in_specs=[row, const((1, d)), const((d, d)), const((d, d))],
        out_specs=[rows_out, rows_out, row, row],
        out_shape=[jax.ShapeDtypeStruct((n * d // LANES, LANES), F32),
                   jax.ShapeDtypeStruct((n * d // LANES, LANES), F32),
                   jax.ShapeDtypeStruct((n, d), BF16), jax.ShapeDtypeStruct((n, d), BF16)],
        compiler_params=pltpu.CompilerParams(
            dimension_semantics=("arbitrary",), vmem_limit_bytes=VMEM_LIMIT),
        name="memory_kv",
    )(mem2d, g_mem, w_xk, w_xv)


def _prompt_xattn_kernel(h_ref, g_ref, wq_ref, wo_ref, mk_ref, mv_ref, out_ref, o_s):
    for r0 in range(0, h_ref.shape[1], SUB_ROWS):
        rows = slice(r0, r0 + SUB_ROWS)
        h = h_ref[0, rows, :]
        xn = _rms(h, g_ref[...]).astype(BF16)
        q = (_dot(xn, wq_ref[...]) * (X_HEAD_DIM ** -0.5)).astype(BF16)
        for hd in range(N_X_HEADS):
            sl = slice(hd * X_HEAD_DIM, (hd + 1) * X_HEAD_DIM)
            s = _dot_nt(q[:, sl], mk_ref[0, :, sl])
            m = jnp.max(s, axis=-1, keepdims=True)
            p = jnp.exp(s - m)
            p = p * (1.0 / jnp.sum(p, axis=-1, keepdims=True))
            o_s[rows, sl] = _dot(p.astype(BF16), mv_ref[0, :, sl]).astype(BF16)
        out_ref[0, rows, :] = h + _dot(o_s[rows, :], wo_ref[...])


def _prompt_xattn(h, g, w_xq, w_xo, mkb, mvb):
    b, s, d = h.shape
    tm = TM_X
    const = lambda shape: pl.BlockSpec(shape, lambda i, j: (0,) * len(shape))
    tok = pl.BlockSpec((1, tm, d), lambda i, j: (i, j, 0))
    mem = pl.BlockSpec((1, N_MEM, d), lambda i, j: (i, 0, 0))
    return pl.pallas_call(
        _prompt_xattn_kernel,
        grid=(b, s // tm),
        in_specs=[tok, const((1, d)), const((d, d)), const((d, d)), mem, mem],
        out_specs=tok,
        out_shape=jax.ShapeDtypeStruct((b, s, d), F32),
        scratch_shapes=[pltpu.VMEM((tm, d), BF16)],
        compiler_params=pltpu.CompilerParams(
            dimension_semantics=("arbitrary", "arbitrary"), vmem_limit_bytes=VMEM_LIMIT),
        name="prompt_xattn",
    )(h, g, w_xq, w_xo, mkb, mvb)


def _mlp_rows(h_ref, y_ref, g_ref, wup_ref, wdn_ref, gf_ref):
    for r0 in range(0, h_ref.shape[0], SUB_ROWS):
        rows = slice(r0, r0 + SUB_ROWS)
        h = h_ref[rows, :]
        xn = _rms(h, g_ref[...]).astype(BF16)
        acc = h
        for c in range(D_FF // FF_CHUNK):
            sl = slice(c * FF_CHUNK, (c + 1) * FF_CHUNK)
            u = jnp.maximum(_dot(xn, wup_ref[:, sl]), 0.0)
            acc = acc + _dot((u * u).astype(BF16), wdn_ref[sl, :])
        y_ref[rows, :] = _rms(acc, gf_ref[...])


def _mlp_kernel(hp_ref, hs_ref, g_ref, wup_ref, wdn_ref, gf_ref, yp_ref, ys_ref):
    last = pl.num_programs(0) - 1

    @pl.when(pl.program_id(0) < last)
    def _():
        _mlp_rows(hp_ref, yp_ref, g_ref, wup_ref, wdn_ref, gf_ref)

    @pl.when(pl.program_id(0) == last)
    def _():
        _mlp_rows(hs_ref, ys_ref, g_ref, wup_ref, wdn_ref, gf_ref)


def _mlp_final(hp2d, hs2d, g_mlp, w_up, w_down, g_final):
    n, d = hp2d.shape
    ns = hs2d.shape[0]
    tm = TM_MLP
    n_tiles = n // tm
    prompt = pl.BlockSpec((tm, d), lambda i: (jnp.minimum(i, n_tiles - 1), 0))
    sample = pl.BlockSpec((ns, d), lambda i: (0, 0))
    const = lambda shape: pl.BlockSpec(shape, lambda i: (0,) * len(shape), pipeline_mode=pl.Buffered(1))
    return pl.pallas_call(
        _mlp_kernel,
        grid=(n_tiles + 1,),
        in_specs=[prompt, sample, const((1, d)), const((d, D_FF)), const((D_FF, d)), const((1, d))],
        out_specs=[prompt, sample],
        out_shape=[jax.ShapeDtypeStruct((n, d), F32), jax.ShapeDtypeStruct((ns, d), F32)],
        compiler_params=pltpu.CompilerParams(
            dimension_semantics=("arbitrary",), vmem_limit_bytes=VMEM_LIMIT),
        name="mlp_final",
    )(hp2d, hs2d, g_mlp, w_up, w_down, g_final)


def _sample_mixer_kernel(sinks_ref, x_ref, gmix_ref, win_ref, wout_ref, gng_ref, gnb_ref,
                         ck_ref, cv_ref, st_ref, bias_ref, dec_ref, xi_ref, zeta_ref,
                         h_ref, swk_ref, swv_ref, sst_ref):
    bb = ck_ref.shape[0]
    nt = bb // 2
    x = x_ref[...]
    xn = _rms(x, gmix_ref[...]).astype(BF16)
    tile3 = lambda a: a.reshape(nt, SUBLANES, a.shape[-1])

    q = _dot(xn, win_ref[:, C_QA:C_QA + ATT_Q_W]) * (HEAD_DIM ** -0.5)
    kv = _dot(xn, win_ref[:, C_KV:C_KV + 2 * ATT_KV_W])
    qkr = _dot(xn, win_ref[:, C_QKR:C_QKR + 2 * RET_QK_W])
    vr = _dot(xn, win_ref[:, C_VR:C_VR + RET_V_W])
    gate3 = tile3(_silu(_dot(xn, win_ref[:, C_GR:C_GR + RET_V_W])))

    lo512, hi512 = _half_masks(ATT_Q_W)
    q_r = pltpu.roll(q, HALF, axis=1)
    q_nat3 = tile3(q)
    q_rot3 = tile3(q_r)
    lo3 = lo512.reshape(1, 1, ATT_Q_W)
    hi3 = hi512.reshape(1, 1, ATT_Q_W)
    qa3 = (q_nat3 * lo3).astype(BF16)
    qb3 = (q_rot3 * lo3).astype(BF16)
    qc3 = (q_rot3 * hi3).astype(BF16)
    qd3 = (q_nat3 * hi3).astype(BF16)
    t128 = lambda a, i: a[:, :, i * LANES:(i + 1) * LANES]
    qs = jnp.concatenate([t128(qa3, 0), t128(qb3, 1), t128(qa3, 1), t128(qb3, 2),
                          t128(qc3, 2), t128(qd3, 2), t128(qc3, 3), t128(qd3, 3)], axis=1)

    k3 = tile3(kv[:, :ATT_KV_W])
    v3 = tile3(kv[:, ATT_KV_W:])
    pad_kv = jnp.zeros((nt, BLK - SUBLANES, LANES), BF16)

    lo256, _ = _half_masks(RET_QK_W)
    qr3 = tile3(qkr[:, :RET_QK_W])
    kr3 = tile3(qkr[:, RET_QK_W:] * (RET_QK_DIM ** -0.5))
    vr3 = tile3(vr)
    lane256 = lax.broadcasted_iota(jnp.int32, (1, 1, RET_QK_W), 2)
    qrs = jnp.concatenate(
        [(qr3 * ((lane256 >= h * RET_QK_DIM) & (lane256 < (h + 1) * RET_QK_DIM)).astype(F32)).astype(BF16)
         for h in range(N_RET_HEADS)],
        axis=1)
    kr_pad = jnp.concatenate([kr3.astype(BF16), jnp.zeros((nt, BLK - SUBLANES, RET_QK_W), BF16)], axis=1)
    vr_pad = jnp.concatenate([vr3.astype(BF16), jnp.zeros((nt, BLK - SUBLANES, RET_V_W), BF16)], axis=1)

    lane = lax.broadcasted_iota(jnp.int32, (1, 1, LANES), 2)
    row8 = lax.broadcasted_iota(jnp.int32, (1, SUBLANES, 1), 1)
    bmm_nt = lambda a, b: jnp.einsum('bqd,bkd->bqk', a, b, preferred_element_type=F32)
    bmm = lambda a, b: jnp.einsum('bqk,bkd->bqd', a, b, preferred_element_type=F32)

    att_par, ret_par = [], []
    for par in range(2):
        bsl = pl.ds(par, nt, stride=2)
        ck = ck_ref[bsl]
        cv = cv_ref[bsl]
        swk_ref[bsl, 0:WINDOW - DEC_SEQ, :] = ck[:, DEC_SEQ:, :]
        swv_ref[bsl, 0:WINDOW - DEC_SEQ, :] = cv[:, DEC_SEQ:, :]
        swk_ref[bsl, WINDOW - DEC_SEQ:WINDOW, :] = k3[:, DEC_SEQ * par:DEC_SEQ * (par + 1), :]
        swv_ref[bsl, WINDOW - DEC_SEQ:WINDOW, :] = v3[:, DEC_SEQ * par:DEC_SEQ * (par + 1), :]

        kfull = jnp.concatenate([ck.astype(BF16), k3.astype(BF16), pad_kv], axis=1)
        vfull = jnp.concatenate([cv.astype(BF16), v3.astype(BF16), pad_kv], axis=1)
        s = bmm_nt(qs, kfull) + bias_ref[par]
        ps = []
        for h in range(N_ATT_HEADS):
            ps.append(_sink_softmax(s[:, h * SUBLANES:(h + 1) * SUBLANES, :], sinks_ref[h]).astype(BF16))
        o = bmm(jnp.concatenate(ps, axis=1), vfull)
        o_r = pltpu.roll(o.reshape(nt * N_ATT_HEADS * SUBLANES, LANES), HALF, axis=1).reshape(o.shape)
        hr = lambda a, h: a[:, h * SUBLANES:(h + 1) * SUBLANES, :]
        low = lane < HALF
        att_par.append(jnp.concatenate([
            jnp.where(low, hr(o, 0), hr(o_r, 1)), jnp.where(low, hr(o, 2), hr(o_r, 3)),
            jnp.where(low, hr(o_r, 4), hr(o, 5)), jnp.where(low, hr(o_r, 6), hr(o, 7))], axis=2))

        st = st_ref[bsl]
        oc = bmm(qrs, st.astype(BF16))
        inner = (bmm_nt(qrs, kr_pad) * dec_ref[par]).astype(BF16)
        oi = bmm(inner, vr_pad)
        rs = []
        for h in range(N_RET_HEADS):
            vsl = slice(h * RET_V_DIM, (h + 1) * RET_V_DIM)
            rsl = slice(h * SUBLANES, (h + 1) * SUBLANES)
            o_h = oi[:, rsl, vsl] + oc[:, rsl, :] * xi_ref[par, rsl, :]
            rs.append(_group_norm(o_h, gng_ref[:, vsl], gnb_ref[:, vsl]) * gate3[:, :, vsl])
        ret_par.append(jnp.concatenate(rs, axis=2))

        kz3 = (kr3 * zeta_ref[par]).astype(BF16)
        vr3_b = vr3.astype(BF16)
        for p in range(nt):
            for i in range(N_RET_HEADS // 2):
                u = _dot_tn(kz3[p][:, i * LANES:(i + 1) * LANES],
                            vr3_b[p][:, 2 * i * RET_V_DIM:(2 * i + 2) * RET_V_DIM])
                for half in range(2):
                    h = 2 * i + half
                    dsl = slice(h * RET_QK_DIM, (h + 1) * RET_QK_DIM)
                    sst_ref[2 * p + par, dsl, :] = (
                        _GL_SAMPLE[h] * st[p, dsl, :]
                        + u[half * RET_QK_DIM:(half + 1) * RET_QK_DIM, half * RET_V_DIM:(half + 1) * RET_V_DIM])

    own0 = row8 < DEC_SEQ
    att3 = jnp.where(own0, att_par[0], att_par[1])
    ret3 = jnp.where(own0, ret_par[0], ret_par[1])
    mix = jnp.concatenate([att3, ret3], axis=2).reshape(2 * nt * DEC_SEQ, MIX_OUT).astype(BF16)
    h_ref[...] = x + _dot(mix, wout_ref[...])


def _sample_mixer(x2d, g_mix, w_in, w_out, sinks, gn_g, gn_b, ck, cv, st):
    n, d = x2d.shape
    nb = ck.shape[0]
    bb = BB_MIX
    r = bb * DEC_SEQ
    const = lambda shape: pl.BlockSpec(shape, lambda i: (0,) * len(shape))
    row = pl.BlockSpec((r, d), lambda i: (i, 0))
    win = pl.BlockSpec((bb, WINDOW, ATT_KV_W), lambda i: (i, 0, 0))
    state = pl.BlockSpec((bb, RET_QK_W, RET_V_DIM), lambda i: (i, 0, 0))
    return pl.pallas_call(
        _sample_mixer_kernel,
        grid=(nb // bb,),
        in_specs=[
            pl.BlockSpec(memory_space=pltpu.SMEM),
            row, const((1, d)), const((d, D_IN)), const((MIX_OUT, d)),
            const((1, RET_V_W)), const((1, RET_V_W)),
            win, win, state,
            const(_S_BIAS.shape), const(_S_DEC.shape), const(_S_XI.shape), const(_S_ZETA.shape),
        ],
        out_specs=[row, win, win, state],
        out_shape=[
            jax.ShapeDtypeStruct((n, d), F32),
            jax.ShapeDtypeStruct((nb, WINDOW, ATT_KV_W), F32),
            jax.ShapeDtypeStruct((nb, WINDOW, ATT_KV_W), F32),
            jax.ShapeDtypeStruct((nb, RET_QK_W, RET_V_DIM), F32),
        ],
        compiler_params=pltpu.CompilerParams(
            dimension_semantics=("arbitrary",), vmem_limit_bytes=VMEM_LIMIT),
        name="sample_mixer",
    )(sinks, x2d, g_mix, w_in, w_out, gn_g, gn_b, ck, cv, st,
      jnp.asarray(_S_BIAS), jnp.asarray(_S_DEC), jnp.asarray(_S_XI), jnp.asarray(_S_ZETA))


def _head_slab(x_ref, b, hd):
    group = X_D_HALVES * N_X_HEADS
    halves = [x_ref[b, pl.ds(dh * N_X_HEADS + hd, N_MEM, stride=group), :] for dh in range(X_D_HALVES)]
    return jnp.concatenate(halves, axis=1).astype(BF16)


def _sample_xattn_kernel(h_ref, g_ref, wq_ref, wo_ref, xk_ref, xv_ref, out_ref):
    bb = xk_ref.shape[0]
    nt = bb // 2
    h = h_ref[...]
    xn = _rms(h, g_ref[...]).astype(BF16)
    q = _dot(xn, wq_ref[...]) * (X_HEAD_DIM ** -0.5)
    units = [(t, par, hd) for t in range(nt) for par in range(2) for hd in range(N_X_HEADS)]
    qts = [q[t * SUBLANES:(t + 1) * SUBLANES].astype(BF16) for t in range(nt)]
    s = jnp.concatenate(
        [_dot_nt(qts[t][:, hd * X_HEAD_DIM:(hd + 1) * X_HEAD_DIM], _head_slab(xk_ref, 2 * t + par, hd))
         for t, par, hd in units], axis=0)
    m = jnp.max(s, axis=-1, keepdims=True)
    p = jnp.exp(s - m)
    p = p * (1.0 / jnp.sum(p, axis=-1, keepdims=True))
    os_ = {}
    for i, (t, par, hd) in enumerate(units):
        pi = p[i * SUBLANES:(i + 1) * SUBLANES].astype(BF16)
        os_[(t, par, hd)] = _dot(pi, _head_slab(xv_ref, 2 * t + par, hd))
    own0 = lax.broadcasted_iota(jnp.int32, (SUBLANES, 1), 0) < DEC_SEQ
    o_tiles = []
    for t in range(nt):
        o_par = [jnp.concatenate([os_[(t, par, hd)] for hd in range(N_X_HEADS)], axis=1) for par in range(2)]
        o_tiles.append(jnp.where(own0, o_par[0], o_par[1]))
    o = jnp.concatenate(o_tiles, axis=0).astype(BF16)
    out_ref[...] = h + _dot(o, wo_ref[...])


def _mem_rows(c):
    nb = c.shape[0]
    c = c.reshape(nb, N_MEM, N_X_HEADS, X_D_HALVES, LANES)
    return jnp.transpose(c, (0, 1, 3, 2, 4)).reshape(nb, N_MEM * X_D_HALVES * N_X_HEADS, LANES)


def _sample_xattn(h2d, g, w_xq, w_xo, mk, mv):
    n, d = h2d.shape
    nb = mk.shape[0]
    bb = BB_X
    r = bb * DEC_SEQ
    const = lambda shape: pl.BlockSpec(shape, lambda i: (0,) * len(shape))
    row = pl.BlockSpec((r, d), lambda i: (i, 0))
    mem = pl.BlockSpec((bb,) + mk.shape[1:], lambda i: (i, 0, 0))
    return pl.pallas_call(
        _sample_xattn_kernel,
        grid=(nb // bb,),
        in_specs=[row, const((1, d)), const((d, d)), const((d, d)), mem, mem],
        out_specs=row,
        out_shape=jax.ShapeDtypeStruct((n, d), F32),
        compiler_params=pltpu.CompilerParams(
            dimension_semantics=("arbitrary",), vmem_limit_bytes=VMEM_LIMIT),
        name="sample_xattn",
    )(h2d, g, w_xq, w_xo, mk, mv)


def kernel(x_prompt, x_sample, mem_prompt, cache_win_k, cache_win_v, state_ret, cache_mem_k, cache_mem_v,
           g_mix, w_in, attn_sinks, ret_gn_g, ret_gn_b, w_out, g_xattn, g_mem, w_xq, w_xk, w_xv, w_xo,
           g_mlp, w_up, w_down, g_final):
    depth = w_in.shape[0]
    assert depth == 1, "single-layer trunk"
    b, s, d = x_prompt.shape
    nb, ls, _ = x_sample.shape
    row = lambda a: a.reshape(1, -1)
    bf = lambda a: a.astype(BF16)

    w_in_b, w_out_b = bf(w_in[0]), bf(w_out[0])
    w_xq_b, w_xk_b, w_xv_b, w_xo_b = bf(w_xq[0]), bf(w_xk[0]), bf(w_xv[0]), bf(w_xo[0])
    w_up_b, w_dn_b = bf(w_up[0]), bf(w_down[0])
    sinks = attn_sinks[0]
    gn_g, gn_b = row(ret_gn_g[0]), row(ret_gn_b[0])
    g_fin = row(g_final)

    hp, p_wk, p_wv, p_rs = _prompt_mixer_p(x_prompt, row(g_mix[0]), w_in_b, w_out_b, sinks, gn_g, gn_b)
    mk, mv, mkb, mvb = _memory_kv(mem_prompt.reshape(b * N_MEM, d), row(g_mem[0]), w_xk_b, w_xv_b)
    hp = _prompt_xattn(hp, row(g_xattn[0]), w_xq_b, w_xo_b,
                       mkb.reshape(b, N_MEM, d), mvb.reshape(b, N_MEM, d))

    hs, s_wk, s_wv, s_rs = _sample_mixer(
        x_sample.reshape(nb * ls, d), row(g_mix[0]), w_in_b, w_out_b, sinks, gn_g, gn_b,
        cache_win_k[0].reshape(nb, WINDOW, ATT_KV_W), cache_win_v[0].reshape(nb, WINDOW, ATT_KV_W),
        state_ret[0].reshape(nb, RET_QK_W, RET_V_DIM))
    hs = _sample_xattn(hs, row(g_xattn[0]), w_xq_b, w_xo_b,
                       _mem_rows(cache_mem_k[0]), _mem_rows(cache_mem_v[0]))

    y_prompt, y_sample = _mlp_final(hp.reshape(b * s, d), hs, row(g_mlp[0]), w_up_b, w_dn_b, g_fin)
    y_prompt = y_prompt.reshape(b, s, d)
    y_sample = y_sample.reshape(nb, ls, d)

    win5 = lambda a, n: a.reshape(1, n, WINDOW, N_KV_HEADS, HEAD_DIM)
    ret5 = lambda a, n: a.reshape(1, n, N_RET_HEADS, RET_QK_DIM, RET_V_DIM)
    mem5 = lambda a: jnp.transpose(a.reshape(b, N_MEM, X_D_HALVES, N_X_HEADS, LANES),
                                   (0, 1, 3, 2, 4)).reshape(1, b, N_MEM, N_X_HEADS, X_HEAD_DIM)
    return (y_prompt, y_sample,
            win5(p_wk, b), win5(p_wv, b), ret5(p_rs, b), mem5(mk), mem5(mv),
            win5(s_wk, nb), win5(s_wv, nb), ret5(s_rs, nb))
```

```python
import functools

import jax
import jax.numpy as jnp
import numpy as np
from jax import lax
from jax.experimental import pallas as pl
from jax.experimental.pallas import tpu as pltpu

F32 = jnp.float32
BF16 = jnp.bfloat16

D_MODEL = 1024
BATCH = 8
SEQ = 2048
DEC_BATCH = 128
DEC_SEQ = 4
HEAD_DIM = 64
N_ATT_HEADS = 8
N_KV_HEADS = 2
KV_GROUP = N_ATT_HEADS // N_KV_HEADS
WINDOW = 128
BLK = 128
N_RET_HEADS = 4
RET_QK_DIM = 64
RET_V_DIM = 128
N_MEM = 256
N_X_HEADS = 4
X_HEAD_DIM = D_MODEL // N_X_HEADS
D_FF = 4 * D_MODEL
RMS_EPS = 1e-6
GN_EPS = 1e-5

ATT_Q_W = N_ATT_HEADS * HEAD_DIM
ATT_KV_W = N_KV_HEADS * HEAD_DIM
RET_QK_W = N_RET_HEADS * RET_QK_DIM
RET_V_W = N_RET_HEADS * RET_V_DIM
MIX_OUT = ATT_Q_W + RET_V_W
D_IN = ATT_Q_W + 2 * ATT_KV_W + 2 * RET_QK_W + 2 * RET_V_W
C_QA, C_KV, C_QKR, C_VR, C_GR = 0, 512, 768, 1280, 1792

LANES = 128
SUBLANES = 8
HALF = LANES // 2
X_D_HALVES = X_HEAD_DIM // LANES
NEG = -1e30
VMEM_LIMIT = 56 * 1024 * 1024

TM_MIX = 512
TM_X = 1024
TM_MLP = 1024
SUB_ROWS = 512
FF_CHUNK = 1024
BB_MIX = 16
BB_X = 4

NEG_SLOPES = [-(2.0 ** (-8.0 * (i + 1) / N_ATT_HEADS)) for i in range(N_ATT_HEADS)]
_LOG_G = np.log(1.0 - 2.0 ** (-5.0 - np.arange(N_RET_HEADS))).astype(np.float32).astype(np.float64)


def _prompt_tables():
    qi = np.arange(BLK)[:, None]
    kj = np.arange(2 * BLK)[None, :]
    dist = (qi + BLK - kj).astype(np.float64)
    mask = np.where((dist >= 0) & (dist < WINDOW), 0.0, NEG)
    l = np.arange(BLK, dtype=np.float64)
    diff = l[:, None] - l[None, :]
    decay = np.where(diff >= 0, np.exp(_LOG_G[:, None, None] * np.maximum(diff, 0.0)), 0.0)
    xi = np.exp((l[:, None] + 1.0) * _LOG_G[None, :])
    zeta = np.exp((BLK - 1.0 - l)[:, None] * _LOG_G[None, :])
    xi_t = np.repeat(xi, RET_V_DIM, axis=1)
    zeta_t = np.repeat(zeta, RET_QK_DIM, axis=1)
    f = lambda a: np.asarray(a, np.float32)
    return f(dist), f(mask), f(decay), f(xi_t), f(zeta_t)


def _sample_tables():
    slopes = -np.asarray(NEG_SLOPES)
    bias = np.full((2, N_ATT_HEADS * SUBLANES, 2 * BLK), NEG, np.float64)
    dec = np.zeros((2, N_RET_HEADS * SUBLANES, BLK), np.float64)
    xi = np.zeros((2, N_RET_HEADS * SUBLANES, RET_V_DIM), np.float64)
    zeta = np.zeros((2, SUBLANES, RET_QK_W), np.float64)
    for par in range(2):
        for r in range(SUBLANES):
            own = DEC_SEQ * par <= r < DEC_SEQ * (par + 1)
            t = r - DEC_SEQ * par if own else r % DEC_SEQ
            for h in range(N_ATT_HEADS):
                row = h * SUBLANES + r
                for j in range(WINDOW):
                    d = t + WINDOW - j
                    if 0 <= d < WINDOW:
                        bias[par, row, j] = -slopes[h] * d
                for c in range(DEC_SEQ):
                    d = t - c
                    if d >= 0:
                        bias[par, row, WINDOW + DEC_SEQ * par + c] = -slopes[h] * d
            for h in range(N_RET_HEADS):
                row = h * SUBLANES + r
                if own:
                    xi[par, row, :] = np.exp((t + 1.0) * _LOG_G[h])
                    zeta[par, r, h * RET_QK_DIM:(h + 1) * RET_QK_DIM] = np.exp((DEC_SEQ - 1.0 - t) * _LOG_G[h])
                    for c in range(t + 1):
                        dec[par, row, DEC_SEQ * par + c] = np.exp(_LOG_G[h] * (t - c))
    f = lambda a: np.asarray(a, np.float32)
    return f(bias), f(dec), f(xi), f(zeta)


_P_DIST, _P_MASK, _P_DECAY, _P_XI, _P_ZETA = _prompt_tables()
_S_BIAS, _S_DEC, _S_XI, _S_ZETA = _sample_tables()
_GL_PROMPT = [float(np.exp(_LOG_G[h] * BLK)) for h in range(N_RET_HEADS)]
_GL_SAMPLE = [float(np.exp(_LOG_G[h] * DEC_SEQ)) for h in range(N_RET_HEADS)]


def _rms(x, g):
    return x * lax.rsqrt(jnp.mean(x * x, axis=-1, keepdims=True) + RMS_EPS) * g


def _dot(a, b):
    return jnp.dot(a, b, preferred_element_type=F32)


def _dot_nt(a, b):
    return lax.dot_general(a, b, (((1,), (1,)), ((), ())), preferred_element_type=F32)


def _dot_tn(a, b):
    return lax.dot_general(a, b, (((0,), (0,)), ((), ())), preferred_element_type=F32)


def _silu(g):
    return g * (1.0 / (1.0 + jnp.exp(-g)))


def _half_masks(width):
    lane = lax.broadcasted_iota(jnp.int32, (1, width), 1)
    lo = ((lane & (LANES - 1)) < HALF).astype(F32)
    return lo, 1.0 - lo


def _sink_softmax(s, sink):
    m = jnp.maximum(jnp.max(s, axis=-1, keepdims=True), sink)
    p = jnp.exp(s - m)
    den = jnp.sum(p, axis=-1, keepdims=True) + jnp.exp(sink - m)
    return p * (1.0 / den)


def _group_norm(o, g, b):
    mu = jnp.mean(o, axis=-1, keepdims=True)
    d = o - mu
    var = jnp.mean(d * d, axis=-1, keepdims=True)
    return d * lax.rsqrt(var + GN_EPS) * g + b


def _prompt_mixer_kernel(sinks_ref, x_ref, gmix_ref, win_ref, wout_ref, gng_ref, gnb_ref,
                         dist_ref, mask_ref, decay_ref, xi_ref, zeta_ref,
                         h_ref, wk_ref, wv_ref, st_ref,
                         qlo_s, qhi_s, kd0_s, kd1_s, vd0_s, vd1_s,
                         qrlo_s, qrhi_s, kr_s, vr_s, gate_s, mix_s, bias_s):
    t = pl.program_id(1)
    nt = pl.num_programs(1)
    tm = x_ref.shape[1]
    nblk = tm // BLK

    @pl.when(t == 0)
    def _():
        kd0_s[0:BLK, :] = jnp.zeros((BLK, LANES), BF16)
        kd1_s[0:BLK, :] = jnp.zeros((BLK, LANES), BF16)
        vd0_s[0:BLK, :] = jnp.zeros((BLK, 2 * LANES), BF16)
        vd1_s[0:BLK, :] = jnp.zeros((BLK, 2 * LANES), BF16)
        st_ref[...] = jnp.zeros_like(st_ref)

    @pl.when(t > 0)
    def _():
        kd0_s[0:BLK, :] = kd0_s[tm:tm + BLK, :]
        kd1_s[0:BLK, :] = kd1_s[tm:tm + BLK, :]
        vd0_s[0:BLK, :] = vd0_s[tm:tm + BLK, :]
        vd1_s[0:BLK, :] = vd1_s[tm:tm + BLK, :]

    @pl.when((t == 0) & (pl.program_id(0) == 0))
    def _():
        for h in range(N_ATT_HEADS):
            bias_s[h] = NEG_SLOPES[h] * dist_ref[...] + mask_ref[...]

    x = x_ref[0]
    xn = _rms(x, gmix_ref[...]).astype(BF16)

    z = _dot(xn, win_ref[...])
    lo512, hi512 = _half_masks(ATT_Q_W)
    q = z[:, C_QA:C_QA + ATT_Q_W]
    qlo_s[...] = (q * (lo512 * HEAD_DIM ** -0.5)).astype(BF16)
    qhi_s[...] = (q * (hi512 * HEAD_DIM ** -0.5)).astype(BF16)

    low = lax.broadcasted_iota(jnp.int32, (tm, LANES), 1) < HALF
    kv = z[:, C_KV:C_KV + 2 * ATT_KV_W]
    k = kv[:, :ATT_KV_W]
    v = kv[:, ATT_KV_W:]
    k_r = pltpu.roll(k, HALF, axis=1)
    v_r = pltpu.roll(v, HALF, axis=1)
    kd0_s[BLK:BLK + tm, :] = jnp.where(low, k, k_r).astype(BF16)
    kd1_s[BLK:BLK + tm, :] = jnp.where(low, k_r, k).astype(BF16)
    vd0_s[BLK:BLK + tm, 0:LANES] = jnp.where(low, v, 1.0).astype(BF16)
    vd0_s[BLK:BLK + tm, LANES:2 * LANES] = jnp.where(low, 1.0, v_r).astype(BF16)
    vd1_s[BLK:BLK + tm, 0:LANES] = jnp.where(low, v_r, 1.0).astype(BF16)
    vd1_s[BLK:BLK + tm, LANES:2 * LANES] = jnp.where(low, 1.0, v).astype(BF16)

    @pl.when(t == nt - 1)
    def _():
        wk_ref[0] = k[tm - WINDOW:, :]
        wv_ref[0] = v[tm - WINDOW:, :]

    lo256, hi256 = _half_masks(RET_QK_W)
    qkr = z[:, C_QKR:C_QKR + 2 * RET_QK_W]
    qr = qkr[:, :RET_QK_W]
    qrlo_s[...] = (qr * lo256).astype(BF16)
    qrhi_s[...] = (qr * hi256).astype(BF16)
    kr_s[...] = qkr[:, RET_QK_W:] * (RET_QK_DIM ** -0.5)
    vr_s[...] = z[:, C_VR:C_VR + RET_V_W].astype(BF16)
    gate_s[...] = _silu(z[:, C_GR:C_GR + RET_V_W])

    lowb =lax.broadcasted_iota(jnp.int32, (BLK, LANES), 1) < HALF
    col = lax.broadcasted_iota(jnp.int32, (BLK, 2 * BLK), 1)
    first_mask = jnp.where((col < BLK) & (t == 0), NEG, 0.0)
    kd_refs = (kd0_s, kd1_s)
    vd_refs = (vd0_s, vd1_s)
    n_pairs = N_RET_HEADS // 2
    state = [st_ref[0, i * LANES:(i + 1) * LANES, :] for i in range(n_pairs)]

    for j in range(nblk):
        rows = slice(j * BLK, (j + 1) * BLK)
        krows = slice(j * BLK, (j + 2) * BLK)

        for kvh in range(N_KV_HEADS):
            kd = kd_refs[kvh][krows, :]
            vd = vd_refs[kvh][krows, :]
            c0 = kvh * KV_GROUP * HEAD_DIM
            qst = jnp.concatenate([qlo_s[rows, c0:c0 + LANES], qhi_s[rows, c0:c0 + LANES],
                                   qlo_s[rows, c0 + LANES:c0 + 2 * LANES],
                                   qhi_s[rows, c0 + LANES:c0 + 2 * LANES]], axis=0)
            s = _dot_nt(qst, kd)
            es, esink = [], []
            for g in range(KV_GROUP):
                h = kvh * KV_GROUP + g
                sg = s[g * BLK:(g + 1) * BLK] + bias_s[h]
                if j == 0:
                    sg = sg + first_mask
                sink = sinks_ref[h]
                m = jnp.maximum(jnp.max(sg, axis=-1, keepdims=True), sink)
                es.append(jnp.exp(sg - m).astype(BF16))
                esink.append(jnp.exp(sink - m))
            o = _dot(jnp.concatenate(es, axis=0), vd)
            for pair in range(KV_GROUP // 2):
                oe = o[2 * pair * BLK:(2 * pair + 1) * BLK]
                oo = o[(2 * pair + 1) * BLK:(2 * pair + 2) * BLK]
                num = jnp.where(lowb, oe[:, :LANES], oo[:, LANES:])
                den = (jnp.where(lowb, oe[:, LANES:], oo[:, :LANES])
                       + jnp.where(lowb, esink[2 * pair], esink[2 * pair + 1]))
                cs = c0 + pair * LANES
                mix_s[rows, cs:cs + LANES] = (num * (1.0 / den)).astype(BF16)

        for i in range(n_pairs):
            lsl = slice(i * LANES, (i + 1) * LANES)
            kp = kr_s[rows, lsl]
            sp = state[i]
            vpair = vr_s[rows, 2 * i * RET_V_DIM:(2 * i + 2) * RET_V_DIM]
            q2 = jnp.concatenate([qrlo_s[rows, lsl], qrhi_s[rows, lsl]], axis=0)
            a = _dot_nt(q2, kp.astype(BF16))
            inner = jnp.concatenate([a[:BLK] * decay_ref[2 * i], a[BLK:] * decay_ref[2 * i + 1]], axis=0)
            oi = _dot(inner.astype(BF16), vpair)
            oc = _dot(q2, sp.astype(BF16))
            for half in range(2):
                h = 2 * i + half
                vsl = slice(h * RET_V_DIM, (h + 1) * RET_V_DIM)
                hr = slice(half * BLK, (half + 1) * BLK)
                o = oi[hr, half * RET_V_DIM:(half + 1) * RET_V_DIM] + oc[hr] * xi_ref[:, vsl]
                r = _group_norm(o, gng_ref[:, vsl], gnb_ref[:, vsl]) * gate_s[rows, vsl]
                mix_s[rows, ATT_Q_W + h * RET_V_DIM:ATT_Q_W + (h + 1) * RET_V_DIM] = r.astype(BF16)
            kz = (kp * zeta_ref[:, lsl]).astype(BF16)
            u = _dot_tn(kz, vpair)
            state[i] = jnp.concatenate(
                [_GL_PROMPT[2 * i] * sp[:RET_QK_DIM] + u[:RET_QK_DIM, :RET_V_DIM],
                 _GL_PROMPT[2 * i + 1] * sp[RET_QK_DIM:] + u[RET_QK_DIM:, RET_V_DIM:]], axis=0)

    for i in range(n_pairs):
        st_ref[0, i * LANES:(i + 1) * LANES, :] = state[i]

    h_ref[0] = x + _dot(mix_s[...], wout_ref[...])


def _prompt_mixer(x, g_mix, w_in, w_out, sinks, gn_g, gn_b):
    b, s, d = x.shape
    tm = TM_MIX
    const = lambda shape: pl.BlockSpec(shape, lambda i, j: (0,) * len(shape))
    return pl.pallas_call(
        _prompt_mixer_kernel,
        grid=(b, s // tm),
        in_specs=[
            pl.BlockSpec(memory_space=pltpu.SMEM),
            pl.BlockSpec((1, tm, d), lambda i, j: (i, j, 0)),
            const((1, d)), const((d, D_IN)), const((MIX_OUT, d)),
            const((1, RET_V_W)), const((1, RET_V_W)),
            const((BLK, 2 * BLK)), const((BLK, 2 * BLK)),
            const((N_RET_HEADS, BLK, BLK)), const((BLK, RET_V_W)), const((BLK, RET_QK_W)),
        ],
        out_specs=[
            pl.BlockSpec((1, tm, d), lambda i, j: (i, j, 0)),
            pl.BlockSpec((1, WINDOW, ATT_KV_W), lambda i, j: (i, 0, 0)),
            pl.BlockSpec((1, WINDOW, ATT_KV_W), lambda i, j: (i, 0, 0)),
            pl.BlockSpec((1, RET_QK_W, RET_V_DIM), lambda i, j: (i, 0, 0)),
        ],
        out_shape=[
            jax.ShapeDtypeStruct((b, s, d), F32),
            jax.ShapeDtypeStruct((b, WINDOW, ATT_KV_W), F32),
            jax.ShapeDtypeStruct((b, WINDOW, ATT_KV_W), F32),
            jax.ShapeDtypeStruct((b, RET_QK_W, RET_V_DIM), F32),
        ],
        scratch_shapes=[
            pltpu.VMEM((tm, ATT_Q_W), BF16), pltpu.VMEM((tm, ATT_Q_W), BF16),
            pltpu.VMEM((tm + BLK, LANES), BF16), pltpu.VMEM((tm + BLK, LANES), BF16),
            pltpu.VMEM((tm + BLK, 2 * LANES), BF16), pltpu.VMEM((tm + BLK, 2 * LANES), BF16),
            pltpu.VMEM((tm, RET_QK_W), BF16), pltpu.VMEM((tm, RET_QK_W), BF16),
            pltpu.VMEM((tm, RET_QK_W), F32), pltpu.VMEM((tm, RET_V_W), BF16),
            pltpu.VMEM((tm, RET_V_W), F32), pltpu.VMEM((tm, MIX_OUT), BF16),
            pltpu.VMEM((N_ATT_HEADS, BLK, 2 * BLK), F32),
        ],
        compiler_params=pltpu.CompilerParams(
            dimension_semantics=("arbitrary", "arbitrary"), vmem_limit_bytes=VMEM_LIMIT),
        name="prompt_mixer",
    )(sinks, x, g_mix, w_in, w_out, gn_g, gn_b,
      jnp.asarray(_P_DIST), jnp.asarray(_P_MASK), jnp.asarray(_P_DECAY), jnp.asarray(_P_XI),
      jnp.asarray(_P_ZETA))


def _pm_project_stages(x, slot, gmix_ref, win_ref, sc, kv_out=None):
    tm = x.shape[0]
    xn = _rms(x, gmix_ref[...]).astype(BF16)

    def stage_q():
        q = _dot(xn, win_ref[:, C_QA:C_QA + ATT_Q_W])
        lo512, hi512 = _half_masks(ATT_Q_W)
        sc["qlo"][slot] = (q * (lo512 * HEAD_DIM ** -0.5)).astype(BF16)
        sc["qhi"][slot] = (q * (hi512 * HEAD_DIM ** -0.5)).astype(BF16)

    def stage_kv_qkr():
        z = _dot(xn, win_ref[:, C_KV:C_VR])
        low = lax.broadcasted_iota(jnp.int32, (tm, LANES), 1) < HALF
        k = z[:, 0:ATT_KV_W]
        v = z[:, ATT_KV_W:2 * ATT_KV_W]
        if kv_out is not None:
            kv_out[0][0] = k[tm - WINDOW:, :]
            kv_out[1][0] = v[tm - WINDOW:, :]
        k_r = pltpu.roll(k, HALF, axis=1)
        v_r = pltpu.roll(v, HALF, axis=1)
        sc["kd0"][slot] = jnp.where(low, k, k_r).astype(BF16)
        sc["kd1"][slot] = jnp.where(low, k_r, k).astype(BF16)
        sc["vd0"][slot, :, 0:LANES] = jnp.where(low, v, 1.0).astype(BF16)
        sc["vd0"][slot, :, LANES:2 * LANES] = jnp.where(low, 1.0, v_r).astype(BF16)
        sc["vd1"][slot, :, 0:LANES] = jnp.where(low, v_r, 1.0).astype(BF16)
        sc["vd1"][slot, :, LANES:2 * LANES] = jnp.where(low, 1.0, v).astype(BF16)
        lo256, hi256 = _half_masks(RET_QK_W)
        qr = z[:, 2 * ATT_KV_W:2 * ATT_KV_W + RET_QK_W]
        sc["qrlo"][slot] = (qr * lo256).astype(BF16)
        sc["qrhi"][slot] = (qr * hi256).astype(BF16)
        sc["kr"][slot] = z[:, 2 * ATT_KV_W + RET_QK_W:] * (RET_QK_DIM ** -0.5)

    def stage_vr():
        sc["vr"][slot] = _dot(xn, win_ref[:, C_VR:C_VR + RET_V_W]).astype(BF16)

    def stage_gate():
        sc["gate"][slot] = _silu(_dot(xn, win_ref[:, C_GR:C_GR + RET_V_W]))

    return [stage_q, stage_kv_qkr, stage_vr, stage_gate]


def _pm_last_block(slot, tm, sc):
    rows = slice(tm - BLK, tm)
    return ([sc["kd0"][slot, rows, :], sc["kd1"][slot, rows, :]],
            [sc["vd0"][slot, rows, :], sc["vd1"][slot, rows, :]])


def _pm_blocks(slot, prev_kd, prev_vd, is_first, state, x, fillers, tm, sinks_ref, wout_ref, gng_ref, gnb_ref,
               decay_ref, xi_ref, zeta_ref, sc):
    nblk = tm // BLK
    lowb = lax.broadcasted_iota(jnp.int32, (BLK, LANES), 1) < HALF
    col = lax.broadcasted_iota(jnp.int32, (BLK, 2 * BLK), 1)
    first_mask = None if is_first is False else jnp.where((col < BLK) & is_first, NEG, 0.0)
    kd_refs = (sc["kd0"], sc["kd1"])
    vd_refs = (sc["vd0"], sc["vd1"])
    qlo, qhi, mix = sc["qlo"], sc["qhi"], sc["mix"]
    n_pairs = N_RET_HEADS // 2

    for j in range(nblk):
        rows = slice(j * BLK, (j + 1) * BLK)
        for kvh in range(N_KV_HEADS):
            if j == 0:
                kd = jnp.concatenate([prev_kd[kvh], kd_refs[kvh][slot, rows, :]], axis=0)
                vd = jnp.concatenate([prev_vd[kvh], vd_refs[kvh][slot, rows, :]], axis=0)
            else:
                krows = slice((j - 1) * BLK, (j + 1) * BLK)
                kd = kd_refs[kvh][slot, krows, :]
                vd = vd_refs[kvh][slot, krows, :]
            c0 = kvh * KV_GROUP * HEAD_DIM
            qst = jnp.concatenate([qlo[slot, rows, c0:c0 + LANES], qhi[slot, rows, c0:c0 + LANES],
                                   qlo[slot, rows, c0 + LANES:c0 + 2 * LANES],
                                   qhi[slot, rows, c0 + LANES:c0 + 2 * LANES]], axis=0)
            s = _dot_nt(qst, kd)
            es, esink = [], []
            for g in range(KV_GROUP):
                h = kvh * KV_GROUP + g
                sg = s[g * BLK:(g + 1) * BLK] + sc["bias"][h]
                if j == 0 and first_mask is not None:
                    sg = sg + first_mask
                sink = sinks_ref[h]
                m = jnp.maximum(jnp.max(sg, axis=-1, keepdims=True), sink)
                es.append(jnp.exp(sg - m).astype(BF16))
                esink.append(jnp.exp(sink - m))
            o = _dot(jnp.concatenate(es, axis=0), vd)
            for pair in range(KV_GROUP // 2):
                oe = o[2 * pair * BLK:(2 * pair + 1) * BLK]
                oo = o[(2 * pair + 1) * BLK:(2 * pair + 2) * BLK]
                num = jnp.where(lowb, oe[:, :LANES], oo[:, LANES:])
                den = (jnp.where(lowb, oe[:, LANES:], oo[:, :LANES])
                       + jnp.where(lowb, esink[2 * pair], esink[2 * pair + 1]))
                cs = c0 + pair * LANES
                mix[slot, rows, cs:cs + LANES] = (num * (1.0 / den)).astype(BF16)

        for i in range(n_pairs):
            lsl = slice(i * LANES, (i + 1) * LANES)
            kp = sc["kr"][slot, rows, lsl]
            sp = state[i]
            vpair = sc["vr"][slot, rows, 2 * i * RET_V_DIM:(2 * i + 2) * RET_V_DIM]
            q2 = jnp.concatenate([sc["qrlo"][slot, rows, lsl], sc["qrhi"][slot, rows, lsl]], axis=0)
            a = _dot_nt(q2, kp.astype(BF16))
            inner = jnp.concatenate([a[:BLK] * decay_ref[2 * i], a[BLK:] * decay_ref[2 * i + 1]], axis=0)
            oi = _dot(inner.astype(BF16), vpair)
            oc = _dot(q2, sp.astype(BF16))
            for half in range(2):
                h = 2 * i + half
                vsl = slice(h * RET_V_DIM, (h + 1) * RET_V_DIM)
                hr = slice(half * BLK, (half + 1) * BLK)
                o = oi[hr, half * RET_V_DIM:(half + 1) * RET_V_DIM] + oc[hr] * xi_ref[:, vsl]
                r = _group_norm(o, gng_ref[:, vsl], gnb_ref[:, vsl]) * sc["gate"][slot, rows, vsl]
                mix[slot, rows, ATT_Q_W + h * RET_V_DIM:ATT_Q_W + (h + 1) * RET_V_DIM] = r.astype(BF16)
            kz = (kp * zeta_ref[:, lsl]).astype(BF16)
            u = _dot_tn(kz, vpair)
            state[i] = jnp.concatenate(
                [_GL_PROMPT[2 * i] * sp[:RET_QK_DIM] + u[:RET_QK_DIM, :RET_V_DIM],
                 _GL_PROMPT[2 * i + 1] * sp[RET_QK_DIM:] + u[RET_QK_DIM:, RET_V_DIM:]], axis=0)

        fillers[j]()

    return x + _dot(mix[slot], wout_ref[...]), state


def _prompt_mixer_kernel_p(sinks_ref, xpair_ref, xnext_ref, gmix_ref, win_ref, wout_ref, gng_ref, gnb_ref,
                           dist_ref, mask_ref, decay_ref, xi_ref, zeta_ref,
                           h_ref, wk_ref, wv_ref, st_ref,
                           qlo_s, qhi_s, kd0_s, kd1_s, vd0_s, vd1_s,
                           qrlo_s, qrhi_s, kr_s, vr_s, gate_s, mix_s, bias_s, state_s):
    u = pl.program_id(0)
    tm = xnext_ref.shape[0]
    sc = dict(qlo=qlo_s, qhi=qhi_s, kd0=kd0_s, kd1=kd1_s, vd0=vd0_s, vd1=vd1_s, qrlo=qrlo_s, qrhi=qrhi_s,
              kr=kr_s, vr=vr_s, gate=gate_s, mix=mix_s, bias=bias_s)
    n_pairs = N_RET_HEADS // 2
    blocks = functools.partial(_pm_blocks, tm=tm, sinks_ref=sinks_ref, wout_ref=wout_ref, gng_ref=gng_ref,
                               gnb_ref=gnb_ref, decay_ref=decay_ref, xi_ref=xi_ref, zeta_ref=zeta_ref, sc=sc)

    @pl.when(u == 0)
    def _():
        for h in range(N_ATT_HEADS):
            bias_s[h] = NEG_SLOPES[h] * dist_ref[...] + mask_ref[...]
        state_s[...] = jnp.zeros_like(state_s)
        kd0_s[1] = jnp.zeros(kd0_s.shape[1:], BF16)
        kd1_s[1] = jnp.zeros(kd1_s.shape[1:], BF16)
        vd0_s[1] = jnp.zeros(vd0_s.shape[1:], BF16)
        vd1_s[1] = jnp.zeros(vd1_s.shape[1:], BF16)
        for stage in _pm_project_stages(xpair_ref[0:tm, :], 0, gmix_ref, win_ref, sc):
            stage()

    seq_start = (u % 2) == 0
    state = [jnp.where(seq_start, 0.0, state_s[i * LANES:(i + 1) * LANES, :]) for i in range(n_pairs)]

    prev_kd, prev_vd = _pm_last_block(1, tm, sc)
    stages = _pm_project_stages(xpair_ref[tm:2 * tm, :], 1, gmix_ref, win_ref, sc, kv_out=(wk_ref, wv_ref))
    h0, state = blocks(0, prev_kd, prev_vd, seq_start, state, xpair_ref[0:tm, :], stages)
    h_ref[0:tm, :] = h0

    prev_kd, prev_vd = _pm_last_block(0, tm, sc)
    stages = _pm_project_stages(xnext_ref[...], 0, gmix_ref, win_ref, sc)
    h1, state = blocks(1, prev_kd, prev_vd, False, state, xpair_ref[tm:2 * tm, :], stages)
    h_ref[tm:2 * tm, :] = h1

    for i in range(n_pairs):
        state_s[i * LANES:(i + 1) * LANES, :] = state[i]
        st_ref[0, i * LANES:(i + 1) * LANES, :] = state[i]


def _prompt_mixer_p(x, g_mix, w_in, w_out, sinks, gn_g, gn_b):
    b, s, d = x.shape
    tm = TM_MIX
    n_tiles = b * s // tm
    steps = n_tiles // 2
    seq_steps = s // (2 * tm)
    assert s % (2 * tm) == 0 and seq_steps == 2, "kernel assumes 4 tiles per sequence"
    x2d = x.reshape(b * s, d)
    const = lambda shape: pl.BlockSpec(shape, lambda i: (0,) * len(shape), pipeline_mode=pl.Buffered(1))
    slot2 = lambda rows, cols, dt: pltpu.VMEM((2, rows, cols), dt)
    outs = pl.pallas_call(
        _prompt_mixer_kernel_p,
        grid=(steps,),
        in_specs=[
            pl.BlockSpec(memory_space=pltpu.SMEM),
            pl.BlockSpec((2 * tm, d), lambda i: (i, 0)),
            pl.BlockSpec((tm, d), lambda i: (jnp.minimum(2 * i + 2, n_tiles - 1), 0)),
            const((1, d)), const((d, D_IN)), const((MIX_OUT, d)),
            const((1, RET_V_W)), const((1, RET_V_W)),
            const((BLK, 2 * BLK)), const((BLK, 2 * BLK)),
            const((N_RET_HEADS, BLK, BLK)), const((BLK, RET_V_W)), const((BLK, RET_QK_W)),
        ],
        out_specs=[
            pl.BlockSpec((2 * tm, d), lambda i: (i, 0)),
            pl.BlockSpec((1, WINDOW, ATT_KV_W), lambda i: (i // seq_steps, 0, 0)),
            pl.BlockSpec((1, WINDOW, ATT_KV_W), lambda i: (i // seq_steps, 0, 0)),
            pl.BlockSpec((1, RET_QK_W, RET_V_DIM), lambda i: (i // seq_steps, 0, 0)),
        ],
        out_shape=[
            jax.ShapeDtypeStruct((b * s, d), F32),
            jax.ShapeDtypeStruct((b, WINDOW, ATT_KV_W), F32),
            jax.ShapeDtypeStruct((b, WINDOW, ATT_KV_W), F32),
            jax.ShapeDtypeStruct((b, RET_QK_W, RET_V_DIM), F32),
        ],
        scratch_shapes=[
            slot2(tm, ATT_Q_W, BF16), slot2(tm, ATT_Q_W, BF16),
            slot2(tm, LANES, BF16), slot2(tm, LANES, BF16),
            slot2(tm, 2 * LANES, BF16), slot2(tm, 2 * LANES, BF16),
            slot2(tm, RET_QK_W, BF16), slot2(tm, RET_QK_W, BF16),
            slot2(tm, RET_QK_W, F32), slot2(tm, RET_V_W, BF16),
            slot2(tm, RET_V_W, F32), slot2(tm, MIX_OUT, BF16),
            pltpu.VMEM((N_ATT_HEADS, BLK, 2 * BLK), F32),
            pltpu.VMEM((RET_QK_W, RET_V_DIM), F32),
        ],
        compiler_params=pltpu.CompilerParams(
            dimension_semantics=("arbitrary",), vmem_limit_bytes=VMEM_LIMIT),
        name="prompt_mixer",
    )(sinks, x2d, x2d, g_mix, w_in, w_out, gn_g, gn_b,
      jnp.asarray(_P_DIST), jnp.asarray(_P_MASK), jnp.asarray(_P_DECAY), jnp.asarray(_P_XI),
      jnp.asarray(_P_ZETA))
    return (outs[0].reshape(b, s, d),) + tuple(outs[1:])


def _memkv_kernel(mem_ref, g_ref, wk_ref, wv_ref, mk_ref, mv_ref, mkb_ref, mvb_ref):
    mn = _rms(mem_ref[...], g_ref[...]).astype(BF16)
    mk = _dot(mn, wk_ref[...])
    mv = _dot(mn, wv_ref[...])
    tm = mem_ref.shape[0]
    group = X_D_HALVES * N_X_HEADS
    for hd in range(N_X_HEADS):
        for dh in range(X_D_HALVES):
            cols = slice(hd * X_HEAD_DIM + dh * LANES, hd * X_HEAD_DIM + (dh + 1) * LANES)
            rows = pl.ds(dh * N_X_HEADS + hd, tm, stride=group)
            mk_ref[rows, :] = mk[:, cols]
            mv_ref[rows, :] = mv[:, cols]
    mkb_ref[...] = mk.astype(BF16)
    mvb_ref[...] = mv.astype(BF16)


def _memory_kv(mem2d, g_mem, w_xk, w_xv):
    n, d = mem2d.shape
    tm = 512
    row = pl.BlockSpec((tm, d), lambda i: (i, 0))
    rows_out = pl.BlockSpec((tm * d // LANES, LANES), lambda i: (i, 0))
    const = lambda shape: pl.BlockSpec(shape, lambda i: (0,) * len(shape))
    return pl.pallas_call(
        _memkv_kernel,
        grid=(n // tm,),
        in_specs=[row, const((1, d)), const((d, d)), const((d, d))],
        out_specs=[rows_out, rows_out, row, row],
        out_shape=[jax.ShapeDtypeStruct((n * d // LANES, LANES), F32),
                   jax.ShapeDtypeStruct((n * d // LANES, LANES), F32),
                   jax.ShapeDtypeStruct((n, d), BF16), jax.ShapeDtypeStruct((n, d), BF16)],
        compiler_params=pltpu.CompilerParams(
            dimension_semantics=("arbitrary",), vmem_limit_bytes=VMEM_LIMIT),
        name="memory_kv",
    )(mem2d, g_mem, w_xk, w_xv)


def _prompt_xattn_kernel(h_ref, g_ref, wq_ref, wo_ref, mk_ref, mv_ref, out_ref, o_s):
    for r0 in range(0, h_ref.shape[1], SUB_ROWS):
        rows = slice(r0, r0 + SUB_ROWS)
        h = h_ref[0, rows, :]
        xn = _rms(h, g_ref[...]).astype(BF16)
        q = (_dot(xn, wq_ref[...]) * (X_HEAD_DIM ** -0.5)).astype(BF16)
        for hd in range(N_X_HEADS):
            sl = slice(hd * X_HEAD_DIM, (hd + 1) * X_HEAD_DIM)
            s = _dot_nt(q[:, sl], mk_ref[0, :, sl])
            m = jnp.max(s, axis=-1, keepdims=True)
            p = jnp.exp(s - m)
            p = p * (1.0 / jnp.sum(p, axis=-1, keepdims=True))
            o_s[rows, sl] = _dot(p.astype(BF16), mv_ref[0, :, sl]).astype(BF16)
        out_ref[0, rows, :] = h + _dot(o_s[rows, :], wo_ref[...])


def _prompt_xattn(h, g, w_xq, w_xo, mkb, mvb):
    b, s, d = h.shape
    tm = TM_X
    const = lambda shape: pl.BlockSpec(shape, lambda i, j: (0,) * len(shape))
    tok = pl.BlockSpec((1, tm, d), lambda i, j: (i, j, 0))
    mem = pl.BlockSpec((1, N_MEM, d), lambda i, j: (i, 0, 0))
    return pl.pallas_call(
        _prompt_xattn_kernel,
        grid=(b, s // tm),
        in_specs=[tok, const((1, d)), const((d, d)), const((d, d)), mem, mem],
        out_specs=tok,
        out_shape=jax.ShapeDtypeStruct((b, s, d), F32),
        scratch_shapes=[pltpu.VMEM((tm, d), BF16)],
        compiler_params=pltpu.CompilerParams(
            dimension_semantics=("arbitrary", "arbitrary"), vmem_limit_bytes=VMEM_LIMIT),
        name="prompt_xattn",
    )(h, g, w_xq, w_xo, mkb, mvb)


def _mlp_rows(h_ref, y_ref, g_ref, wup_ref, wdn_ref, gf_ref):
    for r0 in range(0, h_ref.shape[0], SUB_ROWS):
        rows = slice(r0, r0 + SUB_ROWS)
        h = h_ref[rows, :]
        xn = _rms(h, g_ref[...]).astype(BF16)
        acc = h
        for c in range(D_FF // FF_CHUNK):
            sl = slice(c * FF_CHUNK, (c + 1) * FF_CHUNK)
            u = jnp.maximum(_dot(xn, wup_ref[:, sl]), 0.0)
            acc = acc + _dot((u * u).astype(BF16), wdn_ref[sl, :])
        y_ref[rows, :] = _rms(acc, gf_ref[...])


def _mlp_kernel(hp_ref, hs_ref, g_ref, wup_ref, wdn_ref, gf_ref, yp_ref, ys_ref):
    last = pl.num_programs(0) - 1

    @pl.when(pl.program_id(0) < last)
    def _():
        _mlp_rows(hp_ref, yp_ref, g_ref, wup_ref, wdn_ref, gf_ref)

    @pl.when(pl.program_id(0) == last)
    def _():
        _mlp_rows(hs_ref, ys_ref, g_ref, wup_ref, wdn_ref, gf_ref)


def _mlp_final(hp2d, hs2d, g_mlp, w_up, w_down, g_final):
    n, d = hp2d.shape
    ns = hs2d.shape[0]
    tm = TM_MLP
    n_tiles = n // tm
    prompt = pl.BlockSpec((tm, d), lambda i: (jnp.minimum(i, n_tiles - 1), 0))
    sample = pl.BlockSpec((ns, d), lambda i: (0, 0))
    const = lambda shape: pl.BlockSpec(shape, lambda i: (0,) * len(shape), pipeline_mode=pl.Buffered(1))
    return pl.pallas_call(
        _mlp_kernel,
        grid=(n_tiles + 1,),
        in_specs=[prompt, sample, const((1, d)), const((d, D_FF)), const((D_FF, d)), const((1, d))],
        out_specs=[prompt, sample],
        out_shape=[jax.ShapeDtypeStruct((n, d), F32), jax.ShapeDtypeStruct((ns, d), F32)],
        compiler_params=pltpu.CompilerParams(
            dimension_semantics=("arbitrary",), vmem_limit_bytes=VMEM_LIMIT),
        name="mlp_final",
    )(hp2d, hs2d, g_mlp, w_up, w_down, g_final)


def _sample_mixer_kernel(sinks_ref, x_ref, gmix_ref, win_ref, wout_ref, gng_ref, gnb_ref,
                         ck_ref, cv_ref, st_ref, bias_ref, dec_ref, xi_ref, zeta_ref,
                         h_ref, swk_ref, swv_ref, sst_ref):
    bb = ck_ref.shape[0]
    nt = bb // 2
    x = x_ref[...]
    xn = _rms(x, gmix_ref[...]).astype(BF16)
    tile3 = lambda a: a.reshape(nt, SUBLANES, a.shape[-1])

    q = _dot(xn, win_ref[:, C_QA:C_QA + ATT_Q_W]) * (HEAD_DIM ** -0.5)
    kv = _dot(xn, win_ref[:, C_KV:C_KV + 2 * ATT_KV_W])
    qkr = _dot(xn, win_ref[:, C_QKR:C_QKR + 2 * RET_QK_W])
    vr = _dot(xn, win_ref[:, C_VR:C_VR + RET_V_W])
    gate3 = tile3(_silu(_dot(xn, win_ref[:, C_GR:C_GR + RET_V_W])))

    lo512, hi512 = _half_masks(ATT_Q_W)
    q_r = pltpu.roll(q, HALF, axis=1)
    q_nat3 = tile3(q)
    q_rot3 = tile3(q_r)
    lo3 = lo512.reshape(1, 1, ATT_Q_W)
    hi3 = hi512.reshape(1, 1, ATT_Q_W)
    qa3 = (q_nat3 * lo3).astype(BF16)
    qb3 = (q_rot3 * lo3).astype(BF16)
    qc3 = (q_rot3 * hi3).astype(BF16)
    qd3 = (q_nat3 * hi3).astype(BF16)
    t128 = lambda a, i: a[:, :, i * LANES:(i + 1) * LANES]
    qs = jnp.concatenate([t128(qa3, 0), t128(qb3, 1), t128(qa3, 1), t128(qb3, 2),
                          t128(qc3, 2), t128(qd3, 2), t128(qc3, 3), t128(qd3, 3)], axis=1)

    k3 = tile3(kv[:, :ATT_KV_W])
    v3 = tile3(kv[:, ATT_KV_W:])
    pad_kv = jnp.zeros((nt, BLK - SUBLANES, LANES), BF16)

    lo256, _ = _half_masks(RET_QK_W)
    qr3 = tile3(qkr[:, :RET_QK_W])
    kr3 = tile3(qkr[:, RET_QK_W:] * (RET_QK_DIM ** -0.5))
    vr3 = tile3(vr)
    lane256 = lax.broadcasted_iota(jnp.int32, (1, 1, RET_QK_W), 2)
    qrs = jnp.concatenate(
        [(qr3 * ((lane256 >= h * RET_QK_DIM) & (lane256 < (h + 1) * RET_QK_DIM)).astype(F32)).astype(BF16)
         for h in range(N_RET_HEADS)],
        axis=1)
    kr_pad = jnp.concatenate([kr3.astype(BF16), jnp.zeros((nt, BLK - SUBLANES, RET_QK_W), BF16)], axis=1)
    vr_pad = jnp.concatenate([vr3.astype(BF16), jnp.zeros((nt, BLK - SUBLANES, RET_V_W), BF16)], axis=1)

    lane = lax.broadcasted_iota(jnp.int32, (1, 1, LANES), 2)
    row8 = lax.broadcasted_iota(jnp.int32, (1, SUBLANES, 1), 1)
    bmm_nt = lambda a, b: jnp.einsum('bqd,bkd->bqk', a, b, preferred_element_type=F32)
    bmm = lambda a, b: jnp.einsum('bqk,bkd->bqd', a, b, preferred_element_type=F32)

    att_par, ret_par = [], []
    for par in range(2):
        bsl = pl.ds(par, nt, stride=2)
        ck = ck_ref[bsl]
        cv = cv_ref[bsl]
        swk_ref[bsl, 0:WINDOW - DEC_SEQ, :] = ck[:, DEC_SEQ:, :]
        swv_ref[bsl, 0:WINDOW - DEC_SEQ, :] = cv[:, DEC_SEQ:, :]
        swk_ref[bsl, WINDOW - DEC_SEQ:WINDOW, :] = k3[:, DEC_SEQ * par:DEC_SEQ * (par + 1), :]
        swv_ref[bsl, WINDOW - DEC_SEQ:WINDOW, :] = v3[:, DEC_SEQ * par:DEC_SEQ * (par + 1), :]

        kfull = jnp.concatenate([ck.astype(BF16), k3.astype(BF16), pad_kv], axis=1)
        vfull = jnp.concatenate([cv.astype(BF16), v3.astype(BF16), pad_kv], axis=1)
        s = bmm_nt(qs, kfull) + bias_ref[par]
        ps = []
        for h in range(N_ATT_HEADS):
            ps.append(_sink_softmax(s[:, h * SUBLANES:(h + 1) * SUBLANES, :], sinks_ref[h]).astype(BF16))
        o = bmm(jnp.concatenate(ps, axis=1), vfull)
        o_r = pltpu.roll(o.reshape(nt * N_ATT_HEADS * SUBLANES, LANES), HALF, axis=1).reshape(o.shape)
        hr = lambda a, h: a[:, h * SUBLANES:(h + 1) * SUBLANES, :]
        low = lane < HALF
        att_par.append(jnp.concatenate([
            jnp.where(low, hr(o, 0), hr(o_r, 1)), jnp.where(low, hr(o, 2), hr(o_r, 3)),
            jnp.where(low, hr(o_r, 4), hr(o, 5)), jnp.where(low, hr(o_r, 6), hr(o, 7))], axis=2))

        st = st_ref[bsl]
        oc = bmm(qrs, st.astype(BF16))
        inner = (bmm_nt(qrs, kr_pad) * dec_ref[par]).astype(BF16)
        oi = bmm(inner, vr_pad)
        rs = []
        for h in range(N_RET_HEADS):
            vsl = slice(h * RET_V_DIM, (h + 1) * RET_V_DIM)
            rsl = slice(h * SUBLANES, (h + 1) * SUBLANES)
            o_h = oi[:, rsl, vsl] + oc[:, rsl, :] * xi_ref[par, rsl, :]
            rs.append(_group_norm(o_h, gng_ref[:, vsl], gnb_ref[:, vsl]) * gate3[:, :, vsl])
        ret_par.append(jnp.concatenate(rs, axis=2))

        kz3 = (kr3 * zeta_ref[par]).astype(BF16)
        vr3_b = vr3.astype(BF16)
        for p in range(nt):
            for i in range(N_RET_HEADS // 2):
                u = _dot_tn(kz3[p][:, i * LANES:(i + 1) * LANES],
                            vr3_b[p][:, 2 * i * RET_V_DIM:(2 * i + 2) * RET_V_DIM])
                for half in range(2):
                    h = 2 * i + half
                    dsl = slice(h * RET_QK_DIM, (h + 1) * RET_QK_DIM)
                    sst_ref[2 * p + par, dsl, :] = (
                        _GL_SAMPLE[h] * st[p, dsl, :]
                        + u[half * RET_QK_DIM:(half + 1) * RET_QK_DIM, half * RET_V_DIM:(half + 1) * RET_V_DIM])

    own0 = row8 < DEC_SEQ
    att3 = jnp.where(own0, att_par[0], att_par[1])
    ret3 = jnp.where(own0, ret_par[0], ret_par[1])
    mix = jnp.concatenate([att3, ret3], axis=2).reshape(2 * nt * DEC_SEQ, MIX_OUT).astype(BF16)
    h_ref[...] = x + _dot(mix, wout_ref[...])


def _sample_mixer(x2d, g_mix, w_in, w_out, sinks, gn_g, gn_b, ck, cv, st):
    n, d = x2d.shape
    nb = ck.shape[0]
    bb = BB_MIX
    r = bb * DEC_SEQ
    const = lambda shape: pl.BlockSpec(shape, lambda i: (0,) * len(shape))
    row = pl.BlockSpec((r, d), lambda i: (i, 0))
    win = pl.BlockSpec((bb, WINDOW, ATT_KV_W), lambda i: (i, 0, 0))
    state = pl.BlockSpec((bb, RET_QK_W, RET_V_DIM), lambda i: (i, 0, 0))
    return pl.pallas_call(
        _sample_mixer_kernel,
        grid=(nb // bb,),
        in_specs=[
            pl.BlockSpec(memory_space=pltpu.SMEM),
            row, const((1, d)), const((d, D_IN)), const((MIX_OUT, d)),
            const((1, RET_V_W)), const((1, RET_V_W)),
            win, win, state,
            const(_S_BIAS.shape), const(_S_DEC.shape), const(_S_XI.shape), const(_S_ZETA.shape),
        ],
        out_specs=[row, win, win, state],
        out_shape=[
            jax.ShapeDtypeStruct((n, d), F32),
            jax.ShapeDtypeStruct((nb, WINDOW, ATT_KV_W), F32),
            jax.ShapeDtypeStruct((nb, WINDOW, ATT_KV_W), F32),
            jax.ShapeDtypeStruct((nb, RET_QK_W, RET_V_DIM), F32),
        ],
        compiler_params=pltpu.CompilerParams(
            dimension_semantics=("arbitrary",), vmem_limit_bytes=VMEM_LIMIT),
        name="sample_mixer",
    )(sinks, x2d, g_mix, w_in, w_out, gn_g, gn_b, ck, cv, st,
      jnp.asarray(_S_BIAS), jnp.asarray(_S_DEC), jnp.asarray(_S_XI), jnp.asarray(_S_ZETA))


def _head_slab(x_ref, b, hd):
    group = X_D_HALVES * N_X_HEADS
    halves = [x_ref[b, pl.ds(dh * N_X_HEADS + hd, N_MEM, stride=group), :] for dh in range(X_D_HALVES)]
    return jnp.concatenate(halves, axis=1).astype(BF16)


def _sample_xattn_kernel(h_ref, g_ref, wq_ref, wo_ref, xk_ref, xv_ref, out_ref):
    bb = xk_ref.shape[0]
    nt = bb // 2
    h = h_ref[...]
    xn = _rms(h, g_ref[...]).astype(BF16)
    q = _dot(xn, wq_ref[...]) * (X_HEAD_DIM ** -0.5)
    units = [(t, par, hd) for t in range(nt) for par in range(2) for hd in range(N_X_HEADS)]
    qts = [q[t * SUBLANES:(t + 1) * SUBLANES].astype(BF16) for t in range(nt)]
    s = jnp.concatenate(
        [_dot_nt(qts[t][:, hd * X_HEAD_DIM:(hd + 1) * X_HEAD_DIM], _head_slab(xk_ref, 2 * t + par, hd))
         for t, par, hd in units], axis=0)
    m = jnp.max(s, axis=-1, keepdims=True)
    p = jnp.exp(s - m)
    p = p * (1.0 / jnp.sum(p, axis=-1, keepdims=True))
    os_ = {}
    for i, (t, par, hd) in enumerate(units):
        pi = p[i * SUBLANES:(i + 1) * SUBLANES].astype(BF16)
        os_[(t, par, hd)] = _dot(pi, _head_slab(xv_ref, 2 * t + par, hd))
    own0 = lax.broadcasted_iota(jnp.int32, (SUBLANES, 1), 0) < DEC_SEQ
    o_tiles = []
    for t in range(nt):
        o_par = [jnp.concatenate([os_[(t, par, hd)] for hd in range(N_X_HEADS)], axis=1) for par in range(2)]
        o_tiles.append(jnp.where(own0, o_par[0], o_par[1]))
    o = jnp.concatenate(o_tiles, axis=0).astype(BF16)
    out_ref[...] = h + _dot(o, wo_ref[...])


def _mlp_value(h, g_ref, wup_ref, wdn_ref, gf_ref, fillers=None):
    xn = _rms(h, g_ref[...]).astype(BF16)
    acc = h
    for c in range(D_FF // FF_CHUNK):
        sl = slice(c * FF_CHUNK, (c + 1) * FF_CHUNK)
        u = jnp.maximum(_dot(xn, wup_ref[:, sl]), 0.0)
        acc = acc + _dot((u * u).astype(BF16), wdn_ref[sl, :])
        if fillers is not None:
            fillers[c]()
    return _rms(acc, gf_ref[...])


def _mlp_xattn_kernel(hp_ref, hsm_ref, gx_ref, wq_ref, wo_ref, xk_ref, xv_ref, g_ref, wup_ref, wdn_ref, gf_ref,
                      yp_ref, ys_ref):
    i = pl.program_id(0)
    n = pl.num_programs(0) - 1
    bb = xk_ref.shape[0]
    rows = bb * DEC_SEQ
    assert bb == D_FF // FF_CHUNK and bb % 2 == 0

    @pl.when(i == 0)
    def _():
        xn = _rms(hsm_ref[...], gx_ref[...]).astype(BF16)
        ys_ref[...] = _dot(xn, wq_ref[...]) * (X_HEAD_DIM ** -0.5)

    @pl.when(i < n)
    def _():
        r0 = pl.multiple_of(i * rows, rows)
        own0 = lax.broadcasted_iota(jnp.int32, (SUBLANES, 1), 0) < DEC_SEQ
        o_rows = {}

        def attend(b):
            t = b // 2
            qt = ys_ref[pl.ds(r0 + t * SUBLANES, SUBLANES), :].astype(BF16)
            s = jnp.concatenate(
                [_dot_nt(qt[:, hd * X_HEAD_DIM:(hd + 1) * X_HEAD_DIM], _head_slab(xk_ref, b, hd))
                 for hd in range(N_X_HEADS)], axis=0)
            m = jnp.max(s, axis=-1, keepdims=True)
            p = jnp.exp(s - m)
            p = p * (1.0 / jnp.sum(p, axis=-1, keepdims=True))
            o_rows[b] = jnp.concatenate(
                [_dot(p[hd * SUBLANES:(hd + 1) * SUBLANES].astype(BF16), _head_slab(xv_ref, b, hd))
                 for hd in range(N_X_HEADS)], axis=1)
            if b % 2 == 1:
                ys_ref[pl.ds(r0 + t * SUBLANES, SUBLANES), :] = jnp.where(own0, o_rows[b - 1], o_rows[b])

        fillers = [functools.partial(attend, b) for b in range(bb)]
        yp_ref[...] = _mlp_value(hp_ref[...], g_ref, wup_ref, wdn_ref, gf_ref, fillers)

    @pl.when(i == n)
    def _():
        hs = hsm_ref[...] + _dot(ys_ref[...].astype(BF16), wo_ref[...])
        ys_ref[...] = _mlp_value(hs, g_ref, wup_ref, wdn_ref, gf_ref)


def _mlp_xattn(hp2d, hsm, g_xattn, w_xq, w_xo, xk, xv, g_mlp, w_up, w_down, g_final):
    n, d = hp2d.shape
    ns = hsm.shape[0]
    nb = xk.shape[0]
    bb = BB_X
    tm = n // (nb // bb)
    n_tiles = n // tm
    assert n_tiles * bb == nb and tm % SUBLANES == 0
    clip = lambda i: jnp.minimum(i, n_tiles - 1)
    prompt = pl.BlockSpec((tm, d), lambda i: (clip(i), 0))
    mem = pl.BlockSpec((bb,) + xk.shape[1:], lambda i: (clip(i), 0, 0))
    const = lambda shape: pl.BlockSpec(shape, lambda i: (0,) * len(shape), pipeline_mode=pl.Buffered(1))
    return pl.pallas_call(
        _mlp_xattn_kernel,
        grid=(n_tiles + 1,),
        in_specs=[prompt, const((ns, d)), const((1, d)), const((d, d)), const((d, d)), mem, mem,
                  const((1, d)), const((d, D_FF)), const((D_FF, d)), const((1, d))],
        out_specs=[prompt, pl.BlockSpec((ns, d), lambda i: (0, 0))],
        out_shape=[jax.ShapeDtypeStruct((n, d), F32), jax.ShapeDtypeStruct((ns, d), F32)],
        compiler_params=pltpu.CompilerParams(
            dimension_semantics=("arbitrary",), vmem_limit_bytes=VMEM_LIMIT),
        name="mlp_xattn",
    )(hp2d, hsm, g_xattn, w_xq, w_xo, xk, xv, g_mlp, w_up, w_down, g_final)


def _mem_rows(c):
    nb = c.shape[0]
    c = c.reshape(nb, N_MEM, N_X_HEADS, X_D_HALVES, LANES)
    return jnp.transpose(c, (0, 1, 3, 2, 4)).reshape(nb, N_MEM * X_D_HALVES * N_X_HEADS, LANES)


def _sample_xattn(h2d, g, w_xq, w_xo, mk, mv):
    n, d = h2d.shape
    nb = mk.shape[0]
    bb = BB_X
    r = bb * DEC_SEQ
    const = lambda shape: pl.BlockSpec(shape, lambda i: (0,) * len(shape))
    row = pl.BlockSpec((r, d), lambda i: (i, 0))
    mem = pl.BlockSpec((bb,) + mk.shape[1:], lambda i: (i, 0, 0))
    return pl.pallas_call(
        _sample_xattn_kernel,
        grid=(nb // bb,),
        in_specs=[row, const((1, d)), const((d, d)), const((d, d)), mem, mem],
        out_specs=row,
        out_shape=jax.ShapeDtypeStruct((n, d), F32),
        compiler_params=pltpu.CompilerParams(
            dimension_semantics=("arbitrary",), vmem_limit_bytes=VMEM_LIMIT),
        name="sample_xattn",
    )(h2d, g, w_xq, w_xo, mk, mv)


def kernel(x_prompt, x_sample, mem_prompt, cache_win_k, cache_win_v, state_ret, cache_mem_k, cache_mem_v,
           g_mix, w_in, attn_sinks, ret_gn_g, ret_gn_b, w_out, g_xattn, g_mem, w_xq, w_xk, w_xv, w_xo,
           g_mlp, w_up, w_down, g_final):
    depth = w_in.shape[0]
    assert depth == 1, "single-layer trunk"
    b, s, d = x_prompt.shape
    nb, ls, _ = x_sample.shape
    row = lambda a: a.reshape(1, -1)
    bf = lambda a: a.astype(BF16)

    w_in_b, w_out_b = bf(w_in[0]), bf(w_out[0])
    w_xq_b, w_xk_b, w_xv_b, w_xo_b = bf(w_xq[0]), bf(w_xk[0]), bf(w_xv[0]), bf(w_xo[0])
    w_up_b, w_dn_b = bf(w_up[0]), bf(w_down[0])
    sinks = attn_sinks[0]
    gn_g, gn_b = row(ret_gn_g[0]), row(ret_gn_b[0])
    g_fin = row(g_final)

    hp, p_wk, p_wv, p_rs = _prompt_mixer_p(x_prompt, row(g_mix[0]), w_in_b, w_out_b, sinks, gn_g, gn_b)
    mk, mv, mkb, mvb = _memory_kv(mem_prompt.reshape(b * N_MEM, d), row(g_mem[0]), w_xk_b, w_xv_b)
    hp = _prompt_xattn(hp, row(g_xattn[0]), w_xq_b, w_xo_b,
                       mkb.reshape(b, N_MEM, d), mvb.reshape(b, N_MEM, d))

    hs, s_wk, s_wv, s_rs = _sample_mixer(
        x_sample.reshape(nb * ls, d), row(g_mix[0]), w_in_b, w_out_b, sinks, gn_g, gn_b,
        cache_win_k[0].reshape(nb, WINDOW, ATT_KV_W), cache_win_v[0].reshape(nb, WINDOW, ATT_KV_W),
        state_ret[0].reshape(nb, RET_QK_W, RET_V_DIM))

    y_prompt, y_sample = _mlp_xattn(
        hp.reshape(b * s, d), hs, row(g_xattn[0]), w_xq_b, w_xo_b,
        _mem_rows(cache_mem_k[0]), _mem_rows(cache_mem_v[0]), row(g_mlp[0]), w_up_b, w_dn_b, g_fin)
    y_prompt = y_prompt.reshape(b, s, d)
    y_sample = y_sample.reshape(nb, ls, d)

    win5 = lambda a, n: a.reshape(1, n, WINDOW, N_KV_HEADS, HEAD_DIM)
    ret5 = lambda a, n: a.reshape(1, n, N_RET_HEADS, RET_QK_DIM, RET_V_DIM)
    mem5 = lambda a: jnp.transpose(a.reshape(b, N_MEM, X_D_HALVES, N_X_HEADS, LANES),
                                   (0, 1, 3, 2, 4)).reshape(1, b, N_MEM, N_X_HEADS, X_HEAD_DIM)
    return (y_prompt, y_sample,
            win5(p_wk, b), win5(p_wv, b), ret5(p_rs, b), mem5(mk), mem5(mv),
            win5(s_wk, nb), win5(s_wv, nb), ret5(s_rs, nb))
```

```python
import functools

import jax
import jax.numpy as jnp
import numpy as np
from jax import lax
from jax.experimental import pallas as pl
from jax.experimental.pallas import tpu as pltpu

F32 = jnp.float32
BF16 = jnp.bfloat16

D_MODEL = 1024
BATCH = 8
SEQ = 2048
DEC_BATCH = 128
DEC_SEQ = 4
HEAD_DIM = 64
N_ATT_HEADS = 8
N_KV_HEADS = 2
KV_GROUP = N_ATT_HEADS // N_KV_HEADS
WINDOW = 128
BLK = 128
N_RET_HEADS = 4
RET_QK_DIM = 64
RET_V_DIM = 128
N_MEM = 256
N_X_HEADS = 4
X_HEAD_DIM = D_MODEL // N_X_HEADS
D_FF = 4 * D_MODEL
RMS_EPS = 1e-6
GN_EPS = 1e-5

ATT_Q_W = N_ATT_HEADS * HEAD_DIM
ATT_KV_W = N_KV_HEADS * HEAD_DIM
RET_QK_W = N_RET_HEADS * RET_QK_DIM
RET_V_W = N_RET_HEADS * RET_V_DIM
MIX_OUT = ATT_Q_W + RET_V_W
D_IN = ATT_Q_W + 2 * ATT_KV_W + 2 * RET_QK_W + 2 * RET_V_W
C_QA, C_KV, C_QKR, C_VR, C_GR = 0, 512, 768, 1280, 1792

LANES = 128
SUBLANES = 8
HALF = LANES // 2
X_D_HALVES = X_HEAD_DIM // LANES
NEG = -1e30
VMEM_LIMIT = 56 * 1024 * 1024

TM_MIX = 512
TM_X = 1024
TM_MLP = 1024
SUB_ROWS = 512
FF_CHUNK = 1024
BB_MIX = 16
BB_X = 4

NEG_SLOPES = [-(2.0 ** (-8.0 * (i + 1) / N_ATT_HEADS)) for i in range(N_ATT_HEADS)]
_LOG_G = np.log(1.0 - 2.0 ** (-5.0 - np.arange(N_RET_HEADS))).astype(np.float32).astype(np.float64)


def _prompt_tables():
    qi = np.arange(BLK)[:, None]
    kj = np.arange(2 * BLK)[None, :]
    dist = (qi + BLK - kj).astype(np.float64)
    mask = np.where((dist >= 0) & (dist < WINDOW), 0.0, NEG)
    l = np.arange(BLK, dtype=np.float64)
    diff = l[:, None] - l[None, :]
    decay = np.where(diff >= 0, np.exp(_LOG_G[:, None, None] * np.maximum(diff, 0.0)), 0.0)
    xi = np.exp((l[:, None] + 1.0) * _LOG_G[None, :])
    zeta = np.exp((BLK - 1.0 - l)[:, None] * _LOG_G[None, :])
    xi_t = np.repeat(xi, RET_V_DIM, axis=1)
    zeta_t = np.repeat(zeta, RET_QK_DIM, axis=1)
    f = lambda a: np.asarray(a, np.float32)
    return f(dist), f(mask), f(decay), f(xi_t), f(zeta_t)


def _sample_tables():
    slopes = -np.asarray(NEG_SLOPES)
    bias = np.full((2, N_ATT_HEADS * SUBLANES, 2 * BLK), NEG, np.float64)
    dec = np.zeros((2, N_RET_HEADS * SUBLANES, BLK), np.float64)
    xi = np.zeros((2, N_RET_HEADS * SUBLANES, RET_V_DIM), np.float64)
    zeta = np.zeros((2, SUBLANES, RET_QK_W), np.float64)
    for par in range(2):
        for r in range(SUBLANES):
            own = DEC_SEQ * par <= r < DEC_SEQ * (par + 1)
            t = r - DEC_SEQ * par if own else r % DEC_SEQ
            for h in range(N_ATT_HEADS):
                row = h * SUBLANES + r
                for j in range(WINDOW):
                    d = t + WINDOW - j
                    if 0 <= d < WINDOW:
                        bias[par, row, j] = -slopes[h] * d
                for c in range(DEC_SEQ):
                    d = t - c
                    if d >= 0:
                        bias[par, row, WINDOW + DEC_SEQ * par + c] = -slopes[h] * d
            for h in range(N_RET_HEADS):
                row = h * SUBLANES + r
                if own:
                    xi[par, row, :] = np.exp((t + 1.0) * _LOG_G[h])
                    zeta[par, r, h * RET_QK_DIM:(h + 1) * RET_QK_DIM] = np.exp((DEC_SEQ - 1.0 - t) * _LOG_G[h])
                    for c in range(t + 1):
                        dec[par, row, DEC_SEQ * par + c] = np.exp(_LOG_G[h] * (t - c))
    f = lambda a: np.asarray(a, np.float32)
    return f(bias), f(dec), f(xi), f(zeta)


_P_DIST, _P_MASK, _P_DECAY, _P_XI, _P_ZETA = _prompt_tables()
_S_BIAS, _S_DEC, _S_XI, _S_ZETA = _sample_tables()
_GL_PROMPT = [float(np.exp(_LOG_G[h] * BLK)) for h in range(N_RET_HEADS)]
_GL_SAMPLE = [float(np.exp(_LOG_G[h] * DEC_SEQ)) for h in range(N_RET_HEADS)]


def _rms(x, g):
    return x * lax.rsqrt(jnp.mean(x * x, axis=-1, keepdims=True) + RMS_EPS) * g


def _dot(a, b):
    return jnp.dot(a, b, preferred_element_type=F32)


def _dot_nt(a, b):
    return lax.dot_general(a, b, (((1,), (1,)), ((), ())), preferred_element_type=F32)


def _dot_tn(a, b):
    return lax.dot_general(a, b, (((0,), (0,)), ((), ())), preferred_element_type=F32)


def _silu(g):
    return g * (1.0 / (1.0 + jnp.exp(-g)))


def _half_masks(width):
    lane = lax.broadcasted_iota(jnp.int32, (1, width), 1)
    lo = ((lane & (LANES - 1)) < HALF).astype(F32)
    return lo, 1.0 - lo


def _sink_softmax(s, sink):
    m = jnp.maximum(jnp.max(s, axis=-1, keepdims=True), sink)
    p = jnp.exp(s - m)
    den = jnp.sum(p, axis=-1, keepdims=True) + jnp.exp(sink - m)
    return p * (1.0 / den)


def _group_norm(o, g, b):
    mu = jnp.mean(o, axis=-1, keepdims=True)
    d = o - mu
    var = jnp.mean(d * d, axis=-1, keepdims=True)
    return d * lax.rsqrt(var + GN_EPS) * g + b


def _prompt_mixer_kernel(sinks_ref, x_ref, gmix_ref, win_ref, wout_ref, gng_ref, gnb_ref,
                         dist_ref, mask_ref, decay_ref, xi_ref, zeta_ref,
                         h_ref, wk_ref, wv_ref, st_ref,
                         qlo_s, qhi_s, kd0_s, kd1_s, vd0_s, vd1_s,
                         qrlo_s, qrhi_s, kr_s, vr_s, gate_s, mix_s, bias_s):
    t = pl.program_id(1)
    nt = pl.num_programs(1)
    tm = x_ref.shape[1]
    nblk = tm // BLK

    @pl.when(t == 0)
    def _():
        kd0_s[0:BLK, :] = jnp.zeros((BLK, LANES), BF16)
        kd1_s[0:BLK, :] = jnp.zeros((BLK, LANES), BF16)
        vd0_s[0:BLK, :] = jnp.zeros((BLK, 2 * LANES), BF16)
        vd1_s[0:BLK, :] = jnp.zeros((BLK, 2 * LANES), BF16)
        st_ref[...] = jnp.zeros_like(st_ref)

    @pl.when(t > 0)
    def _():
        kd0_s[0:BLK, :] = kd0_s[tm:tm + BLK, :]
        kd1_s[0:BLK, :] = kd1_s[tm:tm + BLK, :]
        vd0_s[0:BLK, :] = vd0_s[tm:tm + BLK, :]
        vd1_s[0:BLK, :] = vd1_s[tm:tm + BLK, :]

    @pl.when((t == 0) & (pl.program_id(0) == 0))
    def _():
        for h in range(N_ATT_HEADS):
            bias_s[h] = NEG_SLOPES[h] * dist_ref[...] + mask_ref[...]

    x = x_ref[0]
    xn = _rms(x, gmix_ref[...]).astype(BF16)

    z = _dot(xn, win_ref[...])
    lo512, hi512 = _half_masks(ATT_Q_W)
    q = z[:, C_QA:C_QA + ATT_Q_W]
    qlo_s[...] = (q * (lo512 * HEAD_DIM ** -0.5)).astype(BF16)
    qhi_s[...] = (q * (hi512 * HEAD_DIM ** -0.5)).astype(BF16)

    low = lax.broadcasted_iota(jnp.int32, (tm, LANES), 1) < HALF
    kv = z[:, C_KV:C_KV + 2 * ATT_KV_W]
    k = kv[:, :ATT_KV_W]
    v = kv[:, ATT_KV_W:]
    k_r = pltpu.roll(k, HALF, axis=1)
    v_r = pltpu.roll(v, HALF, axis=1)
    kd0_s[BLK:BLK + tm, :] = jnp.where(low, k, k_r).astype(BF16)
    kd1_s[BLK:BLK + tm, :] = jnp.where(low, k_r, k).astype(BF16)
    vd0_s[BLK:BLK + tm, 0:LANES] = jnp.where(low, v, 1.0).astype(BF16)
    vd0_s[BLK:BLK + tm, LANES:2 * LANES] = jnp.where(low, 1.0, v_r).astype(BF16)
    vd1_s[BLK:BLK + tm, 0:LANES] = jnp.where(low, v_r, 1.0).astype(BF16)
    vd1_s[BLK:BLK + tm, LANES:2 * LANES] = jnp.where(low, 1.0, v).astype(BF16)

    @pl.when(t == nt - 1)
    def _():
        wk_ref[0] = k[tm - WINDOW:, :]
        wv_ref[0] = v[tm - WINDOW:, :]

    lo256, hi256 = _half_masks(RET_QK_W)
    qkr = z[:, C_QKR:C_QKR + 2 * RET_QK_W]
    qr = qkr[:, :RET_QK_W]
    qrlo_s[...] = (qr * lo256).astype(BF16)
    qrhi_s[...] = (qr * hi256).astype(BF16)
    kr_s[...] = qkr[:, RET_QK_W:] * (RET_QK_DIM ** -0.5)
    vr_s[...] = z[:, C_VR:C_VR + RET_V_W].astype(BF16)
    gate_s[...] = _silu(z[:, C_GR:C_GR + RET_V_W])

    lowb =lax.broadcasted_iota(jnp.int32, (BLK, LANES), 1) < HALF
    col = lax.broadcasted_iota(jnp.int32, (BLK, 2 * BLK), 1)
    first_mask = jnp.where((col < BLK) & (t == 0), NEG, 0.0)
    kd_refs = (kd0_s, kd1_s)
    vd_refs = (vd0_s, vd1_s)
    n_pairs = N_RET_HEADS // 2
    state = [st_ref[0, i * LANES:(i + 1) * LANES, :] for i in range(n_pairs)]

    for j in range(nblk):
        rows = slice(j * BLK, (j + 1) * BLK)
        krows = slice(j * BLK, (j + 2) * BLK)

        for kvh in range(N_KV_HEADS):
            kd = kd_refs[kvh][krows, :]
            vd = vd_refs[kvh][krows, :]
            c0 = kvh * KV_GROUP * HEAD_DIM
            qst = jnp.concatenate([qlo_s[rows, c0:c0 + LANES], qhi_s[rows, c0:c0 + LANES],
                                   qlo_s[rows, c0 + LANES:c0 + 2 * LANES],
                                   qhi_s[rows, c0 + LANES:c0 + 2 * LANES]], axis=0)
            s = _dot_nt(qst, kd)
            es, esink = [], []
            for g in range(KV_GROUP):
                h = kvh * KV_GROUP + g
                sg = s[g * BLK:(g + 1) * BLK] + bias_s[h]
                if j == 0:
                    sg = sg + first_mask
                sink = sinks_ref[h]
                m = jnp.maximum(jnp.max(sg, axis=-1, keepdims=True), sink)
                es.append(jnp.exp(sg - m).astype(BF16))
                esink.append(jnp.exp(sink - m))
            o = _dot(jnp.concatenate(es, axis=0), vd)
            for pair in range(KV_GROUP // 2):
                oe = o[2 * pair * BLK:(2 * pair + 1) * BLK]
                oo = o[(2 * pair + 1) * BLK:(2 * pair + 2) * BLK]
                num = jnp.where(lowb, oe[:, :LANES], oo[:, LANES:])
                den = (jnp.where(lowb, oe[:, LANES:], oo[:, :LANES])
                       + jnp.where(lowb, esink[2 * pair], esink[2 * pair + 1]))
                cs = c0 + pair * LANES
                mix_s[rows, cs:cs + LANES] = (num * (1.0 / den)).astype(BF16)

        for i in range(n_pairs):
            lsl = slice(i * LANES, (i + 1) * LANES)
            kp = kr_s[rows, lsl]
            sp = state[i]
            vpair = vr_s[rows, 2 * i * RET_V_DIM:(2 * i + 2) * RET_V_DIM]
            q2 = jnp.concatenate([qrlo_s[rows, lsl], qrhi_s[rows, lsl]], axis=0)
            a = _dot_nt(q2, kp.astype(BF16))
            inner = jnp.concatenate([a[:BLK] * decay_ref[2 * i], a[BLK:] * decay_ref[2 * i + 1]], axis=0)
            oi = _dot(inner.astype(BF16), vpair)
            oc = _dot(q2, sp.astype(BF16))
            for half in range(2):
                h = 2 * i + half
                vsl = slice(h * RET_V_DIM, (h + 1) * RET_V_DIM)
                hr = slice(half * BLK, (half + 1) * BLK)
                o = oi[hr, half * RET_V_DIM:(half + 1) * RET_V_DIM] + oc[hr] * xi_ref[:, vsl]
                r = _group_norm(o, gng_ref[:, vsl], gnb_ref[:, vsl]) * gate_s[rows, vsl]
                mix_s[rows, ATT_Q_W + h * RET_V_DIM:ATT_Q_W + (h + 1) * RET_V_DIM] = r.astype(BF16)
            kz = (kp * zeta_ref[:, lsl]).astype(BF16)
            u = _dot_tn(kz, vpair)
            state[i] = jnp.concatenate(
                [_GL_PROMPT[2 * i] * sp[:RET_QK_DIM] + u[:RET_QK_DIM, :RET_V_DIM],
                 _GL_PROMPT[2 * i + 1] * sp[RET_QK_DIM:] + u[RET_QK_DIM:, RET_V_DIM:]], axis=0)

    for i in range(n_pairs):
        st_ref[0, i * LANES:(i + 1) * LANES, :] = state[i]

    h_ref[0] = x + _dot(mix_s[...], wout_ref[...])


def _prompt_mixer(x, g_mix, w_in, w_out, sinks, gn_g, gn_b):
    b, s, d = x.shape
    tm = TM_MIX
    const = lambda shape: pl.BlockSpec(shape, lambda i, j: (0,) * len(shape))
    return pl.pallas_call(
        _prompt_mixer_kernel,
        grid=(b, s // tm),
        in_specs=[
            pl.BlockSpec(memory_space=pltpu.SMEM),
            pl.BlockSpec((1, tm, d), lambda i, j: (i, j, 0)),
            const((1, d)), const((d, D_IN)), const((MIX_OUT, d)),
            const((1, RET_V_W)), const((1, RET_V_W)),
            const((BLK, 2 * BLK)), const((BLK, 2 * BLK)),
            const((N_RET_HEADS, BLK, BLK)), const((BLK, RET_V_W)), const((BLK, RET_QK_W)),
        ],
        out_specs=[
            pl.BlockSpec((1, tm, d), lambda i, j: (i, j, 0)),
            pl.BlockSpec((1, WINDOW, ATT_KV_W), lambda i, j: (i, 0, 0)),
            pl.BlockSpec((1, WINDOW, ATT_KV_W), lambda i, j: (i, 0, 0)),
            pl.BlockSpec((1, RET_QK_W, RET_V_DIM), lambda i, j: (i, 0, 0)),
        ],
        out_shape=[
            jax.ShapeDtypeStruct((b, s, d), F32),
            jax.ShapeDtypeStruct((b, WINDOW, ATT_KV_W), F32),
            jax.ShapeDtypeStruct((b, WINDOW, ATT_KV_W), F32),
            jax.ShapeDtypeStruct((b, RET_QK_W, RET_V_DIM), F32),
        ],
        scratch_shapes=[
            pltpu.VMEM((tm, ATT_Q_W), BF16), pltpu.VMEM((tm, ATT_Q_W), BF16),
            pltpu.VMEM((tm + BLK, LANES), BF16), pltpu.VMEM((tm + BLK, LANES), BF16),
            pltpu.VMEM((tm + BLK, 2 * LANES), BF16), pltpu.VMEM((tm + BLK, 2 * LANES), BF16),
            pltpu.VMEM((tm, RET_QK_W), BF16), pltpu.VMEM((tm, RET_QK_W), BF16),
            pltpu.VMEM((tm, RET_QK_W), F32), pltpu.VMEM((tm, RET_V_W), BF16),
            pltpu.VMEM((tm, RET_V_W), F32), pltpu.VMEM((tm, MIX_OUT), BF16),
            pltpu.VMEM((N_ATT_HEADS, BLK, 2 * BLK), F32),
        ],
        compiler_params=pltpu.CompilerParams(
            dimension_semantics=("arbitrary", "arbitrary"), vmem_limit_bytes=VMEM_LIMIT),
        name="prompt_mixer",
    )(sinks, x, g_mix, w_in, w_out, gn_g, gn_b,
      jnp.asarray(_P_DIST), jnp.asarray(_P_MASK), jnp.asarray(_P_DECAY), jnp.asarray(_P_XI),
      jnp.asarray(_P_ZETA))


def _pm_project_stages(x, slot, gmix_ref, win_ref, sc, kv_out=None):
    tm = x.shape[0]
    xn = _rms(x, gmix_ref[...]).astype(BF16)

    def stage_q():
        q = _dot(xn, win_ref[:, C_QA:C_QA + ATT_Q_W])
        lo512, hi512 = _half_masks(ATT_Q_W)
        sc["qlo"][slot] = (q * (lo512 * HEAD_DIM ** -0.5)).astype(BF16)
        sc["qhi"][slot] = (q * (hi512 * HEAD_DIM ** -0.5)).astype(BF16)

    def stage_kv_qkr():
        z = _dot(xn, win_ref[:, C_KV:C_VR])
        low = lax.broadcasted_iota(jnp.int32, (tm, LANES), 1) < HALF
        k = z[:, 0:ATT_KV_W]
        v = z[:, ATT_KV_W:2 * ATT_KV_W]
        if kv_out is not None:
            kv_out[0][0] = k[tm - WINDOW:, :]
            kv_out[1][0] = v[tm - WINDOW:, :]
        k_r = pltpu.roll(k, HALF, axis=1)
        v_r = pltpu.roll(v, HALF, axis=1)
        sc["kd0"][slot] = jnp.where(low, k, k_r).astype(BF16)
        sc["kd1"][slot] = jnp.where(low, k_r, k).astype(BF16)
        sc["vd0"][slot, :, 0:LANES] = jnp.where(low, v, 1.0).astype(BF16)
        sc["vd0"][slot, :, LANES:2 * LANES] = jnp.where(low, 1.0, v_r).astype(BF16)
        sc["vd1"][slot, :, 0:LANES] = jnp.where(low, v_r, 1.0).astype(BF16)
        sc["vd1"][slot, :, LANES:2 * LANES] = jnp.where(low, 1.0, v).astype(BF16)
        lo256, hi256 = _half_masks(RET_QK_W)
        qr = z[:, 2 * ATT_KV_W:2 * ATT_KV_W + RET_QK_W]
        sc["qrlo"][slot] = (qr * lo256).astype(BF16)
        sc["qrhi"][slot] = (qr * hi256).astype(BF16)
        sc["kr"][slot] = z[:, 2 * ATT_KV_W + RET_QK_W:] * (RET_QK_DIM ** -0.5)

    def stage_vr():
        sc["vr"][slot] = _dot(xn, win_ref[:, C_VR:C_VR + RET_V_W]).astype(BF16)

    def stage_gate():
        sc["gate"][slot] = _silu(_dot(xn, win_ref[:, C_GR:C_GR + RET_V_W]))

    return [stage_q, stage_kv_qkr, stage_vr, stage_gate]


def _pm_last_block(slot, tm, sc):
    rows = slice(tm - BLK, tm)
    return ([sc["kd0"][slot, rows, :], sc["kd1"][slot, rows, :]],
            [sc["vd0"][slot, rows, :], sc["vd1"][slot, rows, :]])


def _pm_blocks(slot, prev_kd, prev_vd, is_first, state, x, fillers, tm, sinks_ref, wout_ref, gng_ref, gnb_ref,
               decay_ref, xi_ref, zeta_ref, sc):
    nblk = tm // BLK
    lowb = lax.broadcasted_iota(jnp.int32, (BLK, LANES), 1) < HALF
    col = lax.broadcasted_iota(jnp.int32, (BLK, 2 * BLK), 1)
    first_mask = None if is_first is False else jnp.where((col < BLK) & is_first, NEG, 0.0)
    kd_refs = (sc["kd0"], sc["kd1"])
    vd_refs = (sc["vd0"], sc["vd1"])
    qlo, qhi, mix = sc["qlo"], sc["qhi"], sc["mix"]
    n_pairs = N_RET_HEADS // 2

    for j in range(nblk):
        rows = slice(j * BLK, (j + 1) * BLK)
        for kvh in range(N_KV_HEADS):
            if j == 0:
                kd = jnp.concatenate([prev_kd[kvh], kd_refs[kvh][slot, rows, :]], axis=0)
                vd = jnp.concatenate([prev_vd[kvh], vd_refs[kvh][slot, rows, :]], axis=0)
            else:
                krows = slice((j - 1) * BLK, (j + 1) * BLK)
                kd = kd_refs[kvh][slot, krows, :]
                vd = vd_refs[kvh][slot, krows, :]
            c0 = kvh * KV_GROUP * HEAD_DIM
            qst = jnp.concatenate([qlo[slot, rows, c0:c0 + LANES], qhi[slot, rows, c0:c0 + LANES],
                                   qlo[slot, rows, c0 + LANES:c0 + 2 * LANES],
                                   qhi[slot, rows, c0 + LANES:c0 + 2 * LANES]], axis=0)
            s = _dot_nt(qst, kd)
            es, esink = [], []
            for g in range(KV_GROUP):
                h = kvh * KV_GROUP + g
                sg = s[g * BLK:(g + 1) * BLK] + sc["bias"][h]
                if j == 0 and first_mask is not None:
                    sg = sg + first_mask
                sink = sinks_ref[h]
                m = jnp.maximum(jnp.max(sg, axis=-1, keepdims=True), sink)
                es.append(jnp.exp(sg - m).astype(BF16))
                esink.append(jnp.exp(sink - m))
            o = _dot(jnp.concatenate(es, axis=0), vd)
            for pair in range(KV_GROUP // 2):
                oe = o[2 * pair * BLK:(2 * pair + 1) * BLK]
                oo = o[(2 * pair + 1) * BLK:(2 * pair + 2) * BLK]
                num = jnp.where(lowb, oe[:, :LANES], oo[:, LANES:])
                den = (jnp.where(lowb, oe[:, LANES:], oo[:, :LANES])
                       + jnp.where(lowb, esink[2 * pair], esink[2 * pair + 1]))
                cs = c0 + pair * LANES
                mix[slot, rows, cs:cs + LANES] = (num * (1.0 / den)).astype(BF16)

        for i in range(n_pairs):
            lsl = slice(i * LANES, (i + 1) * LANES)
            kp = sc["kr"][slot, rows, lsl]
            sp = state[i]
            vpair = sc["vr"][slot, rows, 2 * i * RET_V_DIM:(2 * i + 2) * RET_V_DIM]
            q2 = jnp.concatenate([sc["qrlo"][slot, rows, lsl], sc["qrhi"][slot, rows, lsl]], axis=0)
            a = _dot_nt(q2, kp.astype(BF16))
            inner = jnp.concatenate([a[:BLK] * decay_ref[2 * i], a[BLK:] * decay_ref[2 * i + 1]], axis=0)
            oi = _dot(inner.astype(BF16), vpair)
            oc = _dot(q2, sp.astype(BF16))
            for half in range(2):
                h = 2 * i + half
                vsl = slice(h * RET_V_DIM, (h + 1) * RET_V_DIM)
                hr = slice(half * BLK, (half + 1) * BLK)
                o = oi[hr, half * RET_V_DIM:(half + 1) * RET_V_DIM] + oc[hr] * xi_ref[:, vsl]
                r = _group_norm(o, gng_ref[:, vsl], gnb_ref[:, vsl]) * sc["gate"][slot, rows, vsl]
                mix[slot, rows, ATT_Q_W + h * RET_V_DIM:ATT_Q_W + (h + 1) * RET_V_DIM] = r.astype(BF16)
            kz = (kp * zeta_ref[:, lsl]).astype(BF16)
            u = _dot_tn(kz, vpair)
            state[i] = jnp.concatenate(
                [_GL_PROMPT[2 * i] * sp[:RET_QK_DIM] + u[:RET_QK_DIM, :RET_V_DIM],
                 _GL_PROMPT[2 * i + 1] * sp[RET_QK_DIM:] + u[RET_QK_DIM:, RET_V_DIM:]], axis=0)

        fillers[j]()

    return x + _dot(mix[slot], wout_ref[...]), state


def _prompt_mixer_kernel_p(sinks_ref, xpair_ref, xnext_ref, gmix_ref, win_ref, wout_ref, gng_ref, gnb_ref,
                           dist_ref, mask_ref, decay_ref, xi_ref, zeta_ref,
                           wupf_ref, wdnf_ref, wqf_ref, wof_ref,
                           h_ref, wk_ref, wv_ref, st_ref,
                           wupb_ref, wdnb_ref, wqb_ref, wob_ref,
                           qlo_s, qhi_s, kd0_s, kd1_s, vd0_s, vd1_s,
                           qrlo_s, qrhi_s, kr_s, vr_s, gate_s, mix_s, bias_s, state_s):
    u = pl.program_id(0)
    tm = xnext_ref.shape[0]
    wupb_ref[...] = wupf_ref[...].astype(BF16)
    wdnb_ref[...] = wdnf_ref[...].astype(BF16)
    wqb_ref[...] = wqf_ref[...].astype(BF16)
    wob_ref[...] = wof_ref[...].astype(BF16)
    sc = dict(qlo=qlo_s, qhi=qhi_s, kd0=kd0_s, kd1=kd1_s, vd0=vd0_s, vd1=vd1_s, qrlo=qrlo_s, qrhi=qrhi_s,
              kr=kr_s, vr=vr_s, gate=gate_s, mix=mix_s, bias=bias_s)
    n_pairs = N_RET_HEADS // 2
    blocks = functools.partial(_pm_blocks, tm=tm, sinks_ref=sinks_ref, wout_ref=wout_ref, gng_ref=gng_ref,
                               gnb_ref=gnb_ref, decay_ref=decay_ref, xi_ref=xi_ref, zeta_ref=zeta_ref, sc=sc)

    @pl.when(u == 0)
    def _():
        for h in range(N_ATT_HEADS):
            bias_s[h] = NEG_SLOPES[h] * dist_ref[...] + mask_ref[...]
        state_s[...] = jnp.zeros_like(state_s)
        kd0_s[1] = jnp.zeros(kd0_s.shape[1:], BF16)
        kd1_s[1] = jnp.zeros(kd1_s.shape[1:], BF16)
        vd0_s[1] = jnp.zeros(vd0_s.shape[1:], BF16)
        vd1_s[1] = jnp.zeros(vd1_s.shape[1:], BF16)
        for stage in _pm_project_stages(xpair_ref[0:tm, :], 0, gmix_ref, win_ref, sc):
            stage()

    seq_start = (u % 2) == 0
    state = [jnp.where(seq_start, 0.0, state_s[i * LANES:(i + 1) * LANES, :]) for i in range(n_pairs)]

    prev_kd, prev_vd = _pm_last_block(1, tm, sc)
    stages = _pm_project_stages(xpair_ref[tm:2 * tm, :], 1, gmix_ref, win_ref, sc, kv_out=(wk_ref, wv_ref))
    h0, state = blocks(0, prev_kd, prev_vd, seq_start, state, xpair_ref[0:tm, :], stages)
    h_ref[0:tm, :] = h0

    prev_kd, prev_vd = _pm_last_block(0, tm, sc)
    stages = _pm_project_stages(xnext_ref[...], 0, gmix_ref, win_ref, sc)
    h1, state = blocks(1, prev_kd, prev_vd, False, state, xpair_ref[tm:2 * tm, :], stages)
    h_ref[tm:2 * tm, :] = h1

    for i in range(n_pairs):
        state_s[i * LANES:(i + 1) * LANES, :] = state[i]
        st_ref[0, i * LANES:(i + 1) * LANES, :] = state[i]


def _prompt_mixer_p(x, g_mix, w_in, w_out, sinks, gn_g, gn_b, side_f32):
    b, s, d = x.shape
    tm = TM_MIX
    n_tiles = b * s // tm
    steps = n_tiles // 2
    seq_steps = s // (2 * tm)
    assert s % (2 * tm) == 0 and seq_steps == 2, "kernel assumes 4 tiles per sequence"
    x2d = x.reshape(b * s, d)
    const = lambda shape: pl.BlockSpec(shape, lambda i: (0,) * len(shape), pipeline_mode=pl.Buffered(1))
    slot2 = lambda rows, cols, dt: pltpu.VMEM((2, rows, cols), dt)
    side_specs = [pl.BlockSpec((w.shape[0] // steps, w.shape[1]), lambda i: (i, 0)) for w in side_f32]
    outs = pl.pallas_call(
        _prompt_mixer_kernel_p,
        grid=(steps,),
        in_specs=[
            pl.BlockSpec(memory_space=pltpu.SMEM),
            pl.BlockSpec((2 * tm, d), lambda i: (i, 0)),
            pl.BlockSpec((tm, d), lambda i: (jnp.minimum(2 * i + 2, n_tiles - 1), 0)),
            const((1, d)), const((d, D_IN)), const((MIX_OUT, d)),
            const((1, RET_V_W)), const((1, RET_V_W)),
            const((BLK, 2 * BLK)), const((BLK, 2 * BLK)),
            const((N_RET_HEADS, BLK, BLK)), const((BLK, RET_V_W)), const((BLK, RET_QK_W)),
        ] + side_specs,
        out_specs=[
            pl.BlockSpec((2 * tm, d), lambda i: (i, 0)),
            pl.BlockSpec((1, WINDOW, ATT_KV_W), lambda i: (i // seq_steps, 0, 0)),
            pl.BlockSpec((1, WINDOW, ATT_KV_W), lambda i: (i // seq_steps, 0, 0)),
            pl.BlockSpec((1, RET_QK_W, RET_V_DIM), lambda i: (i // seq_steps, 0, 0)),
        ] + side_specs,
        out_shape=[
            jax.ShapeDtypeStruct((b * s, d), F32),
            jax.ShapeDtypeStruct((b, WINDOW, ATT_KV_W), F32),
            jax.ShapeDtypeStruct((b, WINDOW, ATT_KV_W), F32),
            jax.ShapeDtypeStruct((b, RET_QK_W, RET_V_DIM), F32),
        ] + [jax.ShapeDtypeStruct(w.shape, BF16) for w in side_f32],
        scratch_shapes=[
            slot2(tm, ATT_Q_W, BF16), slot2(tm, ATT_Q_W, BF16),
            slot2(tm, LANES, BF16), slot2(tm, LANES, BF16),
            slot2(tm, 2 * LANES, BF16), slot2(tm, 2 * LANES, BF16),
            slot2(tm, RET_QK_W, BF16), slot2(tm, RET_QK_W, BF16),
            slot2(tm, RET_QK_W, F32), slot2(tm, RET_V_W, BF16),
            slot2(tm, RET_V_W, F32), slot2(tm, MIX_OUT, BF16),
            pltpu.VMEM((N_ATT_HEADS, BLK, 2 * BLK), F32),
            pltpu.VMEM((RET_QK_W, RET_V_DIM), F32),
        ],
        compiler_params=pltpu.CompilerParams(
            dimension_semantics=("arbitrary",), vmem_limit_bytes=VMEM_LIMIT),
        name="prompt_mixer",
    )(sinks, x2d, x2d, g_mix, w_in, w_out, gn_g, gn_b,
      jnp.asarray(_P_DIST), jnp.asarray(_P_MASK), jnp.asarray(_P_DECAY), jnp.asarray(_P_XI),
      jnp.asarray(_P_ZETA), *side_f32)
    return (outs[0].reshape(b, s, d),) + tuple(outs[1:])


def _memkv_kernel(mem_ref, g_ref, wk_ref, wv_ref, win_ref, wout_ref,
                  mk_ref, mv_ref, mkb_ref, mvb_ref, winb_ref, woutb_ref):
    winb_ref[...] = win_ref[...].astype(BF16)
    woutb_ref[...] = wout_ref[...].astype(BF16)
    mn = _rms(mem_ref[...], g_ref[...]).astype(BF16)
    mk = _dot(mn, wk_ref[...].astype(BF16))
    mv = _dot(mn, wv_ref[...].astype(BF16))
    tm = mem_ref.shape[0]
    group = X_D_HALVES * N_X_HEADS
    for hd in range(N_X_HEADS):
        for dh in range(X_D_HALVES):
            cols = slice(hd * X_HEAD_DIM + dh * LANES, hd * X_HEAD_DIM + (dh + 1) * LANES)
            rows = pl.ds(dh * N_X_HEADS + hd, tm, stride=group)
            mk_ref[rows, :] = mk[:, cols]
            mv_ref[rows, :] = mv[:, cols]
    mkb_ref[...] = mk.astype(BF16)
    mvb_ref[...] = mv.astype(BF16)


def _memory_kv(mem2d, g_mem, w_xk, w_xv, w_in, w_out):
    n, d = mem2d.shape
    tm = 512
    row = pl.BlockSpec((tm, d), lambda i: (i, 0))
    rows_out = pl.BlockSpec((tm * d // LANES, LANES), lambda i: (i, 0))
    const = lambda shape: pl.BlockSpec(shape, lambda i: (0,) * len(shape), pipeline_mode=pl.Buffered(1))
    steps = n // tm
    win_blk = pl.BlockSpec((w_in.shape[0] // steps, w_in.shape[1]), lambda i: (i, 0))
    wout_blk = pl.BlockSpec((w_out.shape[0] // steps, w_out.shape[1]), lambda i: (i, 0))
    return pl.pallas_call(
        _memkv_kernel,
        grid=(n // tm,),
        in_specs=[row, const((1, d)), const((d, d)), const((d, d)), win_blk, wout_blk],
        out_specs=[rows_out, rows_out, row, row, win_blk, wout_blk],
        out_shape=[jax.ShapeDtypeStruct((n * d // LANES, LANES), F32),
                   jax.ShapeDtypeStruct((n * d // LANES, LANES), F32),
                   jax.ShapeDtypeStruct((n, d), BF16), jax.ShapeDtypeStruct((n, d), BF16),
                   jax.ShapeDtypeStruct(w_in.shape, BF16), jax.ShapeDtypeStruct(w_out.shape, BF16)],
        compiler_params=pltpu.CompilerParams(
            dimension_semantics=("arbitrary",), vmem_limit_bytes=VMEM_LIMIT),
        name="memory_kv",
    )(mem2d, g_mem, w_xk, w_xv, w_in, w_out)


def _prompt_xattn_kernel(h_ref, g_ref, wq_ref, wo_ref, mk_ref, mv_ref, out_ref, o_s):
    for r0 in range(0, h_ref.shape[1], SUB_ROWS):
        rows = slice(r0, r0 + SUB_ROWS)
        h = h_ref[0, rows, :]
        xn = _rms(h, g_ref[...]).astype(BF16)
        q = (_dot(xn, wq_ref[...]) * (X_HEAD_DIM ** -0.5)).astype(BF16)
        for hd in range(N_X_HEADS):
            sl = slice(hd * X_HEAD_DIM, (hd + 1) * X_HEAD_DIM)
            s = _dot_nt(q[:, sl], mk_ref[0, :, sl])
            m = jnp.max(s, axis=-1, keepdims=True)
            p = jnp.exp(s - m)
            p = p * (1.0 / jnp.sum(p, axis=-1, keepdims=True))
            o_s[rows, sl] = _dot(p.astype(BF16), mv_ref[0, :, sl]).astype(BF16)
        out_ref[0, rows, :] = h + _dot(o_s[rows, :], wo_ref[...])


def _prompt_xattn(h, g, w_xq, w_xo, mkb, mvb):
    b, s, d = h.shape
    tm = TM_X
    const = lambda shape: pl.BlockSpec(shape, lambda i, j: (0,) * len(shape))
    tok = pl.BlockSpec((1, tm, d), lambda i, j: (i, j, 0))
    mem = pl.BlockSpec((1, N_MEM, d), lambda i, j: (i, 0, 0))
    return pl.pallas_call(
        _prompt_xattn_kernel,
        grid=(b, s // tm),
        in_specs=[tok, const((1, d)), const((d, d)), const((d, d)), mem, mem],
        out_specs=tok,
        out_shape=jax.ShapeDtypeStruct((b, s, d), F32),
        scratch_shapes=[pltpu.VMEM((tm, d), BF16)],
        compiler_params=pltpu.CompilerParams(
            dimension_semantics=("arbitrary", "arbitrary"), vmem_limit_bytes=VMEM_LIMIT),
        name="prompt_xattn",
    )(h, g, w_xq, w_xo, mkb, mvb)


def _mlp_rows(h_ref, y_ref, g_ref, wup_ref, wdn_ref, gf_ref):
    for r0 in range(0, h_ref.shape[0], SUB_ROWS):
        rows = slice(r0, r0 + SUB_ROWS)
        h = h_ref[rows, :]
        xn = _rms(h, g_ref[...]).astype(BF16)
        acc = h
        for c in range(D_FF // FF_CHUNK):
            sl = slice(c * FF_CHUNK, (c + 1) * FF_CHUNK)
            u = jnp.maximum(_dot(xn, wup_ref[:, sl]), 0.0)
            acc = acc + _dot((u * u).astype(BF16), wdn_ref[sl, :])
        y_ref[rows, :] = _rms(acc, gf_ref[...])


def _mlp_kernel(hp_ref, hs_ref, g_ref, wup_ref, wdn_ref, gf_ref, yp_ref, ys_ref):
    last = pl.num_programs(0) - 1

    @pl.when(pl.program_id(0) < last)
    def _():
        _mlp_rows(hp_ref, yp_ref, g_ref, wup_ref, wdn_ref, gf_ref)

    @pl.when(pl.program_id(0) == last)
    def _():
        _mlp_rows(hs_ref, ys_ref, g_ref, wup_ref, wdn_ref, gf_ref)


def _mlp_final(hp2d, hs2d, g_mlp, w_up, w_down, g_final):
    n, d = hp2d.shape
    ns = hs2d.shape[0]
    tm = TM_MLP
    n_tiles = n // tm
    prompt = pl.BlockSpec((tm, d), lambda i: (jnp.minimum(i, n_tiles - 1), 0))
    sample = pl.BlockSpec((ns, d), lambda i: (0, 0))
    const = lambda shape: pl.BlockSpec(shape, lambda i: (0,) * len(shape), pipeline_mode=pl.Buffered(1))
    return pl.pallas_call(
        _mlp_kernel,
        grid=(n_tiles + 1,),
        in_specs=[prompt, sample, const((1, d)), const((d, D_FF)), const((D_FF, d)), const((1, d))],
        out_specs=[prompt, sample],
        out_shape=[jax.ShapeDtypeStruct((n, d), F32), jax.ShapeDtypeStruct((ns, d), F32)],
        compiler_params=pltpu.CompilerParams(
            dimension_semantics=("arbitrary",), vmem_limit_bytes=VMEM_LIMIT),
        name="mlp_final",
    )(hp2d, hs2d, g_mlp, w_up, w_down, g_final)


def _sample_mixer_kernel(sinks_ref, x_ref, gmix_ref, win_ref, wout_ref, gng_ref, gnb_ref,
                         ck_ref, cv_ref, st_ref, bias_ref, dec_ref, xi_ref, zeta_ref,
                         h_ref, swk_ref, swv_ref, sst_ref):
    bb = ck_ref.shape[0]
    nt = bb // 2
    x = x_ref[...]
    xn = _rms(x, gmix_ref[...]).astype(BF16)
    tile3 = lambda a: a.reshape(nt, SUBLANES, a.shape[-1])

    q = _dot(xn, win_ref[:, C_QA:C_QA + ATT_Q_W]) * (HEAD_DIM ** -0.5)
    kv = _dot(xn, win_ref[:, C_KV:C_KV + 2 * ATT_KV_W])
    qkr = _dot(xn, win_ref[:, C_QKR:C_QKR + 2 * RET_QK_W])
    vr = _dot(xn, win_ref[:, C_VR:C_VR + RET_V_W])
    gate3 = tile3(_silu(_dot(xn, win_ref[:, C_GR:C_GR + RET_V_W])))

    lo512, hi512 = _half_masks(ATT_Q_W)
    q_r = pltpu.roll(q, HALF, axis=1)
    q_nat3 = tile3(q)
    q_rot3 = tile3(q_r)
    lo3 = lo512.reshape(1, 1, ATT_Q_W)
    hi3 = hi512.reshape(1, 1, ATT_Q_W)
    qa3 = (q_nat3 * lo3).astype(BF16)
    qb3 = (q_rot3 * lo3).astype(BF16)
    qc3 = (q_rot3 * hi3).astype(BF16)
    qd3 = (q_nat3 * hi3).astype(BF16)
    t128 = lambda a, i: a[:, :, i * LANES:(i + 1) * LANES]
    qs = jnp.concatenate([t128(qa3, 0), t128(qb3, 1), t128(qa3, 1), t128(qb3, 2),
                          t128(qc3, 2), t128(qd3, 2), t128(qc3, 3), t128(qd3, 3)], axis=1)

    k3 = tile3(kv[:, :ATT_KV_W])
    v3 = tile3(kv[:, ATT_KV_W:])
    pad_kv = jnp.zeros((nt, BLK - SUBLANES, LANES), BF16)

    lo256, _ = _half_masks(RET_QK_W)
    qr3 = tile3(qkr[:, :RET_QK_W])
    kr3 = tile3(qkr[:, RET_QK_W:] * (RET_QK_DIM ** -0.5))
    vr3 = tile3(vr)
    lane256 = lax.broadcasted_iota(jnp.int32, (1, 1, RET_QK_W), 2)
    qrs = jnp.concatenate(
        [(qr3 * ((lane256 >= h * RET_QK_DIM) & (lane256 < (h + 1) * RET_QK_DIM)).astype(F32)).astype(BF16)
         for h in range(N_RET_HEADS)],
        axis=1)
    kr_pad = jnp.concatenate([kr3.astype(BF16), jnp.zeros((nt, BLK - SUBLANES, RET_QK_W), BF16)], axis=1)
    vr_pad = jnp.concatenate([vr3.astype(BF16), jnp.zeros((nt, BLK - SUBLANES, RET_V_W), BF16)], axis=1)

    lane = lax.broadcasted_iota(jnp.int32, (1, 1, LANES), 2)
    row8 = lax.broadcasted_iota(jnp.int32, (1, SUBLANES, 1), 1)
    bmm_nt = lambda a, b: jnp.einsum('bqd,bkd->bqk', a, b, preferred_element_type=F32)
    bmm = lambda a, b: jnp.einsum('bqk,bkd->bqd', a, b, preferred_element_type=F32)

    att_par, ret_par = [], []
    for par in range(2):
        bsl = pl.ds(par, nt, stride=2)
        ck = ck_ref[bsl]
        cv = cv_ref[bsl]
        swk_ref[bsl, 0:WINDOW - DEC_SEQ, :] = ck[:, DEC_SEQ:, :]
        swv_ref[bsl, 0:WINDOW - DEC_SEQ, :] = cv[:, DEC_SEQ:, :]
        swk_ref[bsl, WINDOW - DEC_SEQ:WINDOW, :] = k3[:, DEC_SEQ * par:DEC_SEQ * (par + 1), :]
        swv_ref[bsl, WINDOW - DEC_SEQ:WINDOW, :] = v3[:, DEC_SEQ * par:DEC_SEQ * (par + 1), :]

        kfull = jnp.concatenate([ck.astype(BF16), k3.astype(BF16), pad_kv], axis=1)
        vfull = jnp.concatenate([cv.astype(BF16), v3.astype(BF16), pad_kv], axis=1)
        s = bmm_nt(qs, kfull) + bias_ref[par]
        ps = []
        for h in range(N_ATT_HEADS):
            ps.append(_sink_softmax(s[:, h * SUBLANES:(h + 1) * SUBLANES, :], sinks_ref[h]).astype(BF16))
        o = bmm(jnp.concatenate(ps, axis=1), vfull)
        o_r = pltpu.roll(o.reshape(nt * N_ATT_HEADS * SUBLANES, LANES), HALF, axis=1).reshape(o.shape)
        hr = lambda a, h: a[:, h * SUBLANES:(h + 1) * SUBLANES, :]
        low = lane < HALF
        att_par.append(jnp.concatenate([
            jnp.where(low, hr(o, 0), hr(o_r, 1)), jnp.where(low, hr(o, 2), hr(o_r, 3)),
            jnp.where(low, hr(o_r, 4), hr(o, 5)), jnp.where(low, hr(o_r, 6), hr(o, 7))], axis=2))

        st = st_ref[bsl]
        oc = bmm(qrs, st.astype(BF16))
        inner = (bmm_nt(qrs, kr_pad) * dec_ref[par]).astype(BF16)
        oi = bmm(inner, vr_pad)
        rs = []
        for h in range(N_RET_HEADS):
            vsl = slice(h * RET_V_DIM, (h + 1) * RET_V_DIM)
            rsl = slice(h * SUBLANES, (h + 1) * SUBLANES)
            o_h = oi[:, rsl, vsl] + oc[:, rsl, :] * xi_ref[par, rsl, :]
            rs.append(_group_norm(o_h, gng_ref[:, vsl], gnb_ref[:, vsl]) * gate3[:, :, vsl])
        ret_par.append(jnp.concatenate(rs, axis=2))

        kz3 = (kr3 * zeta_ref[par]).astype(BF16)
        vr3_b = vr3.astype(BF16)
        for p in range(nt):
            for i in range(N_RET_HEADS // 2):
                u = _dot_tn(kz3[p][:, i * LANES:(i + 1) * LANES],
                            vr3_b[p][:, 2 * i * RET_V_DIM:(2 * i + 2) * RET_V_DIM])
                for half in range(2):
                    h = 2 * i + half
                    dsl = slice(h * RET_QK_DIM, (h + 1) * RET_QK_DIM)
                    sst_ref[2 * p + par, dsl, :] = (
                        _GL_SAMPLE[h] * st[p, dsl, :]
                        + u[half * RET_QK_DIM:(half + 1) * RET_QK_DIM, half * RET_V_DIM:(half + 1) * RET_V_DIM])

    own0 = row8 < DEC_SEQ
    att3 = jnp.where(own0, att_par[0], att_par[1])
    ret3 = jnp.where(own0, ret_par[0], ret_par[1])
    mix = jnp.concatenate([att3, ret3], axis=2).reshape(2 * nt * DEC_SEQ, MIX_OUT).astype(BF16)
    h_ref[...] = x + _dot(mix, wout_ref[...])


def _sample_mixer(x2d, g_mix, w_in, w_out, sinks, gn_g, gn_b, ck, cv, st):
    n, d = x2d.shape
    nb = ck.shape[0]
    bb = BB_MIX
    r = bb * DEC_SEQ
    const = lambda shape: pl.BlockSpec(shape, lambda i: (0,) * len(shape))
    row = pl.BlockSpec((r, d), lambda i: (i, 0))
    win = pl.BlockSpec((bb, WINDOW, ATT_KV_W), lambda i: (i, 0, 0))
    state = pl.BlockSpec((bb, RET_QK_W, RET_V_DIM), lambda i: (i, 0, 0))
    return pl.pallas_call(
        _sample_mixer_kernel,
        grid=(nb // bb,),
        in_specs=[
            pl.BlockSpec(memory_space=pltpu.SMEM),
            row, const((1, d)), const((d, D_IN)), const((MIX_OUT, d)),
            const((1, RET_V_W)), const((1, RET_V_W)),
            win, win, state,
            const(_S_BIAS.shape), const(_S_DEC.shape), const(_S_XI.shape), const(_S_ZETA.shape),
        ],
        out_specs=[row, win, win, state],
        out_shape=[
            jax.ShapeDtypeStruct((n, d), F32),
            jax.ShapeDtypeStruct((nb, WINDOW, ATT_KV_W), F32),
            jax.ShapeDtypeStruct((nb, WINDOW, ATT_KV_W), F32),
            jax.ShapeDtypeStruct((nb, RET_QK_W, RET_V_DIM), F32),
        ],
        compiler_params=pltpu.CompilerParams(
            dimension_semantics=("arbitrary",), vmem_limit_bytes=VMEM_LIMIT),
        name="sample_mixer",
    )(sinks, x2d, g_mix, w_in, w_out, gn_g, gn_b, ck, cv, st,
      jnp.asarray(_S_BIAS), jnp.asarray(_S_DEC), jnp.asarray(_S_XI), jnp.asarray(_S_ZETA))


def _head_slab(x_ref, b, hd):
    group = X_D_HALVES * N_X_HEADS
    halves = [x_ref[b, pl.ds(dh * N_X_HEADS + hd, N_MEM, stride=group), :] for dh in range(X_D_HALVES)]
    return jnp.concatenate(halves, axis=1).astype(BF16)


def _sample_xattn_kernel(h_ref, g_ref, wq_ref, wo_ref, xk_ref, xv_ref, out_ref):
    bb = xk_ref.shape[0]
    nt = bb // 2
    h = h_ref[...]
    xn = _rms(h, g_ref[...]).astype(BF16)
    q = _dot(xn, wq_ref[...]) * (X_HEAD_DIM ** -0.5)
    units = [(t, par, hd) for t in range(nt) for par in range(2) for hd in range(N_X_HEADS)]
    qts = [q[t * SUBLANES:(t + 1) * SUBLANES].astype(BF16) for t in range(nt)]
    s = jnp.concatenate(
        [_dot_nt(qts[t][:, hd * X_HEAD_DIM:(hd + 1) * X_HEAD_DIM], _head_slab(xk_ref, 2 * t + par, hd))
         for t, par, hd in units], axis=0)
    m = jnp.max(s, axis=-1, keepdims=True)
    p = jnp.exp(s - m)
    p = p * (1.0 / jnp.sum(p, axis=-1, keepdims=True))
    os_ = {}
    for i, (t, par, hd) in enumerate(units):
        pi = p[i * SUBLANES:(i + 1) * SUBLANES].astype(BF16)
        os_[(t, par, hd)] = _dot(pi, _head_slab(xv_ref, 2 * t + par, hd))
    own0 = lax.broadcasted_iota(jnp.int32, (SUBLANES, 1), 0) < DEC_SEQ
    o_tiles = []
    for t in range(nt):
        o_par = [jnp.concatenate([os_[(t, par, hd)] for hd in range(N_X_HEADS)], axis=1) for par in range(2)]
        o_tiles.append(jnp.where(own0, o_par[0], o_par[1]))
    o = jnp.concatenate(o_tiles, axis=0).astype(BF16)
    out_ref[...] = h + _dot(o, wo_ref[...])


def _mlp_value(h, g_ref, wup_ref, wdn_ref, gf_ref, fillers=None):
    xn = _rms(h, g_ref[...]).astype(BF16)
    acc = h
    for c in range(D_FF // FF_CHUNK):
        sl = slice(c * FF_CHUNK, (c + 1) * FF_CHUNK)
        u = jnp.maximum(_dot(xn, wup_ref[:, sl]), 0.0)
        acc = acc + _dot((u * u).astype(BF16), wdn_ref[sl, :])
        if fillers is not None:
            fillers[c]()
    return _rms(acc, gf_ref[...])


def _mlp_xattn_kernel(hp_ref, hsm_ref, gx_ref, wq_ref, wo_ref, xk_ref, xv_ref, g_ref, wup_ref, wdn_ref, gf_ref,
                      yp_ref, ys_ref):
    i = pl.program_id(0)
    n = pl.num_programs(0) - 1
    bb = xk_ref.shape[0]
    rows = bb * DEC_SEQ
    assert bb == D_FF // FF_CHUNK and bb % 2 == 0

    @pl.when(i == 0)
    def _():
        xn = _rms(hsm_ref[...], gx_ref[...]).astype(BF16)
        ys_ref[...] = _dot(xn, wq_ref[...]) * (X_HEAD_DIM ** -0.5)

    @pl.when(i < n)
    def _():
        r0 = pl.multiple_of(i * rows, rows)
        own0 = lax.broadcasted_iota(jnp.int32, (SUBLANES, 1), 0) < DEC_SEQ
        o_rows = {}

        def attend(b):
            t = b // 2
            qt = ys_ref[pl.ds(r0 + t * SUBLANES, SUBLANES), :].astype(BF16)
            s = jnp.concatenate(
                [_dot_nt(qt[:, hd * X_HEAD_DIM:(hd + 1) * X_HEAD_DIM], _head_slab(xk_ref, b, hd))
                 for hd in range(N_X_HEADS)], axis=0)
            m = jnp.max(s, axis=-1, keepdims=True)
            p = jnp.exp(s - m)
            p = p * (1.0 / jnp.sum(p, axis=-1, keepdims=True))
            o_rows[b] = jnp.concatenate(
                [_dot(p[hd * SUBLANES:(hd + 1) * SUBLANES].astype(BF16), _head_slab(xv_ref, b, hd))
                 for hd in range(N_X_HEADS)], axis=1)
            if b % 2 == 1:
                ys_ref[pl.ds(r0 + t * SUBLANES, SUBLANES), :] = jnp.where(own0, o_rows[b - 1], o_rows[b])

        fillers = [functools.partial(attend, b) for b in range(bb)]
        yp_ref[...] = _mlp_value(hp_ref[...], g_ref, wup_ref, wdn_ref, gf_ref, fillers)

    @pl.when(i == n)
    def _():
        hs = hsm_ref[...] + _dot(ys_ref[...].astype(BF16), wo_ref[...])
        ys_ref[...] = _mlp_value(hs, g_ref, wup_ref, wdn_ref, gf_ref)


def _mlp_xattn(hp2d, hsm, g_xattn, w_xq, w_xo, xk, xv, g_mlp, w_up, w_down, g_final):
    n, d = hp2d.shape
    ns = hsm.shape[0]
    nb = xk.shape[0]
    bb = BB_X
    tm = n // (nb // bb)
    n_tiles = n // tm
    assert n_tiles * bb == nb and tm % SUBLANES == 0
    clip = lambda i: jnp.minimum(i, n_tiles - 1)
    prompt = pl.BlockSpec((tm, d), lambda i: (clip(i), 0))
    mem = pl.BlockSpec((bb,) + xk.shape[1:], lambda i: (clip(i), 0, 0))
    const = lambda shape: pl.BlockSpec(shape, lambda i: (0,) * len(shape), pipeline_mode=pl.Buffered(1))
    return pl.pallas_call(
        _mlp_xattn_kernel,
        grid=(n_tiles + 1,),
        in_specs=[prompt, const((ns, d)), const((1, d)), const((d, d)), const((d, d)), mem, mem,
                  const((1, d)), const((d, D_FF)), const((D_FF, d)), const((1, d))],
        out_specs=[prompt, pl.BlockSpec((ns, d), lambda i: (0, 0))],
        out_shape=[jax.ShapeDtypeStruct((n, d), F32), jax.ShapeDtypeStruct((ns, d), F32)],
        compiler_params=pltpu.CompilerParams(
            dimension_semantics=("arbitrary",), vmem_limit_bytes=VMEM_LIMIT),
        name="mlp_xattn",
    )(hp2d, hsm, g_xattn, w_xq, w_xo, xk, xv, g_mlp, w_up, w_down, g_final)


def _mem_rows(c):
    nb = c.shape[0]
    c = c.reshape(nb, N_MEM, N_X_HEADS, X_D_HALVES, LANES)
    return jnp.transpose(c, (0, 1, 3, 2, 4)).reshape(nb, N_MEM * X_D_HALVES * N_X_HEADS, LANES)


def _sample_xattn(h2d, g, w_xq, w_xo, mk, mv):
    n, d = h2d.shape
    nb = mk.shape[0]
    bb = BB_X
    r = bb * DEC_SEQ
    const = lambda shape: pl.BlockSpec(shape, lambda i: (0,) * len(shape))
    row = pl.BlockSpec((r, d), lambda i: (i, 0))
    mem = pl.BlockSpec((bb,) + mk.shape[1:], lambda i: (i, 0, 0))
    return pl.pallas_call(
        _sample_xattn_kernel,
        grid=(nb // bb,),
        in_specs=[row, const((1, d)), const((d, d)), const((d, d)), mem, mem],
        out_specs=row,
        out_shape=jax.ShapeDtypeStruct((n, d), F32),
        compiler_params=pltpu.CompilerParams(
            dimension_semantics=("arbitrary",), vmem_limit_bytes=VMEM_LIMIT),
        name="sample_xattn",
    )(h2d, g, w_xq, w_xo, mk, mv)


def kernel(x_prompt, x_sample, mem_prompt, cache_win_k, cache_win_v, state_ret, cache_mem_k, cache_mem_v,
           g_mix, w_in, attn_sinks, ret_gn_g, ret_gn_b, w_out, g_xattn, g_mem, w_xq, w_xk, w_xv, w_xo,
           g_mlp, w_up, w_down, g_final):
    depth = w_in.shape[0]
    assert depth == 1, "single-layer trunk"
    b, s, d = x_prompt.shape
    nb, ls, _ = x_sample.shape
    row = lambda a: a.reshape(1, -1)
    sinks = attn_sinks[0]
    gn_g, gn_b = row(ret_gn_g[0]), row(ret_gn_b[0])
    g_fin = row(g_final)

    mk, mv, mkb, mvb, w_in_b, w_out_b = _memory_kv(
        mem_prompt.reshape(b * N_MEM, d), row(g_mem[0]), w_xk[0], w_xv[0], w_in[0], w_out[0])
    hp, p_wk, p_wv, p_rs, w_up_b, w_dn_b, w_xq_b, w_xo_b = _prompt_mixer_p(
        x_prompt, row(g_mix[0]), w_in_b, w_out_b, sinks, gn_g, gn_b,
        (w_up[0], w_down[0], w_xq[0], w_xo[0]))
    hp = _prompt_xattn(hp, row(g_xattn[0]), w_xq_b, w_xo_b,
                       mkb.reshape(b, N_MEM, d), mvb.reshape(b, N_MEM, d))

    hs, s_wk, s_wv, s_rs = _sample_mixer(
        x_sample.reshape(nb * ls, d), row(g_mix[0]), w_in_b, w_out_b, sinks, gn_g, gn_b,
        cache_win_k[0].reshape(nb, WINDOW, ATT_KV_W), cache_win_v[0].reshape(nb, WINDOW, ATT_KV_W),
        state_ret[0].reshape(nb, RET_QK_W, RET_V_DIM))

    y_prompt, y_sample = _mlp_xattn(
        hp.reshape(b * s, d), hs, row(g_xattn[0]), w_xq_b, w_xo_b,
        _mem_rows(cache_mem_k[0]), _mem_rows(cache_mem_v[0]), row(g_mlp[0]), w_up_b, w_dn_b, g_fin)
    y_prompt = y_prompt.reshape(b, s, d)
    y_sample = y_sample.reshape(nb, ls, d)

    win5 = lambda a, n: a.reshape(1, n, WINDOW, N_KV_HEADS, HEAD_DIM)
    ret5 = lambda a, n: a.reshape(1, n, N_RET_HEADS, RET_QK_DIM, RET_V_DIM)
    mem5 = lambda a: jnp.transpose(a.reshape(b, N_MEM, X_D_HALVES, N_X_HEADS, LANES),
                                   (0, 1, 3, 2, 4)).reshape(1, b, N_MEM, N_X_HEADS, X_HEAD_DIM)
    return (y_prompt, y_sample,
            win5(p_wk, b), win5(p_wv, b), ret5(p_rs, b), mem5(mk), mem5(mv),
            win5(s_wk, nb), win5(s_wv, nb), ret5(s_rs, nb))
```

```python
import functools

import jax
import jax.numpy as jnp
import numpy as np
from jax import lax
from jax.experimental import pallas as pl
from jax.experimental.pallas import tpu as pltpu

F32 = jnp.float32
BF16 = jnp.bfloat16

D_MODEL = 1024
BATCH = 8
SEQ = 2048
DEC_BATCH = 128
DEC_SEQ = 4
HEAD_DIM = 64
N_ATT_HEADS = 8
N_KV_HEADS = 2
KV_GROUP = N_ATT_HEADS // N_KV_HEADS
WINDOW = 128
BLK = 128
N_RET_HEADS = 4
RET_QK_DIM = 64
RET_V_DIM = 128
N_MEM = 256
N_X_HEADS = 4
X_HEAD_DIM = D_MODEL // N_X_HEADS
D_FF = 4 * D_MODEL
RMS_EPS = 1e-6
GN_EPS = 1e-5

ATT_Q_W = N_ATT_HEADS * HEAD_DIM
ATT_KV_W = N_KV_HEADS * HEAD_DIM
RET_QK_W = N_RET_HEADS * RET_QK_DIM
RET_V_W = N_RET_HEADS * RET_V_DIM
MIX_OUT = ATT_Q_W + RET_V_W
D_IN = ATT_Q_W + 2 * ATT_KV_W + 2 * RET_QK_W + 2 * RET_V_W
C_QA, C_KV, C_QKR, C_VR, C_GR = 0, 512, 768, 1280, 1792

LANES = 128
SUBLANES = 8
HALF = LANES // 2
X_D_HALVES = X_HEAD_DIM // LANES
NEG = -1e30
VMEM_LIMIT = 56 * 1024 * 1024

TM_MIX = 512
TM_X = 2048
TM_MLP = 1024
SUB_ROWS = 512
FF_CHUNK = 1024
BB_MIX = 16
BB_X = 4

NEG_SLOPES = [-(2.0 ** (-8.0 * (i + 1) / N_ATT_HEADS)) for i in range(N_ATT_HEADS)]
_LOG_G = np.log(1.0 - 2.0 ** (-5.0 - np.arange(N_RET_HEADS))).astype(np.float32).astype(np.float64)


def _prompt_tables():
    qi = np.arange(BLK)[:, None]
    kj = np.arange(2 * BLK)[None, :]
    dist = (qi + BLK - kj).astype(np.float64)
    mask = np.where((dist >= 0) & (dist < WINDOW), 0.0, NEG)
    l = np.arange(BLK, dtype=np.float64)
    diff = l[:, None] - l[None, :]
    decay = np.where(diff >= 0, np.exp(_LOG_G[:, None, None] * np.maximum(diff, 0.0)), 0.0)
    xi = np.exp((l[:, None] + 1.0) * _LOG_G[None, :])
    zeta = np.exp((BLK - 1.0 - l)[:, None] * _LOG_G[None, :])
    xi_t = np.repeat(xi, RET_V_DIM, axis=1)
    zeta_t = np.repeat(zeta, RET_QK_DIM, axis=1)
    f = lambda a: np.asarray(a, np.float32)
    return f(dist), f(mask), f(decay), f(xi_t), f(zeta_t)


def _sample_tables():
    slopes = -np.asarray(NEG_SLOPES)
    bias = np.full((2, N_ATT_HEADS * SUBLANES, 2 * BLK), NEG, np.float64)
    dec = np.zeros((2, N_RET_HEADS * SUBLANES, BLK), np.float64)
    xi = np.zeros((2, N_RET_HEADS * SUBLANES, RET_V_DIM), np.float64)
    zeta = np.zeros((2, SUBLANES, RET_QK_W), np.float64)
    for par in range(2):
        for r in range(SUBLANES):
            own = DEC_SEQ * par <= r < DEC_SEQ * (par + 1)
            t = r - DEC_SEQ * par if own else r % DEC_SEQ
            for h in range(N_ATT_HEADS):
                row = h * SUBLANES + r
                for j in range(WINDOW):
                    d = t + WINDOW - j
                    if 0 <= d < WINDOW:
                        bias[par, row, j] = -slopes[h] * d
                for c in range(DEC_SEQ):
                    d = t - c
                    if d >= 0:
                        bias[par, row, WINDOW + DEC_SEQ * par + c] = -slopes[h] * d
            for h in range(N_RET_HEADS):
                row = h * SUBLANES + r
                if own:
                    xi[par, row, :] = np.exp((t + 1.0) * _LOG_G[h])
                    zeta[par, r, h * RET_QK_DIM:(h + 1) * RET_QK_DIM] = np.exp((DEC_SEQ - 1.0 - t) * _LOG_G[h])
                    for c in range(t + 1):
                        dec[par, row, DEC_SEQ * par + c] = np.exp(_LOG_G[h] * (t - c))
    f = lambda a: np.asarray(a, np.float32)
    return f(bias), f(dec), f(xi), f(zeta)


_P_DIST, _P_MASK, _P_DECAY, _P_XI, _P_ZETA = _prompt_tables()
_S_BIAS, _S_DEC, _S_XI, _S_ZETA = _sample_tables()
_GL_PROMPT = [float(np.exp(_LOG_G[h] * BLK)) for h in range(N_RET_HEADS)]
_GL_SAMPLE = [float(np.exp(_LOG_G[h] * DEC_SEQ)) for h in range(N_RET_HEADS)]


def _rms(x, g):
    return x * lax.rsqrt(jnp.mean(x * x, axis=-1, keepdims=True) + RMS_EPS) * g


def _dot(a, b):
    return jnp.dot(a, b, preferred_element_type=F32)


def _dot_nt(a, b):
    return lax.dot_general(a, b, (((1,), (1,)), ((), ())), preferred_element_type=F32)


def _dot_tn(a, b):
    return lax.dot_general(a, b, (((0,), (0,)), ((), ())), preferred_element_type=F32)


def _silu(g):
    return g * (1.0 / (1.0 + jnp.exp(-g)))


def _half_masks(width):
    lane = lax.broadcasted_iota(jnp.int32, (1, width), 1)
    lo = ((lane & (LANES - 1)) < HALF).astype(F32)
    return lo, 1.0 - lo


def _sink_softmax(s, sink):
    m = jnp.maximum(jnp.max(s, axis=-1, keepdims=True), sink)
    p = jnp.exp(s - m)
    den = jnp.sum(p, axis=-1, keepdims=True) + jnp.exp(sink - m)
    return p * (1.0 / den)


def _group_norm(o, g, b):
    mu = jnp.mean(o, axis=-1, keepdims=True)
    d = o - mu
    var = jnp.mean(d * d, axis=-1, keepdims=True)
    return d * lax.rsqrt(var + GN_EPS) * g + b


def _prompt_mixer_kernel(sinks_ref, x_ref, gmix_ref, win_ref, wout_ref, gng_ref, gnb_ref,
                         dist_ref, mask_ref, decay_ref, xi_ref, zeta_ref,
                         h_ref, wk_ref, wv_ref, st_ref,
                         qlo_s, qhi_s, kd0_s, kd1_s, vd0_s, vd1_s,
                         qrlo_s, qrhi_s, kr_s, vr_s, gate_s, mix_s, bias_s):
    t = pl.program_id(1)
    nt = pl.num_programs(1)
    tm = x_ref.shape[1]
    nblk = tm // BLK

    @pl.when(t == 0)
    def _():
        kd0_s[0:BLK, :] = jnp.zeros((BLK, LANES), BF16)
        kd1_s[0:BLK, :] = jnp.zeros((BLK, LANES), BF16)
        vd0_s[0:BLK, :] = jnp.zeros((BLK, 2 * LANES), BF16)
        vd1_s[0:BLK, :] = jnp.zeros((BLK, 2 * LANES), BF16)
        st_ref[...] = jnp.zeros_like(st_ref)

    @pl.when(t > 0)
    def _():
        kd0_s[0:BLK, :] = kd0_s[tm:tm + BLK, :]
        kd1_s[0:BLK, :] = kd1_s[tm:tm + BLK, :]
        vd0_s[0:BLK, :] = vd0_s[tm:tm + BLK, :]
        vd1_s[0:BLK, :] = vd1_s[tm:tm + BLK, :]

    @pl.when((t == 0) & (pl.program_id(0) == 0))
    def _():
        for h in range(N_ATT_HEADS):
            bias_s[h] = NEG_SLOPES[h] * dist_ref[...] + mask_ref[...]

    x = x_ref[0]
    xn = _rms(x, gmix_ref[...]).astype(BF16)

    z = _dot(xn, win_ref[...])
    lo512, hi512 = _half_masks(ATT_Q_W)
    q = z[:, C_QA:C_QA + ATT_Q_W]
    qlo_s[...] = (q * (lo512 * HEAD_DIM ** -0.5)).astype(BF16)
    qhi_s[...] = (q * (hi512 * HEAD_DIM ** -0.5)).astype(BF16)

    low = lax.broadcasted_iota(jnp.int32, (tm, LANES), 1) < HALF
    kv = z[:, C_KV:C_KV + 2 * ATT_KV_W]
    k = kv[:, :ATT_KV_W]
    v = kv[:, ATT_KV_W:]
    k_r = pltpu.roll(k, HALF, axis=1)
    v_r = pltpu.roll(v, HALF, axis=1)
    kd0_s[BLK:BLK + tm, :] = jnp.where(low, k, k_r).astype(BF16)
    kd1_s[BLK:BLK + tm, :] = jnp.where(low, k_r, k).astype(BF16)
    vd0_s[BLK:BLK + tm, 0:LANES] = jnp.where(low, v, 1.0).astype(BF16)
    vd0_s[BLK:BLK + tm, LANES:2 * LANES] = jnp.where(low, 1.0, v_r).astype(BF16)
    vd1_s[BLK:BLK + tm, 0:LANES] = jnp.where(low, v_r, 1.0).astype(BF16)
    vd1_s[BLK:BLK + tm, LANES:2 * LANES] = jnp.where(low, 1.0, v).astype(BF16)

    @pl.when(t == nt - 1)
    def _():
        wk_ref[0] = k[tm - WINDOW:, :]
        wv_ref[0] = v[tm - WINDOW:, :]

    lo256, hi256 = _half_masks(RET_QK_W)
    qkr = z[:, C_QKR:C_QKR + 2 * RET_QK_W]
    qr = qkr[:, :RET_QK_W]
    qrlo_s[...] = (qr * lo256).astype(BF16)
    qrhi_s[...] = (qr * hi256).astype(BF16)
    kr_s[...] = qkr[:, RET_QK_W:] * (RET_QK_DIM ** -0.5)
    vr_s[...] = z[:, C_VR:C_VR + RET_V_W].astype(BF16)
    gate_s[...] = _silu(z[:, C_GR:C_GR + RET_V_W])

    lowb =lax.broadcasted_iota(jnp.int32, (BLK, LANES), 1) < HALF
    col = lax.broadcasted_iota(jnp.int32, (BLK, 2 * BLK), 1)
    first_mask = jnp.where((col < BLK) & (t == 0), NEG, 0.0)
    kd_refs = (kd0_s, kd1_s)
    vd_refs = (vd0_s, vd1_s)
    n_pairs = N_RET_HEADS // 2
    state = [st_ref[0, i * LANES:(i + 1) * LANES, :] for i in range(n_pairs)]

    for j in range(nblk):
        rows = slice(j * BLK, (j + 1) * BLK)
        krows = slice(j * BLK, (j + 2) * BLK)

        for kvh in range(N_KV_HEADS):
            kd = kd_refs[kvh][krows, :]
            vd = vd_refs[kvh][krows, :]
            c0 = kvh * KV_GROUP * HEAD_DIM
            qst = jnp.concatenate([qlo_s[rows, c0:c0 + LANES], qhi_s[rows, c0:c0 + LANES],
                                   qlo_s[rows, c0 + LANES:c0 + 2 * LANES],
                                   qhi_s[rows, c0 + LANES:c0 + 2 * LANES]], axis=0)
            s = _dot_nt(qst, kd)
            es, esink = [], []
            for g in range(KV_GROUP):
                h = kvh * KV_GROUP + g
                sg = s[g * BLK:(g + 1) * BLK] + bias_s[h]
                if j == 0:
                    sg = sg + first_mask
                sink = sinks_ref[h]
                m = jnp.maximum(jnp.max(sg, axis=-1, keepdims=True), sink)
                es.append(jnp.exp(sg - m).astype(BF16))
                esink.append(jnp.exp(sink - m))
            o = _dot(jnp.concatenate(es, axis=0), vd)
            for pair in range(KV_GROUP // 2):
                oe = o[2 * pair * BLK:(2 * pair + 1) * BLK]
                oo = o[(2 * pair + 1) * BLK:(2 * pair + 2) * BLK]
                num = jnp.where(lowb, oe[:, :LANES], oo[:, LANES:])
                den = (jnp.where(lowb, oe[:, LANES:], oo[:, :LANES])
                       + jnp.where(lowb, esink[2 * pair], esink[2 * pair + 1]))
                cs = c0 + pair * LANES
                mix_s[rows, cs:cs + LANES] = (num * (1.0 / den)).astype(BF16)

        for i in range(n_pairs):
            lsl = slice(i * LANES, (i + 1) * LANES)
            kp = kr_s[rows, lsl]
            sp = state[i]
            vpair = vr_s[rows, 2 * i * RET_V_DIM:(2 * i + 2) * RET_V_DIM]
            q2 = jnp.concatenate([qrlo_s[rows, lsl], qrhi_s[rows, lsl]], axis=0)
            a = _dot_nt(q2, kp.astype(BF16))
            inner = jnp.concatenate([a[:BLK] * decay_ref[2 * i], a[BLK:] * decay_ref[2 * i + 1]], axis=0)
            oi = _dot(inner.astype(BF16), vpair)
            oc = _dot(q2, sp.astype(BF16))
            for half in range(2):
                h = 2 * i + half
                vsl = slice(h * RET_V_DIM, (h + 1) * RET_V_DIM)
                hr = slice(half * BLK, (half + 1) * BLK)
                o = oi[hr, half * RET_V_DIM:(half + 1) * RET_V_DIM] + oc[hr] * xi_ref[:, vsl]
                r = _group_norm(o, gng_ref[:, vsl], gnb_ref[:, vsl]) * gate_s[rows, vsl]
                mix_s[rows, ATT_Q_W + h * RET_V_DIM:ATT_Q_W + (h + 1) * RET_V_DIM] = r.astype(BF16)
            kz = (kp * zeta_ref[:, lsl]).astype(BF16)
            u = _dot_tn(kz, vpair)
            state[i] = jnp.concatenate(
                [_GL_PROMPT[2 * i] * sp[:RET_QK_DIM] + u[:RET_QK_DIM, :RET_V_DIM],
                 _GL_PROMPT[2 * i + 1] * sp[RET_QK_DIM:] + u[RET_QK_DIM:, RET_V_DIM:]], axis=0)

    for i in range(n_pairs):
        st_ref[0, i * LANES:(i + 1) * LANES, :] = state[i]

    h_ref[0] = x + _dot(mix_s[...], wout_ref[...])


def _prompt_mixer(x, g_mix, w_in, w_out, sinks, gn_g, gn_b):
    b, s, d = x.shape
    tm = TM_MIX
    const = lambda shape: pl.BlockSpec(shape, lambda i, j: (0,) * len(shape))
    return pl.pallas_call(
        _prompt_mixer_kernel,
        grid=(b, s // tm),
        in_specs=[
            pl.BlockSpec(memory_space=pltpu.SMEM),
            pl.BlockSpec((1, tm, d), lambda i, j: (i, j, 0)),
            const((1, d)), const((d, D_IN)), const((MIX_OUT, d)),
            const((1, RET_V_W)), const((1, RET_V_W)),
            const((BLK, 2 * BLK)), const((BLK, 2 * BLK)),
            const((N_RET_HEADS, BLK, BLK)), const((BLK, RET_V_W)), const((BLK, RET_QK_W)),
        ],
        out_specs=[
            pl.BlockSpec((1, tm, d), lambda i, j: (i, j, 0)),
            pl.BlockSpec((1, WINDOW, ATT_KV_W), lambda i, j: (i, 0, 0)),
            pl.BlockSpec((1, WINDOW, ATT_KV_W), lambda i, j: (i, 0, 0)),
            pl.BlockSpec((1, RET_QK_W, RET_V_DIM), lambda i, j: (i, 0, 0)),
        ],
        out_shape=[
            jax.ShapeDtypeStruct((b, s, d), F32),
            jax.ShapeDtypeStruct((b, WINDOW, ATT_KV_W), F32),
            jax.ShapeDtypeStruct((b, WINDOW, ATT_KV_W), F32),
            jax.ShapeDtypeStruct((b, RET_QK_W, RET_V_DIM), F32),
        ],
        scratch_shapes=[
            pltpu.VMEM((tm, ATT_Q_W), BF16), pltpu.VMEM((tm, ATT_Q_W), BF16),
            pltpu.VMEM((tm + BLK, LANES), BF16), pltpu.VMEM((tm + BLK, LANES), BF16),
            pltpu.VMEM((tm + BLK, 2 * LANES), BF16), pltpu.VMEM((tm + BLK, 2 * LANES), BF16),
            pltpu.VMEM((tm, RET_QK_W), BF16), pltpu.VMEM((tm, RET_QK_W), BF16),
            pltpu.VMEM((tm, RET_QK_W), F32), pltpu.VMEM((tm, RET_V_W), BF16),
            pltpu.VMEM((tm, RET_V_W), F32), pltpu.VMEM((tm, MIX_OUT), BF16),
            pltpu.VMEM((N_ATT_HEADS, BLK, 2 * BLK), F32),
        ],
        compiler_params=pltpu.CompilerParams(
            dimension_semantics=("arbitrary", "arbitrary"), vmem_limit_bytes=VMEM_LIMIT),
        name="prompt_mixer",
    )(sinks, x, g_mix, w_in, w_out, gn_g, gn_b,
      jnp.asarray(_P_DIST), jnp.asarray(_P_MASK), jnp.asarray(_P_DECAY), jnp.asarray(_P_XI),
      jnp.asarray(_P_ZETA))


def _pm_project_stages(x, slot, gmix_ref, win_ref, sc, kv_out=None):
    tm = x.shape[0]
    xn = _rms(x, gmix_ref[...]).astype(BF16)

    def stage_q():
        q = _dot(xn, win_ref[:, C_QA:C_QA + ATT_Q_W])
        lo512, hi512 = _half_masks(ATT_Q_W)
        sc["qlo"][slot] = (q * (lo512 * HEAD_DIM ** -0.5)).astype(BF16)
        sc["qhi"][slot] = (q * (hi512 * HEAD_DIM ** -0.5)).astype(BF16)

    def stage_kv_qkr():
        z = _dot(xn, win_ref[:, C_KV:C_VR])
        low = lax.broadcasted_iota(jnp.int32, (tm, LANES), 1) < HALF
        k = z[:, 0:ATT_KV_W]
        v = z[:, ATT_KV_W:2 * ATT_KV_W]
        if kv_out is not None:
            kv_out[0][0] = k[tm - WINDOW:, :]
            kv_out[1][0] = v[tm - WINDOW:, :]
        k_r = pltpu.roll(k, HALF, axis=1)
        v_r = pltpu.roll(v, HALF, axis=1)
        sc["kd0"][slot] = jnp.where(low, k, k_r).astype(BF16)
        sc["kd1"][slot] = jnp.where(low, k_r, k).astype(BF16)
        sc["vd0"][slot, :, 0:LANES] = jnp.where(low, v, 1.0).astype(BF16)
        sc["vd0"][slot, :, LANES:2 * LANES] = jnp.where(low, 1.0, v_r).astype(BF16)
        sc["vd1"][slot, :, 0:LANES] = jnp.where(low, v_r, 1.0).astype(BF16)
        sc["vd1"][slot, :, LANES:2 * LANES] = jnp.where(low, 1.0, v).astype(BF16)
        lo256, hi256 = _half_masks(RET_QK_W)
        qr = z[:, 2 * ATT_KV_W:2 * ATT_KV_W + RET_QK_W]
        sc["qrlo"][slot] = (qr * lo256).astype(BF16)
        sc["qrhi"][slot] = (qr * hi256).astype(BF16)
        sc["kr"][slot] = z[:, 2 * ATT_KV_W + RET_QK_W:] * (RET_QK_DIM ** -0.5)

    def stage_vr():
        sc["vr"][slot] = _dot(xn, win_ref[:, C_VR:C_VR + RET_V_W]).astype(BF16)

    def stage_gate():
        sc["gate"][slot] = _silu(_dot(xn, win_ref[:, C_GR:C_GR + RET_V_W]))

    return [stage_q, stage_kv_qkr, stage_vr, stage_gate]


def _pm_last_block(slot, tm, sc):
    rows = slice(tm - BLK, tm)
    return ([sc["kd0"][slot, rows, :], sc["kd1"][slot, rows, :]],
            [sc["vd0"][slot, rows, :], sc["vd1"][slot, rows, :]])


def _pm_blocks(slot, prev_kd, prev_vd, is_first, state, x, fillers, tm, sinks_ref, wout_ref, gng_ref, gnb_ref,
               decay_ref, xi_ref, zeta_ref, sc):
    nblk = tm // BLK
    lowb = lax.broadcasted_iota(jnp.int32, (BLK, LANES), 1) < HALF
    col = lax.broadcasted_iota(jnp.int32, (BLK, 2 * BLK), 1)
    first_mask = None if is_first is False else jnp.where((col < BLK) & is_first, NEG, 0.0)
    kd_refs = (sc["kd0"], sc["kd1"])
    vd_refs = (sc["vd0"], sc["vd1"])
    qlo, qhi, mix = sc["qlo"], sc["qhi"], sc["mix"]
    n_pairs = N_RET_HEADS // 2

    for j in range(nblk):
        rows = slice(j * BLK, (j + 1) * BLK)
        for kvh in range(N_KV_HEADS):
            if j == 0:
                kd = jnp.concatenate([prev_kd[kvh], kd_refs[kvh][slot, rows, :]], axis=0)
                vd = jnp.concatenate([prev_vd[kvh], vd_refs[kvh][slot, rows, :]], axis=0)
            else:
                krows = slice((j - 1) * BLK, (j + 1) * BLK)
                kd = kd_refs[kvh][slot, krows, :]
                vd = vd_refs[kvh][slot, krows, :]
            c0 = kvh * KV_GROUP * HEAD_DIM
            qst = jnp.concatenate([qlo[slot, rows, c0:c0 + LANES], qhi[slot, rows, c0:c0 + LANES],
                                   qlo[slot, rows, c0 + LANES:c0 + 2 * LANES],
                                   qhi[slot, rows, c0 + LANES:c0 + 2 * LANES]], axis=0)
            s = _dot_nt(qst, kd)
            es, esink = [], []
            for g in range(KV_GROUP):
                h = kvh * KV_GROUP + g
                sg = s[g * BLK:(g + 1) * BLK] + sc["bias"][h]
                if j == 0 and first_mask is not None:
                    sg = sg + first_mask
                sink = sinks_ref[h]
                m = jnp.maximum(jnp.max(sg, axis=-1, keepdims=True), sink)
                es.append(jnp.exp(sg - m).astype(BF16))
                esink.append(jnp.exp(sink - m))
            o = _dot(jnp.concatenate(es, axis=0), vd)
            for pair in range(KV_GROUP // 2):
                oe = o[2 * pair * BLK:(2 * pair + 1) * BLK]
                oo = o[(2 * pair + 1) * BLK:(2 * pair + 2) * BLK]
                num = jnp.where(lowb, oe[:, :LANES], oo[:, LANES:])
                den = (jnp.where(lowb, oe[:, LANES:], oo[:, :LANES])
                       + jnp.where(lowb, esink[2 * pair], esink[2 * pair + 1]))
                cs = c0 + pair * LANES
                mix[slot, rows, cs:cs + LANES] = (num * (1.0 / den)).astype(BF16)

        for i in range(n_pairs):
            lsl = slice(i * LANES, (i + 1) * LANES)
            kp = sc["kr"][slot, rows, lsl]
            sp = state[i]
            vpair = sc["vr"][slot, rows, 2 * i * RET_V_DIM:(2 * i + 2) * RET_V_DIM]
            q2 = jnp.concatenate([sc["qrlo"][slot, rows, lsl], sc["qrhi"][slot, rows, lsl]], axis=0)
            a = _dot_nt(q2, kp.astype(BF16))
            inner = jnp.concatenate([a[:BLK] * decay_ref[2 * i], a[BLK:] * decay_ref[2 * i + 1]], axis=0)
            oi = _dot(inner.astype(BF16), vpair)
            oc = _dot(q2, sp.astype(BF16))
            for half in range(2):
                h = 2 * i + half
                vsl = slice(h * RET_V_DIM, (h + 1) * RET_V_DIM)
                hr = slice(half * BLK, (half + 1) * BLK)
                o = oi[hr, half * RET_V_DIM:(half + 1) * RET_V_DIM] + oc[hr] * xi_ref[:, vsl]
                r = _group_norm(o, gng_ref[:, vsl], gnb_ref[:, vsl]) * sc["gate"][slot, rows, vsl]
                mix[slot, rows, ATT_Q_W + h * RET_V_DIM:ATT_Q_W + (h + 1) * RET_V_DIM] = r.astype(BF16)
            kz = (kp * zeta_ref[:, lsl]).astype(BF16)
            u = _dot_tn(kz, vpair)
            state[i] = jnp.concatenate(
                [_GL_PROMPT[2 * i] * sp[:RET_QK_DIM] + u[:RET_QK_DIM, :RET_V_DIM],
                 _GL_PROMPT[2 * i + 1] * sp[RET_QK_DIM:] + u[RET_QK_DIM:, RET_V_DIM:]], axis=0)

        fillers[j]()

    return x + _dot(mix[slot], wout_ref[...]), state


def _prompt_mixer_kernel_p(sinks_ref, xpair_ref, xnext_ref, gmix_ref, win_ref, wout_ref, gng_ref, gnb_ref,
                           dist_ref, mask_ref, decay_ref, xi_ref, zeta_ref,
                           wupf_ref, wdnf_ref, wqf_ref, wof_ref,
                           h_ref, wk_ref, wv_ref, st_ref,
                           wupb_ref, wdnb_ref, wqb_ref, wob_ref,
                           qlo_s, qhi_s, kd0_s, kd1_s, vd0_s, vd1_s,
                           qrlo_s, qrhi_s, kr_s, vr_s, gate_s, mix_s, bias_s, state_s):
    u = pl.program_id(0)
    tm = xnext_ref.shape[0]
    wupb_ref[...] = wupf_ref[...].astype(BF16)
    wdnb_ref[...] = wdnf_ref[...].astype(BF16)
    wqb_ref[...] = wqf_ref[...].astype(BF16)
    wob_ref[...] = wof_ref[...].astype(BF16)
    sc = dict(qlo=qlo_s, qhi=qhi_s, kd0=kd0_s, kd1=kd1_s, vd0=vd0_s, vd1=vd1_s, qrlo=qrlo_s, qrhi=qrhi_s,
              kr=kr_s, vr=vr_s, gate=gate_s, mix=mix_s, bias=bias_s)
    n_pairs = N_RET_HEADS // 2
    blocks = functools.partial(_pm_blocks, tm=tm, sinks_ref=sinks_ref, wout_ref=wout_ref, gng_ref=gng_ref,
                               gnb_ref=gnb_ref, decay_ref=decay_ref, xi_ref=xi_ref, zeta_ref=zeta_ref, sc=sc)

    @pl.when(u == 0)
    def _():
        for h in range(N_ATT_HEADS):
            bias_s[h] = NEG_SLOPES[h] * dist_ref[...] + mask_ref[...]
        state_s[...] = jnp.zeros_like(state_s)
        kd0_s[1] = jnp.zeros(kd0_s.shape[1:], BF16)
        kd1_s[1] = jnp.zeros(kd1_s.shape[1:], BF16)
        vd0_s[1] = jnp.zeros(vd0_s.shape[1:], BF16)
        vd1_s[1] = jnp.zeros(vd1_s.shape[1:], BF16)
        for stage in _pm_project_stages(xpair_ref[0:tm, :], 0, gmix_ref, win_ref, sc):
            stage()

    seq_start = (u % 2) == 0
    state = [jnp.where(seq_start, 0.0, state_s[i * LANES:(i + 1) * LANES, :]) for i in range(n_pairs)]

    prev_kd, prev_vd = _pm_last_block(1, tm, sc)
    stages = _pm_project_stages(xpair_ref[tm:2 * tm, :], 1, gmix_ref, win_ref, sc, kv_out=(wk_ref, wv_ref))
    h0, state = blocks(0, prev_kd, prev_vd, seq_start, state, xpair_ref[0:tm, :], stages)
    h_ref[0:tm, :] = h0

    prev_kd, prev_vd = _pm_last_block(0, tm, sc)
    stages = _pm_project_stages(xnext_ref[...], 0, gmix_ref, win_ref, sc)
    h1, state = blocks(1, prev_kd, prev_vd, False, state, xpair_ref[tm:2 * tm, :], stages)
    h_ref[tm:2 * tm, :] = h1

    for i in range(n_pairs):
        state_s[i * LANES:(i + 1) * LANES, :] = state[i]
        st_ref[0, i * LANES:(i + 1) * LANES, :] = state[i]


def _prompt_mixer_p(x, g_mix, w_in, w_out, sinks, gn_g, gn_b, side_f32):
    b, s, d = x.shape
    tm = TM_MIX
    n_tiles = b * s // tm
    steps = n_tiles // 2
    seq_steps = s // (2 * tm)
    assert s % (2 * tm) == 0 and seq_steps == 2, "kernel assumes 4 tiles per sequence"
    x2d = x.reshape(b * s, d)
    const = lambda shape: pl.BlockSpec(shape, lambda i: (0,) * len(shape), pipeline_mode=pl.Buffered(1))
    slot2 = lambda rows, cols, dt: pltpu.VMEM((2, rows, cols), dt)
    side_specs = [pl.BlockSpec((w.shape[0] // steps, w.shape[1]), lambda i: (i, 0)) for w in side_f32]
    outs = pl.pallas_call(
        _prompt_mixer_kernel_p,
        grid=(steps,),
        in_specs=[
            pl.BlockSpec(memory_space=pltpu.SMEM),
            pl.BlockSpec((2 * tm, d), lambda i: (i, 0)),
            pl.BlockSpec((tm, d), lambda i: (jnp.minimum(2 * i + 2, n_tiles - 1), 0)),
            const((1, d)), const((d, D_IN)), const((MIX_OUT, d)),
            const((1, RET_V_W)), const((1, RET_V_W)),
            const((BLK, 2 * BLK)), const((BLK, 2 * BLK)),
            const((N_RET_HEADS, BLK, BLK)), const((BLK, RET_V_W)), const((BLK, RET_QK_W)),
        ] + side_specs,
        out_specs=[
            pl.BlockSpec((2 * tm, d), lambda i: (i, 0)),
            pl.BlockSpec((1, WINDOW, ATT_KV_W), lambda i: (i // seq_steps, 0, 0)),
            pl.BlockSpec((1, WINDOW, ATT_KV_W), lambda i: (i // seq_steps, 0, 0)),
            pl.BlockSpec((1, RET_QK_W, RET_V_DIM), lambda i: (i // seq_steps, 0, 0)),
        ] + side_specs,
        out_shape=[
            jax.ShapeDtypeStruct((b * s, d), F32),
            jax.ShapeDtypeStruct((b, WINDOW, ATT_KV_W), F32),
            jax.ShapeDtypeStruct((b, WINDOW, ATT_KV_W), F32),
            jax.ShapeDtypeStruct((b, RET_QK_W, RET_V_DIM), F32),
        ] + [jax.ShapeDtypeStruct(w.shape, BF16) for w in side_f32],
        scratch_shapes=[
            slot2(tm, ATT_Q_W, BF16), slot2(tm, ATT_Q_W, BF16),
            slot2(tm, LANES, BF16), slot2(tm, LANES, BF16),
            slot2(tm, 2 * LANES, BF16), slot2(tm, 2 * LANES, BF16),
            slot2(tm, RET_QK_W, BF16), slot2(tm, RET_QK_W, BF16),
            slot2(tm, RET_QK_W, F32), slot2(tm, RET_V_W, BF16),
            slot2(tm, RET_V_W, F32), slot2(tm, MIX_OUT, BF16),
            pltpu.VMEM((N_ATT_HEADS, BLK, 2 * BLK), F32),
            pltpu.VMEM((RET_QK_W, RET_V_DIM), F32),
        ],
        compiler_params=pltpu.CompilerParams(
            dimension_semantics=("arbitrary",), vmem_limit_bytes=VMEM_LIMIT),
        name="prompt_mixer",
    )(sinks, x2d, x2d, g_mix, w_in, w_out, gn_g, gn_b,
      jnp.asarray(_P_DIST), jnp.asarray(_P_MASK), jnp.asarray(_P_DECAY), jnp.asarray(_P_XI),
      jnp.asarray(_P_ZETA), *side_f32)
    return (outs[0].reshape(b, s, d),) + tuple(outs[1:])


def _memkv_kernel(mem_ref, g_ref, wk_ref, wv_ref, win_ref, wout_ref,
                  mk_ref, mv_ref, mkb_ref, mvb_ref, winb_ref, woutb_ref):
    winb_ref[...] = win_ref[...].astype(BF16)
    woutb_ref[...] = wout_ref[...].astype(BF16)
    mn = _rms(mem_ref[...], g_ref[...]).astype(BF16)
    mk = _dot(mn, wk_ref[...].astype(BF16))
    mv = _dot(mn, wv_ref[...].astype(BF16))
    tm = mem_ref.shape[0]
    group = X_D_HALVES * N_X_HEADS
    for hd in range(N_X_HEADS):
        for dh in range(X_D_HALVES):
            cols = slice(hd * X_HEAD_DIM + dh * LANES, hd * X_HEAD_DIM + (dh + 1) * LANES)
            rows = pl.ds(dh * N_X_HEADS + hd, tm, stride=group)
            mk_ref[rows, :] = mk[:, cols]
            mv_ref[rows, :] = mv[:, cols]
    mkb_ref[...] = mk.astype(BF16)
    mvb_ref[...] = mv.astype(BF16)


def _memory_kv(mem2d, g_mem, w_xk, w_xv, w_in, w_out):
    n, d = mem2d.shape
    tm = 512
    row = pl.BlockSpec((tm, d), lambda i: (i, 0))
    rows_out = pl.BlockSpec((tm * d // LANES, LANES), lambda i: (i, 0))
    const = lambda shape: pl.BlockSpec(shape, lambda i: (0,) * len(shape), pipeline_mode=pl.Buffered(1))
    steps = n // tm
    win_blk = pl.BlockSpec((w_in.shape[0] // steps, w_in.shape[1]), lambda i: (i, 0))
    wout_blk = pl.BlockSpec((w_out.shape[0] // steps, w_out.shape[1]), lambda i: (i, 0))
    return pl.pallas_call(
        _memkv_kernel,
        grid=(n // tm,),
        in_specs=[row, const((1, d)), const((d, d)), const((d, d)), win_blk, wout_blk],
        out_specs=[rows_out, rows_out, row, row, win_blk, wout_blk],
        out_shape=[jax.ShapeDtypeStruct((n * d // LANES, LANES), F32),
                   jax.ShapeDtypeStruct((n * d // LANES, LANES), F32),
                   jax.ShapeDtypeStruct((n, d), BF16), jax.ShapeDtypeStruct((n, d), BF16),
                   jax.ShapeDtypeStruct(w_in.shape, BF16), jax.ShapeDtypeStruct(w_out.shape, BF16)],
        compiler_params=pltpu.CompilerParams(
            dimension_semantics=("arbitrary",), vmem_limit_bytes=VMEM_LIMIT),
        name="memory_kv",
    )(mem2d, g_mem, w_xk, w_xv, w_in, w_out)


def _prompt_xattn_kernel(h_ref, g_ref, wq_ref, wo_ref, mk_ref, mv_ref, out_ref, o_s):
    def stages(r0):
        rows = slice(r0, r0 + SUB_ROWS)
        env = {}

        def project():
            env["h"] = h_ref[0, rows, :]
            xn = _rms(env["h"], g_ref[...]).astype(BF16)
            env["q"] = (_dot(xn, wq_ref[...]) * (X_HEAD_DIM ** -0.5)).astype(BF16)

        def head(hd):
            sl = slice(hd * X_HEAD_DIM, (hd + 1) * X_HEAD_DIM)
            s = _dot_nt(env["q"][:, sl], mk_ref[0, :, sl])
            m = jnp.max(s, axis=-1, keepdims=True)
            p = jnp.exp(s - m)
            p = p * (1.0 / jnp.sum(p, axis=-1, keepdims=True))
            o_s[rows, sl] = _dot(p.astype(BF16), mv_ref[0, :, sl]).astype(BF16)

        def output():
            out_ref[0, rows, :] = env["h"] + _dot(o_s[rows, :], wo_ref[...])

        return [project] + [functools.partial(head, hd) for hd in range(N_X_HEADS)] + [output]

    chains = [stages(r0) for r0 in range(0, h_ref.shape[1], SUB_ROWS)]
    n_stage = len(chains[0])
    for step in range(n_stage + len(chains) - 1):
        for lag, chain in enumerate(chains):
            if 0 <= step - lag < n_stage:
                chain[step - lag]()


def _prompt_xattn(h, g, w_xq, w_xo, mkb, mvb):
    b, s, d = h.shape
    tm = TM_X
    const = lambda shape: pl.BlockSpec(shape, lambda i, j: (0,) * len(shape))
    tok = pl.BlockSpec((1, tm, d), lambda i, j: (i, j, 0))
    mem = pl.BlockSpec((1, N_MEM, d), lambda i, j: (i, 0, 0))
    return pl.pallas_call(
        _prompt_xattn_kernel,
        grid=(b, s // tm),
        in_specs=[tok, const((1, d)), const((d, d)), const((d, d)), mem, mem],
        out_specs=tok,
        out_shape=jax.ShapeDtypeStruct((b, s, d), F32),
        scratch_shapes=[pltpu.VMEM((tm, d), BF16)],
        compiler_params=pltpu.CompilerParams(
            dimension_semantics=("arbitrary", "arbitrary"), vmem_limit_bytes=VMEM_LIMIT),
        name="prompt_xattn",
    )(h, g, w_xq, w_xo, mkb, mvb)


def _mlp_rows(h_ref, y_ref, g_ref, wup_ref, wdn_ref, gf_ref):
    for r0 in range(0, h_ref.shape[0], SUB_ROWS):
        rows = slice(r0, r0 + SUB_ROWS)
        h = h_ref[rows, :]
        xn = _rms(h, g_ref[...]).astype(BF16)
        acc = h
        for c in range(D_FF // FF_CHUNK):
            sl = slice(c * FF_CHUNK, (c + 1) * FF_CHUNK)
            u = jnp.maximum(_dot(xn, wup_ref[:, sl]), 0.0)
            acc = acc + _dot((u * u).astype(BF16), wdn_ref[sl, :])
        y_ref[rows, :] = _rms(acc, gf_ref[...])


def _mlp_kernel(hp_ref, hs_ref, g_ref, wup_ref, wdn_ref, gf_ref, yp_ref, ys_ref):
    last = pl.num_programs(0) - 1

    @pl.when(pl.program_id(0) < last)
    def _():
        _mlp_rows(hp_ref, yp_ref, g_ref, wup_ref, wdn_ref, gf_ref)

    @pl.when(pl.program_id(0) == last)
    def _():
        _mlp_rows(hs_ref, ys_ref, g_ref, wup_ref, wdn_ref, gf_ref)


def _mlp_final(hp2d, hs2d, g_mlp, w_up, w_down, g_final):
    n, d = hp2d.shape
    ns = hs2d.shape[0]
    tm = TM_MLP
    n_tiles = n // tm
    prompt = pl.BlockSpec((tm, d), lambda i: (jnp.minimum(i, n_tiles - 1), 0))
    sample = pl.BlockSpec((ns, d), lambda i: (0, 0))
    const = lambda shape: pl.BlockSpec(shape, lambda i: (0,) * len(shape), pipeline_mode=pl.Buffered(1))
    return pl.pallas_call(
        _mlp_kernel,
        grid=(n_tiles + 1,),
        in_specs=[prompt, sample, const((1, d)), const((d, D_FF)), const((D_FF, d)), const((1, d))],
        out_specs=[prompt, sample],
        out_shape=[jax.ShapeDtypeStruct((n, d), F32), jax.ShapeDtypeStruct((ns, d), F32)],
        compiler_params=pltpu.CompilerParams(
            dimension_semantics=("arbitrary",), vmem_limit_bytes=VMEM_LIMIT),
        name="mlp_final",
    )(hp2d, hs2d, g_mlp, w_up, w_down, g_final)


def _sample_mixer_kernel(sinks_ref, x_ref, gmix_ref, win_ref, wout_ref, gng_ref, gnb_ref,
                         ck_ref, cv_ref, st_ref, bias_ref, dec_ref, xi_ref, zeta_ref,
                         h_ref, swk_ref, swv_ref, sst_ref):
    bb = ck_ref.shape[0]
    nt = bb // 2
    x = x_ref[...]
    xn = _rms(x, gmix_ref[...]).astype(BF16)
    tile3 = lambda a: a.reshape(nt, SUBLANES, a.shape[-1])

    q = _dot(xn, win_ref[:, C_QA:C_QA + ATT_Q_W]) * (HEAD_DIM ** -0.5)
    kv = _dot(xn, win_ref[:, C_KV:C_KV + 2 * ATT_KV_W])
    qkr = _dot(xn, win_ref[:, C_QKR:C_QKR + 2 * RET_QK_W])
    vr = _dot(xn, win_ref[:, C_VR:C_VR + RET_V_W])
    gate3 = tile3(_silu(_dot(xn, win_ref[:, C_GR:C_GR + RET_V_W])))

    lo512, hi512 = _half_masks(ATT_Q_W)
    q_r = pltpu.roll(q, HALF, axis=1)
    q_nat3 = tile3(q)
    q_rot3 = tile3(q_r)
    lo3 = lo512.reshape(1, 1, ATT_Q_W)
    hi3 = hi512.reshape(1, 1, ATT_Q_W)
    qa3 = (q_nat3 * lo3).astype(BF16)
    qb3 = (q_rot3 * lo3).astype(BF16)
    qc3 = (q_rot3 * hi3).astype(BF16)
    qd3 = (q_nat3 * hi3).astype(BF16)
    t128 = lambda a, i: a[:, :, i * LANES:(i + 1) * LANES]
    qs = jnp.concatenate([t128(qa3, 0), t128(qb3, 1), t128(qa3, 1), t128(qb3, 2),
                          t128(qc3, 2), t128(qd3, 2), t128(qc3, 3), t128(qd3, 3)], axis=1)

    k3 = tile3(kv[:, :ATT_KV_W])
    v3 = tile3(kv[:, ATT_KV_W:])
    pad_kv = jnp.zeros((nt, BLK - SUBLANES, LANES), BF16)
    knew_pad = jnp.concatenate([k3.astype(BF16), pad_kv], axis=1)
    vnew_pad = jnp.concatenate([v3.astype(BF16), pad_kv], axis=1)
    to_lanes = lambda a3: jnp.swapaxes(
        jnp.concatenate([a3, jnp.zeros((nt, BLK - SUBLANES, LANES), F32)], axis=1), 1, 2)
    k3t, v3t = to_lanes(k3), to_lanes(v3)
    roll3 = lambda a, sh: pltpu.roll(a.reshape(nt * BLK, LANES), sh, axis=1).reshape(nt, BLK, LANES)

    lo256, _ = _half_masks(RET_QK_W)
    qr3 = tile3(qkr[:, :RET_QK_W])
    kr3 = tile3(qkr[:, RET_QK_W:] * (RET_QK_DIM ** -0.5))
    vr3 = tile3(vr)
    lane256 = lax.broadcasted_iota(jnp.int32, (1, 1, RET_QK_W), 2)
    qrs = jnp.concatenate(
        [(qr3 * ((lane256 >= h * RET_QK_DIM) & (lane256 < (h + 1) * RET_QK_DIM)).astype(F32)).astype(BF16)
         for h in range(N_RET_HEADS)],
        axis=1)
    kr_pad = jnp.concatenate([kr3.astype(BF16), jnp.zeros((nt, BLK - SUBLANES, RET_QK_W), BF16)], axis=1)
    vr_pad = jnp.concatenate([vr3.astype(BF16), jnp.zeros((nt, BLK - SUBLANES, RET_V_W), BF16)], axis=1)

    lane = lax.broadcasted_iota(jnp.int32, (1, 1, LANES), 2)
    row8 = lax.broadcasted_iota(jnp.int32, (1, SUBLANES, 1), 1)
    bmm_nt = lambda a, b: jnp.einsum('bqd,bkd->bqk', a, b, preferred_element_type=F32)
    bmm = lambda a, b: jnp.einsum('bqk,bkd->bqd', a, b, preferred_element_type=F32)

    att_par, ret_par = [], []
    for par in range(2):
        bsl = pl.ds(par, nt, stride=2)
        ckt = ck_ref[bsl]
        cvt = cv_ref[bsl]
        keep = lane < WINDOW - DEC_SEQ
        new_shift = WINDOW - DEC_SEQ - DEC_SEQ * par
        swk_ref[bsl] = jnp.where(keep, roll3(ckt, WINDOW - DEC_SEQ), roll3(k3t, new_shift))
        swv_ref[bsl] = jnp.where(keep, roll3(cvt, WINDOW - DEC_SEQ), roll3(v3t, new_shift))

        s = jnp.concatenate([bmm(qs, ckt.astype(BF16)), bmm_nt(qs, knew_pad)], axis=2) + bias_ref[par]
        ps = []
        for h in range(N_ATT_HEADS):
            ps.append(_sink_softmax(s[:, h * SUBLANES:(h + 1) * SUBLANES, :], sinks_ref[h]).astype(BF16))
        p_all = jnp.concatenate(ps, axis=1)
        o = bmm_nt(p_all[:, :, :BLK], cvt.astype(BF16)) + bmm(p_all[:, :, BLK:], vnew_pad)
        o_r = pltpu.roll(o.reshape(nt * N_ATT_HEADS * SUBLANES, LANES), HALF, axis=1).reshape(o.shape)
        hr = lambda a, h: a[:, h * SUBLANES:(h + 1) * SUBLANES, :]
        low = lane < HALF
        att_par.append(jnp.concatenate([
            jnp.where(low, hr(o, 0), hr(o_r, 1)), jnp.where(low, hr(o, 2), hr(o_r, 3)),
            jnp.where(low, hr(o_r, 4), hr(o, 5)), jnp.where(low, hr(o_r, 6), hr(o, 7))], axis=2))

        st = st_ref[bsl]
        oc = bmm(qrs, st.astype(BF16))
        inner = (bmm_nt(qrs, kr_pad) * dec_ref[par]).astype(BF16)
        oi = bmm(inner, vr_pad)
        rs = []
        for h in range(N_RET_HEADS):
            vsl = slice(h * RET_V_DIM, (h + 1) * RET_V_DIM)
            rsl = slice(h * SUBLANES, (h + 1) * SUBLANES)
            o_h = oi[:, rsl, vsl] + oc[:, rsl, :] * xi_ref[par, rsl, :]
            rs.append(_group_norm(o_h, gng_ref[:, vsl], gnb_ref[:, vsl]) * gate3[:, :, vsl])
        ret_par.append(jnp.concatenate(rs, axis=2))

        kz3 = (kr3 * zeta_ref[par]).astype(BF16)
        vr3_b = vr3.astype(BF16)
        for p in range(nt):
            for i in range(N_RET_HEADS // 2):
                u = _dot_tn(kz3[p][:, i * LANES:(i + 1) * LANES],
                            vr3_b[p][:, 2 * i * RET_V_DIM:(2 * i + 2) * RET_V_DIM])
                for half in range(2):
                    h = 2 * i + half
                    dsl = slice(h * RET_QK_DIM, (h + 1) * RET_QK_DIM)
                    sst_ref[2 * p + par, dsl, :] = (
                        _GL_SAMPLE[h] * st[p, dsl, :]
                        + u[half * RET_QK_DIM:(half + 1) * RET_QK_DIM, half * RET_V_DIM:(half + 1) * RET_V_DIM])

    own0 = row8 < DEC_SEQ
    att3 = jnp.where(own0, att_par[0], att_par[1])
    ret3 = jnp.where(own0, ret_par[0], ret_par[1])
    mix = jnp.concatenate([att3, ret3], axis=2).reshape(2 * nt * DEC_SEQ, MIX_OUT).astype(BF16)
    h_ref[...] = x + _dot(mix, wout_ref[...])


def _sample_mixer(x2d, g_mix, w_in, w_out, sinks, gn_g, gn_b, ck, cv, st):
    n, d = x2d.shape
    nb = ck.shape[0]
    bb = BB_MIX
    r = bb * DEC_SEQ
    const = lambda shape: pl.BlockSpec(shape, lambda i: (0,) * len(shape))
    row = pl.BlockSpec((r, d), lambda i: (i, 0))
    win = pl.BlockSpec((bb, WINDOW, ATT_KV_W), lambda i: (i, 0, 0))
    state = pl.BlockSpec((bb, RET_QK_W, RET_V_DIM), lambda i: (i, 0, 0))
    return pl.pallas_call(
        _sample_mixer_kernel,
        grid=(nb // bb,),
        in_specs=[
            pl.BlockSpec(memory_space=pltpu.SMEM),
            row, const((1, d)), const((d, D_IN)), const((MIX_OUT, d)),
            const((1, RET_V_W)), const((1, RET_V_W)),
            win, win, state,
            const(_S_BIAS.shape), const(_S_DEC.shape), const(_S_XI.shape), const(_S_ZETA.shape),
        ],
        out_specs=[row, win, win, state],
        out_shape=[
            jax.ShapeDtypeStruct((n, d), F32),
            jax.ShapeDtypeStruct((nb, WINDOW, ATT_KV_W), F32),
            jax.ShapeDtypeStruct((nb, WINDOW, ATT_KV_W), F32),
            jax.ShapeDtypeStruct((nb, RET_QK_W, RET_V_DIM), F32),
        ],
        compiler_params=pltpu.CompilerParams(
            dimension_semantics=("arbitrary",), vmem_limit_bytes=VMEM_LIMIT),
        name="sample_mixer",
    )(sinks, x2d, g_mix, w_in, w_out, gn_g, gn_b, ck, cv, st,
      jnp.asarray(_S_BIAS), jnp.asarray(_S_DEC), jnp.asarray(_S_XI), jnp.asarray(_S_ZETA))


def _head_slab(x_ref, b, hd):
    group = X_D_HALVES * N_X_HEADS
    halves = [x_ref[b, pl.ds(dh * N_X_HEADS + hd, N_MEM, stride=group), :] for dh in range(X_D_HALVES)]
    return jnp.concatenate(halves, axis=1).astype(BF16)


def _sample_xattn_kernel(h_ref, g_ref, wq_ref, wo_ref, xk_ref, xv_ref, out_ref):
    bb = xk_ref.shape[0]
    nt = bb // 2
    h = h_ref[...]
    xn = _rms(h, g_ref[...]).astype(BF16)
    q = _dot(xn, wq_ref[...]) * (X_HEAD_DIM ** -0.5)
    units = [(t, par, hd) for t in range(nt) for par in range(2) for hd in range(N_X_HEADS)]
    qts = [q[t * SUBLANES:(t + 1) * SUBLANES].astype(BF16) for t in range(nt)]
    s = jnp.concatenate(
        [_dot_nt(qts[t][:, hd * X_HEAD_DIM:(hd + 1) * X_HEAD_DIM], _head_slab(xk_ref, 2 * t + par, hd))
         for t, par, hd in units], axis=0)
    m = jnp.max(s, axis=-1, keepdims=True)
    p = jnp.exp(s - m)
    p = p * (1.0 / jnp.sum(p, axis=-1, keepdims=True))
    os_ = {}
    for i, (t, par, hd) in enumerate(units):
        pi = p[i * SUBLANES:(i + 1) * SUBLANES].astype(BF16)
        os_[(t, par, hd)] = _dot(pi, _head_slab(xv_ref, 2 * t + par, hd))
    own0 = lax.broadcasted_iota(jnp.int32, (SUBLANES, 1), 0) < DEC_SEQ
    o_tiles = []
    for t in range(nt):
        o_par = [jnp.concatenate([os_[(t, par, hd)] for hd in range(N_X_HEADS)], axis=1) for par in range(2)]
        o_tiles.append(jnp.where(own0, o_par[0], o_par[1]))
    o = jnp.concatenate(o_tiles, axis=0).astype(BF16)
    out_ref[...] = h + _dot(o, wo_ref[...])


def _mlp_value(h, g_ref, wup_ref, wdn_ref, gf_ref, fillers=None):
    xn = _rms(h, g_ref[...]).astype(BF16)
    acc = h
    for c in range(D_FF // FF_CHUNK):
        sl = slice(c * FF_CHUNK, (c + 1) * FF_CHUNK)
        u = jnp.maximum(_dot(xn, wup_ref[:, sl]), 0.0)
        acc = acc + _dot((u * u).astype(BF16), wdn_ref[sl, :])
        if fillers is not None:
            fillers[c]()
    return _rms(acc, gf_ref[...])


def _mlp_xattn_kernel(hp_ref, hsm_ref, gx_ref, wq_ref, wo_ref, xk_ref, xv_ref, g_ref, wup_ref, wdn_ref, gf_ref,
                      yp_ref, ys_ref):
    i = pl.program_id(0)
    n = pl.num_programs(0) - 1
    bb = xk_ref.shape[0]
    rows = bb * DEC_SEQ
    assert bb == D_FF // FF_CHUNK and bb % 2 == 0

    @pl.when(i == 0)
    def _():
        xn = _rms(hsm_ref[...], gx_ref[...]).astype(BF16)
        ys_ref[...] = _dot(xn, wq_ref[...]) * (X_HEAD_DIM ** -0.5)

    @pl.when(i < n)
    def _():
        r0 = pl.multiple_of(i * rows, rows)
        own0 = lax.broadcasted_iota(jnp.int32, (SUBLANES, 1), 0) < DEC_SEQ
        o_rows = {}

        def attend(b):
            t = b // 2
            qt = ys_ref[pl.ds(r0 + t * SUBLANES, SUBLANES), :].astype(BF16)
            s = jnp.concatenate(
                [_dot_nt(qt[:, hd * X_HEAD_DIM:(hd + 1) * X_HEAD_DIM], _head_slab(xk_ref, b, hd))
                 for hd in range(N_X_HEADS)], axis=0)
            m = jnp.max(s, axis=-1, keepdims=True)
            p = jnp.exp(s - m)
            p = p * (1.0 / jnp.sum(p, axis=-1, keepdims=True))
            o_rows[b] = jnp.concatenate(
                [_dot(p[hd * SUBLANES:(hd + 1) * SUBLANES].astype(BF16), _head_slab(xv_ref, b, hd))
                 for hd in range(N_X_HEADS)], axis=1)
            if b % 2 == 1:
                ys_ref[pl.ds(r0 + t * SUBLANES, SUBLANES), :] = jnp.where(own0, o_rows[b - 1], o_rows[b])

        fillers = [functools.partial(attend, b) for b in range(bb)]
        yp_ref[...] = _mlp_value(hp_ref[...], g_ref, wup_ref, wdn_ref, gf_ref, fillers)

    @pl.when(i == n)
    def _():
        hs = hsm_ref[...] + _dot(ys_ref[...].astype(BF16), wo_ref[...])
        ys_ref[...] = _mlp_value(hs, g_ref, wup_ref, wdn_ref, gf_ref)


def _mlp_xattn(hp2d, hsm, g_xattn, w_xq, w_xo, xk, xv, g_mlp, w_up, w_down, g_final):
    n, d = hp2d.shape
    ns = hsm.shape[0]
    nb = xk.shape[0]
    bb = BB_X
    tm = n // (nb // bb)
    n_tiles = n // tm
    assert n_tiles * bb == nb and tm % SUBLANES == 0
    clip = lambda i: jnp.minimum(i, n_tiles - 1)
    prompt = pl.BlockSpec((tm, d), lambda i: (clip(i), 0))
    mem = pl.BlockSpec((bb,) + xk.shape[1:], lambda i: (clip(i), 0, 0))
    const = lambda shape: pl.BlockSpec(shape, lambda i: (0,) * len(shape), pipeline_mode=pl.Buffered(1))
    return pl.pallas_call(
        _mlp_xattn_kernel,
        grid=(n_tiles + 1,),
        in_specs=[prompt, const((ns, d)), const((1, d)), const((d, d)), const((d, d)), mem, mem,
                  const((1, d)), const((d, D_FF)), const((D_FF, d)), const((1, d))],
        out_specs=[prompt, pl.BlockSpec((ns, d), lambda i: (0, 0))],
        out_shape=[jax.ShapeDtypeStruct((n, d), F32), jax.ShapeDtypeStruct((ns, d), F32)],
        compiler_params=pltpu.CompilerParams(
            dimension_semantics=("arbitrary",), vmem_limit_bytes=VMEM_LIMIT),
        name="mlp_xattn",
    )(hp2d, hsm, g_xattn, w_xq, w_xo, xk, xv, g_mlp, w_up, w_down, g_final)


def _mem_rows(c):
    nb = c.shape[0]
    c = c.reshape(nb, N_MEM, N_X_HEADS, X_D_HALVES, LANES)
    return jnp.transpose(c, (0, 1, 3, 2, 4)).reshape(nb, N_MEM * X_D_HALVES * N_X_HEADS, LANES)


def _sample_xattn(h2d, g, w_xq, w_xo, mk, mv):
    n, d = h2d.shape
    nb = mk.shape[0]
    bb = BB_X
    r = bb * DEC_SEQ
    const = lambda shape: pl.BlockSpec(shape, lambda i: (0,) * len(shape))
    row = pl.BlockSpec((r, d), lambda i: (i, 0))
    mem = pl.BlockSpec((bb,) + mk.shape[1:], lambda i: (i, 0, 0))
    return pl.pallas_call(
        _sample_xattn_kernel,
        grid=(nb // bb,),
        in_specs=[row, const((1, d)), const((d, d)), const((d, d)), mem, mem],
        out_specs=row,
        out_shape=jax.ShapeDtypeStruct((n, d), F32),
        compiler_params=pltpu.CompilerParams(
            dimension_semantics=("arbitrary",), vmem_limit_bytes=VMEM_LIMIT),
        name="sample_xattn",
    )(h2d, g, w_xq, w_xo, mk, mv)


def kernel(x_prompt, x_sample, mem_prompt, cache_win_k, cache_win_v, state_ret, cache_mem_k, cache_mem_v,
           g_mix, w_in, attn_sinks, ret_gn_g, ret_gn_b, w_out, g_xattn, g_mem, w_xq, w_xk, w_xv, w_xo,
           g_mlp, w_up, w_down, g_final):
    depth = w_in.shape[0]
    assert depth == 1, "single-layer trunk"
    b, s, d = x_prompt.shape
    nb, ls, _ = x_sample.shape
    row = lambda a: a.reshape(1, -1)
    sinks = attn_sinks[0]
    gn_g, gn_b = row(ret_gn_g[0]), row(ret_gn_b[0])
    g_fin = row(g_final)

    mk, mv, mkb, mvb, w_in_b, w_out_b = _memory_kv(
        mem_prompt.reshape(b * N_MEM, d), row(g_mem[0]), w_xk[0], w_xv[0], w_in[0], w_out[0])
    hp, p_wk, p_wv, p_rs, w_up_b, w_dn_b, w_xq_b, w_xo_b = _prompt_mixer_p(
        x_prompt, row(g_mix[0]), w_in_b, w_out_b, sinks, gn_g, gn_b,
        (w_up[0], w_down[0], w_xq[0], w_xo[0]))
    hp = _prompt_xattn(hp, row(g_xattn[0]), w_xq_b, w_xo_b,
                       mkb.reshape(b, N_MEM, d), mvb.reshape(b, N_MEM, d))

    win_t = lambda c: jnp.transpose(c, (0, 2, 3, 1)).reshape(nb, ATT_KV_W, WINDOW)
    win_t_inv = lambda a: jnp.transpose(a.reshape(nb, N_KV_HEADS, HEAD_DIM, WINDOW),
                                        (0, 3, 1, 2)).reshape(1, nb, WINDOW, N_KV_HEADS, HEAD_DIM)
    hs, s_wk, s_wv, s_rs = _sample_mixer(
        x_sample.reshape(nb * ls, d), row(g_mix[0]), w_in_b, w_out_b, sinks, gn_g, gn_b,
        win_t(cache_win_k[0]), win_t(cache_win_v[0]), state_ret[0].reshape(nb, RET_QK_W, RET_V_DIM))

    y_prompt, y_sample = _mlp_xattn(
        hp.reshape(b * s, d), hs, row(g_xattn[0]), w_xq_b, w_xo_b,
        _mem_rows(cache_mem_k[0]), _mem_rows(cache_mem_v[0]), row(g_mlp[0]), w_up_b, w_dn_b, g_fin)
    y_prompt = y_prompt.reshape(b, s, d)
    y_sample = y_sample.reshape(nb, ls, d)

    win5 = lambda a, n: a.reshape(1, n, WINDOW, N_KV_HEADS, HEAD_DIM)
    ret5 = lambda a, n: a.reshape(1, n, N_RET_HEADS, RET_QK_DIM, RET_V_DIM)
    mem5 = lambda a: jnp.transpose(a.reshape(b, N_MEM, X_D_HALVES, N_X_HEADS, LANES),
                                   (0, 1, 3, 2, 4)).reshape(1, b, N_MEM, N_X_HEADS, X_HEAD_DIM)
    return (y_prompt, y_sample,
            win5(p_wk, b), win5(p_wv, b), ret5(p_rs, b), mem5(mk), mem5(mv),
            win_t_inv(s_wk), win_t_inv(s_wv), ret5(s_rs, nb))
```

```python
import functools

import jax
import jax.numpy as jnp
import numpy as np
from jax import lax
from jax.experimental import pallas as pl
from jax.experimental.pallas import tpu as pltpu

F32 = jnp.float32
BF16 = jnp.bfloat16

D_MODEL = 1024
BATCH = 8
SEQ = 2048
DEC_BATCH = 128
DEC_SEQ = 4
HEAD_DIM = 64
N_ATT_HEADS = 8
N_KV_HEADS = 2
KV_GROUP = N_ATT_HEADS // N_KV_HEADS
WINDOW = 128
BLK = 128
N_RET_HEADS = 4
RET_QK_DIM = 64
RET_V_DIM = 128
N_MEM = 256
N_X_HEADS = 4
X_HEAD_DIM = D_MODEL // N_X_HEADS
D_FF = 4 * D_MODEL
RMS_EPS = 1e-6
GN_EPS = 1e-5

ATT_Q_W = N_ATT_HEADS * HEAD_DIM
ATT_KV_W = N_KV_HEADS * HEAD_DIM
RET_QK_W = N_RET_HEADS * RET_QK_DIM
RET_V_W = N_RET_HEADS * RET_V_DIM
MIX_OUT = ATT_Q_W + RET_V_W
D_IN = ATT_Q_W + 2 * ATT_KV_W + 2 * RET_QK_W + 2 * RET_V_W
C_QA, C_KV, C_QKR, C_VR, C_GR = 0, 512, 768, 1280, 1792

LANES = 128
SUBLANES = 8
HALF = LANES // 2
X_D_HALVES = X_HEAD_DIM // LANES
NEG = -1e30
VMEM_LIMIT = 56 * 1024 * 1024

TM_MIX = 512
TM_X = 2048
TM_MLP = 1024
SUB_ROWS = 512
FF_CHUNK = 1024
BB_MIX = 16
BB_X = 4

NEG_SLOPES = [-(2.0 ** (-8.0 * (i + 1) / N_ATT_HEADS)) for i in range(N_ATT_HEADS)]
_LOG_G = np.log(1.0 - 2.0 ** (-5.0 - np.arange(N_RET_HEADS))).astype(np.float32).astype(np.float64)


def _prompt_tables():
    qi = np.arange(BLK)[:, None]
    kj = np.arange(2 * BLK)[None, :]
    dist = (qi + BLK - kj).astype(np.float64)
    mask = np.where((dist >= 0) & (dist < WINDOW), 0.0, NEG)
    l = np.arange(BLK, dtype=np.float64)
    diff = l[:, None] - l[None, :]
    decay = np.where(diff >= 0, np.exp(_LOG_G[:, None, None] * np.maximum(diff, 0.0)), 0.0)
    xi = np.exp((l[:, None] + 1.0) * _LOG_G[None, :])
    zeta = np.exp((BLK - 1.0 - l)[:, None] * _LOG_G[None, :])
    xi_t = np.repeat(xi, RET_V_DIM, axis=1)
    zeta_t = np.repeat(zeta, RET_QK_DIM, axis=1)
    f = lambda a: np.asarray(a, np.float32)
    return f(dist), f(mask), f(decay), f(xi_t), f(zeta_t)


def _sample_tables():
    slopes = -np.asarray(NEG_SLOPES)
    bias = np.full((2, N_ATT_HEADS * SUBLANES, 2 * BLK), NEG, np.float64)
    dec = np.zeros((2, N_RET_HEADS * SUBLANES, BLK), np.float64)
    xi = np.zeros((2, N_RET_HEADS * SUBLANES, RET_V_DIM), np.float64)
    zeta = np.zeros((2, SUBLANES, RET_QK_W), np.float64)
    for par in range(2):
        for r in range(SUBLANES):
            own = DEC_SEQ * par <= r < DEC_SEQ * (par + 1)
            t = r - DEC_SEQ * par if own else r % DEC_SEQ
            for h in range(N_ATT_HEADS):
                row = h * SUBLANES + r
                for j in range(WINDOW):
                    d = t + WINDOW - j
                    if 0 <= d < WINDOW:
                        bias[par, row, j] = -slopes[h] * d
                for c in range(DEC_SEQ):
                    d = t - c
                    if d >= 0:
                        bias[par, row, WINDOW + DEC_SEQ * par + c] = -slopes[h] * d
            for h in range(N_RET_HEADS):
                row = h * SUBLANES + r
                if own:
                    xi[par, row, :] = np.exp((t + 1.0) * _LOG_G[h])
                    zeta[par, r, h * RET_QK_DIM:(h + 1) * RET_QK_DIM] = np.exp((DEC_SEQ - 1.0 - t) * _LOG_G[h])
                    for c in range(t + 1):
                        dec[par, row, DEC_SEQ * par + c] = np.exp(_LOG_G[h] * (t - c))
    f = lambda a: np.asarray(a, np.float32)
    return f(bias), f(dec), f(xi), f(zeta)


_P_DIST, _P_MASK, _P_DECAY, _P_XI, _P_ZETA = _prompt_tables()
_S_BIAS, _S_DEC, _S_XI, _S_ZETA = _sample_tables()
_GL_PROMPT = [float(np.exp(_LOG_G[h] * BLK)) for h in range(N_RET_HEADS)]
_GL_SAMPLE = [float(np.exp(_LOG_G[h] * DEC_SEQ)) for h in range(N_RET_HEADS)]


def _rms(x, g):
    return x * lax.rsqrt(jnp.mean(x * x, axis=-1, keepdims=True) + RMS_EPS) * g


def _dot(a, b):
    return jnp.dot(a, b, preferred_element_type=F32)


def _dot_nt(a, b):
    return lax.dot_general(a, b, (((1,), (1,)), ((), ())), preferred_element_type=F32)


def _dot_tn(a, b):
    return lax.dot_general(a, b, (((0,), (0,)), ((), ())), preferred_element_type=F32)


def _silu(g):
    return g * (1.0 / (1.0 + jnp.exp(-g)))


def _half_masks(width):
    lane = lax.broadcasted_iota(jnp.int32, (1, width), 1)
    lo = ((lane & (LANES - 1)) < HALF).astype(F32)
    return lo, 1.0 - lo


def _sink_softmax(s, sink):
    m = jnp.maximum(jnp.max(s, axis=-1, keepdims=True), sink)
    p = jnp.exp(s - m)
    den = jnp.sum(p, axis=-1, keepdims=True) + jnp.exp(sink - m)
    return p * (1.0 / den)


def _group_norm(o, g, b):
    mu = jnp.mean(o, axis=-1, keepdims=True)
    d = o - mu
    var = jnp.mean(d * d, axis=-1, keepdims=True)
    return d * lax.rsqrt(var + GN_EPS) * g + b


def _prompt_mixer_kernel(sinks_ref, x_ref, gmix_ref, win_ref, wout_ref, gng_ref, gnb_ref,
                         dist_ref, mask_ref, decay_ref, xi_ref, zeta_ref,
                         h_ref, wk_ref, wv_ref, st_ref,
                         qlo_s, qhi_s, kd0_s, kd1_s, vd0_s, vd1_s,
                         qrlo_s, qrhi_s, kr_s, vr_s, gate_s, mix_s, bias_s):
    t = pl.program_id(1)
    nt = pl.num_programs(1)
    tm = x_ref.shape[1]
    nblk = tm // BLK

    @pl.when(t == 0)
    def _():
        kd0_s[0:BLK, :] = jnp.zeros((BLK, LANES), BF16)
        kd1_s[0:BLK, :] = jnp.zeros((BLK, LANES), BF16)
        vd0_s[0:BLK, :] = jnp.zeros((BLK, 2 * LANES), BF16)
        vd1_s[0:BLK, :] = jnp.zeros((BLK, 2 * LANES), BF16)
        st_ref[...] = jnp.zeros_like(st_ref)

    @pl.when(t > 0)
    def _():
        kd0_s[0:BLK, :] = kd0_s[tm:tm + BLK, :]
        kd1_s[0:BLK, :] = kd1_s[tm:tm + BLK, :]
        vd0_s[0:BLK, :] = vd0_s[tm:tm + BLK, :]
        vd1_s[0:BLK, :] = vd1_s[tm:tm + BLK, :]

    @pl.when((t == 0) & (pl.program_id(0) == 0))
    def _():
        for h in range(N_ATT_HEADS):
            bias_s[h] = NEG_SLOPES[h] * dist_ref[...] + mask_ref[...]

    x = x_ref[0]
    xn = _rms(x, gmix_ref[...]).astype(BF16)

    z = _dot(xn, win_ref[...])
    lo512, hi512 = _half_masks(ATT_Q_W)
    q = z[:, C_QA:C_QA + ATT_Q_W]
    qlo_s[...] = (q * (lo512 * HEAD_DIM ** -0.5)).astype(BF16)
    qhi_s[...] = (q * (hi512 * HEAD_DIM ** -0.5)).astype(BF16)

    low = lax.broadcasted_iota(jnp.int32, (tm, LANES), 1) < HALF
    kv = z[:, C_KV:C_KV + 2 * ATT_KV_W]
    k = kv[:, :ATT_KV_W]
    v = kv[:, ATT_KV_W:]
    k_r = pltpu.roll(k, HALF, axis=1)
    v_r = pltpu.roll(v, HALF, axis=1)
    kd0_s[BLK:BLK + tm, :] = jnp.where(low, k, k_r).astype(BF16)
    kd1_s[BLK:BLK + tm, :] = jnp.where(low, k_r, k).astype(BF16)
    vd0_s[BLK:BLK + tm, 0:LANES] = jnp.where(low, v, 1.0).astype(BF16)
    vd0_s[BLK:BLK + tm, LANES:2 * LANES] = jnp.where(low, 1.0, v_r).astype(BF16)
    vd1_s[BLK:BLK + tm, 0:LANES] = jnp.where(low, v_r, 1.0).astype(BF16)
    vd1_s[BLK:BLK + tm, LANES:2 * LANES] = jnp.where(low, 1.0, v).astype(BF16)

    @pl.when(t == nt - 1)
    def _():
        wk_ref[0] = k[tm - WINDOW:, :]
        wv_ref[0] = v[tm - WINDOW:, :]

    lo256, hi256 = _half_masks(RET_QK_W)
    qkr = z[:, C_QKR:C_QKR + 2 * RET_QK_W]
    qr = qkr[:, :RET_QK_W]
    qrlo_s[...] = (qr * lo256).astype(BF16)
    qrhi_s[...] = (qr * hi256).astype(BF16)
    kr_s[...] = qkr[:, RET_QK_W:] * (RET_QK_DIM ** -0.5)
    vr_s[...] = z[:, C_VR:C_VR + RET_V_W].astype(BF16)
    gate_s[...] = _silu(z[:, C_GR:C_GR + RET_V_W])

    lowb =lax.broadcasted_iota(jnp.int32, (BLK, LANES), 1) < HALF
    col = lax.broadcasted_iota(jnp.int32, (BLK, 2 * BLK), 1)
    first_mask = jnp.where((col < BLK) & (t == 0), NEG, 0.0)
    kd_refs = (kd0_s, kd1_s)
    vd_refs = (vd0_s, vd1_s)
    n_pairs = N_RET_HEADS // 2
    state = [st_ref[0, i * LANES:(i + 1) * LANES, :] for i in range(n_pairs)]

    for j in range(nblk):
        rows = slice(j * BLK, (j + 1) * BLK)
        krows = slice(j * BLK, (j + 2) * BLK)

        for kvh in range(N_KV_HEADS):
            kd = kd_refs[kvh][krows, :]
            vd = vd_refs[kvh][krows, :]
            c0 = kvh * KV_GROUP * HEAD_DIM
            qst = jnp.concatenate([qlo_s[rows, c0:c0 + LANES], qhi_s[rows, c0:c0 + LANES],
                                   qlo_s[rows, c0 + LANES:c0 + 2 * LANES],
                                   qhi_s[rows, c0 + LANES:c0 + 2 * LANES]], axis=0)
            s = _dot_nt(qst, kd)
            es, esink = [], []
            for g in range(KV_GROUP):
                h = kvh * KV_GROUP + g
                sg = s[g * BLK:(g + 1) * BLK] + bias_s[h]
                if j == 0:
                    sg = sg + first_mask
                sink = sinks_ref[h]
                m = jnp.maximum(jnp.max(sg, axis=-1, keepdims=True), sink)
                es.append(jnp.exp(sg - m).astype(BF16))
                esink.append(jnp.exp(sink - m))
            o = _dot(jnp.concatenate(es, axis=0), vd)
            for pair in range(KV_GROUP // 2):
                oe = o[2 * pair * BLK:(2 * pair + 1) * BLK]
                oo = o[(2 * pair + 1) * BLK:(2 * pair + 2) * BLK]
                num = jnp.where(lowb, oe[:, :LANES], oo[:, LANES:])
                den = (jnp.where(lowb, oe[:, LANES:], oo[:, :LANES])
                       + jnp.where(lowb, esink[2 * pair], esink[2 * pair + 1]))
                cs = c0 + pair * LANES
                mix_s[rows, cs:cs + LANES] = (num * (1.0 / den)).astype(BF16)

        for i in range(n_pairs):
            lsl = slice(i * LANES, (i + 1) * LANES)
            kp = kr_s[rows, lsl]
            sp = state[i]
            vpair = vr_s[rows, 2 * i * RET_V_DIM:(2 * i + 2) * RET_V_DIM]
            q2 = jnp.concatenate([qrlo_s[rows, lsl], qrhi_s[rows, lsl]], axis=0)
            a = _dot_nt(q2, kp.astype(BF16))
            inner = jnp.concatenate([a[:BLK] * decay_ref[2 * i], a[BLK:] * decay_ref[2 * i + 1]], axis=0)
            oi = _dot(inner.astype(BF16), vpair)
            oc = _dot(q2, sp.astype(BF16))
            for half in range(2):
                h = 2 * i + half
                vsl = slice(h * RET_V_DIM, (h + 1) * RET_V_DIM)
                hr = slice(half * BLK, (half + 1) * BLK)
                o = oi[hr, half * RET_V_DIM:(half + 1) * RET_V_DIM] + oc[hr] * xi_ref[:, vsl]
                r = _group_norm(o, gng_ref[:, vsl], gnb_ref[:, vsl]) * gate_s[rows, vsl]
                mix_s[rows, ATT_Q_W + h * RET_V_DIM:ATT_Q_W + (h + 1) * RET_V_DIM] = r.astype(BF16)
            kz = (kp * zeta_ref[:, lsl]).astype(BF16)
            u = _dot_tn(kz, vpair)
            state[i] = jnp.concatenate(
                [_GL_PROMPT[2 * i] * sp[:RET_QK_DIM] + u[:RET_QK_DIM, :RET_V_DIM],
                 _GL_PROMPT[2 * i + 1] * sp[RET_QK_DIM:] + u[RET_QK_DIM:, RET_V_DIM:]], axis=0)

    for i in range(n_pairs):
        st_ref[0, i * LANES:(i + 1) * LANES, :] = state[i]

    h_ref[0] = x + _dot(mix_s[...], wout_ref[...])


def _prompt_mixer(x, g_mix, w_in, w_out, sinks, gn_g, gn_b):
    b, s, d = x.shape
    tm = TM_MIX
    const = lambda shape: pl.BlockSpec(shape, lambda i, j: (0,) * len(shape))
    return pl.pallas_call(
        _prompt_mixer_kernel,
        grid=(b, s // tm),
        in_specs=[
            pl.BlockSpec(memory_space=pltpu.SMEM),
            pl.BlockSpec((1, tm, d), lambda i, j: (i, j, 0)),
            const((1, d)), const((d, D_IN)), const((MIX_OUT, d)),
            const((1, RET_V_W)), const((1, RET_V_W)),
            const((BLK, 2 * BLK)), const((BLK, 2 * BLK)),
            const((N_RET_HEADS, BLK, BLK)), const((BLK, RET_V_W)), const((BLK, RET_QK_W)),
        ],
        out_specs=[
            pl.BlockSpec((1, tm, d), lambda i, j: (i, j, 0)),
            pl.BlockSpec((1, WINDOW, ATT_KV_W), lambda i, j: (i, 0, 0)),
            pl.BlockSpec((1, WINDOW, ATT_KV_W), lambda i, j: (i, 0, 0)),
            pl.BlockSpec((1, RET_QK_W, RET_V_DIM), lambda i, j: (i, 0, 0)),
        ],
        out_shape=[
            jax.ShapeDtypeStruct((b, s, d), F32),
            jax.ShapeDtypeStruct((b, WINDOW, ATT_KV_W), F32),
            jax.ShapeDtypeStruct((b, WINDOW, ATT_KV_W), F32),
            jax.ShapeDtypeStruct((b, RET_QK_W, RET_V_DIM), F32),
        ],
        scratch_shapes=[
            pltpu.VMEM((tm, ATT_Q_W), BF16), pltpu.VMEM((tm, ATT_Q_W), BF16),
            pltpu.VMEM((tm + BLK, LANES), BF16), pltpu.VMEM((tm + BLK, LANES), BF16),
            pltpu.VMEM((tm + BLK, 2 * LANES), BF16), pltpu.VMEM((tm + BLK, 2 * LANES), BF16),
            pltpu.VMEM((tm, RET_QK_W), BF16), pltpu.VMEM((tm, RET_QK_W), BF16),
            pltpu.VMEM((tm, RET_QK_W), F32), pltpu.VMEM((tm, RET_V_W), BF16),
            pltpu.VMEM((tm, RET_V_W), F32), pltpu.VMEM((tm, MIX_OUT), BF16),
            pltpu.VMEM((N_ATT_HEADS, BLK, 2 * BLK), F32),
        ],
        compiler_params=pltpu.CompilerParams(
            dimension_semantics=("arbitrary", "arbitrary"), vmem_limit_bytes=VMEM_LIMIT),
        name="prompt_mixer",
    )(sinks, x, g_mix, w_in, w_out, gn_g, gn_b,
      jnp.asarray(_P_DIST), jnp.asarray(_P_MASK), jnp.asarray(_P_DECAY), jnp.asarray(_P_XI),
      jnp.asarray(_P_ZETA))


def _pm_project_stages(x, slot, gmix_ref, win_ref, sc, kv_out=None):
    tm = x.shape[0]
    xn = _rms(x, gmix_ref[...]).astype(BF16)

    def stage_q():
        q = _dot(xn, win_ref[:, C_QA:C_QA + ATT_Q_W])
        lo512, hi512 = _half_masks(ATT_Q_W)
        sc["qlo"][slot] = (q * (lo512 * HEAD_DIM ** -0.5)).astype(BF16)
        sc["qhi"][slot] = (q * (hi512 * HEAD_DIM ** -0.5)).astype(BF16)

    def stage_kv_qkr():
        z = _dot(xn, win_ref[:, C_KV:C_VR])
        low = lax.broadcasted_iota(jnp.int32, (tm, LANES), 1) < HALF
        k = z[:, 0:ATT_KV_W]
        v = z[:, ATT_KV_W:2 * ATT_KV_W]
        if kv_out is not None:
            kv_out[0][0] = k[tm - WINDOW:, :]
            kv_out[1][0] = v[tm - WINDOW:, :]
        k_r = pltpu.roll(k, HALF, axis=1)
        v_r = pltpu.roll(v, HALF, axis=1)
        sc["kd0"][slot] = jnp.where(low, k, k_r).astype(BF16)
        sc["kd1"][slot] = jnp.where(low, k_r, k).astype(BF16)
        sc["vd0"][slot, :, 0:LANES] = jnp.where(low, v, 1.0).astype(BF16)
        sc["vd0"][slot, :, LANES:2 * LANES] = jnp.where(low, 1.0, v_r).astype(BF16)
        sc["vd1"][slot, :, 0:LANES] = jnp.where(low, v_r, 1.0).astype(BF16)
        sc["vd1"][slot, :, LANES:2 * LANES] = jnp.where(low, 1.0, v).astype(BF16)
        lo256, hi256 = _half_masks(RET_QK_W)
        qr = z[:, 2 * ATT_KV_W:2 * ATT_KV_W + RET_QK_W]
        sc["qrlo"][slot] = (qr * lo256).astype(BF16)
        sc["qrhi"][slot] = (qr * hi256).astype(BF16)
        sc["kr"][slot] = z[:, 2 * ATT_KV_W + RET_QK_W:] * (RET_QK_DIM ** -0.5)

    def stage_vr():
        sc["vr"][slot] = _dot(xn, win_ref[:, C_VR:C_VR + RET_V_W]).astype(BF16)

    def stage_gate():
        sc["gate"][slot] = _silu(_dot(xn, win_ref[:, C_GR:C_GR + RET_V_W]))

    return [stage_q, stage_kv_qkr, stage_vr, stage_gate]


def _pm_last_block(slot, tm, sc):
    rows = slice(tm - BLK, tm)
    return ([sc["kd0"][slot, rows, :], sc["kd1"][slot, rows, :]],
            [sc["vd0"][slot, rows, :], sc["vd1"][slot, rows, :]])


def _pm_blocks(slot, prev_kd, prev_vd, is_first, state, x, fillers, tm, sinks_ref, wout_ref, gng_ref, gnb_ref,
               decay_ref, xi_ref, zeta_ref, sc):
    nblk = tm // BLK
    lowb = lax.broadcasted_iota(jnp.int32, (BLK, LANES), 1) < HALF
    col = lax.broadcasted_iota(jnp.int32, (BLK, 2 * BLK), 1)
    first_mask = None if is_first is False else jnp.where((col < BLK) & is_first, NEG, 0.0)
    kd_refs = (sc["kd0"], sc["kd1"])
    vd_refs = (sc["vd0"], sc["vd1"])
    qlo, qhi, mix = sc["qlo"], sc["qhi"], sc["mix"]
    n_pairs = N_RET_HEADS // 2

    for j in range(nblk):
        rows = slice(j * BLK, (j + 1) * BLK)
        for kvh in range(N_KV_HEADS):
            if j == 0:
                kd = jnp.concatenate([prev_kd[kvh], kd_refs[kvh][slot, rows, :]], axis=0)
                vd = jnp.concatenate([prev_vd[kvh], vd_refs[kvh][slot, rows, :]], axis=0)
            else:
                krows = slice((j - 1) * BLK, (j + 1) * BLK)
                kd = kd_refs[kvh][slot, krows, :]
                vd = vd_refs[kvh][slot, krows, :]
            c0 = kvh * KV_GROUP * HEAD_DIM
            qst = jnp.concatenate([qlo[slot, rows, c0:c0 + LANES], qhi[slot, rows, c0:c0 + LANES],
                                   qlo[slot, rows, c0 + LANES:c0 + 2 * LANES],
                                   qhi[slot, rows, c0 + LANES:c0 + 2 * LANES]], axis=0)
            s = _dot_nt(qst, kd)
            es, esink = [], []
            for g in range(KV_GROUP):
                h = kvh * KV_GROUP + g
                sg = s[g * BLK:(g + 1) * BLK] + sc["bias"][h]
                if j == 0 and first_mask is not None:
                    sg = sg + first_mask
                sink = sinks_ref[h]
                m = jnp.maximum(jnp.max(sg, axis=-1, keepdims=True), sink)
                es.append(jnp.exp(sg - m).astype(BF16))
                esink.append(jnp.exp(sink - m))
            o = _dot(jnp.concatenate(es, axis=0), vd)
            for pair in range(KV_GROUP // 2):
                oe = o[2 * pair * BLK:(2 * pair + 1) * BLK]
                oo = o[(2 * pair + 1) * BLK:(2 * pair + 2) * BLK]
                num = jnp.where(lowb, oe[:, :LANES], oo[:, LANES:])
                den = (jnp.where(lowb, oe[:, LANES:], oo[:, :LANES])
                       + jnp.where(lowb, esink[2 * pair], esink[2 * pair + 1]))
                cs = c0 + pair * LANES
                mix[slot, rows, cs:cs + LANES] = (num * (1.0 / den)).astype(BF16)

        for i in range(n_pairs):
            lsl = slice(i * LANES, (i + 1) * LANES)
            kp = sc["kr"][slot, rows, lsl]
            sp = state[i]
            vpair = sc["vr"][slot, rows, 2 * i * RET_V_DIM:(2 * i + 2) * RET_V_DIM]
            q2 = jnp.concatenate([sc["qrlo"][slot, rows, lsl], sc["qrhi"][slot, rows, lsl]], axis=0)
            a = _dot_nt(q2, kp.astype(BF16))
            inner = jnp.concatenate([a[:BLK] * decay_ref[2 * i], a[BLK:] * decay_ref[2 * i + 1]], axis=0)
            oi = _dot(inner.astype(BF16), vpair)
            oc = _dot(q2, sp.astype(BF16))
            for half in range(2):
                h = 2 * i + half
                vsl = slice(h * RET_V_DIM, (h + 1) * RET_V_DIM)
                hr = slice(half * BLK, (half + 1) * BLK)
                o = oi[hr, half * RET_V_DIM:(half + 1) * RET_V_DIM] + oc[hr] * xi_ref[:, vsl]
                r = _group_norm(o, gng_ref[:, vsl], gnb_ref[:, vsl]) * sc["gate"][slot, rows, vsl]
                mix[slot, rows, ATT_Q_W + h * RET_V_DIM:ATT_Q_W + (h + 1) * RET_V_DIM] = r.astype(BF16)
            kz = (kp * zeta_ref[:, lsl]).astype(BF16)
            u = _dot_tn(kz, vpair)
            state[i] = jnp.concatenate(
                [_GL_PROMPT[2 * i] * sp[:RET_QK_DIM] + u[:RET_QK_DIM, :RET_V_DIM],
                 _GL_PROMPT[2 * i + 1] * sp[RET_QK_DIM:] + u[RET_QK_DIM:, RET_V_DIM:]], axis=0)

        fillers[j]()

    return x + _dot(mix[slot], wout_ref[...]), state


def _prompt_mixer_kernel_p(sinks_ref, xpair_ref, xnext_ref, gmix_ref, win_ref, wout_ref, gng_ref, gnb_ref,
                           dist_ref, mask_ref, decay_ref, xi_ref, zeta_ref,
                           wupf_ref, wdnf_ref, wqf_ref, wof_ref,
                           h_ref, wk_ref, wv_ref, st_ref,
                           wupb_ref, wdnb_ref, wqb_ref, wob_ref,
                           qlo_s, qhi_s, kd0_s, kd1_s, vd0_s, vd1_s,
                           qrlo_s, qrhi_s, kr_s, vr_s, gate_s, mix_s, bias_s, state_s):
    u = pl.program_id(0)
    tm = xnext_ref.shape[0]
    wupb_ref[...] = wupf_ref[...].astype(BF16)
    wdnb_ref[...] = wdnf_ref[...].astype(BF16)
    wqb_ref[...] = wqf_ref[...].astype(BF16)
    wob_ref[...] = wof_ref[...].astype(BF16)
    sc = dict(qlo=qlo_s, qhi=qhi_s, kd0=kd0_s, kd1=kd1_s, vd0=vd0_s, vd1=vd1_s, qrlo=qrlo_s, qrhi=qrhi_s,
              kr=kr_s, vr=vr_s, gate=gate_s, mix=mix_s, bias=bias_s)
    n_pairs = N_RET_HEADS // 2
    blocks = functools.partial(_pm_blocks, tm=tm, sinks_ref=sinks_ref, wout_ref=wout_ref, gng_ref=gng_ref,
                               gnb_ref=gnb_ref, decay_ref=decay_ref, xi_ref=xi_ref, zeta_ref=zeta_ref, sc=sc)

    @pl.when(u == 0)
    def _():
        for h in range(N_ATT_HEADS):
            bias_s[h] = NEG_SLOPES[h] * dist_ref[...] + mask_ref[...]
        state_s[...] = jnp.zeros_like(state_s)
        kd0_s[1] = jnp.zeros(kd0_s.shape[1:], BF16)
        kd1_s[1] = jnp.zeros(kd1_s.shape[1:], BF16)
        vd0_s[1] = jnp.zeros(vd0_s.shape[1:], BF16)
        vd1_s[1] = jnp.zeros(vd1_s.shape[1:], BF16)
        for stage in _pm_project_stages(xpair_ref[0:tm, :], 0, gmix_ref, win_ref, sc):
            stage()

    seq_start = (u % 2) == 0
    state = [jnp.where(seq_start, 0.0, state_s[i * LANES:(i + 1) * LANES, :]) for i in range(n_pairs)]

    prev_kd, prev_vd = _pm_last_block(1, tm, sc)
    stages = _pm_project_stages(xpair_ref[tm:2 * tm, :], 1, gmix_ref, win_ref, sc, kv_out=(wk_ref, wv_ref))
    h0, state = blocks(0, prev_kd, prev_vd, seq_start, state, xpair_ref[0:tm, :], stages)
    h_ref[0:tm, :] = h0

    prev_kd, prev_vd = _pm_last_block(0, tm, sc)
    stages = _pm_project_stages(xnext_ref[...], 0, gmix_ref, win_ref, sc)
    h1, state = blocks(1, prev_kd, prev_vd, False, state, xpair_ref[tm:2 * tm, :], stages)
    h_ref[tm:2 * tm, :] = h1

    for i in range(n_pairs):
        state_s[i * LANES:(i + 1) * LANES, :] = state[i]
        st_ref[0, i * LANES:(i + 1) * LANES, :] = state[i]


def _prompt_mixer_p(x, g_mix, w_in, w_out, sinks, gn_g, gn_b, side_f32):
    b, s, d = x.shape
    tm = TM_MIX
    n_tiles = b * s // tm
    steps = n_tiles // 2
    seq_steps = s // (2 * tm)
    assert s % (2 * tm) == 0 and seq_steps == 2, "kernel assumes 4 tiles per sequence"
    x2d = x.reshape(b * s, d)
    const = lambda shape: pl.BlockSpec(shape, lambda i: (0,) * len(shape), pipeline_mode=pl.Buffered(1))
    slot2 = lambda rows, cols, dt: pltpu.VMEM((2, rows, cols), dt)
    side_specs = [pl.BlockSpec((w.shape[0] // steps, w.shape[1]), lambda i: (i, 0)) for w in side_f32]
    outs = pl.pallas_call(
        _prompt_mixer_kernel_p,
        grid=(steps,),
        in_specs=[
            pl.BlockSpec(memory_space=pltpu.SMEM),
            pl.BlockSpec((2 * tm, d), lambda i: (i, 0)),
            pl.BlockSpec((tm, d), lambda i: (jnp.minimum(2 * i + 2, n_tiles - 1), 0)),
            const((1, d)), const((d, D_IN)), const((MIX_OUT, d)),
            const((1, RET_V_W)), const((1, RET_V_W)),
            const((BLK, 2 * BLK)), const((BLK, 2 * BLK)),
            const((N_RET_HEADS, BLK, BLK)), const((BLK, RET_V_W)), const((BLK, RET_QK_W)),
        ] + side_specs,
        out_specs=[
            pl.BlockSpec((2 * tm, d), lambda i: (i, 0)),
            pl.BlockSpec((1, WINDOW, ATT_KV_W), lambda i: (i // seq_steps, 0, 0)),
            pl.BlockSpec((1, WINDOW, ATT_KV_W), lambda i: (i // seq_steps, 0, 0)),
            pl.BlockSpec((1, RET_QK_W, RET_V_DIM), lambda i: (i // seq_steps, 0, 0)),
        ] + side_specs,
        out_shape=[
            jax.ShapeDtypeStruct((b * s, d), F32),
            jax.ShapeDtypeStruct((b, WINDOW, ATT_KV_W), F32),
            jax.ShapeDtypeStruct((b, WINDOW, ATT_KV_W), F32),
            jax.ShapeDtypeStruct((b, RET_QK_W, RET_V_DIM), F32),
        ] + [jax.ShapeDtypeStruct(w.shape, BF16) for w in side_f32],
        scratch_shapes=[
            slot2(tm, ATT_Q_W, BF16), slot2(tm, ATT_Q_W, BF16),
            slot2(tm, LANES, BF16), slot2(tm, LANES, BF16),
            slot2(tm, 2 * LANES, BF16), slot2(tm, 2 * LANES, BF16),
            slot2(tm, RET_QK_W, BF16), slot2(tm, RET_QK_W, BF16),
            slot2(tm, RET_QK_W, F32), slot2(tm, RET_V_W, BF16),
            slot2(tm, RET_V_W, F32), slot2(tm, MIX_OUT, BF16),
            pltpu.VMEM((N_ATT_HEADS, BLK, 2 * BLK), F32),
            pltpu.VMEM((RET_QK_W, RET_V_DIM), F32),
        ],
        compiler_params=pltpu.CompilerParams(
            dimension_semantics=("arbitrary",), vmem_limit_bytes=VMEM_LIMIT),
        name="prompt_mixer",
    )(sinks, x2d, x2d, g_mix, w_in, w_out, gn_g, gn_b,
      jnp.asarray(_P_DIST), jnp.asarray(_P_MASK), jnp.asarray(_P_DECAY), jnp.asarray(_P_XI),
      jnp.asarray(_P_ZETA), *side_f32)
    return (outs[0].reshape(b, s, d),) + tuple(outs[1:])


def _memkv_kernel(mem_ref, g_ref, wk_ref, wv_ref, win_ref, wout_ref,
                  mk_ref, mv_ref, mkb_ref, mvb_ref, winb_ref, woutb_ref):
    winb_ref[...] = win_ref[...].astype(BF16)
    woutb_ref[...] = wout_ref[...].astype(BF16)
    mn = _rms(mem_ref[...], g_ref[...]).astype(BF16)
    mk = _dot(mn, wk_ref[...].astype(BF16))
    mv = _dot(mn, wv_ref[...].astype(BF16))
    tm = mem_ref.shape[0]
    group = X_D_HALVES * N_X_HEADS
    for hd in range(N_X_HEADS):
        for dh in range(X_D_HALVES):
            cols = slice(hd * X_HEAD_DIM + dh * LANES, hd * X_HEAD_DIM + (dh + 1) * LANES)
            rows = pl.ds(dh * N_X_HEADS + hd, tm, stride=group)
            mk_ref[rows, :] = mk[:, cols]
            mv_ref[rows, :] = mv[:, cols]
    mkb_ref[...] = mk.astype(BF16)
    mvb_ref[...] = mv.astype(BF16)


def _memory_kv(mem2d, g_mem, w_xk, w_xv, w_in, w_out):
    n, d = mem2d.shape
    tm = 512
    row = pl.BlockSpec((tm, d), lambda i: (i, 0))
    rows_out = pl.BlockSpec((tm * d // LANES, LANES), lambda i: (i, 0))
    const = lambda shape: pl.BlockSpec(shape, lambda i: (0,) * len(shape), pipeline_mode=pl.Buffered(1))
    steps = n // tm
    win_blk = pl.BlockSpec((w_in.shape[0] // steps, w_in.shape[1]), lambda i: (i, 0))
    wout_blk = pl.BlockSpec((w_out.shape[0] // steps, w_out.shape[1]), lambda i: (i, 0))
    return pl.pallas_call(
        _memkv_kernel,
        grid=(n // tm,),
        in_specs=[row, const((1, d)), const((d, d)), const((d, d)), win_blk, wout_blk],
        out_specs=[rows_out, rows_out, row, row, win_blk, wout_blk],
        out_shape=[jax.ShapeDtypeStruct((n * d // LANES, LANES), F32),
                   jax.ShapeDtypeStruct((n * d // LANES, LANES), F32),
                   jax.ShapeDtypeStruct((n, d), BF16), jax.ShapeDtypeStruct((n, d), BF16),
                   jax.ShapeDtypeStruct(w_in.shape, BF16), jax.ShapeDtypeStruct(w_out.shape, BF16)],
        compiler_params=pltpu.CompilerParams(
            dimension_semantics=("arbitrary",), vmem_limit_bytes=VMEM_LIMIT),
        name="memory_kv",
    )(mem2d, g_mem, w_xk, w_xv, w_in, w_out)


def _prompt_xattn_kernel(h_ref, g_ref, wq_ref, wo_ref, mk_ref, mv_ref, out_ref, o_s):
    def stages(r0):
        rows = slice(r0, r0 + SUB_ROWS)
        env = {}

        def project():
            env["h"] = h_ref[0, rows, :]
            xn = _rms(env["h"], g_ref[...]).astype(BF16)
            env["q"] = (_dot(xn, wq_ref[...]) * (X_HEAD_DIM ** -0.5)).astype(BF16)

        def head(hd):
            sl = slice(hd * X_HEAD_DIM, (hd + 1) * X_HEAD_DIM)
            s = _dot_nt(env["q"][:, sl], mk_ref[0, :, sl])
            m = jnp.max(s, axis=-1, keepdims=True)
            p = jnp.exp(s - m)
            p = p * (1.0 / jnp.sum(p, axis=-1, keepdims=True))
            o_s[rows, sl] = _dot(p.astype(BF16), mv_ref[0, :, sl]).astype(BF16)

        def output():
            out_ref[0, rows, :] = env["h"] + _dot(o_s[rows, :], wo_ref[...])

        return [project] + [functools.partial(head, hd) for hd in range(N_X_HEADS)] + [output]

    chains = [stages(r0) for r0 in range(0, h_ref.shape[1], SUB_ROWS)]
    n_stage = len(chains[0])
    for step in range(n_stage + len(chains) - 1):
        for lag, chain in enumerate(chains):
            if 0 <= step - lag < n_stage:
                chain[step - lag]()


def _prompt_xattn(h, g, w_xq, w_xo, mkb, mvb):
    b, s, d = h.shape
    tm = TM_X
    const = lambda shape: pl.BlockSpec(shape, lambda i, j: (0,) * len(shape))
    tok = pl.BlockSpec((1, tm, d), lambda i, j: (i, j, 0))
    mem = pl.BlockSpec((1, N_MEM, d), lambda i, j: (i, 0, 0))
    return pl.pallas_call(
        _prompt_xattn_kernel,
        grid=(b, s // tm),
        in_specs=[tok, const((1, d)), const((d, d)), const((d, d)), mem, mem],
        out_specs=tok,
        out_shape=jax.ShapeDtypeStruct((b, s, d), F32),
        scratch_shapes=[pltpu.VMEM((tm, d), BF16)],
        compiler_params=pltpu.CompilerParams(
            dimension_semantics=("arbitrary", "arbitrary"), vmem_limit_bytes=VMEM_LIMIT),
        name="prompt_xattn",
    )(h, g, w_xq, w_xo, mkb, mvb)


def _mlp_rows(h_ref, y_ref, g_ref, wup_ref, wdn_ref, gf_ref):
    for r0 in range(0, h_ref.shape[0], SUB_ROWS):
        rows = slice(r0, r0 + SUB_ROWS)
        h = h_ref[rows, :]
        xn = _rms(h, g_ref[...]).astype(BF16)
        acc = h
        for c in range(D_FF // FF_CHUNK):
            sl = slice(c * FF_CHUNK, (c + 1) * FF_CHUNK)
            u = jnp.maximum(_dot(xn, wup_ref[:, sl]), 0.0)
            acc = acc + _dot((u * u).astype(BF16), wdn_ref[sl, :])
        y_ref[rows, :] = _rms(acc, gf_ref[...])


def _mlp_kernel(hp_ref, hs_ref, g_ref, wup_ref, wdn_ref, gf_ref, yp_ref, ys_ref):
    last = pl.num_programs(0) - 1

    @pl.when(pl.program_id(0) < last)
    def _():
        _mlp_rows(hp_ref, yp_ref, g_ref, wup_ref, wdn_ref, gf_ref)

    @pl.when(pl.program_id(0) == last)
    def _():
        _mlp_rows(hs_ref, ys_ref, g_ref, wup_ref, wdn_ref, gf_ref)


def _mlp_final(hp2d, hs2d, g_mlp, w_up, w_down, g_final):
    n, d = hp2d.shape
    ns = hs2d.shape[0]
    tm = TM_MLP
    n_tiles = n // tm
    prompt = pl.BlockSpec((tm, d), lambda i: (jnp.minimum(i, n_tiles - 1), 0))
    sample = pl.BlockSpec((ns, d), lambda i: (0, 0))
    const = lambda shape: pl.BlockSpec(shape, lambda i: (0,) * len(shape), pipeline_mode=pl.Buffered(1))
    return pl.pallas_call(
        _mlp_kernel,
        grid=(n_tiles + 1,),
        in_specs=[prompt, sample, const((1, d)), const((d, D_FF)), const((D_FF, d)), const((1, d))],
        out_specs=[prompt, sample],
        out_shape=[jax.ShapeDtypeStruct((n, d), F32), jax.ShapeDtypeStruct((ns, d), F32)],
        compiler_params=pltpu.CompilerParams(
            dimension_semantics=("arbitrary",), vmem_limit_bytes=VMEM_LIMIT),
        name="mlp_final",
    )(hp2d, hs2d, g_mlp, w_up, w_down, g_final)


def _sample_mixer_kernel(sinks_ref, x_ref, gmix_ref, win_ref, wout_ref, gng_ref, gnb_ref,
                         ck_ref, cv_ref, st_ref, bias_ref, dec_ref, xi_ref, zeta_ref,
                         h_ref, swk_ref, swv_ref, sst_ref):
    bb = ck_ref.shape[0]
    nt = bb // 2
    x = x_ref[...]
    xn = _rms(x, gmix_ref[...]).astype(BF16)
    tile3 = lambda a: a.reshape(nt, SUBLANES, a.shape[-1])

    q = _dot(xn, win_ref[:, C_QA:C_QA + ATT_Q_W]) * (HEAD_DIM ** -0.5)
    kv = _dot(xn, win_ref[:, C_KV:C_KV + 2 * ATT_KV_W])
    qkr = _dot(xn, win_ref[:, C_QKR:C_QKR + 2 * RET_QK_W])
    vr = _dot(xn, win_ref[:, C_VR:C_VR + RET_V_W])
    gate3 = tile3(_silu(_dot(xn, win_ref[:, C_GR:C_GR + RET_V_W])))

    lo512, hi512 = _half_masks(ATT_Q_W)
    q_r = pltpu.roll(q, HALF, axis=1)
    q_nat3 = tile3(q)
    q_rot3 = tile3(q_r)
    lo3 = lo512.reshape(1, 1, ATT_Q_W)
    hi3 = hi512.reshape(1, 1, ATT_Q_W)
    qa3 = (q_nat3 * lo3).astype(BF16)
    qb3 = (q_rot3 * lo3).astype(BF16)
    qc3 = (q_rot3 * hi3).astype(BF16)
    qd3 = (q_nat3 * hi3).astype(BF16)
    t128 = lambda a, i: a[:, :, i * LANES:(i + 1) * LANES]
    qs = jnp.concatenate([t128(qa3, 0), t128(qb3, 1), t128(qa3, 1), t128(qb3, 2),
                          t128(qc3, 2), t128(qd3, 2), t128(qc3, 3), t128(qd3, 3)], axis=1)

    k3 = tile3(kv[:, :ATT_KV_W])
    v3 = tile3(kv[:, ATT_KV_W:])
    pad_kv = jnp.zeros((nt, BLK - SUBLANES, LANES), BF16)
    knew_pad = jnp.concatenate([k3.astype(BF16), pad_kv], axis=1)
    vnew_pad = jnp.concatenate([v3.astype(BF16), pad_kv], axis=1)
    to_lanes = lambda a3: jnp.swapaxes(
        jnp.concatenate([a3, jnp.zeros((nt, BLK - SUBLANES, LANES), F32)], axis=1), 1, 2)
    k3t, v3t = to_lanes(k3), to_lanes(v3)
    roll3 = lambda a, sh: pltpu.roll(a.reshape(nt * BLK, LANES), sh, axis=1).reshape(nt, BLK, LANES)

    lo256, _ = _half_masks(RET_QK_W)
    qr3 = tile3(qkr[:, :RET_QK_W])
    kr3 = tile3(qkr[:, RET_QK_W:] * (RET_QK_DIM ** -0.5))
    vr3 = tile3(vr)
    lane256 = lax.broadcasted_iota(jnp.int32, (1, 1, RET_QK_W), 2)
    qrs = jnp.concatenate(
        [(qr3 * ((lane256 >= h * RET_QK_DIM) & (lane256 < (h + 1) * RET_QK_DIM)).astype(F32)).astype(BF16)
         for h in range(N_RET_HEADS)],
        axis=1)
    kr_pad = jnp.concatenate([kr3.astype(BF16), jnp.zeros((nt, BLK - SUBLANES, RET_QK_W), BF16)], axis=1)
    vr_pad = jnp.concatenate([vr3.astype(BF16), jnp.zeros((nt, BLK - SUBLANES, RET_V_W), BF16)], axis=1)

    lane = lax.broadcasted_iota(jnp.int32, (1, 1, LANES), 2)
    row8 = lax.broadcasted_iota(jnp.int32, (1, SUBLANES, 1), 1)
    bmm_nt = lambda a, b: jnp.einsum('bqd,bkd->bqk', a, b, preferred_element_type=F32)
    bmm = lambda a, b: jnp.einsum('bqk,bkd->bqd', a, b, preferred_element_type=F32)

    att_par, ret_par = [], []
    for par in range(2):
        bsl = pl.ds(par, nt, stride=2)
        ckt = ck_ref[bsl]
        cvt = cv_ref[bsl]
        keep = lane < WINDOW - DEC_SEQ
        new_shift = WINDOW - DEC_SEQ - DEC_SEQ * par
        swk_ref[bsl] = jnp.where(keep, roll3(ckt, WINDOW - DEC_SEQ), roll3(k3t, new_shift))
        swv_ref[bsl] = jnp.where(keep, roll3(cvt, WINDOW - DEC_SEQ), roll3(v3t, new_shift))

        s = jnp.concatenate([bmm(qs, ckt.astype(BF16)), bmm_nt(qs, knew_pad)], axis=2) + bias_ref[par]
        ps = []
        for h in range(N_ATT_HEADS):
            ps.append(_sink_softmax(s[:, h * SUBLANES:(h + 1) * SUBLANES, :], sinks_ref[h]).astype(BF16))
        p_all = jnp.concatenate(ps, axis=1)
        o = bmm_nt(p_all[:, :, :BLK], cvt.astype(BF16)) + bmm(p_all[:, :, BLK:], vnew_pad)
        o_r = pltpu.roll(o.reshape(nt * N_ATT_HEADS * SUBLANES, LANES), HALF, axis=1).reshape(o.shape)
        hr = lambda a, h: a[:, h * SUBLANES:(h + 1) * SUBLANES, :]
        low = lane < HALF
        att_par.append(jnp.concatenate([
            jnp.where(low, hr(o, 0), hr(o_r, 1)), jnp.where(low, hr(o, 2), hr(o_r, 3)),
            jnp.where(low, hr(o_r, 4), hr(o, 5)), jnp.where(low, hr(o_r, 6), hr(o, 7))], axis=2))

        st = st_ref[bsl]
        oc = bmm(qrs, st.astype(BF16))
        inner = (bmm_nt(qrs, kr_pad) * dec_ref[par]).astype(BF16)
        oi = bmm(inner, vr_pad)
        rs = []
        for h in range(N_RET_HEADS):
            vsl = slice(h * RET_V_DIM, (h + 1) * RET_V_DIM)
            rsl = slice(h * SUBLANES, (h + 1) * SUBLANES)
            o_h = oi[:, rsl, vsl] + oc[:, rsl, :] * xi_ref[par, rsl, :]
            rs.append(_group_norm(o_h, gng_ref[:, vsl], gnb_ref[:, vsl]) * gate3[:, :, vsl])
        ret_par.append(jnp.concatenate(rs, axis=2))

        kz3 = (kr3 * zeta_ref[par]).astype(BF16)
        vr3_b = vr3.astype(BF16)
        for p in range(nt):
            for i in range(N_RET_HEADS // 2):
                u = _dot_tn(kz3[p][:, i * LANES:(i + 1) * LANES],
                            vr3_b[p][:, 2 * i * RET_V_DIM:(2 * i + 2) * RET_V_DIM])
                for half in range(2):
                    h = 2 * i + half
                    dsl = slice(h * RET_QK_DIM, (h + 1) * RET_QK_DIM)
                    sst_ref[2 * p + par, dsl, :] = (
                        _GL_SAMPLE[h] * st[p, dsl, :]
                        + u[half * RET_QK_DIM:(half + 1) * RET_QK_DIM, half * RET_V_DIM:(half + 1) * RET_V_DIM])

    own0 = row8 < DEC_SEQ
    att3 = jnp.where(own0, att_par[0], att_par[1])
    ret3 = jnp.where(own0, ret_par[0], ret_par[1])
    mix = jnp.concatenate([att3, ret3], axis=2).reshape(2 * nt * DEC_SEQ, MIX_OUT).astype(BF16)
    h_ref[...] = x + _dot(mix, wout_ref[...])


def _sample_mixer(x2d, g_mix, w_in, w_out, sinks, gn_g, gn_b, ck, cv, st):
    n, d = x2d.shape
    nb = ck.shape[0]
    bb = BB_MIX
    r = bb * DEC_SEQ
    const = lambda shape: pl.BlockSpec(shape, lambda i: (0,) * len(shape))
    row = pl.BlockSpec((r, d), lambda i: (i, 0))
    win = pl.BlockSpec((bb, WINDOW, ATT_KV_W), lambda i: (i, 0, 0))
    state = pl.BlockSpec((bb, RET_QK_W, RET_V_DIM), lambda i: (i, 0, 0))
    return pl.pallas_call(
        _sample_mixer_kernel,
        grid=(nb // bb,),
        in_specs=[
            pl.BlockSpec(memory_space=pltpu.SMEM),
            row, const((1, d)), const((d, D_IN)), const((MIX_OUT, d)),
            const((1, RET_V_W)), const((1, RET_V_W)),
            win, win, state,
            const(_S_BIAS.shape), const(_S_DEC.shape), const(_S_XI.shape), const(_S_ZETA.shape),
        ],
        out_specs=[row, win, win, state],
        out_shape=[
            jax.ShapeDtypeStruct((n, d), F32),
            jax.ShapeDtypeStruct((nb, WINDOW, ATT_KV_W), F32),
            jax.ShapeDtypeStruct((nb, WINDOW, ATT_KV_W), F32),
            jax.ShapeDtypeStruct((nb, RET_QK_W, RET_V_DIM), F32),
        ],
        compiler_params=pltpu.CompilerParams(
            dimension_semantics=("arbitrary",), vmem_limit_bytes=VMEM_LIMIT),
        name="sample_mixer",
    )(sinks, x2d, g_mix, w_in, w_out, gn_g, gn_b, ck, cv, st,
      jnp.asarray(_S_BIAS), jnp.asarray(_S_DEC), jnp.asarray(_S_XI), jnp.asarray(_S_ZETA))


def _head_slab(x_ref, b, hd):
    group = X_D_HALVES * N_X_HEADS
    halves = [x_ref[b, pl.ds(dh * N_X_HEADS + hd, N_MEM, stride=group), :] for dh in range(X_D_HALVES)]
    return jnp.concatenate(halves, axis=1).astype(BF16)


def _sample_xattn_kernel(h_ref, g_ref, wq_ref, wo_ref, xk_ref, xv_ref, out_ref):
    bb = xk_ref.shape[0]
    nt = bb // 2
    h = h_ref[...]
    xn = _rms(h, g_ref[...]).astype(BF16)
    q = _dot(xn, wq_ref[...]) * (X_HEAD_DIM ** -0.5)
    units = [(t, par, hd) for t in range(nt) for par in range(2) for hd in range(N_X_HEADS)]
    qts = [q[t * SUBLANES:(t + 1) * SUBLANES].astype(BF16) for t in range(nt)]
    s = jnp.concatenate(
        [_dot_nt(qts[t][:, hd * X_HEAD_DIM:(hd + 1) * X_HEAD_DIM], _head_slab(xk_ref, 2 * t + par, hd))
         for t, par, hd in units], axis=0)
    m = jnp.max(s, axis=-1, keepdims=True)
    p = jnp.exp(s - m)
    p = p * (1.0 / jnp.sum(p, axis=-1, keepdims=True))
    os_ = {}
    for i, (t, par, hd) in enumerate(units):
        pi = p[i * SUBLANES:(i + 1) * SUBLANES].astype(BF16)
        os_[(t, par, hd)] = _dot(pi, _head_slab(xv_ref, 2 * t + par, hd))
    own0 = lax.broadcasted_iota(jnp.int32, (SUBLANES, 1), 0) < DEC_SEQ
    o_tiles = []
    for t in range(nt):
        o_par = [jnp.concatenate([os_[(t, par, hd)] for hd in range(N_X_HEADS)], axis=1) for par in range(2)]
        o_tiles.append(jnp.where(own0, o_par[0], o_par[1]))
    o = jnp.concatenate(o_tiles, axis=0).astype(BF16)
    out_ref[...] = h + _dot(o, wo_ref[...])


def _mlp_value(h, g_ref, wup_ref, wdn_ref, gf_ref, fillers=None):
    xn = _rms(h, g_ref[...]).astype(BF16)
    acc = h
    piece = FF_CHUNK // N_X_HEADS
    opiece = D_MODEL // N_X_HEADS
    for c in range(D_FF // FF_CHUNK):
        qk, softmax, pv = fillers[c] if fillers is not None else (None, None, None)
        hid = []
        for k in range(N_X_HEADS):
            cols = slice(c * FF_CHUNK + k * piece, c * FF_CHUNK + (k + 1) * piece)
            u = jnp.maximum(_dot(xn, wup_ref[:, cols]), 0.0)
            hid.append((u * u).astype(BF16))
            if qk is not None:
                qk(k)
        if softmax is not None:
            softmax()
        hid = jnp.concatenate(hid, axis=1)
        rows_c = slice(c * FF_CHUNK, (c + 1) * FF_CHUNK)
        out = []
        for k in range(N_X_HEADS):
            out.append(_dot(hid, wdn_ref[rows_c, k * opiece:(k + 1) * opiece]))
            if pv is not None:
                pv(k)
        acc = acc + jnp.concatenate(out, axis=1)
    return _rms(acc, gf_ref[...])


def _mlp_xattn_kernel(hp_ref, hsm_ref, gx_ref, wq_ref, wo_ref, xk_ref, xv_ref, g_ref, wup_ref, wdn_ref, gf_ref,
                      yp_ref, ys_ref):
    i = pl.program_id(0)
    n = pl.num_programs(0) - 1
    bb = xk_ref.shape[0]
    rows = bb * DEC_SEQ
    assert bb == D_FF // FF_CHUNK and bb % 2 == 0

    @pl.when(i == 0)
    def _():
        xn = _rms(hsm_ref[...], gx_ref[...]).astype(BF16)
        ys_ref[...] = _dot(xn, wq_ref[...]) * (X_HEAD_DIM ** -0.5)

    @pl.when(i < n)
    def _():
        r0 = pl.multiple_of(i * rows, rows)
        own0 = lax.broadcasted_iota(jnp.int32, (SUBLANES, 1), 0) < DEC_SEQ
        o_rows = {}

        def attend(b):
            t = b // 2
            tile_rows = pl.ds(r0 + t * SUBLANES, SUBLANES)
            env = dict(s=[], o=[])

            def qk(hd):
                if hd == 0:
                    env["q"] = ys_ref[tile_rows, :].astype(BF16)
                env["s"].append(_dot_nt(env["q"][:, hd * X_HEAD_DIM:(hd + 1) * X_HEAD_DIM],
                                        _head_slab(xk_ref, b, hd)))

            def softmax():
                s = jnp.concatenate(env["s"], axis=0)
                m = jnp.max(s, axis=-1, keepdims=True)
                p = jnp.exp(s - m)
                env["p"] = p * (1.0 / jnp.sum(p, axis=-1, keepdims=True))

            def pv(hd):
                p = env["p"][hd * SUBLANES:(hd + 1) * SUBLANES].astype(BF16)
                env["o"].append(_dot(p, _head_slab(xv_ref, b, hd)))
                if hd == N_X_HEADS - 1:
                    o_rows[b] = jnp.concatenate(env["o"], axis=1)
                    if b % 2 == 1:
                        ys_ref[tile_rows, :] = jnp.where(own0, o_rows[b - 1], o_rows[b])

            return qk, softmax, pv

        fillers = [attend(b) for b in range(bb)]
        yp_ref[...] = _mlp_value(hp_ref[...], g_ref, wup_ref, wdn_ref, gf_ref, fillers)

    @pl.when(i == n)
    def _():
        hs = hsm_ref[...] + _dot(ys_ref[...].astype(BF16), wo_ref[...])
        ys_ref[...] = _mlp_value(hs, g_ref, wup_ref, wdn_ref, gf_ref)


def _mlp_xattn(hp2d, hsm, g_xattn, w_xq, w_xo, xk, xv, g_mlp, w_up, w_down, g_final):
    n, d = hp2d.shape
    ns = hsm.shape[0]
    nb = xk.shape[0]
    bb = BB_X
    tm = n // (nb // bb)
    n_tiles = n // tm
    assert n_tiles * bb == nb and tm % SUBLANES == 0
    clip = lambda i: jnp.minimum(i, n_tiles - 1)
    prompt = pl.BlockSpec((tm, d), lambda i: (clip(i), 0))
    mem = pl.BlockSpec((bb,) + xk.shape[1:], lambda i: (clip(i), 0, 0))
    const = lambda shape: pl.BlockSpec(shape, lambda i: (0,) * len(shape), pipeline_mode=pl.Buffered(1))
    return pl.pallas_call(
        _mlp_xattn_kernel,
        grid=(n_tiles + 1,),
        in_specs=[prompt, const((ns, d)), const((1, d)), const((d, d)), const((d, d)), mem, mem,
                  const((1, d)), const((d, D_FF)), const((D_FF, d)), const((1, d))],
        out_specs=[prompt, pl.BlockSpec((ns, d), lambda i: (0, 0))],
        out_shape=[jax.ShapeDtypeStruct((n, d), F32), jax.ShapeDtypeStruct((ns, d), F32)],
        compiler_params=pltpu.CompilerParams(
            dimension_semantics=("arbitrary",), vmem_limit_bytes=VMEM_LIMIT),
        name="mlp_xattn",
    )(hp2d, hsm, g_xattn, w_xq, w_xo, xk, xv, g_mlp, w_up, w_down, g_final)


def _mem_rows(c):
    nb = c.shape[0]
    c = c.reshape(nb, N_MEM, N_X_HEADS, X_D_HALVES, LANES)
    return jnp.transpose(c, (0, 1, 3, 2, 4)).reshape(nb, N_MEM * X_D_HALVES * N_X_HEADS, LANES)


def _sample_xattn(h2d, g, w_xq, w_xo, mk, mv):
    n, d = h2d.shape
    nb = mk.shape[0]
    bb = BB_X
    r = bb * DEC_SEQ
    const = lambda shape: pl.BlockSpec(shape, lambda i: (0,) * len(shape))
    row = pl.BlockSpec((r, d), lambda i: (i, 0))
    mem = pl.BlockSpec((bb,) + mk.shape[1:], lambda i: (i, 0, 0))
    return pl.pallas_call(
        _sample_xattn_kernel,
        grid=(nb // bb,),
        in_specs=[row, const((1, d)), const((d, d)), const((d, d)), mem, mem],
        out_specs=row,
        out_shape=jax.ShapeDtypeStruct((n, d), F32),
        compiler_params=pltpu.CompilerParams(
            dimension_semantics=("arbitrary",), vmem_limit_bytes=VMEM_LIMIT),
        name="sample_xattn",
    )(h2d, g, w_xq, w_xo, mk, mv)


def kernel(x_prompt, x_sample, mem_prompt, cache_win_k, cache_win_v, state_ret, cache_mem_k, cache_mem_v,
           g_mix, w_in, attn_sinks, ret_gn_g, ret_gn_b, w_out, g_xattn, g_mem, w_xq, w_xk, w_xv, w_xo,
           g_mlp, w_up, w_down, g_final):
    depth = w_in.shape[0]
    assert depth == 1, "single-layer trunk"
    b, s, d = x_prompt.shape
    nb, ls, _ = x_sample.shape
    row = lambda a: a.reshape(1, -1)
    sinks = attn_sinks[0]
    gn_g, gn_b = row(ret_gn_g[0]), row(ret_gn_b[0])
    g_fin = row(g_final)

    mk, mv, mkb, mvb, w_in_b, w_out_b = _memory_kv(
        mem_prompt.reshape(b * N_MEM, d), row(g_mem[0]), w_xk[0], w_xv[0], w_in[0], w_out[0])
    hp, p_wk, p_wv, p_rs, w_up_b, w_dn_b, w_xq_b, w_xo_b = _prompt_mixer_p(
        x_prompt, row(g_mix[0]), w_in_b, w_out_b, sinks, gn_g, gn_b,
        (w_up[0], w_down[0], w_xq[0], w_xo[0]))
    hp = _prompt_xattn(hp, row(g_xattn[0]), w_xq_b, w_xo_b,
                       mkb.reshape(b, N_MEM, d), mvb.reshape(b, N_MEM, d))

    win_t = lambda c: jnp.transpose(c, (0, 2, 3, 1)).reshape(nb, ATT_KV_W, WINDOW)
    win_t_inv = lambda a: jnp.transpose(a.reshape(nb, N_KV_HEADS, HEAD_DIM, WINDOW),
                                        (0, 3, 1, 2)).reshape(1, nb, WINDOW, N_KV_HEADS, HEAD_DIM)
    hs, s_wk, s_wv, s_rs = _sample_mixer(
        x_sample.reshape(nb * ls, d), row(g_mix[0]), w_in_b, w_out_b, sinks, gn_g, gn_b,
        win_t(cache_win_k[0]), win_t(cache_win_v[0]), state_ret[0].reshape(nb, RET_QK_W, RET_V_DIM))

    y_prompt, y_sample = _mlp_xattn(
        hp.reshape(b * s, d), hs, row(g_xattn[0]), w_xq_b, w_xo_b,
        _mem_rows(cache_mem_k[0]), _mem_rows(cache_mem_v[0]), row(g_mlp[0]), w_up_b, w_dn_b, g_fin)
    y_prompt = y_prompt.reshape(b, s, d)
    y_sample = y_sample.reshape(nb, ls, d)

    win5 = lambda a, n: a.reshape(1, n, WINDOW, N_KV_HEADS, HEAD_DIM)
    ret5 = lambda a, n: a.reshape(1, n, N_RET_HEADS, RET_QK_DIM, RET_V_DIM)
    mem5 = lambda a: jnp.transpose(a.reshape(b, N_MEM, X_D_HALVES, N_X_HEADS, LANES),
                                   (0, 1, 3, 2, 4)).reshape(1, b, N_MEM, N_X_HEADS, X_HEAD_DIM)
    return (y_prompt, y_sample,
            win5(p_wk, b), win5(p_wv, b), ret5(p_rs, b), mem5(mk), mem5(mv),
            win_t_inv(s_wk), win_t_inv(s_wv), ret5(s_rs, nb))
```

```python
import functools

import jax
import jax.numpy as jnp
import numpy as np
from jax import lax
from jax.experimental import pallas as pl
from jax.experimental.pallas import tpu as pltpu

F32 = jnp.float32
BF16 = jnp.bfloat16

D_MODEL = 1024
BATCH = 8
SEQ = 2048
DEC_BATCH = 128
DEC_SEQ = 4
HEAD_DIM = 64
N_ATT_HEADS = 8
N_KV_HEADS = 2
KV_GROUP = N_ATT_HEADS // N_KV_HEADS
WINDOW = 128
BLK = 128
N_RET_HEADS = 4
RET_QK_DIM = 64
RET_V_DIM = 128
N_MEM = 256
N_X_HEADS = 4
X_HEAD_DIM = D_MODEL // N_X_HEADS
D_FF = 4 * D_MODEL
RMS_EPS = 1e-6
GN_EPS = 1e-5

ATT_Q_W = N_ATT_HEADS * HEAD_DIM
ATT_KV_W = N_KV_HEADS * HEAD_DIM
RET_QK_W = N_RET_HEADS * RET_QK_DIM
RET_V_W = N_RET_HEADS * RET_V_DIM
MIX_OUT = ATT_Q_W + RET_V_W
D_IN = ATT_Q_W + 2 * ATT_KV_W + 2 * RET_QK_W + 2 * RET_V_W
C_QA, C_KV, C_QKR, C_VR, C_GR = 0, 512, 768, 1280, 1792

LANES = 128
SUBLANES = 8
HALF = LANES // 2
X_D_HALVES = X_HEAD_DIM // LANES
NEG = -1e30
VMEM_LIMIT = 56 * 1024 * 1024

TM_MIX = 512
TM_X = 2048
TM_MLP = 1024
SUB_ROWS = 512
FF_CHUNK = 1024
BB_MIX = 16
BB_X = 4

NEG_SLOPES = [-(2.0 ** (-8.0 * (i + 1) / N_ATT_HEADS)) for i in range(N_ATT_HEADS)]
_LOG_G = np.log(1.0 - 2.0 ** (-5.0 - np.arange(N_RET_HEADS))).astype(np.float32).astype(np.float64)


def _prompt_tables():
    qi = np.arange(BLK)[:, None]
    kj = np.arange(2 * BLK)[None, :]
    dist = (qi + BLK - kj).astype(np.float64)
    mask = np.where((dist >= 0) & (dist < WINDOW), 0.0, NEG)
    l = np.arange(BLK, dtype=np.float64)
    diff = l[:, None] - l[None, :]
    decay = np.where(diff >= 0, np.exp(_LOG_G[:, None, None] * np.maximum(diff, 0.0)), 0.0)
    xi = np.exp((l[:, None] + 1.0) * _LOG_G[None, :])
    zeta = np.exp((BLK - 1.0 - l)[:, None] * _LOG_G[None, :])
    xi_t = np.repeat(xi, RET_V_DIM, axis=1)
    zeta_t = np.repeat(zeta, RET_QK_DIM, axis=1)
    f = lambda a: np.asarray(a, np.float32)
    return f(dist), f(mask), f(decay), f(xi_t), f(zeta_t)


def _sample_tables():
    slopes = -np.asarray(NEG_SLOPES)
    bias = np.full((2, N_ATT_HEADS * SUBLANES, 2 * BLK), NEG, np.float64)
    dec = np.zeros((2, N_RET_HEADS * SUBLANES, BLK), np.float64)
    xi = np.zeros((2, N_RET_HEADS * SUBLANES, RET_V_DIM), np.float64)
    zeta = np.zeros((2, SUBLANES, RET_QK_W), np.float64)
    for par in range(2):
        for r in range(SUBLANES):
            own = DEC_SEQ * par <= r < DEC_SEQ * (par + 1)
            t = r - DEC_SEQ * par if own else r % DEC_SEQ
            for h in range(N_ATT_HEADS):
                row = h * SUBLANES + r
                for j in range(WINDOW):
                    d = t + WINDOW - j
                    if 0 <= d < WINDOW:
                        bias[par, row, j] = -slopes[h] * d
                for c in range(DEC_SEQ):
                    d = t - c
                    if d >= 0:
                        bias[par, row, WINDOW + DEC_SEQ * par + c] = -slopes[h] * d
            for h in range(N_RET_HEADS):
                row = h * SUBLANES + r
                if own:
                    xi[par, row, :] = np.exp((t + 1.0) * _LOG_G[h])
                    zeta[par, r, h * RET_QK_DIM:(h + 1) * RET_QK_DIM] = np.exp((DEC_SEQ - 1.0 - t) * _LOG_G[h])
                    for c in range(t + 1):
                        dec[par, row, DEC_SEQ * par + c] = np.exp(_LOG_G[h] * (t - c))
    f = lambda a: np.asarray(a, np.float32)
    return f(bias), f(dec), f(xi), f(zeta)


_P_DIST, _P_MASK, _P_DECAY, _P_XI, _P_ZETA = _prompt_tables()
_S_BIAS, _S_DEC, _S_XI, _S_ZETA = _sample_tables()
_GL_PROMPT = [float(np.exp(_LOG_G[h] * BLK)) for h in range(N_RET_HEADS)]
_GL_SAMPLE = [float(np.exp(_LOG_G[h] * DEC_SEQ)) for h in range(N_RET_HEADS)]


def _rms(x, g):
    return x * lax.rsqrt(jnp.mean(x * x, axis=-1, keepdims=True) + RMS_EPS) * g


def _dot(a, b):
    return jnp.dot(a, b, preferred_element_type=F32)


def _dot_nt(a, b):
    return lax.dot_general(a, b, (((1,), (1,)), ((), ())), preferred_element_type=F32)


def _dot_tn(a, b):
    return lax.dot_general(a, b, (((0,), (0,)), ((), ())), preferred_element_type=F32)


def _silu(g):
    return g * (1.0 / (1.0 + jnp.exp(-g)))


def _half_masks(width):
    lane = lax.broadcasted_iota(jnp.int32, (1, width), 1)
    lo = ((lane & (LANES - 1)) < HALF).astype(F32)
    return lo, 1.0 - lo


def _sink_softmax(s, sink):
    m = jnp.maximum(jnp.max(s, axis=-1, keepdims=True), sink)
    p = jnp.exp(s - m)
    den = jnp.sum(p, axis=-1, keepdims=True) + jnp.exp(sink - m)
    return p * (1.0 / den)


def _group_norm(o, g, b):
    mu = jnp.mean(o, axis=-1, keepdims=True)
    d = o - mu
    var = jnp.mean(d * d, axis=-1, keepdims=True)
    return d * lax.rsqrt(var + GN_EPS) * g + b


def _prompt_mixer_kernel(sinks_ref, x_ref, gmix_ref, win_ref, wout_ref, gng_ref, gnb_ref,
                         dist_ref, mask_ref, decay_ref, xi_ref, zeta_ref,
                         h_ref, wk_ref, wv_ref, st_ref,
                         qlo_s, qhi_s, kd0_s, kd1_s, vd0_s, vd1_s,
                         qrlo_s, qrhi_s, kr_s, vr_s, gate_s, mix_s, bias_s):
    t = pl.program_id(1)
    nt = pl.num_programs(1)
    tm = x_ref.shape[1]
    nblk = tm // BLK

    @pl.when(t == 0)
    def _():
        kd0_s[0:BLK, :] = jnp.zeros((BLK, LANES), BF16)
        kd1_s[0:BLK, :] = jnp.zeros((BLK, LANES), BF16)
        vd0_s[0:BLK, :] = jnp.zeros((BLK, 2 * LANES), BF16)
        vd1_s[0:BLK, :] = jnp.zeros((BLK, 2 * LANES), BF16)
        st_ref[...] = jnp.zeros_like(st_ref)

    @pl.when(t > 0)
    def _():
        kd0_s[0:BLK, :] = kd0_s[tm:tm + BLK, :]
        kd1_s[0:BLK, :] = kd1_s[tm:tm + BLK, :]
        vd0_s[0:BLK, :] = vd0_s[tm:tm + BLK, :]
        vd1_s[0:BLK, :] = vd1_s[tm:tm + BLK, :]

    @pl.when((t == 0) & (pl.program_id(0) == 0))
    def _():
        for h in range(N_ATT_HEADS):
            bias_s[h] = NEG_SLOPES[h] * dist_ref[...] + mask_ref[...]

    x = x_ref[0]
    xn = _rms(x, gmix_ref[...]).astype(BF16)

    z = _dot(xn, win_ref[...])
    lo512, hi512 = _half_masks(ATT_Q_W)
    q = z[:, C_QA:C_QA + ATT_Q_W]
    qlo_s[...] = (q * (lo512 * HEAD_DIM ** -0.5)).astype(BF16)
    qhi_s[...] = (q * (hi512 * HEAD_DIM ** -0.5)).astype(BF16)

    low = lax.broadcasted_iota(jnp.int32, (tm, LANES), 1) < HALF
    kv = z[:, C_KV:C_KV + 2 * ATT_KV_W]
    k = kv[:, :ATT_KV_W]
    v = kv[:, ATT_KV_W:]
    k_r = pltpu.roll(k, HALF, axis=1)
    v_r = pltpu.roll(v, HALF, axis=1)
    kd0_s[BLK:BLK + tm, :] = jnp.where(low, k, k_r).astype(BF16)
    kd1_s[BLK:BLK + tm, :] = jnp.where(low, k_r, k).astype(BF16)
    vd0_s[BLK:BLK + tm, 0:LANES] = jnp.where(low, v, 1.0).astype(BF16)
    vd0_s[BLK:BLK + tm, LANES:2 * LANES] = jnp.where(low, 1.0, v_r).astype(BF16)
    vd1_s[BLK:BLK + tm, 0:LANES] = jnp.where(low, v_r, 1.0).astype(BF16)
    vd1_s[BLK:BLK + tm, LANES:2 * LANES] = jnp.where(low, 1.0, v).astype(BF16)

    @pl.when(t == nt - 1)
    def _():
        wk_ref[0] = k[tm - WINDOW:, :]
        wv_ref[0] = v[tm - WINDOW:, :]

    lo256, hi256 = _half_masks(RET_QK_W)
    qkr = z[:, C_QKR:C_QKR + 2 * RET_QK_W]
    qr = qkr[:, :RET_QK_W]
    qrlo_s[...] = (qr * lo256).astype(BF16)
    qrhi_s[...] = (qr * hi256).astype(BF16)
    kr_s[...] = qkr[:, RET_QK_W:] * (RET_QK_DIM ** -0.5)
    vr_s[...] = z[:, C_VR:C_VR + RET_V_W].astype(BF16)
    gate_s[...] = _silu(z[:, C_GR:C_GR + RET_V_W])

    lowb =lax.broadcasted_iota(jnp.int32, (BLK, LANES), 1) < HALF
    col = lax.broadcasted_iota(jnp.int32, (BLK, 2 * BLK), 1)
    first_mask = jnp.where((col < BLK) & (t == 0), NEG, 0.0)
    kd_refs = (kd0_s, kd1_s)
    vd_refs = (vd0_s, vd1_s)
    n_pairs = N_RET_HEADS // 2
    state = [st_ref[0, i * LANES:(i + 1) * LANES, :] for i in range(n_pairs)]

    for j in range(nblk):
        rows = slice(j * BLK, (j + 1) * BLK)
        krows = slice(j * BLK, (j + 2) * BLK)

        for kvh in range(N_KV_HEADS):
            kd = kd_refs[kvh][krows, :]
            vd = vd_refs[kvh][krows, :]
            c0 = kvh * KV_GROUP * HEAD_DIM
            qst = jnp.concatenate([qlo_s[rows, c0:c0 + LANES], qhi_s[rows, c0:c0 + LANES],
                                   qlo_s[rows, c0 + LANES:c0 + 2 * LANES],
                                   qhi_s[rows, c0 + LANES:c0 + 2 * LANES]], axis=0)
            s = _dot_nt(qst, kd)
            es, esink = [], []
            for g in range(KV_GROUP):
                h = kvh * KV_GROUP + g
                sg = s[g * BLK:(g + 1) * BLK] + bias_s[h]
                if j == 0:
                    sg = sg + first_mask
                sink = sinks_ref[h]
                m = jnp.maximum(jnp.max(sg, axis=-1, keepdims=True), sink)
                es.append(jnp.exp(sg - m).astype(BF16))
                esink.append(jnp.exp(sink - m))
            o = _dot(jnp.concatenate(es, axis=0), vd)
            for pair in range(KV_GROUP // 2):
                oe = o[2 * pair * BLK:(2 * pair + 1) * BLK]
                oo = o[(2 * pair + 1) * BLK:(2 * pair + 2) * BLK]
                num = jnp.where(lowb, oe[:, :LANES], oo[:, LANES:])
                den = (jnp.where(lowb, oe[:, LANES:], oo[:, :LANES])
                       + jnp.where(lowb, esink[2 * pair], esink[2 * pair + 1]))
                cs = c0 + pair * LANES
                mix_s[rows, cs:cs + LANES] = (num * (1.0 / den)).astype(BF16)

        for i in range(n_pairs):
            lsl = slice(i * LANES, (i + 1) * LANES)
            kp = kr_s[rows, lsl]
            sp = state[i]
            vpair = vr_s[rows, 2 * i * RET_V_DIM:(2 * i + 2) * RET_V_DIM]
            q2 = jnp.concatenate([qrlo_s[rows, lsl], qrhi_s[rows, lsl]], axis=0)
            a = _dot_nt(q2, kp.astype(BF16))
            inner = jnp.concatenate([a[:BLK] * decay_ref[2 * i], a[BLK:] * decay_ref[2 * i + 1]], axis=0)
            oi = _dot(inner.astype(BF16), vpair)
            oc = _dot(q2, sp.astype(BF16))
            for half in range(2):
                h = 2 * i + half
                vsl = slice(h * RET_V_DIM, (h + 1) * RET_V_DIM)
                hr = slice(half * BLK, (half + 1) * BLK)
                o = oi[hr, half * RET_V_DIM:(half + 1) * RET_V_DIM] + oc[hr] * xi_ref[:, vsl]
                r = _group_norm(o, gng_ref[:, vsl], gnb_ref[:, vsl]) * gate_s[rows, vsl]
                mix_s[rows, ATT_Q_W + h * RET_V_DIM:ATT_Q_W + (h + 1) * RET_V_DIM] = r.astype(BF16)
            kz = (kp * zeta_ref[:, lsl]).astype(BF16)
            u = _dot_tn(kz, vpair)
            state[i] = jnp.concatenate(
                [_GL_PROMPT[2 * i] * sp[:RET_QK_DIM] + u[:RET_QK_DIM, :RET_V_DIM],
                 _GL_PROMPT[2 * i + 1] * sp[RET_QK_DIM:] + u[RET_QK_DIM:, RET_V_DIM:]], axis=0)

    for i in range(n_pairs):
        st_ref[0, i * LANES:(i + 1) * LANES, :] = state[i]

    h_ref[0] = x + _dot(mix_s[...], wout_ref[...])


def _prompt_mixer(x, g_mix, w_in, w_out, sinks, gn_g, gn_b):
    b, s, d = x.shape
    tm = TM_MIX
    const = lambda shape: pl.BlockSpec(shape, lambda i, j: (0,) * len(shape))
    return pl.pallas_call(
        _prompt_mixer_kernel,
        grid=(b, s // tm),
        in_specs=[
            pl.BlockSpec(memory_space=pltpu.SMEM),
            pl.BlockSpec((1, tm, d), lambda i, j: (i, j, 0)),
            const((1, d)), const((d, D_IN)), const((MIX_OUT, d)),
            const((1, RET_V_W)), const((1, RET_V_W)),
            const((BLK, 2 * BLK)), const((BLK, 2 * BLK)),
            const((N_RET_HEADS, BLK, BLK)), const((BLK, RET_V_W)), const((BLK, RET_QK_W)),
        ],
        out_specs=[
            pl.BlockSpec((1, tm, d), lambda i, j: (i, j, 0)),
            pl.BlockSpec((1, WINDOW, ATT_KV_W), lambda i, j: (i, 0, 0)),
            pl.BlockSpec((1, WINDOW, ATT_KV_W), lambda i, j: (i, 0, 0)),
            pl.BlockSpec((1, RET_QK_W, RET_V_DIM), lambda i, j: (i, 0, 0)),
        ],
        out_shape=[
            jax.ShapeDtypeStruct((b, s, d), F32),
            jax.ShapeDtypeStruct((b, WINDOW, ATT_KV_W), F32),
            jax.ShapeDtypeStruct((b, WINDOW, ATT_KV_W), F32),
            jax.ShapeDtypeStruct((b, RET_QK_W, RET_V_DIM), F32),
        ],
        scratch_shapes=[
            pltpu.VMEM((tm, ATT_Q_W), BF16), pltpu.VMEM((tm, ATT_Q_W), BF16),
            pltpu.VMEM((tm + BLK, LANES), BF16), pltpu.VMEM((tm + BLK, LANES), BF16),
            pltpu.VMEM((tm + BLK, 2 * LANES), BF16), pltpu.VMEM((tm + BLK, 2 * LANES), BF16),
            pltpu.VMEM((tm, RET_QK_W), BF16), pltpu.VMEM((tm, RET_QK_W), BF16),
            pltpu.VMEM((tm, RET_QK_W), F32), pltpu.VMEM((tm, RET_V_W), BF16),
            pltpu.VMEM((tm, RET_V_W), F32), pltpu.VMEM((tm, MIX_OUT), BF16),
            pltpu.VMEM((N_ATT_HEADS, BLK, 2 * BLK), F32),
        ],
        compiler_params=pltpu.CompilerParams(
            dimension_semantics=("arbitrary", "arbitrary"), vmem_limit_bytes=VMEM_LIMIT),
        name="prompt_mixer",
    )(sinks, x, g_mix, w_in, w_out, gn_g, gn_b,
      jnp.asarray(_P_DIST), jnp.asarray(_P_MASK), jnp.asarray(_P_DECAY), jnp.asarray(_P_XI),
      jnp.asarray(_P_ZETA))


def _pm_project_stages(x, slot, gmix_ref, win_ref, sc, kv_out=None):
    tm = x.shape[0]
    xn = _rms(x, gmix_ref[...]).astype(BF16)

    pw = 2 * LANES
    lo, hi = _half_masks(pw)

    def stage_q(i):
        cols = slice(i * pw, (i + 1) * pw)
        q = _dot(xn, win_ref[:, C_QA + i * pw:C_QA + (i + 1) * pw])
        sc["qlo"][slot, :, cols] = (q * (lo * HEAD_DIM ** -0.5)).astype(BF16)
        sc["qhi"][slot, :, cols] = (q * (hi * HEAD_DIM ** -0.5)).astype(BF16)

    def stage_kv():
        z = _dot(xn, win_ref[:, C_KV:C_KV + pw])
        low = lax.broadcasted_iota(jnp.int32, (tm, LANES), 1) < HALF
        k = z[:, 0:ATT_KV_W]
        v = z[:, ATT_KV_W:2 * ATT_KV_W]
        if kv_out is not None:
            kv_out[0][0] = k[tm - WINDOW:, :]
            kv_out[1][0] = v[tm - WINDOW:, :]
        k_r = pltpu.roll(k, HALF, axis=1)
        v_r = pltpu.roll(v, HALF, axis=1)
        sc["kd0"][slot] = jnp.where(low, k, k_r).astype(BF16)
        sc["kd1"][slot] = jnp.where(low, k_r, k).astype(BF16)
        sc["vd0"][slot, :, 0:LANES] = jnp.where(low, v, 1.0).astype(BF16)
        sc["vd0"][slot, :, LANES:2 * LANES] = jnp.where(low, 1.0, v_r).astype(BF16)
        sc["vd1"][slot, :, 0:LANES] = jnp.where(low, v_r, 1.0).astype(BF16)
        sc["vd1"][slot, :, LANES:2 * LANES] = jnp.where(low, 1.0, v).astype(BF16)

    def stage_qr():
        qr = _dot(xn, win_ref[:, C_QKR:C_QKR + pw])
        sc["qrlo"][slot] = (qr * lo).astype(BF16)
        sc["qrhi"][slot] = (qr * hi).astype(BF16)

    def stage_kr():
        sc["kr"][slot] = _dot(xn, win_ref[:, C_QKR + pw:C_QKR + 2 * pw]) * (RET_QK_DIM ** -0.5)

    def stage_vr(i):
        cols = slice(i * pw, (i + 1) * pw)
        sc["vr"][slot, :, cols] = _dot(xn, win_ref[:, C_VR + i * pw:C_VR + (i + 1) * pw]).astype(BF16)

    def stage_gate(i):
        cols = slice(i * pw, (i + 1) * pw)
        sc["gate"][slot, :, cols] = _silu(_dot(xn, win_ref[:, C_GR + i * pw:C_GR + (i + 1) * pw]))

    part = functools.partial
    return [part(stage_q, 0), part(stage_q, 1), stage_kv, stage_qr, stage_kr,
            part(stage_vr, 0), part(stage_vr, 1), part(stage_gate, 0), part(stage_gate, 1)]


def _pm_last_block(slot, tm, sc):
    rows = slice(tm - BLK, tm)
    return ([sc["kd0"][slot, rows, :], sc["kd1"][slot, rows, :]],
            [sc["vd0"][slot, rows, :], sc["vd1"][slot, rows, :]])


def _pm_blocks(slot, prev_kd, prev_vd, is_first, state, fillers, tm, sinks_ref, gng_ref, gnb_ref,
               decay_ref, xi_ref, zeta_ref, sc):
    nblk = tm // BLK
    n_units = nblk * (N_KV_HEADS + N_RET_HEADS // 2)
    pending = list(fillers)
    done_units = [0]

    def unit_done():
        done_units[0] += 1
        while pending and (len(fillers) - len(pending)) * n_units < done_units[0] * len(fillers):
            pending.pop(0)()
    lowb = lax.broadcasted_iota(jnp.int32, (BLK, LANES), 1) < HALF
    col = lax.broadcasted_iota(jnp.int32, (BLK, 2 * BLK), 1)
    first_mask = None if is_first is False else jnp.where((col < BLK) & is_first, NEG, 0.0)
    kd_refs = (sc["kd0"], sc["kd1"])
    vd_refs = (sc["vd0"], sc["vd1"])
    qlo, qhi, mix = sc["qlo"], sc["qhi"], sc["mix"]
    n_pairs = N_RET_HEADS // 2

    for j in range(nblk):
        rows = slice(j * BLK, (j + 1) * BLK)
        for kvh in range(N_KV_HEADS):
            if j == 0:
                kd = jnp.concatenate([prev_kd[kvh], kd_refs[kvh][slot, rows, :]], axis=0)
                vd = jnp.concatenate([prev_vd[kvh], vd_refs[kvh][slot, rows, :]], axis=0)
            else:
                krows = slice((j - 1) * BLK, (j + 1) * BLK)
                kd = kd_refs[kvh][slot, krows, :]
                vd = vd_refs[kvh][slot, krows, :]
            c0 = kvh * KV_GROUP * HEAD_DIM
            qst = jnp.concatenate([qlo[slot, rows, c0:c0 + LANES], qhi[slot, rows, c0:c0 + LANES],
                                   qlo[slot, rows, c0 + LANES:c0 + 2 * LANES],
                                   qhi[slot, rows, c0 + LANES:c0 + 2 * LANES]], axis=0)
            s = _dot_nt(qst, kd)
            es, esink = [], []
            for g in range(KV_GROUP):
                h = kvh * KV_GROUP + g
                sg = s[g * BLK:(g + 1) * BLK] + sc["bias"][h]
                if j == 0 and first_mask is not None:
                    sg = sg + first_mask
                sink = sinks_ref[h]
                m = jnp.maximum(jnp.max(sg, axis=-1, keepdims=True), sink)
                es.append(jnp.exp(sg - m).astype(BF16))
                esink.append(jnp.exp(sink - m))
            o = _dot(jnp.concatenate(es, axis=0), vd)
            for pair in range(KV_GROUP // 2):
                oe = o[2 * pair * BLK:(2 * pair + 1) * BLK]
                oo = o[(2 * pair + 1) * BLK:(2 * pair + 2) * BLK]
                num = jnp.where(lowb, oe[:, :LANES], oo[:, LANES:])
                den = (jnp.where(lowb, oe[:, LANES:], oo[:, :LANES])
                       + jnp.where(lowb, esink[2 * pair], esink[2 * pair + 1]))
                cs = c0 + pair * LANES
                mix[slot, rows, cs:cs + LANES] = (num * (1.0 / den)).astype(BF16)
            unit_done()

        for i in range(n_pairs):
            lsl = slice(i * LANES, (i + 1) * LANES)
            kp = sc["kr"][slot, rows, lsl]
            sp = state[i]
            vpair = sc["vr"][slot, rows, 2 * i * RET_V_DIM:(2 * i + 2) * RET_V_DIM]
            q2 = jnp.concatenate([sc["qrlo"][slot, rows, lsl], sc["qrhi"][slot, rows, lsl]], axis=0)
            a = _dot_nt(q2, kp.astype(BF16))
            inner = jnp.concatenate([a[:BLK] * decay_ref[2 * i], a[BLK:] * decay_ref[2 * i + 1]], axis=0)
            oi = _dot(inner.astype(BF16), vpair)
            oc = _dot(q2, sp.astype(BF16))
            for half in range(2):
                h = 2 * i + half
                vsl = slice(h * RET_V_DIM, (h + 1) * RET_V_DIM)
                hr = slice(half * BLK, (half + 1) * BLK)
                o = oi[hr, half * RET_V_DIM:(half + 1) * RET_V_DIM] + oc[hr] * xi_ref[:, vsl]
                r = _group_norm(o, gng_ref[:, vsl], gnb_ref[:, vsl]) * sc["gate"][slot, rows, vsl]
                mix[slot, rows, ATT_Q_W + h * RET_V_DIM:ATT_Q_W + (h + 1) * RET_V_DIM] = r.astype(BF16)
            kz = (kp * zeta_ref[:, lsl]).astype(BF16)
            u = _dot_tn(kz, vpair)
            state[i] = jnp.concatenate(
                [_GL_PROMPT[2 * i] * sp[:RET_QK_DIM] + u[:RET_QK_DIM, :RET_V_DIM],
                 _GL_PROMPT[2 * i + 1] * sp[RET_QK_DIM:] + u[RET_QK_DIM:, RET_V_DIM:]], axis=0)
            unit_done()

    assert not pending
    return state


def _pm_wout_pieces(slot, x_ref, rows, wout_ref, h_ref, sc):
    pw = 2 * LANES
    n = D_MODEL // pw
    parts = []

    def piece(k):
        parts.append(_dot(sc["mix"][slot], wout_ref[:, k * pw:(k + 1) * pw]))
        if k == n - 1:
            h_ref[rows, :] = x_ref[rows, :] + jnp.concatenate(parts, axis=1)

    return [functools.partial(piece, k) for k in range(n)]


def _prompt_mixer_kernel_p(sinks_ref, xpair_ref, xnext_ref, gmix_ref, win_ref, wout_ref, gng_ref, gnb_ref,
                           dist_ref, mask_ref, decay_ref, xi_ref, zeta_ref,
                           wupf_ref, wdnf_ref, wqf_ref, wof_ref,
                           h_ref, wk_ref, wv_ref, st_ref,
                           wupb_ref, wdnb_ref, wqb_ref, wob_ref,
                           qlo_s, qhi_s, kd0_s, kd1_s, vd0_s, vd1_s,
                           qrlo_s, qrhi_s, kr_s, vr_s, gate_s, mix_s, bias_s, state_s):
    u = pl.program_id(0)
    tm = xnext_ref.shape[0]
    wupb_ref[...] = wupf_ref[...].astype(BF16)
    wdnb_ref[...] = wdnf_ref[...].astype(BF16)
    wqb_ref[...] = wqf_ref[...].astype(BF16)
    wob_ref[...] = wof_ref[...].astype(BF16)
    sc = dict(qlo=qlo_s, qhi=qhi_s, kd0=kd0_s, kd1=kd1_s, vd0=vd0_s, vd1=vd1_s, qrlo=qrlo_s, qrhi=qrhi_s,
              kr=kr_s, vr=vr_s, gate=gate_s, mix=mix_s, bias=bias_s)
    n_pairs = N_RET_HEADS // 2
    blocks = functools.partial(_pm_blocks, tm=tm, sinks_ref=sinks_ref, gng_ref=gng_ref,
                               gnb_ref=gnb_ref, decay_ref=decay_ref, xi_ref=xi_ref, zeta_ref=zeta_ref, sc=sc)

    @pl.when(u == 0)
    def _():
        for h in range(N_ATT_HEADS):
            bias_s[h] = NEG_SLOPES[h] * dist_ref[...] + mask_ref[...]
        state_s[...] = jnp.zeros_like(state_s)
        kd0_s[1] = jnp.zeros(kd0_s.shape[1:], BF16)
        kd1_s[1] = jnp.zeros(kd1_s.shape[1:], BF16)
        vd0_s[1] = jnp.zeros(vd0_s.shape[1:], BF16)
        vd1_s[1] = jnp.zeros(vd1_s.shape[1:], BF16)
        for stage in _pm_project_stages(xpair_ref[0:tm, :], 0, gmix_ref, win_ref, sc):
            stage()

    seq_start = (u % 2) == 0
    state = [jnp.where(seq_start, 0.0, state_s[i * LANES:(i + 1) * LANES, :]) for i in range(n_pairs)]

    prev_kd, prev_vd = _pm_last_block(1, tm, sc)
    stages = _pm_project_stages(xpair_ref[tm:2 * tm, :], 1, gmix_ref, win_ref, sc, kv_out=(wk_ref, wv_ref))
    state = blocks(0, prev_kd, prev_vd, seq_start, state, stages)
    wout0 = _pm_wout_pieces(0, xpair_ref, slice(0, tm), wout_ref, h_ref, sc)

    prev_kd, prev_vd = _pm_last_block(0, tm, sc)
    stages = _pm_project_stages(xnext_ref[...], 0, gmix_ref, win_ref, sc)
    state = blocks(1, prev_kd, prev_vd, False, state, wout0 + stages)
    for piece in _pm_wout_pieces(1, xpair_ref, slice(tm, 2 * tm), wout_ref, h_ref, sc):
        piece()

    for i in range(n_pairs):
        state_s[i * LANES:(i + 1) * LANES, :] = state[i]
        st_ref[0, i * LANES:(i + 1) * LANES, :] = state[i]


def _prompt_mixer_p(x, g_mix, w_in, w_out, sinks, gn_g, gn_b, side_f32):
    b, s, d = x.shape
    tm = TM_MIX
    n_tiles = b * s // tm
    steps = n_tiles // 2
    seq_steps = s // (2 * tm)
    assert s % (2 * tm) == 0 and seq_steps == 2, "kernel assumes 4 tiles per sequence"
    x2d = x.reshape(b * s, d)
    const = lambda shape: pl.BlockSpec(shape, lambda i: (0,) * len(shape), pipeline_mode=pl.Buffered(1))
    slot2 = lambda rows, cols, dt: pltpu.VMEM((2, rows, cols), dt)
    side_specs = [pl.BlockSpec((w.shape[0] // steps, w.shape[1]), lambda i: (i, 0)) for w in side_f32]
    outs = pl.pallas_call(
        _prompt_mixer_kernel_p,
        grid=(steps,),
        in_specs=[
            pl.BlockSpec(memory_space=pltpu.SMEM),
            pl.BlockSpec((2 * tm, d), lambda i: (i, 0)),
            pl.BlockSpec((tm, d), lambda i: (jnp.minimum(2 * i + 2, n_tiles - 1), 0)),
            const((1, d)), const((d, D_IN)), const((MIX_OUT, d)),
            const((1, RET_V_W)), const((1, RET_V_W)),
            const((BLK, 2 * BLK)), const((BLK, 2 * BLK)),
            const((N_RET_HEADS, BLK, BLK)), const((BLK, RET_V_W)), const((BLK, RET_QK_W)),
        ] + side_specs,
        out_specs=[
            pl.BlockSpec((2 * tm, d), lambda i: (i, 0)),
            pl.BlockSpec((1, WINDOW, ATT_KV_W), lambda i: (i // seq_steps, 0, 0)),
            pl.BlockSpec((1, WINDOW, ATT_KV_W), lambda i: (i // seq_steps, 0, 0)),
            pl.BlockSpec((1, RET_QK_W, RET_V_DIM), lambda i: (i // seq_steps, 0, 0)),
        ] + side_specs,
        out_shape=[
            jax.ShapeDtypeStruct((b * s, d), F32),
            jax.ShapeDtypeStruct((b, WINDOW, ATT_KV_W), F32),
            jax.ShapeDtypeStruct((b, WINDOW, ATT_KV_W), F32),
            jax.ShapeDtypeStruct((b, RET_QK_W, RET_V_DIM), F32),
        ] + [jax.ShapeDtypeStruct(w.shape, BF16) for w in side_f32],
        scratch_shapes=[
            slot2(tm, ATT_Q_W, BF16), slot2(tm, ATT_Q_W, BF16),
            slot2(tm, LANES, BF16), slot2(tm, LANES, BF16),
            slot2(tm, 2 * LANES, BF16), slot2(tm, 2 * LANES, BF16),
            slot2(tm, RET_QK_W, BF16), slot2(tm, RET_QK_W, BF16),
            slot2(tm, RET_QK_W, F32), slot2(tm, RET_V_W, BF16),
            slot2(tm, RET_V_W, F32), slot2(tm, MIX_OUT, BF16),
            pltpu.VMEM((N_ATT_HEADS, BLK, 2 * BLK), F32),
            pltpu.VMEM((RET_QK_W, RET_V_DIM), F32),
        ],
        compiler_params=pltpu.CompilerParams(
            dimension_semantics=("arbitrary",), vmem_limit_bytes=VMEM_LIMIT),
        name="prompt_mixer",
    )(sinks, x2d, x2d, g_mix, w_in, w_out, gn_g, gn_b,
      jnp.asarray(_P_DIST), jnp.asarray(_P_MASK), jnp.asarray(_P_DECAY), jnp.asarray(_P_XI),
      jnp.asarray(_P_ZETA), *side_f32)
    return (outs[0].reshape(b, s, d),) + tuple(outs[1:])


def _memkv_kernel(mem_ref, g_ref, wk_ref, wv_ref, win_ref, wout_ref,
                  mk_ref, mv_ref, mkb_ref, mvb_ref, winb_ref, woutb_ref):
    winb_ref[...] = win_ref[...].astype(BF16)
    woutb_ref[...] = wout_ref[...].astype(BF16)
    mn = _rms(mem_ref[...], g_ref[...]).astype(BF16)
    mk = _dot(mn, wk_ref[...].astype(BF16))
    mv = _dot(mn, wv_ref[...].astype(BF16))
    tm = mem_ref.shape[0]
    group = X_D_HALVES * N_X_HEADS
    for hd in range(N_X_HEADS):
        for dh in range(X_D_HALVES):
            cols = slice(hd * X_HEAD_DIM + dh * LANES, hd * X_HEAD_DIM + (dh + 1) * LANES)
            rows = pl.ds(dh * N_X_HEADS + hd, tm, stride=group)
            mk_ref[rows, :] = mk[:, cols]
            mv_ref[rows, :] = mv[:, cols]
    mkb_ref[...] = mk.astype(BF16)
    mvb_ref[...] = mv.astype(BF16)


def _memory_kv(mem2d, g_mem, w_xk, w_xv, w_in, w_out):
    n, d = mem2d.shape
    tm = 512
    row = pl.BlockSpec((tm, d), lambda i: (i, 0))
    rows_out = pl.BlockSpec((tm * d // LANES, LANES), lambda i: (i, 0))
    const = lambda shape: pl.BlockSpec(shape, lambda i: (0,) * len(shape), pipeline_mode=pl.Buffered(1))
    steps = n // tm
    win_blk = pl.BlockSpec((w_in.shape[0] // steps, w_in.shape[1]), lambda i: (i, 0))
    wout_blk = pl.BlockSpec((w_out.shape[0] // steps, w_out.shape[1]), lambda i: (i, 0))
    return pl.pallas_call(
        _memkv_kernel,
        grid=(n // tm,),
        in_specs=[row, const((1, d)), const((d, d)), const((d, d)), win_blk, wout_blk],
        out_specs=[rows_out, rows_out, row, row, win_blk, wout_blk],
        out_shape=[jax.ShapeDtypeStruct((n * d // LANES, LANES), F32),
                   jax.ShapeDtypeStruct((n * d // LANES, LANES), F32),
                   jax.ShapeDtypeStruct((n, d), BF16), jax.ShapeDtypeStruct((n, d), BF16),
                   jax.ShapeDtypeStruct(w_in.shape, BF16), jax.ShapeDtypeStruct(w_out.shape, BF16)],
        compiler_params=pltpu.CompilerParams(
            dimension_semantics=("arbitrary",), vmem_limit_bytes=VMEM_LIMIT),
        name="memory_kv",
    )(mem2d, g_mem, w_xk, w_xv, w_in, w_out)


def _prompt_xattn_kernel(h_ref, g_ref, wq_ref, wo_ref, mk_ref, mv_ref, out_ref, o_s):
    def stages(r0):
        rows = slice(r0, r0 + SUB_ROWS)
        env = {}

        def project():
            env["h"] = h_ref[0, rows, :]
            xn = _rms(env["h"], g_ref[...]).astype(BF16)
            env["q"] = (_dot(xn, wq_ref[...]) * (X_HEAD_DIM ** -0.5)).astype(BF16)

        def head(hd):
            sl = slice(hd * X_HEAD_DIM, (hd + 1) * X_HEAD_DIM)
            s = _dot_nt(env["q"][:, sl], mk_ref[0, :, sl])
            m = jnp.max(s, axis=-1, keepdims=True)
            p = jnp.exp(s - m)
            p = p * (1.0 / jnp.sum(p, axis=-1, keepdims=True))
            o_s[rows, sl] = _dot(p.astype(BF16), mv_ref[0, :, sl]).astype(BF16)

        def output():
            out_ref[0, rows, :] = env["h"] + _dot(o_s[rows, :], wo_ref[...])

        return [project] + [functools.partial(head, hd) for hd in range(N_X_HEADS)] + [output]

    chains = [stages(r0) for r0 in range(0, h_ref.shape[1], SUB_ROWS)]
    n_stage = len(chains[0])
    for step in range(n_stage + len(chains) - 1):
        for lag, chain in enumerate(chains):
            if 0 <= step - lag < n_stage:
                chain[step - lag]()


def _prompt_xattn(h, g, w_xq, w_xo, mkb, mvb):
    b, s, d = h.shape
    tm = TM_X
    const = lambda shape: pl.BlockSpec(shape, lambda i, j: (0,) * len(shape))
    tok = pl.BlockSpec((1, tm, d), lambda i, j: (i, j, 0))
    mem = pl.BlockSpec((1, N_MEM, d), lambda i, j: (i, 0, 0))
    return pl.pallas_call(
        _prompt_xattn_kernel,
        grid=(b, s // tm),
        in_specs=[tok, const((1, d)), const((d, d)), const((d, d)), mem, mem],
        out_specs=tok,
        out_shape=jax.ShapeDtypeStruct((b, s, d), F32),
        scratch_shapes=[pltpu.VMEM((tm, d), BF16)],
        compiler_params=pltpu.CompilerParams(
            dimension_semantics=("arbitrary", "arbitrary"), vmem_limit_bytes=VMEM_LIMIT),
        name="prompt_xattn",
    )(h, g, w_xq, w_xo, mkb, mvb)


def _mlp_rows(h_ref, y_ref, g_ref, wup_ref, wdn_ref, gf_ref):
    for r0 in range(0, h_ref.shape[0], SUB_ROWS):
        rows = slice(r0, r0 + SUB_ROWS)
        h = h_ref[rows, :]
        xn = _rms(h, g_ref[...]).astype(BF16)
        acc = h
        for c in range(D_FF // FF_CHUNK):
            sl = slice(c * FF_CHUNK, (c + 1) * FF_CHUNK)
            u = jnp.maximum(_dot(xn, wup_ref[:, sl]), 0.0)
            acc = acc + _dot((u * u).astype(BF16), wdn_ref[sl, :])
        y_ref[rows, :] = _rms(acc, gf_ref[...])


def _mlp_kernel(hp_ref, hs_ref, g_ref, wup_ref, wdn_ref, gf_ref, yp_ref, ys_ref):
    last = pl.num_programs(0) - 1

    @pl.when(pl.program_id(0) < last)
    def _():
        _mlp_rows(hp_ref, yp_ref, g_ref, wup_ref, wdn_ref, gf_ref)

    @pl.when(pl.program_id(0) == last)
    def _():
        _mlp_rows(hs_ref, ys_ref, g_ref, wup_ref, wdn_ref, gf_ref)


def _mlp_final(hp2d, hs2d, g_mlp, w_up, w_down, g_final):
    n, d = hp2d.shape
    ns = hs2d.shape[0]
    tm = TM_MLP
    n_tiles = n // tm
    prompt = pl.BlockSpec((tm, d), lambda i: (jnp.minimum(i, n_tiles - 1), 0))
    sample = pl.BlockSpec((ns, d), lambda i: (0, 0))
    const = lambda shape: pl.BlockSpec(shape, lambda i: (0,) * len(shape), pipeline_mode=pl.Buffered(1))
    return pl.pallas_call(
        _mlp_kernel,
        grid=(n_tiles + 1,),
        in_specs=[prompt, sample, const((1, d)), const((d, D_FF)), const((D_FF, d)), const((1, d))],
        out_specs=[prompt, sample],
        out_shape=[jax.ShapeDtypeStruct((n, d), F32), jax.ShapeDtypeStruct((ns, d), F32)],
        compiler_params=pltpu.CompilerParams(
            dimension_semantics=("arbitrary",), vmem_limit_bytes=VMEM_LIMIT),
        name="mlp_final",
    )(hp2d, hs2d, g_mlp, w_up, w_down, g_final)


def _sample_mixer_kernel(sinks_ref, x_ref, gmix_ref, win_ref, wout_ref, gng_ref, gnb_ref,
                         ck_ref, cv_ref, st_ref, bias_ref, dec_ref, xi_ref, zeta_ref,
                         h_ref, swk_ref, swv_ref, sst_ref):
    bb = ck_ref.shape[0]
    nt = bb // 2
    x = x_ref[...]
    xn = _rms(x, gmix_ref[...]).astype(BF16)
    tile3 = lambda a: a.reshape(nt, SUBLANES, a.shape[-1])

    q = _dot(xn, win_ref[:, C_QA:C_QA + ATT_Q_W]) * (HEAD_DIM ** -0.5)
    kv = _dot(xn, win_ref[:, C_KV:C_KV + 2 * ATT_KV_W])
    qkr = _dot(xn, win_ref[:, C_QKR:C_QKR + 2 * RET_QK_W])
    vr = _dot(xn, win_ref[:, C_VR:C_VR + RET_V_W])
    gate3 = tile3(_silu(_dot(xn, win_ref[:, C_GR:C_GR + RET_V_W])))

    lo512, hi512 = _half_masks(ATT_Q_W)
    q_r = pltpu.roll(q, HALF, axis=1)
    q_nat3 = tile3(q)
    q_rot3 = tile3(q_r)
    lo3 = lo512.reshape(1, 1, ATT_Q_W)
    hi3 = hi512.reshape(1, 1, ATT_Q_W)
    qa3 = (q_nat3 * lo3).astype(BF16)
    qb3 = (q_rot3 * lo3).astype(BF16)
    qc3 = (q_rot3 * hi3).astype(BF16)
    qd3 = (q_nat3 * hi3).astype(BF16)
    t128 = lambda a, i: a[:, :, i * LANES:(i + 1) * LANES]
    qs = jnp.concatenate([t128(qa3, 0), t128(qb3, 1), t128(qa3, 1), t128(qb3, 2),
                          t128(qc3, 2), t128(qd3, 2), t128(qc3, 3), t128(qd3, 3)], axis=1)

    k3 = tile3(kv[:, :ATT_KV_W])
    v3 = tile3(kv[:, ATT_KV_W:])
    pad_kv = jnp.zeros((nt, BLK - SUBLANES, LANES), BF16)
    knew_pad = jnp.concatenate([k3.astype(BF16), pad_kv], axis=1)
    vnew_pad = jnp.concatenate([v3.astype(BF16), pad_kv], axis=1)
    to_lanes = lambda a3: jnp.swapaxes(
        jnp.concatenate([a3, jnp.zeros((nt, BLK - SUBLANES, LANES), F32)], axis=1), 1, 2)
    k3t, v3t = to_lanes(k3), to_lanes(v3)
    roll3 = lambda a, sh: pltpu.roll(a.reshape(nt * BLK, LANES), sh, axis=1).reshape(nt, BLK, LANES)

    lo256, _ = _half_masks(RET_QK_W)
    qr3 = tile3(qkr[:, :RET_QK_W])
    kr3 = tile3(qkr[:, RET_QK_W:] * (RET_QK_DIM ** -0.5))
    vr3 = tile3(vr)
    lane256 = lax.broadcasted_iota(jnp.int32, (1, 1, RET_QK_W), 2)
    qrs = jnp.concatenate(
        [(qr3 * ((lane256 >= h * RET_QK_DIM) & (lane256 < (h + 1) * RET_QK_DIM)).astype(F32)).astype(BF16)
         for h in range(N_RET_HEADS)],
        axis=1)
    kr_pad = jnp.concatenate([kr3.astype(BF16), jnp.zeros((nt, BLK - SUBLANES, RET_QK_W), BF16)], axis=1)
    vr_pad = jnp.concatenate([vr3.astype(BF16), jnp.zeros((nt, BLK - SUBLANES, RET_V_W), BF16)], axis=1)

    lane = lax.broadcasted_iota(jnp.int32, (1, 1, LANES), 2)
    row8 = lax.broadcasted_iota(jnp.int32, (1, SUBLANES, 1), 1)
    bmm_nt = lambda a, b: jnp.einsum('bqd,bkd->bqk', a, b, preferred_element_type=F32)
    bmm = lambda a, b: jnp.einsum('bqk,bkd->bqd', a, b, preferred_element_type=F32)

    att_par, ret_par = [], []
    for par in range(2):
        bsl = pl.ds(par, nt, stride=2)
        ckt = ck_ref[bsl]
        cvt = cv_ref[bsl]
        keep = lane < WINDOW - DEC_SEQ
        new_shift = WINDOW - DEC_SEQ - DEC_SEQ * par
        swk_ref[bsl] = jnp.where(keep, roll3(ckt, WINDOW - DEC_SEQ), roll3(k3t, new_shift))
        swv_ref[bsl] = jnp.where(keep, roll3(cvt, WINDOW - DEC_SEQ), roll3(v3t, new_shift))

        s = jnp.concatenate([bmm(qs, ckt.astype(BF16)), bmm_nt(qs, knew_pad)], axis=2) + bias_ref[par]
        ps = []
        for h in range(N_ATT_HEADS):
            ps.append(_sink_softmax(s[:, h * SUBLANES:(h + 1) * SUBLANES, :], sinks_ref[h]).astype(BF16))
        p_all = jnp.concatenate(ps, axis=1)
        o = bmm_nt(p_all[:, :, :BLK], cvt.astype(BF16)) + bmm(p_all[:, :, BLK:], vnew_pad)
        o_r = pltpu.roll(o.reshape(nt * N_ATT_HEADS * SUBLANES, LANES), HALF, axis=1).reshape(o.shape)
        hr = lambda a, h: a[:, h * SUBLANES:(h + 1) * SUBLANES, :]
        low = lane < HALF
        att_par.append(jnp.concatenate([
            jnp.where(low, hr(o, 0), hr(o_r, 1)), jnp.where(low, hr(o, 2), hr(o_r, 3)),
            jnp.where(low, hr(o_r, 4), hr(o, 5)), jnp.where(low, hr(o_r, 6), hr(o, 7))], axis=2))

        st = st_ref[bsl]
        oc = bmm(qrs, st.astype(BF16))
        inner = (bmm_nt(qrs, kr_pad) * dec_ref[par]).astype(BF16)
        oi = bmm(inner, vr_pad)
        rs = []
        for h in range(N_RET_HEADS):
            vsl = slice(h * RET_V_DIM, (h + 1) * RET_V_DIM)
            rsl = slice(h * SUBLANES, (h + 1) * SUBLANES)
            o_h = oi[:, rsl, vsl] + oc[:, rsl, :] * xi_ref[par, rsl, :]
            rs.append(_group_norm(o_h, gng_ref[:, vsl], gnb_ref[:, vsl]) * gate3[:, :, vsl])
        ret_par.append(jnp.concatenate(rs, axis=2))

        kz3 = (kr3 * zeta_ref[par]).astype(BF16)
        vr3_b = vr3.astype(BF16)
        for p in range(nt):
            for i in range(N_RET_HEADS // 2):
                u = _dot_tn(kz3[p][:, i * LANES:(i + 1) * LANES],
                            vr3_b[p][:, 2 * i * RET_V_DIM:(2 * i + 2) * RET_V_DIM])
                for half in range(2):
                    h = 2 * i + half
                    dsl = slice(h * RET_QK_DIM, (h + 1) * RET_QK_DIM)
                    sst_ref[2 * p + par, dsl, :] = (
                        _GL_SAMPLE[h] * st[p, dsl, :]
                        + u[half * RET_QK_DIM:(half + 1) * RET_QK_DIM, half * RET_V_DIM:(half + 1) * RET_V_DIM])

    own0 = row8 < DEC_SEQ
    att3 = jnp.where(own0, att_par[0], att_par[1])
    ret3 = jnp.where(own0, ret_par[0], ret_par[1])
    mix = jnp.concatenate([att3, ret3], axis=2).reshape(2 * nt * DEC_SEQ, MIX_OUT).astype(BF16)
    h_ref[...] = x + _dot(mix, wout_ref[...])


def _sample_mixer(x2d, g_mix, w_in, w_out, sinks, gn_g, gn_b, ck, cv, st):
    n, d = x2d.shape
    nb = ck.shape[0]
    bb = BB_MIX
    r = bb * DEC_SEQ
    const = lambda shape: pl.BlockSpec(shape, lambda i: (0,) * len(shape))
    row = pl.BlockSpec((r, d), lambda i: (i, 0))
    win = pl.BlockSpec((bb, WINDOW, ATT_KV_W), lambda i: (i, 0, 0))
    state = pl.BlockSpec((bb, RET_QK_W, RET_V_DIM), lambda i: (i, 0, 0))
    return pl.pallas_call(
        _sample_mixer_kernel,
        grid=(nb // bb,),
        in_specs=[
            pl.BlockSpec(memory_space=pltpu.SMEM),
            row, const((1, d)), const((d, D_IN)), const((MIX_OUT, d)),
            const((1, RET_V_W)), const((1, RET_V_W)),
            win, win, state,
            const(_S_BIAS.shape), const(_S_DEC.shape), const(_S_XI.shape), const(_S_ZETA.shape),
        ],
        out_specs=[row, win, win, state],
        out_shape=[
            jax.ShapeDtypeStruct((n, d), F32),
            jax.ShapeDtypeStruct((nb, WINDOW, ATT_KV_W), F32),
            jax.ShapeDtypeStruct((nb, WINDOW, ATT_KV_W), F32),
            jax.ShapeDtypeStruct((nb, RET_QK_W, RET_V_DIM), F32),
        ],
        compiler_params=pltpu.CompilerParams(
            dimension_semantics=("arbitrary",), vmem_limit_bytes=VMEM_LIMIT),
        name="sample_mixer",
    )(sinks, x2d, g_mix, w_in, w_out, gn_g, gn_b, ck, cv, st,
      jnp.asarray(_S_BIAS), jnp.asarray(_S_DEC), jnp.asarray(_S_XI), jnp.asarray(_S_ZETA))


def _head_slab(x_ref, b, hd):
    group = X_D_HALVES * N_X_HEADS
    halves = [x_ref[b, pl.ds(dh * N_X_HEADS + hd, N_MEM, stride=group), :] for dh in range(X_D_HALVES)]
    return jnp.concatenate(halves, axis=1).astype(BF16)


def _sample_xattn_kernel(h_ref, g_ref, wq_ref, wo_ref, xk_ref, xv_ref, out_ref):
    bb = xk_ref.shape[0]
    nt = bb // 2
    h = h_ref[...]
    xn = _rms(h, g_ref[...]).astype(BF16)
    q = _dot(xn, wq_ref[...]) * (X_HEAD_DIM ** -0.5)
    units = [(t, par, hd) for t in range(nt) for par in range(2) for hd in range(N_X_HEADS)]
    qts = [q[t * SUBLANES:(t + 1) * SUBLANES].astype(BF16) for t in range(nt)]
    s = jnp.concatenate(
        [_dot_nt(qts[t][:, hd * X_HEAD_DIM:(hd + 1) * X_HEAD_DIM], _head_slab(xk_ref, 2 * t + par, hd))
         for t, par, hd in units], axis=0)
    m = jnp.max(s, axis=-1, keepdims=True)
    p = jnp.exp(s - m)
    p = p * (1.0 / jnp.sum(p, axis=-1, keepdims=True))
    os_ = {}
    for i, (t, par, hd) in enumerate(units):
        pi = p[i * SUBLANES:(i + 1) * SUBLANES].astype(BF16)
        os_[(t, par, hd)] = _dot(pi, _head_slab(xv_ref, 2 * t + par, hd))
    own0 = lax.broadcasted_iota(jnp.int32, (SUBLANES, 1), 0) < DEC_SEQ
    o_tiles = []
    for t in range(nt):
        o_par = [jnp.concatenate([os_[(t, par, hd)] for hd in range(N_X_HEADS)], axis=1) for par in range(2)]
        o_tiles.append(jnp.where(own0, o_par[0], o_par[1]))
    o = jnp.concatenate(o_tiles, axis=0).astype(BF16)
    out_ref[...] = h + _dot(o, wo_ref[...])


def _mlp_value(h, g_ref, wup_ref, wdn_ref, gf_ref, fillers=None):
    xn = _rms(h, g_ref[...]).astype(BF16)
    acc = h
    piece = FF_CHUNK // N_X_HEADS
    opiece = D_MODEL // N_X_HEADS
    for c in range(D_FF // FF_CHUNK):
        qk, softmax, pv = fillers[c] if fillers is not None else (None, None, None)
        hid = []
        for k in range(N_X_HEADS):
            cols = slice(c * FF_CHUNK + k * piece, c * FF_CHUNK + (k + 1) * piece)
            u = jnp.maximum(_dot(xn, wup_ref[:, cols]), 0.0)
            hid.append((u * u).astype(BF16))
            if qk is not None:
                qk(k)
        if softmax is not None:
            softmax()
        hid = jnp.concatenate(hid, axis=1)
        rows_c = slice(c * FF_CHUNK, (c + 1) * FF_CHUNK)
        out = []
        for k in range(N_X_HEADS):
            out.append(_dot(hid, wdn_ref[rows_c, k * opiece:(k + 1) * opiece]))
            if pv is not None:
                pv(k)
        acc = acc + jnp.concatenate(out, axis=1)
    return _rms(acc, gf_ref[...])


def _mlp_xattn_kernel(hp_ref, hsm_ref, gx_ref, wq_ref, wo_ref, xk_ref, xv_ref, g_ref, wup_ref, wdn_ref, gf_ref,
                      yp_ref, ys_ref):
    i = pl.program_id(0)
    n = pl.num_programs(0) - 1
    bb = xk_ref.shape[0]
    rows = bb * DEC_SEQ
    assert bb == D_FF // FF_CHUNK and bb % 2 == 0

    @pl.when(i == 0)
    def _():
        xn = _rms(hsm_ref[...], gx_ref[...]).astype(BF16)
        ys_ref[...] = _dot(xn, wq_ref[...]) * (X_HEAD_DIM ** -0.5)

    @pl.when(i < n)
    def _():
        r0 = pl.multiple_of(i * rows, rows)
        own0 = lax.broadcasted_iota(jnp.int32, (SUBLANES, 1), 0) < DEC_SEQ
        o_rows = {}

        def attend(b):
            t = b // 2
            tile_rows = pl.ds(r0 + t * SUBLANES, SUBLANES)
            env = dict(s=[], o=[])

            def qk(hd):
                if hd == 0:
                    env["q"] = ys_ref[tile_rows, :].astype(BF16)
                env["s"].append(_dot_nt(env["q"][:, hd * X_HEAD_DIM:(hd + 1) * X_HEAD_DIM],
                                        _head_slab(xk_ref, b, hd)))

            def softmax():
                s = jnp.concatenate(env["s"], axis=0)
                m = jnp.max(s, axis=-1, keepdims=True)
                p = jnp.exp(s - m)
                env["p"] = p * (1.0 / jnp.sum(p, axis=-1, keepdims=True))

            def pv(hd):
                p = env["p"][hd * SUBLANES:(hd + 1) * SUBLANES].astype(BF16)
                env["o"].append(_dot(p, _head_slab(xv_ref, b, hd)))
                if hd == N_X_HEADS - 1:
                    o_rows[b] = jnp.concatenate(env["o"], axis=1)
                    if b % 2 == 1:
                        ys_ref[tile_rows, :] = jnp.where(own0, o_rows[b - 1], o_rows[b])

            return qk, softmax, pv

        fillers = [attend(b) for b in range(bb)]
        yp_ref[...] = _mlp_value(hp_ref[...], g_ref, wup_ref, wdn_ref, gf_ref, fillers)

    @pl.when(i == n)
    def _():
        hs = hsm_ref[...] + _dot(ys_ref[...].astype(BF16), wo_ref[...])
        ys_ref[...] = _mlp_value(hs, g_ref, wup_ref, wdn_ref, gf_ref)


def _mlp_xattn(hp2d, hsm, g_xattn, w_xq, w_xo, xk, xv, g_mlp, w_up, w_down, g_final):
    n, d = hp2d.shape
    ns = hsm.shape[0]
    nb = xk.shape[0]
    bb = BB_X
    tm = n // (nb // bb)
    n_tiles = n // tm
    assert n_tiles * bb == nb and tm % SUBLANES == 0
    clip = lambda i: jnp.minimum(i, n_tiles - 1)
    prompt = pl.BlockSpec((tm, d), lambda i: (clip(i), 0))
    mem = pl.BlockSpec((bb,) + xk.shape[1:], lambda i: (clip(i), 0, 0))
    const = lambda shape: pl.BlockSpec(shape, lambda i: (0,) * len(shape), pipeline_mode=pl.Buffered(1))
    return pl.pallas_call(
        _mlp_xattn_kernel,
        grid=(n_tiles + 1,),
        in_specs=[prompt, const((ns, d)), const((1, d)), const((d, d)), const((d, d)), mem, mem,
                  const((1, d)), const((d, D_FF)), const((D_FF, d)), const((1, d))],
        out_specs=[prompt, pl.BlockSpec((ns, d), lambda i: (0, 0))],
        out_shape=[jax.ShapeDtypeStruct((n, d), F32), jax.ShapeDtypeStruct((ns, d), F32)],
        compiler_params=pltpu.CompilerParams(
            dimension_semantics=("arbitrary",), vmem_limit_bytes=VMEM_LIMIT),
        name="mlp_xattn",
    )(hp2d, hsm, g_xattn, w_xq, w_xo, xk, xv, g_mlp, w_up, w_down, g_final)


def _mem_rows(c):
    nb = c.shape[0]
    c = c.reshape(nb, N_MEM, N_X_HEADS, X_D_HALVES, LANES)
    return jnp.transpose(c, (0, 1, 3, 2, 4)).reshape(nb, N_MEM * X_D_HALVES * N_X_HEADS, LANES)


def _sample_xattn(h2d, g, w_xq, w_xo, mk, mv):
    n, d = h2d.shape
    nb = mk.shape[0]
    bb = BB_X
    r = bb * DEC_SEQ
    const = lambda shape: pl.BlockSpec(shape, lambda i: (0,) * len(shape))
    row = pl.BlockSpec((r, d), lambda i: (i, 0))
    mem = pl.BlockSpec((bb,) + mk.shape[1:], lambda i: (i, 0, 0))
    return pl.pallas_call(
        _sample_xattn_kernel,
        grid=(nb // bb,),
        in_specs=[row, const((1, d)), const((d, d)), const((d, d)), mem, mem],
        out_specs=row,
        out_shape=jax.ShapeDtypeStruct((n, d), F32),
        compiler_params=pltpu.CompilerParams(
            dimension_semantics=("arbitrary",), vmem_limit_bytes=VMEM_LIMIT),
        name="sample_xattn",
    )(h2d, g, w_xq, w_xo, mk, mv)


def kernel(x_prompt, x_sample, mem_prompt, cache_win_k, cache_win_v, state_ret, cache_mem_k, cache_mem_v,
           g_mix, w_in, attn_sinks, ret_gn_g, ret_gn_b, w_out, g_xattn, g_mem, w_xq, w_xk, w_xv, w_xo,
           g_mlp, w_up, w_down, g_final):
    depth = w_in.shape[0]
    assert depth == 1, "single-layer trunk"
    b, s, d = x_prompt.shape
    nb, ls, _ = x_sample.shape
    row = lambda a: a.reshape(1, -1)
    sinks = attn_sinks[0]
    gn_g, gn_b = row(ret_gn_g[0]), row(ret_gn_b[0])
    g_fin = row(g_final)

    mk, mv, mkb, mvb, w_in_b, w_out_b = _memory_kv(
        mem_prompt.reshape(b * N_MEM, d), row(g_mem[0]), w_xk[0], w_xv[0], w_in[0], w_out[0])
    hp, p_wk, p_wv, p_rs, w_up_b, w_dn_b, w_xq_b, w_xo_b = _prompt_mixer_p(
        x_prompt, row(g_mix[0]), w_in_b, w_out_b, sinks, gn_g, gn_b,
        (w_up[0], w_down[0], w_xq[0], w_xo[0]))
    hp = _prompt_xattn(hp, row(g_xattn[0]), w_xq_b, w_xo_b,
                       mkb.reshape(b, N_MEM, d), mvb.reshape(b, N_MEM, d))

    win_t = lambda c: jnp.transpose(c, (0, 2, 3, 1)).reshape(nb, ATT_KV_W, WINDOW)
    win_t_inv = lambda a: jnp.transpose(a.reshape(nb, N_KV_HEADS, HEAD_DIM, WINDOW),
                                        (0, 3, 1, 2)).reshape(1, nb, WINDOW, N_KV_HEADS, HEAD_DIM)
    hs, s_wk, s_wv, s_rs = _sample_mixer(
        x_sample.reshape(nb * ls, d), row(g_mix[0]), w_in_b, w_out_b, sinks, gn_g, gn_b,
        win_t(cache_win_k[0]), win_t(cache_win_v[0]), state_ret[0].reshape(nb, RET_QK_W, RET_V_DIM))

    y_prompt, y_sample = _mlp_xattn(
        hp.reshape(b * s, d), hs, row(g_xattn[0]), w_xq_b, w_xo_b,
        _mem_rows(cache_mem_k[0]), _mem_rows(cache_mem_v[0]), row(g_mlp[0]), w_up_b, w_dn_b, g_fin)
    y_prompt = y_prompt.reshape(b, s, d)
    y_sample = y_sample.reshape(nb, ls, d)

    win5 = lambda a, n: a.reshape(1, n, WINDOW, N_KV_HEADS, HEAD_DIM)
    ret5 = lambda a, n: a.reshape(1, n, N_RET_HEADS, RET_QK_DIM, RET_V_DIM)
    mem5 = lambda a: jnp.transpose(a.reshape(b, N_MEM, X_D_HALVES, N_X_HEADS, LANES),
                                   (0, 1, 3, 2, 4)).reshape(1, b, N_MEM, N_X_HEADS, X_HEAD_DIM)
    return (y_prompt, y_sample,
            win5(p_wk, b), win5(p_wv, b), ret5(p_rs, b), mem5(mk), mem5(mv),
            win_t_inv(s_wk), win_t_inv(s_wv), ret5(s_rs, nb))
```

```python
import functools

import jax
import jax.numpy as jnp
import numpy as np
from jax import lax
from jax.experimental import pallas as pl
from jax.experimental.pallas import tpu as pltpu

F32 = jnp.float32
BF16 = jnp.bfloat16

D_MODEL = 1024
BATCH = 8
SEQ = 2048
DEC_BATCH = 128
DEC_SEQ = 4
HEAD_DIM = 64
N_ATT_HEADS = 8
N_KV_HEADS = 2
KV_GROUP = N_ATT_HEADS // N_KV_HEADS
WINDOW = 128
BLK = 128
N_RET_HEADS = 4
RET_QK_DIM = 64
RET_V_DIM = 128
N_MEM = 256
N_X_HEADS = 4
X_HEAD_DIM = D_MODEL // N_X_HEADS
D_FF = 4 * D_MODEL
RMS_EPS = 1e-6
GN_EPS = 1e-5

ATT_Q_W = N_ATT_HEADS * HEAD_DIM
ATT_KV_W = N_KV_HEADS * HEAD_DIM
RET_QK_W = N_RET_HEADS * RET_QK_DIM
RET_V_W = N_RET_HEADS * RET_V_DIM
MIX_OUT = ATT_Q_W + RET_V_W
D_IN = ATT_Q_W + 2 * ATT_KV_W + 2 * RET_QK_W + 2 * RET_V_W
C_QA, C_KV, C_QKR, C_VR, C_GR = 0, 512, 768, 1280, 1792

LANES = 128
SUBLANES = 8
HALF = LANES // 2
X_D_HALVES = X_HEAD_DIM // LANES
NEG = -1e30
VMEM_LIMIT = 56 * 1024 * 1024

TM_MIX = 512
TM_X = 2048
SUB_ROWS = 512
FF_CHUNK = 1024
BB_MIX = 32
BB_X = 4

NEG_SLOPES = [-(2.0 ** (-8.0 * (i + 1) / N_ATT_HEADS)) for i in range(N_ATT_HEADS)]
_LOG_G = np.log(1.0 - 2.0 ** (-5.0 - np.arange(N_RET_HEADS))).astype(np.float32).astype(np.float64)


def _prompt_tables():
    qi = np.arange(BLK)[:, None]
    kj = np.arange(2 * BLK)[None, :]
    dist = (qi + BLK - kj).astype(np.float64)
    mask = np.where((dist >= 0) & (dist < WINDOW), 0.0, NEG)
    l = np.arange(BLK, dtype=np.float64)
    diff = l[:, None] - l[None, :]
    decay = np.where(diff >= 0, np.exp(_LOG_G[:, None, None] * np.maximum(diff, 0.0)), 0.0)
    xi = np.exp((l[:, None] + 1.0) * _LOG_G[None, :])
    zeta = np.exp((BLK - 1.0 - l)[:, None] * _LOG_G[None, :])
    xi_t = np.repeat(xi, RET_V_DIM, axis=1)
    zeta_t = np.repeat(zeta, RET_QK_DIM, axis=1)
    f = lambda a: np.asarray(a, np.float32)
    return f(dist), f(mask), f(decay), f(xi_t), f(zeta_t)


def _sample_tables():
    slopes = -np.asarray(NEG_SLOPES)
    bias = np.full((2, N_ATT_HEADS * SUBLANES, 2 * BLK), NEG, np.float64)
    dec = np.zeros((2, N_RET_HEADS * SUBLANES, BLK), np.float64)
    xi = np.zeros((2, N_RET_HEADS * SUBLANES, RET_V_DIM), np.float64)
    zeta = np.zeros((2, SUBLANES, RET_QK_W), np.float64)
    for par in range(2):
        for r in range(SUBLANES):
            own = DEC_SEQ * par <= r < DEC_SEQ * (par + 1)
            t = r - DEC_SEQ * par if own else r % DEC_SEQ
            for h in range(N_ATT_HEADS):
                row = h * SUBLANES + r
                for j in range(WINDOW):
                    d = t + WINDOW - j
                    if 0 <= d < WINDOW:
                        bias[par, row, j] = -slopes[h] * d
                for c in range(DEC_SEQ):
                    d = t - c
                    if d >= 0:
                        bias[par, row, WINDOW + DEC_SEQ * par + c] = -slopes[h] * d
            for h in range(N_RET_HEADS):
                row = h * SUBLANES + r
                if own:
                    xi[par, row, :] = np.exp((t + 1.0) * _LOG_G[h])
                    zeta[par, r, h * RET_QK_DIM:(h + 1) * RET_QK_DIM] = np.exp((DEC_SEQ - 1.0 - t) * _LOG_G[h])
                    for c in range(t + 1):
                        dec[par, row, DEC_SEQ * par + c] = np.exp(_LOG_G[h] * (t - c))
    f = lambda a: np.asarray(a, np.float32)
    return f(bias), f(dec), f(xi), f(zeta)


_P_DIST, _P_MASK, _P_DECAY, _P_XI, _P_ZETA = _prompt_tables()
_S_BIAS, _S_DEC, _S_XI, _S_ZETA = _sample_tables()
_GL_PROMPT = [float(np.exp(_LOG_G[h] * BLK)) for h in range(N_RET_HEADS)]
_GL_SAMPLE = [float(np.exp(_LOG_G[h] * DEC_SEQ)) for h in range(N_RET_HEADS)]


def _rms(x, g):
    return x * lax.rsqrt(jnp.mean(x * x, axis=-1, keepdims=True) + RMS_EPS) * g


def _dot(a, b):
    return jnp.dot(a, b, preferred_element_type=F32)


def _dot_nt(a, b):
    return lax.dot_general(a, b, (((1,), (1,)), ((), ())), preferred_element_type=F32)


def _dot_tn(a, b):
    return lax.dot_general(a, b, (((0,), (0,)), ((), ())), preferred_element_type=F32)


def _silu(g):
    return g * (1.0 / (1.0 + jnp.exp(-g)))


def _half_masks(width):
    lane = lax.broadcasted_iota(jnp.int32, (1, width), 1)
    lo = ((lane & (LANES - 1)) < HALF).astype(F32)
    return lo, 1.0 - lo


def _sink_softmax(s, sink):
    m = jnp.maximum(jnp.max(s, axis=-1, keepdims=True), sink)
    p = jnp.exp(s - m)
    den = jnp.sum(p, axis=-1, keepdims=True) + jnp.exp(sink - m)
    return p * (1.0 / den)


def _group_norm(o, g, b):
    mu = jnp.mean(o, axis=-1, keepdims=True)
    d = o - mu
    var = jnp.mean(d * d, axis=-1, keepdims=True)
    return d * lax.rsqrt(var + GN_EPS) * g + b


def _pm_project_stages(x, slot, gmix_ref, win_ref, sc, kv_out=None):
    tm = x.shape[0]
    xn = _rms(x, gmix_ref[...]).astype(BF16)

    pw = 2 * LANES
    lo, hi = _half_masks(pw)

    def stage_q(i):
        cols = slice(i * pw, (i + 1) * pw)
        q = _dot(xn, win_ref[:, C_QA + i * pw:C_QA + (i + 1) * pw])
        sc["qlo"][slot, :, cols] = (q * (lo * HEAD_DIM ** -0.5)).astype(BF16)
        sc["qhi"][slot, :, cols] = (q * (hi * HEAD_DIM ** -0.5)).astype(BF16)

    def stage_kv():
        z = _dot(xn, win_ref[:, C_KV:C_KV + pw])
        low = lax.broadcasted_iota(jnp.int32, (tm, LANES), 1) < HALF
        k = z[:, 0:ATT_KV_W]
        v = z[:, ATT_KV_W:2 * ATT_KV_W]
        if kv_out is not None:
            kv_out[0][0] = k[tm - WINDOW:, :]
            kv_out[1][0] = v[tm - WINDOW:, :]
        k_r = pltpu.roll(k, HALF, axis=1)
        v_r = pltpu.roll(v, HALF, axis=1)
        sc["kd0"][slot] = jnp.where(low, k, k_r).astype(BF16)
        sc["kd1"][slot] = jnp.where(low, k_r, k).astype(BF16)
        sc["vd0"][slot, :, 0:LANES] = jnp.where(low, v, 1.0).astype(BF16)
        sc["vd0"][slot, :, LANES:2 * LANES] = jnp.where(low, 1.0, v_r).astype(BF16)
        sc["vd1"][slot, :, 0:LANES] = jnp.where(low, v_r, 1.0).astype(BF16)
        sc["vd1"][slot, :, LANES:2 * LANES] = jnp.where(low, 1.0, v).astype(BF16)

    def stage_qr():
        qr = _dot(xn, win_ref[:, C_QKR:C_QKR + pw])
        sc["qrlo"][slot] = (qr * lo).astype(BF16)
        sc["qrhi"][slot] = (qr * hi).astype(BF16)

    def stage_kr():
        sc["kr"][slot] = _dot(xn, win_ref[:, C_QKR + pw:C_QKR + 2 * pw]) * (RET_QK_DIM ** -0.5)

    def stage_vr(i):
        cols = slice(i * pw, (i + 1) * pw)
        sc["vr"][slot, :, cols] = _dot(xn, win_ref[:, C_VR + i * pw:C_VR + (i + 1) * pw]).astype(BF16)

    def stage_gate(i):
        cols = slice(i * pw, (i + 1) * pw)
        sc["gate"][slot, :, cols] = _silu(_dot(xn, win_ref[:, C_GR + i * pw:C_GR + (i + 1) * pw]))

    part = functools.partial
    return [part(stage_q, 0), part(stage_q, 1), stage_kv, stage_qr, stage_kr,
            part(stage_vr, 0), part(stage_vr, 1), part(stage_gate, 0), part(stage_gate, 1)]


def _pm_last_block(slot, tm, sc):
    rows = slice(tm - BLK, tm)
    return ([sc["kd0"][slot, rows, :], sc["kd1"][slot, rows, :]],
            [sc["vd0"][slot, rows, :], sc["vd1"][slot, rows, :]])


def _pm_blocks(slot, prev_kd, prev_vd, is_first, state, fillers, tm, sinks_ref, gng_ref, gnb_ref,
               decay_ref, xi_ref, zeta_ref, sc):
    nblk = tm // BLK
    n_units = nblk * (N_KV_HEADS + N_RET_HEADS // 2)
    pending = list(fillers)
    done_units = [0]

    def unit_done():
        done_units[0] += 1
        while pending and (len(fillers) - len(pending)) * n_units < done_units[0] * len(fillers):
            pending.pop(0)()
    lowb = lax.broadcasted_iota(jnp.int32, (BLK, LANES), 1) < HALF
    col = lax.broadcasted_iota(jnp.int32, (BLK, 2 * BLK), 1)
    first_mask = None if is_first is False else jnp.where((col < BLK) & is_first, NEG, 0.0)
    kd_refs = (sc["kd0"], sc["kd1"])
    vd_refs = (sc["vd0"], sc["vd1"])
    qlo, qhi, mix = sc["qlo"], sc["qhi"], sc["mix"]
    n_pairs = N_RET_HEADS // 2

    for j in range(nblk):
        rows = slice(j * BLK, (j + 1) * BLK)
        for kvh in range(N_KV_HEADS):
            if j == 0:
                kd = jnp.concatenate([prev_kd[kvh], kd_refs[kvh][slot, rows, :]], axis=0)
                vd = jnp.concatenate([prev_vd[kvh], vd_refs[kvh][slot, rows, :]], axis=0)
            else:
                krows = slice((j - 1) * BLK, (j + 1) * BLK)
                kd = kd_refs[kvh][slot, krows, :]
                vd = vd_refs[kvh][slot, krows, :]
            c0 = kvh * KV_GROUP * HEAD_DIM
            qst = jnp.concatenate([qlo[slot, rows, c0:c0 + LANES], qhi[slot, rows, c0:c0 + LANES],
                                   qlo[slot, rows, c0 + LANES:c0 + 2 * LANES],
                                   qhi[slot, rows, c0 + LANES:c0 + 2 * LANES]], axis=0)
            s = _dot_nt(qst, kd)
            es, esink = [], []
            for g in range(KV_GROUP):
                h = kvh * KV_GROUP + g
                sg = s[g * BLK:(g + 1) * BLK] + sc["bias"][h]
                if j == 0 and first_mask is not None:
                    sg = sg + first_mask
                sink = sinks_ref[h]
                m = jnp.maximum(jnp.max(sg, axis=-1, keepdims=True), sink)
                es.append(jnp.exp(sg - m).astype(BF16))
                esink.append(jnp.exp(sink - m))
            o = _dot(jnp.concatenate(es, axis=0), vd)
            for pair in range(KV_GROUP // 2):
                oe = o[2 * pair * BLK:(2 * pair + 1) * BLK]
                oo = o[(2 * pair + 1) * BLK:(2 * pair + 2) * BLK]
                num = jnp.where(lowb, oe[:, :LANES], oo[:, LANES:])
                den = (jnp.where(lowb, oe[:, LANES:], oo[:, :LANES])
                       + jnp.where(lowb, esink[2 * pair], esink[2 * pair + 1]))
                cs = c0 + pair * LANES
                mix[slot, rows, cs:cs + LANES] = (num * (1.0 / den)).astype(BF16)
            unit_done()

        for i in range(n_pairs):
            lsl = slice(i * LANES, (i + 1) * LANES)
            kp = sc["kr"][slot, rows, lsl]
            sp = state[i]
            vpair = sc["vr"][slot, rows, 2 * i * RET_V_DIM:(2 * i + 2) * RET_V_DIM]
            q2 = jnp.concatenate([sc["qrlo"][slot, rows, lsl], sc["qrhi"][slot, rows, lsl]], axis=0)
            a = _dot_nt(q2, kp.astype(BF16))
            inner = jnp.concatenate([a[:BLK] * decay_ref[2 * i], a[BLK:] * decay_ref[2 * i + 1]], axis=0)
            oi = _dot(inner.astype(BF16), vpair)
            oc = _dot(q2, sp.astype(BF16))
            for half in range(2):
                h = 2 * i + half
                vsl = slice(h * RET_V_DIM, (h + 1) * RET_V_DIM)
                hr = slice(half * BLK, (half + 1) * BLK)
                o = oi[hr, half * RET_V_DIM:(half + 1) * RET_V_DIM] + oc[hr] * xi_ref[:, vsl]
                r = _group_norm(o, gng_ref[:, vsl], gnb_ref[:, vsl]) * sc["gate"][slot, rows, vsl]
                mix[slot, rows, ATT_Q_W + h * RET_V_DIM:ATT_Q_W + (h + 1) * RET_V_DIM] = r.astype(BF16)
            kz = (kp * zeta_ref[:, lsl]).astype(BF16)
            u = _dot_tn(kz, vpair)
            state[i] = jnp.concatenate(
                [_GL_PROMPT[2 * i] * sp[:RET_QK_DIM] + u[:RET_QK_DIM, :RET_V_DIM],
                 _GL_PROMPT[2 * i + 1] * sp[RET_QK_DIM:] + u[RET_QK_DIM:, RET_V_DIM:]], axis=0)
            unit_done()

    assert not pending
    return state


def _pm_wout_pieces(slot, x_ref, rows, wout_ref, h_ref, sc):
    pw = 2 * LANES
    n = D_MODEL // pw
    parts = []

    def piece(k):
        parts.append(_dot(sc["mix"][slot], wout_ref[:, k * pw:(k + 1) * pw]))
        if k == n - 1:
            h_ref[rows, :] = x_ref[rows, :] + jnp.concatenate(parts, axis=1)

    return [functools.partial(piece, k) for k in range(n)]


def _prompt_mixer_kernel(sinks_ref, xpair_ref, xnext_ref, gmix_ref, win_ref, wout_ref, gng_ref, gnb_ref,
                           dist_ref, mask_ref, decay_ref, xi_ref, zeta_ref,
                           wupf_ref, wdnf_ref, wqf_ref, wof_ref,
                           h_ref, wk_ref, wv_ref, st_ref,
                           wupb_ref, wdnb_ref, wqb_ref, wob_ref,
                           qlo_s, qhi_s, kd0_s, kd1_s, vd0_s, vd1_s,
                           qrlo_s, qrhi_s, kr_s, vr_s, gate_s, mix_s, bias_s, state_s):
    u = pl.program_id(0)
    tm = xnext_ref.shape[0]
    wupb_ref[...] = wupf_ref[...].astype(BF16)
    wdnb_ref[...] = wdnf_ref[...].astype(BF16)
    wqb_ref[...] = wqf_ref[...].astype(BF16)
    wob_ref[...] = wof_ref[...].astype(BF16)
    sc = dict(qlo=qlo_s, qhi=qhi_s, kd0=kd0_s, kd1=kd1_s, vd0=vd0_s, vd1=vd1_s, qrlo=qrlo_s, qrhi=qrhi_s,
              kr=kr_s, vr=vr_s, gate=gate_s, mix=mix_s, bias=bias_s)
    n_pairs = N_RET_HEADS // 2
    blocks = functools.partial(_pm_blocks, tm=tm, sinks_ref=sinks_ref, gng_ref=gng_ref,
                               gnb_ref=gnb_ref, decay_ref=decay_ref, xi_ref=xi_ref, zeta_ref=zeta_ref, sc=sc)

    @pl.when(u == 0)
    def _():
        for h in range(N_ATT_HEADS):
            bias_s[h] = NEG_SLOPES[h] * dist_ref[...] + mask_ref[...]
        state_s[...] = jnp.zeros_like(state_s)
        kd0_s[1] = jnp.zeros(kd0_s.shape[1:], BF16)
        kd1_s[1] = jnp.zeros(kd1_s.shape[1:], BF16)
        vd0_s[1] = jnp.zeros(vd0_s.shape[1:], BF16)
        vd1_s[1] = jnp.zeros(vd1_s.shape[1:], BF16)
        for stage in _pm_project_stages(xpair_ref[0:tm, :], 0, gmix_ref, win_ref, sc):
            stage()

    seq_start = (u % 2) == 0
    state = [jnp.where(seq_start, 0.0, state_s[i * LANES:(i + 1) * LANES, :]) for i in range(n_pairs)]

    prev_kd, prev_vd = _pm_last_block(1, tm, sc)
    stages = _pm_project_stages(xpair_ref[tm:2 * tm, :], 1, gmix_ref, win_ref, sc, kv_out=(wk_ref, wv_ref))
    state = blocks(0, prev_kd, prev_vd, seq_start, state, stages)
    wout0 = _pm_wout_pieces(0, xpair_ref, slice(0, tm), wout_ref, h_ref, sc)

    prev_kd, prev_vd = _pm_last_block(0, tm, sc)
    stages = _pm_project_stages(xnext_ref[...], 0, gmix_ref, win_ref, sc)
    state = blocks(1, prev_kd, prev_vd, False, state, wout0 + stages)
    for piece in _pm_wout_pieces(1, xpair_ref, slice(tm, 2 * tm), wout_ref, h_ref, sc):
        piece()

    for i in range(n_pairs):
        state_s[i * LANES:(i + 1) * LANES, :] = state[i]
        st_ref[0, i * LANES:(i + 1) * LANES, :] = state[i]


def _prompt_mixer(x, g_mix, w_in, w_out, sinks, gn_g, gn_b, side_f32):
    b, s, d = x.shape
    tm = TM_MIX
    n_tiles = b * s // tm
    steps = n_tiles // 2
    seq_steps = s // (2 * tm)
    assert s % (2 * tm) == 0 and seq_steps == 2, "kernel assumes 4 tiles per sequence"
    x2d = x.reshape(b * s, d)
    const = lambda shape: pl.BlockSpec(shape, lambda i: (0,) * len(shape), pipeline_mode=pl.Buffered(1))
    slot2 = lambda rows, cols, dt: pltpu.VMEM((2, rows, cols), dt)
    side_specs = [pl.BlockSpec((w.shape[0] // steps, w.shape[1]), lambda i: (i, 0)) for w in side_f32]
    outs = pl.pallas_call(
        _prompt_mixer_kernel,
        grid=(steps,),
        in_specs=[
            pl.BlockSpec(memory_space=pltpu.SMEM),
            pl.BlockSpec((2 * tm, d), lambda i: (i, 0)),
            pl.BlockSpec((tm, d), lambda i: (jnp.minimum(2 * i + 2, n_tiles - 1), 0)),
            const((1, d)), const((d, D_IN)), const((MIX_OUT, d)),
            const((1, RET_V_W)), const((1, RET_V_W)),
            const((BLK, 2 * BLK)), const((BLK, 2 * BLK)),
            const((N_RET_HEADS, BLK, BLK)), const((BLK, RET_V_W)), const((BLK, RET_QK_W)),
        ] + side_specs,
        out_specs=[
            pl.BlockSpec((2 * tm, d), lambda i: (i, 0)),
            pl.BlockSpec((1, WINDOW, ATT_KV_W), lambda i: (i // seq_steps, 0, 0)),
            pl.BlockSpec((1, WINDOW, ATT_KV_W), lambda i: (i // seq_steps, 0, 0)),
            pl.BlockSpec((1, RET_QK_W, RET_V_DIM), lambda i: (i // seq_steps, 0, 0)),
        ] + side_specs,
        out_shape=[
            jax.ShapeDtypeStruct((b * s, d), F32),
            jax.ShapeDtypeStruct((b, WINDOW, ATT_KV_W), F32),
            jax.ShapeDtypeStruct((b, WINDOW, ATT_KV_W), F32),
            jax.ShapeDtypeStruct((b, RET_QK_W, RET_V_DIM), F32),
        ] + [jax.ShapeDtypeStruct(w.shape, BF16) for w in side_f32],
        scratch_shapes=[
            slot2(tm, ATT_Q_W, BF16), slot2(tm, ATT_Q_W, BF16),
            slot2(tm, LANES, BF16), slot2(tm, LANES, BF16),
            slot2(tm, 2 * LANES, BF16), slot2(tm, 2 * LANES, BF16),
            slot2(tm, RET_QK_W, BF16), slot2(tm, RET_QK_W, BF16),
            slot2(tm, RET_QK_W, F32), slot2(tm, RET_V_W, BF16),
            slot2(tm, RET_V_W, F32), slot2(tm, MIX_OUT, BF16),
            pltpu.VMEM((N_ATT_HEADS, BLK, 2 * BLK), F32),
            pltpu.VMEM((RET_QK_W, RET_V_DIM), F32),
        ],
        compiler_params=pltpu.CompilerParams(
            dimension_semantics=("arbitrary",), vmem_limit_bytes=VMEM_LIMIT),
        name="prompt_mixer",
    )(sinks, x2d, x2d, g_mix, w_in, w_out, gn_g, gn_b,
      jnp.asarray(_P_DIST), jnp.asarray(_P_MASK), jnp.asarray(_P_DECAY), jnp.asarray(_P_XI),
      jnp.asarray(_P_ZETA), *side_f32)
    return (outs[0].reshape(b, s, d),) + tuple(outs[1:])


def _memkv_kernel(mem_ref, g_ref, wk_ref, wv_ref, win_ref, wout_ref,
                  mk_ref, mv_ref, mkb_ref, mvb_ref, winb_ref, woutb_ref):
    winb_ref[...] = win_ref[...].astype(BF16)
    woutb_ref[...] = wout_ref[...].astype(BF16)
    mn = _rms(mem_ref[...], g_ref[...]).astype(BF16)
    mk = _dot(mn, wk_ref[...].astype(BF16))
    mv = _dot(mn, wv_ref[...].astype(BF16))
    tm = mem_ref.shape[0]
    group = X_D_HALVES * N_X_HEADS
    for hd in range(N_X_HEADS):
        for dh in range(X_D_HALVES):
            cols = slice(hd * X_HEAD_DIM + dh * LANES, hd * X_HEAD_DIM + (dh + 1) * LANES)
            rows = pl.ds(dh * N_X_HEADS + hd, tm, stride=group)
            mk_ref[rows, :] = mk[:, cols]
            mv_ref[rows, :] = mv[:, cols]
    mkb_ref[...] = mk.astype(BF16)
    mvb_ref[...] = mv.astype(BF16)


def _memory_kv(mem2d, g_mem, w_xk, w_xv, w_in, w_out):
    n, d = mem2d.shape
    tm = 512
    row = pl.BlockSpec((tm, d), lambda i: (i, 0))
    rows_out = pl.BlockSpec((tm * d // LANES, LANES), lambda i: (i, 0))
    const = lambda shape: pl.BlockSpec(shape, lambda i: (0,) * len(shape), pipeline_mode=pl.Buffered(1))
    steps = n // tm
    win_blk = pl.BlockSpec((w_in.shape[0] // steps, w_in.shape[1]), lambda i: (i, 0))
    wout_blk = pl.BlockSpec((w_out.shape[0] // steps, w_out.shape[1]), lambda i: (i, 0))
    return pl.pallas_call(
        _memkv_kernel,
        grid=(n // tm,),
        in_specs=[row, const((1, d)), const((d, d)), const((d, d)), win_blk, wout_blk],
        out_specs=[rows_out, rows_out, row, row, win_blk, wout_blk],
        out_shape=[jax.ShapeDtypeStruct((n * d // LANES, LANES), F32),
                   jax.ShapeDtypeStruct((n * d // LANES, LANES), F32),
                   jax.ShapeDtypeStruct((n, d), BF16), jax.ShapeDtypeStruct((n, d), BF16),
                   jax.ShapeDtypeStruct(w_in.shape, BF16), jax.ShapeDtypeStruct(w_out.shape, BF16)],
        compiler_params=pltpu.CompilerParams(
            dimension_semantics=("arbitrary",), vmem_limit_bytes=VMEM_LIMIT),
        name="memory_kv",
    )(mem2d, g_mem, w_xk, w_xv, w_in, w_out)


def _prompt_xattn_kernel(h_ref, g_ref, wq_ref, wo_ref, mk_ref, mv_ref, out_ref, o_s):
    def stages(r0):
        rows = slice(r0, r0 + SUB_ROWS)
        env = {}

        def project():
            env["h"] = h_ref[0, rows, :]
            xn = _rms(env["h"], g_ref[...]).astype(BF16)
            env["q"] = (_dot(xn, wq_ref[...]) * (X_HEAD_DIM ** -0.5)).astype(BF16)

        def head(hd):
            sl = slice(hd * X_HEAD_DIM, (hd + 1) * X_HEAD_DIM)
            s = _dot_nt(env["q"][:, sl], mk_ref[0, :, sl])
            m = jnp.max(s, axis=-1, keepdims=True)
            p = jnp.exp(s - m)
            p = p * (1.0 / jnp.sum(p, axis=-1, keepdims=True))
            o_s[rows, sl] = _dot(p.astype(BF16), mv_ref[0, :, sl]).astype(BF16)

        def output():
            out_ref[0, rows, :] = env["h"] + _dot(o_s[rows, :], wo_ref[...])

        return [project] + [functools.partial(head, hd) for hd in range(N_X_HEADS)] + [output]

    chains = [stages(r0) for r0 in range(0, h_ref.shape[1], SUB_ROWS)]
    n_stage = len(chains[0])
    for step in range(n_stage + len(chains) - 1):
        for lag, chain in enumerate(chains):
            if 0 <= step - lag < n_stage:
                chain[step - lag]()


def _prompt_xattn(h, g, w_xq, w_xo, mkb, mvb):
    b, s, d = h.shape
    tm = TM_X
    const = lambda shape: pl.BlockSpec(shape, lambda i, j: (0,) * len(shape))
    tok = pl.BlockSpec((1, tm, d), lambda i, j: (i, j, 0))
    mem = pl.BlockSpec((1, N_MEM, d), lambda i, j: (i, 0, 0))
    return pl.pallas_call(
        _prompt_xattn_kernel,
        grid=(b, s // tm),
        in_specs=[tok, const((1, d)), const((d, d)), const((d, d)), mem, mem],
        out_specs=tok,
        out_shape=jax.ShapeDtypeStruct((b, s, d), F32),
        scratch_shapes=[pltpu.VMEM((tm, d), BF16)],
        compiler_params=pltpu.CompilerParams(
            dimension_semantics=("arbitrary", "arbitrary"), vmem_limit_bytes=VMEM_LIMIT),
        name="prompt_xattn",
    )(h, g, w_xq, w_xo, mkb, mvb)


def _sample_mixer_kernel(sinks_ref, x_ref, gmix_ref, win_ref, wout_ref, gng_ref, gnb_ref,
                         ck_ref, cv_ref, st_ref, bias_ref, dec_ref, xi_ref, zeta_ref,
                         h_ref, swk_ref, swv_ref, sst_ref):
    bb = ck_ref.shape[0]
    nt = bb // 2
    x = x_ref[...]
    xn = _rms(x, gmix_ref[...]).astype(BF16)
    tile3 = lambda a: a.reshape(nt, SUBLANES, a.shape[-1])

    q = _dot(xn, win_ref[:, C_QA:C_QA + ATT_Q_W]) * (HEAD_DIM ** -0.5)
    kv = _dot(xn, win_ref[:, C_KV:C_KV + 2 * ATT_KV_W])
    qkr = _dot(xn, win_ref[:, C_QKR:C_QKR + 2 * RET_QK_W])
    vr = _dot(xn, win_ref[:, C_VR:C_VR + RET_V_W])
    gate3 = tile3(_silu(_dot(xn, win_ref[:, C_GR:C_GR + RET_V_W])))

    lo512, hi512 = _half_masks(ATT_Q_W)
    q_r = pltpu.roll(q, HALF, axis=1)
    q_nat3 = tile3(q)
    q_rot3 = tile3(q_r)
    lo3 = lo512.reshape(1, 1, ATT_Q_W)
    hi3 = hi512.reshape(1, 1, ATT_Q_W)
    qa3 = (q_nat3 * lo3).astype(BF16)
    qb3 = (q_rot3 * lo3).astype(BF16)
    qc3 = (q_rot3 * hi3).astype(BF16)
    qd3 = (q_nat3 * hi3).astype(BF16)
    t128 = lambda a, i: a[:, :, i * LANES:(i + 1) * LANES]
    qs = jnp.concatenate([t128(qa3, 0), t128(qb3, 1), t128(qa3, 1), t128(qb3, 2),
                          t128(qc3, 2), t128(qd3, 2), t128(qc3, 3), t128(qd3, 3)], axis=1)

    k3 = tile3(kv[:, :ATT_KV_W])
    v3 = tile3(kv[:, ATT_KV_W:])
    pad_kv = jnp.zeros((nt, BLK - SUBLANES, LANES), BF16)
    knew_pad = jnp.concatenate([k3.astype(BF16), pad_kv], axis=1)
    vnew_pad = jnp.concatenate([v3.astype(BF16), pad_kv], axis=1)
    to_lanes = lambda a3: jnp.swapaxes(
        jnp.concatenate([a3, jnp.zeros((nt, BLK - SUBLANES, LANES), F32)], axis=1), 1, 2)
    k3t, v3t = to_lanes(k3), to_lanes(v3)
    roll3 = lambda a, sh: pltpu.roll(a.reshape(nt * BLK, LANES), sh, axis=1).reshape(nt, BLK, LANES)

    lo256, _ = _half_masks(RET_QK_W)
    qr3 = tile3(qkr[:, :RET_QK_W])
    kr3 = tile3(qkr[:, RET_QK_W:] * (RET_QK_DIM ** -0.5))
    vr3 = tile3(vr)
    lane256 = lax.broadcasted_iota(jnp.int32, (1, 1, RET_QK_W), 2)
    qrs = jnp.concatenate(
        [(qr3 * ((lane256 >= h * RET_QK_DIM) & (lane256 < (h + 1) * RET_QK_DIM)).astype(F32)).astype(BF16)
         for h in range(N_RET_HEADS)],
        axis=1)
    kr_pad = jnp.concatenate([kr3.astype(BF16), jnp.zeros((nt, BLK - SUBLANES, RET_QK_W), BF16)], axis=1)
    vr_pad = jnp.concatenate([vr3.astype(BF16), jnp.zeros((nt, BLK - SUBLANES, RET_V_W), BF16)], axis=1)

    lane = lax.broadcasted_iota(jnp.int32, (1, 1, LANES), 2)
    row8 = lax.broadcasted_iota(jnp.int32, (1, SUBLANES, 1), 1)
    bmm_nt = lambda a, b: jnp.einsum('bqd,bkd->bqk', a, b, preferred_element_type=F32)
    bmm = lambda a, b: jnp.einsum('bqk,bkd->bqd', a, b, preferred_element_type=F32)

    att_par, ret_par = [], []
    for par in range(2):
        bsl = pl.ds(par, nt, stride=2)
        ckt = ck_ref[bsl]
        cvt = cv_ref[bsl]
        keep = lane < WINDOW - DEC_SEQ
        new_shift = WINDOW - DEC_SEQ - DEC_SEQ * par
        swk_ref[bsl] = jnp.where(keep, roll3(ckt, WINDOW - DEC_SEQ), roll3(k3t, new_shift))
        swv_ref[bsl] = jnp.where(keep, roll3(cvt, WINDOW - DEC_SEQ), roll3(v3t, new_shift))

        s = jnp.concatenate([bmm(qs, ckt.astype(BF16)), bmm_nt(qs, knew_pad)], axis=2) + bias_ref[par]
        ps = []
        for h in range(N_ATT_HEADS):
            ps.append(_sink_softmax(s[:, h * SUBLANES:(h + 1) * SUBLANES, :], sinks_ref[h]).astype(BF16))
        p_all = jnp.concatenate(ps, axis=1)
        o = bmm_nt(p_all[:, :, :BLK], cvt.astype(BF16)) + bmm(p_all[:, :, BLK:], vnew_pad)
        o_r = pltpu.roll(o.reshape(nt * N_ATT_HEADS * SUBLANES, LANES), HALF, axis=1).reshape(o.shape)
        hr = lambda a, h: a[:, h * SUBLANES:(h + 1) * SUBLANES, :]
        low = lane < HALF
        att_par.append(jnp.concatenate([
            jnp.where(low, hr(o, 0), hr(o_r, 1)), jnp.where(low, hr(o, 2), hr(o_r, 3)),
            jnp.where(low, hr(o_r, 4), hr(o, 5)), jnp.where(low, hr(o_r, 6), hr(o, 7))], axis=2))

        st = st_ref[bsl]
        oc = bmm(qrs, st.astype(BF16))
        inner = (bmm_nt(qrs, kr_pad) * dec_ref[par]).astype(BF16)
        oi = bmm(inner, vr_pad)
        rs = []
        for h in range(N_RET_HEADS):
            vsl = slice(h * RET_V_DIM, (h + 1) * RET_V_DIM)
            rsl = slice(h * SUBLANES, (h + 1) * SUBLANES)
            o_h = oi[:, rsl, vsl] + oc[:, rsl, :] * xi_ref[par, rsl, :]
            rs.append(_group_norm(o_h, gng_ref[:, vsl], gnb_ref[:, vsl]) * gate3[:, :, vsl])
        ret_par.append(jnp.concatenate(rs, axis=2))

        kz3 = (kr3 * zeta_ref[par]).astype(BF16)
        vr3_b = vr3.astype(BF16)
        for p in range(nt):
            for i in range(N_RET_HEADS // 2):
                u = _dot_tn(kz3[p][:, i * LANES:(i + 1) * LANES],
                            vr3_b[p][:, 2 * i * RET_V_DIM:(2 * i + 2) * RET_V_DIM])
                for half in range(2):
                    h = 2 * i + half
                    dsl = slice(h * RET_QK_DIM, (h + 1) * RET_QK_DIM)
                    sst_ref[2 * p + par, dsl, :] = (
                        _GL_SAMPLE[h] * st[p, dsl, :]
                        + u[half * RET_QK_DIM:(half + 1) * RET_QK_DIM, half * RET_V_DIM:(half + 1) * RET_V_DIM])

    own0 = row8 < DEC_SEQ
    att3 = jnp.where(own0, att_par[0], att_par[1])
    ret3 = jnp.where(own0, ret_par[0], ret_par[1])
    mix = jnp.concatenate([att3, ret3], axis=2).reshape(2 * nt * DEC_SEQ, MIX_OUT).astype(BF16)
    h_ref[...] = x + _dot(mix, wout_ref[...])


def _sample_mixer(x2d, g_mix, w_in, w_out, sinks, gn_g, gn_b, ck, cv, st):
    n, d = x2d.shape
    nb = ck.shape[0]
    bb = BB_MIX
    r = bb * DEC_SEQ
    const = lambda shape: pl.BlockSpec(shape, lambda i: (0,) * len(shape))
    row = pl.BlockSpec((r, d), lambda i: (i, 0))
    win = pl.BlockSpec((bb, WINDOW, ATT_KV_W), lambda i: (i, 0, 0))
    state = pl.BlockSpec((bb, RET_QK_W, RET_V_DIM), lambda i: (i, 0, 0))
    return pl.pallas_call(
        _sample_mixer_kernel,
        grid=(nb // bb,),
        in_specs=[
            pl.BlockSpec(memory_space=pltpu.SMEM),
            row, const((1, d)), const((d, D_IN)), const((MIX_OUT, d)),
            const((1, RET_V_W)), const((1, RET_V_W)),
            win, win, state,
            const(_S_BIAS.shape), const(_S_DEC.shape), const(_S_XI.shape), const(_S_ZETA.shape),
        ],
        out_specs=[row, win, win, state],
        out_shape=[
            jax.ShapeDtypeStruct((n, d), F32),
            jax.ShapeDtypeStruct((nb, WINDOW, ATT_KV_W), F32),
            jax.ShapeDtypeStruct((nb, WINDOW, ATT_KV_W), F32),
            jax.ShapeDtypeStruct((nb, RET_QK_W, RET_V_DIM), F32),
        ],
        compiler_params=pltpu.CompilerParams(
            dimension_semantics=("arbitrary",), vmem_limit_bytes=VMEM_LIMIT),
        name="sample_mixer",
    )(sinks, x2d, g_mix, w_in, w_out, gn_g, gn_b, ck, cv, st,
      jnp.asarray(_S_BIAS), jnp.asarray(_S_DEC), jnp.asarray(_S_XI), jnp.asarray(_S_ZETA))


def _head_slab(x_ref, b, hd):
    group = X_D_HALVES * N_X_HEADS
    halves = [x_ref[b, pl.ds(dh * N_X_HEADS + hd, N_MEM, stride=group), :] for dh in range(X_D_HALVES)]
    return jnp.concatenate(halves, axis=1).astype(BF16)


def _mlp_value(h, g_ref, wup_ref, wdn_ref, gf_ref, fillers=None):
    xn = _rms(h, g_ref[...]).astype(BF16)
    acc = h
    piece = FF_CHUNK // N_X_HEADS
    opiece = D_MODEL // N_X_HEADS
    for c in range(D_FF // FF_CHUNK):
        qk, softmax, pv = fillers[c] if fillers is not None else (None, None, None)
        hid = []
        for k in range(N_X_HEADS):
            cols = slice(c * FF_CHUNK + k * piece, c * FF_CHUNK + (k + 1) * piece)
            u = jnp.maximum(_dot(xn, wup_ref[:, cols]), 0.0)
            hid.append((u * u).astype(BF16))
            if qk is not None:
                qk(k)
        if softmax is not None:
            softmax()
        hid = jnp.concatenate(hid, axis=1)
        rows_c = slice(c * FF_CHUNK, (c + 1) * FF_CHUNK)
        out = []
        for k in range(N_X_HEADS):
            out.append(_dot(hid, wdn_ref[rows_c, k * opiece:(k + 1) * opiece]))
            if pv is not None:
                pv(k)
        acc = acc + jnp.concatenate(out, axis=1)
    return _rms(acc, gf_ref[...])


def _mlp_xattn_kernel(hp_ref, hsm_ref, gx_ref, wq_ref, wo_ref, xk_ref, xv_ref, g_ref, wup_ref, wdn_ref, gf_ref,
                      yp_ref, ys_ref):
    i = pl.program_id(0)
    n = pl.num_programs(0) - 1
    bb = xk_ref.shape[0]
    rows = bb * DEC_SEQ
    assert bb == D_FF // FF_CHUNK and bb % 2 == 0

    @pl.when(i == 0)
    def _():
        xn = _rms(hsm_ref[...], gx_ref[...]).astype(BF16)
        ys_ref[...] = _dot(xn, wq_ref[...]) * (X_HEAD_DIM ** -0.5)

    @pl.when(i < n)
    def _():
        r0 = pl.multiple_of(i * rows, rows)
        own0 = lax.broadcasted_iota(jnp.int32, (SUBLANES, 1), 0) < DEC_SEQ
        o_rows = {}

        def attend(b):
            t = b // 2
            tile_rows = pl.ds(r0 + t * SUBLANES, SUBLANES)
            env = dict(s=[], o=[])

            def qk(hd):
                if hd == 0:
                    env["q"] = ys_ref[tile_rows, :].astype(BF16)
                env["s"].append(_dot_nt(env["q"][:, hd * X_HEAD_DIM:(hd + 1) * X_HEAD_DIM],
                                        _head_slab(xk_ref, b, hd)))

            def softmax():
                s = jnp.concatenate(env["s"], axis=0)
                m = jnp.max(s, axis=-1, keepdims=True)
                p = jnp.exp(s - m)
                env["p"] = p * (1.0 / jnp.sum(p, axis=-1, keepdims=True))

            def pv(hd):
                p = env["p"][hd * SUBLANES:(hd + 1) * SUBLANES].astype(BF16)
                env["o"].append(_dot(p, _head_slab(xv_ref, b, hd)))
                if hd == N_X_HEADS - 1:
                    o_rows[b] = jnp.concatenate(env["o"], axis=1)
                    if b % 2 == 1:
                        ys_ref[tile_rows, :] = jnp.where(own0, o_rows[b - 1], o_rows[b])

            return qk, softmax, pv

        fillers = [attend(b) for b in range(bb)]
        yp_ref[...] = _mlp_value(hp_ref[...], g_ref, wup_ref, wdn_ref, gf_ref, fillers)

    @pl.when(i == n)
    def _():
        hs = hsm_ref[...] + _dot(ys_ref[...].astype(BF16), wo_ref[...])
        ys_ref[...] = _mlp_value(hs, g_ref, wup_ref, wdn_ref, gf_ref)


def _mlp_xattn(hp2d, hsm, g_xattn, w_xq, w_xo, xk, xv, g_mlp, w_up, w_down, g_final):
    n, d = hp2d.shape
    ns = hsm.shape[0]
    nb = xk.shape[0]
    bb = BB_X
    tm = n // (nb // bb)
    n_tiles = n // tm
    assert n_tiles * bb == nb and tm % SUBLANES == 0
    clip = lambda i: jnp.minimum(i, n_tiles - 1)
    prompt = pl.BlockSpec((tm, d), lambda i: (clip(i), 0))
    mem = pl.BlockSpec((bb,) + xk.shape[1:], lambda i: (clip(i), 0, 0))
    const = lambda shape: pl.BlockSpec(shape, lambda i: (0,) * len(shape), pipeline_mode=pl.Buffered(1))
    return pl.pallas_call(
        _mlp_xattn_kernel,
        grid=(n_tiles + 1,),
        in_specs=[prompt, const((ns, d)), const((1, d)), const((d, d)), const((d, d)), mem, mem,
                  const((1, d)), const((d, D_FF)), const((D_FF, d)), const((1, d))],
        out_specs=[prompt, pl.BlockSpec((ns, d), lambda i: (0, 0))],
        out_shape=[jax.ShapeDtypeStruct((n, d), F32), jax.ShapeDtypeStruct((ns, d), F32)],
        compiler_params=pltpu.CompilerParams(
            dimension_semantics=("arbitrary",), vmem_limit_bytes=VMEM_LIMIT),
        name="mlp_xattn",
    )(hp2d, hsm, g_xattn, w_xq, w_xo, xk, xv, g_mlp, w_up, w_down, g_final)


def _mem_rows(c):
    nb = c.shape[0]
    c = c.reshape(nb, N_MEM, N_X_HEADS, X_D_HALVES, LANES)
    return jnp.transpose(c, (0, 1, 3, 2, 4)).reshape(nb, N_MEM * X_D_HALVES * N_X_HEADS, LANES)


def kernel(x_prompt, x_sample, mem_prompt, cache_win_k, cache_win_v, state_ret, cache_mem_k, cache_mem_v,
           g_mix, w_in, attn_sinks, ret_gn_g, ret_gn_b, w_out, g_xattn, g_mem, w_xq, w_xk, w_xv, w_xo,
           g_mlp, w_up, w_down, g_final):
    depth = w_in.shape[0]
    assert depth == 1, "single-layer trunk"
    b, s, d = x_prompt.shape
    nb, ls, _ = x_sample.shape
    row = lambda a: a.reshape(1, -1)
    sinks = attn_sinks[0]
    gn_g, gn_b = row(ret_gn_g[0]), row(ret_gn_b[0])
    g_fin = row(g_final)

    mk, mv, mkb, mvb, w_in_b, w_out_b = _memory_kv(
        mem_prompt.reshape(b * N_MEM, d), row(g_mem[0]), w_xk[0], w_xv[0], w_in[0], w_out[0])
    hp, p_wk, p_wv, p_rs, w_up_b, w_dn_b, w_xq_b, w_xo_b = _prompt_mixer(
        x_prompt, row(g_mix[0]), w_in_b, w_out_b, sinks, gn_g, gn_b,
        (w_up[0], w_down[0], w_xq[0], w_xo[0]))
    hp = _prompt_xattn(hp, row(g_xattn[0]), w_xq_b, w_xo_b,
                       mkb.reshape(b, N_MEM, d), mvb.reshape(b, N_MEM, d))

    win_t = lambda c: jnp.transpose(c, (0, 2, 3, 1)).reshape(nb, ATT_KV_W, WINDOW)
    win_t_inv = lambda a: jnp.transpose(a.reshape(nb, N_KV_HEADS, HEAD_DIM, WINDOW),
                                        (0, 3, 1, 2)).reshape(1, nb, WINDOW, N_KV_HEADS, HEAD_DIM)
    hs, s_wk, s_wv, s_rs = _sample_mixer(
        x_sample.reshape(nb * ls, d), row(g_mix[0]), w_in_b, w_out_b, sinks, gn_g, gn_b,
        win_t(cache_win_k[0]), win_t(cache_win_v[0]), state_ret[0].reshape(nb, RET_QK_W, RET_V_DIM))

    y_prompt, y_sample = _mlp_xattn(
        hp.reshape(b * s, d), hs, row(g_xattn[0]), w_xq_b, w_xo_b,
        _mem_rows(cache_mem_k[0]), _mem_rows(cache_mem_v[0]), row(g_mlp[0]), w_up_b, w_dn_b, g_fin)
    y_prompt = y_prompt.reshape(b, s, d)
    y_sample = y_sample.reshape(nb, ls, d)

    win5 = lambda a, n: a.reshape(1, n, WINDOW, N_KV_HEADS, HEAD_DIM)
    ret5 = lambda a, n: a.reshape(1, n, N_RET_HEADS, RET_QK_DIM, RET_V_DIM)
    mem5 = lambda a: jnp.transpose(a.reshape(b, N_MEM, X_D_HALVES, N_X_HEADS, LANES),
                                   (0, 1, 3, 2, 4)).reshape(1, b, N_MEM, N_X_HEADS, X_HEAD_DIM)
    return (y_prompt, y_sample,
            win5(p_wk, b), win5(p_wv, b), ret5(p_rs, b), mem5(mk), mem5(mv),
            win_t_inv(s_wk), win_t_inv(s_wv), ret5(s_rs, nb))
```

```python
import functools

import jax
import jax.numpy as jnp
import numpy as np
from jax import lax
from jax.experimental import pallas as pl
from jax.experimental.pallas import tpu as pltpu

F32 = jnp.float32
BF16 = jnp.bfloat16

D_MODEL = 1024
BATCH = 8
SEQ = 2048
DEC_BATCH = 128
DEC_SEQ = 4
HEAD_DIM = 64
N_ATT_HEADS = 8
N_KV_HEADS = 2
KV_GROUP = N_ATT_HEADS // N_KV_HEADS
WINDOW = 128
BLK = 128
N_RET_HEADS = 4
RET_QK_DIM = 64
RET_V_DIM = 128
N_MEM = 256
N_X_HEADS = 4
X_HEAD_DIM = D_MODEL // N_X_HEADS
D_FF = 4 * D_MODEL
RMS_EPS = 1e-6
GN_EPS = 1e-5

ATT_Q_W = N_ATT_HEADS * HEAD_DIM
ATT_KV_W = N_KV_HEADS * HEAD_DIM
RET_QK_W = N_RET_HEADS * RET_QK_DIM
RET_V_W = N_RET_HEADS * RET_V_DIM
MIX_OUT = ATT_Q_W + RET_V_W
D_IN = ATT_Q_W + 2 * ATT_KV_W + 2 * RET_QK_W + 2 * RET_V_W
C_QA, C_KV, C_QKR, C_VR, C_GR = 0, 512, 768, 1280, 1792

LANES = 128
SUBLANES = 8
HALF = LANES // 2
X_D_HALVES = X_HEAD_DIM // LANES
NEG = -1e30
VMEM_LIMIT = 56 * 1024 * 1024

TM_MIX = 512
TM_X = 2048
SUB_ROWS = 512
FF_CHUNK = 1024
BB_MIX = 32
BB_X = 4

NEG_SLOPES = [-(2.0 ** (-8.0 * (i + 1) / N_ATT_HEADS)) for i in range(N_ATT_HEADS)]
_LOG_G = np.log(1.0 - 2.0 ** (-5.0 - np.arange(N_RET_HEADS))).astype(np.float32).astype(np.float64)


def _prompt_tables():
    qi = np.arange(BLK)[:, None]
    kj = np.arange(2 * BLK)[None, :]
    dist = (qi + BLK - kj).astype(np.float64)
    mask = np.where((dist >= 0) & (dist < WINDOW), 0.0, NEG)
    l = np.arange(BLK, dtype=np.float64)
    diff = l[:, None] - l[None, :]
    decay = np.where(diff >= 0, np.exp(_LOG_G[:, None, None] * np.maximum(diff, 0.0)), 0.0)
    xi = np.exp((l[:, None] + 1.0) * _LOG_G[None, :])
    zeta = np.exp((BLK - 1.0 - l)[:, None] * _LOG_G[None, :])
    xi_t = np.repeat(xi, RET_V_DIM, axis=1)
    zeta_t = np.repeat(zeta, RET_QK_DIM, axis=1)
    f = lambda a: np.asarray(a, np.float32)
    return f(dist), f(mask), f(decay), f(xi_t), f(zeta_t)


def _sample_tables():
    slopes = -np.asarray(NEG_SLOPES)
    bias = np.full((2, N_ATT_HEADS * SUBLANES, 2 * BLK), NEG, np.float64)
    dec = np.zeros((2, N_RET_HEADS * SUBLANES, BLK), np.float64)
    xi = np.zeros((2, N_RET_HEADS * SUBLANES, RET_V_DIM), np.float64)
    zeta = np.zeros((2, SUBLANES, RET_QK_W), np.float64)
    for par in range(2):
        for r in range(SUBLANES):
            own = DEC_SEQ * par <= r < DEC_SEQ * (par + 1)
            t = r - DEC_SEQ * par if own else r % DEC_SEQ
            for h in range(N_ATT_HEADS):
                row = h * SUBLANES + r
                for j in range(WINDOW):
                    d = t + WINDOW - j
                    if 0 <= d < WINDOW:
                        bias[par, row, j] = -slopes[h] * d
                for c in range(DEC_SEQ):
                    d = t - c
                    if d >= 0:
                        bias[par, row, WINDOW + DEC_SEQ * par + c] = -slopes[h] * d
            for h in range(N_RET_HEADS):
                row = h * SUBLANES + r
                if own:
                    xi[par, row, :] = np.exp((t + 1.0) * _LOG_G[h])
                    zeta[par, r, h * RET_QK_DIM:(h + 1) * RET_QK_DIM] = np.exp((DEC_SEQ - 1.0 - t) * _LOG_G[h])
                    for c in range(t + 1):
                        dec[par, row, DEC_SEQ * par + c] = np.exp(_LOG_G[h] * (t - c))
    f = lambda a: np.asarray(a, np.float32)
    return f(bias), f(dec), f(xi), f(zeta)


_P_DIST, _P_MASK, _P_DECAY, _P_XI, _P_ZETA = _prompt_tables()
_S_BIAS, _S_DEC, _S_XI, _S_ZETA = _sample_tables()
_GL_PROMPT = [float(np.exp(_LOG_G[h] * BLK)) for h in range(N_RET_HEADS)]
_GL_SAMPLE = [float(np.exp(_LOG_G[h] * DEC_SEQ)) for h in range(N_RET_HEADS)]


def _rms(x, g):
    return x * lax.rsqrt(jnp.mean(x * x, axis=-1, keepdims=True) + RMS_EPS) * g


def _dot(a, b):
    return jnp.dot(a, b, preferred_element_type=F32)


def _dot_nt(a, b):
    return lax.dot_general(a, b, (((1,), (1,)), ((), ())), preferred_element_type=F32)


def _dot_tn(a, b):
    return lax.dot_general(a, b, (((0,), (0,)), ((), ())), preferred_element_type=F32)


def _silu(g):
    return g * (1.0 / (1.0 + jnp.exp(-g)))


def _half_masks(width):
    lane = lax.broadcasted_iota(jnp.int32, (1, width), 1)
    lo = ((lane & (LANES - 1)) < HALF).astype(F32)
    return lo, 1.0 - lo


def _sink_softmax(s, sink):
    m = jnp.maximum(jnp.max(s, axis=-1, keepdims=True), sink)
    p = jnp.exp(s - m)
    den = jnp.sum(p, axis=-1, keepdims=True) + jnp.exp(sink - m)
    return p * (1.0 / den)


def _group_norm(o, g, b):
    mu = jnp.mean(o, axis=-1, keepdims=True)
    d = o - mu
    var = jnp.mean(d * d, axis=-1, keepdims=True)
    return d * lax.rsqrt(var + GN_EPS) * g + b


def _pm_project_stages(x, slot, gmix_ref, win_ref, sc, kv_out=None):
    tm = x.shape[0]
    xn = _rms(x, gmix_ref[...]).astype(BF16)

    pw = 2 * LANES
    lo, hi = _half_masks(pw)

    def stage_q(i):
        cols = slice(i * pw, (i + 1) * pw)
        q = _dot(xn, win_ref[:, C_QA + i * pw:C_QA + (i + 1) * pw])
        sc["qlo"][slot, :, cols] = (q * (lo * HEAD_DIM ** -0.5)).astype(BF16)
        sc["qhi"][slot, :, cols] = (q * (hi * HEAD_DIM ** -0.5)).astype(BF16)

    def stage_kv():
        z = _dot(xn, win_ref[:, C_KV:C_KV + pw])
        low = lax.broadcasted_iota(jnp.int32, (tm, LANES), 1) < HALF
        k = z[:, 0:ATT_KV_W]
        v = z[:, ATT_KV_W:2 * ATT_KV_W]
        if kv_out is not None:
            kv_out[0][0] = k[tm - WINDOW:, :]
            kv_out[1][0] = v[tm - WINDOW:, :]
        k_r = pltpu.roll(k, HALF, axis=1)
        v_r = pltpu.roll(v, HALF, axis=1)
        sc["kd0"][slot] = jnp.where(low, k, k_r).astype(BF16)
        sc["kd1"][slot] = jnp.where(low, k_r, k).astype(BF16)
        sc["vd0"][slot, :, 0:LANES] = jnp.where(low, v, 1.0).astype(BF16)
        sc["vd0"][slot, :, LANES:2 * LANES] = jnp.where(low, 1.0, v_r).astype(BF16)
        sc["vd1"][slot, :, 0:LANES] = jnp.where(low, v_r, 1.0).astype(BF16)
        sc["vd1"][slot, :, LANES:2 * LANES] = jnp.where(low, 1.0, v).astype(BF16)

    def stage_qr():
        qr = _dot(xn, win_ref[:, C_QKR:C_QKR + pw])
        sc["qrlo"][slot] = (qr * lo).astype(BF16)
        sc["qrhi"][slot] = (qr * hi).astype(BF16)

    def stage_kr():
        sc["kr"][slot] = _dot(xn, win_ref[:, C_QKR + pw:C_QKR + 2 * pw]) * (RET_QK_DIM ** -0.5)

    def stage_vr(i):
        cols = slice(i * pw, (i + 1) * pw)
        sc["vr"][slot, :, cols] = _dot(xn, win_ref[:, C_VR + i * pw:C_VR + (i + 1) * pw]).astype(BF16)

    def stage_gate(i):
        cols = slice(i * pw, (i + 1) * pw)
        sc["gate"][slot, :, cols] = _silu(_dot(xn, win_ref[:, C_GR + i * pw:C_GR + (i + 1) * pw]))

    part = functools.partial
    return [part(stage_q, 0), part(stage_q, 1), stage_kv, stage_qr, stage_kr,
            part(stage_vr, 0), part(stage_vr, 1), part(stage_gate, 0), part(stage_gate, 1)]


def _pm_last_block(slot, tm, sc):
    rows = slice(tm - BLK, tm)
    return ([sc["kd0"][slot, rows, :], sc["kd1"][slot, rows, :]],
            [sc["vd0"][slot, rows, :], sc["vd1"][slot, rows, :]])


def _pm_blocks(slot, prev_kd, prev_vd, is_first, state, fillers, tm, sinks_ref, gng_ref, gnb_ref,
               decay_ref, xi_ref, zeta_ref, sc):
    nblk = tm // BLK
    n_units = nblk * (N_KV_HEADS + N_RET_HEADS // 2)
    pending = list(fillers)
    done_units = [0]

    def unit_done():
        done_units[0] += 1
        while pending and (len(fillers) - len(pending)) * n_units < done_units[0] * len(fillers):
            pending.pop(0)()
    lowb = lax.broadcasted_iota(jnp.int32, (BLK, LANES), 1) < HALF
    col = lax.broadcasted_iota(jnp.int32, (BLK, 2 * BLK), 1)
    first_mask = None if is_first is False else jnp.where((col < BLK) & is_first, NEG, 0.0)
    kd_refs = (sc["kd0"], sc["kd1"])
    vd_refs = (sc["vd0"], sc["vd1"])
    qlo, qhi, mix = sc["qlo"], sc["qhi"], sc["mix"]
    n_pairs = N_RET_HEADS // 2

    for j in range(nblk):
        rows = slice(j * BLK, (j + 1) * BLK)
        c0s = [kvh * KV_GROUP * HEAD_DIM for kvh in range(N_KV_HEADS)]
        lsls = [slice(i * LANES, (i + 1) * LANES) for i in range(n_pairs)]
        vds, scores = [], []
        for kvh in range(N_KV_HEADS):
            if j == 0:
                kd = jnp.concatenate([prev_kd[kvh], kd_refs[kvh][slot, rows, :]], axis=0)
                vds.append(jnp.concatenate([prev_vd[kvh], vd_refs[kvh][slot, rows, :]], axis=0))
            else:
                krows = slice((j - 1) * BLK, (j + 1) * BLK)
                kd = kd_refs[kvh][slot, krows, :]
                vds.append(vd_refs[kvh][slot, krows, :])
            c0 = c0s[kvh]
            qst = jnp.concatenate([qlo[slot, rows, c0:c0 + LANES], qhi[slot, rows, c0:c0 + LANES],
                                   qlo[slot, rows, c0 + LANES:c0 + 2 * LANES],
                                   qhi[slot, rows, c0 + LANES:c0 + 2 * LANES]], axis=0)
            scores.append(_dot_nt(qst, kd))
        unit_done()

        kps = [sc["kr"][slot, rows, lsls[i]] for i in range(n_pairs)]
        vpairs = [sc["vr"][slot, rows, 2 * i * RET_V_DIM:(2 * i + 2) * RET_V_DIM] for i in range(n_pairs)]
        q2s = [jnp.concatenate([sc["qrlo"][slot, rows, lsls[i]], sc["qrhi"][slot, rows, lsls[i]]], axis=0)
               for i in range(n_pairs)]
        a_s = [_dot_nt(q2s[i], kps[i].astype(BF16)) for i in range(n_pairs)]
        ocs = [_dot(q2s[i], state[i].astype(BF16)) for i in range(n_pairs)]
        us = [_dot_tn((kps[i] * zeta_ref[:, lsls[i]]).astype(BF16), vpairs[i]) for i in range(n_pairs)]
        unit_done()

        for kvh in range(N_KV_HEADS):
            s, vd, c0 = scores[kvh], vds[kvh], c0s[kvh]
            es, esink = [], []
            for g in range(KV_GROUP):
                h = kvh * KV_GROUP + g
                sg = s[g * BLK:(g + 1) * BLK] + sc["bias"][h]
                if j == 0 and first_mask is not None:
                    sg = sg + first_mask
                sink = sinks_ref[h]
                m = jnp.maximum(jnp.max(sg, axis=-1, keepdims=True), sink)
                es.append(jnp.exp(sg - m).astype(BF16))
                esink.append(jnp.exp(sink - m))
            o = _dot(jnp.concatenate(es, axis=0), vd)
            for pair in range(KV_GROUP // 2):
                oe = o[2 * pair * BLK:(2 * pair + 1) * BLK]
                oo = o[(2 * pair + 1) * BLK:(2 * pair + 2) * BLK]
                num = jnp.where(lowb, oe[:, :LANES], oo[:, LANES:])
                den = (jnp.where(lowb, oe[:, LANES:], oo[:, :LANES])
                       + jnp.where(lowb, esink[2 * pair], esink[2 * pair + 1]))
                cs = c0 + pair * LANES
                mix[slot, rows, cs:cs + LANES] = (num * (1.0 / den)).astype(BF16)
            unit_done()

        for i in range(n_pairs):
            a, oc, u, sp = a_s[i], ocs[i], us[i], state[i]
            inner = jnp.concatenate([a[:BLK] * decay_ref[2 * i], a[BLK:] * decay_ref[2 * i + 1]], axis=0)
            oi = _dot(inner.astype(BF16), vpairs[i])
            for half in range(2):
                h = 2 * i + half
                vsl = slice(h * RET_V_DIM, (h + 1) * RET_V_DIM)
                hr = slice(half * BLK, (half + 1) * BLK)
                o = oi[hr, half * RET_V_DIM:(half + 1) * RET_V_DIM] + oc[hr] * xi_ref[:, vsl]
                r = _group_norm(o, gng_ref[:, vsl], gnb_ref[:, vsl]) * sc["gate"][slot, rows, vsl]
                mix[slot, rows, ATT_Q_W + h * RET_V_DIM:ATT_Q_W + (h + 1) * RET_V_DIM] = r.astype(BF16)
            state[i] = jnp.concatenate(
                [_GL_PROMPT[2 * i] * sp[:RET_QK_DIM] + u[:RET_QK_DIM, :RET_V_DIM],
                 _GL_PROMPT[2 * i + 1] * sp[RET_QK_DIM:] + u[RET_QK_DIM:, RET_V_DIM:]], axis=0)

    assert not pending
    return state


def _pm_wout_pieces(slot, x_ref, rows, wout_ref, h_ref, sc):
    pw = 2 * LANES
    n = D_MODEL // pw
    parts = []

    def piece(k):
        parts.append(_dot(sc["mix"][slot], wout_ref[:, k * pw:(k + 1) * pw]))
        if k == n - 1:
            h_ref[rows, :] = x_ref[rows, :] + jnp.concatenate(parts, axis=1)

    return [functools.partial(piece, k) for k in range(n)]


def _prompt_mixer_kernel(sinks_ref, xpair_ref, xnext_ref, gmix_ref, win_ref, wout_ref, gng_ref, gnb_ref,
                           dist_ref, mask_ref, decay_ref, xi_ref, zeta_ref,
                           wupf_ref, wdnf_ref, wqf_ref, wof_ref,
                           h_ref, wk_ref, wv_ref, st_ref,
                           wupb_ref, wdnb_ref, wqb_ref, wob_ref,
                           qlo_s, qhi_s, kd0_s, kd1_s, vd0_s, vd1_s,
                           qrlo_s, qrhi_s, kr_s, vr_s, gate_s, mix_s, bias_s, state_s):
    u = pl.program_id(0)
    tm = xnext_ref.shape[0]
    wupb_ref[...] = wupf_ref[...].astype(BF16)
    wdnb_ref[...] = wdnf_ref[...].astype(BF16)
    wqb_ref[...] = wqf_ref[...].astype(BF16)
    wob_ref[...] = wof_ref[...].astype(BF16)
    sc = dict(qlo=qlo_s, qhi=qhi_s, kd0=kd0_s, kd1=kd1_s, vd0=vd0_s, vd1=vd1_s, qrlo=qrlo_s, qrhi=qrhi_s,
              kr=kr_s, vr=vr_s, gate=gate_s, mix=mix_s, bias=bias_s)
    n_pairs = N_RET_HEADS // 2
    blocks = functools.partial(_pm_blocks, tm=tm, sinks_ref=sinks_ref, gng_ref=gng_ref,
                               gnb_ref=gnb_ref, decay_ref=decay_ref, xi_ref=xi_ref, zeta_ref=zeta_ref, sc=sc)

    @pl.when(u == 0)
    def _():
        for h in range(N_ATT_HEADS):
            bias_s[h] = NEG_SLOPES[h] * dist_ref[...] + mask_ref[...]
        state_s[...] = jnp.zeros_like(state_s)
        kd0_s[1] = jnp.zeros(kd0_s.shape[1:], BF16)
        kd1_s[1] = jnp.zeros(kd1_s.shape[1:], BF16)
        vd0_s[1] = jnp.zeros(vd0_s.shape[1:], BF16)
        vd1_s[1] = jnp.zeros(vd1_s.shape[1:], BF16)
        for stage in _pm_project_stages(xpair_ref[0:tm, :], 0, gmix_ref, win_ref, sc):
            stage()

    seq_start = (u % 2) == 0
    state = [jnp.where(seq_start, 0.0, state_s[i * LANES:(i + 1) * LANES, :]) for i in range(n_pairs)]

    prev_kd, prev_vd = _pm_last_block(1, tm, sc)
    stages = _pm_project_stages(xpair_ref[tm:2 * tm, :], 1, gmix_ref, win_ref, sc, kv_out=(wk_ref, wv_ref))
    state = blocks(0, prev_kd, prev_vd, seq_start, state, stages)
    wout0 = _pm_wout_pieces(0, xpair_ref, slice(0, tm), wout_ref, h_ref, sc)

    prev_kd, prev_vd = _pm_last_block(0, tm, sc)
    stages = _pm_project_stages(xnext_ref[...], 0, gmix_ref, win_ref, sc)
    state = blocks(1, prev_kd, prev_vd, False, state, wout0 + stages)
    for piece in _pm_wout_pieces(1, xpair_ref, slice(tm, 2 * tm), wout_ref, h_ref, sc):
        piece()

    for i in range(n_pairs):
        state_s[i * LANES:(i + 1) * LANES, :] = state[i]
        st_ref[0, i * LANES:(i + 1) * LANES, :] = state[i]


def _prompt_mixer(x, g_mix, w_in, w_out, sinks, gn_g, gn_b, side_f32):
    b, s, d = x.shape
    tm = TM_MIX
    n_tiles = b * s // tm
    steps = n_tiles // 2
    seq_steps = s // (2 * tm)
    assert s % (2 * tm) == 0 and seq_steps == 2, "kernel assumes 4 tiles per sequence"
    x2d = x.reshape(b * s, d)
    const = lambda shape: pl.BlockSpec(shape, lambda i: (0,) * len(shape), pipeline_mode=pl.Buffered(1))
    slot2 = lambda rows, cols, dt: pltpu.VMEM((2, rows, cols), dt)
    side_specs = [pl.BlockSpec((w.shape[0] // steps, w.shape[1]), lambda i: (i, 0)) for w in side_f32]
    outs = pl.pallas_call(
        _prompt_mixer_kernel,
        grid=(steps,),
        in_specs=[
            pl.BlockSpec(memory_space=pltpu.SMEM),
            pl.BlockSpec((2 * tm, d), lambda i: (i, 0)),
            pl.BlockSpec((tm, d), lambda i: (jnp.minimum(2 * i + 2, n_tiles - 1), 0)),
            const((1, d)), const((d, D_IN)), const((MIX_OUT, d)),
            const((1, RET_V_W)), const((1, RET_V_W)),
            const((BLK, 2 * BLK)), const((BLK, 2 * BLK)),
            const((N_RET_HEADS, BLK, BLK)), const((BLK, RET_V_W)), const((BLK, RET_QK_W)),
        ] + side_specs,
        out_specs=[
            pl.BlockSpec((2 * tm, d), lambda i: (i, 0)),
            pl.BlockSpec((1, WINDOW, ATT_KV_W), lambda i: (i // seq_steps, 0, 0)),
            pl.BlockSpec((1, WINDOW, ATT_KV_W), lambda i: (i // seq_steps, 0, 0)),
            pl.BlockSpec((1, RET_QK_W, RET_V_DIM), lambda i: (i // seq_steps, 0, 0)),
        ] + side_specs,
        out_shape=[
            jax.ShapeDtypeStruct((b * s, d), F32),
            jax.ShapeDtypeStruct((b, WINDOW, ATT_KV_W), F32),
            jax.ShapeDtypeStruct((b, WINDOW, ATT_KV_W), F32),
            jax.ShapeDtypeStruct((b, RET_QK_W, RET_V_DIM), F32),
        ] + [jax.ShapeDtypeStruct(w.shape, BF16) for w in side_f32],
        scratch_shapes=[
            slot2(tm, ATT_Q_W, BF16), slot2(tm, ATT_Q_W, BF16),
            slot2(tm, LANES, BF16), slot2(tm, LANES, BF16),
            slot2(tm, 2 * LANES, BF16), slot2(tm, 2 * LANES, BF16),
            slot2(tm, RET_QK_W, BF16), slot2(tm, RET_QK_W, BF16),
            slot2(tm, RET_QK_W, F32), slot2(tm, RET_V_W, BF16),
            slot2(tm, RET_V_W, F32), slot2(tm, MIX_OUT, BF16),
            pltpu.VMEM((N_ATT_HEADS, BLK, 2 * BLK), F32),
            pltpu.VMEM((RET_QK_W, RET_V_DIM), F32),
        ],
        compiler_params=pltpu.CompilerParams(
            dimension_semantics=("arbitrary",), vmem_limit_bytes=VMEM_LIMIT),
        name="prompt_mixer",
    )(sinks, x2d, x2d, g_mix, w_in, w_out, gn_g, gn_b,
      jnp.asarray(_P_DIST), jnp.asarray(_P_MASK), jnp.asarray(_P_DECAY), jnp.asarray(_P_XI),
      jnp.asarray(_P_ZETA), *side_f32)
    return (outs[0].reshape(b, s, d),) + tuple(outs[1:])


def _memkv_kernel(mem_ref, g_ref, wk_ref, wv_ref, win_ref, wout_ref,
                  mk_ref, mv_ref, mkb_ref, mvb_ref, winb_ref, woutb_ref):
    winb_ref[...] = win_ref[...].astype(BF16)
    woutb_ref[...] = wout_ref[...].astype(BF16)
    mn = _rms(mem_ref[...], g_ref[...]).astype(BF16)
    mk = _dot(mn, wk_ref[...].astype(BF16))
    mv = _dot(mn, wv_ref[...].astype(BF16))
    tm = mem_ref.shape[0]
    group = X_D_HALVES * N_X_HEADS
    for hd in range(N_X_HEADS):
        for dh in range(X_D_HALVES):
            cols = slice(hd * X_HEAD_DIM + dh * LANES, hd * X_HEAD_DIM + (dh + 1) * LANES)
            rows = pl.ds(dh * N_X_HEADS + hd, tm, stride=group)
            mk_ref[rows, :] = mk[:, cols]
            mv_ref[rows, :] = mv[:, cols]
    mkb_ref[...] = mk.astype(BF16)
    mvb_ref[...] = mv.astype(BF16)


def _memory_kv(mem2d, g_mem, w_xk, w_xv, w_in, w_out):
    n, d = mem2d.shape
    tm = 512
    row = pl.BlockSpec((tm, d), lambda i: (i, 0))
    rows_out = pl.BlockSpec((tm * d // LANES, LANES), lambda i: (i, 0))
    const = lambda shape: pl.BlockSpec(shape, lambda i: (0,) * len(shape), pipeline_mode=pl.Buffered(1))
    steps = n // tm
    win_blk = pl.BlockSpec((w_in.shape[0] // steps, w_in.shape[1]), lambda i: (i, 0))
    wout_blk = pl.BlockSpec((w_out.shape[0] // steps, w_out.shape[1]), lambda i: (i, 0))
    return pl.pallas_call(
        _memkv_kernel,
        grid=(n // tm,),
        in_specs=[row, const((1, d)), const((d, d)), const((d, d)), win_blk, wout_blk],
        out_specs=[rows_out, rows_out, row, row, win_blk, wout_blk],
        out_shape=[jax.ShapeDtypeStruct((n * d // LANES, LANES), F32),
                   jax.ShapeDtypeStruct((n * d // LANES, LANES), F32),
                   jax.ShapeDtypeStruct((n, d), BF16), jax.ShapeDtypeStruct((n, d), BF16),
                   jax.ShapeDtypeStruct(w_in.shape, BF16), jax.ShapeDtypeStruct(w_out.shape, BF16)],
        compiler_params=pltpu.CompilerParams(
            dimension_semantics=("arbitrary",), vmem_limit_bytes=VMEM_LIMIT),
        name="memory_kv",
    )(mem2d, g_mem, w_xk, w_xv, w_in, w_out)


def _prompt_xattn_kernel(h_ref, g_ref, wq_ref, wo_ref, mk_ref, mv_ref, out_ref, o_s):
    def stages(r0):
        rows = slice(r0, r0 + SUB_ROWS)
        env = {}

        def project():
            env["h"] = h_ref[0, rows, :]
            xn = _rms(env["h"], g_ref[...]).astype(BF16)
            env["q"] = (_dot(xn, wq_ref[...]) * (X_HEAD_DIM ** -0.5)).astype(BF16)

        def head(hd):
            sl = slice(hd * X_HEAD_DIM, (hd + 1) * X_HEAD_DIM)
            s = _dot_nt(env["q"][:, sl], mk_ref[0, :, sl])
            m = jnp.max(s, axis=-1, keepdims=True)
            p = jnp.exp(s - m)
            p = p * (1.0 / jnp.sum(p, axis=-1, keepdims=True))
            o_s[rows, sl] = _dot(p.astype(BF16), mv_ref[0, :, sl]).astype(BF16)

        def output():
            out_ref[0, rows, :] = env["h"] + _dot(o_s[rows, :], wo_ref[...])

        return [project] + [functools.partial(head, hd) for hd in range(N_X_HEADS)] + [output]

    chains = [stages(r0) for r0 in range(0, h_ref.shape[1], SUB_ROWS)]
    n_stage = len(chains[0])
    for step in range(n_stage + len(chains) - 1):
        for lag, chain in enumerate(chains):
            if 0 <= step - lag < n_stage:
                chain[step - lag]()


def _prompt_xattn(h, g, w_xq, w_xo, mkb, mvb):
    b, s, d = h.shape
    tm = TM_X
    const = lambda shape: pl.BlockSpec(shape, lambda i, j: (0,) * len(shape))
    tok = pl.BlockSpec((1, tm, d), lambda i, j: (i, j, 0))
    mem = pl.BlockSpec((1, N_MEM, d), lambda i, j: (i, 0, 0))
    return pl.pallas_call(
        _prompt_xattn_kernel,
        grid=(b, s // tm),
        in_specs=[tok, const((1, d)), const((d, d)), const((d, d)), mem, mem],
        out_specs=tok,
        out_shape=jax.ShapeDtypeStruct((b, s, d), F32),
        scratch_shapes=[pltpu.VMEM((tm, d), BF16)],
        compiler_params=pltpu.CompilerParams(
            dimension_semantics=("arbitrary", "arbitrary"), vmem_limit_bytes=VMEM_LIMIT),
        name="prompt_xattn",
    )(h, g, w_xq, w_xo, mkb, mvb)


def _sample_mixer_kernel(sinks_ref, x_ref, gmix_ref, win_ref, wout_ref, gng_ref, gnb_ref,
                         ck_ref, cv_ref, st_ref, bias_ref, dec_ref, xi_ref, zeta_ref,
                         h_ref, swk_ref, swv_ref, sst_ref):
    bb = ck_ref.shape[0]
    nt = bb // 2
    x = x_ref[...]
    xn = _rms(x, gmix_ref[...]).astype(BF16)
    tile3 = lambda a: a.reshape(nt, SUBLANES, a.shape[-1])

    q = _dot(xn, win_ref[:, C_QA:C_QA + ATT_Q_W]) * (HEAD_DIM ** -0.5)
    kv = _dot(xn, win_ref[:, C_KV:C_KV + 2 * ATT_KV_W])
    qkr = _dot(xn, win_ref[:, C_QKR:C_QKR + 2 * RET_QK_W])
    vr = _dot(xn, win_ref[:, C_VR:C_VR + RET_V_W])
    gate3 = tile3(_silu(_dot(xn, win_ref[:, C_GR:C_GR + RET_V_W])))

    lo512, hi512 = _half_masks(ATT_Q_W)
    q_r = pltpu.roll(q, HALF, axis=1)
    q_nat3 = tile3(q)
    q_rot3 = tile3(q_r)
    lo3 = lo512.reshape(1, 1, ATT_Q_W)
    hi3 = hi512.reshape(1, 1, ATT_Q_W)
    qa3 = (q_nat3 * lo3).astype(BF16)
    qb3 = (q_rot3 * lo3).astype(BF16)
    qc3 = (q_rot3 * hi3).astype(BF16)
    qd3 = (q_nat3 * hi3).astype(BF16)
    t128 = lambda a, i: a[:, :, i * LANES:(i + 1) * LANES]
    qs = jnp.concatenate([t128(qa3, 0), t128(qb3, 1), t128(qa3, 1), t128(qb3, 2),
                          t128(qc3, 2), t128(qd3, 2), t128(qc3, 3), t128(qd3, 3)], axis=1)

    k3 = tile3(kv[:, :ATT_KV_W])
    v3 = tile3(kv[:, ATT_KV_W:])
    pad_kv = jnp.zeros((nt, BLK - SUBLANES, LANES), BF16)
    knew_pad = jnp.concatenate([k3.astype(BF16), pad_kv], axis=1)
    vnew_pad = jnp.concatenate([v3.astype(BF16), pad_kv], axis=1)
    to_lanes = lambda a3: jnp.swapaxes(
        jnp.concatenate([a3, jnp.zeros((nt, BLK - SUBLANES, LANES), F32)], axis=1), 1, 2)
    k3t, v3t = to_lanes(k3), to_lanes(v3)
    roll3 = lambda a, sh: pltpu.roll(a.reshape(nt * BLK, LANES), sh, axis=1).reshape(nt, BLK, LANES)

    lo256, _ = _half_masks(RET_QK_W)
    qr3 = tile3(qkr[:, :RET_QK_W])
    kr3 = tile3(qkr[:, RET_QK_W:] * (RET_QK_DIM ** -0.5))
    vr3 = tile3(vr)
    lane256 = lax.broadcasted_iota(jnp.int32, (1, 1, RET_QK_W), 2)
    qrs = jnp.concatenate(
        [(qr3 * ((lane256 >= h * RET_QK_DIM) & (lane256 < (h + 1) * RET_QK_DIM)).astype(F32)).astype(BF16)
         for h in range(N_RET_HEADS)],
        axis=1)
    kr_pad = jnp.concatenate([kr3.astype(BF16), jnp.zeros((nt, BLK - SUBLANES, RET_QK_W), BF16)], axis=1)
    vr_pad = jnp.concatenate([vr3.astype(BF16), jnp.zeros((nt, BLK - SUBLANES, RET_V_W), BF16)], axis=1)

    lane = lax.broadcasted_iota(jnp.int32, (1, 1, LANES), 2)
    row8 = lax.broadcasted_iota(jnp.int32, (1, SUBLANES, 1), 1)
    bmm_nt = lambda a, b: jnp.einsum('bqd,bkd->bqk', a, b, preferred_element_type=F32)
    bmm = lambda a, b: jnp.einsum('bqk,bkd->bqd', a, b, preferred_element_type=F32)

    att_par, ret_par = [], []
    for par in range(2):
        bsl = pl.ds(par, nt, stride=2)
        ckt = ck_ref[bsl]
        cvt = cv_ref[bsl]
        keep = lane < WINDOW - DEC_SEQ
        new_shift = WINDOW - DEC_SEQ - DEC_SEQ * par
        swk_ref[bsl] = jnp.where(keep, roll3(ckt, WINDOW - DEC_SEQ), roll3(k3t, new_shift))
        swv_ref[bsl] = jnp.where(keep, roll3(cvt, WINDOW - DEC_SEQ), roll3(v3t, new_shift))

        s = jnp.concatenate([bmm(qs, ckt.astype(BF16)), bmm_nt(qs, knew_pad)], axis=2) + bias_ref[par]
        ps = []
        for h in range(N_ATT_HEADS):
            ps.append(_sink_softmax(s[:, h * SUBLANES:(h + 1) * SUBLANES, :], sinks_ref[h]).astype(BF16))
        p_all = jnp.concatenate(ps, axis=1)
        o = bmm_nt(p_all[:, :, :BLK], cvt.astype(BF16)) + bmm(p_all[:, :, BLK:], vnew_pad)
        o_r = pltpu.roll(o.reshape(nt * N_ATT_HEADS * SUBLANES, LANES), HALF, axis=1).reshape(o.shape)
        hr = lambda a, h: a[:, h * SUBLANES:(h + 1) * SUBLANES, :]
        low = lane < HALF
        att_par.append(jnp.concatenate([
            jnp.where(low, hr(o, 0), hr(o_r, 1)), jnp.where(low, hr(o, 2), hr(o_r, 3)),
            jnp.where(low, hr(o_r, 4), hr(o, 5)), jnp.where(low, hr(o_r, 6), hr(o, 7))], axis=2))

        st = st_ref[bsl]
        oc = bmm(qrs, st.astype(BF16))
        inner = (bmm_nt(qrs, kr_pad) * dec_ref[par]).astype(BF16)
        oi = bmm(inner, vr_pad)
        rs = []
        for h in range(N_RET_HEADS):
            vsl = slice(h * RET_V_DIM, (h + 1) * RET_V_DIM)
            rsl = slice(h * SUBLANES, (h + 1) * SUBLANES)
            o_h = oi[:, rsl, vsl] + oc[:, rsl, :] * xi_ref[par, rsl, :]
            rs.append(_group_norm(o_h, gng_ref[:, vsl], gnb_ref[:, vsl]) * gate3[:, :, vsl])
        ret_par.append(jnp.concatenate(rs, axis=2))

        kz3 = (kr3 * zeta_ref[par]).astype(BF16)
        vr3_b = vr3.astype(BF16)
        for p in range(nt):
            for i in range(N_RET_HEADS // 2):
                u = _dot_tn(kz3[p][:, i * LANES:(i + 1) * LANES],
                            vr3_b[p][:, 2 * i * RET_V_DIM:(2 * i + 2) * RET_V_DIM])
                for half in range(2):
                    h = 2 * i + half
                    dsl = slice(h * RET_QK_DIM, (h + 1) * RET_QK_DIM)
                    sst_ref[2 * p + par, dsl, :] = (
                        _GL_SAMPLE[h] * st[p, dsl, :]
                        + u[half * RET_QK_DIM:(half + 1) * RET_QK_DIM, half * RET_V_DIM:(half + 1) * RET_V_DIM])

    own0 = row8 < DEC_SEQ
    att3 = jnp.where(own0, att_par[0], att_par[1])
    ret3 = jnp.where(own0, ret_par[0], ret_par[1])
    mix = jnp.concatenate([att3, ret3], axis=2).reshape(2 * nt * DEC_SEQ, MIX_OUT).astype(BF16)
    h_ref[...] = x + _dot(mix, wout_ref[...])


def _sample_mixer(x2d, g_mix, w_in, w_out, sinks, gn_g, gn_b, ck, cv, st):
    n, d = x2d.shape
    nb = ck.shape[0]
    bb = BB_MIX
    r = bb * DEC_SEQ
    const = lambda shape: pl.BlockSpec(shape, lambda i: (0,) * len(shape))
    row = pl.BlockSpec((r, d), lambda i: (i, 0))
    win = pl.BlockSpec((bb, WINDOW, ATT_KV_W), lambda i: (i, 0, 0))
    state = pl.BlockSpec((bb, RET_QK_W, RET_V_DIM), lambda i: (i, 0, 0))
    return pl.pallas_call(
        _sample_mixer_kernel,
        grid=(nb // bb,),
        in_specs=[
            pl.BlockSpec(memory_space=pltpu.SMEM),
            row, const((1, d)), const((d, D_IN)), const((MIX_OUT, d)),
            const((1, RET_V_W)), const((1, RET_V_W)),
            win, win, state,
            const(_S_BIAS.shape), const(_S_DEC.shape), const(_S_XI.shape), const(_S_ZETA.shape),
        ],
        out_specs=[row, win, win, state],
        out_shape=[
            jax.ShapeDtypeStruct((n, d), F32),
            jax.ShapeDtypeStruct((nb, WINDOW, ATT_KV_W), F32),
            jax.ShapeDtypeStruct((nb, WINDOW, ATT_KV_W), F32),
            jax.ShapeDtypeStruct((nb, RET_QK_W, RET_V_DIM), F32),
        ],
        compiler_params=pltpu.CompilerParams(
            dimension_semantics=("arbitrary",), vmem_limit_bytes=VMEM_LIMIT),
        name="sample_mixer",
    )(sinks, x2d, g_mix, w_in, w_out, gn_g, gn_b, ck, cv, st,
      jnp.asarray(_S_BIAS), jnp.asarray(_S_DEC), jnp.asarray(_S_XI), jnp.asarray(_S_ZETA))


def _head_slab(x_ref, b, hd):
    group = X_D_HALVES * N_X_HEADS
    halves = [x_ref[b, pl.ds(dh * N_X_HEADS + hd, N_MEM, stride=group), :] for dh in range(X_D_HALVES)]
    return jnp.concatenate(halves, axis=1).astype(BF16)


def _mlp_value(h, g_ref, wup_ref, wdn_ref, gf_ref, fillers=None):
    xn = _rms(h, g_ref[...]).astype(BF16)
    acc = h
    piece = FF_CHUNK // N_X_HEADS
    opiece = D_MODEL // N_X_HEADS
    for c in range(D_FF // FF_CHUNK):
        qk, softmax, pv = fillers[c] if fillers is not None else (None, None, None)
        hid = []
        for k in range(N_X_HEADS):
            cols = slice(c * FF_CHUNK + k * piece, c * FF_CHUNK + (k + 1) * piece)
            u = jnp.maximum(_dot(xn, wup_ref[:, cols]), 0.0)
            hid.append((u * u).astype(BF16))
            if qk is not None:
                qk(k)
        if softmax is not None:
            softmax()
        hid = jnp.concatenate(hid, axis=1)
        rows_c = slice(c * FF_CHUNK, (c + 1) * FF_CHUNK)
        out = []
        for k in range(N_X_HEADS):
            out.append(_dot(hid, wdn_ref[rows_c, k * opiece:(k + 1) * opiece]))
            if pv is not None:
                pv(k)
        acc = acc + jnp.concatenate(out, axis=1)
    return _rms(acc, gf_ref[...])


def _mlp_xattn_kernel(hp_ref, hsm_ref, gx_ref, wq_ref, wo_ref, xk_ref, xv_ref, g_ref, wup_ref, wdn_ref, gf_ref,
                      yp_ref, ys_ref):
    i = pl.program_id(0)
    n = pl.num_programs(0) - 1
    bb = xk_ref.shape[0]
    rows = bb * DEC_SEQ
    assert bb == D_FF // FF_CHUNK and bb % 2 == 0

    @pl.when(i == 0)
    def _():
        xn = _rms(hsm_ref[...], gx_ref[...]).astype(BF16)
        ys_ref[...] = _dot(xn, wq_ref[...]) * (X_HEAD_DIM ** -0.5)

    @pl.when(i < n)
    def _():
        r0 = pl.multiple_of(i * rows, rows)
        own0 = lax.broadcasted_iota(jnp.int32, (SUBLANES, 1), 0) < DEC_SEQ
        o_rows = {}

        def attend(b):
            t = b // 2
            tile_rows = pl.ds(r0 + t * SUBLANES, SUBLANES)
            env = dict(s=[], o=[])

            def qk(hd):
                if hd == 0:
                    env["q"] = ys_ref[tile_rows, :].astype(BF16)
                env["s"].append(_dot_nt(env["q"][:, hd * X_HEAD_DIM:(hd + 1) * X_HEAD_DIM],
                                        _head_slab(xk_ref, b, hd)))

            def softmax():
                s = jnp.concatenate(env["s"], axis=0)
                m = jnp.max(s, axis=-1, keepdims=True)
                p = jnp.exp(s - m)
                env["p"] = p * (1.0 / jnp.sum(p, axis=-1, keepdims=True))

            def pv(hd):
                p = env["p"][hd * SUBLANES:(hd + 1) * SUBLANES].astype(BF16)
                env["o"].append(_dot(p, _head_slab(xv_ref, b, hd)))
                if hd == N_X_HEADS - 1:
                    o_rows[b] = jnp.concatenate(env["o"], axis=1)
                    if b % 2 == 1:
                        ys_ref[tile_rows, :] = jnp.where(own0, o_rows[b - 1], o_rows[b])

            return qk, softmax, pv

        fillers = [attend(b) for b in range(bb)]
        yp_ref[...] = _mlp_value(hp_ref[...], g_ref, wup_ref, wdn_ref, gf_ref, fillers)

    @pl.when(i == n)
    def _():
        hs = hsm_ref[...] + _dot(ys_ref[...].astype(BF16), wo_ref[...])
        ys_ref[...] = _mlp_value(hs, g_ref, wup_ref, wdn_ref, gf_ref)


def _mlp_xattn(hp2d, hsm, g_xattn, w_xq, w_xo, xk, xv, g_mlp, w_up, w_down, g_final):
    n, d = hp2d.shape
    ns = hsm.shape[0]
    nb = xk.shape[0]
    bb = BB_X
    tm = n // (nb // bb)
    n_tiles = n // tm
    assert n_tiles * bb == nb and tm % SUBLANES == 0
    clip = lambda i: jnp.minimum(i, n_tiles - 1)
    prompt = pl.BlockSpec((tm, d), lambda i: (clip(i), 0))
    mem = pl.BlockSpec((bb,) + xk.shape[1:], lambda i: (clip(i), 0, 0))
    const = lambda shape: pl.BlockSpec(shape, lambda i: (0,) * len(shape), pipeline_mode=pl.Buffered(1))
    return pl.pallas_call(
        _mlp_xattn_kernel,
        grid=(n_tiles + 1,),
        in_specs=[prompt, const((ns, d)), const((1, d)), const((d, d)), const((d, d)), mem, mem,
                  const((1, d)), const((d, D_FF)), const((D_FF, d)), const((1, d))],
        out_specs=[prompt, pl.BlockSpec((ns, d), lambda i: (0, 0))],
        out_shape=[jax.ShapeDtypeStruct((n, d), F32), jax.ShapeDtypeStruct((ns, d), F32)],
        compiler_params=pltpu.CompilerParams(
            dimension_semantics=("arbitrary",), vmem_limit_bytes=VMEM_LIMIT),
        name="mlp_xattn",
    )(hp2d, hsm, g_xattn, w_xq, w_xo, xk, xv, g_mlp, w_up, w_down, g_final)


def _mem_rows(c):
    nb = c.shape[0]
    c = c.reshape(nb, N_MEM, N_X_HEADS, X_D_HALVES, LANES)
    return jnp.transpose(c, (0, 1, 3, 2, 4)).reshape(nb, N_MEM * X_D_HALVES * N_X_HEADS, LANES)


def kernel(x_prompt, x_sample, mem_prompt, cache_win_k, cache_win_v, state_ret, cache_mem_k, cache_mem_v,
           g_mix, w_in, attn_sinks, ret_gn_g, ret_gn_b, w_out, g_xattn, g_mem, w_xq, w_xk, w_xv, w_xo,
           g_mlp, w_up, w_down, g_final):
    depth = w_in.shape[0]
    assert depth == 1, "single-layer trunk"
    b, s, d = x_prompt.shape
    nb, ls, _ = x_sample.shape
    row = lambda a: a.reshape(1, -1)
    sinks = attn_sinks[0]
    gn_g, gn_b = row(ret_gn_g[0]), row(ret_gn_b[0])
    g_fin = row(g_final)

    mk, mv, mkb, mvb, w_in_b, w_out_b = _memory_kv(
        mem_prompt.reshape(b * N_MEM, d), row(g_mem[0]), w_xk[0], w_xv[0], w_in[0], w_out[0])
    hp, p_wk, p_wv, p_rs, w_up_b, w_dn_b, w_xq_b, w_xo_b = _prompt_mixer(
        x_prompt, row(g_mix[0]), w_in_b, w_out_b, sinks, gn_g, gn_b,
        (w_up[0], w_down[0], w_xq[0], w_xo[0]))
    hp = _prompt_xattn(hp, row(g_xattn[0]), w_xq_b, w_xo_b,
                       mkb.reshape(b, N_MEM, d), mvb.reshape(b, N_MEM, d))

    win_t = lambda c: jnp.transpose(c, (0, 2, 3, 1)).reshape(nb, ATT_KV_W, WINDOW)
    win_t_inv = lambda a: jnp.transpose(a.reshape(nb, N_KV_HEADS, HEAD_DIM, WINDOW),
                                        (0, 3, 1, 2)).reshape(1, nb, WINDOW, N_KV_HEADS, HEAD_DIM)
    hs, s_wk, s_wv, s_rs = _sample_mixer(
        x_sample.reshape(nb * ls, d), row(g_mix[0]), w_in_b, w_out_b, sinks, gn_g, gn_b,
        win_t(cache_win_k[0]), win_t(cache_win_v[0]), state_ret[0].reshape(nb, RET_QK_W, RET_V_DIM))

    y_prompt, y_sample = _mlp_xattn(
        hp.reshape(b * s, d), hs, row(g_xattn[0]), w_xq_b, w_xo_b,
        _mem_rows(cache_mem_k[0]), _mem_rows(cache_mem_v[0]), row(g_mlp[0]), w_up_b, w_dn_b, g_fin)
    y_prompt = y_prompt.reshape(b, s, d)
    y_sample = y_sample.reshape(nb, ls, d)

    win5 = lambda a, n: a.reshape(1, n, WINDOW, N_KV_HEADS, HEAD_DIM)
    ret5 = lambda a, n: a.reshape(1, n, N_RET_HEADS, RET_QK_DIM, RET_V_DIM)
    mem5 = lambda a: jnp.transpose(a.reshape(b, N_MEM, X_D_HALVES, N_X_HEADS, LANES),
                                   (0, 1, 3, 2, 4)).reshape(1, b, N_MEM, N_X_HEADS, X_HEAD_DIM)
    return (y_prompt, y_sample,
            win5(p_wk, b), win5(p_wv, b), ret5(p_rs, b), mem5(mk), mem5(mv),
            win_t_inv(s_wk), win_t_inv(s_wv), ret5(s_rs, nb))
```

```python
import functools

import jax
import jax.numpy as jnp
import numpy as np
from jax import lax
from jax.experimental import pallas as pl
from jax.experimental.pallas import tpu as pltpu

F32 = jnp.float32
BF16 = jnp.bfloat16

D_MODEL = 1024
BATCH = 8
SEQ = 2048
DEC_BATCH = 128
DEC_SEQ = 4
HEAD_DIM = 64
N_ATT_HEADS = 8
N_KV_HEADS = 2
KV_GROUP = N_ATT_HEADS // N_KV_HEADS
WINDOW = 128
BLK = 128
N_RET_HEADS = 4
RET_QK_DIM = 64
RET_V_DIM = 128
N_MEM = 256
N_X_HEADS = 4
X_HEAD_DIM = D_MODEL // N_X_HEADS
D_FF = 4 * D_MODEL
RMS_EPS = 1e-6
GN_EPS = 1e-5

ATT_Q_W = N_ATT_HEADS * HEAD_DIM
ATT_KV_W = N_KV_HEADS * HEAD_DIM
RET_QK_W = N_RET_HEADS * RET_QK_DIM
RET_V_W = N_RET_HEADS * RET_V_DIM
MIX_OUT = ATT_Q_W + RET_V_W
D_IN = ATT_Q_W + 2 * ATT_KV_W + 2 * RET_QK_W + 2 * RET_V_W
C_QA, C_KV, C_QKR, C_VR, C_GR = 0, 512, 768, 1280, 1792

LANES = 128
SUBLANES = 8
HALF = LANES // 2
X_D_HALVES = X_HEAD_DIM // LANES
NEG = -1e30
VMEM_LIMIT = 56 * 1024 * 1024

TM_MIX = 512
TM_X = 2048
SUB_ROWS = 512
FF_CHUNK = 1024
BB_MIX = 32
BB_X = 4

NEG_SLOPES = [-(2.0 ** (-8.0 * (i + 1) / N_ATT_HEADS)) for i in range(N_ATT_HEADS)]
_LOG_G = np.log(1.0 - 2.0 ** (-5.0 - np.arange(N_RET_HEADS))).astype(np.float32).astype(np.float64)


def _prompt_tables():
    qi = np.arange(BLK)[:, None]
    kj = np.arange(2 * BLK)[None, :]
    dist = (qi + BLK - kj).astype(np.float64)
    mask = np.where((dist >= 0) & (dist < WINDOW), 0.0, NEG)
    l = np.arange(BLK, dtype=np.float64)
    diff = l[:, None] - l[None, :]
    decay = np.where(diff >= 0, np.exp(_LOG_G[:, None, None] * np.maximum(diff, 0.0)), 0.0)
    xi = np.exp((l[:, None] + 1.0) * _LOG_G[None, :])
    zeta = np.exp((BLK - 1.0 - l)[:, None] * _LOG_G[None, :])
    xi_t = np.repeat(xi, RET_V_DIM, axis=1)
    zeta_t = np.repeat(zeta, RET_QK_DIM, axis=1)
    f = lambda a: np.asarray(a, np.float32)
    return f(dist), f(mask), f(decay), f(xi_t), f(zeta_t)


def _sample_tables():
    slopes = -np.asarray(NEG_SLOPES)
    bias = np.full((2, N_ATT_HEADS * SUBLANES, 2 * BLK), NEG, np.float64)
    dec = np.zeros((2, N_RET_HEADS * SUBLANES, BLK), np.float64)
    xi = np.zeros((2, N_RET_HEADS * SUBLANES, RET_V_DIM), np.float64)
    zeta = np.zeros((2, SUBLANES, RET_QK_W), np.float64)
    for par in range(2):
        for r in range(SUBLANES):
            own = DEC_SEQ * par <= r < DEC_SEQ * (par + 1)
            t = r - DEC_SEQ * par if own else r % DEC_SEQ
            for h in range(N_ATT_HEADS):
                row = h * SUBLANES + r
                for j in range(WINDOW):
                    d = t + WINDOW - j
                    if 0 <= d < WINDOW:
                        bias[par, row, j] = -slopes[h] * d
                for c in range(DEC_SEQ):
                    d = t - c
                    if d >= 0:
                        bias[par, row, WINDOW + DEC_SEQ * par + c] = -slopes[h] * d
            for h in range(N_RET_HEADS):
                row = h * SUBLANES + r
                if own:
                    xi[par, row, :] = np.exp((t + 1.0) * _LOG_G[h])
                    zeta[par, r, h * RET_QK_DIM:(h + 1) * RET_QK_DIM] = np.exp((DEC_SEQ - 1.0 - t) * _LOG_G[h])
                    for c in range(t + 1):
                        dec[par, row, DEC_SEQ * par + c] = np.exp(_LOG_G[h] * (t - c))
    f = lambda a: np.asarray(a, np.float32)
    return f(bias), f(dec), f(xi), f(zeta)


_P_DIST, _P_MASK, _P_DECAY, _P_XI, _P_ZETA = _prompt_tables()
_S_BIAS, _S_DEC, _S_XI, _S_ZETA = _sample_tables()
_GL_PROMPT = [float(np.exp(_LOG_G[h] * BLK)) for h in range(N_RET_HEADS)]
_GL_SAMPLE = [float(np.exp(_LOG_G[h] * DEC_SEQ)) for h in range(N_RET_HEADS)]


def _rms(x, g):
    return x * lax.rsqrt(jnp.mean(x * x, axis=-1, keepdims=True) + RMS_EPS) * g


def _dot(a, b):
    return jnp.dot(a, b, preferred_element_type=F32)


def _dot_nt(a, b):
    return lax.dot_general(a, b, (((1,), (1,)), ((), ())), preferred_element_type=F32)


def _dot_tn(a, b):
    return lax.dot_general(a, b, (((0,), (0,)), ((), ())), preferred_element_type=F32)


def _silu(g):
    return g * (1.0 / (1.0 + jnp.exp(-g)))


def _half_masks(width):
    lane = lax.broadcasted_iota(jnp.int32, (1, width), 1)
    lo = ((lane & (LANES - 1)) < HALF).astype(F32)
    return lo, 1.0 - lo


def _sink_softmax(s, sink):
    m = jnp.maximum(jnp.max(s, axis=-1, keepdims=True), sink)
    p = jnp.exp(s - m)
    den = jnp.sum(p, axis=-1, keepdims=True) + jnp.exp(sink - m)
    return p * (1.0 / den)


def _group_norm(o, g, b):
    mu = jnp.mean(o, axis=-1, keepdims=True)
    d = o - mu
    var = jnp.mean(d * d, axis=-1, keepdims=True)
    return d * lax.rsqrt(var + GN_EPS) * g + b


def _pm_project_stages(x, slot, gmix_ref, win_ref, sc, kv_out=None):
    tm = x.shape[0]
    xn = _rms(x, gmix_ref[...]).astype(BF16)

    pw = 2 * LANES
    lo, hi = _half_masks(pw)

    def stage_q(i):
        cols = slice(i * pw, (i + 1) * pw)
        q = _dot(xn, win_ref[:, C_QA + i * pw:C_QA + (i + 1) * pw])
        sc["qlo"][slot, :, cols] = (q * (lo * HEAD_DIM ** -0.5)).astype(BF16)
        sc["qhi"][slot, :, cols] = (q * (hi * HEAD_DIM ** -0.5)).astype(BF16)

    def stage_kv():
        z = _dot(xn, win_ref[:, C_KV:C_KV + pw])
        low = lax.broadcasted_iota(jnp.int32, (tm, LANES), 1) < HALF
        k = z[:, 0:ATT_KV_W]
        v = z[:, ATT_KV_W:2 * ATT_KV_W]
        if kv_out is not None:
            kv_out[0][0] = k[tm - WINDOW:, :]
            kv_out[1][0] = v[tm - WINDOW:, :]
        k_r = pltpu.roll(k, HALF, axis=1)
        v_r = pltpu.roll(v, HALF, axis=1)
        sc["kd0"][slot] = jnp.where(low, k, k_r).astype(BF16)
        sc["kd1"][slot] = jnp.where(low, k_r, k).astype(BF16)
        sc["vd0"][slot, :, 0:LANES] = jnp.where(low, v, 1.0).astype(BF16)
        sc["vd0"][slot, :, LANES:2 * LANES] = jnp.where(low, 1.0, v_r).astype(BF16)
        sc["vd1"][slot, :, 0:LANES] = jnp.where(low, v_r, 1.0).astype(BF16)
        sc["vd1"][slot, :, LANES:2 * LANES] = jnp.where(low, 1.0, v).astype(BF16)

    def stage_qr():
        qr = _dot(xn, win_ref[:, C_QKR:C_QKR + pw])
        sc["qrlo"][slot] = (qr * lo).astype(BF16)
        sc["qrhi"][slot] = (qr * hi).astype(BF16)

    def stage_kr():
        sc["kr"][slot] = _dot(xn, win_ref[:, C_QKR + pw:C_QKR + 2 * pw]) * (RET_QK_DIM ** -0.5)

    def stage_vr(i):
        cols = slice(i * pw, (i + 1) * pw)
        sc["vr"][slot, :, cols] = _dot(xn, win_ref[:, C_VR + i * pw:C_VR + (i + 1) * pw]).astype(BF16)

    def stage_gate(i):
        cols = slice(i * pw, (i + 1) * pw)
        sc["gate"][slot, :, cols] = _silu(_dot(xn, win_ref[:, C_GR + i * pw:C_GR + (i + 1) * pw]))

    part = functools.partial
    return [part(stage_q, 0), part(stage_q, 1), stage_kv, stage_qr, stage_kr,
            part(stage_vr, 0), part(stage_vr, 1), part(stage_gate, 0), part(stage_gate, 1)]


def _pm_last_block(slot, tm, sc):
    rows = slice(tm - BLK, tm)
    return ([sc["kd0"][slot, rows, :], sc["kd1"][slot, rows, :]],
            [sc["vd0"][slot, rows, :], sc["vd1"][slot, rows, :]])


def _pm_blocks(slot, prev_kd, prev_vd, is_first, state, fillers, tm, sinks_ref, gng_ref, gnb_ref,
               decay_ref, xi_ref, zeta_ref, sc):
    nblk = tm // BLK
    n_units = nblk * (N_KV_HEADS + N_RET_HEADS // 2)
    pending = list(fillers)
    done_units = [0]

    def unit_done():
        done_units[0] += 1
        while pending and (len(fillers) - len(pending)) * n_units < done_units[0] * len(fillers):
            pending.pop(0)()
    lowb = lax.broadcasted_iota(jnp.int32, (BLK, LANES), 1) < HALF
    col = lax.broadcasted_iota(jnp.int32, (BLK, 2 * BLK), 1)
    first_mask = None if is_first is False else jnp.where((col < BLK) & is_first, NEG, 0.0)
    kd_refs = (sc["kd0"], sc["kd1"])
    vd_refs = (sc["vd0"], sc["vd1"])
    qlo, qhi, mix = sc["qlo"], sc["qhi"], sc["mix"]
    n_pairs = N_RET_HEADS // 2

    for j in range(nblk):
        rows = slice(j * BLK, (j + 1) * BLK)
        c0s = [kvh * KV_GROUP * HEAD_DIM for kvh in range(N_KV_HEADS)]
        lsls = [slice(i * LANES, (i + 1) * LANES) for i in range(n_pairs)]
        vds, scores = [], []
        for kvh in range(N_KV_HEADS):
            if j == 0:
                kd = jnp.concatenate([prev_kd[kvh], kd_refs[kvh][slot, rows, :]], axis=0)
                vds.append(jnp.concatenate([prev_vd[kvh], vd_refs[kvh][slot, rows, :]], axis=0))
            else:
                krows = slice((j - 1) * BLK, (j + 1) * BLK)
                kd = kd_refs[kvh][slot, krows, :]
                vds.append(vd_refs[kvh][slot, krows, :])
            c0 = c0s[kvh]
            qst = jnp.concatenate([qlo[slot, rows, c0:c0 + LANES], qhi[slot, rows, c0:c0 + LANES],
                                   qlo[slot, rows, c0 + LANES:c0 + 2 * LANES],
                                   qhi[slot, rows, c0 + LANES:c0 + 2 * LANES]], axis=0)
            scores.append(_dot_nt(qst, kd))
        unit_done()

        kps = [sc["kr"][slot, rows, lsls[i]] for i in range(n_pairs)]
        vpairs = [sc["vr"][slot, rows, 2 * i * RET_V_DIM:(2 * i + 2) * RET_V_DIM] for i in range(n_pairs)]
        q2s = [jnp.concatenate([sc["qrlo"][slot, rows, lsls[i]], sc["qrhi"][slot, rows, lsls[i]]], axis=0)
               for i in range(n_pairs)]
        a_s = [_dot_nt(q2s[i], kps[i].astype(BF16)) for i in range(n_pairs)]
        ocs = [_dot(q2s[i], state[i].astype(BF16)) for i in range(n_pairs)]
        us = [_dot_tn((kps[i] * zeta_ref[:, lsls[i]]).astype(BF16), vpairs[i]) for i in range(n_pairs)]
        unit_done()

        for kvh in range(N_KV_HEADS):
            s, vd, c0 = scores[kvh], vds[kvh], c0s[kvh]
            es, esink = [], []
            for g in range(KV_GROUP):
                h = kvh * KV_GROUP + g
                sg = s[g * BLK:(g + 1) * BLK] + sc["bias"][h]
                if j == 0 and first_mask is not None:
                    sg = sg + first_mask
                sink = sinks_ref[h]
                m = jnp.maximum(jnp.max(sg, axis=-1, keepdims=True), sink)
                es.append(jnp.exp(sg - m).astype(BF16))
                esink.append(jnp.exp(sink - m))
            o = _dot(jnp.concatenate(es, axis=0), vd)
            for pair in range(KV_GROUP // 2):
                oe = o[2 * pair * BLK:(2 * pair + 1) * BLK]
                oo = o[(2 * pair + 1) * BLK:(2 * pair + 2) * BLK]
                num = jnp.where(lowb, oe[:, :LANES], oo[:, LANES:])
                den = (jnp.where(lowb, oe[:, LANES:], oo[:, :LANES])
                       + jnp.where(lowb, esink[2 * pair], esink[2 * pair + 1]))
                cs = c0 + pair * LANES
                mix[slot, rows, cs:cs + LANES] = (num * (1.0 / den)).astype(BF16)
            unit_done()

        for i in range(n_pairs):
            a, oc, u, sp = a_s[i], ocs[i], us[i], state[i]
            inner = jnp.concatenate([a[:BLK] * decay_ref[2 * i], a[BLK:] * decay_ref[2 * i + 1]], axis=0)
            oi = _dot(inner.astype(BF16), vpairs[i])
            for half in range(2):
                h = 2 * i + half
                vsl = slice(h * RET_V_DIM, (h + 1) * RET_V_DIM)
                hr = slice(half * BLK, (half + 1) * BLK)
                o = oi[hr, half * RET_V_DIM:(half + 1) * RET_V_DIM] + oc[hr] * xi_ref[:, vsl]
                r = _group_norm(o, gng_ref[:, vsl], gnb_ref[:, vsl]) * sc["gate"][slot, rows, vsl]
                mix[slot, rows, ATT_Q_W + h * RET_V_DIM:ATT_Q_W + (h + 1) * RET_V_DIM] = r.astype(BF16)
            state[i] = jnp.concatenate(
                [_GL_PROMPT[2 * i] * sp[:RET_QK_DIM] + u[:RET_QK_DIM, :RET_V_DIM],
                 _GL_PROMPT[2 * i + 1] * sp[RET_QK_DIM:] + u[RET_QK_DIM:, RET_V_DIM:]], axis=0)

    assert not pending
    return state


def _pm_wout_pieces(slot, x_ref, rows, wout_ref, h_ref, sc):
    pw = 2 * LANES
    n = D_MODEL // pw
    parts = []

    def piece(k):
        parts.append(_dot(sc["mix"][slot], wout_ref[:, k * pw:(k + 1) * pw]))
        if k == n - 1:
            h_ref[rows, :] = x_ref[rows, :] + jnp.concatenate(parts, axis=1)

    return [functools.partial(piece, k) for k in range(n)]


def _prompt_mixer_kernel(sinks_ref, xpair_ref, xnext_ref, gmix_ref, win_ref, wout_ref, gng_ref, gnb_ref,
                           dist_ref, mask_ref, decay_ref, xi_ref, zeta_ref,
                           wupf_ref, wdnf_ref, wqf_ref, wof_ref,
                           h_ref, wk_ref, wv_ref, st_ref,
                           wupb_ref, wdnb_ref, wqb_ref, wob_ref,
                           qlo_s, qhi_s, kd0_s, kd1_s, vd0_s, vd1_s,
                           qrlo_s, qrhi_s, kr_s, vr_s, gate_s, mix_s, bias_s, state_s):
    u = pl.program_id(0)
    tm = xnext_ref.shape[0]
    wupb_ref[...] = wupf_ref[...].astype(BF16)
    wdnb_ref[...] = wdnf_ref[...].astype(BF16)
    wqb_ref[...] = wqf_ref[...].astype(BF16)
    wob_ref[...] = wof_ref[...].astype(BF16)
    sc = dict(qlo=qlo_s, qhi=qhi_s, kd0=kd0_s, kd1=kd1_s, vd0=vd0_s, vd1=vd1_s, qrlo=qrlo_s, qrhi=qrhi_s,
              kr=kr_s, vr=vr_s, gate=gate_s, mix=mix_s, bias=bias_s)
    n_pairs = N_RET_HEADS // 2
    blocks = functools.partial(_pm_blocks, tm=tm, sinks_ref=sinks_ref, gng_ref=gng_ref,
                               gnb_ref=gnb_ref, decay_ref=decay_ref, xi_ref=xi_ref, zeta_ref=zeta_ref, sc=sc)

    @pl.when(u == 0)
    def _():
        for h in range(N_ATT_HEADS):
            bias_s[h] = NEG_SLOPES[h] * dist_ref[...] + mask_ref[...]
        state_s[...] = jnp.zeros_like(state_s)
        kd0_s[1] = jnp.zeros(kd0_s.shape[1:], BF16)
        kd1_s[1] = jnp.zeros(kd1_s.shape[1:], BF16)
        vd0_s[1] = jnp.zeros(vd0_s.shape[1:], BF16)
        vd1_s[1] = jnp.zeros(vd1_s.shape[1:], BF16)
        for stage in _pm_project_stages(xpair_ref[0:tm, :], 0, gmix_ref, win_ref, sc):
            stage()

    seq_start = (u % 2) == 0
    state = [jnp.where(seq_start, 0.0, state_s[i * LANES:(i + 1) * LANES, :]) for i in range(n_pairs)]

    prev_kd, prev_vd = _pm_last_block(1, tm, sc)
    stages = _pm_project_stages(xpair_ref[tm:2 * tm, :], 1, gmix_ref, win_ref, sc, kv_out=(wk_ref, wv_ref))
    state = blocks(0, prev_kd, prev_vd, seq_start, state, stages)
    wout0 = _pm_wout_pieces(0, xpair_ref, slice(0, tm), wout_ref, h_ref, sc)

    prev_kd, prev_vd = _pm_last_block(0, tm, sc)
    stages = _pm_project_stages(xnext_ref[...], 0, gmix_ref, win_ref, sc)
    state = blocks(1, prev_kd, prev_vd, False, state, wout0 + stages)
    for piece in _pm_wout_pieces(1, xpair_ref, slice(tm, 2 * tm), wout_ref, h_ref, sc):
        piece()

    for i in range(n_pairs):
        state_s[i * LANES:(i + 1) * LANES, :] = state[i]
        st_ref[0, i * LANES:(i + 1) * LANES, :] = state[i]


def _prompt_mixer(x, g_mix, w_in, w_out, sinks, gn_g, gn_b, side_f32):
    b, s, d = x.shape
    tm = TM_MIX
    n_tiles = b * s // tm
    steps = n_tiles // 2
    seq_steps = s // (2 * tm)
    assert s % (2 * tm) == 0 and seq_steps == 2, "kernel assumes 4 tiles per sequence"
    x2d = x.reshape(b * s, d)
    const = lambda shape: pl.BlockSpec(shape, lambda i: (0,) * len(shape), pipeline_mode=pl.Buffered(1))
    slot2 = lambda rows, cols, dt: pltpu.VMEM((2, rows, cols), dt)
    side_specs = [pl.BlockSpec((w.shape[0] // steps, w.shape[1]), lambda i: (i, 0)) for w in side_f32]
    outs = pl.pallas_call(
        _prompt_mixer_kernel,
        grid=(steps,),
        in_specs=[
            pl.BlockSpec(memory_space=pltpu.SMEM),
            pl.BlockSpec((2 * tm, d), lambda i: (i, 0)),
            pl.BlockSpec((tm, d), lambda i: (jnp.minimum(2 * i + 2, n_tiles - 1), 0)),
            const((1, d)), const((d, D_IN)), const((MIX_OUT, d)),
            const((1, RET_V_W)), const((1, RET_V_W)),
            const((BLK, 2 * BLK)), const((BLK, 2 * BLK)),
            const((N_RET_HEADS, BLK, BLK)), const((BLK, RET_V_W)), const((BLK, RET_QK_W)),
        ] + side_specs,
        out_specs=[
            pl.BlockSpec((2 * tm, d), lambda i: (i, 0)),
            pl.BlockSpec((1, WINDOW, ATT_KV_W), lambda i: (i // seq_steps, 0, 0)),
            pl.BlockSpec((1, WINDOW, ATT_KV_W), lambda i: (i // seq_steps, 0, 0)),
            pl.BlockSpec((1, RET_QK_W, RET_V_DIM), lambda i: (i // seq_steps, 0, 0)),
        ] + side_specs,
        out_shape=[
            jax.ShapeDtypeStruct((b * s, d), F32),
            jax.ShapeDtypeStruct((b, WINDOW, ATT_KV_W), F32),
            jax.ShapeDtypeStruct((b, WINDOW, ATT_KV_W), F32),
            jax.ShapeDtypeStruct((b, RET_QK_W, RET_V_DIM), F32),
        ] + [jax.ShapeDtypeStruct(w.shape, BF16) for w in side_f32],
        scratch_shapes=[
            slot2(tm, ATT_Q_W, BF16), slot2(tm, ATT_Q_W, BF16),
            slot2(tm, LANES, BF16), slot2(tm, LANES, BF16),
            slot2(tm, 2 * LANES, BF16), slot2(tm, 2 * LANES, BF16),
            slot2(tm, RET_QK_W, BF16), slot2(tm, RET_QK_W, BF16),
            slot2(tm, RET_QK_W, F32), slot2(tm, RET_V_W, BF16),
            slot2(tm, RET_V_W, F32), slot2(tm, MIX_OUT, BF16),
            pltpu.VMEM((N_ATT_HEADS, BLK, 2 * BLK), F32),
            pltpu.VMEM((RET_QK_W, RET_V_DIM), F32),
        ],
        compiler_params=pltpu.CompilerParams(
            dimension_semantics=("arbitrary",), vmem_limit_bytes=VMEM_LIMIT),
        name="prompt_mixer",
    )(sinks, x2d, x2d, g_mix, w_in, w_out, gn_g, gn_b,
      jnp.asarray(_P_DIST), jnp.asarray(_P_MASK), jnp.asarray(_P_DECAY), jnp.asarray(_P_XI),
      jnp.asarray(_P_ZETA), *side_f32)
    return (outs[0].reshape(b, s, d),) + tuple(outs[1:])


def _memkv_kernel(mem_ref, g_ref, wk_ref, wv_ref, win_ref, wout_ref,
                  mk_ref, mv_ref, mkb_ref, mvb_ref, winb_ref, woutb_ref):
    winb_ref[...] = win_ref[...].astype(BF16)
    woutb_ref[...] = wout_ref[...].astype(BF16)
    mn = _rms(mem_ref[...], g_ref[...]).astype(BF16)
    mk = _dot(mn, wk_ref[...].astype(BF16))
    mv = _dot(mn, wv_ref[...].astype(BF16))
    tm = mem_ref.shape[0]
    group = X_D_HALVES * N_X_HEADS
    for hd in range(N_X_HEADS):
        for dh in range(X_D_HALVES):
            cols = slice(hd * X_HEAD_DIM + dh * LANES, hd * X_HEAD_DIM + (dh + 1) * LANES)
            rows = pl.ds(dh * N_X_HEADS + hd, tm, stride=group)
            mk_ref[rows, :] = mk[:, cols]
            mv_ref[rows, :] = mv[:, cols]
    mkb_ref[...] = mk.astype(BF16)
    mvb_ref[...] = mv.astype(BF16)


def _memory_kv(mem2d, g_mem, w_xk, w_xv, w_in, w_out):
    n, d = mem2d.shape
    tm = 512
    row = pl.BlockSpec((tm, d), lambda i: (i, 0))
    rows_out = pl.BlockSpec((tm * d // LANES, LANES), lambda i: (i, 0))
    const = lambda shape: pl.BlockSpec(shape, lambda i: (0,) * len(shape), pipeline_mode=pl.Buffered(1))
    steps = n // tm
    win_blk = pl.BlockSpec((w_in.shape[0] // steps, w_in.shape[1]), lambda i: (i, 0))
    wout_blk = pl.BlockSpec((w_out.shape[0] // steps, w_out.shape[1]), lambda i: (i, 0))
    return pl.pallas_call(
        _memkv_kernel,
        grid=(n // tm,),
        in_specs=[row, const((1, d)), const((d, d)), const((d, d)), win_blk, wout_blk],
        out_specs=[rows_out, rows_out, row, row, win_blk, wout_blk],
        out_shape=[jax.ShapeDtypeStruct((n * d // LANES, LANES), F32),
                   jax.ShapeDtypeStruct((n * d // LANES, LANES), F32),
                   jax.ShapeDtypeStruct((n, d), BF16), jax.ShapeDtypeStruct((n, d), BF16),
                   jax.ShapeDtypeStruct(w_in.shape, BF16), jax.ShapeDtypeStruct(w_out.shape, BF16)],
        compiler_params=pltpu.CompilerParams(
            dimension_semantics=("arbitrary",), vmem_limit_bytes=VMEM_LIMIT),
        name="memory_kv",
    )(mem2d, g_mem, w_xk, w_xv, w_in, w_out)


def _prompt_xattn_kernel(h_ref, g_ref, wq_ref, wo_ref, mk_ref, mv_ref, out_ref, o_s):
    def stages(r0):
        rows = slice(r0, r0 + SUB_ROWS)
        env = {}

        def project():
            env["h"] = h_ref[0, rows, :]
            xn = _rms(env["h"], g_ref[...]).astype(BF16)
            env["q"] = (_dot(xn, wq_ref[...]) * (X_HEAD_DIM ** -0.5)).astype(BF16)

        def scores(hd):
            sl = slice(hd * X_HEAD_DIM, (hd + 1) * X_HEAD_DIM)
            env[hd] = _dot_nt(env["q"][:, sl], mk_ref[0, :, sl])

        def head(hd):
            if hd + 1 < N_X_HEADS:
                scores(hd + 1)
            sl = slice(hd * X_HEAD_DIM, (hd + 1) * X_HEAD_DIM)
            s = env.pop(hd)
            m = jnp.max(s, axis=-1, keepdims=True)
            p = jnp.exp(s - m)
            p = p * (1.0 / jnp.sum(p, axis=-1, keepdims=True))
            o_s[rows, sl] = _dot(p.astype(BF16), mv_ref[0, :, sl]).astype(BF16)

        def output():
            out_ref[0, rows, :] = env["h"] + _dot(o_s[rows, :], wo_ref[...])

        def project_and_first_scores():
            project()
            scores(0)

        return ([project_and_first_scores] + [functools.partial(head, hd) for hd in range(N_X_HEADS)]
                + [output])

    chains = [stages(r0) for r0 in range(0, h_ref.shape[1], SUB_ROWS)]
    n_stage = len(chains[0])
    for step in range(n_stage + len(chains) - 1):
        for lag, chain in enumerate(chains):
            if 0 <= step - lag < n_stage:
                chain[step - lag]()


def _prompt_xattn(h, g, w_xq, w_xo, mkb, mvb):
    b, s, d = h.shape
    tm = TM_X
    const = lambda shape: pl.BlockSpec(shape, lambda i, j: (0,) * len(shape))
    tok = pl.BlockSpec((1, tm, d), lambda i, j: (i, j, 0))
    mem = pl.BlockSpec((1, N_MEM, d), lambda i, j: (i, 0, 0))
    return pl.pallas_call(
        _prompt_xattn_kernel,
        grid=(b, s // tm),
        in_specs=[tok, const((1, d)), const((d, d)), const((d, d)), mem, mem],
        out_specs=tok,
        out_shape=jax.ShapeDtypeStruct((b, s, d), F32),
        scratch_shapes=[pltpu.VMEM((tm, d), BF16)],
        compiler_params=pltpu.CompilerParams(
            dimension_semantics=("arbitrary", "arbitrary"), vmem_limit_bytes=VMEM_LIMIT),
        name="prompt_xattn",
    )(h, g, w_xq, w_xo, mkb, mvb)


def _sample_mixer_kernel(sinks_ref, x_ref, gmix_ref, win_ref, wout_ref, gng_ref, gnb_ref,
                         ck_ref, cv_ref, st_ref, bias_ref, dec_ref, xi_ref, zeta_ref,
                         h_ref, swk_ref, swv_ref, sst_ref):
    bb = ck_ref.shape[0]
    nt = bb // 2
    x = x_ref[...]
    xn = _rms(x, gmix_ref[...]).astype(BF16)
    tile3 = lambda a: a.reshape(nt, SUBLANES, a.shape[-1])

    q = _dot(xn, win_ref[:, C_QA:C_QA + ATT_Q_W]) * (HEAD_DIM ** -0.5)
    kv = _dot(xn, win_ref[:, C_KV:C_KV + 2 * ATT_KV_W])
    qkr = _dot(xn, win_ref[:, C_QKR:C_QKR + 2 * RET_QK_W])
    vr = _dot(xn, win_ref[:, C_VR:C_VR + RET_V_W])
    gate3 = tile3(_silu(_dot(xn, win_ref[:, C_GR:C_GR + RET_V_W])))

    lo512, hi512 = _half_masks(ATT_Q_W)
    q_r = pltpu.roll(q, HALF, axis=1)
    q_nat3 = tile3(q)
    q_rot3 = tile3(q_r)
    lo3 = lo512.reshape(1, 1, ATT_Q_W)
    hi3 = hi512.reshape(1, 1, ATT_Q_W)
    qa3 = (q_nat3 * lo3).astype(BF16)
    qb3 = (q_rot3 * lo3).astype(BF16)
    qc3 = (q_rot3 * hi3).astype(BF16)
    qd3 = (q_nat3 * hi3).astype(BF16)
    t128 = lambda a, i: a[:, :, i * LANES:(i + 1) * LANES]
    qs = jnp.concatenate([t128(qa3, 0), t128(qb3, 1), t128(qa3, 1), t128(qb3, 2),
                          t128(qc3, 2), t128(qd3, 2), t128(qc3, 3), t128(qd3, 3)], axis=1)

    k3 = tile3(kv[:, :ATT_KV_W])
    v3 = tile3(kv[:, ATT_KV_W:])
    pad_kv = jnp.zeros((nt, BLK - SUBLANES, LANES), BF16)
    knew_pad = jnp.concatenate([k3.astype(BF16), pad_kv], axis=1)
    vnew_pad = jnp.concatenate([v3.astype(BF16), pad_kv], axis=1)
    to_lanes = lambda a3: jnp.swapaxes(
        jnp.concatenate([a3, jnp.zeros((nt, BLK - SUBLANES, LANES), F32)], axis=1), 1, 2)
    k3t, v3t = to_lanes(k3), to_lanes(v3)
    roll3 = lambda a, sh: pltpu.roll(a.reshape(nt * BLK, LANES), sh, axis=1).reshape(nt, BLK, LANES)

    lo256, _ = _half_masks(RET_QK_W)
    qr3 = tile3(qkr[:, :RET_QK_W])
    kr3 = tile3(qkr[:, RET_QK_W:] * (RET_QK_DIM ** -0.5))
    vr3 = tile3(vr)
    lane256 = lax.broadcasted_iota(jnp.int32, (1, 1, RET_QK_W), 2)
    qrs = jnp.concatenate(
        [(qr3 * ((lane256 >= h * RET_QK_DIM) & (lane256 < (h + 1) * RET_QK_DIM)).astype(F32)).astype(BF16)
         for h in range(N_RET_HEADS)],
        axis=1)
    kr_pad = jnp.concatenate([kr3.astype(BF16), jnp.zeros((nt, BLK - SUBLANES, RET_QK_W), BF16)], axis=1)
    vr_pad = jnp.concatenate([vr3.astype(BF16), jnp.zeros((nt, BLK - SUBLANES, RET_V_W), BF16)], axis=1)

    lane = lax.broadcasted_iota(jnp.int32, (1, 1, LANES), 2)
    row8 = lax.broadcasted_iota(jnp.int32, (1, SUBLANES, 1), 1)
    bmm_nt = lambda a, b: jnp.einsum('bqd,bkd->bqk', a, b, preferred_element_type=F32)
    bmm = lambda a, b: jnp.einsum('bqk,bkd->bqd', a, b, preferred_element_type=F32)

    att_par, ret_par = [], []
    for par in range(2):
        bsl = pl.ds(par, nt, stride=2)
        ckt = ck_ref[bsl]
        cvt = cv_ref[bsl]
        keep = lane < WINDOW - DEC_SEQ
        new_shift = WINDOW - DEC_SEQ - DEC_SEQ * par
        swk_ref[bsl] = jnp.where(keep, roll3(ckt, WINDOW - DEC_SEQ), roll3(k3t, new_shift))
        swv_ref[bsl] = jnp.where(keep, roll3(cvt, WINDOW - DEC_SEQ), roll3(v3t, new_shift))

        s = jnp.concatenate([bmm(qs, ckt.astype(BF16)), bmm_nt(qs, knew_pad)], axis=2) + bias_ref[par]
        ps = []
        for h in range(N_ATT_HEADS):
            ps.append(_sink_softmax(s[:, h * SUBLANES:(h + 1) * SUBLANES, :], sinks_ref[h]).astype(BF16))
        p_all = jnp.concatenate(ps, axis=1)
        o = bmm_nt(p_all[:, :, :BLK], cvt.astype(BF16)) + bmm(p_all[:, :, BLK:], vnew_pad)
        o_r = pltpu.roll(o.reshape(nt * N_ATT_HEADS * SUBLANES, LANES), HALF, axis=1).reshape(o.shape)
        hr = lambda a, h: a[:, h * SUBLANES:(h + 1) * SUBLANES, :]
        low = lane < HALF
        att_par.append(jnp.concatenate([
            jnp.where(low, hr(o, 0), hr(o_r, 1)), jnp.where(low, hr(o, 2), hr(o_r, 3)),
            jnp.where(low, hr(o_r, 4), hr(o, 5)), jnp.where(low, hr(o_r, 6), hr(o, 7))], axis=2))

        st = st_ref[bsl]
        oc = bmm(qrs, st.astype(BF16))
        inner = (bmm_nt(qrs, kr_pad) * dec_ref[par]).astype(BF16)
        oi = bmm(inner, vr_pad)
        rs = []
        for h in range(N_RET_HEADS):
            vsl = slice(h * RET_V_DIM, (h + 1) * RET_V_DIM)
            rsl = slice(h * SUBLANES, (h + 1) * SUBLANES)
            o_h = oi[:, rsl, vsl] + oc[:, rsl, :] * xi_ref[par, rsl, :]
            rs.append(_group_norm(o_h, gng_ref[:, vsl], gnb_ref[:, vsl]) * gate3[:, :, vsl])
        ret_par.append(jnp.concatenate(rs, axis=2))

        kz3 = (kr3 * zeta_ref[par]).astype(BF16)
        vr3_b = vr3.astype(BF16)
        for p in range(nt):
            for i in range(N_RET_HEADS // 2):
                u = _dot_tn(kz3[p][:, i * LANES:(i + 1) * LANES],
                            vr3_b[p][:, 2 * i * RET_V_DIM:(2 * i + 2) * RET_V_DIM])
                for half in range(2):
                    h = 2 * i + half
                    dsl = slice(h * RET_QK_DIM, (h + 1) * RET_QK_DIM)
                    sst_ref[2 * p + par, dsl, :] = (
                        _GL_SAMPLE[h] * st[p, dsl, :]
                        + u[half * RET_QK_DIM:(half + 1) * RET_QK_DIM, half * RET_V_DIM:(half + 1) * RET_V_DIM])

    own0 = row8 < DEC_SEQ
    att3 = jnp.where(own0, att_par[0], att_par[1])
    ret3 = jnp.where(own0, ret_par[0], ret_par[1])
    mix = jnp.concatenate([att3, ret3], axis=2).reshape(2 * nt * DEC_SEQ, MIX_OUT).astype(BF16)
    h_ref[...] = x + _dot(mix, wout_ref[...])


def _sample_mixer(x2d, g_mix, w_in, w_out, sinks, gn_g, gn_b, ck, cv, st):
    n, d = x2d.shape
    nb = ck.shape[0]
    bb = BB_MIX
    r = bb * DEC_SEQ
    const = lambda shape: pl.BlockSpec(shape, lambda i: (0,) * len(shape))
    row = pl.BlockSpec((r, d), lambda i: (i, 0))
    win = pl.BlockSpec((bb, WINDOW, ATT_KV_W), lambda i: (i, 0, 0))
    state = pl.BlockSpec((bb, RET_QK_W, RET_V_DIM), lambda i: (i, 0, 0))
    return pl.pallas_call(
        _sample_mixer_kernel,
        grid=(nb // bb,),
        in_specs=[
            pl.BlockSpec(memory_space=pltpu.SMEM),
            row, const((1, d)), const((d, D_IN)), const((MIX_OUT, d)),
            const((1, RET_V_W)), const((1, RET_V_W)),
            win, win, state,
            const(_S_BIAS.shape), const(_S_DEC.shape), const(_S_XI.shape), const(_S_ZETA.shape),
        ],
        out_specs=[row, win, win, state],
        out_shape=[
            jax.ShapeDtypeStruct((n, d), F32),
            jax.ShapeDtypeStruct((nb, WINDOW, ATT_KV_W), F32),
            jax.ShapeDtypeStruct((nb, WINDOW, ATT_KV_W), F32),
            jax.ShapeDtypeStruct((nb, RET_QK_W, RET_V_DIM), F32),
        ],
        compiler_params=pltpu.CompilerParams(
            dimension_semantics=("arbitrary",), vmem_limit_bytes=VMEM_LIMIT),
        name="sample_mixer",
    )(sinks, x2d, g_mix, w_in, w_out, gn_g, gn_b, ck, cv, st,
      jnp.asarray(_S_BIAS), jnp.asarray(_S_DEC), jnp.asarray(_S_XI), jnp.asarray(_S_ZETA))


def _head_slab(x_ref, b, hd):
    group = X_D_HALVES * N_X_HEADS
    halves = [x_ref[b, pl.ds(dh * N_X_HEADS + hd, N_MEM, stride=group), :] for dh in range(X_D_HALVES)]
    return jnp.concatenate(halves, axis=1).astype(BF16)


def _mlp_value(h, g_ref, wup_ref, wdn_ref, gf_ref, fillers=None):
    xn = _rms(h, g_ref[...]).astype(BF16)
    piece = FF_CHUNK // N_X_HEADS
    opiece = D_MODEL // N_X_HEADS
    n_chunks = D_FF // FF_CHUNK
    nofill = (None, None, None)

    def up(c):
        qk, softmax, _ = fillers[c] if fillers is not None else nofill
        hid = []
        for k in range(N_X_HEADS):
            cols = slice(c * FF_CHUNK + k * piece, c * FF_CHUNK + (k + 1) * piece)
            u = jnp.maximum(_dot(xn, wup_ref[:, cols]), 0.0)
            hid.append((u * u).astype(BF16))
            if qk is not None:
                qk(k)
        if softmax is not None:
            softmax()
        return jnp.concatenate(hid, axis=1)

    def down(c, hid):
        pv = (fillers[c] if fillers is not None else nofill)[2]
        rows_c = slice(c * FF_CHUNK, (c + 1) * FF_CHUNK)
        out = []
        for k in range(N_X_HEADS):
            out.append(_dot(hid, wdn_ref[rows_c, k * opiece:(k + 1) * opiece]))
            if pv is not None:
                pv(k)
        return jnp.concatenate(out, axis=1)

    acc = h
    hid = up(0)
    for c in range(n_chunks):
        nxt = up(c + 1) if c + 1 < n_chunks else None
        acc = acc + down(c, hid)
        hid = nxt
    return _rms(acc, gf_ref[...])


def _mlp_xattn_kernel(hp_ref, hsm_ref, gx_ref, wq_ref, wo_ref, xk_ref, xv_ref, g_ref, wup_ref, wdn_ref, gf_ref,
                      yp_ref, ys_ref):
    i = pl.program_id(0)
    n = pl.num_programs(0) - 1
    bb = xk_ref.shape[0]
    rows = bb * DEC_SEQ
    assert bb == D_FF // FF_CHUNK and bb % 2 == 0

    @pl.when(i == 0)
    def _():
        xn = _rms(hsm_ref[...], gx_ref[...]).astype(BF16)
        ys_ref[...] = _dot(xn, wq_ref[...]) * (X_HEAD_DIM ** -0.5)

    @pl.when(i < n)
    def _():
        r0 = pl.multiple_of(i * rows, rows)
        own0 = lax.broadcasted_iota(jnp.int32, (SUBLANES, 1), 0) < DEC_SEQ
        o_rows = {}

        def attend(b):
            t = b // 2
            tile_rows = pl.ds(r0 + t * SUBLANES, SUBLANES)
            env = dict(s=[], o=[])

            def qk(hd):
                if hd == 0:
                    env["q"] = ys_ref[tile_rows, :].astype(BF16)
                env["s"].append(_dot_nt(env["q"][:, hd * X_HEAD_DIM:(hd + 1) * X_HEAD_DIM],
                                        _head_slab(xk_ref, b, hd)))

            def softmax():
                s = jnp.concatenate(env["s"], axis=0)
                m = jnp.max(s, axis=-1, keepdims=True)
                p = jnp.exp(s - m)
                env["p"] = p * (1.0 / jnp.sum(p, axis=-1, keepdims=True))

            def pv(hd):
                p = env["p"][hd * SUBLANES:(hd + 1) * SUBLANES].astype(BF16)
                env["o"].append(_dot(p, _head_slab(xv_ref, b, hd)))
                if hd == N_X_HEADS - 1:
                    o_rows[b] = jnp.concatenate(env["o"], axis=1)
                    if b % 2 == 1:
                        ys_ref[tile_rows, :] = jnp.where(own0, o_rows[b - 1], o_rows[b])

            return qk, softmax, pv

        fillers = [attend(b) for b in range(bb)]
        yp_ref[...] = _mlp_value(hp_ref[...], g_ref, wup_ref, wdn_ref, gf_ref, fillers)

    @pl.when(i == n)
    def _():
        hs = hsm_ref[...] + _dot(ys_ref[...].astype(BF16), wo_ref[...])
        ys_ref[...] = _mlp_value(hs, g_ref, wup_ref, wdn_ref, gf_ref)


def _mlp_xattn(hp2d, hsm, g_xattn, w_xq, w_xo, xk, xv, g_mlp, w_up, w_down, g_final):
    n, d = hp2d.shape
    ns = hsm.shape[0]
    nb = xk.shape[0]
    bb = BB_X
    tm = n // (nb // bb)
    n_tiles = n // tm
    assert n_tiles * bb == nb and tm % SUBLANES == 0
    clip = lambda i: jnp.minimum(i, n_tiles - 1)
    prompt = pl.BlockSpec((tm, d), lambda i: (clip(i), 0))
    mem = pl.BlockSpec((bb,) + xk.shape[1:], lambda i: (clip(i), 0, 0))
    const = lambda shape: pl.BlockSpec(shape, lambda i: (0,) * len(shape), pipeline_mode=pl.Buffered(1))
    return pl.pallas_call(
        _mlp_xattn_kernel,
        grid=(n_tiles + 1,),
        in_specs=[prompt, const((ns, d)), const((1, d)), const((d, d)), const((d, d)), mem, mem,
                  const((1, d)), const((d, D_FF)), const((D_FF, d)), const((1, d))],
        out_specs=[prompt, pl.BlockSpec((ns, d), lambda i: (0, 0))],
        out_shape=[jax.ShapeDtypeStruct((n, d), F32), jax.ShapeDtypeStruct((ns, d), F32)],
        compiler_params=pltpu.CompilerParams(
            dimension_semantics=("arbitrary",), vmem_limit_bytes=VMEM_LIMIT),
        name="mlp_xattn",
    )(hp2d, hsm, g_xattn, w_xq, w_xo, xk, xv, g_mlp, w_up, w_down, g_final)


def _mem_rows(c):
    nb = c.shape[0]
    c = c.reshape(nb, N_MEM, N_X_HEADS, X_D_HALVES, LANES)
    return jnp.transpose(c, (0, 1, 3, 2, 4)).reshape(nb, N_MEM * X_D_HALVES * N_X_HEADS, LANES)


def kernel(x_prompt, x_sample, mem_prompt, cache_win_k, cache_win_v, state_ret, cache_mem_k, cache_mem_v,
           g_mix, w_in, attn_sinks, ret_gn_g, ret_gn_b, w_out, g_xattn, g_mem, w_xq, w_xk, w_xv, w_xo,
           g_mlp, w_up, w_down, g_final):
    depth = w_in.shape[0]
    assert depth == 1, "single-layer trunk"
    b, s, d = x_prompt.shape
    nb, ls, _ = x_sample.shape
    row = lambda a: a.reshape(1, -1)
    sinks = attn_sinks[0]
    gn_g, gn_b = row(ret_gn_g[0]), row(ret_gn_b[0])
    g_fin = row(g_final)

    mk, mv, mkb, mvb, w_in_b, w_out_b = _memory_kv(
        mem_prompt.reshape(b * N_MEM, d), row(g_mem[0]), w_xk[0], w_xv[0], w_in[0], w_out[0])
    hp, p_wk, p_wv, p_rs, w_up_b, w_dn_b, w_xq_b, w_xo_b = _prompt_mixer(
        x_prompt, row(g_mix[0]), w_in_b, w_out_b, sinks, gn_g, gn_b,
        (w_up[0], w_down[0], w_xq[0], w_xo[0]))
    hp = _prompt_xattn(hp, row(g_xattn[0]), w_xq_b, w_xo_b,
                       mkb.reshape(b, N_MEM, d), mvb.reshape(b, N_MEM, d))

    win_t = lambda c: jnp.transpose(c, (0, 2, 3, 1)).reshape(nb, ATT_KV_W, WINDOW)
    win_t_inv = lambda a: jnp.transpose(a.reshape(nb, N_KV_HEADS, HEAD_DIM, WINDOW),
                                        (0, 3, 1, 2)).reshape(1, nb, WINDOW, N_KV_HEADS, HEAD_DIM)
    hs, s_wk, s_wv, s_rs = _sample_mixer(
        x_sample.reshape(nb * ls, d), row(g_mix[0]), w_in_b, w_out_b, sinks, gn_g, gn_b,
        win_t(cache_win_k[0]), win_t(cache_win_v[0]), state_ret[0].reshape(nb, RET_QK_W, RET_V_DIM))

    y_prompt, y_sample = _mlp_xattn(
        hp.reshape(b * s, d), hs, row(g_xattn[0]), w_xq_b, w_xo_b,
        _mem_rows(cache_mem_k[0]), _mem_rows(cache_mem_v[0]), row(g_mlp[0]), w_up_b, w_dn_b, g_fin)
    y_prompt = y_prompt.reshape(b, s, d)
    y_sample = y_sample.reshape(nb, ls, d)

    win5 = lambda a, n: a.reshape(1, n, WINDOW, N_KV_HEADS, HEAD_DIM)
    ret5 = lambda a, n: a.reshape(1, n, N_RET_HEADS, RET_QK_DIM, RET_V_DIM)
    mem5 = lambda a: jnp.transpose(a.reshape(b, N_MEM, X_D_HALVES, N_X_HEADS, LANES),
                                   (0, 1, 3, 2, 4)).reshape(1, b, N_MEM, N_X_HEADS, X_HEAD_DIM)
    return (y_prompt, y_sample,
            win5(p_wk, b), win5(p_wv, b), ret5(p_rs, b), mem5(mk), mem5(mv),
            win_t_inv(s_wk), win_t_inv(s_wv), ret5(s_rs, nb))
```

```python
import functools

import jax
import jax.numpy as jnp
import numpy as np
from jax import lax
from jax.experimental import pallas as pl
from jax.experimental.pallas import tpu as pltpu

F32 = jnp.float32
BF16 = jnp.bfloat16

D_MODEL = 1024
BATCH = 8
SEQ = 2048
DEC_BATCH = 128
DEC_SEQ = 4
HEAD_DIM = 64
N_ATT_HEADS = 8
N_KV_HEADS = 2
KV_GROUP = N_ATT_HEADS // N_KV_HEADS
WINDOW = 128
BLK = 128
N_RET_HEADS = 4
RET_QK_DIM = 64
RET_V_DIM = 128
N_MEM = 256
N_X_HEADS = 4
X_HEAD_DIM = D_MODEL // N_X_HEADS
D_FF = 4 * D_MODEL
RMS_EPS = 1e-6
GN_EPS = 1e-5

ATT_Q_W = N_ATT_HEADS * HEAD_DIM
ATT_KV_W = N_KV_HEADS * HEAD_DIM
RET_QK_W = N_RET_HEADS * RET_QK_DIM
RET_V_W = N_RET_HEADS * RET_V_DIM
MIX_OUT = ATT_Q_W + RET_V_W
D_IN = ATT_Q_W + 2 * ATT_KV_W + 2 * RET_QK_W + 2 * RET_V_W
C_QA, C_KV, C_QKR, C_VR, C_GR = 0, 512, 768, 1280, 1792

LANES = 128
SUBLANES = 8
HALF = LANES // 2
X_D_HALVES = X_HEAD_DIM // LANES
NEG = -1e30
VMEM_LIMIT = 56 * 1024 * 1024

TM_MIX = 512
TM_X = 2048
SUB_ROWS = 512
FF_CHUNK = 1024
BB_MIX = 32
BB_X = 4

NEG_SLOPES = [-(2.0 ** (-8.0 * (i + 1) / N_ATT_HEADS)) for i in range(N_ATT_HEADS)]
_LOG_G = np.log(1.0 - 2.0 ** (-5.0 - np.arange(N_RET_HEADS))).astype(np.float32).astype(np.float64)


def _prompt_tables():
    qi = np.arange(BLK)[:, None]
    kj = np.arange(2 * BLK)[None, :]
    dist = (qi + BLK - kj).astype(np.float64)
    mask = np.where((dist >= 0) & (dist < WINDOW), 0.0, NEG)
    l = np.arange(BLK, dtype=np.float64)
    diff = l[:, None] - l[None, :]
    decay = np.where(diff >= 0, np.exp(_LOG_G[:, None, None] * np.maximum(diff, 0.0)), 0.0)
    xi = np.exp((l[:, None] + 1.0) * _LOG_G[None, :])
    zeta = np.exp((BLK - 1.0 - l)[:, None] * _LOG_G[None, :])
    xi_t = np.repeat(xi, RET_V_DIM, axis=1)
    zeta_t = np.repeat(zeta, RET_QK_DIM, axis=1)
    f = lambda a: np.asarray(a, np.float32)
    return f(dist), f(mask), f(decay), f(xi_t), f(zeta_t)


def _sample_tables():
    slopes = -np.asarray(NEG_SLOPES)
    bias = np.full((2, N_ATT_HEADS * SUBLANES, 2 * BLK), NEG, np.float64)
    dec = np.zeros((2, N_RET_HEADS * SUBLANES, BLK), np.float64)
    xi = np.zeros((2, N_RET_HEADS * SUBLANES, RET_V_DIM), np.float64)
    zeta = np.zeros((2, SUBLANES, RET_QK_W), np.float64)
    for par in range(2):
        for r in range(SUBLANES):
            own = DEC_SEQ * par <= r < DEC_SEQ * (par + 1)
            t = r - DEC_SEQ * par if own else r % DEC_SEQ
            for h in range(N_ATT_HEADS):
                row = h * SUBLANES + r
                for j in range(WINDOW):
                    d = t + WINDOW - j
                    if 0 <= d < WINDOW:
                        bias[par, row, j] = -slopes[h] * d
                for c in range(DEC_SEQ):
                    d = t - c
                    if d >= 0:
                        bias[par, row, WINDOW + DEC_SEQ * par + c] = -slopes[h] * d
            for h in range(N_RET_HEADS):
                row = h * SUBLANES + r
                if own:
                    xi[par, row, :] = np.exp((t + 1.0) * _LOG_G[h])
                    zeta[par, r, h * RET_QK_DIM:(h + 1) * RET_QK_DIM] = np.exp((DEC_SEQ - 1.0 - t) * _LOG_G[h])
                    for c in range(t + 1):
                        dec[par, row, DEC_SEQ * par + c] = np.exp(_LOG_G[h] * (t - c))
    f = lambda a: np.asarray(a, np.float32)
    return f(bias), f(dec), f(xi), f(zeta)


_P_DIST, _P_MASK, _P_DECAY, _P_XI, _P_ZETA = _prompt_tables()
_S_BIAS, _S_DEC, _S_XI, _S_ZETA = _sample_tables()
_GL_PROMPT = [float(np.exp(_LOG_G[h] * BLK)) for h in range(N_RET_HEADS)]
_GL_SAMPLE = [float(np.exp(_LOG_G[h] * DEC_SEQ)) for h in range(N_RET_HEADS)]


def _rms(x, g):
    return x * lax.rsqrt(jnp.mean(x * x, axis=-1, keepdims=True) + RMS_EPS) * g


def _dot(a, b):
    return jnp.dot(a, b, preferred_element_type=F32)


def _dot_nt(a, b):
    return lax.dot_general(a, b, (((1,), (1,)), ((), ())), preferred_element_type=F32)


def _dot_tn(a, b):
    return lax.dot_general(a, b, (((0,), (0,)), ((), ())), preferred_element_type=F32)


def _silu(g):
    return g * (1.0 / (1.0 + jnp.exp(-g)))


def _half_masks(width):
    lane = lax.broadcasted_iota(jnp.int32, (1, width), 1)
    lo = ((lane & (LANES - 1)) < HALF).astype(F32)
    return lo, 1.0 - lo


def _sink_softmax(s, sink):
    m = jnp.maximum(jnp.max(s, axis=-1, keepdims=True), sink)
    p = jnp.exp(s - m)
    den = jnp.sum(p, axis=-1, keepdims=True) + jnp.exp(sink - m)
    return p * (1.0 / den)


def _group_norm(o, g, b):
    mu = jnp.mean(o, axis=-1, keepdims=True)
    d = o - mu
    var = jnp.mean(d * d, axis=-1, keepdims=True)
    return d * lax.rsqrt(var + GN_EPS) * g + b


def _pm_project_stages(x, slot, gmix_ref, win_ref, sc, kv_out=None):
    tm = x.shape[0]
    xn = _rms(x, gmix_ref[...]).astype(BF16)

    pw = 2 * LANES
    lo, hi = _half_masks(pw)

    def stage_q(i):
        cols = slice(i * pw, (i + 1) * pw)
        q = _dot(xn, win_ref[:, C_QA + i * pw:C_QA + (i + 1) * pw])
        sc["qlo"][slot, :, cols] = (q * (lo * HEAD_DIM ** -0.5)).astype(BF16)
        sc["qhi"][slot, :, cols] = (q * (hi * HEAD_DIM ** -0.5)).astype(BF16)

    def stage_kv():
        z = _dot(xn, win_ref[:, C_KV:C_KV + pw])
        low = lax.broadcasted_iota(jnp.int32, (tm, LANES), 1) < HALF
        k = z[:, 0:ATT_KV_W]
        v = z[:, ATT_KV_W:2 * ATT_KV_W]
        if kv_out is not None:
            kv_out[0][0] = k[tm - WINDOW:, :]
            kv_out[1][0] = v[tm - WINDOW:, :]
        k_r = pltpu.roll(k, HALF, axis=1)
        v_r = pltpu.roll(v, HALF, axis=1)
        sc["kd0"][slot] = jnp.where(low, k, k_r).astype(BF16)
        sc["kd1"][slot] = jnp.where(low, k_r, k).astype(BF16)
        sc["vd0"][slot, :, 0:LANES] = jnp.where(low, v, 1.0).astype(BF16)
        sc["vd0"][slot, :, LANES:2 * LANES] = jnp.where(low, 1.0, v_r).astype(BF16)
        sc["vd1"][slot, :, 0:LANES] = jnp.where(low, v_r, 1.0).astype(BF16)
        sc["vd1"][slot, :, LANES:2 * LANES] = jnp.where(low, 1.0, v).astype(BF16)

    def stage_qr():
        qr = _dot(xn, win_ref[:, C_QKR:C_QKR + pw])
        sc["qrlo"][slot] = (qr * lo).astype(BF16)
        sc["qrhi"][slot] = (qr * hi).astype(BF16)

    def stage_kr():
        sc["kr"][slot] = _dot(xn, win_ref[:, C_QKR + pw:C_QKR + 2 * pw]) * (RET_QK_DIM ** -0.5)

    def stage_vr(i):
        cols = slice(i * pw, (i + 1) * pw)
        sc["vr"][slot, :, cols] = _dot(xn, win_ref[:, C_VR + i * pw:C_VR + (i + 1) * pw]).astype(BF16)

    def stage_gate(i):
        cols = slice(i * pw, (i + 1) * pw)
        sc["gate"][slot, :, cols] = _silu(_dot(xn, win_ref[:, C_GR + i * pw:C_GR + (i + 1) * pw]))

    part = functools.partial
    return [part(stage_q, 0), part(stage_q, 1), stage_kv, stage_qr, stage_kr,
            part(stage_vr, 0), part(stage_vr, 1), part(stage_gate, 0), part(stage_gate, 1)]


def _pm_last_block(slot, tm, sc):
    rows = slice(tm - BLK, tm)
    return ([sc["kd0"][slot, rows, :], sc["kd1"][slot, rows, :]],
            [sc["vd0"][slot, rows, :], sc["vd1"][slot, rows, :]])


def _pm_blocks(slot, prev_kd, prev_vd, is_first, state, fillers, tm, sinks_ref, gng_ref, gnb_ref,
               decay_ref, xi_ref, zeta_ref, sc):
    nblk = tm // BLK
    n_units = nblk * (N_KV_HEADS + N_RET_HEADS // 2)
    pending = list(fillers)
    done_units = [0]

    def unit_done():
        done_units[0] += 1
        while pending and (len(fillers) - len(pending)) * n_units < done_units[0] * len(fillers):
            pending.pop(0)()
    lowb = lax.broadcasted_iota(jnp.int32, (BLK, LANES), 1) < HALF
    col = lax.broadcasted_iota(jnp.int32, (BLK, 2 * BLK), 1)
    first_mask = None if is_first is False else jnp.where((col < BLK) & is_first, NEG, 0.0)
    kd_refs = (sc["kd0"], sc["kd1"])
    vd_refs = (sc["vd0"], sc["vd1"])
    qlo, qhi, mix = sc["qlo"], sc["qhi"], sc["mix"]
    n_pairs = N_RET_HEADS // 2

    for j in range(nblk):
        rows = slice(j * BLK, (j + 1) * BLK)
        c0s = [kvh * KV_GROUP * HEAD_DIM for kvh in range(N_KV_HEADS)]
        lsls = [slice(i * LANES, (i + 1) * LANES) for i in range(n_pairs)]
        vds, scores = [], []
        for kvh in range(N_KV_HEADS):
            if j == 0:
                kd = jnp.concatenate([prev_kd[kvh], kd_refs[kvh][slot, rows, :]], axis=0)
                vds.append(jnp.concatenate([prev_vd[kvh], vd_refs[kvh][slot, rows, :]], axis=0))
            else:
                krows = slice((j - 1) * BLK, (j + 1) * BLK)
                kd = kd_refs[kvh][slot, krows, :]
                vds.append(vd_refs[kvh][slot, krows, :])
            c0 = c0s[kvh]
            qst = jnp.concatenate([qlo[slot, rows, c0:c0 + LANES], qhi[slot, rows, c0:c0 + LANES],
                                   qlo[slot, rows, c0 + LANES:c0 + 2 * LANES],
                                   qhi[slot, rows, c0 + LANES:c0 + 2 * LANES]], axis=0)
            scores.append(_dot_nt(qst, kd))
        unit_done()

        kps = [sc["kr"][slot, rows, lsls[i]] for i in range(n_pairs)]
        vpairs = [sc["vr"][slot, rows, 2 * i * RET_V_DIM:(2 * i + 2) * RET_V_DIM] for i in range(n_pairs)]
        q2s = [jnp.concatenate([sc["qrlo"][slot, rows, lsls[i]], sc["qrhi"][slot, rows, lsls[i]]], axis=0)
               for i in range(n_pairs)]
        a_s = [_dot_nt(q2s[i], kps[i].astype(BF16)) for i in range(n_pairs)]
        ocs = [_dot(q2s[i], state[i].astype(BF16)) for i in range(n_pairs)]
        us = [_dot_tn((kps[i] * zeta_ref[:, lsls[i]]).astype(BF16), vpairs[i]) for i in range(n_pairs)]
        unit_done()

        for kvh in range(N_KV_HEADS):
            s, vd, c0 = scores[kvh], vds[kvh], c0s[kvh]
            es, esink = [], []
            for g in range(KV_GROUP):
                h = kvh * KV_GROUP + g
                sg = s[g * BLK:(g + 1) * BLK] + sc["bias"][h]
                if j == 0 and first_mask is not None:
                    sg = sg + first_mask
                sink = sinks_ref[h]
                m = jnp.maximum(jnp.max(sg, axis=-1, keepdims=True), sink)
                es.append(jnp.exp(sg - m).astype(BF16))
                esink.append(jnp.exp(sink - m))
            o = _dot(jnp.concatenate(es, axis=0), vd)
            for pair in range(KV_GROUP // 2):
                oe = o[2 * pair * BLK:(2 * pair + 1) * BLK]
                oo = o[(2 * pair + 1) * BLK:(2 * pair + 2) * BLK]
                num = jnp.where(lowb, oe[:, :LANES], oo[:, LANES:])
                den = (jnp.where(lowb, oe[:, LANES:], oo[:, :LANES])
                       + jnp.where(lowb, esink[2 * pair], esink[2 * pair + 1]))
                cs = c0 + pair * LANES
                mix[slot, rows, cs:cs + LANES] = (num * (1.0 / den)).astype(BF16)
            unit_done()

        for i in range(n_pairs):
            a, oc, u, sp = a_s[i], ocs[i], us[i], state[i]
            inner = jnp.concatenate([a[:BLK] * decay_ref[2 * i], a[BLK:] * decay_ref[2 * i + 1]], axis=0)
            oi = _dot(inner.astype(BF16), vpairs[i])
            for half in range(2):
                h = 2 * i + half
                vsl = slice(h * RET_V_DIM, (h + 1) * RET_V_DIM)
                hr = slice(half * BLK, (half + 1) * BLK)
                o = oi[hr, half * RET_V_DIM:(half + 1) * RET_V_DIM] + oc[hr] * xi_ref[:, vsl]
                r = _group_norm(o, gng_ref[:, vsl], gnb_ref[:, vsl]) * sc["gate"][slot, rows, vsl]
                mix[slot, rows, ATT_Q_W + h * RET_V_DIM:ATT_Q_W + (h + 1) * RET_V_DIM] = r.astype(BF16)
            state[i] = jnp.concatenate(
                [_GL_PROMPT[2 * i] * sp[:RET_QK_DIM] + u[:RET_QK_DIM, :RET_V_DIM],
                 _GL_PROMPT[2 * i + 1] * sp[RET_QK_DIM:] + u[RET_QK_DIM:, RET_V_DIM:]], axis=0)

    assert not pending
    return state


def _pm_wout_pieces(slot, x_ref, rows, wout_ref, h_ref, sc):
    pw = 2 * LANES
    n = D_MODEL // pw
    parts = []

    def piece(k):
        parts.append(_dot(sc["mix"][slot], wout_ref[:, k * pw:(k + 1) * pw]))
        if k == n - 1:
            h_ref[rows, :] = x_ref[rows, :] + jnp.concatenate(parts, axis=1)

    return [functools.partial(piece, k) for k in range(n)]


def _prompt_mixer_kernel(sinks_ref, xpair_ref, xnext_ref, gmix_ref, win_ref, wout_ref, gng_ref, gnb_ref,
                           dist_ref, mask_ref, decay_ref, xi_ref, zeta_ref,
                           wupf_ref, wdnf_ref, wqf_ref, wof_ref,
                           h_ref, wk_ref, wv_ref, st_ref,
                           wupb_ref, wdnb_ref, wqb_ref, wob_ref,
                           qlo_s, qhi_s, kd0_s, kd1_s, vd0_s, vd1_s,
                           qrlo_s, qrhi_s, kr_s, vr_s, gate_s, mix_s, bias_s, state_s):
    u = pl.program_id(0)
    tm = xnext_ref.shape[0]
    wupb_ref[...] = wupf_ref[...].astype(BF16)
    wdnb_ref[...] = wdnf_ref[...].astype(BF16)
    wqb_ref[...] = wqf_ref[...].astype(BF16)
    wob_ref[...] = wof_ref[...].astype(BF16)
    sc = dict(qlo=qlo_s, qhi=qhi_s, kd0=kd0_s, kd1=kd1_s, vd0=vd0_s, vd1=vd1_s, qrlo=qrlo_s, qrhi=qrhi_s,
              kr=kr_s, vr=vr_s, gate=gate_s, mix=mix_s, bias=bias_s)
    n_pairs = N_RET_HEADS // 2
    blocks = functools.partial(_pm_blocks, tm=tm, sinks_ref=sinks_ref, gng_ref=gng_ref,
                               gnb_ref=gnb_ref, decay_ref=decay_ref, xi_ref=xi_ref, zeta_ref=zeta_ref, sc=sc)

    @pl.when(u == 0)
    def _():
        for h in range(N_ATT_HEADS):
            bias_s[h] = NEG_SLOPES[h] * dist_ref[...] + mask_ref[...]
        state_s[...] = jnp.zeros_like(state_s)
        kd0_s[1] = jnp.zeros(kd0_s.shape[1:], BF16)
        kd1_s[1] = jnp.zeros(kd1_s.shape[1:], BF16)
        vd0_s[1] = jnp.zeros(vd0_s.shape[1:], BF16)
        vd1_s[1] = jnp.zeros(vd1_s.shape[1:], BF16)
        for stage in _pm_project_stages(xpair_ref[0:tm, :], 0, gmix_ref, win_ref, sc):
            stage()

    seq_start = (u % 2) == 0
    state = [jnp.where(seq_start, 0.0, state_s[i * LANES:(i + 1) * LANES, :]) for i in range(n_pairs)]

    prev_kd, prev_vd = _pm_last_block(1, tm, sc)
    stages = _pm_project_stages(xpair_ref[tm:2 * tm, :], 1, gmix_ref, win_ref, sc, kv_out=(wk_ref, wv_ref))
    state = blocks(0, prev_kd, prev_vd, seq_start, state, stages)
    wout0 = _pm_wout_pieces(0, xpair_ref, slice(0, tm), wout_ref, h_ref, sc)

    prev_kd, prev_vd = _pm_last_block(0, tm, sc)
    stages = _pm_project_stages(xnext_ref[...], 0, gmix_ref, win_ref, sc)
    state = blocks(1, prev_kd, prev_vd, False, state, wout0 + stages)
    for piece in _pm_wout_pieces(1, xpair_ref, slice(tm, 2 * tm), wout_ref, h_ref, sc):
        piece()

    for i in range(n_pairs):
        state_s[i * LANES:(i + 1) * LANES, :] = state[i]
        st_ref[0, i * LANES:(i + 1) * LANES, :] = state[i]


def _prompt_mixer(x, g_mix, w_in, w_out, sinks, gn_g, gn_b, side_f32):
    b, s, d = x.shape
    tm = TM_MIX
    n_tiles = b * s // tm
    steps = n_tiles // 2
    seq_steps = s // (2 * tm)
    assert s % (2 * tm) == 0 and seq_steps == 2, "kernel assumes 4 tiles per sequence"
    x2d = x.reshape(b * s, d)
    const = lambda shape: pl.BlockSpec(shape, lambda i: (0,) * len(shape), pipeline_mode=pl.Buffered(1))
    slot2 = lambda rows, cols, dt: pltpu.VMEM((2, rows, cols), dt)
    side_specs = [pl.BlockSpec((w.shape[0] // steps, w.shape[1]), lambda i: (i, 0)) for w in side_f32]
    outs = pl.pallas_call(
        _prompt_mixer_kernel,
        grid=(steps,),
        in_specs=[
            pl.BlockSpec(memory_space=pltpu.SMEM),
            pl.BlockSpec((2 * tm, d), lambda i: (i, 0)),
            pl.BlockSpec((tm, d), lambda i: (jnp.minimum(2 * i + 2, n_tiles - 1), 0)),
            const((1, d)), const((d, D_IN)), const((MIX_OUT, d)),
            const((1, RET_V_W)), const((1, RET_V_W)),
            const((BLK, 2 * BLK)), const((BLK, 2 * BLK)),
            const((N_RET_HEADS, BLK, BLK)), const((BLK, RET_V_W)), const((BLK, RET_QK_W)),
        ] + side_specs,
        out_specs=[
            pl.BlockSpec((2 * tm, d), lambda i: (i, 0)),
            pl.BlockSpec((1, WINDOW, ATT_KV_W), lambda i: (i // seq_steps, 0, 0)),
            pl.BlockSpec((1, WINDOW, ATT_KV_W), lambda i: (i // seq_steps, 0, 0)),
            pl.BlockSpec((1, RET_QK_W, RET_V_DIM), lambda i: (i // seq_steps, 0, 0)),
        ] + side_specs,
        out_shape=[
            jax.ShapeDtypeStruct((b * s, d), F32),
            jax.ShapeDtypeStruct((b, WINDOW, ATT_KV_W), F32),
            jax.ShapeDtypeStruct((b, WINDOW, ATT_KV_W), F32),
            jax.ShapeDtypeStruct((b, RET_QK_W, RET_V_DIM), F32),
        ] + [jax.ShapeDtypeStruct(w.shape, BF16) for w in side_f32],
        scratch_shapes=[
            slot2(tm, ATT_Q_W, BF16), slot2(tm, ATT_Q_W, BF16),
            slot2(tm, LANES, BF16), slot2(tm, LANES, BF16),
            slot2(tm, 2 * LANES, BF16), slot2(tm, 2 * LANES, BF16),
            slot2(tm, RET_QK_W, BF16), slot2(tm, RET_QK_W, BF16),
            slot2(tm, RET_QK_W, F32), slot2(tm, RET_V_W, BF16),
            slot2(tm, RET_V_W, F32), slot2(tm, MIX_OUT, BF16),
            pltpu.VMEM((N_ATT_HEADS, BLK, 2 * BLK), F32),
            pltpu.VMEM((RET_QK_W, RET_V_DIM), F32),
        ],
        compiler_params=pltpu.CompilerParams(
            dimension_semantics=("arbitrary",), vmem_limit_bytes=VMEM_LIMIT),
        name="prompt_mixer",
    )(sinks, x2d, x2d, g_mix, w_in, w_out, gn_g, gn_b,
      jnp.asarray(_P_DIST), jnp.asarray(_P_MASK), jnp.asarray(_P_DECAY), jnp.asarray(_P_XI),
      jnp.asarray(_P_ZETA), *side_f32)
    return (outs[0].reshape(b, s, d),) + tuple(outs[1:])


def _memkv_kernel(mem_ref, g_ref, wk_ref, wv_ref, win_ref, wout_ref,
                  mk_ref, mv_ref, mkb_ref, mvb_ref, winb_ref, woutb_ref):
    winb_ref[...] = win_ref[...].astype(BF16)
    woutb_ref[...] = wout_ref[...].astype(BF16)
    mn = _rms(mem_ref[...], g_ref[...]).astype(BF16)
    mk = _dot(mn, wk_ref[...].astype(BF16))
    mv = _dot(mn, wv_ref[...].astype(BF16))
    tm = mem_ref.shape[0]
    group = X_D_HALVES * N_X_HEADS
    for hd in range(N_X_HEADS):
        for dh in range(X_D_HALVES):
            cols = slice(hd * X_HEAD_DIM + dh * LANES, hd * X_HEAD_DIM + (dh + 1) * LANES)
            rows = pl.ds(dh * N_X_HEADS + hd, tm, stride=group)
            mk_ref[rows, :] = mk[:, cols]
            mv_ref[rows, :] = mv[:, cols]
    mkb_ref[...] = mk.astype(BF16)
    mvb_ref[...] = mv.astype(BF16)


def _memory_kv(mem2d, g_mem, w_xk, w_xv, w_in, w_out):
    n, d = mem2d.shape
    tm = 512
    row = pl.BlockSpec((tm, d), lambda i: (i, 0))
    rows_out = pl.BlockSpec((tm * d // LANES, LANES), lambda i: (i, 0))
    const = lambda shape: pl.BlockSpec(shape, lambda i: (0,) * len(shape), pipeline_mode=pl.Buffered(1))
    steps = n // tm
    win_blk = pl.BlockSpec((w_in.shape[0] // steps, w_in.shape[1]), lambda i: (i, 0))
    wout_blk = pl.BlockSpec((w_out.shape[0] // steps, w_out.shape[1]), lambda i: (i, 0))
    return pl.pallas_call(
        _memkv_kernel,
        grid=(n // tm,),
        in_specs=[row, const((1, d)), const((d, d)), const((d, d)), win_blk, wout_blk],
        out_specs=[rows_out, rows_out, row, row, win_blk, wout_blk],
        out_shape=[jax.ShapeDtypeStruct((n * d // LANES, LANES), F32),
                   jax.ShapeDtypeStruct((n * d // LANES, LANES), F32),
                   jax.ShapeDtypeStruct((n, d), BF16), jax.ShapeDtypeStruct((n, d), BF16),
                   jax.ShapeDtypeStruct(w_in.shape, BF16), jax.ShapeDtypeStruct(w_out.shape, BF16)],
        compiler_params=pltpu.CompilerParams(
            dimension_semantics=("arbitrary",), vmem_limit_bytes=VMEM_LIMIT),
        name="memory_kv",
    )(mem2d, g_mem, w_xk, w_xv, w_in, w_out)


def _prompt_xattn_kernel(h_ref, g_ref, wq_ref, wo_ref, mk_ref, mv_ref, out_ref, o_s):
    def stages(r0):
        rows = slice(r0, r0 + SUB_ROWS)
        env = {}

        def project():
            env["h"] = h_ref[0, rows, :]
            xn = _rms(env["h"], g_ref[...]).astype(BF16)
            env["q"] = (_dot(xn, wq_ref[...]) * (X_HEAD_DIM ** -0.5)).astype(BF16)

        def scores(hd):
            sl = slice(hd * X_HEAD_DIM, (hd + 1) * X_HEAD_DIM)
            env[hd] = _dot_nt(env["q"][:, sl], mk_ref[0, :, sl])

        def head(hd):
            if hd + 1 < N_X_HEADS:
                scores(hd + 1)
            sl = slice(hd * X_HEAD_DIM, (hd + 1) * X_HEAD_DIM)
            s = env.pop(hd)
            m = jnp.max(s, axis=-1, keepdims=True)
            p = jnp.exp(s - m)
            p = p * (1.0 / jnp.sum(p, axis=-1, keepdims=True))
            o_s[rows, sl] = _dot(p.astype(BF16), mv_ref[0, :, sl]).astype(BF16)

        def output():
            out_ref[0, rows, :] = env["h"] + _dot(o_s[rows, :], wo_ref[...])

        def project_and_first_scores():
            project()
            scores(0)

        return ([project_and_first_scores] + [functools.partial(head, hd) for hd in range(N_X_HEADS)]
                + [output])

    chains = [stages(r0) for r0 in range(0, h_ref.shape[1], SUB_ROWS)]
    n_stage = len(chains[0])
    for step in range(n_stage + len(chains) - 1):
        for lag, chain in enumerate(chains):
            if 0 <= step - lag < n_stage:
                chain[step - lag]()


def _prompt_xattn(h, g, w_xq, w_xo, mkb, mvb):
    b, s, d = h.shape
    tm = TM_X
    const = lambda shape: pl.BlockSpec(shape, lambda i, j: (0,) * len(shape))
    tok = pl.BlockSpec((1, tm, d), lambda i, j: (i, j, 0))
    mem = pl.BlockSpec((1, N_MEM, d), lambda i, j: (i, 0, 0))
    return pl.pallas_call(
        _prompt_xattn_kernel,
        grid=(b, s // tm),
        in_specs=[tok, const((1, d)), const((d, d)), const((d, d)), mem, mem],
        out_specs=tok,
        out_shape=jax.ShapeDtypeStruct((b, s, d), F32),
        scratch_shapes=[pltpu.VMEM((tm, d), BF16)],
        compiler_params=pltpu.CompilerParams(
            dimension_semantics=("arbitrary", "arbitrary"), vmem_limit_bytes=VMEM_LIMIT),
        name="prompt_xattn",
    )(h, g, w_xq, w_xo, mkb, mvb)


def _sample_mixer_kernel(sinks_ref, x_ref, gmix_ref, win_ref, wout_ref, gng_ref, gnb_ref,
                         ck_ref, cv_ref, st_ref, bias_ref, dec_ref, xi_ref, zeta_ref,
                         h_ref, swk_ref, swv_ref, sst_ref):
    bb = ck_ref.shape[0]
    nt = bb // 2
    x = x_ref[...].reshape(bb * DEC_SEQ, D_MODEL)
    xn = _rms(x, gmix_ref[...]).astype(BF16)
    tile3 = lambda a: a.reshape(nt, SUBLANES, a.shape[-1])

    q = _dot(xn, win_ref[:, C_QA:C_QA + ATT_Q_W]) * (HEAD_DIM ** -0.5)
    kv = _dot(xn, win_ref[:, C_KV:C_KV + 2 * ATT_KV_W])
    qkr = _dot(xn, win_ref[:, C_QKR:C_QKR + 2 * RET_QK_W])
    vr = _dot(xn, win_ref[:, C_VR:C_VR + RET_V_W])
    gate3 = tile3(_silu(_dot(xn, win_ref[:, C_GR:C_GR + RET_V_W])))

    lo512, hi512 = _half_masks(ATT_Q_W)
    q_r = pltpu.roll(q, HALF, axis=1)
    q_nat3 = tile3(q)
    q_rot3 = tile3(q_r)
    lo3 = lo512.reshape(1, 1, ATT_Q_W)
    hi3 = hi512.reshape(1, 1, ATT_Q_W)
    qa3 = (q_nat3 * lo3).astype(BF16)
    qb3 = (q_rot3 * lo3).astype(BF16)
    qc3 = (q_rot3 * hi3).astype(BF16)
    qd3 = (q_nat3 * hi3).astype(BF16)
    t128 = lambda a, i: a[:, :, i * LANES:(i + 1) * LANES]
    qs = jnp.concatenate([t128(qa3, 0), t128(qb3, 1), t128(qa3, 1), t128(qb3, 2),
                          t128(qc3, 2), t128(qd3, 2), t128(qc3, 3), t128(qd3, 3)], axis=1)

    k3 = tile3(kv[:, :ATT_KV_W])
    v3 = tile3(kv[:, ATT_KV_W:])
    pad_kv = jnp.zeros((nt, BLK - SUBLANES, LANES), BF16)
    knew_pad = jnp.concatenate([k3.astype(BF16), pad_kv], axis=1)
    vnew_pad = jnp.concatenate([v3.astype(BF16), pad_kv], axis=1)
    to_lanes = lambda a3: jnp.swapaxes(
        jnp.concatenate([a3, jnp.zeros((nt, BLK - SUBLANES, LANES), F32)], axis=1), 1, 2)
    k3t, v3t = to_lanes(k3), to_lanes(v3)
    roll3 = lambda a, sh: pltpu.roll(a.reshape(nt * BLK, LANES), sh, axis=1).reshape(nt, BLK, LANES)

    lo256, _ = _half_masks(RET_QK_W)
    qr3 = tile3(qkr[:, :RET_QK_W])
    kr3 = tile3(qkr[:, RET_QK_W:] * (RET_QK_DIM ** -0.5))
    vr3 = tile3(vr)
    lane256 = lax.broadcasted_iota(jnp.int32, (1, 1, RET_QK_W), 2)
    qrs = jnp.concatenate(
        [(qr3 * ((lane256 >= h * RET_QK_DIM) & (lane256 < (h + 1) * RET_QK_DIM)).astype(F32)).astype(BF16)
         for h in range(N_RET_HEADS)],
        axis=1)
    kr_pad = jnp.concatenate([kr3.astype(BF16), jnp.zeros((nt, BLK - SUBLANES, RET_QK_W), BF16)], axis=1)
    vr_pad = jnp.concatenate([vr3.astype(BF16), jnp.zeros((nt, BLK - SUBLANES, RET_V_W), BF16)], axis=1)

    lane = lax.broadcasted_iota(jnp.int32, (1, 1, LANES), 2)
    row8 = lax.broadcasted_iota(jnp.int32, (1, SUBLANES, 1), 1)
    bmm_nt = lambda a, b: jnp.einsum('bqd,bkd->bqk', a, b, preferred_element_type=F32)
    bmm = lambda a, b: jnp.einsum('bqk,bkd->bqd', a, b, preferred_element_type=F32)

    att_par, ret_par = [], []
    for par in range(2):
        bsl = pl.ds(par, nt, stride=2)
        ckt = ck_ref[bsl]
        cvt = cv_ref[bsl]
        keep = lane < WINDOW - DEC_SEQ
        new_shift = WINDOW - DEC_SEQ - DEC_SEQ * par
        swk_ref[bsl] = jnp.where(keep, roll3(ckt, WINDOW - DEC_SEQ), roll3(k3t, new_shift))
        swv_ref[bsl] = jnp.where(keep, roll3(cvt, WINDOW - DEC_SEQ), roll3(v3t, new_shift))

        s = jnp.concatenate([bmm(qs, ckt.astype(BF16)), bmm_nt(qs, knew_pad)], axis=2) + bias_ref[par]
        ps = []
        for h in range(N_ATT_HEADS):
            ps.append(_sink_softmax(s[:, h * SUBLANES:(h + 1) * SUBLANES, :], sinks_ref[h]).astype(BF16))
        p_all = jnp.concatenate(ps, axis=1)
        o = bmm_nt(p_all[:, :, :BLK], cvt.astype(BF16)) + bmm(p_all[:, :, BLK:], vnew_pad)
        o_r = pltpu.roll(o.reshape(nt * N_ATT_HEADS * SUBLANES, LANES), HALF, axis=1).reshape(o.shape)
        hr = lambda a, h: a[:, h * SUBLANES:(h + 1) * SUBLANES, :]
        low = lane < HALF
        att_par.append(jnp.concatenate([
            jnp.where(low, hr(o, 0), hr(o_r, 1)), jnp.where(low, hr(o, 2), hr(o_r, 3)),
            jnp.where(low, hr(o_r, 4), hr(o, 5)), jnp.where(low, hr(o_r, 6), hr(o, 7))], axis=2))

        st = st_ref[bsl]
        oc = bmm(qrs, st.astype(BF16))
        inner = (bmm_nt(qrs, kr_pad) * dec_ref[par]).astype(BF16)
        oi = bmm(inner, vr_pad)
        rs = []
        for h in range(N_RET_HEADS):
            vsl = slice(h * RET_V_DIM, (h + 1) * RET_V_DIM)
            rsl = slice(h * SUBLANES, (h + 1) * SUBLANES)
            o_h = oi[:, rsl, vsl] + oc[:, rsl, :] * xi_ref[par, rsl, :]
            rs.append(_group_norm(o_h, gng_ref[:, vsl], gnb_ref[:, vsl]) * gate3[:, :, vsl])
        ret_par.append(jnp.concatenate(rs, axis=2))

        kz3 = (kr3 * zeta_ref[par]).astype(BF16)
        vr3_b = vr3.astype(BF16)
        for p in range(nt):
            for i in range(N_RET_HEADS // 2):
                u = _dot_tn(kz3[p][:, i * LANES:(i + 1) * LANES],
                            vr3_b[p][:, 2 * i * RET_V_DIM:(2 * i + 2) * RET_V_DIM])
                for half in range(2):
                    h = 2 * i + half
                    dsl = slice(h * RET_QK_DIM, (h + 1) * RET_QK_DIM)
                    sst_ref[2 * p + par, dsl, :] = (
                        _GL_SAMPLE[h] * st[p, dsl, :]
                        + u[half * RET_QK_DIM:(half + 1) * RET_QK_DIM, half * RET_V_DIM:(half + 1) * RET_V_DIM])

    own0 = row8 < DEC_SEQ
    att3 = jnp.where(own0, att_par[0], att_par[1])
    ret3 = jnp.where(own0, ret_par[0], ret_par[1])
    mix = jnp.concatenate([att3, ret3], axis=2).reshape(2 * nt * DEC_SEQ, MIX_OUT).astype(BF16)
    h_ref[...] = x + _dot(mix, wout_ref[...])


def _sample_mixer(x3d, g_mix, w_in, w_out, sinks, gn_g, gn_b, ck, cv, st):
    nb, ls, d = x3d.shape
    n = nb * ls
    bb = BB_MIX
    r = bb * DEC_SEQ
    const = lambda shape: pl.BlockSpec(shape, lambda i: (0,) * len(shape))
    row = pl.BlockSpec((r, d), lambda i: (i, 0))
    win = pl.BlockSpec((bb, WINDOW, ATT_KV_W), lambda i: (i, 0, 0))
    state = pl.BlockSpec((bb, RET_QK_W, RET_V_DIM), lambda i: (i, 0, 0))
    return pl.pallas_call(
        _sample_mixer_kernel,
        grid=(nb // bb,),
        in_specs=[
            pl.BlockSpec(memory_space=pltpu.SMEM),
            pl.BlockSpec((bb, ls, d), lambda i: (i, 0, 0)), const((1, d)), const((d, D_IN)), const((MIX_OUT, d)),
            const((1, RET_V_W)), const((1, RET_V_W)),
            win, win, state,
            const(_S_BIAS.shape), const(_S_DEC.shape), const(_S_XI.shape), const(_S_ZETA.shape),
        ],
        out_specs=[row, win, win, state],
        out_shape=[
            jax.ShapeDtypeStruct((n, d), F32),
            jax.ShapeDtypeStruct((nb, WINDOW, ATT_KV_W), F32),
            jax.ShapeDtypeStruct((nb, WINDOW, ATT_KV_W), F32),
            jax.ShapeDtypeStruct((nb, RET_QK_W, RET_V_DIM), F32),
        ],
        compiler_params=pltpu.CompilerParams(
            dimension_semantics=("arbitrary",), vmem_limit_bytes=VMEM_LIMIT),
        name="sample_mixer",
    )(sinks, x3d, g_mix, w_in, w_out, gn_g, gn_b, ck, cv, st,
      jnp.asarray(_S_BIAS), jnp.asarray(_S_DEC), jnp.asarray(_S_XI), jnp.asarray(_S_ZETA))


def _head_slab(x_ref, b, hd):
    group = X_D_HALVES * N_X_HEADS
    halves = [x_ref[b, pl.ds(dh * N_X_HEADS + hd, N_MEM, stride=group), :] for dh in range(X_D_HALVES)]
    return jnp.concatenate(halves, axis=1).astype(BF16)


def _mlp_value(h, g_ref, wup_ref, wdn_ref, gf_ref, fillers=None):
    xn = _rms(h, g_ref[...]).astype(BF16)
    piece = FF_CHUNK // N_X_HEADS
    opiece = D_MODEL // N_X_HEADS
    n_chunks = D_FF // FF_CHUNK
    nofill = (None, None, None)

    def up(c):
        qk, softmax, _ = fillers[c] if fillers is not None else nofill
        hid = []
        for k in range(N_X_HEADS):
            cols = slice(c * FF_CHUNK + k * piece, c * FF_CHUNK + (k + 1) * piece)
            u = jnp.maximum(_dot(xn, wup_ref[:, cols]), 0.0)
            hid.append((u * u).astype(BF16))
            if qk is not None:
                qk(k)
        if softmax is not None:
            softmax()
        return jnp.concatenate(hid, axis=1)

    def down(c, hid):
        pv = (fillers[c] if fillers is not None else nofill)[2]
        rows_c = slice(c * FF_CHUNK, (c + 1) * FF_CHUNK)
        out = []
        for k in range(N_X_HEADS):
            out.append(_dot(hid, wdn_ref[rows_c, k * opiece:(k + 1) * opiece]))
            if pv is not None:
                pv(k)
        return jnp.concatenate(out, axis=1)

    acc = h
    hid = up(0)
    for c in range(n_chunks):
        nxt = up(c + 1) if c + 1 < n_chunks else None
        acc = acc + down(c, hid)
        hid = nxt
    return _rms(acc, gf_ref[...])


def _mlp_xattn_kernel(hp_ref, hsm_ref, gx_ref, wq_ref, wo_ref, xk_ref, xv_ref, g_ref, wup_ref, wdn_ref, gf_ref,
                      yp_ref, ys_ref):
    i = pl.program_id(0)
    n = pl.num_programs(0) - 1
    bb = xk_ref.shape[0]
    rows = bb * DEC_SEQ
    assert bb == D_FF // FF_CHUNK and bb % 2 == 0

    @pl.when(i == 0)
    def _():
        xn = _rms(hsm_ref[...], gx_ref[...]).astype(BF16)
        ys_ref[...] = _dot(xn, wq_ref[...]) * (X_HEAD_DIM ** -0.5)

    @pl.when(i < n)
    def _():
        r0 = pl.multiple_of(i * rows, rows)
        own0 = lax.broadcasted_iota(jnp.int32, (SUBLANES, 1), 0) < DEC_SEQ
        o_rows = {}

        def attend(b):
            t = b // 2
            tile_rows = pl.ds(r0 + t * SUBLANES, SUBLANES)
            env = dict(s=[], o=[])

            def qk(hd):
                if hd == 0:
                    env["q"] = ys_ref[tile_rows, :].astype(BF16)
                env["s"].append(_dot_nt(env["q"][:, hd * X_HEAD_DIM:(hd + 1) * X_HEAD_DIM],
                                        _head_slab(xk_ref, b, hd)))

            def softmax():
                s = jnp.concatenate(env["s"], axis=0)
                m = jnp.max(s, axis=-1, keepdims=True)
                p = jnp.exp(s - m)
                env["p"] = p * (1.0 / jnp.sum(p, axis=-1, keepdims=True))

            def pv(hd):
                p = env["p"][hd * SUBLANES:(hd + 1) * SUBLANES].astype(BF16)
                env["o"].append(_dot(p, _head_slab(xv_ref, b, hd)))
                if hd == N_X_HEADS - 1:
                    o_rows[b] = jnp.concatenate(env["o"], axis=1)
                    if b % 2 == 1:
                        ys_ref[tile_rows, :] = jnp.where(own0, o_rows[b - 1], o_rows[b])

            return qk, softmax, pv

        fillers = [attend(b) for b in range(bb)]
        yp_ref[...] = _mlp_value(hp_ref[...], g_ref, wup_ref, wdn_ref, gf_ref, fillers)

    @pl.when(i == n)
    def _():
        hs = hsm_ref[...] + _dot(ys_ref[...].astype(BF16), wo_ref[...])
        ys_ref[...] = _mlp_value(hs, g_ref, wup_ref, wdn_ref, gf_ref)


def _mlp_xattn(hp2d, hsm, g_xattn, w_xq, w_xo, xk, xv, g_mlp, w_up, w_down, g_final):
    n, d = hp2d.shape
    ns = hsm.shape[0]
    nb = xk.shape[0]
    bb = BB_X
    tm = n // (nb // bb)
    n_tiles = n // tm
    assert n_tiles * bb == nb and tm % SUBLANES == 0
    clip = lambda i: jnp.minimum(i, n_tiles - 1)
    prompt = pl.BlockSpec((tm, d), lambda i: (clip(i), 0))
    mem = pl.BlockSpec((bb,) + xk.shape[1:], lambda i: (clip(i), 0, 0))
    const = lambda shape: pl.BlockSpec(shape, lambda i: (0,) * len(shape), pipeline_mode=pl.Buffered(1))
    return pl.pallas_call(
        _mlp_xattn_kernel,
        grid=(n_tiles + 1,),
        in_specs=[prompt, const((ns, d)), const((1, d)), const((d, d)), const((d, d)), mem, mem,
                  const((1, d)), const((d, D_FF)), const((D_FF, d)), const((1, d))],
        out_specs=[prompt, pl.BlockSpec((ns, d), lambda i: (0, 0))],
        out_shape=[jax.ShapeDtypeStruct((n, d), F32), jax.ShapeDtypeStruct((ns, d), F32)],
        compiler_params=pltpu.CompilerParams(
            dimension_semantics=("arbitrary",), vmem_limit_bytes=VMEM_LIMIT),
        name="mlp_xattn",
    )(hp2d, hsm, g_xattn, w_xq, w_xo, xk, xv, g_mlp, w_up, w_down, g_final)


def _mem_rows(c):
    nb = c.shape[0]
    c = c.reshape(nb, N_MEM, N_X_HEADS, X_D_HALVES, LANES)
    return jnp.transpose(c, (0, 1, 3, 2, 4)).reshape(nb, N_MEM * X_D_HALVES * N_X_HEADS, LANES)


def kernel(x_prompt, x_sample, mem_prompt, cache_win_k, cache_win_v, state_ret, cache_mem_k, cache_mem_v,
           g_mix, w_in, attn_sinks, ret_gn_g, ret_gn_b, w_out, g_xattn, g_mem, w_xq, w_xk, w_xv, w_xo,
           g_mlp, w_up, w_down, g_final):
    depth = w_in.shape[0]
    assert depth == 1, "single-layer trunk"
    b, s, d = x_prompt.shape
    nb, ls, _ = x_sample.shape
    row = lambda a: a.reshape(1, -1)
    sinks = attn_sinks[0]
    gn_g, gn_b = row(ret_gn_g[0]), row(ret_gn_b[0])
    g_fin = row(g_final)

    mk, mv, mkb, mvb, w_in_b, w_out_b = _memory_kv(
        mem_prompt.reshape(b * N_MEM, d), row(g_mem[0]), w_xk[0], w_xv[0], w_in[0], w_out[0])
    hp, p_wk, p_wv, p_rs, w_up_b, w_dn_b, w_xq_b, w_xo_b = _prompt_mixer(
        x_prompt, row(g_mix[0]), w_in_b, w_out_b, sinks, gn_g, gn_b,
        (w_up[0], w_down[0], w_xq[0], w_xo[0]))
    hp = _prompt_xattn(hp, row(g_xattn[0]), w_xq_b, w_xo_b,
                       mkb.reshape(b, N_MEM, d), mvb.reshape(b, N_MEM, d))

    win_t = lambda c: jnp.transpose(c, (0, 2, 3, 1)).reshape(nb, ATT_KV_W, WINDOW)
    win_t_inv = lambda a: jnp.transpose(a.reshape(nb, N_KV_HEADS, HEAD_DIM, WINDOW),
                                        (0, 3, 1, 2)).reshape(1, nb, WINDOW, N_KV_HEADS, HEAD_DIM)
    hs, s_wk, s_wv, s_rs = _sample_mixer(
        x_sample, row(g_mix[0]), w_in_b, w_out_b, sinks, gn_g, gn_b,
        win_t(cache_win_k[0]), win_t(cache_win_v[0]), state_ret[0].reshape(nb, RET_QK_W, RET_V_DIM))

    y_prompt, y_sample = _mlp_xattn(
        hp.reshape(b * s, d), hs, row(g_xattn[0]), w_xq_b, w_xo_b,
        _mem_rows(cache_mem_k[0]), _mem_rows(cache_mem_v[0]), row(g_mlp[0]), w_up_b, w_dn_b, g_fin)
    y_prompt = y_prompt.reshape(b, s, d)
    y_sample = y_sample.reshape(nb, ls, d)

    win5 = lambda a, n: a.reshape(1, n, WINDOW, N_KV_HEADS, HEAD_DIM)
    ret5 = lambda a, n: a.reshape(1, n, N_RET_HEADS, RET_QK_DIM, RET_V_DIM)
    mem5 = lambda a: jnp.transpose(a.reshape(b, N_MEM, X_D_HALVES, N_X_HEADS, LANES),
                                   (0, 1, 3, 2, 4)).reshape(1, b, N_MEM, N_X_HEADS, X_HEAD_DIM)
    return (y_prompt, y_sample,
            win5(p_wk, b), win5(p_wv, b), ret5(p_rs, b), mem5(mk), mem5(mv),
            win_t_inv(s_wk), win_t_inv(s_wv), ret5(s_rs, nb))
```

```python
import functools

import jax
import jax.numpy as jnp
import numpy as np
from jax import lax
from jax.experimental import pallas as pl
from jax.experimental.pallas import tpu as pltpu

F32 = jnp.float32
BF16 = jnp.bfloat16

D_MODEL = 1024
BATCH = 8
SEQ = 2048
DEC_BATCH = 128
DEC_SEQ = 4
HEAD_DIM = 64
N_ATT_HEADS = 8
N_KV_HEADS = 2
KV_GROUP = N_ATT_HEADS // N_KV_HEADS
WINDOW = 128
BLK = 128
N_RET_HEADS = 4
RET_QK_DIM = 64
RET_V_DIM = 128
N_MEM = 256
N_X_HEADS = 4
X_HEAD_DIM = D_MODEL // N_X_HEADS
D_FF = 4 * D_MODEL
RMS_EPS = 1e-6
GN_EPS = 1e-5

ATT_Q_W = N_ATT_HEADS * HEAD_DIM
ATT_KV_W = N_KV_HEADS * HEAD_DIM
RET_QK_W = N_RET_HEADS * RET_QK_DIM
RET_V_W = N_RET_HEADS * RET_V_DIM
MIX_OUT = ATT_Q_W + RET_V_W
D_IN = ATT_Q_W + 2 * ATT_KV_W + 2 * RET_QK_W + 2 * RET_V_W
C_QA, C_KV, C_QKR, C_VR, C_GR = 0, 512, 768, 1280, 1792

LANES = 128
SUBLANES = 8
HALF = LANES // 2
X_D_HALVES = X_HEAD_DIM // LANES
NEG = -1e30
VMEM_LIMIT = 56 * 1024 * 1024

TM_MIX = 512
TM_X = 2048
SUB_ROWS = 512
FF_CHUNK = 1024
BB_MIX = 32
BB_X = 4

NEG_SLOPES = [-(2.0 ** (-8.0 * (i + 1) / N_ATT_HEADS)) for i in range(N_ATT_HEADS)]
_LOG_G = np.log(1.0 - 2.0 ** (-5.0 - np.arange(N_RET_HEADS))).astype(np.float32).astype(np.float64)


def _prompt_tables():
    qi = np.arange(BLK)[:, None]
    kj = np.arange(2 * BLK)[None, :]
    dist = (qi + BLK - kj).astype(np.float64)
    mask = np.where((dist >= 0) & (dist < WINDOW), 0.0, NEG)
    l = np.arange(BLK, dtype=np.float64)
    diff = l[:, None] - l[None, :]
    decay = np.where(diff >= 0, np.exp(_LOG_G[:, None, None] * np.maximum(diff, 0.0)), 0.0)
    xi = np.exp((l[:, None] + 1.0) * _LOG_G[None, :])
    zeta = np.exp((BLK - 1.0 - l)[:, None] * _LOG_G[None, :])
    xi_t = np.repeat(xi, RET_V_DIM, axis=1)
    zeta_t = np.repeat(zeta, RET_QK_DIM, axis=1)
    f = lambda a: np.asarray(a, np.float32)
    return f(dist), f(mask), f(decay), f(xi_t), f(zeta_t)


def _sample_tables():
    slopes = -np.asarray(NEG_SLOPES)
    bias = np.full((2, N_ATT_HEADS * SUBLANES, 2 * BLK), NEG, np.float64)
    dec = np.zeros((2, N_RET_HEADS * SUBLANES, BLK), np.float64)
    xi = np.zeros((2, N_RET_HEADS * SUBLANES, RET_V_DIM), np.float64)
    zeta = np.zeros((2, SUBLANES, RET_QK_W), np.float64)
    for par in range(2):
        for r in range(SUBLANES):
            own = DEC_SEQ * par <= r < DEC_SEQ * (par + 1)
            t = r - DEC_SEQ * par if own else r % DEC_SEQ
            for h in range(N_ATT_HEADS):
                row = h * SUBLANES + r
                for j in range(WINDOW):
                    d = t + WINDOW - j
                    if 0 <= d < WINDOW:
                        bias[par, row, j] = -slopes[h] * d
                for c in range(DEC_SEQ):
                    d = t - c
                    if d >= 0:
                        bias[par, row, WINDOW + DEC_SEQ * par + c] = -slopes[h] * d
            for h in range(N_RET_HEADS):
                row = h * SUBLANES + r
                if own:
                    xi[par, row, :] = np.exp((t + 1.0) * _LOG_G[h])
                    zeta[par, r, h * RET_QK_DIM:(h + 1) * RET_QK_DIM] = np.exp((DEC_SEQ - 1.0 - t) * _LOG_G[h])
                    for c in range(t + 1):
                        dec[par, row, DEC_SEQ * par + c] = np.exp(_LOG_G[h] * (t - c))
    f = lambda a: np.asarray(a, np.float32)
    return f(bias), f(dec), f(xi), f(zeta)


_P_DIST, _P_MASK, _P_DECAY, _P_XI, _P_ZETA = _prompt_tables()
_S_BIAS, _S_DEC, _S_XI, _S_ZETA = _sample_tables()
_GL_PROMPT = [float(np.exp(_LOG_G[h] * BLK)) for h in range(N_RET_HEADS)]
_GL_SAMPLE = [float(np.exp(_LOG_G[h] * DEC_SEQ)) for h in range(N_RET_HEADS)]


def _rms(x, g):
    return x * lax.rsqrt(jnp.mean(x * x, axis=-1, keepdims=True) + RMS_EPS) * g


def _dot(a, b):
    return jnp.dot(a, b, preferred_element_type=F32)


def _dot_nt(a, b):
    return lax.dot_general(a, b, (((1,), (1,)), ((), ())), preferred_element_type=F32)


def _dot_tn(a, b):
    return lax.dot_general(a, b, (((0,), (0,)), ((), ())), preferred_element_type=F32)


def _silu(g):
    return g * (1.0 / (1.0 + jnp.exp(-g)))


def _half_masks(width):
    lane = lax.broadcasted_iota(jnp.int32, (1, width), 1)
    lo = ((lane & (LANES - 1)) < HALF).astype(F32)
    return lo, 1.0 - lo


def _sink_softmax(s, sink):
    m = jnp.maximum(jnp.max(s, axis=-1, keepdims=True), sink)
    p = jnp.exp(s - m)
    den = jnp.sum(p, axis=-1, keepdims=True) + jnp.exp(sink - m)
    return p * (1.0 / den)


def _group_norm(o, g, b):
    mu = jnp.mean(o, axis=-1, keepdims=True)
    d = o - mu
    var = jnp.mean(d * d, axis=-1, keepdims=True)
    return d * lax.rsqrt(var + GN_EPS) * g + b


def _pm_project_stages(x, slot, gmix_ref, win_ref, sc, kv_out=None):
    tm = x.shape[0]
    xn = _rms(x, gmix_ref[...]).astype(BF16)

    pw = 2 * LANES
    lo, hi = _half_masks(pw)

    def stage_q(i):
        cols = slice(i * pw, (i + 1) * pw)
        q = _dot(xn, win_ref[:, C_QA + i * pw:C_QA + (i + 1) * pw])
        sc["qlo"][slot, :, cols] = (q * (lo * HEAD_DIM ** -0.5)).astype(BF16)
        sc["qhi"][slot, :, cols] = (q * (hi * HEAD_DIM ** -0.5)).astype(BF16)

    def stage_kv():
        z = _dot(xn, win_ref[:, C_KV:C_KV + pw])
        low = lax.broadcasted_iota(jnp.int32, (tm, LANES), 1) < HALF
        k = z[:, 0:ATT_KV_W]
        v = z[:, ATT_KV_W:2 * ATT_KV_W]
        if kv_out is not None:
            kv_out[0][0] = k[tm - WINDOW:, :].T
            kv_out[1][0] = v[tm - WINDOW:, :].T
        k_r = pltpu.roll(k, HALF, axis=1)
        v_r = pltpu.roll(v, HALF, axis=1)
        sc["kd0"][slot] = jnp.where(low, k, k_r).astype(BF16)
        sc["kd1"][slot] = jnp.where(low, k_r, k).astype(BF16)
        sc["vd0"][slot, :, 0:LANES] = jnp.where(low, v, 1.0).astype(BF16)
        sc["vd0"][slot, :, LANES:2 * LANES] = jnp.where(low, 1.0, v_r).astype(BF16)
        sc["vd1"][slot, :, 0:LANES] = jnp.where(low, v_r, 1.0).astype(BF16)
        sc["vd1"][slot, :, LANES:2 * LANES] = jnp.where(low, 1.0, v).astype(BF16)

    def stage_qr():
        qr = _dot(xn, win_ref[:, C_QKR:C_QKR + pw])
        sc["qrlo"][slot] = (qr * lo).astype(BF16)
        sc["qrhi"][slot] = (qr * hi).astype(BF16)

    def stage_kr():
        sc["kr"][slot] = _dot(xn, win_ref[:, C_QKR + pw:C_QKR + 2 * pw]) * (RET_QK_DIM ** -0.5)

    def stage_vr(i):
        cols = slice(i * pw, (i + 1) * pw)
        sc["vr"][slot, :, cols] = _dot(xn, win_ref[:, C_VR + i * pw:C_VR + (i + 1) * pw]).astype(BF16)

    def stage_gate(i):
        cols = slice(i * pw, (i + 1) * pw)
        sc["gate"][slot, :, cols] = _silu(_dot(xn, win_ref[:, C_GR + i * pw:C_GR + (i + 1) * pw]))

    part = functools.partial
    return [part(stage_q, 0), part(stage_q, 1), stage_kv, stage_qr, stage_kr,
            part(stage_vr, 0), part(stage_vr, 1), part(stage_gate, 0), part(stage_gate, 1)]


def _pm_last_block(slot, tm, sc):
    rows = slice(tm - BLK, tm)
    return ([sc["kd0"][slot, rows, :], sc["kd1"][slot, rows, :]],
            [sc["vd0"][slot, rows, :], sc["vd1"][slot, rows, :]])


def _pm_blocks(slot, prev_kd, prev_vd, is_first, state, fillers, tm, sinks_ref, gng_ref, gnb_ref,
               decay_ref, xi_ref, zeta_ref, sc):
    nblk = tm // BLK
    n_units = nblk * (N_KV_HEADS + N_RET_HEADS // 2)
    pending = list(fillers)
    done_units = [0]

    def unit_done():
        done_units[0] += 1
        while pending and (len(fillers) - len(pending)) * n_units < done_units[0] * len(fillers):
            pending.pop(0)()
    lowb = lax.broadcasted_iota(jnp.int32, (BLK, LANES), 1) < HALF
    col = lax.broadcasted_iota(jnp.int32, (BLK, 2 * BLK), 1)
    first_mask = None if is_first is False else jnp.where((col < BLK) & is_first, NEG, 0.0)
    kd_refs = (sc["kd0"], sc["kd1"])
    vd_refs = (sc["vd0"], sc["vd1"])
    qlo, qhi, mix = sc["qlo"], sc["qhi"], sc["mix"]
    n_pairs = N_RET_HEADS // 2

    for j in range(nblk):
        rows = slice(j * BLK, (j + 1) * BLK)
        c0s = [kvh * KV_GROUP * HEAD_DIM for kvh in range(N_KV_HEADS)]
        lsls = [slice(i * LANES, (i + 1) * LANES) for i in range(n_pairs)]
        vds, scores = [], []
        for kvh in range(N_KV_HEADS):
            if j == 0:
                kd = jnp.concatenate([prev_kd[kvh], kd_refs[kvh][slot, rows, :]], axis=0)
                vds.append(jnp.concatenate([prev_vd[kvh], vd_refs[kvh][slot, rows, :]], axis=0))
            else:
                krows = slice((j - 1) * BLK, (j + 1) * BLK)
                kd = kd_refs[kvh][slot, krows, :]
                vds.append(vd_refs[kvh][slot, krows, :])
            c0 = c0s[kvh]
            qst = jnp.concatenate([qlo[slot, rows, c0:c0 + LANES], qhi[slot, rows, c0:c0 + LANES],
                                   qlo[slot, rows, c0 + LANES:c0 + 2 * LANES],
                                   qhi[slot, rows, c0 + LANES:c0 + 2 * LANES]], axis=0)
            scores.append(_dot_nt(qst, kd))
        unit_done()

        kps = [sc["kr"][slot, rows, lsls[i]] for i in range(n_pairs)]
        vpairs = [sc["vr"][slot, rows, 2 * i * RET_V_DIM:(2 * i + 2) * RET_V_DIM] for i in range(n_pairs)]
        q2s = [jnp.concatenate([sc["qrlo"][slot, rows, lsls[i]], sc["qrhi"][slot, rows, lsls[i]]], axis=0)
               for i in range(n_pairs)]
        a_s = [_dot_nt(q2s[i], kps[i].astype(BF16)) for i in range(n_pairs)]
        ocs = [_dot(q2s[i], state[i].astype(BF16)) for i in range(n_pairs)]
        us = [_dot_tn((kps[i] * zeta_ref[:, lsls[i]]).astype(BF16), vpairs[i]) for i in range(n_pairs)]
        unit_done()

        for kvh in range(N_KV_HEADS):
            s, vd, c0 = scores[kvh], vds[kvh], c0s[kvh]
            es, esink = [], []
            for g in range(KV_GROUP):
                h = kvh * KV_GROUP + g
                sg = s[g * BLK:(g + 1) * BLK] + sc["bias"][h]
                if j == 0 and first_mask is not None:
                    sg = sg + first_mask
                sink = sinks_ref[h]
                m = jnp.maximum(jnp.max(sg, axis=-1, keepdims=True), sink)
                es.append(jnp.exp(sg - m).astype(BF16))
                esink.append(jnp.exp(sink - m))
            o = _dot(jnp.concatenate(es, axis=0), vd)
            for pair in range(KV_GROUP // 2):
                oe = o[2 * pair * BLK:(2 * pair + 1) * BLK]
                oo = o[(2 * pair + 1) * BLK:(2 * pair + 2) * BLK]
                num = jnp.where(lowb, oe[:, :LANES], oo[:, LANES:])
                den = (jnp.where(lowb, oe[:, LANES:], oo[:, :LANES])
                       + jnp.where(lowb, esink[2 * pair], esink[2 * pair + 1]))
                cs = c0 + pair * LANES
                mix[slot, rows, cs:cs + LANES] = (num * (1.0 / den)).astype(BF16)
            unit_done()

        for i in range(n_pairs):
            a, oc, u, sp = a_s[i], ocs[i], us[i], state[i]
            inner = jnp.concatenate([a[:BLK] * decay_ref[2 * i], a[BLK:] * decay_ref[2 * i + 1]], axis=0)
            oi = _dot(inner.astype(BF16), vpairs[i])
            for half in range(2):
                h = 2 * i + half
                vsl = slice(h * RET_V_DIM, (h + 1) * RET_V_DIM)
                hr = slice(half * BLK, (half + 1) * BLK)
                o = oi[hr, half * RET_V_DIM:(half + 1) * RET_V_DIM] + oc[hr] * xi_ref[:, vsl]
                r = _group_norm(o, gng_ref[:, vsl], gnb_ref[:, vsl]) * sc["gate"][slot, rows, vsl]
                mix[slot, rows, ATT_Q_W + h * RET_V_DIM:ATT_Q_W + (h + 1) * RET_V_DIM] = r.astype(BF16)
            state[i] = jnp.concatenate(
                [_GL_PROMPT[2 * i] * sp[:RET_QK_DIM] + u[:RET_QK_DIM, :RET_V_DIM],
                 _GL_PROMPT[2 * i + 1] * sp[RET_QK_DIM:] + u[RET_QK_DIM:, RET_V_DIM:]], axis=0)

    assert not pending
    return state


def _pm_wout_pieces(slot, x_ref, rows, wout_ref, h_ref, sc):
    pw = 2 * LANES
    n = D_MODEL // pw
    parts = []

    def piece(k):
        parts.append(_dot(sc["mix"][slot], wout_ref[:, k * pw:(k + 1) * pw]))
        if k == n - 1:
            h_ref[rows, :] = x_ref[rows, :] + jnp.concatenate(parts, axis=1)

    return [functools.partial(piece, k) for k in range(n)]


def _prompt_mixer_kernel(sinks_ref, xpair_ref, xnext_ref, gmix_ref, win_ref, wout_ref, gng_ref, gnb_ref,
                           dist_ref, mask_ref, decay_ref, xi_ref, zeta_ref,
                           wupf_ref, wdnf_ref, wqf_ref, wof_ref,
                           h_ref, wk_ref, wv_ref, st_ref,
                           wupb_ref, wdnb_ref, wqb_ref, wob_ref,
                           qlo_s, qhi_s, kd0_s, kd1_s, vd0_s, vd1_s,
                           qrlo_s, qrhi_s, kr_s, vr_s, gate_s, mix_s, bias_s, state_s):
    u = pl.program_id(0)
    tm = xnext_ref.shape[0]
    wupb_ref[...] = wupf_ref[...].astype(BF16)
    wdnb_ref[...] = wdnf_ref[...].astype(BF16)
    wqb_ref[...] = wqf_ref[...].astype(BF16)
    wob_ref[...] = wof_ref[...].astype(BF16)
    sc = dict(qlo=qlo_s, qhi=qhi_s, kd0=kd0_s, kd1=kd1_s, vd0=vd0_s, vd1=vd1_s, qrlo=qrlo_s, qrhi=qrhi_s,
              kr=kr_s, vr=vr_s, gate=gate_s, mix=mix_s, bias=bias_s)
    n_pairs = N_RET_HEADS // 2
    blocks = functools.partial(_pm_blocks, tm=tm, sinks_ref=sinks_ref, gng_ref=gng_ref,
                               gnb_ref=gnb_ref, decay_ref=decay_ref, xi_ref=xi_ref, zeta_ref=zeta_ref, sc=sc)

    @pl.when(u == 0)
    def _():
        for h in range(N_ATT_HEADS):
            bias_s[h] = NEG_SLOPES[h] * dist_ref[...] + mask_ref[...]
        state_s[...] = jnp.zeros_like(state_s)
        kd0_s[1] = jnp.zeros(kd0_s.shape[1:], BF16)
        kd1_s[1] = jnp.zeros(kd1_s.shape[1:], BF16)
        vd0_s[1] = jnp.zeros(vd0_s.shape[1:], BF16)
        vd1_s[1] = jnp.zeros(vd1_s.shape[1:], BF16)
        for stage in _pm_project_stages(xpair_ref[0:tm, :], 0, gmix_ref, win_ref, sc):
            stage()

    seq_start = (u % 2) == 0
    state = [jnp.where(seq_start, 0.0, state_s[i * LANES:(i + 1) * LANES, :]) for i in range(n_pairs)]

    prev_kd, prev_vd = _pm_last_block(1, tm, sc)
    stages = _pm_project_stages(xpair_ref[tm:2 * tm, :], 1, gmix_ref, win_ref, sc, kv_out=(wk_ref, wv_ref))
    state = blocks(0, prev_kd, prev_vd, seq_start, state, stages)
    wout0 = _pm_wout_pieces(0, xpair_ref, slice(0, tm), wout_ref, h_ref, sc)

    prev_kd, prev_vd = _pm_last_block(0, tm, sc)
    stages = _pm_project_stages(xnext_ref[...], 0, gmix_ref, win_ref, sc)
    state = blocks(1, prev_kd, prev_vd, False, state, wout0 + stages)
    for piece in _pm_wout_pieces(1, xpair_ref, slice(tm, 2 * tm), wout_ref, h_ref, sc):
        piece()

    for i in range(n_pairs):
        state_s[i * LANES:(i + 1) * LANES, :] = state[i]
        st_ref[0, i * LANES:(i + 1) * LANES, :] = state[i]


def _prompt_mixer(x, g_mix, w_in, w_out, sinks, gn_g, gn_b, side_f32):
    b, s, d = x.shape
    tm = TM_MIX
    n_tiles = b * s // tm
    steps = n_tiles // 2
    seq_steps = s // (2 * tm)
    assert s % (2 * tm) == 0 and seq_steps == 2, "kernel assumes 4 tiles per sequence"
    x2d = x.reshape(b * s, d)
    const = lambda shape: pl.BlockSpec(shape, lambda i: (0,) * len(shape), pipeline_mode=pl.Buffered(1))
    slot2 = lambda rows, cols, dt: pltpu.VMEM((2, rows, cols), dt)
    side_specs = [pl.BlockSpec((w.shape[0] // steps, w.shape[1]), lambda i: (i, 0)) for w in side_f32]
    outs = pl.pallas_call(
        _prompt_mixer_kernel,
        grid=(steps,),
        in_specs=[
            pl.BlockSpec(memory_space=pltpu.SMEM),
            pl.BlockSpec((2 * tm, d), lambda i: (i, 0)),
            pl.BlockSpec((tm, d), lambda i: (jnp.minimum(2 * i + 2, n_tiles - 1), 0)),
            const((1, d)), const((d, D_IN)), const((MIX_OUT, d)),
            const((1, RET_V_W)), const((1, RET_V_W)),
            const((BLK, 2 * BLK)), const((BLK, 2 * BLK)),
            const((N_RET_HEADS, BLK, BLK)), const((BLK, RET_V_W)), const((BLK, RET_QK_W)),
        ] + side_specs,
        out_specs=[
            pl.BlockSpec((2 * tm, d), lambda i: (i, 0)),
            pl.BlockSpec((1, WINDOW, ATT_KV_W), lambda i: (i // seq_steps, 0, 0)),
            pl.BlockSpec((1, WINDOW, ATT_KV_W), lambda i: (i // seq_steps, 0, 0)),
            pl.BlockSpec((1, RET_QK_W, RET_V_DIM), lambda i: (i // seq_steps, 0, 0)),
        ] + side_specs,
        out_shape=[
            jax.ShapeDtypeStruct((b * s, d), F32),
            jax.ShapeDtypeStruct((b, WINDOW, ATT_KV_W), F32),
            jax.ShapeDtypeStruct((b, WINDOW, ATT_KV_W), F32),
            jax.ShapeDtypeStruct((b, RET_QK_W, RET_V_DIM), F32),
        ] + [jax.ShapeDtypeStruct(w.shape, BF16) for w in side_f32],
        scratch_shapes=[
            slot2(tm, ATT_Q_W, BF16), slot2(tm, ATT_Q_W, BF16),
            slot2(tm, LANES, BF16), slot2(tm, LANES, BF16),
            slot2(tm, 2 * LANES, BF16), slot2(tm, 2 * LANES, BF16),
            slot2(tm, RET_QK_W, BF16), slot2(tm, RET_QK_W, BF16),
            slot2(tm, RET_QK_W, F32), slot2(tm, RET_V_W, BF16),
            slot2(tm, RET_V_W, F32), slot2(tm, MIX_OUT, BF16),
            pltpu.VMEM((N_ATT_HEADS, BLK, 2 * BLK), F32),
            pltpu.VMEM((RET_QK_W, RET_V_DIM), F32),
        ],
        compiler_params=pltpu.CompilerParams(
            dimension_semantics=("arbitrary",), vmem_limit_bytes=VMEM_LIMIT),
        name="prompt_mixer",
    )(sinks, x2d, x2d, g_mix, w_in, w_out, gn_g, gn_b,
      jnp.asarray(_P_DIST), jnp.asarray(_P_MASK), jnp.asarray(_P_DECAY), jnp.asarray(_P_XI),
      jnp.asarray(_P_ZETA), *side_f32)
    return (outs[0].reshape(b, s, d),) + tuple(outs[1:])


def _memkv_kernel(mem_ref, g_ref, wk_ref, wv_ref, win_ref, wout_ref,
                  mk_ref, mv_ref, mkb_ref, mvb_ref, winb_ref, woutb_ref):
    winb_ref[...] = win_ref[...].astype(BF16)
    woutb_ref[...] = wout_ref[...].astype(BF16)
    mn = _rms(mem_ref[...], g_ref[...]).astype(BF16)
    mk = _dot(mn, wk_ref[...].astype(BF16))
    mv = _dot(mn, wv_ref[...].astype(BF16))
    tm = mem_ref.shape[0]
    group = X_D_HALVES * N_X_HEADS
    for hd in range(N_X_HEADS):
        for dh in range(X_D_HALVES):
            cols = slice(hd * X_HEAD_DIM + dh * LANES, hd * X_HEAD_DIM + (dh + 1) * LANES)
            rows = pl.ds(dh * N_X_HEADS + hd, tm, stride=group)
            mk_ref[rows, :] = mk[:, cols]
            mv_ref[rows, :] = mv[:, cols]
    mkb_ref[...] = mk.astype(BF16)
    mvb_ref[...] = mv.astype(BF16)


def _memory_kv(mem2d, g_mem, w_xk, w_xv, w_in, w_out):
    n, d = mem2d.shape
    tm = 512
    row = pl.BlockSpec((tm, d), lambda i: (i, 0))
    rows_out = pl.BlockSpec((tm * d // LANES, LANES), lambda i: (i, 0))
    const = lambda shape: pl.BlockSpec(shape, lambda i: (0,) * len(shape), pipeline_mode=pl.Buffered(1))
    steps = n // tm
    win_blk = pl.BlockSpec((w_in.shape[0] // steps, w_in.shape[1]), lambda i: (i, 0))
    wout_blk = pl.BlockSpec((w_out.shape[0] // steps, w_out.shape[1]), lambda i: (i, 0))
    return pl.pallas_call(
        _memkv_kernel,
        grid=(n // tm,),
        in_specs=[row, const((1, d)), const((d, d)), const((d, d)), win_blk, wout_blk],
        out_specs=[rows_out, rows_out, row, row, win_blk, wout_blk],
        out_shape=[jax.ShapeDtypeStruct((n * d // LANES, LANES), F32),
                   jax.ShapeDtypeStruct((n * d // LANES, LANES), F32),
                   jax.ShapeDtypeStruct((n, d), BF16), jax.ShapeDtypeStruct((n, d), BF16),
                   jax.ShapeDtypeStruct(w_in.shape, BF16), jax.ShapeDtypeStruct(w_out.shape, BF16)],
        compiler_params=pltpu.CompilerParams(
            dimension_semantics=("arbitrary",), vmem_limit_bytes=VMEM_LIMIT),
        name="memory_kv",
    )(mem2d, g_mem, w_xk, w_xv, w_in, w_out)


def _prompt_xattn_kernel(h_ref, g_ref, wq_ref, wo_ref, mk_ref, mv_ref, out_ref, o_s):
    def stages(r0):
        rows = slice(r0, r0 + SUB_ROWS)
        env = {}

        def project():
            env["h"] = h_ref[0, rows, :]
            xn = _rms(env["h"], g_ref[...]).astype(BF16)
            env["q"] = (_dot(xn, wq_ref[...]) * (X_HEAD_DIM ** -0.5)).astype(BF16)

        def scores(hd):
            sl = slice(hd * X_HEAD_DIM, (hd + 1) * X_HEAD_DIM)
            env[hd] = _dot_nt(env["q"][:, sl], mk_ref[0, :, sl])

        def head(hd):
            if hd + 1 < N_X_HEADS:
                scores(hd + 1)
            sl = slice(hd * X_HEAD_DIM, (hd + 1) * X_HEAD_DIM)
            s = env.pop(hd)
            m = jnp.max(s, axis=-1, keepdims=True)
            p = jnp.exp(s - m)
            p = p * (1.0 / jnp.sum(p, axis=-1, keepdims=True))
            o_s[rows, sl] = _dot(p.astype(BF16), mv_ref[0, :, sl]).astype(BF16)

        def output():
            out_ref[0, rows, :] = env["h"] + _dot(o_s[rows, :], wo_ref[...])

        def project_and_first_scores():
            project()
            scores(0)

        return ([project_and_first_scores] + [functools.partial(head, hd) for hd in range(N_X_HEADS)]
                + [output])

    chains = [stages(r0) for r0 in range(0, h_ref.shape[1], SUB_ROWS)]
    n_stage = len(chains[0])
    for step in range(n_stage + len(chains) - 1):
        for lag, chain in enumerate(chains):
            if 0 <= step - lag < n_stage:
                chain[step - lag]()


def _prompt_xattn(h, g, w_xq, w_xo, mkb, mvb):
    b, s, d = h.shape
    tm = TM_X
    const = lambda shape: pl.BlockSpec(shape, lambda i, j: (0,) * len(shape))
    tok = pl.BlockSpec((1, tm, d), lambda i, j: (i, j, 0))
    mem = pl.BlockSpec((1, N_MEM, d), lambda i, j: (i, 0, 0))
    return pl.pallas_call(
        _prompt_xattn_kernel,
        grid=(b, s // tm),
        in_specs=[tok, const((1, d)), const((d, d)), const((d, d)), mem, mem],
        out_specs=tok,
        out_shape=jax.ShapeDtypeStruct((b, s, d), F32),
        scratch_shapes=[pltpu.VMEM((tm, d), BF16)],
        compiler_params=pltpu.CompilerParams(
            dimension_semantics=("arbitrary", "arbitrary"), vmem_limit_bytes=VMEM_LIMIT),
        name="prompt_xattn",
    )(h, g, w_xq, w_xo, mkb, mvb)


def _sample_mixer_kernel(sinks_ref, x_ref, gmix_ref, win_ref, wout_ref, gng_ref, gnb_ref,
                         ck_ref, cv_ref, st_ref, bias_ref, dec_ref, xi_ref, zeta_ref,
                         h_ref, swk_ref, swv_ref, sst_ref):
    bb = ck_ref.shape[0]
    nt = bb // 2
    x = x_ref[...].reshape(bb * DEC_SEQ, D_MODEL)
    xn = _rms(x, gmix_ref[...]).astype(BF16)
    tile3 = lambda a: a.reshape(nt, SUBLANES, a.shape[-1])

    q = _dot(xn, win_ref[:, C_QA:C_QA + ATT_Q_W]) * (HEAD_DIM ** -0.5)
    kv = _dot(xn, win_ref[:, C_KV:C_KV + 2 * ATT_KV_W])
    qkr = _dot(xn, win_ref[:, C_QKR:C_QKR + 2 * RET_QK_W])
    vr = _dot(xn, win_ref[:, C_VR:C_VR + RET_V_W])
    gate3 = tile3(_silu(_dot(xn, win_ref[:, C_GR:C_GR + RET_V_W])))

    lo512, hi512 = _half_masks(ATT_Q_W)
    q_r = pltpu.roll(q, HALF, axis=1)
    q_nat3 = tile3(q)
    q_rot3 = tile3(q_r)
    lo3 = lo512.reshape(1, 1, ATT_Q_W)
    hi3 = hi512.reshape(1, 1, ATT_Q_W)
    qa3 = (q_nat3 * lo3).astype(BF16)
    qb3 = (q_rot3 * lo3).astype(BF16)
    qc3 = (q_rot3 * hi3).astype(BF16)
    qd3 = (q_nat3 * hi3).astype(BF16)
    t128 = lambda a, i: a[:, :, i * LANES:(i + 1) * LANES]
    qs = jnp.concatenate([t128(qa3, 0), t128(qb3, 1), t128(qa3, 1), t128(qb3, 2),
                          t128(qc3, 2), t128(qd3, 2), t128(qc3, 3), t128(qd3, 3)], axis=1)

    k3 = tile3(kv[:, :ATT_KV_W])
    v3 = tile3(kv[:, ATT_KV_W:])
    pad_kv = jnp.zeros((nt, BLK - SUBLANES, LANES), BF16)
    knew_pad = jnp.concatenate([k3.astype(BF16), pad_kv], axis=1)
    vnew_pad = jnp.concatenate([v3.astype(BF16), pad_kv], axis=1)
    to_lanes = lambda a3: jnp.swapaxes(
        jnp.concatenate([a3, jnp.zeros((nt, BLK - SUBLANES, LANES), F32)], axis=1), 1, 2)
    k3t, v3t = to_lanes(k3), to_lanes(v3)
    roll3 = lambda a, sh: pltpu.roll(a.reshape(nt * BLK, LANES), sh, axis=1).reshape(nt, BLK, LANES)

    lo256, _ = _half_masks(RET_QK_W)
    qr3 = tile3(qkr[:, :RET_QK_W])
    kr3 = tile3(qkr[:, RET_QK_W:] * (RET_QK_DIM ** -0.5))
    vr3 = tile3(vr)
    lane256 = lax.broadcasted_iota(jnp.int32, (1, 1, RET_QK_W), 2)
    qrs = jnp.concatenate(
        [(qr3 * ((lane256 >= h * RET_QK_DIM) & (lane256 < (h + 1) * RET_QK_DIM)).astype(F32)).astype(BF16)
         for h in range(N_RET_HEADS)],
        axis=1)
    kr_pad = jnp.concatenate([kr3.astype(BF16), jnp.zeros((nt, BLK - SUBLANES, RET_QK_W), BF16)], axis=1)
    vr_pad = jnp.concatenate([vr3.astype(BF16), jnp.zeros((nt, BLK - SUBLANES, RET_V_W), BF16)], axis=1)

    lane = lax.broadcasted_iota(jnp.int32, (1, 1, LANES), 2)
    row8 = lax.broadcasted_iota(jnp.int32, (1, SUBLANES, 1), 1)
    bmm_nt = lambda a, b: jnp.einsum('bqd,bkd->bqk', a, b, preferred_element_type=F32)
    bmm = lambda a, b: jnp.einsum('bqk,bkd->bqd', a, b, preferred_element_type=F32)

    att_par, ret_par = [], []
    for par in range(2):
        bsl = pl.ds(par, nt, stride=2)
        ckt = ck_ref[bsl]
        cvt = cv_ref[bsl]
        keep = lane < WINDOW - DEC_SEQ
        new_shift = WINDOW - DEC_SEQ - DEC_SEQ * par
        swk_ref[bsl] = jnp.where(keep, roll3(ckt, WINDOW - DEC_SEQ), roll3(k3t, new_shift))
        swv_ref[bsl] = jnp.where(keep, roll3(cvt, WINDOW - DEC_SEQ), roll3(v3t, new_shift))

        s = jnp.concatenate([bmm(qs, ckt.astype(BF16)), bmm_nt(qs, knew_pad)], axis=2) + bias_ref[par]
        ps = []
        for h in range(N_ATT_HEADS):
            ps.append(_sink_softmax(s[:, h * SUBLANES:(h + 1) * SUBLANES, :], sinks_ref[h]).astype(BF16))
        p_all = jnp.concatenate(ps, axis=1)
        o = bmm_nt(p_all[:, :, :BLK], cvt.astype(BF16)) + bmm(p_all[:, :, BLK:], vnew_pad)
        o_r = pltpu.roll(o.reshape(nt * N_ATT_HEADS * SUBLANES, LANES), HALF, axis=1).reshape(o.shape)
        hr = lambda a, h: a[:, h * SUBLANES:(h + 1) * SUBLANES, :]
        low = lane < HALF
        att_par.append(jnp.concatenate([
            jnp.where(low, hr(o, 0), hr(o_r, 1)), jnp.where(low, hr(o, 2), hr(o_r, 3)),
            jnp.where(low, hr(o_r, 4), hr(o, 5)), jnp.where(low, hr(o_r, 6), hr(o, 7))], axis=2))

        st = st_ref[bsl]
        oc = bmm(qrs, st.astype(BF16))
        inner = (bmm_nt(qrs, kr_pad) * dec_ref[par]).astype(BF16)
        oi = bmm(inner, vr_pad)
        rs = []
        for h in range(N_RET_HEADS):
            vsl = slice(h * RET_V_DIM, (h + 1) * RET_V_DIM)
            rsl = slice(h * SUBLANES, (h + 1) * SUBLANES)
            o_h = oi[:, rsl, vsl] + oc[:, rsl, :] * xi_ref[par, rsl, :]
            rs.append(_group_norm(o_h, gng_ref[:, vsl], gnb_ref[:, vsl]) * gate3[:, :, vsl])
        ret_par.append(jnp.concatenate(rs, axis=2))

        kz3 = (kr3 * zeta_ref[par]).astype(BF16)
        vr3_b = vr3.astype(BF16)
        for p in range(nt):
            for i in range(N_RET_HEADS // 2):
                u = _dot_tn(kz3[p][:, i * LANES:(i + 1) * LANES],
                            vr3_b[p][:, 2 * i * RET_V_DIM:(2 * i + 2) * RET_V_DIM])
                for half in range(2):
                    h = 2 * i + half
                    dsl = slice(h * RET_QK_DIM, (h + 1) * RET_QK_DIM)
                    sst_ref[2 * p + par, dsl, :] = (
                        _GL_SAMPLE[h] * st[p, dsl, :]
                        + u[half * RET_QK_DIM:(half + 1) * RET_QK_DIM, half * RET_V_DIM:(half + 1) * RET_V_DIM])

    own0 = row8 < DEC_SEQ
    att3 = jnp.where(own0, att_par[0], att_par[1])
    ret3 = jnp.where(own0, ret_par[0], ret_par[1])
    mix = jnp.concatenate([att3, ret3], axis=2).reshape(2 * nt * DEC_SEQ, MIX_OUT).astype(BF16)
    h_ref[...] = x + _dot(mix, wout_ref[...])


def _sample_mixer(x3d, g_mix, w_in, w_out, sinks, gn_g, gn_b, ck, cv, st):
    nb, ls, d = x3d.shape
    n = nb * ls
    bb = BB_MIX
    r = bb * DEC_SEQ
    const = lambda shape: pl.BlockSpec(shape, lambda i: (0,) * len(shape))
    row = pl.BlockSpec((r, d), lambda i: (i, 0))
    win = pl.BlockSpec((bb, WINDOW, ATT_KV_W), lambda i: (i, 0, 0))
    state = pl.BlockSpec((bb, RET_QK_W, RET_V_DIM), lambda i: (i, 0, 0))
    return pl.pallas_call(
        _sample_mixer_kernel,
        grid=(nb // bb,),
        in_specs=[
            pl.BlockSpec(memory_space=pltpu.SMEM),
            pl.BlockSpec((bb, ls, d), lambda i: (i, 0, 0)), const((1, d)), const((d, D_IN)), const((MIX_OUT, d)),
            const((1, RET_V_W)), const((1, RET_V_W)),
            win, win, state,
            const(_S_BIAS.shape), const(_S_DEC.shape), const(_S_XI.shape), const(_S_ZETA.shape),
        ],
        out_specs=[row, win, win, state],
        out_shape=[
            jax.ShapeDtypeStruct((n, d), F32),
            jax.ShapeDtypeStruct((nb, WINDOW, ATT_KV_W), F32),
            jax.ShapeDtypeStruct((nb, WINDOW, ATT_KV_W), F32),
            jax.ShapeDtypeStruct((nb, RET_QK_W, RET_V_DIM), F32),
        ],
        compiler_params=pltpu.CompilerParams(
            dimension_semantics=("arbitrary",), vmem_limit_bytes=VMEM_LIMIT),
        name="sample_mixer",
    )(sinks, x3d, g_mix, w_in, w_out, gn_g, gn_b, ck, cv, st,
      jnp.asarray(_S_BIAS), jnp.asarray(_S_DEC), jnp.asarray(_S_XI), jnp.asarray(_S_ZETA))


def _head_slab(x_ref, b, hd):
    group = X_D_HALVES * N_X_HEADS
    halves = [x_ref[b, pl.ds(dh * N_X_HEADS + hd, N_MEM, stride=group), :] for dh in range(X_D_HALVES)]
    return jnp.concatenate(halves, axis=1).astype(BF16)


def _mlp_value(h, g_ref, wup_ref, wdn_ref, gf_ref, fillers=None):
    xn = _rms(h, g_ref[...]).astype(BF16)
    piece = FF_CHUNK // N_X_HEADS
    opiece = D_MODEL // N_X_HEADS
    n_chunks = D_FF // FF_CHUNK
    nofill = (None, None, None)

    def up(c):
        qk, softmax, _ = fillers[c] if fillers is not None else nofill
        hid = []
        for k in range(N_X_HEADS):
            cols = slice(c * FF_CHUNK + k * piece, c * FF_CHUNK + (k + 1) * piece)
            u = jnp.maximum(_dot(xn, wup_ref[:, cols]), 0.0)
            hid.append((u * u).astype(BF16))
            if qk is not None:
                qk(k)
        if softmax is not None:
            softmax()
        return jnp.concatenate(hid, axis=1)

    def down(c, hid):
        pv = (fillers[c] if fillers is not None else nofill)[2]
        rows_c = slice(c * FF_CHUNK, (c + 1) * FF_CHUNK)
        out = []
        for k in range(N_X_HEADS):
            out.append(_dot(hid, wdn_ref[rows_c, k * opiece:(k + 1) * opiece]))
            if pv is not None:
                pv(k)
        return jnp.concatenate(out, axis=1)

    acc = h
    hid = up(0)
    for c in range(n_chunks):
        nxt = up(c + 1) if c + 1 < n_chunks else None
        acc = acc + down(c, hid)
        hid = nxt
    return _rms(acc, gf_ref[...])


def _mlp_xattn_kernel(hp_ref, hsm_ref, gx_ref, wq_ref, wo_ref, xk_ref, xv_ref, g_ref, wup_ref, wdn_ref, gf_ref,
                      yp_ref, ys_ref):
    i = pl.program_id(0)
    n = pl.num_programs(0) - 1
    bb = xk_ref.shape[0]
    assert bb == D_FF // FF_CHUNK and bb % 2 == 0

    @pl.when(i == 0)
    def _():
        xn = _rms(hsm_ref[...], gx_ref[...]).astype(BF16)
        ys_ref[...] = (_dot(xn, wq_ref[...]) * (X_HEAD_DIM ** -0.5)).reshape(ys_ref.shape)

    @pl.when(i < n)
    def _():
        own0 = lax.broadcasted_iota(jnp.int32, (SUBLANES, 1), 0) < DEC_SEQ
        o_rows = {}

        def attend(b):
            t = b // 2
            tile_b = pl.ds(i * bb + 2 * t, 2)
            env = dict(s=[], o=[])

            def qk(hd):
                if hd == 0:
                    env["q"] = ys_ref[tile_b].reshape(SUBLANES, D_MODEL).astype(BF16)
                env["s"].append(_dot_nt(env["q"][:, hd * X_HEAD_DIM:(hd + 1) * X_HEAD_DIM],
                                        _head_slab(xk_ref, b, hd)))

            def softmax():
                s = jnp.concatenate(env["s"], axis=0)
                m = jnp.max(s, axis=-1, keepdims=True)
                p = jnp.exp(s - m)
                env["p"] = p * (1.0 / jnp.sum(p, axis=-1, keepdims=True))

            def pv(hd):
                p = env["p"][hd * SUBLANES:(hd + 1) * SUBLANES].astype(BF16)
                env["o"].append(_dot(p, _head_slab(xv_ref, b, hd)))
                if hd == N_X_HEADS - 1:
                    o_rows[b] = jnp.concatenate(env["o"], axis=1)
                    if b % 2 == 1:
                        ys_ref[tile_b] = jnp.where(own0, o_rows[b - 1], o_rows[b]).reshape(2, DEC_SEQ, D_MODEL)

            return qk, softmax, pv

        fillers = [attend(b) for b in range(bb)]
        yp_ref[...] = _mlp_value(hp_ref[...], g_ref, wup_ref, wdn_ref, gf_ref, fillers)

    @pl.when(i == n)
    def _():
        o = ys_ref[...].reshape(hsm_ref.shape).astype(BF16)
        hs = hsm_ref[...] + _dot(o, wo_ref[...])
        ys_ref[...] = _mlp_value(hs, g_ref, wup_ref, wdn_ref, gf_ref).reshape(ys_ref.shape)


def _mlp_xattn(hp2d, hsm, g_xattn, w_xq, w_xo, xk, xv, g_mlp, w_up, w_down, g_final):
    n, d = hp2d.shape
    ns = hsm.shape[0]
    nb = xk.shape[0]
    bb = BB_X
    tm = n // (nb // bb)
    n_tiles = n // tm
    assert n_tiles * bb == nb and tm % SUBLANES == 0
    clip = lambda i: jnp.minimum(i, n_tiles - 1)
    prompt = pl.BlockSpec((tm, d), lambda i: (clip(i), 0))
    mem = pl.BlockSpec((bb,) + xk.shape[1:], lambda i: (clip(i), 0, 0))
    const = lambda shape: pl.BlockSpec(shape, lambda i: (0,) * len(shape), pipeline_mode=pl.Buffered(1))
    return pl.pallas_call(
        _mlp_xattn_kernel,
        grid=(n_tiles + 1,),
        in_specs=[prompt, const((ns, d)), const((1, d)), const((d, d)), const((d, d)), mem, mem,
                  const((1, d)), const((d, D_FF)), const((D_FF, d)), const((1, d))],
        out_specs=[prompt, pl.BlockSpec((ns // DEC_SEQ, DEC_SEQ, d), lambda i: (0, 0, 0))],
        out_shape=[jax.ShapeDtypeStruct((n, d), F32), jax.ShapeDtypeStruct((ns // DEC_SEQ, DEC_SEQ, d), F32)],
        compiler_params=pltpu.CompilerParams(
            dimension_semantics=("arbitrary",), vmem_limit_bytes=VMEM_LIMIT),
        name="mlp_xattn",
    )(hp2d, hsm, g_xattn, w_xq, w_xo, xk, xv, g_mlp, w_up, w_down, g_final)


def _mem_rows(c):
    nb = c.shape[0]
    c = c.reshape(nb, N_MEM, N_X_HEADS, X_D_HALVES, LANES)
    return jnp.transpose(c, (0, 1, 3, 2, 4)).reshape(nb, N_MEM * X_D_HALVES * N_X_HEADS, LANES)


def kernel(x_prompt, x_sample, mem_prompt, cache_win_k, cache_win_v, state_ret, cache_mem_k, cache_mem_v,
           g_mix, w_in, attn_sinks, ret_gn_g, ret_gn_b, w_out, g_xattn, g_mem, w_xq, w_xk, w_xv, w_xo,
           g_mlp, w_up, w_down, g_final):
    depth = w_in.shape[0]
    assert depth == 1, "single-layer trunk"
    b, s, d = x_prompt.shape
    nb, ls, _ = x_sample.shape
    row = lambda a: a.reshape(1, -1)
    sinks = attn_sinks[0]
    gn_g, gn_b = row(ret_gn_g[0]), row(ret_gn_b[0])
    g_fin = row(g_final)

    mk, mv, mkb, mvb, w_in_b, w_out_b = _memory_kv(
        mem_prompt.reshape(b * N_MEM, d), row(g_mem[0]), w_xk[0], w_xv[0], w_in[0], w_out[0])
    hp, p_wk, p_wv, p_rs, w_up_b, w_dn_b, w_xq_b, w_xo_b = _prompt_mixer(
        x_prompt, row(g_mix[0]), w_in_b, w_out_b, sinks, gn_g, gn_b,
        (w_up[0], w_down[0], w_xq[0], w_xo[0]))
    hp = _prompt_xattn(hp, row(g_xattn[0]), w_xq_b, w_xo_b,
                       mkb.reshape(b, N_MEM, d), mvb.reshape(b, N_MEM, d))

    win_t = lambda c: jnp.transpose(c, (0, 2, 3, 1)).reshape(nb, ATT_KV_W, WINDOW)
    win_t_inv = lambda a: jnp.transpose(a.reshape(-1, N_KV_HEADS, HEAD_DIM, WINDOW),
                                        (0, 3, 1, 2)).reshape(1, -1, WINDOW, N_KV_HEADS, HEAD_DIM)
    hs, s_wk, s_wv, s_rs = _sample_mixer(
        x_sample, row(g_mix[0]), w_in_b, w_out_b, sinks, gn_g, gn_b,
        win_t(cache_win_k[0]), win_t(cache_win_v[0]), state_ret[0].reshape(nb, RET_QK_W, RET_V_DIM))

    y_prompt, y_sample = _mlp_xattn(
        hp.reshape(b * s, d), hs, row(g_xattn[0]), w_xq_b, w_xo_b,
        _mem_rows(cache_mem_k[0]), _mem_rows(cache_mem_v[0]), row(g_mlp[0]), w_up_b, w_dn_b, g_fin)
    y_prompt = y_prompt.reshape(b, s, d)

    ret5 = lambda a, n: a.reshape(1, n, N_RET_HEADS, RET_QK_DIM, RET_V_DIM)
    mem5 = lambda a: jnp.transpose(a.reshape(b, N_MEM, X_D_HALVES, N_X_HEADS, LANES),
                                   (0, 1, 3, 2, 4)).reshape(1, b, N_MEM, N_X_HEADS, X_HEAD_DIM)
    return (y_prompt, y_sample,
            win_t_inv(p_wk), win_t_inv(p_wv), ret5(p_rs, b), mem5(mk), mem5(mv),
            win_t_inv(s_wk), win_t_inv(s_wv), ret5(s_rs, nb))
```

```python
import functools

import jax
import jax.numpy as jnp
import numpy as np
from jax import lax
from jax.experimental import pallas as pl
from jax.experimental.pallas import tpu as pltpu

F32 = jnp.float32
BF16 = jnp.bfloat16

D_MODEL = 1024
BATCH = 8
SEQ = 2048
DEC_BATCH = 128
DEC_SEQ = 4
HEAD_DIM = 64
N_ATT_HEADS = 8
N_KV_HEADS = 2
KV_GROUP = N_ATT_HEADS // N_KV_HEADS
WINDOW = 128
BLK = 128
N_RET_HEADS = 4
RET_QK_DIM = 64
RET_V_DIM = 128
N_MEM = 256
N_X_HEADS = 4
X_HEAD_DIM = D_MODEL // N_X_HEADS
D_FF = 4 * D_MODEL
RMS_EPS = 1e-6
GN_EPS = 1e-5

ATT_Q_W = N_ATT_HEADS * HEAD_DIM
ATT_KV_W = N_KV_HEADS * HEAD_DIM
RET_QK_W = N_RET_HEADS * RET_QK_DIM
RET_V_W = N_RET_HEADS * RET_V_DIM
MIX_OUT = ATT_Q_W + RET_V_W
D_IN = ATT_Q_W + 2 * ATT_KV_W + 2 * RET_QK_W + 2 * RET_V_W
C_QA, C_KV, C_QKR, C_VR, C_GR = 0, 512, 768, 1280, 1792

LANES = 128
SUBLANES = 8
HALF = LANES // 2
X_D_HALVES = X_HEAD_DIM // LANES
NEG = -1e30
VMEM_LIMIT = 56 * 1024 * 1024

TM_MIX = 512
TM_X = 2048
SUB_ROWS = 512
FF_CHUNK = 1024
BB_MIX = 32
BB_X = 4

NEG_SLOPES = [-(2.0 ** (-8.0 * (i + 1) / N_ATT_HEADS)) for i in range(N_ATT_HEADS)]
_LOG_G = np.log(1.0 - 2.0 ** (-5.0 - np.arange(N_RET_HEADS))).astype(np.float32).astype(np.float64)


def _prompt_tables():
    qi = np.arange(BLK)[:, None]
    kj = np.arange(2 * BLK)[None, :]
    dist = (qi + BLK - kj).astype(np.float64)
    mask = np.where((dist >= 0) & (dist < WINDOW), 0.0, NEG)
    l = np.arange(BLK, dtype=np.float64)
    diff = l[:, None] - l[None, :]
    decay = np.where(diff >= 0, np.exp(_LOG_G[:, None, None] * np.maximum(diff, 0.0)), 0.0)
    xi = np.exp((l[:, None] + 1.0) * _LOG_G[None, :])
    zeta = np.exp((BLK - 1.0 - l)[:, None] * _LOG_G[None, :])
    xi_t = np.repeat(xi, RET_V_DIM, axis=1)
    zeta_t = np.repeat(zeta, RET_QK_DIM, axis=1)
    f = lambda a: np.asarray(a, np.float32)
    return f(dist), f(mask), f(decay), f(xi_t), f(zeta_t)


def _sample_tables():
    slopes = -np.asarray(NEG_SLOPES)
    bias = np.full((2, N_ATT_HEADS * SUBLANES, 2 * BLK), NEG, np.float64)
    dec = np.zeros((2, N_RET_HEADS * SUBLANES, BLK), np.float64)
    xi = np.zeros((2, N_RET_HEADS * SUBLANES, RET_V_DIM), np.float64)
    zeta = np.zeros((2, SUBLANES, RET_QK_W), np.float64)
    for par in range(2):
        for r in range(SUBLANES):
            own = DEC_SEQ * par <= r < DEC_SEQ * (par + 1)
            t = r - DEC_SEQ * par if own else r % DEC_SEQ
            for h in range(N_ATT_HEADS):
                row = h * SUBLANES + r
                for j in range(WINDOW):
                    d = t + WINDOW - j
                    if 0 <= d < WINDOW:
                        bias[par, row, j] = -slopes[h] * d
                for c in range(DEC_SEQ):
                    d = t - c
                    if d >= 0:
                        bias[par, row, WINDOW + DEC_SEQ * par + c] = -slopes[h] * d
            for h in range(N_RET_HEADS):
                row = h * SUBLANES + r
                if own:
                    xi[par, row, :] = np.exp((t + 1.0) * _LOG_G[h])
                    zeta[par, r, h * RET_QK_DIM:(h + 1) * RET_QK_DIM] = np.exp((DEC_SEQ - 1.0 - t) * _LOG_G[h])
                    for c in range(t + 1):
                        dec[par, row, DEC_SEQ * par + c] = np.exp(_LOG_G[h] * (t - c))
    f = lambda a: np.asarray(a, np.float32)
    return f(bias), f(dec), f(xi), f(zeta)


_P_DIST, _P_MASK, _P_DECAY, _P_XI, _P_ZETA = _prompt_tables()
_S_BIAS, _S_DEC, _S_XI, _S_ZETA = _sample_tables()
_GL_PROMPT = [float(np.exp(_LOG_G[h] * BLK)) for h in range(N_RET_HEADS)]
_GL_SAMPLE = [float(np.exp(_LOG_G[h] * DEC_SEQ)) for h in range(N_RET_HEADS)]


def _rms(x, g):
    return x * lax.rsqrt(jnp.mean(x * x, axis=-1, keepdims=True) + RMS_EPS) * g


def _dot(a, b):
    return jnp.dot(a, b, preferred_element_type=F32)


def _dot_nt(a, b):
    return lax.dot_general(a, b, (((1,), (1,)), ((), ())), preferred_element_type=F32)


def _dot_tn(a, b):
    return lax.dot_general(a, b, (((0,), (0,)), ((), ())), preferred_element_type=F32)


def _silu(g):
    return g * (1.0 / (1.0 + jnp.exp(-g)))


def _half_masks(width):
    lane = lax.broadcasted_iota(jnp.int32, (1, width), 1)
    lo = ((lane & (LANES - 1)) < HALF).astype(F32)
    return lo, 1.0 - lo


def _sink_softmax(s, sink):
    m = jnp.maximum(jnp.max(s, axis=-1, keepdims=True), sink)
    p = jnp.exp(s - m)
    den = jnp.sum(p, axis=-1, keepdims=True) + jnp.exp(sink - m)
    return p * (1.0 / den)


def _group_norm(o, g, b):
    mu = jnp.mean(o, axis=-1, keepdims=True)
    d = o - mu
    var = jnp.mean(d * d, axis=-1, keepdims=True)
    return d * lax.rsqrt(var + GN_EPS) * g + b


def _pm_project_stages(x, slot, gmix_ref, win_ref, sc, kv_out=None):
    tm = x.shape[0]
    xn = _rms(x, gmix_ref[...]).astype(BF16)

    pw = 2 * LANES
    lo, hi = _half_masks(pw)

    def stage_q(i):
        cols = slice(i * pw, (i + 1) * pw)
        q = _dot(xn, win_ref[:, C_QA + i * pw:C_QA + (i + 1) * pw])
        sc["qlo"][slot, :, cols] = (q * (lo * HEAD_DIM ** -0.5)).astype(BF16)
        sc["qhi"][slot, :, cols] = (q * (hi * HEAD_DIM ** -0.5)).astype(BF16)

    def stage_kv():
        z = _dot(xn, win_ref[:, C_KV:C_KV + pw])
        low = lax.broadcasted_iota(jnp.int32, (tm, LANES), 1) < HALF
        k = z[:, 0:ATT_KV_W]
        v = z[:, ATT_KV_W:2 * ATT_KV_W]
        if kv_out is not None:
            kv_out[0][0] = k[tm - WINDOW:, :].T
            kv_out[1][0] = v[tm - WINDOW:, :].T
        k_r = pltpu.roll(k, HALF, axis=1)
        v_r = pltpu.roll(v, HALF, axis=1)
        sc["kd0"][slot] = jnp.where(low, k, k_r).astype(BF16)
        sc["kd1"][slot] = jnp.where(low, k_r, k).astype(BF16)
        sc["vd0"][slot, :, 0:LANES] = jnp.where(low, v, 1.0).astype(BF16)
        sc["vd0"][slot, :, LANES:2 * LANES] = jnp.where(low, 1.0, v_r).astype(BF16)
        sc["vd1"][slot, :, 0:LANES] = jnp.where(low, v_r, 1.0).astype(BF16)
        sc["vd1"][slot, :, LANES:2 * LANES] = jnp.where(low, 1.0, v).astype(BF16)

    def stage_qr():
        qr = _dot(xn, win_ref[:, C_QKR:C_QKR + pw])
        sc["qrlo"][slot] = (qr * lo).astype(BF16)
        sc["qrhi"][slot] = (qr * hi).astype(BF16)

    def stage_kr():
        sc["kr"][slot] = _dot(xn, win_ref[:, C_QKR + pw:C_QKR + 2 * pw]) * (RET_QK_DIM ** -0.5)

    def stage_vr(i):
        cols = slice(i * pw, (i + 1) * pw)
        sc["vr"][slot, :, cols] = _dot(xn, win_ref[:, C_VR + i * pw:C_VR + (i + 1) * pw]).astype(BF16)

    def stage_gate(i):
        cols = slice(i * pw, (i + 1) * pw)
        sc["gate"][slot, :, cols] = _silu(_dot(xn, win_ref[:, C_GR + i * pw:C_GR + (i + 1) * pw]))

    part = functools.partial
    return [part(stage_q, 0), part(stage_q, 1), stage_kv, stage_qr, stage_kr,
            part(stage_vr, 0), part(stage_vr, 1), part(stage_gate, 0), part(stage_gate, 1)]


def _pm_last_block(slot, tm, sc):
    rows = slice(tm - BLK, tm)
    return ([sc["kd0"][slot, rows, :], sc["kd1"][slot, rows, :]],
            [sc["vd0"][slot, rows, :], sc["vd1"][slot, rows, :]])


def _pm_blocks(slot, prev_kd, prev_vd, is_first, state, fillers, tm, sinks_ref, gng_ref, gnb_ref,
               decay_ref, xi_ref, zeta_ref, sc):
    nblk = tm // BLK
    n_units = nblk * (N_KV_HEADS + N_RET_HEADS // 2)
    pending = list(fillers)
    done_units = [0]

    def unit_done():
        done_units[0] += 1
        while pending and (len(fillers) - len(pending)) * n_units < done_units[0] * len(fillers):
            pending.pop(0)()
    lowb = lax.broadcasted_iota(jnp.int32, (BLK, LANES), 1) < HALF
    col = lax.broadcasted_iota(jnp.int32, (BLK, 2 * BLK), 1)
    first_mask = None if is_first is False else jnp.where((col < BLK) & is_first, NEG, 0.0)
    kd_refs = (sc["kd0"], sc["kd1"])
    vd_refs = (sc["vd0"], sc["vd1"])
    qlo, qhi, mix = sc["qlo"], sc["qhi"], sc["mix"]
    n_pairs = N_RET_HEADS // 2

    for j in range(nblk):
        rows = slice(j * BLK, (j + 1) * BLK)
        c0s = [kvh * KV_GROUP * HEAD_DIM for kvh in range(N_KV_HEADS)]
        lsls = [slice(i * LANES, (i + 1) * LANES) for i in range(n_pairs)]
        vds, scores = [], []
        for kvh in range(N_KV_HEADS):
            if j == 0:
                kd = jnp.concatenate([prev_kd[kvh], kd_refs[kvh][slot, rows, :]], axis=0)
                vds.append(jnp.concatenate([prev_vd[kvh], vd_refs[kvh][slot, rows, :]], axis=0))
            else:
                krows = slice((j - 1) * BLK, (j + 1) * BLK)
                kd = kd_refs[kvh][slot, krows, :]
                vds.append(vd_refs[kvh][slot, krows, :])
            c0 = c0s[kvh]
            qst = jnp.concatenate([qlo[slot, rows, c0:c0 + LANES], qhi[slot, rows, c0:c0 + LANES],
                                   qlo[slot, rows, c0 + LANES:c0 + 2 * LANES],
                                   qhi[slot, rows, c0 + LANES:c0 + 2 * LANES]], axis=0)
            scores.append(_dot_nt(qst, kd))
        unit_done()

        kps = [sc["kr"][slot, rows, lsls[i]] for i in range(n_pairs)]
        vpairs = [sc["vr"][slot, rows, 2 * i * RET_V_DIM:(2 * i + 2) * RET_V_DIM] for i in range(n_pairs)]
        q2s = [jnp.concatenate([sc["qrlo"][slot, rows, lsls[i]], sc["qrhi"][slot, rows, lsls[i]]], axis=0)
               for i in range(n_pairs)]
        a_s = [_dot_nt(q2s[i], kps[i].astype(BF16)) for i in range(n_pairs)]
        ocs = [_dot(q2s[i], state[i].astype(BF16)) for i in range(n_pairs)]
        us = [_dot_tn((kps[i] * zeta_ref[:, lsls[i]]).astype(BF16), vpairs[i]) for i in range(n_pairs)]
        unit_done()

        for kvh in range(N_KV_HEADS):
            s, vd, c0 = scores[kvh], vds[kvh], c0s[kvh]
            es, esink = [], []
            for g in range(KV_GROUP):
                h = kvh * KV_GROUP + g
                sg = s[g * BLK:(g + 1) * BLK] + sc["bias"][h]
                if j == 0 and first_mask is not None:
                    sg = sg + first_mask
                sink = sinks_ref[h]
                m = jnp.maximum(jnp.max(sg, axis=-1, keepdims=True), sink)
                es.append(jnp.exp(sg - m).astype(BF16))
                esink.append(jnp.exp(sink - m))
            o = _dot(jnp.concatenate(es, axis=0), vd)
            for pair in range(KV_GROUP // 2):
                oe = o[2 * pair * BLK:(2 * pair + 1) * BLK]
                oo = o[(2 * pair + 1) * BLK:(2 * pair + 2) * BLK]
                num = jnp.where(lowb, oe[:, :LANES], oo[:, LANES:])
                den = (jnp.where(lowb, oe[:, LANES:], oo[:, :LANES])
                       + jnp.where(lowb, esink[2 * pair], esink[2 * pair + 1]))
                cs = c0 + pair * LANES
                mix[slot, rows, cs:cs + LANES] = (num * (1.0 / den)).astype(BF16)
            unit_done()

        for i in range(n_pairs):
            a, oc, u, sp = a_s[i], ocs[i], us[i], state[i]
            inner = jnp.concatenate([a[:BLK] * decay_ref[2 * i], a[BLK:] * decay_ref[2 * i + 1]], axis=0)
            oi = _dot(inner.astype(BF16), vpairs[i])
            for half in range(2):
                h = 2 * i + half
                vsl = slice(h * RET_V_DIM, (h + 1) * RET_V_DIM)
                hr = slice(half * BLK, (half + 1) * BLK)
                o = oi[hr, half * RET_V_DIM:(half + 1) * RET_V_DIM] + oc[hr] * xi_ref[:, vsl]
                r = _group_norm(o, gng_ref[:, vsl], gnb_ref[:, vsl]) * sc["gate"][slot, rows, vsl]
                mix[slot, rows, ATT_Q_W + h * RET_V_DIM:ATT_Q_W + (h + 1) * RET_V_DIM] = r.astype(BF16)
            state[i] = jnp.concatenate(
                [_GL_PROMPT[2 * i] * sp[:RET_QK_DIM] + u[:RET_QK_DIM, :RET_V_DIM],
                 _GL_PROMPT[2 * i + 1] * sp[RET_QK_DIM:] + u[RET_QK_DIM:, RET_V_DIM:]], axis=0)

    assert not pending
    return state


def _pm_wout_pieces(slot, x_ref, rows, wout_ref, h_ref, sc):
    pw = 2 * LANES
    n = D_MODEL // pw
    parts = []

    def piece(k):
        parts.append(_dot(sc["mix"][slot], wout_ref[:, k * pw:(k + 1) * pw]))
        if k == n - 1:
            h_ref[rows, :] = x_ref[rows, :] + jnp.concatenate(parts, axis=1)

    return [functools.partial(piece, k) for k in range(n)]


def _prompt_mixer_kernel(sinks_ref, xpair_ref, xnext_ref, gmix_ref, win_ref, wout_ref, gng_ref, gnb_ref,
                           dist_ref, mask_ref, decay_ref, xi_ref, zeta_ref,
                           wupf_ref, wdnf_ref, wqf_ref, wof_ref, wkf_ref, wvf_ref,
                           h_ref, wk_ref, wv_ref, st_ref,
                           wupb_ref, wdnb_ref, wqb_ref, wob_ref, wkb_ref, wvb_ref,
                           qlo_s, qhi_s, kd0_s, kd1_s, vd0_s, vd1_s,
                           qrlo_s, qrhi_s, kr_s, vr_s, gate_s, mix_s, bias_s, state_s):
    u = pl.program_id(0)
    tm = xnext_ref.shape[0]
    wupb_ref[...] = wupf_ref[...].astype(BF16)
    wdnb_ref[...] = wdnf_ref[...].astype(BF16)
    wqb_ref[...] = wqf_ref[...].astype(BF16)
    wob_ref[...] = wof_ref[...].astype(BF16)
    wkb_ref[...] = wkf_ref[...].astype(BF16)
    wvb_ref[...] = wvf_ref[...].astype(BF16)
    sc = dict(qlo=qlo_s, qhi=qhi_s, kd0=kd0_s, kd1=kd1_s, vd0=vd0_s, vd1=vd1_s, qrlo=qrlo_s, qrhi=qrhi_s,
              kr=kr_s, vr=vr_s, gate=gate_s, mix=mix_s, bias=bias_s)
    n_pairs = N_RET_HEADS // 2
    blocks = functools.partial(_pm_blocks, tm=tm, sinks_ref=sinks_ref, gng_ref=gng_ref,
                               gnb_ref=gnb_ref, decay_ref=decay_ref, xi_ref=xi_ref, zeta_ref=zeta_ref, sc=sc)

    @pl.when(u == 0)
    def _():
        for h in range(N_ATT_HEADS):
            bias_s[h] = NEG_SLOPES[h] * dist_ref[...] + mask_ref[...]
        state_s[...] = jnp.zeros_like(state_s)
        kd0_s[1] = jnp.zeros(kd0_s.shape[1:], BF16)
        kd1_s[1] = jnp.zeros(kd1_s.shape[1:], BF16)
        vd0_s[1] = jnp.zeros(vd0_s.shape[1:], BF16)
        vd1_s[1] = jnp.zeros(vd1_s.shape[1:], BF16)
        for stage in _pm_project_stages(xpair_ref[0:tm, :], 0, gmix_ref, win_ref, sc):
            stage()

    seq_start = (u % 2) == 0
    state = [jnp.where(seq_start, 0.0, state_s[i * LANES:(i + 1) * LANES, :]) for i in range(n_pairs)]

    prev_kd, prev_vd = _pm_last_block(1, tm, sc)
    stages = _pm_project_stages(xpair_ref[tm:2 * tm, :], 1, gmix_ref, win_ref, sc, kv_out=(wk_ref, wv_ref))
    state = blocks(0, prev_kd, prev_vd, seq_start, state, stages)
    wout0 = _pm_wout_pieces(0, xpair_ref, slice(0, tm), wout_ref, h_ref, sc)

    prev_kd, prev_vd = _pm_last_block(0, tm, sc)
    stages = _pm_project_stages(xnext_ref[...], 0, gmix_ref, win_ref, sc)
    state = blocks(1, prev_kd, prev_vd, False, state, wout0 + stages)
    for piece in _pm_wout_pieces(1, xpair_ref, slice(tm, 2 * tm), wout_ref, h_ref, sc):
        piece()

    for i in range(n_pairs):
        state_s[i * LANES:(i + 1) * LANES, :] = state[i]
        st_ref[0, i * LANES:(i + 1) * LANES, :] = state[i]


def _prompt_mixer(x, g_mix, w_in, w_out, sinks, gn_g, gn_b, side_f32):
    b, s, d = x.shape
    tm = TM_MIX
    n_tiles = b * s // tm
    steps = n_tiles // 2
    seq_steps = s // (2 * tm)
    assert s % (2 * tm) == 0 and seq_steps == 2, "kernel assumes 4 tiles per sequence"
    x2d = x.reshape(b * s, d)
    const = lambda shape: pl.BlockSpec(shape, lambda i: (0,) * len(shape), pipeline_mode=pl.Buffered(1))
    slot2 = lambda rows, cols, dt: pltpu.VMEM((2, rows, cols), dt)
    side_specs = [pl.BlockSpec((w.shape[0] // steps, w.shape[1]), lambda i: (i, 0)) for w in side_f32]
    outs = pl.pallas_call(
        _prompt_mixer_kernel,
        grid=(steps,),
        in_specs=[
            pl.BlockSpec(memory_space=pltpu.SMEM),
            pl.BlockSpec((2 * tm, d), lambda i: (i, 0)),
            pl.BlockSpec((tm, d), lambda i: (jnp.minimum(2 * i + 2, n_tiles - 1), 0)),
            const((1, d)), const((d, D_IN)), const((MIX_OUT, d)),
            const((1, RET_V_W)), const((1, RET_V_W)),
            const((BLK, 2 * BLK)), const((BLK, 2 * BLK)),
            const((N_RET_HEADS, BLK, BLK)), const((BLK, RET_V_W)), const((BLK, RET_QK_W)),
        ] + side_specs,
        out_specs=[
            pl.BlockSpec((2 * tm, d), lambda i: (i, 0)),
            pl.BlockSpec((1, WINDOW, ATT_KV_W), lambda i: (i // seq_steps, 0, 0)),
            pl.BlockSpec((1, WINDOW, ATT_KV_W), lambda i: (i // seq_steps, 0, 0)),
            pl.BlockSpec((1, RET_QK_W, RET_V_DIM), lambda i: (i // seq_steps, 0, 0)),
        ] + side_specs,
        out_shape=[
            jax.ShapeDtypeStruct((b * s, d), F32),
            jax.ShapeDtypeStruct((b, WINDOW, ATT_KV_W), F32),
            jax.ShapeDtypeStruct((b, WINDOW, ATT_KV_W), F32),
            jax.ShapeDtypeStruct((b, RET_QK_W, RET_V_DIM), F32),
        ] + [jax.ShapeDtypeStruct(w.shape, BF16) for w in side_f32],
        scratch_shapes=[
            slot2(tm, ATT_Q_W, BF16), slot2(tm, ATT_Q_W, BF16),
            slot2(tm, LANES, BF16), slot2(tm, LANES, BF16),
            slot2(tm, 2 * LANES, BF16), slot2(tm, 2 * LANES, BF16),
            slot2(tm, RET_QK_W, BF16), slot2(tm, RET_QK_W, BF16),
            slot2(tm, RET_QK_W, F32), slot2(tm, RET_V_W, BF16),
            slot2(tm, RET_V_W, F32), slot2(tm, MIX_OUT, BF16),
            pltpu.VMEM((N_ATT_HEADS, BLK, 2 * BLK), F32),
            pltpu.VMEM((RET_QK_W, RET_V_DIM), F32),
        ],
        compiler_params=pltpu.CompilerParams(
            dimension_semantics=("arbitrary",), vmem_limit_bytes=VMEM_LIMIT),
        name="prompt_mixer",
    )(sinks, x2d, x2d, g_mix, w_in, w_out, gn_g, gn_b,
      jnp.asarray(_P_DIST), jnp.asarray(_P_MASK), jnp.asarray(_P_DECAY), jnp.asarray(_P_XI),
      jnp.asarray(_P_ZETA), *side_f32)
    return (outs[0].reshape(b, s, d),) + tuple(outs[1:])


def _cast_kernel(win_ref, wout_ref, winb_ref, woutb_ref):
    winb_ref[...] = win_ref[...].astype(BF16)
    woutb_ref[...] = wout_ref[...].astype(BF16)


def _cast_mixer_weights(w_in, w_out):
    steps = 4
    blk = lambda w: pl.BlockSpec((w.shape[0] // steps, w.shape[1]), lambda i: (i, 0))
    return pl.pallas_call(
        _cast_kernel,
        grid=(steps,),
        in_specs=[blk(w_in), blk(w_out)],
        out_specs=[blk(w_in), blk(w_out)],
        out_shape=[jax.ShapeDtypeStruct(w_in.shape, BF16), jax.ShapeDtypeStruct(w_out.shape, BF16)],
        compiler_params=pltpu.CompilerParams(
            dimension_semantics=("arbitrary",), vmem_limit_bytes=VMEM_LIMIT),
        name="cast_mixer_weights",
    )(w_in, w_out)


def _prompt_xattn_kernel(h_ref, g_ref, wq_ref, wo_ref, mem_ref, gm_ref, wk_ref, wv_ref,
                         out_ref, mko_ref, mvo_ref, o_s, mk_s, mv_s):
    mn = _rms(mem_ref[0], gm_ref[...]).astype(BF16)
    mk = _dot(mn, wk_ref[...])
    mv = _dot(mn, wv_ref[...])
    mk_s[...] = mk.astype(BF16)
    mv_s[...] = mv.astype(BF16)
    group = X_D_HALVES * N_X_HEADS
    for hd in range(N_X_HEADS):
        for dh in range(X_D_HALVES):
            cols = slice(hd * X_HEAD_DIM + dh * LANES, hd * X_HEAD_DIM + (dh + 1) * LANES)
            krows = pl.ds(dh * N_X_HEADS + hd, N_MEM, stride=group)
            mko_ref[0, krows, :] = mk[:, cols]
            mvo_ref[0, krows, :] = mv[:, cols]
    mk_ref, mv_ref = mk_s, mv_s

    def stages(r0):
        rows = slice(r0, r0 + SUB_ROWS)
        env = {}

        def project():
            env["h"] = h_ref[0, rows, :]
            xn = _rms(env["h"], g_ref[...]).astype(BF16)
            env["q"] = (_dot(xn, wq_ref[...]) * (X_HEAD_DIM ** -0.5)).astype(BF16)

        def scores(hd):
            sl = slice(hd * X_HEAD_DIM, (hd + 1) * X_HEAD_DIM)
            env[hd] = _dot_nt(env["q"][:, sl], mk_ref[:, sl])

        def head(hd):
            if hd + 1 < N_X_HEADS:
                scores(hd + 1)
            sl = slice(hd * X_HEAD_DIM, (hd + 1) * X_HEAD_DIM)
            s = env.pop(hd)
            m = jnp.max(s, axis=-1, keepdims=True)
            p = jnp.exp(s - m)
            p = p * (1.0 / jnp.sum(p, axis=-1, keepdims=True))
            o_s[rows, sl] = _dot(p.astype(BF16), mv_ref[:, sl]).astype(BF16)

        def output():
            out_ref[0, rows, :] = env["h"] + _dot(o_s[rows, :], wo_ref[...])

        def project_and_first_scores():
            project()
            scores(0)

        return ([project_and_first_scores] + [functools.partial(head, hd) for hd in range(N_X_HEADS)]
                + [output])

    chains = [stages(r0) for r0 in range(0, h_ref.shape[1], SUB_ROWS)]
    n_stage = len(chains[0])
    for step in range(n_stage + len(chains) - 1):
        for lag, chain in enumerate(chains):
            if 0 <= step - lag < n_stage:
                chain[step - lag]()


def _prompt_xattn(h, g, w_xq, w_xo, mem, g_mem, w_xk, w_xv):
    b, s, d = h.shape
    tm = TM_X
    assert s == tm, "one sequence per grid step: its memory K/V is computed at the top of the step"
    mem_rows = N_MEM * d // LANES
    const = lambda shape: pl.BlockSpec(shape, lambda i: (0,) * len(shape), pipeline_mode=pl.Buffered(1))
    tok = pl.BlockSpec((1, tm, d), lambda i: (i, 0, 0))
    mem_in = pl.BlockSpec((1, N_MEM, d), lambda i: (i, 0, 0))
    mem_out = pl.BlockSpec((1, mem_rows, LANES), lambda i: (i, 0, 0))
    return pl.pallas_call(
        _prompt_xattn_kernel,
        grid=(b,),
        in_specs=[tok, const((1, d)), const((d, d)), const((d, d)), mem_in, const((1, d)),
                  const((d, d)), const((d, d))],
        out_specs=[tok, mem_out, mem_out],
        out_shape=[jax.ShapeDtypeStruct((b, s, d), F32),
                   jax.ShapeDtypeStruct((b, mem_rows, LANES), F32),
                   jax.ShapeDtypeStruct((b, mem_rows, LANES), F32)],
        scratch_shapes=[pltpu.VMEM((tm, d), BF16), pltpu.VMEM((N_MEM, d), BF16), pltpu.VMEM((N_MEM, d), BF16)],
        compiler_params=pltpu.CompilerParams(
            dimension_semantics=("arbitrary",), vmem_limit_bytes=VMEM_LIMIT),
        name="prompt_xattn",
    )(h, g, w_xq, w_xo, mem, g_mem, w_xk, w_xv)


def _sample_mixer_kernel(sinks_ref, x_ref, gmix_ref, win_ref, wout_ref, gng_ref, gnb_ref,
                         ck_ref, cv_ref, st_ref, bias_ref, dec_ref, xi_ref, zeta_ref,
                         h_ref, swk_ref, swv_ref, sst_ref):
    bb = ck_ref.shape[0]
    nt = bb // 2
    x = x_ref[...].reshape(bb * DEC_SEQ, D_MODEL)
    xn = _rms(x, gmix_ref[...]).astype(BF16)
    tile3 = lambda a: a.reshape(nt, SUBLANES, a.shape[-1])

    q = _dot(xn, win_ref[:, C_QA:C_QA + ATT_Q_W]) * (HEAD_DIM ** -0.5)
    kv = _dot(xn, win_ref[:, C_KV:C_KV + 2 * ATT_KV_W])
    qkr = _dot(xn, win_ref[:, C_QKR:C_QKR + 2 * RET_QK_W])
    vr = _dot(xn, win_ref[:, C_VR:C_VR + RET_V_W])
    gate3 = tile3(_silu(_dot(xn, win_ref[:, C_GR:C_GR + RET_V_W])))

    lo512, hi512 = _half_masks(ATT_Q_W)
    q_r = pltpu.roll(q, HALF, axis=1)
    q_nat3 = tile3(q)
    q_rot3 = tile3(q_r)
    lo3 = lo512.reshape(1, 1, ATT_Q_W)
    hi3 = hi512.reshape(1, 1, ATT_Q_W)
    qa3 = (q_nat3 * lo3).astype(BF16)
    qb3 = (q_rot3 * lo3).astype(BF16)
    qc3 = (q_rot3 * hi3).astype(BF16)
    qd3 = (q_nat3 * hi3).astype(BF16)
    t128 = lambda a, i: a[:, :, i * LANES:(i + 1) * LANES]
    qs = jnp.concatenate([t128(qa3, 0), t128(qb3, 1), t128(qa3, 1), t128(qb3, 2),
                          t128(qc3, 2), t128(qd3, 2), t128(qc3, 3), t128(qd3, 3)], axis=1)

    k3 = tile3(kv[:, :ATT_KV_W])
    v3 = tile3(kv[:, ATT_KV_W:])
    pad_kv = jnp.zeros((nt, BLK - SUBLANES, LANES), BF16)
    knew_pad = jnp.concatenate([k3.astype(BF16), pad_kv], axis=1)
    vnew_pad = jnp.concatenate([v3.astype(BF16), pad_kv], axis=1)
    to_lanes = lambda a3: jnp.swapaxes(
        jnp.concatenate([a3, jnp.zeros((nt, BLK - SUBLANES, LANES), F32)], axis=1), 1, 2)
    k3t, v3t = to_lanes(k3), to_lanes(v3)
    roll3 = lambda a, sh: pltpu.roll(a.reshape(nt * BLK, LANES), sh, axis=1).reshape(nt, BLK, LANES)

    lo256, _ = _half_masks(RET_QK_W)
    qr3 = tile3(qkr[:, :RET_QK_W])
    kr3 = tile3(qkr[:, RET_QK_W:] * (RET_QK_DIM ** -0.5))
    vr3 = tile3(vr)
    lane256 = lax.broadcasted_iota(jnp.int32, (1, 1, RET_QK_W), 2)
    qrs = jnp.concatenate(
        [(qr3 * ((lane256 >= h * RET_QK_DIM) & (lane256 < (h + 1) * RET_QK_DIM)).astype(F32)).astype(BF16)
         for h in range(N_RET_HEADS)],
        axis=1)
    kr_pad = jnp.concatenate([kr3.astype(BF16), jnp.zeros((nt, BLK - SUBLANES, RET_QK_W), BF16)], axis=1)
    vr_pad = jnp.concatenate([vr3.astype(BF16), jnp.zeros((nt, BLK - SUBLANES, RET_V_W), BF16)], axis=1)

    lane = lax.broadcasted_iota(jnp.int32, (1, 1, LANES), 2)
    row8 = lax.broadcasted_iota(jnp.int32, (1, SUBLANES, 1), 1)
    bmm_nt = lambda a, b: jnp.einsum('bqd,bkd->bqk', a, b, preferred_element_type=F32)
    bmm = lambda a, b: jnp.einsum('bqk,bkd->bqd', a, b, preferred_element_type=F32)

    att_par, ret_par = [], []
    for par in range(2):
        bsl = pl.ds(par, nt, stride=2)
        ckt = ck_ref[bsl]
        cvt = cv_ref[bsl]
        keep = lane < WINDOW - DEC_SEQ
        new_shift = WINDOW - DEC_SEQ - DEC_SEQ * par
        swk_ref[bsl] = jnp.where(keep, roll3(ckt, WINDOW - DEC_SEQ), roll3(k3t, new_shift))
        swv_ref[bsl] = jnp.where(keep, roll3(cvt, WINDOW - DEC_SEQ), roll3(v3t, new_shift))

        s = jnp.concatenate([bmm(qs, ckt.astype(BF16)), bmm_nt(qs, knew_pad)], axis=2) + bias_ref[par]
        ps = []
        for h in range(N_ATT_HEADS):
            ps.append(_sink_softmax(s[:, h * SUBLANES:(h + 1) * SUBLANES, :], sinks_ref[h]).astype(BF16))
        p_all = jnp.concatenate(ps, axis=1)
        o = bmm_nt(p_all[:, :, :BLK], cvt.astype(BF16)) + bmm(p_all[:, :, BLK:], vnew_pad)
        o_r = pltpu.roll(o.reshape(nt * N_ATT_HEADS * SUBLANES, LANES), HALF, axis=1).reshape(o.shape)
        hr = lambda a, h: a[:, h * SUBLANES:(h + 1) * SUBLANES, :]
        low = lane < HALF
        att_par.append(jnp.concatenate([
            jnp.where(low, hr(o, 0), hr(o_r, 1)), jnp.where(low, hr(o, 2), hr(o_r, 3)),
            jnp.where(low, hr(o_r, 4), hr(o, 5)), jnp.where(low, hr(o_r, 6), hr(o, 7))], axis=2))

        st = st_ref[bsl]
        oc = bmm(qrs, st.astype(BF16))
        inner = (bmm_nt(qrs, kr_pad) * dec_ref[par]).astype(BF16)
        oi = bmm(inner, vr_pad)
        rs = []
        for h in range(N_RET_HEADS):
            vsl = slice(h * RET_V_DIM, (h + 1) * RET_V_DIM)
            rsl = slice(h * SUBLANES, (h + 1) * SUBLANES)
            o_h = oi[:, rsl, vsl] + oc[:, rsl, :] * xi_ref[par, rsl, :]
            rs.append(_group_norm(o_h, gng_ref[:, vsl], gnb_ref[:, vsl]) * gate3[:, :, vsl])
        ret_par.append(jnp.concatenate(rs, axis=2))

        kz3 = (kr3 * zeta_ref[par]).astype(BF16)
        vr3_b = vr3.astype(BF16)
        for p in range(nt):
            for i in range(N_RET_HEADS // 2):
                u = _dot_tn(kz3[p][:, i * LANES:(i + 1) * LANES],
                            vr3_b[p][:, 2 * i * RET_V_DIM:(2 * i + 2) * RET_V_DIM])
                for half in range(2):
                    h = 2 * i + half
                    dsl = slice(h * RET_QK_DIM, (h + 1) * RET_QK_DIM)
                    sst_ref[2 * p + par, dsl, :] = (
                        _GL_SAMPLE[h] * st[p, dsl, :]
                        + u[half * RET_QK_DIM:(half + 1) * RET_QK_DIM, half * RET_V_DIM:(half + 1) * RET_V_DIM])

    own0 = row8 < DEC_SEQ
    att3 = jnp.where(own0, att_par[0], att_par[1])
    ret3 = jnp.where(own0, ret_par[0], ret_par[1])
    mix = jnp.concatenate([att3, ret3], axis=2).reshape(2 * nt * DEC_SEQ, MIX_OUT).astype(BF16)
    h_ref[...] = x + _dot(mix, wout_ref[...])


def _sample_mixer(x3d, g_mix, w_in, w_out, sinks, gn_g, gn_b, ck, cv, st):
    nb, ls, d = x3d.shape
    n = nb * ls
    bb = BB_MIX
    r = bb * DEC_SEQ
    const = lambda shape: pl.BlockSpec(shape, lambda i: (0,) * len(shape))
    row = pl.BlockSpec((r, d), lambda i: (i, 0))
    win = pl.BlockSpec((bb, WINDOW, ATT_KV_W), lambda i: (i, 0, 0))
    state = pl.BlockSpec((bb, RET_QK_W, RET_V_DIM), lambda i: (i, 0, 0))
    return pl.pallas_call(
        _sample_mixer_kernel,
        grid=(nb // bb,),
        in_specs=[
            pl.BlockSpec(memory_space=pltpu.SMEM),
            pl.BlockSpec((bb, ls, d), lambda i: (i, 0, 0)), const((1, d)), const((d, D_IN)), const((MIX_OUT, d)),
            const((1, RET_V_W)), const((1, RET_V_W)),
            win, win, state,
            const(_S_BIAS.shape), const(_S_DEC.shape), const(_S_XI.shape), const(_S_ZETA.shape),
        ],
        out_specs=[row, win, win, state],
        out_shape=[
            jax.ShapeDtypeStruct((n, d), F32),
            jax.ShapeDtypeStruct((nb, WINDOW, ATT_KV_W), F32),
            jax.ShapeDtypeStruct((nb, WINDOW, ATT_KV_W), F32),
            jax.ShapeDtypeStruct((nb, RET_QK_W, RET_V_DIM), F32),
        ],
        compiler_params=pltpu.CompilerParams(
            dimension_semantics=("arbitrary",), vmem_limit_bytes=VMEM_LIMIT),
        name="sample_mixer",
    )(sinks, x3d, g_mix, w_in, w_out, gn_g, gn_b, ck, cv, st,
      jnp.asarray(_S_BIAS), jnp.asarray(_S_DEC), jnp.asarray(_S_XI), jnp.asarray(_S_ZETA))


def _head_slab(x_ref, b, hd):
    group = X_D_HALVES * N_X_HEADS
    halves = [x_ref[b, pl.ds(dh * N_X_HEADS + hd, N_MEM, stride=group), :] for dh in range(X_D_HALVES)]
    return jnp.concatenate(halves, axis=1).astype(BF16)


def _mlp_value(h, g_ref, wup_ref, wdn_ref, gf_ref, fillers=None):
    xn = _rms(h, g_ref[...]).astype(BF16)
    piece = FF_CHUNK // N_X_HEADS
    opiece = D_MODEL // N_X_HEADS
    n_chunks = D_FF // FF_CHUNK
    nofill = (None, None, None)

    def up(c):
        qk, softmax, _ = fillers[c] if fillers is not None else nofill
        hid = []
        for k in range(N_X_HEADS):
            cols = slice(c * FF_CHUNK + k * piece, c * FF_CHUNK + (k + 1) * piece)
            u = jnp.maximum(_dot(xn, wup_ref[:, cols]), 0.0)
            hid.append((u * u).astype(BF16))
            if qk is not None:
                qk(k)
        if softmax is not None:
            softmax()
        return jnp.concatenate(hid, axis=1)

    def down(c, hid):
        pv = (fillers[c] if fillers is not None else nofill)[2]
        rows_c = slice(c * FF_CHUNK, (c + 1) * FF_CHUNK)
        out = []
        for k in range(N_X_HEADS):
            out.append(_dot(hid, wdn_ref[rows_c, k * opiece:(k + 1) * opiece]))
            if pv is not None:
                pv(k)
        return jnp.concatenate(out, axis=1)

    acc = h
    hid = up(0)
    for c in range(n_chunks):
        nxt = up(c + 1) if c + 1 < n_chunks else None
        acc = acc + down(c, hid)
        hid = nxt
    return _rms(acc, gf_ref[...])


def _mlp_xattn_kernel(hp_ref, hsm_ref, gx_ref, wq_ref, wo_ref, xk_ref, xv_ref, g_ref, wup_ref, wdn_ref, gf_ref,
                      yp_ref, ys_ref):
    i = pl.program_id(0)
    n = pl.num_programs(0) - 1
    bb = xk_ref.shape[0]
    assert bb == D_FF // FF_CHUNK and bb % 2 == 0

    @pl.when(i == 0)
    def _():
        xn = _rms(hsm_ref[...], gx_ref[...]).astype(BF16)
        ys_ref[...] = (_dot(xn, wq_ref[...]) * (X_HEAD_DIM ** -0.5)).reshape(ys_ref.shape)

    @pl.when(i < n)
    def _():
        own0 = lax.broadcasted_iota(jnp.int32, (SUBLANES, 1), 0) < DEC_SEQ
        o_rows = {}

        def attend(b):
            t = b // 2
            tile_b = pl.ds(i * bb + 2 * t, 2)
            env = dict(s=[], o=[])

            def qk(hd):
                if hd == 0:
                    env["q"] = ys_ref[tile_b].reshape(SUBLANES, D_MODEL).astype(BF16)
                env["s"].append(_dot_nt(env["q"][:, hd * X_HEAD_DIM:(hd + 1) * X_HEAD_DIM],
                                        _head_slab(xk_ref, b, hd)))

            def softmax():
                s = jnp.concatenate(env["s"], axis=0)
                m = jnp.max(s, axis=-1, keepdims=True)
                p = jnp.exp(s - m)
                env["p"] = p * (1.0 / jnp.sum(p, axis=-1, keepdims=True))

            def pv(hd):
                p = env["p"][hd * SUBLANES:(hd + 1) * SUBLANES].astype(BF16)
                env["o"].append(_dot(p, _head_slab(xv_ref, b, hd)))
                if hd == N_X_HEADS - 1:
                    o_rows[b] = jnp.concatenate(env["o"], axis=1)
                    if b % 2 == 1:
                        ys_ref[tile_b] = jnp.where(own0, o_rows[b - 1], o_rows[b]).reshape(2, DEC_SEQ, D_MODEL)

            return qk, softmax, pv

        fillers = [attend(b) for b in range(bb)]
        yp_ref[...] = _mlp_value(hp_ref[...], g_ref, wup_ref, wdn_ref, gf_ref, fillers)

    @pl.when(i == n)
    def _():
        o = ys_ref[...].reshape(hsm_ref.shape).astype(BF16)
        hs = hsm_ref[...] + _dot(o, wo_ref[...])
        ys_ref[...] = _mlp_value(hs, g_ref, wup_ref, wdn_ref, gf_ref).reshape(ys_ref.shape)


def _mlp_xattn(hp2d, hsm, g_xattn, w_xq, w_xo, xk, xv, g_mlp, w_up, w_down, g_final):
    n, d = hp2d.shape
    ns = hsm.shape[0]
    nb = xk.shape[0]
    bb = BB_X
    tm = n // (nb // bb)
    n_tiles = n // tm
    assert n_tiles * bb == nb and tm % SUBLANES == 0
    clip = lambda i: jnp.minimum(i, n_tiles - 1)
    prompt = pl.BlockSpec((tm, d), lambda i: (clip(i), 0))
    mem = pl.BlockSpec((bb,) + xk.shape[1:], lambda i: (clip(i), 0, 0))
    const = lambda shape: pl.BlockSpec(shape, lambda i: (0,) * len(shape), pipeline_mode=pl.Buffered(1))
    return pl.pallas_call(
        _mlp_xattn_kernel,
        grid=(n_tiles + 1,),
        in_specs=[prompt, const((ns, d)), const((1, d)), const((d, d)), const((d, d)), mem, mem,
                  const((1, d)), const((d, D_FF)), const((D_FF, d)), const((1, d))],
        out_specs=[prompt, pl.BlockSpec((ns // DEC_SEQ, DEC_SEQ, d), lambda i: (0, 0, 0))],
        out_shape=[jax.ShapeDtypeStruct((n, d), F32), jax.ShapeDtypeStruct((ns // DEC_SEQ, DEC_SEQ, d), F32)],
        compiler_params=pltpu.CompilerParams(
            dimension_semantics=("arbitrary",), vmem_limit_bytes=VMEM_LIMIT),
        name="mlp_xattn",
    )(hp2d, hsm, g_xattn, w_xq, w_xo, xk, xv, g_mlp, w_up, w_down, g_final)


def _mem_rows(c):
    nb = c.shape[0]
    c = c.reshape(nb, N_MEM, N_X_HEADS, X_D_HALVES, LANES)
    return jnp.transpose(c, (0, 1, 3, 2, 4)).reshape(nb, N_MEM * X_D_HALVES * N_X_HEADS, LANES)


def kernel(x_prompt, x_sample, mem_prompt, cache_win_k, cache_win_v, state_ret, cache_mem_k, cache_mem_v,
           g_mix, w_in, attn_sinks, ret_gn_g, ret_gn_b, w_out, g_xattn, g_mem, w_xq, w_xk, w_xv, w_xo,
           g_mlp, w_up, w_down, g_final):
    depth = w_in.shape[0]
    assert depth == 1, "single-layer trunk"
    b, s, d = x_prompt.shape
    nb, ls, _ = x_sample.shape
    row = lambda a: a.reshape(1, -1)
    sinks = attn_sinks[0]
    gn_g, gn_b = row(ret_gn_g[0]), row(ret_gn_b[0])
    g_fin = row(g_final)

    w_in_b, w_out_b = _cast_mixer_weights(w_in[0], w_out[0])
    hp, p_wk, p_wv, p_rs, w_up_b, w_dn_b, w_xq_b, w_xo_b, w_xk_b, w_xv_b = _prompt_mixer(
        x_prompt, row(g_mix[0]), w_in_b, w_out_b, sinks, gn_g, gn_b,
        (w_up[0], w_down[0], w_xq[0], w_xo[0], w_xk[0], w_xv[0]))
    hp, mk, mv = _prompt_xattn(hp, row(g_xattn[0]), w_xq_b, w_xo_b,
                               mem_prompt, row(g_mem[0]), w_xk_b, w_xv_b)

    win_t = lambda c: jnp.transpose(c, (0, 2, 3, 1)).reshape(nb, ATT_KV_W, WINDOW)
    win_t_inv = lambda a: jnp.transpose(a.reshape(-1, N_KV_HEADS, HEAD_DIM, WINDOW),
                                        (0, 3, 1, 2)).reshape(1, -1, WINDOW, N_KV_HEADS, HEAD_DIM)
    hs, s_wk, s_wv, s_rs = _sample_mixer(
        x_sample, row(g_mix[0]), w_in_b, w_out_b, sinks, gn_g, gn_b,
        win_t(cache_win_k[0]), win_t(cache_win_v[0]), state_ret[0].reshape(nb, RET_QK_W, RET_V_DIM))

    y_prompt, y_sample = _mlp_xattn(
        hp.reshape(b * s, d), hs, row(g_xattn[0]), w_xq_b, w_xo_b,
        _mem_rows(cache_mem_k[0]), _mem_rows(cache_mem_v[0]), row(g_mlp[0]), w_up_b, w_dn_b, g_fin)
    y_prompt = y_prompt.reshape(b, s, d)

    ret5 = lambda a, n: a.reshape(1, n, N_RET_HEADS, RET_QK_DIM, RET_V_DIM)
    mem5 = lambda a: jnp.transpose(a.reshape(b, N_MEM, X_D_HALVES, N_X_HEADS, LANES),
                                   (0, 1, 3, 2, 4)).reshape(1, b, N_MEM, N_X_HEADS, X_HEAD_DIM)
    return (y_prompt, y_sample,
            win_t_inv(p_wk), win_t_inv(p_wv), ret5(p_rs, b), mem5(mk), mem5(mv),
            win_t_inv(s_wk), win_t_inv(s_wv), ret5(s_rs, nb))
```

```python
import functools

import jax
import jax.numpy as jnp
import numpy as np
from jax import lax
from jax.experimental import pallas as pl
from jax.experimental.pallas import tpu as pltpu

F32 = jnp.float32
BF16 = jnp.bfloat16

D_MODEL = 1024
BATCH = 8
SEQ = 2048
DEC_BATCH = 128
DEC_SEQ = 4
HEAD_DIM = 64
N_ATT_HEADS = 8
N_KV_HEADS = 2
KV_GROUP = N_ATT_HEADS // N_KV_HEADS
WINDOW = 128
BLK = 128
N_RET_HEADS = 4
RET_QK_DIM = 64
RET_V_DIM = 128
N_MEM = 256
N_X_HEADS = 4
X_HEAD_DIM = D_MODEL // N_X_HEADS
D_FF = 4 * D_MODEL
RMS_EPS = 1e-6
GN_EPS = 1e-5

ATT_Q_W = N_ATT_HEADS * HEAD_DIM
ATT_KV_W = N_KV_HEADS * HEAD_DIM
RET_QK_W = N_RET_HEADS * RET_QK_DIM
RET_V_W = N_RET_HEADS * RET_V_DIM
MIX_OUT = ATT_Q_W + RET_V_W
D_IN = ATT_Q_W + 2 * ATT_KV_W + 2 * RET_QK_W + 2 * RET_V_W
C_QA, C_KV, C_QKR, C_VR, C_GR = 0, 512, 768, 1280, 1792

LANES = 128
SUBLANES = 8
HALF = LANES // 2
X_D_HALVES = X_HEAD_DIM // LANES
NEG = -1e30
VMEM_LIMIT = 56 * 1024 * 1024

TM_MIX = 512
TM_X = 2048
SUB_ROWS = 512
FF_CHUNK = 1024
BB_MIX = 32
BB_X = 4

NEG_SLOPES = [-(2.0 ** (-8.0 * (i + 1) / N_ATT_HEADS)) for i in range(N_ATT_HEADS)]
_LOG_G = np.log(1.0 - 2.0 ** (-5.0 - np.arange(N_RET_HEADS))).astype(np.float32).astype(np.float64)


def _prompt_tables():
    qi = np.arange(BLK)[:, None]
    kj = np.arange(2 * BLK)[None, :]
    dist = (qi + BLK - kj).astype(np.float64)
    mask = np.where((dist >= 0) & (dist < WINDOW), 0.0, NEG)
    l = np.arange(BLK, dtype=np.float64)
    diff = l[:, None] - l[None, :]
    decay = np.where(diff >= 0, np.exp(_LOG_G[:, None, None] * np.maximum(diff, 0.0)), 0.0)
    xi = np.exp((l[:, None] + 1.0) * _LOG_G[None, :])
    zeta = np.exp((BLK - 1.0 - l)[:, None] * _LOG_G[None, :])
    xi_t = np.repeat(xi, RET_V_DIM, axis=1)
    zeta_t = np.repeat(zeta, RET_QK_DIM, axis=1)
    f = lambda a: np.asarray(a, np.float32)
    return f(dist), f(mask), f(decay), f(xi_t), f(zeta_t)


def _sample_tables():
    slopes = -np.asarray(NEG_SLOPES)
    bias = np.full((2, N_ATT_HEADS * SUBLANES, 2 * BLK), NEG, np.float64)
    dec = np.zeros((2, N_RET_HEADS * SUBLANES, BLK), np.float64)
    xi = np.zeros((2, N_RET_HEADS * SUBLANES, RET_V_DIM), np.float64)
    zeta = np.zeros((2, SUBLANES, RET_QK_W), np.float64)
    for par in range(2):
        for r in range(SUBLANES):
            own = DEC_SEQ * par <= r < DEC_SEQ * (par + 1)
            t = r - DEC_SEQ * par if own else r % DEC_SEQ
            for h in range(N_ATT_HEADS):
                row = h * SUBLANES + r
                for j in range(WINDOW):
                    d = t + WINDOW - j
                    if 0 <= d < WINDOW:
                        bias[par, row, j] = -slopes[h] * d
                for c in range(DEC_SEQ):
                    d = t - c
                    if d >= 0:
                        bias[par, row, WINDOW + DEC_SEQ * par + c] = -slopes[h] * d
            for h in range(N_RET_HEADS):
                row = h * SUBLANES + r
                if own:
                    xi[par, row, :] = np.exp((t + 1.0) * _LOG_G[h])
                    zeta[par, r, h * RET_QK_DIM:(h + 1) * RET_QK_DIM] = np.exp((DEC_SEQ - 1.0 - t) * _LOG_G[h])
                    for c in range(t + 1):
                        dec[par, row, DEC_SEQ * par + c] = np.exp(_LOG_G[h] * (t - c))
    f = lambda a: np.asarray(a, np.float32)
    return f(bias), f(dec), f(xi), f(zeta)


_P_DIST, _P_MASK, _P_DECAY, _P_XI, _P_ZETA = _prompt_tables()
_S_BIAS, _S_DEC, _S_XI, _S_ZETA = _sample_tables()
_GL_PROMPT = [float(np.exp(_LOG_G[h] * BLK)) for h in range(N_RET_HEADS)]
_GL_SAMPLE = [float(np.exp(_LOG_G[h] * DEC_SEQ)) for h in range(N_RET_HEADS)]


def _rms(x, g):
    return x * lax.rsqrt(jnp.mean(x * x, axis=-1, keepdims=True) + RMS_EPS) * g


def _dot(a, b):
    return jnp.dot(a, b, preferred_element_type=F32)


def _dot_nt(a, b):
    return lax.dot_general(a, b, (((1,), (1,)), ((), ())), preferred_element_type=F32)


def _dot_tn(a, b):
    return lax.dot_general(a, b, (((0,), (0,)), ((), ())), preferred_element_type=F32)


def _silu(g):
    return g * (1.0 / (1.0 + jnp.exp(-g)))


def _half_masks(width):
    lane = lax.broadcasted_iota(jnp.int32, (1, width), 1)
    lo = ((lane & (LANES - 1)) < HALF).astype(F32)
    return lo, 1.0 - lo


def _sink_softmax(s, sink):
    m = jnp.maximum(jnp.max(s, axis=-1, keepdims=True), sink)
    p = jnp.exp(s - m)
    den = jnp.sum(p, axis=-1, keepdims=True) + jnp.exp(sink - m)
    return p * (1.0 / den)


def _group_norm(o, g, b):
    mu = jnp.mean(o, axis=-1, keepdims=True)
    d = o - mu
    var = jnp.mean(d * d, axis=-1, keepdims=True)
    return d * lax.rsqrt(var + GN_EPS) * g + b


def _pm_project_stages(x, slot, gmix_ref, win_ref, sc, kv_out=None):
    tm = x.shape[0]
    xn = _rms(x, gmix_ref[...]).astype(BF16)

    pw = 2 * LANES
    lo, hi = _half_masks(pw)

    def stage_q(i):
        cols = slice(i * pw, (i + 1) * pw)
        q = _dot(xn, win_ref[:, C_QA + i * pw:C_QA + (i + 1) * pw])
        sc["qlo"][slot, :, cols] = (q * (lo * HEAD_DIM ** -0.5)).astype(BF16)
        sc["qhi"][slot, :, cols] = (q * (hi * HEAD_DIM ** -0.5)).astype(BF16)

    def stage_kv():
        z = _dot(xn, win_ref[:, C_KV:C_KV + pw])
        low = lax.broadcasted_iota(jnp.int32, (tm, LANES), 1) < HALF
        k = z[:, 0:ATT_KV_W]
        v = z[:, ATT_KV_W:2 * ATT_KV_W]
        if kv_out is not None:
            kv_out[0][0] = k[tm - WINDOW:, :].T
            kv_out[1][0] = v[tm - WINDOW:, :].T
        k_r = pltpu.roll(k, HALF, axis=1)
        v_r = pltpu.roll(v, HALF, axis=1)
        sc["kd0"][slot] = jnp.where(low, k, k_r).astype(BF16)
        sc["kd1"][slot] = jnp.where(low, k_r, k).astype(BF16)
        sc["vd0"][slot, :, 0:LANES] = jnp.where(low, v, 1.0).astype(BF16)
        sc["vd0"][slot, :, LANES:2 * LANES] = jnp.where(low, 1.0, v_r).astype(BF16)
        sc["vd1"][slot, :, 0:LANES] = jnp.where(low, v_r, 1.0).astype(BF16)
        sc["vd1"][slot, :, LANES:2 * LANES] = jnp.where(low, 1.0, v).astype(BF16)

    def stage_qr():
        qr = _dot(xn, win_ref[:, C_QKR:C_QKR + pw])
        sc["qrlo"][slot] = (qr * lo).astype(BF16)
        sc["qrhi"][slot] = (qr * hi).astype(BF16)

    def stage_kr():
        sc["kr"][slot] = _dot(xn, win_ref[:, C_QKR + pw:C_QKR + 2 * pw]) * (RET_QK_DIM ** -0.5)

    def stage_vr(i):
        cols = slice(i * pw, (i + 1) * pw)
        sc["vr"][slot, :, cols] = _dot(xn, win_ref[:, C_VR + i * pw:C_VR + (i + 1) * pw]).astype(BF16)

    def stage_gate(i):
        cols = slice(i * pw, (i + 1) * pw)
        sc["gate"][slot, :, cols] = _silu(_dot(xn, win_ref[:, C_GR + i * pw:C_GR + (i + 1) * pw]))

    part = functools.partial
    return [part(stage_q, 0), part(stage_q, 1), stage_kv, stage_qr, stage_kr,
            part(stage_vr, 0), part(stage_vr, 1), part(stage_gate, 0), part(stage_gate, 1)]


def _pm_last_block(slot, tm, sc):
    rows = slice(tm - BLK, tm)
    return ([sc["kd0"][slot, rows, :], sc["kd1"][slot, rows, :]],
            [sc["vd0"][slot, rows, :], sc["vd1"][slot, rows, :]])


def _pm_blocks(slot, prev_kd, prev_vd, is_first, state, fillers, tm, sinks_ref, gng_ref, gnb_ref,
               decay_ref, xi_ref, zeta_ref, sc):
    nblk = tm // BLK
    n_units = nblk * (N_KV_HEADS + N_RET_HEADS // 2)
    pending = list(fillers)
    done_units = [0]

    def unit_done():
        done_units[0] += 1
        while pending and (len(fillers) - len(pending)) * n_units < done_units[0] * len(fillers):
            pending.pop(0)()
    lowb = lax.broadcasted_iota(jnp.int32, (BLK, LANES), 1) < HALF
    col = lax.broadcasted_iota(jnp.int32, (BLK, 2 * BLK), 1)
    first_mask = None if is_first is False else jnp.where((col < BLK) & is_first, NEG, 0.0)
    kd_refs = (sc["kd0"], sc["kd1"])
    vd_refs = (sc["vd0"], sc["vd1"])
    qlo, qhi, mix = sc["qlo"], sc["qhi"], sc["mix"]
    n_pairs = N_RET_HEADS // 2

    for j in range(nblk):
        rows = slice(j * BLK, (j + 1) * BLK)
        c0s = [kvh * KV_GROUP * HEAD_DIM for kvh in range(N_KV_HEADS)]
        lsls = [slice(i * LANES, (i + 1) * LANES) for i in range(n_pairs)]
        vds, scores = [], []
        for kvh in range(N_KV_HEADS):
            if j == 0:
                kd = jnp.concatenate([prev_kd[kvh], kd_refs[kvh][slot, rows, :]], axis=0)
                vds.append(jnp.concatenate([prev_vd[kvh], vd_refs[kvh][slot, rows, :]], axis=0))
            else:
                krows = slice((j - 1) * BLK, (j + 1) * BLK)
                kd = kd_refs[kvh][slot, krows, :]
                vds.append(vd_refs[kvh][slot, krows, :])
            c0 = c0s[kvh]
            qst = jnp.concatenate([qlo[slot, rows, c0:c0 + LANES], qhi[slot, rows, c0:c0 + LANES],
                                   qlo[slot, rows, c0 + LANES:c0 + 2 * LANES],
                                   qhi[slot, rows, c0 + LANES:c0 + 2 * LANES]], axis=0)
            scores.append(_dot_nt(qst, kd))
        unit_done()

        kps = [sc["kr"][slot, rows, lsls[i]] for i in range(n_pairs)]
        vpairs = [sc["vr"][slot, rows, 2 * i * RET_V_DIM:(2 * i + 2) * RET_V_DIM] for i in range(n_pairs)]
        q2s = [jnp.concatenate([sc["qrlo"][slot, rows, lsls[i]], sc["qrhi"][slot, rows, lsls[i]]], axis=0)
               for i in range(n_pairs)]
        a_s = [_dot_nt(q2s[i], kps[i].astype(BF16)) for i in range(n_pairs)]
        ocs = [_dot(q2s[i], state[i].astype(BF16)) for i in range(n_pairs)]
        us = [_dot_tn((kps[i] * zeta_ref[:, lsls[i]]).astype(BF16), vpairs[i]) for i in range(n_pairs)]
        unit_done()

        for kvh in range(N_KV_HEADS):
            s, vd, c0 = scores[kvh], vds[kvh], c0s[kvh]
            es, esink = [], []
            for g in range(KV_GROUP):
                h = kvh * KV_GROUP + g
                sg = s[g * BLK:(g + 1) * BLK] + sc["bias"][h]
                if j == 0 and first_mask is not None:
                    sg = sg + first_mask
                sink = sinks_ref[h]
                m = jnp.maximum(jnp.max(sg, axis=-1, keepdims=True), sink)
                es.append(jnp.exp(sg - m).astype(BF16))
                esink.append(jnp.exp(sink - m))
            o = _dot(jnp.concatenate(es, axis=0), vd)
            for pair in range(KV_GROUP // 2):
                oe = o[2 * pair * BLK:(2 * pair + 1) * BLK]
                oo = o[(2 * pair + 1) * BLK:(2 * pair + 2) * BLK]
                num = jnp.where(lowb, oe[:, :LANES], oo[:, LANES:])
                den = (jnp.where(lowb, oe[:, LANES:], oo[:, :LANES])
                       + jnp.where(lowb, esink[2 * pair], esink[2 * pair + 1]))
                cs = c0 + pair * LANES
                mix[slot, rows, cs:cs + LANES] = (num * (1.0 / den)).astype(BF16)
            unit_done()

        for i in range(n_pairs):
            a, oc, u, sp = a_s[i], ocs[i], us[i], state[i]
            inner = jnp.concatenate([a[:BLK] * decay_ref[2 * i], a[BLK:] * decay_ref[2 * i + 1]], axis=0)
            oi = _dot(inner.astype(BF16), vpairs[i])
            for half in range(2):
                h = 2 * i + half
                vsl = slice(h * RET_V_DIM, (h + 1) * RET_V_DIM)
                hr = slice(half * BLK, (half + 1) * BLK)
                o = oi[hr, half * RET_V_DIM:(half + 1) * RET_V_DIM] + oc[hr] * xi_ref[:, vsl]
                r = _group_norm(o, gng_ref[:, vsl], gnb_ref[:, vsl]) * sc["gate"][slot, rows, vsl]
                mix[slot, rows, ATT_Q_W + h * RET_V_DIM:ATT_Q_W + (h + 1) * RET_V_DIM] = r.astype(BF16)
            state[i] = jnp.concatenate(
                [_GL_PROMPT[2 * i] * sp[:RET_QK_DIM] + u[:RET_QK_DIM, :RET_V_DIM],
                 _GL_PROMPT[2 * i + 1] * sp[RET_QK_DIM:] + u[RET_QK_DIM:, RET_V_DIM:]], axis=0)

    assert not pending
    return state


def _pm_wout_pieces(slot, x_ref, rows, wout_ref, h_ref, sc):
    pw = 2 * LANES
    n = D_MODEL // pw
    parts = []

    def piece(k):
        parts.append(_dot(sc["mix"][slot], wout_ref[:, k * pw:(k + 1) * pw]))
        if k == n - 1:
            h_ref[rows, :] = x_ref[...] + jnp.concatenate(parts, axis=1)

    return [functools.partial(piece, k) for k in range(n)]


def _prompt_mixer_kernel(sinks_ref, xfirst_ref, xodd_ref, xnext_ref, gmix_ref, win_ref, wout_ref, gng_ref, gnb_ref,
                           dist_ref, mask_ref, decay_ref, xi_ref, zeta_ref,
                           wupf_ref, wdnf_ref, wqf_ref, wof_ref,
                           h_ref, wk_ref, wv_ref, st_ref,
                           wupb_ref, wdnb_ref, wqb_ref, wob_ref,
                           qlo_s, qhi_s, kd0_s, kd1_s, vd0_s, vd1_s,
                           qrlo_s, qrhi_s, kr_s, vr_s, gate_s, mix_s, bias_s, state_s, xkeep_s):
    u = pl.program_id(0)
    tm = xnext_ref.shape[0]
    wupb_ref[...] = wupf_ref[...].astype(BF16)
    wdnb_ref[...] = wdnf_ref[...].astype(BF16)
    wqb_ref[...] = wqf_ref[...].astype(BF16)
    wob_ref[...] = wof_ref[...].astype(BF16)
    sc = dict(qlo=qlo_s, qhi=qhi_s, kd0=kd0_s, kd1=kd1_s, vd0=vd0_s, vd1=vd1_s, qrlo=qrlo_s, qrhi=qrhi_s,
              kr=kr_s, vr=vr_s, gate=gate_s, mix=mix_s, bias=bias_s)
    n_pairs = N_RET_HEADS // 2
    blocks = functools.partial(_pm_blocks, tm=tm, sinks_ref=sinks_ref, gng_ref=gng_ref,
                               gnb_ref=gnb_ref, decay_ref=decay_ref, xi_ref=xi_ref, zeta_ref=zeta_ref, sc=sc)

    @pl.when(u == 0)
    def _():
        for h in range(N_ATT_HEADS):
            bias_s[h] = NEG_SLOPES[h] * dist_ref[...] + mask_ref[...]
        state_s[...] = jnp.zeros_like(state_s)
        kd0_s[1] = jnp.zeros(kd0_s.shape[1:], BF16)
        kd1_s[1] = jnp.zeros(kd1_s.shape[1:], BF16)
        vd0_s[1] = jnp.zeros(vd0_s.shape[1:], BF16)
        vd1_s[1] = jnp.zeros(vd1_s.shape[1:], BF16)
        xkeep_s[...] = xfirst_ref[...]
        for stage in _pm_project_stages(xfirst_ref[...], 0, gmix_ref, win_ref, sc):
            stage()

    seq_start = (u % 2) == 0
    state = [jnp.where(seq_start, 0.0, state_s[i * LANES:(i + 1) * LANES, :]) for i in range(n_pairs)]

    prev_kd, prev_vd = _pm_last_block(1, tm, sc)
    stages = _pm_project_stages(xodd_ref[...], 1, gmix_ref, win_ref, sc, kv_out=(wk_ref, wv_ref))
    state = blocks(0, prev_kd, prev_vd, seq_start, state, stages)
    wout0 = _pm_wout_pieces(0, xkeep_s, slice(0, tm), wout_ref, h_ref, sc)

    prev_kd, prev_vd = _pm_last_block(0, tm, sc)
    stages = _pm_project_stages(xnext_ref[...], 0, gmix_ref, win_ref, sc)
    state = blocks(1, prev_kd, prev_vd, False, state, wout0 + stages)
    for piece in _pm_wout_pieces(1, xodd_ref, slice(tm, 2 * tm), wout_ref, h_ref, sc):
        piece()
    xkeep_s[...] = xnext_ref[...]

    for i in range(n_pairs):
        state_s[i * LANES:(i + 1) * LANES, :] = state[i]
        st_ref[0, i * LANES:(i + 1) * LANES, :] = state[i]


def _prompt_mixer(x, g_mix, w_in, w_out, sinks, gn_g, gn_b, side_f32):
    b, s, d = x.shape
    tm = TM_MIX
    n_tiles = b * s // tm
    steps = n_tiles // 2
    seq_steps = s // (2 * tm)
    assert s % (2 * tm) == 0 and seq_steps == 2, "kernel assumes 4 tiles per sequence"
    x2d = x.reshape(b * s, d)
    const = lambda shape: pl.BlockSpec(shape, lambda i: (0,) * len(shape), pipeline_mode=pl.Buffered(1))
    slot2 = lambda rows, cols, dt: pltpu.VMEM((2, rows, cols), dt)
    side_specs = [pl.BlockSpec((w.shape[0] // steps, w.shape[1]), lambda i: (i, 0)) for w in side_f32]
    outs = pl.pallas_call(
        _prompt_mixer_kernel,
        grid=(steps,),
        in_specs=[
            pl.BlockSpec(memory_space=pltpu.SMEM),
            const((tm, d)),
            pl.BlockSpec((tm, d), lambda i: (2 * i + 1, 0)),
            pl.BlockSpec((tm, d), lambda i: (jnp.minimum(2 * i + 2, n_tiles - 1), 0)),
            const((1, d)), const((d, D_IN)), const((MIX_OUT, d)),
            const((1, RET_V_W)), const((1, RET_V_W)),
            const((BLK, 2 * BLK)), const((BLK, 2 * BLK)),
            const((N_RET_HEADS, BLK, BLK)), const((BLK, RET_V_W)), const((BLK, RET_QK_W)),
        ] + side_specs,
        out_specs=[
            pl.BlockSpec((2 * tm, d), lambda i: (i, 0)),
            pl.BlockSpec((1, WINDOW, ATT_KV_W), lambda i: (i // seq_steps, 0, 0)),
            pl.BlockSpec((1, WINDOW, ATT_KV_W), lambda i: (i // seq_steps, 0, 0)),
            pl.BlockSpec((1, RET_QK_W, RET_V_DIM), lambda i: (i // seq_steps, 0, 0)),
        ] + side_specs,
        out_shape=[
            jax.ShapeDtypeStruct((b * s, d), F32),
            jax.ShapeDtypeStruct((b, WINDOW, ATT_KV_W), F32),
            jax.ShapeDtypeStruct((b, WINDOW, ATT_KV_W), F32),
            jax.ShapeDtypeStruct((b, RET_QK_W, RET_V_DIM), F32),
        ] + [jax.ShapeDtypeStruct(w.shape, BF16) for w in side_f32],
        scratch_shapes=[
            slot2(tm, ATT_Q_W, BF16), slot2(tm, ATT_Q_W, BF16),
            slot2(tm, LANES, BF16), slot2(tm, LANES, BF16),
            slot2(tm, 2 * LANES, BF16), slot2(tm, 2 * LANES, BF16),
            slot2(tm, RET_QK_W, BF16), slot2(tm, RET_QK_W, BF16),
            slot2(tm, RET_QK_W, F32), slot2(tm, RET_V_W, BF16),
            slot2(tm, RET_V_W, F32), slot2(tm, MIX_OUT, BF16),
            pltpu.VMEM((N_ATT_HEADS, BLK, 2 * BLK), F32),
            pltpu.VMEM((RET_QK_W, RET_V_DIM), F32),
            pltpu.VMEM((tm, d), F32),
        ],
        compiler_params=pltpu.CompilerParams(
            dimension_semantics=("arbitrary",), vmem_limit_bytes=VMEM_LIMIT),
        name="prompt_mixer",
    )(sinks, x2d, x2d, x2d, g_mix, w_in, w_out, gn_g, gn_b,
      jnp.asarray(_P_DIST), jnp.asarray(_P_MASK), jnp.asarray(_P_DECAY), jnp.asarray(_P_XI),
      jnp.asarray(_P_ZETA), *side_f32)
    return (outs[0].reshape(b, s, d),) + tuple(outs[1:])


def _memkv_kernel(mem_ref, g_ref, wk_ref, wv_ref, win_ref, wout_ref,
                  mk_ref, mv_ref, mkb_ref, mvb_ref, winb_ref, woutb_ref):
    winb_ref[...] = win_ref[...].astype(BF16)
    woutb_ref[...] = wout_ref[...].astype(BF16)
    mn = _rms(mem_ref[...], g_ref[...]).astype(BF16)
    mk = _dot(mn, wk_ref[...].astype(BF16))
    mv = _dot(mn, wv_ref[...].astype(BF16))
    tm = mem_ref.shape[0]
    group = X_D_HALVES * N_X_HEADS
    for hd in range(N_X_HEADS):
        for dh in range(X_D_HALVES):
            cols = slice(hd * X_HEAD_DIM + dh * LANES, hd * X_HEAD_DIM + (dh + 1) * LANES)
            rows = pl.ds(dh * N_X_HEADS + hd, tm, stride=group)
            mk_ref[rows, :] = mk[:, cols]
            mv_ref[rows, :] = mv[:, cols]
    mkb_ref[...] = mk.astype(BF16)
    mvb_ref[...] = mv.astype(BF16)


def _memory_kv(mem2d, g_mem, w_xk, w_xv, w_in, w_out):
    n, d = mem2d.shape
    tm = 512
    row = pl.BlockSpec((tm, d), lambda i: (i, 0))
    rows_out = pl.BlockSpec((tm * d // LANES, LANES), lambda i: (i, 0))
    const = lambda shape: pl.BlockSpec(shape, lambda i: (0,) * len(shape), pipeline_mode=pl.Buffered(1))
    steps = n // tm
    win_blk = pl.BlockSpec((w_in.shape[0] // steps, w_in.shape[1]), lambda i: (i, 0))
    wout_blk = pl.BlockSpec((w_out.shape[0] // steps, w_out.shape[1]), lambda i: (i, 0))
    return pl.pallas_call(
        _memkv_kernel,
        grid=(n // tm,),
        in_specs=[row, const((1, d)), const((d, d)), const((d, d)), win_blk, wout_blk],
        out_specs=[rows_out, rows_out, row, row, win_blk, wout_blk],
        out_shape=[jax.ShapeDtypeStruct((n * d // LANES, LANES), F32),
                   jax.ShapeDtypeStruct((n * d // LANES, LANES), F32),
                   jax.ShapeDtypeStruct((n, d), BF16), jax.ShapeDtypeStruct((n, d), BF16),
                   jax.ShapeDtypeStruct(w_in.shape, BF16), jax.ShapeDtypeStruct(w_out.shape, BF16)],
        compiler_params=pltpu.CompilerParams(
            dimension_semantics=("arbitrary",), vmem_limit_bytes=VMEM_LIMIT),
        name="memory_kv",
    )(mem2d, g_mem, w_xk, w_xv, w_in, w_out)


def _prompt_xattn_kernel(h_ref, g_ref, wq_ref, wo_ref, mk_ref, mv_ref, out_ref, o_s):
    def stages(r0):
        rows = slice(r0, r0 + SUB_ROWS)
        env = {}

        def project():
            env["h"] = h_ref[0, rows, :]
            xn = _rms(env["h"], g_ref[...]).astype(BF16)
            env["q"] = (_dot(xn, wq_ref[...]) * (X_HEAD_DIM ** -0.5)).astype(BF16)

        def scores(hd):
            sl = slice(hd * X_HEAD_DIM, (hd + 1) * X_HEAD_DIM)
            env[hd] = _dot_nt(env["q"][:, sl], mk_ref[0, :, sl])

        def head(hd):
            if hd + 1 < N_X_HEADS:
                scores(hd + 1)
            sl = slice(hd * X_HEAD_DIM, (hd + 1) * X_HEAD_DIM)
            s = env.pop(hd)
            m = jnp.max(s, axis=-1, keepdims=True)
            p = jnp.exp(s - m)
            p = p * (1.0 / jnp.sum(p, axis=-1, keepdims=True))
            o_s[rows, sl] = _dot(p.astype(BF16), mv_ref[0, :, sl]).astype(BF16)

        def output():
            out_ref[0, rows, :] = env["h"] + _dot(o_s[rows, :], wo_ref[...])

        def project_and_first_scores():
            project()
            scores(0)

        return ([project_and_first_scores] + [functools.partial(head, hd) for hd in range(N_X_HEADS)]
                + [output])

    chains = [stages(r0) for r0 in range(0, h_ref.shape[1], SUB_ROWS)]
    n_stage = len(chains[0])
    for step in range(n_stage + len(chains) - 1):
        for lag, chain in enumerate(chains):
            if 0 <= step - lag < n_stage:
                chain[step - lag]()


def _prompt_xattn(h, g, w_xq, w_xo, mkb, mvb):
    b, s, d = h.shape
    tm = TM_X
    const = lambda shape: pl.BlockSpec(shape, lambda i, j: (0,) * len(shape))
    tok = pl.BlockSpec((1, tm, d), lambda i, j: (i, j, 0))
    mem = pl.BlockSpec((1, N_MEM, d), lambda i, j: (i, 0, 0))
    return pl.pallas_call(
        _prompt_xattn_kernel,
        grid=(b, s // tm),
        in_specs=[tok, const((1, d)), const((d, d)), const((d, d)), mem, mem],
        out_specs=tok,
        out_shape=jax.ShapeDtypeStruct((b, s, d), F32),
        scratch_shapes=[pltpu.VMEM((tm, d), BF16)],
        compiler_params=pltpu.CompilerParams(
            dimension_semantics=("arbitrary", "arbitrary"), vmem_limit_bytes=VMEM_LIMIT),
        name="prompt_xattn",
    )(h, g, w_xq, w_xo, mkb, mvb)


def _sample_mixer_kernel(sinks_ref, x_ref, gmix_ref, win_ref, wout_ref, gng_ref, gnb_ref,
                         ck_ref, cv_ref, st_ref, bias_ref, dec_ref, xi_ref, zeta_ref,
                         h_ref, swk_ref, swv_ref, sst_ref):
    bb = ck_ref.shape[0]
    nt = bb // 2
    x = x_ref[...].reshape(bb * DEC_SEQ, D_MODEL)
    xn = _rms(x, gmix_ref[...]).astype(BF16)
    tile3 = lambda a: a.reshape(nt, SUBLANES, a.shape[-1])

    q = _dot(xn, win_ref[:, C_QA:C_QA + ATT_Q_W]) * (HEAD_DIM ** -0.5)
    kv = _dot(xn, win_ref[:, C_KV:C_KV + 2 * ATT_KV_W])
    qkr = _dot(xn, win_ref[:, C_QKR:C_QKR + 2 * RET_QK_W])
    vr = _dot(xn, win_ref[:, C_VR:C_VR + RET_V_W])
    gate3 = tile3(_silu(_dot(xn, win_ref[:, C_GR:C_GR + RET_V_W])))

    lo512, hi512 = _half_masks(ATT_Q_W)
    q_r = pltpu.roll(q, HALF, axis=1)
    q_nat3 = tile3(q)
    q_rot3 = tile3(q_r)
    lo3 = lo512.reshape(1, 1, ATT_Q_W)
    hi3 = hi512.reshape(1, 1, ATT_Q_W)
    qa3 = (q_nat3 * lo3).astype(BF16)
    qb3 = (q_rot3 * lo3).astype(BF16)
    qc3 = (q_rot3 * hi3).astype(BF16)
    qd3 = (q_nat3 * hi3).astype(BF16)
    t128 = lambda a, i: a[:, :, i * LANES:(i + 1) * LANES]
    qs = jnp.concatenate([t128(qa3, 0), t128(qb3, 1), t128(qa3, 1), t128(qb3, 2),
                          t128(qc3, 2), t128(qd3, 2), t128(qc3, 3), t128(qd3, 3)], axis=1)

    k3 = tile3(kv[:, :ATT_KV_W])
    v3 = tile3(kv[:, ATT_KV_W:])
    pad_kv = jnp.zeros((nt, BLK - SUBLANES, LANES), BF16)
    knew_pad = jnp.concatenate([k3.astype(BF16), pad_kv], axis=1)
    vnew_pad = jnp.concatenate([v3.astype(BF16), pad_kv], axis=1)
    to_lanes = lambda a3: jnp.swapaxes(
        jnp.concatenate([a3, jnp.zeros((nt, BLK - SUBLANES, LANES), F32)], axis=1), 1, 2)
    k3t, v3t = to_lanes(k3), to_lanes(v3)
    roll3 = lambda a, sh: pltpu.roll(a.reshape(nt * BLK, LANES), sh, axis=1).reshape(nt, BLK, LANES)

    lo256, _ = _half_masks(RET_QK_W)
    qr3 = tile3(qkr[:, :RET_QK_W])
    kr3 = tile3(qkr[:, RET_QK_W:] * (RET_QK_DIM ** -0.5))
    vr3 = tile3(vr)
    lane256 = lax.broadcasted_iota(jnp.int32, (1, 1, RET_QK_W), 2)
    qrs = jnp.concatenate(
        [(qr3 * ((lane256 >= h * RET_QK_DIM) & (lane256 < (h + 1) * RET_QK_DIM)).astype(F32)).astype(BF16)
         for h in range(N_RET_HEADS)],
        axis=1)
    kr_pad = jnp.concatenate([kr3.astype(BF16), jnp.zeros((nt, BLK - SUBLANES, RET_QK_W), BF16)], axis=1)
    vr_pad = jnp.concatenate([vr3.astype(BF16), jnp.zeros((nt, BLK - SUBLANES, RET_V_W), BF16)], axis=1)

    lane = lax.broadcasted_iota(jnp.int32, (1, 1, LANES), 2)
    row8 = lax.broadcasted_iota(jnp.int32, (1, SUBLANES, 1), 1)
    bmm_nt = lambda a, b: jnp.einsum('bqd,bkd->bqk', a, b, preferred_element_type=F32)
    bmm = lambda a, b: jnp.einsum('bqk,bkd->bqd', a, b, preferred_element_type=F32)

    att_par, ret_par = [], []
    for par in range(2):
        bsl = pl.ds(par, nt, stride=2)
        ckt = ck_ref[bsl]
        cvt = cv_ref[bsl]
        keep = lane < WINDOW - DEC_SEQ
        new_shift = WINDOW - DEC_SEQ - DEC_SEQ * par
        swk_ref[bsl] = jnp.where(keep, roll3(ckt, WINDOW - DEC_SEQ), roll3(k3t, new_shift))
        swv_ref[bsl] = jnp.where(keep, roll3(cvt, WINDOW - DEC_SEQ), roll3(v3t, new_shift))

        s = jnp.concatenate([bmm(qs, ckt.astype(BF16)), bmm_nt(qs, knew_pad)], axis=2) + bias_ref[par]
        ps = []
        for h in range(N_ATT_HEADS):
            ps.append(_sink_softmax(s[:, h * SUBLANES:(h + 1) * SUBLANES, :], sinks_ref[h]).astype(BF16))
        p_all = jnp.concatenate(ps, axis=1)
        o = bmm_nt(p_all[:, :, :BLK], cvt.astype(BF16)) + bmm(p_all[:, :, BLK:], vnew_pad)
        o_r = pltpu.roll(o.reshape(nt * N_ATT_HEADS * SUBLANES, LANES), HALF, axis=1).reshape(o.shape)
        hr = lambda a, h: a[:, h * SUBLANES:(h + 1) * SUBLANES, :]
        low = lane < HALF
        att_par.append(jnp.concatenate([
            jnp.where(low, hr(o, 0), hr(o_r, 1)), jnp.where(low, hr(o, 2), hr(o_r, 3)),
            jnp.where(low, hr(o_r, 4), hr(o, 5)), jnp.where(low, hr(o_r, 6), hr(o, 7))], axis=2))

        st = st_ref[bsl]
        oc = bmm(qrs, st.astype(BF16))
        inner = (bmm_nt(qrs, kr_pad) * dec_ref[par]).astype(BF16)
        oi = bmm(inner, vr_pad)
        rs = []
        for h in range(N_RET_HEADS):
            vsl = slice(h * RET_V_DIM, (h + 1) * RET_V_DIM)
            rsl = slice(h * SUBLANES, (h + 1) * SUBLANES)
            o_h = oi[:, rsl, vsl] + oc[:, rsl, :] * xi_ref[par, rsl, :]
            rs.append(_group_norm(o_h, gng_ref[:, vsl], gnb_ref[:, vsl]) * gate3[:, :, vsl])
        ret_par.append(jnp.concatenate(rs, axis=2))

        kz3 = (kr3 * zeta_ref[par]).astype(BF16)
        vr3_b = vr3.astype(BF16)
        for p in range(nt):
            for i in range(N_RET_HEADS // 2):
                u = _dot_tn(kz3[p][:, i * LANES:(i + 1) * LANES],
                            vr3_b[p][:, 2 * i * RET_V_DIM:(2 * i + 2) * RET_V_DIM])
                for half in range(2):
                    h = 2 * i + half
                    dsl = slice(h * RET_QK_DIM, (h + 1) * RET_QK_DIM)
                    sst_ref[2 * p + par, dsl, :] = (
                        _GL_SAMPLE[h] * st[p, dsl, :]
                        + u[half * RET_QK_DIM:(half + 1) * RET_QK_DIM, half * RET_V_DIM:(half + 1) * RET_V_DIM])

    own0 = row8 < DEC_SEQ
    att3 = jnp.where(own0, att_par[0], att_par[1])
    ret3 = jnp.where(own0, ret_par[0], ret_par[1])
    mix = jnp.concatenate([att3, ret3], axis=2).reshape(2 * nt * DEC_SEQ, MIX_OUT).astype(BF16)
    h_ref[...] = x + _dot(mix, wout_ref[...])


def _sample_mixer(x3d, g_mix, w_in, w_out, sinks, gn_g, gn_b, ck, cv, st):
    nb, ls, d = x3d.shape
    n = nb * ls
    bb = BB_MIX
    r = bb * DEC_SEQ
    const = lambda shape: pl.BlockSpec(shape, lambda i: (0,) * len(shape))
    row = pl.BlockSpec((r, d), lambda i: (i, 0))
    win = pl.BlockSpec((bb, WINDOW, ATT_KV_W), lambda i: (i, 0, 0))
    state = pl.BlockSpec((bb, RET_QK_W, RET_V_DIM), lambda i: (i, 0, 0))
    return pl.pallas_call(
        _sample_mixer_kernel,
        grid=(nb // bb,),
        in_specs=[
            pl.BlockSpec(memory_space=pltpu.SMEM),
            pl.BlockSpec((bb, ls, d), lambda i: (i, 0, 0)), const((1, d)), const((d, D_IN)), const((MIX_OUT, d)),
            const((1, RET_V_W)), const((1, RET_V_W)),
            win, win, state,
            const(_S_BIAS.shape), const(_S_DEC.shape), const(_S_XI.shape), const(_S_ZETA.shape),
        ],
        out_specs=[row, win, win, state],
        out_shape=[
            jax.ShapeDtypeStruct((n, d), F32),
            jax.ShapeDtypeStruct((nb, WINDOW, ATT_KV_W), F32),
            jax.ShapeDtypeStruct((nb, WINDOW, ATT_KV_W), F32),
            jax.ShapeDtypeStruct((nb, RET_QK_W, RET_V_DIM), F32),
        ],
        compiler_params=pltpu.CompilerParams(
            dimension_semantics=("arbitrary",), vmem_limit_bytes=VMEM_LIMIT),
        name="sample_mixer",
    )(sinks, x3d, g_mix, w_in, w_out, gn_g, gn_b, ck, cv, st,
      jnp.asarray(_S_BIAS), jnp.asarray(_S_DEC), jnp.asarray(_S_XI), jnp.asarray(_S_ZETA))


def _head_slab(x_ref, b, hd):
    group = X_D_HALVES * N_X_HEADS
    halves = [x_ref[b, pl.ds(dh * N_X_HEADS + hd, N_MEM, stride=group), :] for dh in range(X_D_HALVES)]
    return jnp.concatenate(halves, axis=1).astype(BF16)


def _mlp_value(h, g_ref, wup_ref, wdn_ref, gf_ref, fillers=None):
    xn = _rms(h, g_ref[...]).astype(BF16)
    piece = FF_CHUNK // N_X_HEADS
    opiece = D_MODEL // N_X_HEADS
    n_chunks = D_FF // FF_CHUNK
    nofill = (None, None, None)

    def up(c):
        qk, softmax, _ = fillers[c] if fillers is not None else nofill
        hid = []
        for k in range(N_X_HEADS):
            cols = slice(c * FF_CHUNK + k * piece, c * FF_CHUNK + (k + 1) * piece)
            u = jnp.maximum(_dot(xn, wup_ref[:, cols]), 0.0)
            hid.append((u * u).astype(BF16))
            if qk is not None:
                qk(k)
        if softmax is not None:
            softmax()
        return jnp.concatenate(hid, axis=1)

    def down(c, hid):
        pv = (fillers[c] if fillers is not None else nofill)[2]
        rows_c = slice(c * FF_CHUNK, (c + 1) * FF_CHUNK)
        out = []
        for k in range(N_X_HEADS):
            out.append(_dot(hid, wdn_ref[rows_c, k * opiece:(k + 1) * opiece]))
            if pv is not None:
                pv(k)
        return jnp.concatenate(out, axis=1)

    acc = h
    hid = up(0)
    for c in range(n_chunks):
        nxt = up(c + 1) if c + 1 < n_chunks else None
        acc = acc + down(c, hid)
        hid = nxt
    return _rms(acc, gf_ref[...])


def _mlp_xattn_kernel(hp_ref, hsm_ref, gx_ref, wq_ref, wo_ref, xk_ref, xv_ref, g_ref, wup_ref, wdn_ref, gf_ref,
                      yp_ref, ys_ref):
    i = pl.program_id(0)
    n = pl.num_programs(0) - 1
    bb = xk_ref.shape[0]
    assert bb == D_FF // FF_CHUNK and bb % 2 == 0

    @pl.when(i == 0)
    def _():
        xn = _rms(hsm_ref[...], gx_ref[...]).astype(BF16)
        ys_ref[...] = (_dot(xn, wq_ref[...]) * (X_HEAD_DIM ** -0.5)).reshape(ys_ref.shape)

    @pl.when(i < n)
    def _():
        own0 = lax.broadcasted_iota(jnp.int32, (SUBLANES, 1), 0) < DEC_SEQ
        o_rows = {}

        def attend(b):
            t = b // 2
            tile_b = pl.ds(i * bb + 2 * t, 2)
            env = dict(s=[], o=[])

            def qk(hd):
                if hd == 0:
                    env["q"] = ys_ref[tile_b].reshape(SUBLANES, D_MODEL).astype(BF16)
                env["s"].append(_dot_nt(env["q"][:, hd * X_HEAD_DIM:(hd + 1) * X_HEAD_DIM],
                                        _head_slab(xk_ref, b, hd)))

            def softmax():
                s = jnp.concatenate(env["s"], axis=0)
                m = jnp.max(s, axis=-1, keepdims=True)
                p = jnp.exp(s - m)
                env["p"] = p * (1.0 / jnp.sum(p, axis=-1, keepdims=True))

            def pv(hd):
                p = env["p"][hd * SUBLANES:(hd + 1) * SUBLANES].astype(BF16)
                env["o"].append(_dot(p, _head_slab(xv_ref, b, hd)))
                if hd == N_X_HEADS - 1:
                    o_rows[b] = jnp.concatenate(env["o"], axis=1)
                    if b % 2 == 1:
                        ys_ref[tile_b] = jnp.where(own0, o_rows[b - 1], o_rows[b]).reshape(2, DEC_SEQ, D_MODEL)

            return qk, softmax, pv

        fillers = [attend(b) for b in range(bb)]
        yp_ref[...] = _mlp_value(hp_ref[...], g_ref, wup_ref, wdn_ref, gf_ref, fillers)

    @pl.when(i == n)
    def _():
        o = ys_ref[...].reshape(hsm_ref.shape).astype(BF16)
        hs = hsm_ref[...] + _dot(o, wo_ref[...])
        ys_ref[...] = _mlp_value(hs, g_ref, wup_ref, wdn_ref, gf_ref).reshape(ys_ref.shape)


def _mlp_xattn(hp2d, hsm, g_xattn, w_xq, w_xo, xk, xv, g_mlp, w_up, w_down, g_final):
    n, d = hp2d.shape
    ns = hsm.shape[0]
    nb = xk.shape[0]
    bb = BB_X
    tm = n // (nb // bb)
    n_tiles = n // tm
    assert n_tiles * bb == nb and tm % SUBLANES == 0
    clip = lambda i: jnp.minimum(i, n_tiles - 1)
    prompt = pl.BlockSpec((tm, d), lambda i: (clip(i), 0))
    mem = pl.BlockSpec((bb,) + xk.shape[1:], lambda i: (clip(i), 0, 0))
    const = lambda shape: pl.BlockSpec(shape, lambda i: (0,) * len(shape), pipeline_mode=pl.Buffered(1))
    return pl.pallas_call(
        _mlp_xattn_kernel,
        grid=(n_tiles + 1,),
        in_specs=[prompt, const((ns, d)), const((1, d)), const((d, d)), const((d, d)), mem, mem,
                  const((1, d)), const((d, D_FF)), const((D_FF, d)), const((1, d))],
        out_specs=[prompt, pl.BlockSpec((ns // DEC_SEQ, DEC_SEQ, d), lambda i: (0, 0, 0))],
        out_shape=[jax.ShapeDtypeStruct((n, d), F32), jax.ShapeDtypeStruct((ns // DEC_SEQ, DEC_SEQ, d), F32)],
        compiler_params=pltpu.CompilerParams(
            dimension_semantics=("arbitrary",), vmem_limit_bytes=VMEM_LIMIT),
        name="mlp_xattn",
    )(hp2d, hsm, g_xattn, w_xq, w_xo, xk, xv, g_mlp, w_up, w_down, g_final)


def _mem_rows(c):
    nb = c.shape[0]
    c = c.reshape(nb, N_MEM, N_X_HEADS, X_D_HALVES, LANES)
    return jnp.transpose(c, (0, 1, 3, 2, 4)).reshape(nb, N_MEM * X_D_HALVES * N_X_HEADS, LANES)


def kernel(x_prompt, x_sample, mem_prompt, cache_win_k, cache_win_v, state_ret, cache_mem_k, cache_mem_v,
           g_mix, w_in, attn_sinks, ret_gn_g, ret_gn_b, w_out, g_xattn, g_mem, w_xq, w_xk, w_xv, w_xo,
           g_mlp, w_up, w_down, g_final):
    depth = w_in.shape[0]
    assert depth == 1, "single-layer trunk"
    b, s, d = x_prompt.shape
    nb, ls, _ = x_sample.shape
    row = lambda a: a.reshape(1, -1)
    sinks = attn_sinks[0]
    gn_g, gn_b = row(ret_gn_g[0]), row(ret_gn_b[0])
    g_fin = row(g_final)

    mk, mv, mkb, mvb, w_in_b, w_out_b = _memory_kv(
        mem_prompt.reshape(b * N_MEM, d), row(g_mem[0]), w_xk[0], w_xv[0], w_in[0], w_out[0])
    hp, p_wk, p_wv, p_rs, w_up_b, w_dn_b, w_xq_b, w_xo_b = _prompt_mixer(
        x_prompt, row(g_mix[0]), w_in_b, w_out_b, sinks, gn_g, gn_b,
        (w_up[0], w_down[0], w_xq[0], w_xo[0]))
    hp = _prompt_xattn(hp, row(g_xattn[0]), w_xq_b, w_xo_b,
                       mkb.reshape(b, N_MEM, d), mvb.reshape(b, N_MEM, d))

    win_t = lambda c: jnp.transpose(c, (0, 2, 3, 1)).reshape(nb, ATT_KV_W, WINDOW)
    win_t_inv = lambda a: jnp.transpose(a.reshape(-1, N_KV_HEADS, HEAD_DIM, WINDOW),
                                        (0, 3, 1, 2)).reshape(1, -1, WINDOW, N_KV_HEADS, HEAD_DIM)
    hs, s_wk, s_wv, s_rs = _sample_mixer(
        x_sample, row(g_mix[0]), w_in_b, w_out_b, sinks, gn_g, gn_b,
        win_t(cache_win_k[0]), win_t(cache_win_v[0]), state_ret[0].reshape(nb, RET_QK_W, RET_V_DIM))

    y_prompt, y_sample = _mlp_xattn(
        hp.reshape(b * s, d), hs, row(g_xattn[0]), w_xq_b, w_xo_b,
        _mem_rows(cache_mem_k[0]), _mem_rows(cache_mem_v[0]), row(g_mlp[0]), w_up_b, w_dn_b, g_fin)
    y_prompt = y_prompt.reshape(b, s, d)

    ret5 = lambda a, n: a.reshape(1, n, N_RET_HEADS, RET_QK_DIM, RET_V_DIM)
    mem5 = lambda a: jnp.transpose(a.reshape(b, N_MEM, X_D_HALVES, N_X_HEADS, LANES),
                                   (0, 1, 3, 2, 4)).reshape(1, b, N_MEM, N_X_HEADS, X_HEAD_DIM)
    return (y_prompt, y_sample,
            win_t_inv(p_wk), win_t_inv(p_wv), ret5(p_rs, b), mem5(mk), mem5(mv),
            win_t_inv(s_wk), win_t_inv(s_wv), ret5(s_rs, nb))
```

```python
import functools

import jax
import jax.numpy as jnp
import numpy as np
from jax import lax
from jax.experimental import pallas as pl
from jax.experimental.pallas import tpu as pltpu

F32 = jnp.float32
BF16 = jnp.bfloat16

D_MODEL = 1024
BATCH = 8
SEQ = 2048
DEC_BATCH = 128
DEC_SEQ = 4
HEAD_DIM = 64
N_ATT_HEADS = 8
N_KV_HEADS = 2
KV_GROUP = N_ATT_HEADS // N_KV_HEADS
WINDOW = 128
BLK = 128
N_RET_HEADS = 4
RET_QK_DIM = 64
RET_V_DIM = 128
N_MEM = 256
N_X_HEADS = 4
X_HEAD_DIM = D_MODEL // N_X_HEADS
D_FF = 4 * D_MODEL
RMS_EPS = 1e-6
GN_EPS = 1e-5

ATT_Q_W = N_ATT_HEADS * HEAD_DIM
ATT_KV_W = N_KV_HEADS * HEAD_DIM
RET_QK_W = N_RET_HEADS * RET_QK_DIM
RET_V_W = N_RET_HEADS * RET_V_DIM
MIX_OUT = ATT_Q_W + RET_V_W
D_IN = ATT_Q_W + 2 * ATT_KV_W + 2 * RET_QK_W + 2 * RET_V_W
C_QA, C_KV, C_QKR, C_VR, C_GR = 0, 512, 768, 1280, 1792

LANES = 128
SUBLANES = 8
HALF = LANES // 2
X_D_HALVES = X_HEAD_DIM // LANES
NEG = -1e30
VMEM_LIMIT = 56 * 1024 * 1024

TM_MIX = 512
TM_X = 2048
SUB_ROWS = 512
FF_CHUNK = 1024
BB_MIX = 32
BB_X = 4

NEG_SLOPES = [-(2.0 ** (-8.0 * (i + 1) / N_ATT_HEADS)) for i in range(N_ATT_HEADS)]
_LOG_G = np.log(1.0 - 2.0 ** (-5.0 - np.arange(N_RET_HEADS))).astype(np.float32).astype(np.float64)


def _prompt_tables():
    qi = np.arange(BLK)[:, None]
    kj = np.arange(2 * BLK)[None, :]
    dist = (qi + BLK - kj).astype(np.float64)
    mask = np.where((dist >= 0) & (dist < WINDOW), 0.0, NEG)
    l = np.arange(BLK, dtype=np.float64)
    diff = l[:, None] - l[None, :]
    decay = np.where(diff >= 0, np.exp(_LOG_G[:, None, None] * np.maximum(diff, 0.0)), 0.0)
    xi = np.exp((l[:, None] + 1.0) * _LOG_G[None, :])
    zeta = np.exp((BLK - 1.0 - l)[:, None] * _LOG_G[None, :])
    xi_t = np.repeat(xi, RET_V_DIM, axis=1)
    zeta_t = np.repeat(zeta, RET_QK_DIM, axis=1)
    f = lambda a: np.asarray(a, np.float32)
    return f(dist), f(mask), f(decay), f(xi_t), f(zeta_t)


def _sample_tables():
    slopes = -np.asarray(NEG_SLOPES)
    bias = np.full((2, N_ATT_HEADS * SUBLANES, 2 * BLK), NEG, np.float64)
    dec = np.zeros((2, N_RET_HEADS * SUBLANES, BLK), np.float64)
    xi = np.zeros((2, N_RET_HEADS * SUBLANES, RET_V_DIM), np.float64)
    zeta = np.zeros((2, SUBLANES, RET_QK_W), np.float64)
    for par in range(2):
        for r in range(SUBLANES):
            own = DEC_SEQ * par <= r < DEC_SEQ * (par + 1)
            t = r - DEC_SEQ * par if own else r % DEC_SEQ
            for h in range(N_ATT_HEADS):
                row = h * SUBLANES + r
                for j in range(WINDOW):
                    d = t + WINDOW - j
                    if 0 <= d < WINDOW:
                        bias[par, row, j] = -slopes[h] * d
                for c in range(DEC_SEQ):
                    d = t - c
                    if d >= 0:
                        bias[par, row, WINDOW + DEC_SEQ * par + c] = -slopes[h] * d
            for h in range(N_RET_HEADS):
                row = h * SUBLANES + r
                if own:
                    xi[par, row, :] = np.exp((t + 1.0) * _LOG_G[h])
                    zeta[par, r, h * RET_QK_DIM:(h + 1) * RET_QK_DIM] = np.exp((DEC_SEQ - 1.0 - t) * _LOG_G[h])
                    for c in range(t + 1):
                        dec[par, row, DEC_SEQ * par + c] = np.exp(_LOG_G[h] * (t - c))
    f = lambda a: np.asarray(a, np.float32)
    return f(bias), f(dec), f(xi), f(zeta)


_P_DIST, _P_MASK, _P_DECAY, _P_XI, _P_ZETA = _prompt_tables()
_S_BIAS, _S_DEC, _S_XI, _S_ZETA = _sample_tables()
_GL_PROMPT = [float(np.exp(_LOG_G[h] * BLK)) for h in range(N_RET_HEADS)]
_GL_SAMPLE = [float(np.exp(_LOG_G[h] * DEC_SEQ)) for h in range(N_RET_HEADS)]


def _rms(x, g):
    return x * lax.rsqrt(jnp.mean(x * x, axis=-1, keepdims=True) + RMS_EPS) * g


def _dot(a, b):
    return jnp.dot(a, b, preferred_element_type=F32)


def _dot_nt(a, b):
    return lax.dot_general(a, b, (((1,), (1,)), ((), ())), preferred_element_type=F32)


def _dot_tn(a, b):
    return lax.dot_general(a, b, (((0,), (0,)), ((), ())), preferred_element_type=F32)


def _silu(g):
    return g * (1.0 / (1.0 + jnp.exp(-g)))


def _half_masks(width):
    lane = lax.broadcasted_iota(jnp.int32, (1, width), 1)
    lo = ((lane & (LANES - 1)) < HALF).astype(F32)
    return lo, 1.0 - lo


def _sink_softmax(s, sink):
    m = jnp.maximum(jnp.max(s, axis=-1, keepdims=True), sink)
    p = jnp.exp(s - m)
    den = jnp.sum(p, axis=-1, keepdims=True) + jnp.exp(sink - m)
    return p * (1.0 / den)


def _group_norm(o, g, b):
    mu = jnp.mean(o, axis=-1, keepdims=True)
    d = o - mu
    var = jnp.mean(d * d, axis=-1, keepdims=True)
    return d * lax.rsqrt(var + GN_EPS) * g + b


def _pm_project_stages(x, slot, gmix_ref, win_ref, sc, kv_out=None):
    tm = x.shape[0]
    xn = _rms(x, gmix_ref[...]).astype(BF16)

    pw = 2 * LANES
    lo, hi = _half_masks(pw)

    def stage_q(i):
        cols = slice(i * pw, (i + 1) * pw)
        q = _dot(xn, win_ref[:, C_QA + i * pw:C_QA + (i + 1) * pw])
        sc["qlo"][slot, :, cols] = (q * (lo * HEAD_DIM ** -0.5)).astype(BF16)
        sc["qhi"][slot, :, cols] = (q * (hi * HEAD_DIM ** -0.5)).astype(BF16)

    def stage_kv():
        z = _dot(xn, win_ref[:, C_KV:C_KV + pw])
        low = lax.broadcasted_iota(jnp.int32, (tm, LANES), 1) < HALF
        k = z[:, 0:ATT_KV_W]
        v = z[:, ATT_KV_W:2 * ATT_KV_W]
        if kv_out is not None:
            kv_out[0][0] = k[tm - WINDOW:, :].T
            kv_out[1][0] = v[tm - WINDOW:, :].T
        k_r = pltpu.roll(k, HALF, axis=1)
        v_r = pltpu.roll(v, HALF, axis=1)
        sc["kd0"][slot] = jnp.where(low, k, k_r).astype(BF16)
        sc["kd1"][slot] = jnp.where(low, k_r, k).astype(BF16)
        sc["vd0"][slot, :, 0:LANES] = jnp.where(low, v, 1.0).astype(BF16)
        sc["vd0"][slot, :, LANES:2 * LANES] = jnp.where(low, 1.0, v_r).astype(BF16)
        sc["vd1"][slot, :, 0:LANES] = jnp.where(low, v_r, 1.0).astype(BF16)
        sc["vd1"][slot, :, LANES:2 * LANES] = jnp.where(low, 1.0, v).astype(BF16)

    def stage_qr():
        qr = _dot(xn, win_ref[:, C_QKR:C_QKR + pw])
        sc["qrlo"][slot] = (qr * lo).astype(BF16)
        sc["qrhi"][slot] = (qr * hi).astype(BF16)

    def stage_kr():
        sc["kr"][slot] = _dot(xn, win_ref[:, C_QKR + pw:C_QKR + 2 * pw]) * (RET_QK_DIM ** -0.5)

    def stage_vr(i):
        cols = slice(i * pw, (i + 1) * pw)
        sc["vr"][slot, :, cols] = _dot(xn, win_ref[:, C_VR + i * pw:C_VR + (i + 1) * pw]).astype(BF16)

    def stage_gate(i):
        cols = slice(i * pw, (i + 1) * pw)
        sc["gate"][slot, :, cols] = _silu(_dot(xn, win_ref[:, C_GR + i * pw:C_GR + (i + 1) * pw]))

    part = functools.partial
    return [part(stage_q, 0), part(stage_q, 1), stage_kv, stage_qr, stage_kr,
            part(stage_vr, 0), part(stage_vr, 1), part(stage_gate, 0), part(stage_gate, 1)]


def _pm_last_block(slot, tm, sc):
    rows = slice(tm - BLK, tm)
    return ([sc["kd0"][slot, rows, :], sc["kd1"][slot, rows, :]],
            [sc["vd0"][slot, rows, :], sc["vd1"][slot, rows, :]])


def _pm_blocks(slot, prev_kd, prev_vd, is_first, state, fillers, tm, sinks_ref, gng_ref, gnb_ref,
               decay_ref, xi_ref, zeta_ref, sc):
    nblk = tm // BLK
    n_units = nblk * (N_KV_HEADS + N_RET_HEADS // 2)
    pending = list(fillers)
    done_units = [0]

    def unit_done():
        done_units[0] += 1
        while pending and (len(fillers) - len(pending)) * n_units < done_units[0] * len(fillers):
            pending.pop(0)()
    lowb = lax.broadcasted_iota(jnp.int32, (BLK, LANES), 1) < HALF
    col = lax.broadcasted_iota(jnp.int32, (BLK, 2 * BLK), 1)
    first_mask = None if is_first is False else jnp.where((col < BLK) & is_first, NEG, 0.0)
    kd_refs = (sc["kd0"], sc["kd1"])
    vd_refs = (sc["vd0"], sc["vd1"])
    qlo, qhi, mix = sc["qlo"], sc["qhi"], sc["mix"]
    n_pairs = N_RET_HEADS // 2

    for j in range(nblk):
        rows = slice(j * BLK, (j + 1) * BLK)
        c0s = [kvh * KV_GROUP * HEAD_DIM for kvh in range(N_KV_HEADS)]
        lsls = [slice(i * LANES, (i + 1) * LANES) for i in range(n_pairs)]
        vds, scores = [], []
        for kvh in range(N_KV_HEADS):
            if j == 0:
                kd = jnp.concatenate([prev_kd[kvh], kd_refs[kvh][slot, rows, :]], axis=0)
                vds.append(jnp.concatenate([prev_vd[kvh], vd_refs[kvh][slot, rows, :]], axis=0))
            else:
                krows = slice((j - 1) * BLK, (j + 1) * BLK)
                kd = kd_refs[kvh][slot, krows, :]
                vds.append(vd_refs[kvh][slot, krows, :])
            c0 = c0s[kvh]
            qst = jnp.concatenate([qlo[slot, rows, c0:c0 + LANES], qhi[slot, rows, c0:c0 + LANES],
                                   qlo[slot, rows, c0 + LANES:c0 + 2 * LANES],
                                   qhi[slot, rows, c0 + LANES:c0 + 2 * LANES]], axis=0)
            scores.append(_dot_nt(qst, kd))
        unit_done()

        kps = [sc["kr"][slot, rows, lsls[i]] for i in range(n_pairs)]
        vpairs = [sc["vr"][slot, rows, 2 * i * RET_V_DIM:(2 * i + 2) * RET_V_DIM] for i in range(n_pairs)]
        q2s = [jnp.concatenate([sc["qrlo"][slot, rows, lsls[i]], sc["qrhi"][slot, rows, lsls[i]]], axis=0)
               for i in range(n_pairs)]
        a_s = [_dot_nt(q2s[i], kps[i].astype(BF16)) for i in range(n_pairs)]
        ocs = [_dot(q2s[i], state[i].astype(BF16)) for i in range(n_pairs)]
        us = [_dot_tn((kps[i] * zeta_ref[:, lsls[i]]).astype(BF16), vpairs[i]) for i in range(n_pairs)]
        unit_done()

        for kvh in range(N_KV_HEADS):
            s, vd, c0 = scores[kvh], vds[kvh], c0s[kvh]
            es, esink = [], []
            for g in range(KV_GROUP):
                h = kvh * KV_GROUP + g
                sg = s[g * BLK:(g + 1) * BLK] + sc["bias"][h]
                if j == 0 and first_mask is not None:
                    sg = sg + first_mask
                sink = sinks_ref[h]
                m = jnp.maximum(jnp.max(sg, axis=-1, keepdims=True), sink)
                es.append(jnp.exp(sg - m).astype(BF16))
                esink.append(jnp.exp(sink - m))
            o = _dot(jnp.concatenate(es, axis=0), vd)
            for pair in range(KV_GROUP // 2):
                oe = o[2 * pair * BLK:(2 * pair + 1) * BLK]
                oo = o[(2 * pair + 1) * BLK:(2 * pair + 2) * BLK]
                num = jnp.where(lowb, oe[:, :LANES], oo[:, LANES:])
                den = (jnp.where(lowb, oe[:, LANES:], oo[:, :LANES])
                       + jnp.where(lowb, esink[2 * pair], esink[2 * pair + 1]))
                cs = c0 + pair * LANES
                mix[slot, rows, cs:cs + LANES] = (num * (1.0 / den)).astype(BF16)
            unit_done()

        for i in range(n_pairs):
            a, oc, u, sp = a_s[i], ocs[i], us[i], state[i]
            inner = jnp.concatenate([a[:BLK] * decay_ref[2 * i], a[BLK:] * decay_ref[2 * i + 1]], axis=0)
            oi = _dot(inner.astype(BF16), vpairs[i])
            for half in range(2):
                h = 2 * i + half
                vsl = slice(h * RET_V_DIM, (h + 1) * RET_V_DIM)
                hr = slice(half * BLK, (half + 1) * BLK)
                o = oi[hr, half * RET_V_DIM:(half + 1) * RET_V_DIM] + oc[hr] * xi_ref[:, vsl]
                r = _group_norm(o, gng_ref[:, vsl], gnb_ref[:, vsl]) * sc["gate"][slot, rows, vsl]
                mix[slot, rows, ATT_Q_W + h * RET_V_DIM:ATT_Q_W + (h + 1) * RET_V_DIM] = r.astype(BF16)
            state[i] = jnp.concatenate(
                [_GL_PROMPT[2 * i] * sp[:RET_QK_DIM] + u[:RET_QK_DIM, :RET_V_DIM],
                 _GL_PROMPT[2 * i + 1] * sp[RET_QK_DIM:] + u[RET_QK_DIM:, RET_V_DIM:]], axis=0)

    assert not pending
    return state


def _pm_wout_pieces(slot, x_ref, rows, wout_ref, h_ref, sc):
    pw = 2 * LANES
    n = D_MODEL // pw
    parts = []

    def piece(k):
        parts.append(_dot(sc["mix"][slot], wout_ref[:, k * pw:(k + 1) * pw]))
        if k == n - 1:
            h_ref[rows, :] = x_ref[...] + jnp.concatenate(parts, axis=1)

    return [functools.partial(piece, k) for k in range(n)]


def _prompt_mixer_kernel(sinks_ref, xfirst_ref, xodd_ref, xnext_ref, gmix_ref, win_ref, wout_ref, gng_ref, gnb_ref,
                           dist_ref, mask_ref, decay_ref, xi_ref, zeta_ref,
                           wupf_ref, wdnf_ref, wqf_ref, wof_ref,
                           h_ref, wk_ref, wv_ref, st_ref,
                           wupb_ref, wdnb_ref, wqb_ref, wob_ref,
                           qlo_s, qhi_s, kd0_s, kd1_s, vd0_s, vd1_s,
                           qrlo_s, qrhi_s, kr_s, vr_s, gate_s, mix_s, bias_s, state_s, xkeep_s):
    u = pl.program_id(0)
    tm = xnext_ref.shape[0]
    wupb_ref[...] = wupf_ref[...].astype(BF16)
    wdnb_ref[...] = wdnf_ref[...].astype(BF16)
    wqb_ref[...] = wqf_ref[...].astype(BF16)
    wob_ref[...] = wof_ref[...].astype(BF16)
    sc = dict(qlo=qlo_s, qhi=qhi_s, kd0=kd0_s, kd1=kd1_s, vd0=vd0_s, vd1=vd1_s, qrlo=qrlo_s, qrhi=qrhi_s,
              kr=kr_s, vr=vr_s, gate=gate_s, mix=mix_s, bias=bias_s)
    n_pairs = N_RET_HEADS // 2
    blocks = functools.partial(_pm_blocks, tm=tm, sinks_ref=sinks_ref, gng_ref=gng_ref,
                               gnb_ref=gnb_ref, decay_ref=decay_ref, xi_ref=xi_ref, zeta_ref=zeta_ref, sc=sc)

    @pl.when(u == 0)
    def _():
        for h in range(N_ATT_HEADS):
            bias_s[h] = NEG_SLOPES[h] * dist_ref[...] + mask_ref[...]
        state_s[...] = jnp.zeros_like(state_s)
        kd0_s[1] = jnp.zeros(kd0_s.shape[1:], BF16)
        kd1_s[1] = jnp.zeros(kd1_s.shape[1:], BF16)
        vd0_s[1] = jnp.zeros(vd0_s.shape[1:], BF16)
        vd1_s[1] = jnp.zeros(vd1_s.shape[1:], BF16)
        xkeep_s[...] = xfirst_ref[...]
        for stage in _pm_project_stages(xfirst_ref[...], 0, gmix_ref, win_ref, sc):
            stage()

    seq_start = (u % 2) == 0
    state = [jnp.where(seq_start, 0.0, state_s[i * LANES:(i + 1) * LANES, :]) for i in range(n_pairs)]

    prev_kd, prev_vd = _pm_last_block(1, tm, sc)
    stages = _pm_project_stages(xodd_ref[...], 1, gmix_ref, win_ref, sc, kv_out=(wk_ref, wv_ref))
    state = blocks(0, prev_kd, prev_vd, seq_start, state, stages)
    wout0 = _pm_wout_pieces(0, xkeep_s, slice(0, tm), wout_ref, h_ref, sc)

    prev_kd, prev_vd = _pm_last_block(0, tm, sc)
    stages = _pm_project_stages(xnext_ref[...], 0, gmix_ref, win_ref, sc)
    state = blocks(1, prev_kd, prev_vd, False, state, wout0 + stages)
    for piece in _pm_wout_pieces(1, xodd_ref, slice(tm, 2 * tm), wout_ref, h_ref, sc):
        piece()
    xkeep_s[...] = xnext_ref[...]

    for i in range(n_pairs):
        state_s[i * LANES:(i + 1) * LANES, :] = state[i]
        st_ref[0, i * LANES:(i + 1) * LANES, :] = state[i]


def _prompt_mixer(x, g_mix, w_in, w_out, sinks, gn_g, gn_b, side_f32):
    b, s, d = x.shape
    tm = TM_MIX
    n_tiles = b * s // tm
    steps = n_tiles // 2
    seq_steps = s // (2 * tm)
    assert s % (2 * tm) == 0 and seq_steps == 2, "kernel assumes 4 tiles per sequence"
    x2d = x.reshape(b * s, d)
    const = lambda shape: pl.BlockSpec(shape, lambda i: (0,) * len(shape), pipeline_mode=pl.Buffered(1))
    slot2 = lambda rows, cols, dt: pltpu.VMEM((2, rows, cols), dt)
    side_specs = [pl.BlockSpec((w.shape[0] // steps, w.shape[1]), lambda i: (i, 0)) for w in side_f32]
    outs = pl.pallas_call(
        _prompt_mixer_kernel,
        grid=(steps,),
        in_specs=[
            pl.BlockSpec(memory_space=pltpu.SMEM),
            const((tm, d)),
            pl.BlockSpec((tm, d), lambda i: (2 * i + 1, 0)),
            pl.BlockSpec((tm, d), lambda i: (jnp.minimum(2 * i + 2, n_tiles - 1), 0)),
            const((1, d)), const((d, D_IN)), const((MIX_OUT, d)),
            const((1, RET_V_W)), const((1, RET_V_W)),
            const((BLK, 2 * BLK)), const((BLK, 2 * BLK)),
            const((N_RET_HEADS, BLK, BLK)), const((BLK, RET_V_W)), const((BLK, RET_QK_W)),
        ] + side_specs,
        out_specs=[
            pl.BlockSpec((2 * tm, d), lambda i: (i, 0)),
            pl.BlockSpec((1, WINDOW, ATT_KV_W), lambda i: (i // seq_steps, 0, 0)),
            pl.BlockSpec((1, WINDOW, ATT_KV_W), lambda i: (i // seq_steps, 0, 0)),
            pl.BlockSpec((1, RET_QK_W, RET_V_DIM), lambda i: (i // seq_steps, 0, 0)),
        ] + side_specs,
        out_shape=[
            jax.ShapeDtypeStruct((b * s, d), F32),
            jax.ShapeDtypeStruct((b, WINDOW, ATT_KV_W), F32),
            jax.ShapeDtypeStruct((b, WINDOW, ATT_KV_W), F32),
            jax.ShapeDtypeStruct((b, RET_QK_W, RET_V_DIM), F32),
        ] + [jax.ShapeDtypeStruct(w.shape, BF16) for w in side_f32],
        scratch_shapes=[
            slot2(tm, ATT_Q_W, BF16), slot2(tm, ATT_Q_W, BF16),
            slot2(tm, LANES, BF16), slot2(tm, LANES, BF16),
            slot2(tm, 2 * LANES, BF16), slot2(tm, 2 * LANES, BF16),
            slot2(tm, RET_QK_W, BF16), slot2(tm, RET_QK_W, BF16),
            slot2(tm, RET_QK_W, F32), slot2(tm, RET_V_W, BF16),
            slot2(tm, RET_V_W, F32), slot2(tm, MIX_OUT, BF16),
            pltpu.VMEM((N_ATT_HEADS, BLK, 2 * BLK), F32),
            pltpu.VMEM((RET_QK_W, RET_V_DIM), F32),
            pltpu.VMEM((tm, d), F32),
        ],
        compiler_params=pltpu.CompilerParams(
            dimension_semantics=("arbitrary",), vmem_limit_bytes=VMEM_LIMIT),
        name="prompt_mixer",
    )(sinks, x2d, x2d, x2d, g_mix, w_in, w_out, gn_g, gn_b,
      jnp.asarray(_P_DIST), jnp.asarray(_P_MASK), jnp.asarray(_P_DECAY), jnp.asarray(_P_XI),
      jnp.asarray(_P_ZETA), *side_f32)
    return (outs[0].reshape(b, s, d),) + tuple(outs[1:])


def _memkv_kernel(mem_ref, g_ref, wk_ref, wv_ref, win_ref, wout_ref,
                  mk_ref, mv_ref, mkb_ref, mvb_ref, winb_ref, woutb_ref):
    winb_ref[...] = win_ref[...].astype(BF16)
    woutb_ref[...] = wout_ref[...].astype(BF16)
    mn = _rms(mem_ref[...], g_ref[...]).astype(BF16)
    mk = _dot(mn, wk_ref[...].astype(BF16))
    mv = _dot(mn, wv_ref[...].astype(BF16))
    tm = mem_ref.shape[0]
    group = X_D_HALVES * N_X_HEADS
    for hd in range(N_X_HEADS):
        for dh in range(X_D_HALVES):
            cols = slice(hd * X_HEAD_DIM + dh * LANES, hd * X_HEAD_DIM + (dh + 1) * LANES)
            rows = pl.ds(dh * N_X_HEADS + hd, tm, stride=group)
            mk_ref[rows, :] = mk[:, cols]
            mv_ref[rows, :] = mv[:, cols]
    for s0 in range(0, tm, N_MEM):
        r0 = s0 // N_MEM * D_MODEL
        mkb_ref[r0:r0 + D_MODEL, :] = mk[s0:s0 + N_MEM, :].T.astype(BF16)
    mvb_ref[...] = mv.astype(BF16)


def _memory_kv(mem2d, g_mem, w_xk, w_xv, w_in, w_out):
    n, d = mem2d.shape
    tm = 512
    row = pl.BlockSpec((tm, d), lambda i: (i, 0))
    rows_out = pl.BlockSpec((tm * d // LANES, LANES), lambda i: (i, 0))
    const = lambda shape: pl.BlockSpec(shape, lambda i: (0,) * len(shape), pipeline_mode=pl.Buffered(1))
    steps = n // tm
    win_blk = pl.BlockSpec((w_in.shape[0] // steps, w_in.shape[1]), lambda i: (i, 0))
    wout_blk = pl.BlockSpec((w_out.shape[0] // steps, w_out.shape[1]), lambda i: (i, 0))
    return pl.pallas_call(
        _memkv_kernel,
        grid=(n // tm,),
        in_specs=[row, const((1, d)), const((d, d)), const((d, d)), win_blk, wout_blk],
        out_specs=[rows_out, rows_out, pl.BlockSpec((tm // N_MEM * d, N_MEM), lambda i: (i, 0)), row,
                   win_blk, wout_blk],
        out_shape=[jax.ShapeDtypeStruct((n * d // LANES, LANES), F32),
                   jax.ShapeDtypeStruct((n * d // LANES, LANES), F32),
                   jax.ShapeDtypeStruct((n // N_MEM * d, N_MEM), BF16), jax.ShapeDtypeStruct((n, d), BF16),
                   jax.ShapeDtypeStruct(w_in.shape, BF16), jax.ShapeDtypeStruct(w_out.shape, BF16)],
        compiler_params=pltpu.CompilerParams(
            dimension_semantics=("arbitrary",), vmem_limit_bytes=VMEM_LIMIT),
        name="memory_kv",
    )(mem2d, g_mem, w_xk, w_xv, w_in, w_out)


def _prompt_xattn_kernel(h_ref, g_ref, wq_ref, wo_ref, mk_ref, mv_ref, out_ref, o_s):
    def stages(r0):
        rows = slice(r0, r0 + SUB_ROWS)
        env = {}

        def project():
            env["h"] = h_ref[0, rows, :]
            xn = _rms(env["h"], g_ref[...]).astype(BF16)
            env["q"] = (_dot(xn, wq_ref[...]) * (X_HEAD_DIM ** -0.5)).astype(BF16)

        def scores(hd):
            sl = slice(hd * X_HEAD_DIM, (hd + 1) * X_HEAD_DIM)
            env[hd] = _dot(env["q"][:, sl], mk_ref[0, sl, :])

        def head(hd):
            if hd + 1 < N_X_HEADS:
                scores(hd + 1)
            sl = slice(hd * X_HEAD_DIM, (hd + 1) * X_HEAD_DIM)
            s = env.pop(hd)
            m = jnp.max(s, axis=-1, keepdims=True)
            p = jnp.exp(s - m)
            p = p * (1.0 / jnp.sum(p, axis=-1, keepdims=True))
            o_s[rows, sl] = _dot(p.astype(BF16), mv_ref[0, :, sl]).astype(BF16)

        def output():
            out_ref[0, rows, :] = env["h"] + _dot(o_s[rows, :], wo_ref[...])

        def project_and_first_scores():
            project()
            scores(0)

        return ([project_and_first_scores] + [functools.partial(head, hd) for hd in range(N_X_HEADS)]
                + [output])

    chains = [stages(r0) for r0 in range(0, h_ref.shape[1], SUB_ROWS)]
    n_stage = len(chains[0])
    for step in range(n_stage + len(chains) - 1):
        for lag, chain in enumerate(chains):
            if 0 <= step - lag < n_stage:
                chain[step - lag]()


def _prompt_xattn(h, g, w_xq, w_xo, mkb, mvb):
    b, s, d = h.shape
    tm = TM_X
    const = lambda shape: pl.BlockSpec(shape, lambda i, j: (0,) * len(shape))
    tok = pl.BlockSpec((1, tm, d), lambda i, j: (i, j, 0))
    mem = pl.BlockSpec((1, N_MEM, d), lambda i, j: (i, 0, 0))
    return pl.pallas_call(
        _prompt_xattn_kernel,
        grid=(b, s // tm),
        in_specs=[tok, const((1, d)), const((d, d)), const((d, d)),
                  pl.BlockSpec((1, d, N_MEM), lambda i, j: (i, 0, 0)), mem],
        out_specs=tok,
        out_shape=jax.ShapeDtypeStruct((b, s, d), F32),
        scratch_shapes=[pltpu.VMEM((tm, d), BF16)],
        compiler_params=pltpu.CompilerParams(
            dimension_semantics=("arbitrary", "arbitrary"), vmem_limit_bytes=VMEM_LIMIT),
        name="prompt_xattn",
    )(h, g, w_xq, w_xo, mkb, mvb)


def _sample_mixer_kernel(sinks_ref, x_ref, gmix_ref, win_ref, wout_ref, gng_ref, gnb_ref,
                         ck_ref, cv_ref, st_ref, bias_ref, dec_ref, xi_ref, zeta_ref,
                         h_ref, swk_ref, swv_ref, sst_ref):
    bb = ck_ref.shape[0]
    nt = bb // 2
    x = x_ref[...].reshape(bb * DEC_SEQ, D_MODEL)
    xn = _rms(x, gmix_ref[...]).astype(BF16)
    tile3 = lambda a: a.reshape(nt, SUBLANES, a.shape[-1])

    q = _dot(xn, win_ref[:, C_QA:C_QA + ATT_Q_W]) * (HEAD_DIM ** -0.5)
    kv = _dot(xn, win_ref[:, C_KV:C_KV + 2 * ATT_KV_W])
    qkr = _dot(xn, win_ref[:, C_QKR:C_QKR + 2 * RET_QK_W])
    vr = _dot(xn, win_ref[:, C_VR:C_VR + RET_V_W])
    gate3 = tile3(_silu(_dot(xn, win_ref[:, C_GR:C_GR + RET_V_W])))

    lo512, hi512 = _half_masks(ATT_Q_W)
    q_r = pltpu.roll(q, HALF, axis=1)
    q_nat3 = tile3(q)
    q_rot3 = tile3(q_r)
    lo3 = lo512.reshape(1, 1, ATT_Q_W)
    hi3 = hi512.reshape(1, 1, ATT_Q_W)
    qa3 = (q_nat3 * lo3).astype(BF16)
    qb3 = (q_rot3 * lo3).astype(BF16)
    qc3 = (q_rot3 * hi3).astype(BF16)
    qd3 = (q_nat3 * hi3).astype(BF16)
    t128 = lambda a, i: a[:, :, i * LANES:(i + 1) * LANES]
    qs = jnp.concatenate([t128(qa3, 0), t128(qb3, 1), t128(qa3, 1), t128(qb3, 2),
                          t128(qc3, 2), t128(qd3, 2), t128(qc3, 3), t128(qd3, 3)], axis=1)

    k3 = tile3(kv[:, :ATT_KV_W])
    v3 = tile3(kv[:, ATT_KV_W:])
    pad_kv = jnp.zeros((nt, BLK - SUBLANES, LANES), BF16)
    knew_pad = jnp.concatenate([k3.astype(BF16), pad_kv], axis=1)
    vnew_pad = jnp.concatenate([v3.astype(BF16), pad_kv], axis=1)
    to_lanes = lambda a3: jnp.swapaxes(
        jnp.concatenate([a3, jnp.zeros((nt, BLK - SUBLANES, LANES), F32)], axis=1), 1, 2)
    k3t, v3t = to_lanes(k3), to_lanes(v3)
    roll3 = lambda a, sh: pltpu.roll(a.reshape(nt * BLK, LANES), sh, axis=1).reshape(nt, BLK, LANES)

    lo256, _ = _half_masks(RET_QK_W)
    qr3 = tile3(qkr[:, :RET_QK_W])
    kr3 = tile3(qkr[:, RET_QK_W:] * (RET_QK_DIM ** -0.5))
    vr3 = tile3(vr)
    lane256 = lax.broadcasted_iota(jnp.int32, (1, 1, RET_QK_W), 2)
    qrs = jnp.concatenate(
        [(qr3 * ((lane256 >= h * RET_QK_DIM) & (lane256 < (h + 1) * RET_QK_DIM)).astype(F32)).astype(BF16)
         for h in range(N_RET_HEADS)],
        axis=1)
    kr_pad = jnp.concatenate([kr3.astype(BF16), jnp.zeros((nt, BLK - SUBLANES, RET_QK_W), BF16)], axis=1)
    vr_pad = jnp.concatenate([vr3.astype(BF16), jnp.zeros((nt, BLK - SUBLANES, RET_V_W), BF16)], axis=1)

    lane = lax.broadcasted_iota(jnp.int32, (1, 1, LANES), 2)
    row8 = lax.broadcasted_iota(jnp.int32, (1, SUBLANES, 1), 1)
    bmm_nt = lambda a, b: jnp.einsum('bqd,bkd->bqk', a, b, preferred_element_type=F32)
    bmm = lambda a, b: jnp.einsum('bqk,bkd->bqd', a, b, preferred_element_type=F32)

    att_par, ret_par = [], []
    for par in range(2):
        bsl = pl.ds(par, nt, stride=2)
        ckt = ck_ref[bsl]
        cvt = cv_ref[bsl]
        keep = lane < WINDOW - DEC_SEQ
        new_shift = WINDOW - DEC_SEQ - DEC_SEQ * par
        swk_ref[bsl] = jnp.where(keep, roll3(ckt, WINDOW - DEC_SEQ), roll3(k3t, new_shift))
        swv_ref[bsl] = jnp.where(keep, roll3(cvt, WINDOW - DEC_SEQ), roll3(v3t, new_shift))

        s = jnp.concatenate([bmm(qs, ckt.astype(BF16)), bmm_nt(qs, knew_pad)], axis=2) + bias_ref[par]
        ps = []
        for h in range(N_ATT_HEADS):
            ps.append(_sink_softmax(s[:, h * SUBLANES:(h + 1) * SUBLANES, :], sinks_ref[h]).astype(BF16))
        p_all = jnp.concatenate(ps, axis=1)
        o = bmm_nt(p_all[:, :, :BLK], cvt.astype(BF16)) + bmm(p_all[:, :, BLK:], vnew_pad)
        o_r = pltpu.roll(o.reshape(nt * N_ATT_HEADS * SUBLANES, LANES), HALF, axis=1).reshape(o.shape)
        hr = lambda a, h: a[:, h * SUBLANES:(h + 1) * SUBLANES, :]
        low = lane < HALF
        att_par.append(jnp.concatenate([
            jnp.where(low, hr(o, 0), hr(o_r, 1)), jnp.where(low, hr(o, 2), hr(o_r, 3)),
            jnp.where(low, hr(o_r, 4), hr(o, 5)), jnp.where(low, hr(o_r, 6), hr(o, 7))], axis=2))

        st = st_ref[bsl]
        oc = bmm(qrs, st.astype(BF16))
        inner = (bmm_nt(qrs, kr_pad) * dec_ref[par]).astype(BF16)
        oi = bmm(inner, vr_pad)
        rs = []
        for h in range(N_RET_HEADS):
            vsl = slice(h * RET_V_DIM, (h + 1) * RET_V_DIM)
            rsl = slice(h * SUBLANES, (h + 1) * SUBLANES)
            o_h = oi[:, rsl, vsl] + oc[:, rsl, :] * xi_ref[par, rsl, :]
            rs.append(_group_norm(o_h, gng_ref[:, vsl], gnb_ref[:, vsl]) * gate3[:, :, vsl])
        ret_par.append(jnp.concatenate(rs, axis=2))

        kz3 = (kr3 * zeta_ref[par]).astype(BF16)
        vr3_b = vr3.astype(BF16)
        for p in range(nt):
            for i in range(N_RET_HEADS // 2):
                u = _dot_tn(kz3[p][:, i * LANES:(i + 1) * LANES],
                            vr3_b[p][:, 2 * i * RET_V_DIM:(2 * i + 2) * RET_V_DIM])
                for half in range(2):
                    h = 2 * i + half
                    dsl = slice(h * RET_QK_DIM, (h + 1) * RET_QK_DIM)
                    sst_ref[2 * p + par, dsl, :] = (
                        _GL_SAMPLE[h] * st[p, dsl, :]
                        + u[half * RET_QK_DIM:(half + 1) * RET_QK_DIM, half * RET_V_DIM:(half + 1) * RET_V_DIM])

    own0 = row8 < DEC_SEQ
    att3 = jnp.where(own0, att_par[0], att_par[1])
    ret3 = jnp.where(own0, ret_par[0], ret_par[1])
    mix = jnp.concatenate([att3, ret3], axis=2).reshape(2 * nt * DEC_SEQ, MIX_OUT).astype(BF16)
    h_ref[...] = x + _dot(mix, wout_ref[...])


def _sample_mixer(x3d, g_mix, w_in, w_out, sinks, gn_g, gn_b, ck, cv, st):
    nb, ls, d = x3d.shape
    n = nb * ls
    bb = BB_MIX
    r = bb * DEC_SEQ
    const = lambda shape: pl.BlockSpec(shape, lambda i: (0,) * len(shape))
    row = pl.BlockSpec((r, d), lambda i: (i, 0))
    win = pl.BlockSpec((bb, WINDOW, ATT_KV_W), lambda i: (i, 0, 0))
    state = pl.BlockSpec((bb, RET_QK_W, RET_V_DIM), lambda i: (i, 0, 0))
    return pl.pallas_call(
        _sample_mixer_kernel,
        grid=(nb // bb,),
        in_specs=[
            pl.BlockSpec(memory_space=pltpu.SMEM),
            pl.BlockSpec((bb, ls, d), lambda i: (i, 0, 0)), const((1, d)), const((d, D_IN)), const((MIX_OUT, d)),
            const((1, RET_V_W)), const((1, RET_V_W)),
            win, win, state,
            const(_S_BIAS.shape), const(_S_DEC.shape), const(_S_XI.shape), const(_S_ZETA.shape),
        ],
        out_specs=[row, win, win, state],
        out_shape=[
            jax.ShapeDtypeStruct((n, d), F32),
            jax.ShapeDtypeStruct((nb, WINDOW, ATT_KV_W), F32),
            jax.ShapeDtypeStruct((nb, WINDOW, ATT_KV_W), F32),
            jax.ShapeDtypeStruct((nb, RET_QK_W, RET_V_DIM), F32),
        ],
        compiler_params=pltpu.CompilerParams(
            dimension_semantics=("arbitrary",), vmem_limit_bytes=VMEM_LIMIT),
        name="sample_mixer",
    )(sinks, x3d, g_mix, w_in, w_out, gn_g, gn_b, ck, cv, st,
      jnp.asarray(_S_BIAS), jnp.asarray(_S_DEC), jnp.asarray(_S_XI), jnp.asarray(_S_ZETA))


def _head_slab(x_ref, b, hd):
    group = X_D_HALVES * N_X_HEADS
    halves = [x_ref[b, pl.ds(dh * N_X_HEADS + hd, N_MEM, stride=group), :] for dh in range(X_D_HALVES)]
    return jnp.concatenate(halves, axis=1).astype(BF16)


def _mlp_value(h, g_ref, wup_ref, wdn_ref, gf_ref, fillers=None):
    xn = _rms(h, g_ref[...]).astype(BF16)
    piece = FF_CHUNK // N_X_HEADS
    opiece = D_MODEL // N_X_HEADS
    n_chunks = D_FF // FF_CHUNK
    nofill = (None, None, None)

    def up(c):
        qk, softmax, _ = fillers[c] if fillers is not None else nofill
        hid = []
        for k in range(N_X_HEADS):
            cols = slice(c * FF_CHUNK + k * piece, c * FF_CHUNK + (k + 1) * piece)
            u = jnp.maximum(_dot(xn, wup_ref[:, cols]), 0.0)
            hid.append((u * u).astype(BF16))
            if qk is not None:
                qk(k)
        if softmax is not None:
            softmax()
        return jnp.concatenate(hid, axis=1)

    def down(c, hid):
        pv = (fillers[c] if fillers is not None else nofill)[2]
        rows_c = slice(c * FF_CHUNK, (c + 1) * FF_CHUNK)
        out = []
        for k in range(N_X_HEADS):
            out.append(_dot(hid, wdn_ref[rows_c, k * opiece:(k + 1) * opiece]))
            if pv is not None:
                pv(k)
        return jnp.concatenate(out, axis=1)

    acc = h
    hid = up(0)
    for c in range(n_chunks):
        nxt = up(c + 1) if c + 1 < n_chunks else None
        acc = acc + down(c, hid)
        hid = nxt
    return _rms(acc, gf_ref[...])


def _mlp_xattn_kernel(hp_ref, hsm_ref, gx_ref, wq_ref, wo_ref, xk_ref, xv_ref, g_ref, wup_ref, wdn_ref, gf_ref,
                      yp_ref, ys_ref):
    i = pl.program_id(0)
    n = pl.num_programs(0) - 1
    bb = xk_ref.shape[0]
    assert bb == D_FF // FF_CHUNK and bb % 2 == 0

    @pl.when(i == 0)
    def _():
        xn = _rms(hsm_ref[...], gx_ref[...]).astype(BF16)
        ys_ref[...] = (_dot(xn, wq_ref[...]) * (X_HEAD_DIM ** -0.5)).reshape(ys_ref.shape)

    @pl.when(i < n)
    def _():
        own0 = lax.broadcasted_iota(jnp.int32, (SUBLANES, 1), 0) < DEC_SEQ
        o_rows = {}

        def attend(b):
            t = b // 2
            tile_b = pl.ds(i * bb + 2 * t, 2)
            env = dict(s=[], o=[])

            def qk(hd):
                if hd == 0:
                    env["q"] = ys_ref[tile_b].reshape(SUBLANES, D_MODEL).astype(BF16)
                env["s"].append(_dot_nt(env["q"][:, hd * X_HEAD_DIM:(hd + 1) * X_HEAD_DIM],
                                        _head_slab(xk_ref, b, hd)))

            def softmax():
                s = jnp.concatenate(env["s"], axis=0)
                m = jnp.max(s, axis=-1, keepdims=True)
                p = jnp.exp(s - m)
                env["p"] = p * (1.0 / jnp.sum(p, axis=-1, keepdims=True))

            def pv(hd):
                p = env["p"][hd * SUBLANES:(hd + 1) * SUBLANES].astype(BF16)
                env["o"].append(_dot(p, _head_slab(xv_ref, b, hd)))
                if hd == N_X_HEADS - 1:
                    o_rows[b] = jnp.concatenate(env["o"], axis=1)
                    if b % 2 == 1:
                        ys_ref[tile_b] = jnp.where(own0, o_rows[b - 1], o_rows[b]).reshape(2, DEC_SEQ, D_MODEL)

            return qk, softmax, pv

        fillers = [attend(b) for b in range(bb)]
        yp_ref[...] = _mlp_value(hp_ref[...], g_ref, wup_ref, wdn_ref, gf_ref, fillers)

    @pl.when(i == n)
    def _():
        o = ys_ref[...].reshape(hsm_ref.shape).astype(BF16)
        hs = hsm_ref[...] + _dot(o, wo_ref[...])
        ys_ref[...] = _mlp_value(hs, g_ref, wup_ref, wdn_ref, gf_ref).reshape(ys_ref.shape)


def _mlp_xattn(hp2d, hsm, g_xattn, w_xq, w_xo, xk, xv, g_mlp, w_up, w_down, g_final):
    n, d = hp2d.shape
    ns = hsm.shape[0]
    nb = xk.shape[0]
    bb = BB_X
    tm = n // (nb // bb)
    n_tiles = n // tm
    assert n_tiles * bb == nb and tm % SUBLANES == 0
    clip = lambda i: jnp.minimum(i, n_tiles - 1)
    prompt = pl.BlockSpec((tm, d), lambda i: (clip(i), 0))
    mem = pl.BlockSpec((bb,) + xk.shape[1:], lambda i: (clip(i), 0, 0))
    const = lambda shape: pl.BlockSpec(shape, lambda i: (0,) * len(shape), pipeline_mode=pl.Buffered(1))
    return pl.pallas_call(
        _mlp_xattn_kernel,
        grid=(n_tiles + 1,),
        in_specs=[prompt, const((ns, d)), const((1, d)), const((d, d)), const((d, d)), mem, mem,
                  const((1, d)), const((d, D_FF)), const((D_FF, d)), const((1, d))],
        out_specs=[prompt, pl.BlockSpec((ns // DEC_SEQ, DEC_SEQ, d), lambda i: (0, 0, 0))],
        out_shape=[jax.ShapeDtypeStruct((n, d), F32), jax.ShapeDtypeStruct((ns // DEC_SEQ, DEC_SEQ, d), F32)],
        compiler_params=pltpu.CompilerParams(
            dimension_semantics=("arbitrary",), vmem_limit_bytes=VMEM_LIMIT),
        name="mlp_xattn",
    )(hp2d, hsm, g_xattn, w_xq, w_xo, xk, xv, g_mlp, w_up, w_down, g_final)


def _mem_rows(c):
    nb = c.shape[0]
    c = c.reshape(nb, N_MEM, N_X_HEADS, X_D_HALVES, LANES)
    return jnp.transpose(c, (0, 1, 3, 2, 4)).reshape(nb, N_MEM * X_D_HALVES * N_X_HEADS, LANES)


def kernel(x_prompt, x_sample, mem_prompt, cache_win_k, cache_win_v, state_ret, cache_mem_k, cache_mem_v,
           g_mix, w_in, attn_sinks, ret_gn_g, ret_gn_b, w_out, g_xattn, g_mem, w_xq, w_xk, w_xv, w_xo,
           g_mlp, w_up, w_down, g_final):
    depth = w_in.shape[0]
    assert depth == 1, "single-layer trunk"
    b, s, d = x_prompt.shape
    nb, ls, _ = x_sample.shape
    row = lambda a: a.reshape(1, -1)
    sinks = attn_sinks[0]
    gn_g, gn_b = row(ret_gn_g[0]), row(ret_gn_b[0])
    g_fin = row(g_final)

    mk, mv, mkb, mvb, w_in_b, w_out_b = _memory_kv(
        mem_prompt.reshape(b * N_MEM, d), row(g_mem[0]), w_xk[0], w_xv[0], w_in[0], w_out[0])
    hp, p_wk, p_wv, p_rs, w_up_b, w_dn_b, w_xq_b, w_xo_b = _prompt_mixer(
        x_prompt, row(g_mix[0]), w_in_b, w_out_b, sinks, gn_g, gn_b,
        (w_up[0], w_down[0], w_xq[0], w_xo[0]))
    hp = _prompt_xattn(hp, row(g_xattn[0]), w_xq_b, w_xo_b,
                       mkb.reshape(b, d, N_MEM), mvb.reshape(b, N_MEM, d))

    win_t = lambda c: jnp.transpose(c, (0, 2, 3, 1)).reshape(nb, ATT_KV_W, WINDOW)
    win_t_inv = lambda a: jnp.transpose(a.reshape(-1, N_KV_HEADS, HEAD_DIM, WINDOW),
                                        (0, 3, 1, 2)).reshape(1, -1, WINDOW, N_KV_HEADS, HEAD_DIM)
    hs, s_wk, s_wv, s_rs = _sample_mixer(
        x_sample, row(g_mix[0]), w_in_b, w_out_b, sinks, gn_g, gn_b,
        win_t(cache_win_k[0]), win_t(cache_win_v[0]), state_ret[0].reshape(nb, RET_QK_W, RET_V_DIM))

    y_prompt, y_sample = _mlp_xattn(
        hp.reshape(b * s, d), hs, row(g_xattn[0]), w_xq_b, w_xo_b,
        _mem_rows(cache_mem_k[0]), _mem_rows(cache_mem_v[0]), row(g_mlp[0]), w_up_b, w_dn_b, g_fin)
    y_prompt = y_prompt.reshape(b, s, d)

    ret5 = lambda a, n: a.reshape(1, n, N_RET_HEADS, RET_QK_DIM, RET_V_DIM)
    mem5 = lambda a: jnp.transpose(a.reshape(b, N_MEM, X_D_HALVES, N_X_HEADS, LANES),
                                   (0, 1, 3, 2, 4)).reshape(1, b, N_MEM, N_X_HEADS, X_HEAD_DIM)
    return (y_prompt, y_sample,
            win_t_inv(p_wk), win_t_inv(p_wv), ret5(p_rs, b), mem5(mk), mem5(mv),
            win_t_inv(s_wk), win_t_inv(s_wv), ret5(s_rs, nb))
```

```python
import functools

import jax
import jax.numpy as jnp
import numpy as np
from jax import lax
from jax.experimental import pallas as pl
from jax.experimental.pallas import tpu as pltpu

F32 = jnp.float32
BF16 = jnp.bfloat16

D_MODEL = 1024
BATCH = 8
SEQ = 2048
DEC_BATCH = 128
DEC_SEQ = 4
HEAD_DIM = 64
N_ATT_HEADS = 8
N_KV_HEADS = 2
KV_GROUP = N_ATT_HEADS // N_KV_HEADS
WINDOW = 128
BLK = 128
N_RET_HEADS = 4
RET_QK_DIM = 64
RET_V_DIM = 128
N_MEM = 256
N_X_HEADS = 4
X_HEAD_DIM = D_MODEL // N_X_HEADS
D_FF = 4 * D_MODEL
RMS_EPS = 1e-6
GN_EPS = 1e-5

ATT_Q_W = N_ATT_HEADS * HEAD_DIM
ATT_KV_W = N_KV_HEADS * HEAD_DIM
RET_QK_W = N_RET_HEADS * RET_QK_DIM
RET_V_W = N_RET_HEADS * RET_V_DIM
MIX_OUT = ATT_Q_W + RET_V_W
D_IN = ATT_Q_W + 2 * ATT_KV_W + 2 * RET_QK_W + 2 * RET_V_W
C_QA, C_KV, C_QKR, C_VR, C_GR = 0, 512, 768, 1280, 1792

LANES = 128
SUBLANES = 8
HALF = LANES // 2
X_D_HALVES = X_HEAD_DIM // LANES
NEG = -1e30
VMEM_LIMIT = 56 * 1024 * 1024

TM_MIX = 512
TM_X = 2048
TM_MEM = 256
SUB_ROWS = 512
FF_CHUNK = 1024
BB_MIX = 32
BB_X = 4

NEG_SLOPES = [-(2.0 ** (-8.0 * (i + 1) / N_ATT_HEADS)) for i in range(N_ATT_HEADS)]
_LOG_G = np.log(1.0 - 2.0 ** (-5.0 - np.arange(N_RET_HEADS))).astype(np.float32).astype(np.float64)


def _prompt_tables():
    qi = np.arange(BLK)[:, None]
    kj = np.arange(2 * BLK)[None, :]
    dist = (qi + BLK - kj).astype(np.float64)
    mask = np.where((dist >= 0) & (dist < WINDOW), 0.0, NEG)
    l = np.arange(BLK, dtype=np.float64)
    diff = l[:, None] - l[None, :]
    decay = np.where(diff >= 0, np.exp(_LOG_G[:, None, None] * np.maximum(diff, 0.0)), 0.0)
    xi = np.exp((l[:, None] + 1.0) * _LOG_G[None, :])
    zeta = np.exp((BLK - 1.0 - l)[:, None] * _LOG_G[None, :])
    xi_t = np.repeat(xi, RET_V_DIM, axis=1)
    zeta_t = np.repeat(zeta, RET_QK_DIM, axis=1)
    f = lambda a: np.asarray(a, np.float32)
    return f(dist), f(mask), f(decay), f(xi_t), f(zeta_t)


def _sample_tables():
    slopes = -np.asarray(NEG_SLOPES)
    bias = np.full((2, N_ATT_HEADS * SUBLANES, 2 * BLK), NEG, np.float64)
    dec = np.zeros((2, N_RET_HEADS * SUBLANES, BLK), np.float64)
    xi = np.zeros((2, N_RET_HEADS * SUBLANES, RET_V_DIM), np.float64)
    zeta = np.zeros((2, SUBLANES, RET_QK_W), np.float64)
    for par in range(2):
        for r in range(SUBLANES):
            own = DEC_SEQ * par <= r < DEC_SEQ * (par + 1)
            t = r - DEC_SEQ * par if own else r % DEC_SEQ
            for h in range(N_ATT_HEADS):
                row = h * SUBLANES + r
                for j in range(WINDOW):
                    d = t + WINDOW - j
                    if 0 <= d < WINDOW:
                        bias[par, row, j] = -slopes[h] * d
                for c in range(DEC_SEQ):
                    d = t - c
                    if d >= 0:
                        bias[par, row, WINDOW + DEC_SEQ * par + c] = -slopes[h] * d
            for h in range(N_RET_HEADS):
                row = h * SUBLANES + r
                if own:
                    xi[par, row, :] = np.exp((t + 1.0) * _LOG_G[h])
                    zeta[par, r, h * RET_QK_DIM:(h + 1) * RET_QK_DIM] = np.exp((DEC_SEQ - 1.0 - t) * _LOG_G[h])
                    for c in range(t + 1):
                        dec[par, row, DEC_SEQ * par + c] = np.exp(_LOG_G[h] * (t - c))
    f = lambda a: np.asarray(a, np.float32)
    return f(bias), f(dec), f(xi), f(zeta)


_P_DIST, _P_MASK, _P_DECAY, _P_XI, _P_ZETA = _prompt_tables()
_S_BIAS, _S_DEC, _S_XI, _S_ZETA = _sample_tables()
_GL_PROMPT = [float(np.exp(_LOG_G[h] * BLK)) for h in range(N_RET_HEADS)]
_GL_SAMPLE = [float(np.exp(_LOG_G[h] * DEC_SEQ)) for h in range(N_RET_HEADS)]


def _rms(x, g):
    return x * lax.rsqrt(jnp.mean(x * x, axis=-1, keepdims=True) + RMS_EPS) * g


def _dot(a, b):
    return jnp.dot(a, b, preferred_element_type=F32)


def _dot_nt(a, b):
    return lax.dot_general(a, b, (((1,), (1,)), ((), ())), preferred_element_type=F32)


def _dot_tn(a, b):
    return lax.dot_general(a, b, (((0,), (0,)), ((), ())), preferred_element_type=F32)


def _silu(g):
    return g * (1.0 / (1.0 + jnp.exp(-g)))


def _half_masks(width):
    lane = lax.broadcasted_iota(jnp.int32, (1, width), 1)
    lo = ((lane & (LANES - 1)) < HALF).astype(F32)
    return lo, 1.0 - lo


def _sink_softmax(s, sink):
    m = jnp.maximum(jnp.max(s, axis=-1, keepdims=True), sink)
    p = jnp.exp(s - m)
    den = jnp.sum(p, axis=-1, keepdims=True) + jnp.exp(sink - m)
    return p * (1.0 / den)


def _group_norm(o, g, b):
    mu = jnp.mean(o, axis=-1, keepdims=True)
    d = o - mu
    var = jnp.mean(d * d, axis=-1, keepdims=True)
    return d * lax.rsqrt(var + GN_EPS) * g + b


def _pm_project_stages(x, slot, gmix_ref, win_ref, sc, kv_out=None):
    tm = x.shape[0]
    xn = _rms(x, gmix_ref[...]).astype(BF16)

    pw = 2 * LANES
    lo, hi = _half_masks(pw)

    def stage_q(i):
        cols = slice(i * pw, (i + 1) * pw)
        q = _dot(xn, win_ref[:, C_QA + i * pw:C_QA + (i + 1) * pw])
        sc["qlo"][slot, :, cols] = (q * (lo * HEAD_DIM ** -0.5)).astype(BF16)
        sc["qhi"][slot, :, cols] = (q * (hi * HEAD_DIM ** -0.5)).astype(BF16)

    def stage_kv():
        z = _dot(xn, win_ref[:, C_KV:C_KV + pw])
        low = lax.broadcasted_iota(jnp.int32, (tm, LANES), 1) < HALF
        k = z[:, 0:ATT_KV_W]
        v = z[:, ATT_KV_W:2 * ATT_KV_W]
        if kv_out is not None:
            kv_out[0][0] = k[tm - WINDOW:, :].T
            kv_out[1][0] = v[tm - WINDOW:, :].T
        k_r = pltpu.roll(k, HALF, axis=1)
        v_r = pltpu.roll(v, HALF, axis=1)
        sc["kd0"][slot] = jnp.where(low, k, k_r).astype(BF16)
        sc["kd1"][slot] = jnp.where(low, k_r, k).astype(BF16)
        sc["vd0"][slot, :, 0:LANES] = jnp.where(low, v, 1.0).astype(BF16)
        sc["vd0"][slot, :, LANES:2 * LANES] = jnp.where(low, 1.0, v_r).astype(BF16)
        sc["vd1"][slot, :, 0:LANES] = jnp.where(low, v_r, 1.0).astype(BF16)
        sc["vd1"][slot, :, LANES:2 * LANES] = jnp.where(low, 1.0, v).astype(BF16)

    def stage_qr():
        qr = _dot(xn, win_ref[:, C_QKR:C_QKR + pw])
        sc["qrlo"][slot] = (qr * lo).astype(BF16)
        sc["qrhi"][slot] = (qr * hi).astype(BF16)

    def stage_kr():
        sc["kr"][slot] = _dot(xn, win_ref[:, C_QKR + pw:C_QKR + 2 * pw]) * (RET_QK_DIM ** -0.5)

    def stage_vr(i):
        cols = slice(i * pw, (i + 1) * pw)
        sc["vr"][slot, :, cols] = _dot(xn, win_ref[:, C_VR + i * pw:C_VR + (i + 1) * pw]).astype(BF16)

    def stage_gate(i):
        cols = slice(i * pw, (i + 1) * pw)
        sc["gate"][slot, :, cols] = _silu(_dot(xn, win_ref[:, C_GR + i * pw:C_GR + (i + 1) * pw]))

    part = functools.partial
    return [part(stage_q, 0), part(stage_q, 1), stage_kv, stage_qr, stage_kr,
            part(stage_vr, 0), part(stage_vr, 1), part(stage_gate, 0), part(stage_gate, 1)]


def _pm_last_block(slot, tm, sc):
    rows = slice(tm - BLK, tm)
    return ([sc["kd0"][slot, rows, :], sc["kd1"][slot, rows, :]],
            [sc["vd0"][slot, rows, :], sc["vd1"][slot, rows, :]])


def _pm_blocks(slot, prev_kd, prev_vd, is_first, state, fillers, tm, sinks_ref, gng_ref, gnb_ref,
               decay_ref, xi_ref, zeta_ref, sc):
    nblk = tm // BLK
    n_units = nblk * (N_KV_HEADS + N_RET_HEADS // 2)
    pending = list(fillers)
    done_units = [0]

    def unit_done():
        done_units[0] += 1
        while pending and (len(fillers) - len(pending)) * n_units < done_units[0] * len(fillers):
            pending.pop(0)()
    lowb = lax.broadcasted_iota(jnp.int32, (BLK, LANES), 1) < HALF
    col = lax.broadcasted_iota(jnp.int32, (BLK, 2 * BLK), 1)
    first_mask = None if is_first is False else jnp.where((col < BLK) & is_first, NEG, 0.0)
    kd_refs = (sc["kd0"], sc["kd1"])
    vd_refs = (sc["vd0"], sc["vd1"])
    qlo, qhi, mix = sc["qlo"], sc["qhi"], sc["mix"]
    n_pairs = N_RET_HEADS // 2

    for j in range(nblk):
        rows = slice(j * BLK, (j + 1) * BLK)
        c0s = [kvh * KV_GROUP * HEAD_DIM for kvh in range(N_KV_HEADS)]
        lsls = [slice(i * LANES, (i + 1) * LANES) for i in range(n_pairs)]
        vds, scores = [], []
        for kvh in range(N_KV_HEADS):
            if j == 0:
                kd = jnp.concatenate([prev_kd[kvh], kd_refs[kvh][slot, rows, :]], axis=0)
                vds.append(jnp.concatenate([prev_vd[kvh], vd_refs[kvh][slot, rows, :]], axis=0))
            else:
                krows = slice((j - 1) * BLK, (j + 1) * BLK)
                kd = kd_refs[kvh][slot, krows, :]
                vds.append(vd_refs[kvh][slot, krows, :])
            c0 = c0s[kvh]
            qst = jnp.concatenate([qlo[slot, rows, c0:c0 + LANES], qhi[slot, rows, c0:c0 + LANES],
                                   qlo[slot, rows, c0 + LANES:c0 + 2 * LANES],
                                   qhi[slot, rows, c0 + LANES:c0 + 2 * LANES]], axis=0)
            scores.append(_dot_nt(qst, kd))
        unit_done()

        kps = [sc["kr"][slot, rows, lsls[i]] for i in range(n_pairs)]
        vpairs = [sc["vr"][slot, rows, 2 * i * RET_V_DIM:(2 * i + 2) * RET_V_DIM] for i in range(n_pairs)]
        q2s = [jnp.concatenate([sc["qrlo"][slot, rows, lsls[i]], sc["qrhi"][slot, rows, lsls[i]]], axis=0)
               for i in range(n_pairs)]
        a_s = [_dot_nt(q2s[i], kps[i].astype(BF16)) for i in range(n_pairs)]
        ocs = [_dot(q2s[i], state[i].astype(BF16)) for i in range(n_pairs)]
        us = [_dot_tn((kps[i] * zeta_ref[:, lsls[i]]).astype(BF16), vpairs[i]) for i in range(n_pairs)]
        unit_done()

        for kvh in range(N_KV_HEADS):
            s, vd, c0 = scores[kvh], vds[kvh], c0s[kvh]
            es, esink = [], []
            for g in range(KV_GROUP):
                h = kvh * KV_GROUP + g
                sg = s[g * BLK:(g + 1) * BLK] + sc["bias"][h]
                if j == 0 and first_mask is not None:
                    sg = sg + first_mask
                sink = sinks_ref[h]
                m = jnp.maximum(jnp.max(sg, axis=-1, keepdims=True), sink)
                es.append(jnp.exp(sg - m).astype(BF16))
                esink.append(jnp.exp(sink - m))
            o = _dot(jnp.concatenate(es, axis=0), vd)
            for pair in range(KV_GROUP // 2):
                oe = o[2 * pair * BLK:(2 * pair + 1) * BLK]
                oo = o[(2 * pair + 1) * BLK:(2 * pair + 2) * BLK]
                num = jnp.where(lowb, oe[:, :LANES], oo[:, LANES:])
                den = (jnp.where(lowb, oe[:, LANES:], oo[:, :LANES])
                       + jnp.where(lowb, esink[2 * pair], esink[2 * pair + 1]))
                cs = c0 + pair * LANES
                mix[slot, rows, cs:cs + LANES] = (num * (1.0 / den)).astype(BF16)
            unit_done()

        for i in range(n_pairs):
            a, oc, u, sp = a_s[i], ocs[i], us[i], state[i]
            inner = jnp.concatenate([a[:BLK] * decay_ref[2 * i], a[BLK:] * decay_ref[2 * i + 1]], axis=0)
            oi = _dot(inner.astype(BF16), vpairs[i])
            for half in range(2):
                h = 2 * i + half
                vsl = slice(h * RET_V_DIM, (h + 1) * RET_V_DIM)
                hr = slice(half * BLK, (half + 1) * BLK)
                o = oi[hr, half * RET_V_DIM:(half + 1) * RET_V_DIM] + oc[hr] * xi_ref[:, vsl]
                r = _group_norm(o, gng_ref[:, vsl], gnb_ref[:, vsl]) * sc["gate"][slot, rows, vsl]
                mix[slot, rows, ATT_Q_W + h * RET_V_DIM:ATT_Q_W + (h + 1) * RET_V_DIM] = r.astype(BF16)
            state[i] = jnp.concatenate(
                [_GL_PROMPT[2 * i] * sp[:RET_QK_DIM] + u[:RET_QK_DIM, :RET_V_DIM],
                 _GL_PROMPT[2 * i + 1] * sp[RET_QK_DIM:] + u[RET_QK_DIM:, RET_V_DIM:]], axis=0)

    assert not pending
    return state


def _pm_wout_pieces(slot, x_ref, rows, wout_ref, h_ref, sc):
    pw = 2 * LANES
    n = D_MODEL // pw
    parts = []

    def piece(k):
        parts.append(_dot(sc["mix"][slot], wout_ref[:, k * pw:(k + 1) * pw]))
        if k == n - 1:
            h_ref[rows, :] = x_ref[...] + jnp.concatenate(parts, axis=1)

    return [functools.partial(piece, k) for k in range(n)]


def _prompt_mixer_kernel(sinks_ref, xfirst_ref, xodd_ref, xnext_ref, gmix_ref, win_ref, wout_ref, gng_ref, gnb_ref,
                           dist_ref, mask_ref, decay_ref, xi_ref, zeta_ref,
                           wupf_ref, wdnf_ref, wqf_ref, wof_ref,
                           h_ref, wk_ref, wv_ref, st_ref,
                           wupb_ref, wdnb_ref, wqb_ref, wob_ref,
                           qlo_s, qhi_s, kd0_s, kd1_s, vd0_s, vd1_s,
                           qrlo_s, qrhi_s, kr_s, vr_s, gate_s, mix_s, bias_s, state_s, xkeep_s):
    u = pl.program_id(0)
    tm = xnext_ref.shape[0]
    wupb_ref[...] = wupf_ref[...].astype(BF16)
    wdnb_ref[...] = wdnf_ref[...].astype(BF16)
    wqb_ref[...] = wqf_ref[...].astype(BF16)
    wob_ref[...] = wof_ref[...].astype(BF16)
    sc = dict(qlo=qlo_s, qhi=qhi_s, kd0=kd0_s, kd1=kd1_s, vd0=vd0_s, vd1=vd1_s, qrlo=qrlo_s, qrhi=qrhi_s,
              kr=kr_s, vr=vr_s, gate=gate_s, mix=mix_s, bias=bias_s)
    n_pairs = N_RET_HEADS // 2
    blocks = functools.partial(_pm_blocks, tm=tm, sinks_ref=sinks_ref, gng_ref=gng_ref,
                               gnb_ref=gnb_ref, decay_ref=decay_ref, xi_ref=xi_ref, zeta_ref=zeta_ref, sc=sc)

    @pl.when(u == 0)
    def _():
        for h in range(N_ATT_HEADS):
            bias_s[h] = NEG_SLOPES[h] * dist_ref[...] + mask_ref[...]
        state_s[...] = jnp.zeros_like(state_s)
        kd0_s[1] = jnp.zeros(kd0_s.shape[1:], BF16)
        kd1_s[1] = jnp.zeros(kd1_s.shape[1:], BF16)
        vd0_s[1] = jnp.zeros(vd0_s.shape[1:], BF16)
        vd1_s[1] = jnp.zeros(vd1_s.shape[1:], BF16)
        xkeep_s[...] = xfirst_ref[...]
        for stage in _pm_project_stages(xfirst_ref[...], 0, gmix_ref, win_ref, sc):
            stage()

    seq_start = (u % 2) == 0
    state = [jnp.where(seq_start, 0.0, state_s[i * LANES:(i + 1) * LANES, :]) for i in range(n_pairs)]

    prev_kd, prev_vd = _pm_last_block(1, tm, sc)
    stages = _pm_project_stages(xodd_ref[...], 1, gmix_ref, win_ref, sc, kv_out=(wk_ref, wv_ref))
    state = blocks(0, prev_kd, prev_vd, seq_start, state, stages)
    wout0 = _pm_wout_pieces(0, xkeep_s, slice(0, tm), wout_ref, h_ref, sc)

    prev_kd, prev_vd = _pm_last_block(0, tm, sc)
    stages = _pm_project_stages(xnext_ref[...], 0, gmix_ref, win_ref, sc)
    state = blocks(1, prev_kd, prev_vd, False, state, wout0 + stages)
    for piece in _pm_wout_pieces(1, xodd_ref, slice(tm, 2 * tm), wout_ref, h_ref, sc):
        piece()
    xkeep_s[...] = xnext_ref[...]

    for i in range(n_pairs):
        state_s[i * LANES:(i + 1) * LANES, :] = state[i]
        st_ref[0, i * LANES:(i + 1) * LANES, :] = state[i]


def _prompt_mixer(x, g_mix, w_in, w_out, sinks, gn_g, gn_b, side_f32):
    b, s, d = x.shape
    tm = TM_MIX
    n_tiles = b * s // tm
    steps = n_tiles // 2
    seq_steps = s // (2 * tm)
    assert s % (2 * tm) == 0 and seq_steps == 2, "kernel assumes 4 tiles per sequence"
    x2d = x.reshape(b * s, d)
    const = lambda shape: pl.BlockSpec(shape, lambda i: (0,) * len(shape), pipeline_mode=pl.Buffered(1))
    slot2 = lambda rows, cols, dt: pltpu.VMEM((2, rows, cols), dt)
    side_specs = [pl.BlockSpec((w.shape[0] // steps, w.shape[1]), lambda i: (i, 0)) for w in side_f32]
    outs = pl.pallas_call(
        _prompt_mixer_kernel,
        grid=(steps,),
        in_specs=[
            pl.BlockSpec(memory_space=pltpu.SMEM),
            const((tm, d)),
            pl.BlockSpec((tm, d), lambda i: (2 * i + 1, 0)),
            pl.BlockSpec((tm, d), lambda i: (jnp.minimum(2 * i + 2, n_tiles - 1), 0)),
            const((1, d)), const((d, D_IN)), const((MIX_OUT, d)),
            const((1, RET_V_W)), const((1, RET_V_W)),
            const((BLK, 2 * BLK)), const((BLK, 2 * BLK)),
            const((N_RET_HEADS, BLK, BLK)), const((BLK, RET_V_W)), const((BLK, RET_QK_W)),
        ] + side_specs,
        out_specs=[
            pl.BlockSpec((2 * tm, d), lambda i: (i, 0)),
            pl.BlockSpec((1, WINDOW, ATT_KV_W), lambda i: (i // seq_steps, 0, 0)),
            pl.BlockSpec((1, WINDOW, ATT_KV_W), lambda i: (i // seq_steps, 0, 0)),
            pl.BlockSpec((1, RET_QK_W, RET_V_DIM), lambda i: (i // seq_steps, 0, 0)),
        ] + side_specs,
        out_shape=[
            jax.ShapeDtypeStruct((b * s, d), F32),
            jax.ShapeDtypeStruct((b, WINDOW, ATT_KV_W), F32),
            jax.ShapeDtypeStruct((b, WINDOW, ATT_KV_W), F32),
            jax.ShapeDtypeStruct((b, RET_QK_W, RET_V_DIM), F32),
        ] + [jax.ShapeDtypeStruct(w.shape, BF16) for w in side_f32],
        scratch_shapes=[
            slot2(tm, ATT_Q_W, BF16), slot2(tm, ATT_Q_W, BF16),
            slot2(tm, LANES, BF16), slot2(tm, LANES, BF16),
            slot2(tm, 2 * LANES, BF16), slot2(tm, 2 * LANES, BF16),
            slot2(tm, RET_QK_W, BF16), slot2(tm, RET_QK_W, BF16),
            slot2(tm, RET_QK_W, F32), slot2(tm, RET_V_W, BF16),
            slot2(tm, RET_V_W, F32), slot2(tm, MIX_OUT, BF16),
            pltpu.VMEM((N_ATT_HEADS, BLK, 2 * BLK), F32),
            pltpu.VMEM((RET_QK_W, RET_V_DIM), F32),
            pltpu.VMEM((tm, d), F32),
        ],
        compiler_params=pltpu.CompilerParams(
            dimension_semantics=("arbitrary",), vmem_limit_bytes=VMEM_LIMIT),
        name="prompt_mixer",
    )(sinks, x2d, x2d, x2d, g_mix, w_in, w_out, gn_g, gn_b,
      jnp.asarray(_P_DIST), jnp.asarray(_P_MASK), jnp.asarray(_P_DECAY), jnp.asarray(_P_XI),
      jnp.asarray(_P_ZETA), *side_f32)
    return (outs[0].reshape(b, s, d),) + tuple(outs[1:])


def _memkv_kernel(mem_ref, g_ref, wk_ref, wv_ref, win_ref, wout_ref,
                  mk_ref, mv_ref, mkb_ref, mvb_ref, winb_ref, woutb_ref):
    winb_ref[...] = win_ref[...].astype(BF16)
    woutb_ref[...] = wout_ref[...].astype(BF16)
    mn = _rms(mem_ref[...], g_ref[...]).astype(BF16)
    mk = _dot(mn, wk_ref[...].astype(BF16))
    mv = _dot(mn, wv_ref[...].astype(BF16))
    tm = mem_ref.shape[0]
    group = X_D_HALVES * N_X_HEADS
    for hd in range(N_X_HEADS):
        for dh in range(X_D_HALVES):
            cols = slice(hd * X_HEAD_DIM + dh * LANES, hd * X_HEAD_DIM + (dh + 1) * LANES)
            rows = pl.ds(dh * N_X_HEADS + hd, tm, stride=group)
            mk_ref[rows, :] = mk[:, cols]
            mv_ref[rows, :] = mv[:, cols]
    mkb_ref[...] = mk.astype(BF16)
    mvb_ref[...] = mv.astype(BF16)


def _memory_kv(mem2d, g_mem, w_xk, w_xv, w_in, w_out):
    n, d = mem2d.shape
    tm = TM_MEM
    row = pl.BlockSpec((tm, d), lambda i: (i, 0))
    rows_out = pl.BlockSpec((tm * d // LANES, LANES), lambda i: (i, 0))
    const = lambda shape: pl.BlockSpec(shape, lambda i: (0,) * len(shape), pipeline_mode=pl.Buffered(1))
    steps = n // tm
    win_blk = pl.BlockSpec((w_in.shape[0] // steps, w_in.shape[1]), lambda i: (i, 0))
    wout_blk = pl.BlockSpec((w_out.shape[0] // steps, w_out.shape[1]), lambda i: (i, 0))
    return pl.pallas_call(
        _memkv_kernel,
        grid=(n // tm,),
        in_specs=[row, const((1, d)), const((d, d)), const((d, d)), win_blk, wout_blk],
        out_specs=[rows_out, rows_out, row, row, win_blk, wout_blk],
        out_shape=[jax.ShapeDtypeStruct((n * d // LANES, LANES), F32),
                   jax.ShapeDtypeStruct((n * d // LANES, LANES), F32),
                   jax.ShapeDtypeStruct((n, d), BF16), jax.ShapeDtypeStruct((n, d), BF16),
                   jax.ShapeDtypeStruct(w_in.shape, BF16), jax.ShapeDtypeStruct(w_out.shape, BF16)],
        compiler_params=pltpu.CompilerParams(
            dimension_semantics=("arbitrary",), vmem_limit_bytes=VMEM_LIMIT),
        name="memory_kv",
    )(mem2d, g_mem, w_xk, w_xv, w_in, w_out)


def _prompt_xattn_kernel(h_ref, g_ref, wq_ref, wo_ref, mk_ref, mv_ref, out_ref, o_s):
    def stages(r0):
        rows = slice(r0, r0 + SUB_ROWS)
        env = {}

        def project():
            env["h"] = h_ref[0, rows, :]
            xn = _rms(env["h"], g_ref[...]).astype(BF16)
            env["q"] = (_dot(xn, wq_ref[...]) * (X_HEAD_DIM ** -0.5)).astype(BF16)

        def scores(hd):
            sl = slice(hd * X_HEAD_DIM, (hd + 1) * X_HEAD_DIM)
            env[hd] = _dot_nt(env["q"][:, sl], mk_ref[0, :, sl])

        def head(hd):
            if hd + 1 < N_X_HEADS:
                scores(hd + 1)
            sl = slice(hd * X_HEAD_DIM, (hd + 1) * X_HEAD_DIM)
            s = env.pop(hd)
            m = jnp.max(s, axis=-1, keepdims=True)
            p = jnp.exp(s - m)
            p = p * (1.0 / jnp.sum(p, axis=-1, keepdims=True))
            o_s[rows, sl] = _dot(p.astype(BF16), mv_ref[0, :, sl]).astype(BF16)

        def output():
            out_ref[0, rows, :] = env["h"] + _dot(o_s[rows, :], wo_ref[...])

        def project_and_first_scores():
            project()
            scores(0)

        return ([project_and_first_scores] + [functools.partial(head, hd) for hd in range(N_X_HEADS)]
                + [output])

    chains = [stages(r0) for r0 in range(0, h_ref.shape[1], SUB_ROWS)]
    n_stage = len(chains[0])
    for step in range(n_stage + len(chains) - 1):
        for lag, chain in enumerate(chains):
            if 0 <= step - lag < n_stage:
                chain[step - lag]()


def _prompt_xattn(h, g, w_xq, w_xo, mkb, mvb):
    b, s, d = h.shape
    tm = TM_X
    const = lambda shape: pl.BlockSpec(shape, lambda i, j: (0,) * len(shape))
    tok = pl.BlockSpec((1, tm, d), lambda i, j: (i, j, 0))
    mem = pl.BlockSpec((1, N_MEM, d), lambda i, j: (i, 0, 0))
    return pl.pallas_call(
        _prompt_xattn_kernel,
        grid=(b, s // tm),
        in_specs=[tok, const((1, d)), const((d, d)), const((d, d)), mem, mem],
        out_specs=tok,
        out_shape=jax.ShapeDtypeStruct((b, s, d), F32),
        scratch_shapes=[pltpu.VMEM((tm, d), BF16)],
        compiler_params=pltpu.CompilerParams(
            dimension_semantics=("arbitrary", "arbitrary"), vmem_limit_bytes=VMEM_LIMIT),
        name="prompt_xattn",
    )(h, g, w_xq, w_xo, mkb, mvb)


def _sample_mixer_kernel(sinks_ref, x_ref, gmix_ref, win_ref, wout_ref, gng_ref, gnb_ref,
                         ck_ref, cv_ref, st_ref, bias_ref, dec_ref, xi_ref, zeta_ref,
                         h_ref, swk_ref, swv_ref, sst_ref):
    bb = ck_ref.shape[0]
    nt = bb // 2
    x = x_ref[...].reshape(bb * DEC_SEQ, D_MODEL)
    xn = _rms(x, gmix_ref[...]).astype(BF16)
    tile3 = lambda a: a.reshape(nt, SUBLANES, a.shape[-1])

    q = _dot(xn, win_ref[:, C_QA:C_QA + ATT_Q_W]) * (HEAD_DIM ** -0.5)
    kv = _dot(xn, win_ref[:, C_KV:C_KV + 2 * ATT_KV_W])
    qkr = _dot(xn, win_ref[:, C_QKR:C_QKR + 2 * RET_QK_W])
    vr = _dot(xn, win_ref[:, C_VR:C_VR + RET_V_W])
    gate3 = tile3(_silu(_dot(xn, win_ref[:, C_GR:C_GR + RET_V_W])))

    lo512, hi512 = _half_masks(ATT_Q_W)
    q_r = pltpu.roll(q, HALF, axis=1)
    q_nat3 = tile3(q)
    q_rot3 = tile3(q_r)
    lo3 = lo512.reshape(1, 1, ATT_Q_W)
    hi3 = hi512.reshape(1, 1, ATT_Q_W)
    qa3 = (q_nat3 * lo3).astype(BF16)
    qb3 = (q_rot3 * lo3).astype(BF16)
    qc3 = (q_rot3 * hi3).astype(BF16)
    qd3 = (q_nat3 * hi3).astype(BF16)
    t128 = lambda a, i: a[:, :, i * LANES:(i + 1) * LANES]
    qs = jnp.concatenate([t128(qa3, 0), t128(qb3, 1), t128(qa3, 1), t128(qb3, 2),
                          t128(qc3, 2), t128(qd3, 2), t128(qc3, 3), t128(qd3, 3)], axis=1)

    k3 = tile3(kv[:, :ATT_KV_W])
    v3 = tile3(kv[:, ATT_KV_W:])
    pad_kv = jnp.zeros((nt, BLK - SUBLANES, LANES), BF16)
    knew_pad = jnp.concatenate([k3.astype(BF16), pad_kv], axis=1)
    vnew_pad = jnp.concatenate([v3.astype(BF16), pad_kv], axis=1)
    to_lanes = lambda a3: jnp.swapaxes(
        jnp.concatenate([a3, jnp.zeros((nt, BLK - SUBLANES, LANES), F32)], axis=1), 1, 2)
    k3t, v3t = to_lanes(k3), to_lanes(v3)
    roll3 = lambda a, sh: pltpu.roll(a.reshape(nt * BLK, LANES), sh, axis=1).reshape(nt, BLK, LANES)

    lo256, _ = _half_masks(RET_QK_W)
    qr3 = tile3(qkr[:, :RET_QK_W])
    kr3 = tile3(qkr[:, RET_QK_W:] * (RET_QK_DIM ** -0.5))
    vr3 = tile3(vr)
    lane256 = lax.broadcasted_iota(jnp.int32, (1, 1, RET_QK_W), 2)
    qrs = jnp.concatenate(
        [(qr3 * ((lane256 >= h * RET_QK_DIM) & (lane256 < (h + 1) * RET_QK_DIM)).astype(F32)).astype(BF16)
         for h in range(N_RET_HEADS)],
        axis=1)
    kr_pad = jnp.concatenate([kr3.astype(BF16), jnp.zeros((nt, BLK - SUBLANES, RET_QK_W), BF16)], axis=1)
    vr_pad = jnp.concatenate([vr3.astype(BF16), jnp.zeros((nt, BLK - SUBLANES, RET_V_W), BF16)], axis=1)

    lane = lax.broadcasted_iota(jnp.int32, (1, 1, LANES), 2)
    row8 = lax.broadcasted_iota(jnp.int32, (1, SUBLANES, 1), 1)
    bmm_nt = lambda a, b: jnp.einsum('bqd,bkd->bqk', a, b, preferred_element_type=F32)
    bmm = lambda a, b: jnp.einsum('bqk,bkd->bqd', a, b, preferred_element_type=F32)

    att_par, ret_par = [], []
    for par in range(2):
        bsl = pl.ds(par, nt, stride=2)
        ckt = ck_ref[bsl]
        cvt = cv_ref[bsl]
        keep = lane < WINDOW - DEC_SEQ
        new_shift = WINDOW - DEC_SEQ - DEC_SEQ * par
        swk_ref[bsl] = jnp.where(keep, roll3(ckt, WINDOW - DEC_SEQ), roll3(k3t, new_shift))
        swv_ref[bsl] = jnp.where(keep, roll3(cvt, WINDOW - DEC_SEQ), roll3(v3t, new_shift))

        s = jnp.concatenate([bmm(qs, ckt.astype(BF16)), bmm_nt(qs, knew_pad)], axis=2) + bias_ref[par]
        ps = []
        for h in range(N_ATT_HEADS):
            ps.append(_sink_softmax(s[:, h * SUBLANES:(h + 1) * SUBLANES, :], sinks_ref[h]).astype(BF16))
        p_all = jnp.concatenate(ps, axis=1)
        o = bmm_nt(p_all[:, :, :BLK], cvt.astype(BF16)) + bmm(p_all[:, :, BLK:], vnew_pad)
        o_r = pltpu.roll(o.reshape(nt * N_ATT_HEADS * SUBLANES, LANES), HALF, axis=1).reshape(o.shape)
        hr = lambda a, h: a[:, h * SUBLANES:(h + 1) * SUBLANES, :]
        low = lane < HALF
        att_par.append(jnp.concatenate([
            jnp.where(low, hr(o, 0), hr(o_r, 1)), jnp.where(low, hr(o, 2), hr(o_r, 3)),
            jnp.where(low, hr(o_r, 4), hr(o, 5)), jnp.where(low, hr(o_r, 6), hr(o, 7))], axis=2))

        st = st_ref[bsl]
        oc = bmm(qrs, st.astype(BF16))
        inner = (bmm_nt(qrs, kr_pad) * dec_ref[par]).astype(BF16)
        oi = bmm(inner, vr_pad)
        rs = []
        for h in range(N_RET_HEADS):
            vsl = slice(h * RET_V_DIM, (h + 1) * RET_V_DIM)
            rsl = slice(h * SUBLANES, (h + 1) * SUBLANES)
            o_h = oi[:, rsl, vsl] + oc[:, rsl, :] * xi_ref[par, rsl, :]
            rs.append(_group_norm(o_h, gng_ref[:, vsl], gnb_ref[:, vsl]) * gate3[:, :, vsl])
        ret_par.append(jnp.concatenate(rs, axis=2))

        kz3 = (kr3 * zeta_ref[par]).astype(BF16)
        vr3_b = vr3.astype(BF16)
        for p in range(nt):
            for i in range(N_RET_HEADS // 2):
                u = _dot_tn(kz3[p][:, i * LANES:(i + 1) * LANES],
                            vr3_b[p][:, 2 * i * RET_V_DIM:(2 * i + 2) * RET_V_DIM])
                for half in range(2):
                    h = 2 * i + half
                    dsl = slice(h * RET_QK_DIM, (h + 1) * RET_QK_DIM)
                    sst_ref[2 * p + par, dsl, :] = (
                        _GL_SAMPLE[h] * st[p, dsl, :]
                        + u[half * RET_QK_DIM:(half + 1) * RET_QK_DIM, half * RET_V_DIM:(half + 1) * RET_V_DIM])

    own0 = row8 < DEC_SEQ
    att3 = jnp.where(own0, att_par[0], att_par[1])
    ret3 = jnp.where(own0, ret_par[0], ret_par[1])
    mix = jnp.concatenate([att3, ret3], axis=2).reshape(2 * nt * DEC_SEQ, MIX_OUT).astype(BF16)
    h_ref[...] = x + _dot(mix, wout_ref[...])


def _sample_mixer(x3d, g_mix, w_in, w_out, sinks, gn_g, gn_b, ck, cv, st):
    nb, ls, d = x3d.shape
    n = nb * ls
    bb = BB_MIX
    r = bb * DEC_SEQ
    const = lambda shape: pl.BlockSpec(shape, lambda i: (0,) * len(shape))
    row = pl.BlockSpec((r, d), lambda i: (i, 0))
    win = pl.BlockSpec((bb, WINDOW, ATT_KV_W), lambda i: (i, 0, 0))
    state = pl.BlockSpec((bb, RET_QK_W, RET_V_DIM), lambda i: (i, 0, 0))
    return pl.pallas_call(
        _sample_mixer_kernel,
        grid=(nb // bb,),
        in_specs=[
            pl.BlockSpec(memory_space=pltpu.SMEM),
            pl.BlockSpec((bb, ls, d), lambda i: (i, 0, 0)), const((1, d)), const((d, D_IN)), const((MIX_OUT, d)),
            const((1, RET_V_W)), const((1, RET_V_W)),
            win, win, state,
            const(_S_BIAS.shape), const(_S_DEC.shape), const(_S_XI.shape), const(_S_ZETA.shape),
        ],
        out_specs=[row, win, win, state],
        out_shape=[
            jax.ShapeDtypeStruct((n, d), F32),
            jax.ShapeDtypeStruct((nb, WINDOW, ATT_KV_W), F32),
            jax.ShapeDtypeStruct((nb, WINDOW, ATT_KV_W), F32),
            jax.ShapeDtypeStruct((nb, RET_QK_W, RET_V_DIM), F32),
        ],
        compiler_params=pltpu.CompilerParams(
            dimension_semantics=("arbitrary",), vmem_limit_bytes=VMEM_LIMIT),
        name="sample_mixer",
    )(sinks, x3d, g_mix, w_in, w_out, gn_g, gn_b, ck, cv, st,
      jnp.asarray(_S_BIAS), jnp.asarray(_S_DEC), jnp.asarray(_S_XI), jnp.asarray(_S_ZETA))


def _head_slab(x_ref, b, hd):
    group = X_D_HALVES * N_X_HEADS
    halves = [x_ref[b, pl.ds(dh * N_X_HEADS + hd, N_MEM, stride=group), :] for dh in range(X_D_HALVES)]
    return jnp.concatenate(halves, axis=1).astype(BF16)


def _mlp_value(h, g_ref, wup_ref, wdn_ref, gf_ref, fillers=None):
    xn = _rms(h, g_ref[...]).astype(BF16)
    piece = FF_CHUNK // N_X_HEADS
    opiece = D_MODEL // N_X_HEADS
    n_chunks = D_FF // FF_CHUNK
    nofill = (None, None, None)

    def up(c):
        qk, softmax, _ = fillers[c] if fillers is not None else nofill
        hid = []
        for k in range(N_X_HEADS):
            cols = slice(c * FF_CHUNK + k * piece, c * FF_CHUNK + (k + 1) * piece)
            u = jnp.maximum(_dot(xn, wup_ref[:, cols]), 0.0)
            hid.append((u * u).astype(BF16))
            if qk is not None:
                qk(k)
        if softmax is not None:
            softmax()
        return jnp.concatenate(hid, axis=1)

    def down(c, hid):
        pv = (fillers[c] if fillers is not None else nofill)[2]
        rows_c = slice(c * FF_CHUNK, (c + 1) * FF_CHUNK)
        out = []
        for k in range(N_X_HEADS):
            out.append(_dot(hid, wdn_ref[rows_c, k * opiece:(k + 1) * opiece]))
            if pv is not None:
                pv(k)
        return jnp.concatenate(out, axis=1)

    acc = h
    hid = up(0)
    for c in range(n_chunks):
        nxt = up(c + 1) if c + 1 < n_chunks else None
        acc = acc + down(c, hid)
        hid = nxt
    return _rms(acc, gf_ref[...])


def _mlp_xattn_kernel(hp_ref, hsm_ref, gx_ref, wq_ref, wo_ref, xk_ref, xv_ref, g_ref, wup_ref, wdn_ref, gf_ref,
                      yp_ref, ys_ref):
    i = pl.program_id(0)
    n = pl.num_programs(0) - 1
    bb = xk_ref.shape[0]
    assert bb == D_FF // FF_CHUNK and bb % 2 == 0

    @pl.when(i == 0)
    def _():
        xn = _rms(hsm_ref[...], gx_ref[...]).astype(BF16)
        ys_ref[...] = (_dot(xn, wq_ref[...]) * (X_HEAD_DIM ** -0.5)).reshape(ys_ref.shape)

    @pl.when(i < n)
    def _():
        own0 = lax.broadcasted_iota(jnp.int32, (SUBLANES, 1), 0) < DEC_SEQ
        o_rows = {}

        def attend(b):
            t = b // 2
            tile_b = pl.ds(i * bb + 2 * t, 2)
            env = dict(s=[], o=[])

            def qk(hd):
                if hd == 0:
                    env["q"] = ys_ref[tile_b].reshape(SUBLANES, D_MODEL).astype(BF16)
                env["s"].append(_dot_nt(env["q"][:, hd * X_HEAD_DIM:(hd + 1) * X_HEAD_DIM],
                                        _head_slab(xk_ref, b, hd)))

            def softmax():
                s = jnp.concatenate(env["s"], axis=0)
                m = jnp.max(s, axis=-1, keepdims=True)
                p = jnp.exp(s - m)
                env["p"] = p * (1.0 / jnp.sum(p, axis=-1, keepdims=True))

            def pv(hd):
                p = env["p"][hd * SUBLANES:(hd + 1) * SUBLANES].astype(BF16)
                env["o"].append(_dot(p, _head_slab(xv_ref, b, hd)))
                if hd == N_X_HEADS - 1:
                    o_rows[b] = jnp.concatenate(env["o"], axis=1)
                    if b % 2 == 1:
                        ys_ref[tile_b] = jnp.where(own0, o_rows[b - 1], o_rows[b]).reshape(2, DEC_SEQ, D_MODEL)

            return qk, softmax, pv

        fillers = [attend(b) for b in range(bb)]
        yp_ref[...] = _mlp_value(hp_ref[...], g_ref, wup_ref, wdn_ref, gf_ref, fillers)

    @pl.when(i == n)
    def _():
        o = ys_ref[...].reshape(hsm_ref.shape).astype(BF16)
        hs = hsm_ref[...] + _dot(o, wo_ref[...])
        ys_ref[...] = _mlp_value(hs, g_ref, wup_ref, wdn_ref, gf_ref).reshape(ys_ref.shape)


def _mlp_xattn(hp2d, hsm, g_xattn, w_xq, w_xo, xk, xv, g_mlp, w_up, w_down, g_final):
    n, d = hp2d.shape
    ns = hsm.shape[0]
    nb = xk.shape[0]
    bb = BB_X
    tm = n // (nb // bb)
    n_tiles = n // tm
    assert n_tiles * bb == nb and tm % SUBLANES == 0
    clip = lambda i: jnp.minimum(i, n_tiles - 1)
    prompt = pl.BlockSpec((tm, d), lambda i: (clip(i), 0))
    mem = pl.BlockSpec((bb,) + xk.shape[1:], lambda i: (clip(i), 0, 0))
    const = lambda shape: pl.BlockSpec(shape, lambda i: (0,) * len(shape), pipeline_mode=pl.Buffered(1))
    return pl.pallas_call(
        _mlp_xattn_kernel,
        grid=(n_tiles + 1,),
        in_specs=[prompt, const((ns, d)), const((1, d)), const((d, d)), const((d, d)), mem, mem,
                  const((1, d)), const((d, D_FF)), const((D_FF, d)), const((1, d))],
        out_specs=[prompt, pl.BlockSpec((ns // DEC_SEQ, DEC_SEQ, d), lambda i: (0, 0, 0))],
        out_shape=[jax.ShapeDtypeStruct((n, d), F32), jax.ShapeDtypeStruct((ns // DEC_SEQ, DEC_SEQ, d), F32)],
        compiler_params=pltpu.CompilerParams(
            dimension_semantics=("arbitrary",), vmem_limit_bytes=VMEM_LIMIT),
        name="mlp_xattn",
    )(hp2d, hsm, g_xattn, w_xq, w_xo, xk, xv, g_mlp, w_up, w_down, g_final)


def _mem_rows(c):
    nb = c.shape[0]
    c = c.reshape(nb, N_MEM, N_X_HEADS, X_D_HALVES, LANES)
    return jnp.transpose(c, (0, 1, 3, 2, 4)).reshape(nb, N_MEM * X_D_HALVES * N_X_HEADS, LANES)


def kernel(x_prompt, x_sample, mem_prompt, cache_win_k, cache_win_v, state_ret, cache_mem_k, cache_mem_v,
           g_mix, w_in, attn_sinks, ret_gn_g, ret_gn_b, w_out, g_xattn, g_mem, w_xq, w_xk, w_xv, w_xo,
           g_mlp, w_up, w_down, g_final):
    depth = w_in.shape[0]
    assert depth == 1, "single-layer trunk"
    b, s, d = x_prompt.shape
    nb, ls, _ = x_sample.shape
    row = lambda a: a.reshape(1, -1)
    sinks = attn_sinks[0]
    gn_g, gn_b = row(ret_gn_g[0]), row(ret_gn_b[0])
    g_fin = row(g_final)

    mk, mv, mkb, mvb, w_in_b, w_out_b = _memory_kv(
        mem_prompt.reshape(b * N_MEM, d), row(g_mem[0]), w_xk[0], w_xv[0], w_in[0], w_out[0])
    hp, p_wk, p_wv, p_rs, w_up_b, w_dn_b, w_xq_b, w_xo_b = _prompt_mixer(
        x_prompt, row(g_mix[0]), w_in_b, w_out_b, sinks, gn_g, gn_b,
        (w_up[0], w_down[0], w_xq[0], w_xo[0]))
    hp = _prompt_xattn(hp, row(g_xattn[0]), w_xq_b, w_xo_b,
                       mkb.reshape(b, N_MEM, d), mvb.reshape(b, N_MEM, d))

    win_t = lambda c: jnp.transpose(c, (0, 2, 3, 1)).reshape(nb, ATT_KV_W, WINDOW)
    win_t_inv = lambda a: jnp.transpose(a.reshape(-1, N_KV_HEADS, HEAD_DIM, WINDOW),
                                        (0, 3, 1, 2)).reshape(1, -1, WINDOW, N_KV_HEADS, HEAD_DIM)
    hs, s_wk, s_wv, s_rs = _sample_mixer(
        x_sample, row(g_mix[0]), w_in_b, w_out_b, sinks, gn_g, gn_b,
        win_t(cache_win_k[0]), win_t(cache_win_v[0]), state_ret[0].reshape(nb, RET_QK_W, RET_V_DIM))

    y_prompt, y_sample = _mlp_xattn(
        hp.reshape(b * s, d), hs, row(g_xattn[0]), w_xq_b, w_xo_b,
        _mem_rows(cache_mem_k[0]), _mem_rows(cache_mem_v[0]), row(g_mlp[0]), w_up_b, w_dn_b, g_fin)
    y_prompt = y_prompt.reshape(b, s, d)

    ret5 = lambda a, n: a.reshape(1, n, N_RET_HEADS, RET_QK_DIM, RET_V_DIM)
    mem5 = lambda a: jnp.transpose(a.reshape(b, N_MEM, X_D_HALVES, N_X_HEADS, LANES),
                                   (0, 1, 3, 2, 4)).reshape(1, b, N_MEM, N_X_HEADS, X_HEAD_DIM)
    return (y_prompt, y_sample,
            win_t_inv(p_wk), win_t_inv(p_wv), ret5(p_rs, b), mem5(mk), mem5(mv),
            win_t_inv(s_wk), win_t_inv(s_wv), ret5(s_rs, nb))
```

```python
import functools

import jax
import jax.numpy as jnp
import numpy as np
from jax import lax
from jax.experimental import pallas as pl
from jax.experimental.pallas import tpu as pltpu
from jax.experimental.pallas import tpu_sc as plsc

F32 = jnp.float32
BF16 = jnp.bfloat16

D_MODEL = 1024
BATCH = 8
SEQ = 2048
DEC_BATCH = 128
DEC_SEQ = 4
HEAD_DIM = 64
N_ATT_HEADS = 8
N_KV_HEADS = 2
KV_GROUP = N_ATT_HEADS // N_KV_HEADS
WINDOW = 128
BLK = 128
N_RET_HEADS = 4
RET_QK_DIM = 64
RET_V_DIM = 128
N_MEM = 256
N_X_HEADS = 4
X_HEAD_DIM = D_MODEL // N_X_HEADS
D_FF = 4 * D_MODEL
RMS_EPS = 1e-6
GN_EPS = 1e-5

ATT_Q_W = N_ATT_HEADS * HEAD_DIM
ATT_KV_W = N_KV_HEADS * HEAD_DIM
RET_QK_W = N_RET_HEADS * RET_QK_DIM
RET_V_W = N_RET_HEADS * RET_V_DIM
MIX_OUT = ATT_Q_W + RET_V_W
D_IN = ATT_Q_W + 2 * ATT_KV_W + 2 * RET_QK_W + 2 * RET_V_W
C_QA, C_KV, C_QKR, C_VR, C_GR = 0, 512, 768, 1280, 1792

LANES = 128
SUBLANES = 8
HALF = LANES // 2
X_D_HALVES = X_HEAD_DIM // LANES
NEG = -1e30
VMEM_LIMIT = 56 * 1024 * 1024

TM_MIX = 512
TM_X = 2048
SUB_ROWS = 512
FF_CHUNK = 1024
BB_MIX = 32
BB_X = 4

NEG_SLOPES = [-(2.0 ** (-8.0 * (i + 1) / N_ATT_HEADS)) for i in range(N_ATT_HEADS)]
_LOG_G = np.log(1.0 - 2.0 ** (-5.0 - np.arange(N_RET_HEADS))).astype(np.float32).astype(np.float64)


def _prompt_tables():
    qi = np.arange(BLK)[:, None]
    kj = np.arange(2 * BLK)[None, :]
    dist = (qi + BLK - kj).astype(np.float64)
    mask = np.where((dist >= 0) & (dist < WINDOW), 0.0, NEG)
    l = np.arange(BLK, dtype=np.float64)
    diff = l[:, None] - l[None, :]
    decay = np.where(diff >= 0, np.exp(_LOG_G[:, None, None] * np.maximum(diff, 0.0)), 0.0)
    xi = np.exp((l[:, None] + 1.0) * _LOG_G[None, :])
    zeta = np.exp((BLK - 1.0 - l)[:, None] * _LOG_G[None, :])
    xi_t = np.repeat(xi, RET_V_DIM, axis=1)
    zeta_t = np.repeat(zeta, RET_QK_DIM, axis=1)
    f = lambda a: np.asarray(a, np.float32)
    return f(dist), f(mask), f(decay), f(xi_t), f(zeta_t)


def _sample_tables():
    slopes = -np.asarray(NEG_SLOPES)
    bias = np.full((2, N_ATT_HEADS * SUBLANES, 2 * BLK), NEG, np.float64)
    dec = np.zeros((2, N_RET_HEADS * SUBLANES, BLK), np.float64)
    xi = np.zeros((2, N_RET_HEADS * SUBLANES, RET_V_DIM), np.float64)
    zeta = np.zeros((2, SUBLANES, RET_QK_W), np.float64)
    for par in range(2):
        for r in range(SUBLANES):
            own = DEC_SEQ * par <= r < DEC_SEQ * (par + 1)
            t = r - DEC_SEQ * par if own else r % DEC_SEQ
            for h in range(N_ATT_HEADS):
                row = h * SUBLANES + r
                for j in range(WINDOW):
                    d = t + WINDOW - j
                    if 0 <= d < WINDOW:
                        bias[par, row, j] = -slopes[h] * d
                for c in range(DEC_SEQ):
                    d = t - c
                    if d >= 0:
                        bias[par, row, WINDOW + DEC_SEQ * par + c] = -slopes[h] * d
            for h in range(N_RET_HEADS):
                row = h * SUBLANES + r
                if own:
                    xi[par, row, :] = np.exp((t + 1.0) * _LOG_G[h])
                    zeta[par, r, h * RET_QK_DIM:(h + 1) * RET_QK_DIM] = np.exp((DEC_SEQ - 1.0 - t) * _LOG_G[h])
                    for c in range(t + 1):
                        dec[par, row, DEC_SEQ * par + c] = np.exp(_LOG_G[h] * (t - c))
    f = lambda a: np.asarray(a, np.float32)
    return f(bias), f(dec), f(xi), f(zeta)


_P_DIST, _P_MASK, _P_DECAY, _P_XI, _P_ZETA = _prompt_tables()
_S_BIAS, _S_DEC, _S_XI, _S_ZETA = _sample_tables()
_GL_PROMPT = [float(np.exp(_LOG_G[h] * BLK)) for h in range(N_RET_HEADS)]
_GL_SAMPLE = [float(np.exp(_LOG_G[h] * DEC_SEQ)) for h in range(N_RET_HEADS)]


def _rms(x, g):
    return x * lax.rsqrt(jnp.mean(x * x, axis=-1, keepdims=True) + RMS_EPS) * g


def _dot(a, b):
    return jnp.dot(a, b, preferred_element_type=F32)


def _dot_nt(a, b):
    return lax.dot_general(a, b, (((1,), (1,)), ((), ())), preferred_element_type=F32)


def _dot_tn(a, b):
    return lax.dot_general(a, b, (((0,), (0,)), ((), ())), preferred_element_type=F32)


def _silu(g):
    return g * (1.0 / (1.0 + jnp.exp(-g)))


def _half_masks(width):
    lane = lax.broadcasted_iota(jnp.int32, (1, width), 1)
    lo = ((lane & (LANES - 1)) < HALF).astype(F32)
    return lo, 1.0 - lo


def _sink_softmax(s, sink):
    m = jnp.maximum(jnp.max(s, axis=-1, keepdims=True), sink)
    p = jnp.exp(s - m)
    den = jnp.sum(p, axis=-1, keepdims=True) + jnp.exp(sink - m)
    return p * (1.0 / den)


def _group_norm(o, g, b):
    mu = jnp.mean(o, axis=-1, keepdims=True)
    d = o - mu
    var = jnp.mean(d * d, axis=-1, keepdims=True)
    return d * lax.rsqrt(var + GN_EPS) * g + b


def _pm_project_stages(x, slot, gmix_ref, win_ref, sc, kv_out=None):
    tm = x.shape[0]
    xn = _rms(x, gmix_ref[...]).astype(BF16)

    pw = 2 * LANES
    lo, hi = _half_masks(pw)

    def stage_q(i):
        cols = slice(i * pw, (i + 1) * pw)
        q = _dot(xn, win_ref[:, C_QA + i * pw:C_QA + (i + 1) * pw])
        sc["qlo"][slot, :, cols] = (q * (lo * HEAD_DIM ** -0.5)).astype(BF16)
        sc["qhi"][slot, :, cols] = (q * (hi * HEAD_DIM ** -0.5)).astype(BF16)

    def stage_kv():
        z = _dot(xn, win_ref[:, C_KV:C_KV + pw])
        low = lax.broadcasted_iota(jnp.int32, (tm, LANES), 1) < HALF
        k = z[:, 0:ATT_KV_W]
        v = z[:, ATT_KV_W:2 * ATT_KV_W]
        if kv_out is not None:
            kv_out[0][0] = k[tm - WINDOW:, :].T
            kv_out[1][0] = v[tm - WINDOW:, :].T
        k_r = pltpu.roll(k, HALF, axis=1)
        v_r = pltpu.roll(v, HALF, axis=1)
        sc["kd0"][slot] = jnp.where(low, k, k_r).astype(BF16)
        sc["kd1"][slot] = jnp.where(low, k_r, k).astype(BF16)
        sc["vd0"][slot, :, 0:LANES] = jnp.where(low, v, 1.0).astype(BF16)
        sc["vd0"][slot, :, LANES:2 * LANES] = jnp.where(low, 1.0, v_r).astype(BF16)
        sc["vd1"][slot, :, 0:LANES] = jnp.where(low, v_r, 1.0).astype(BF16)
        sc["vd1"][slot, :, LANES:2 * LANES] = jnp.where(low, 1.0, v).astype(BF16)

    def stage_qr():
        qr = _dot(xn, win_ref[:, C_QKR:C_QKR + pw])
        sc["qrlo"][slot] = (qr * lo).astype(BF16)
        sc["qrhi"][slot] = (qr * hi).astype(BF16)

    def stage_kr():
        sc["kr"][slot] = _dot(xn, win_ref[:, C_QKR + pw:C_QKR + 2 * pw]) * (RET_QK_DIM ** -0.5)

    def stage_vr(i):
        cols = slice(i * pw, (i + 1) * pw)
        sc["vr"][slot, :, cols] = _dot(xn, win_ref[:, C_VR + i * pw:C_VR + (i + 1) * pw]).astype(BF16)

    def stage_gate(i):
        cols = slice(i * pw, (i + 1) * pw)
        sc["gate"][slot, :, cols] = _silu(_dot(xn, win_ref[:, C_GR + i * pw:C_GR + (i + 1) * pw]))

    part = functools.partial
    return [part(stage_q, 0), part(stage_q, 1), stage_kv, stage_qr, stage_kr,
            part(stage_vr, 0), part(stage_vr, 1), part(stage_gate, 0), part(stage_gate, 1)]


def _pm_last_block(slot, tm, sc):
    rows = slice(tm - BLK, tm)
    return ([sc["kd0"][slot, rows, :], sc["kd1"][slot, rows, :]],
            [sc["vd0"][slot, rows, :], sc["vd1"][slot, rows, :]])


def _pm_blocks(slot, prev_kd, prev_vd, is_first, state, fillers, tm, sinks_ref, gng_ref, gnb_ref,
               decay_ref, xi_ref, zeta_ref, sc):
    nblk = tm // BLK
    n_units = nblk * (N_KV_HEADS + N_RET_HEADS // 2)
    pending = list(fillers)
    done_units = [0]

    def unit_done():
        done_units[0] += 1
        while pending and (len(fillers) - len(pending)) * n_units < done_units[0] * len(fillers):
            pending.pop(0)()
    lowb = lax.broadcasted_iota(jnp.int32, (BLK, LANES), 1) < HALF
    col = lax.broadcasted_iota(jnp.int32, (BLK, 2 * BLK), 1)
    first_mask = None if is_first is False else jnp.where((col < BLK) & is_first, NEG, 0.0)
    kd_refs = (sc["kd0"], sc["kd1"])
    vd_refs = (sc["vd0"], sc["vd1"])
    qlo, qhi, mix = sc["qlo"], sc["qhi"], sc["mix"]
    n_pairs = N_RET_HEADS // 2

    for j in range(nblk):
        rows = slice(j * BLK, (j + 1) * BLK)
        c0s = [kvh * KV_GROUP * HEAD_DIM for kvh in range(N_KV_HEADS)]
        lsls = [slice(i * LANES, (i + 1) * LANES) for i in range(n_pairs)]
        vds, scores = [], []
        for kvh in range(N_KV_HEADS):
            if j == 0:
                kd = jnp.concatenate([prev_kd[kvh], kd_refs[kvh][slot, rows, :]], axis=0)
                vds.append(jnp.concatenate([prev_vd[kvh], vd_refs[kvh][slot, rows, :]], axis=0))
            else:
                krows = slice((j - 1) * BLK, (j + 1) * BLK)
                kd = kd_refs[kvh][slot, krows, :]
                vds.append(vd_refs[kvh][slot, krows, :])
            c0 = c0s[kvh]
            qst = jnp.concatenate([qlo[slot, rows, c0:c0 + LANES], qhi[slot, rows, c0:c0 + LANES],
                                   qlo[slot, rows, c0 + LANES:c0 + 2 * LANES],
                                   qhi[slot, rows, c0 + LANES:c0 + 2 * LANES]], axis=0)
            scores.append(_dot_nt(qst, kd))
        unit_done()

        kps = [sc["kr"][slot, rows, lsls[i]] for i in range(n_pairs)]
        vpairs = [sc["vr"][slot, rows, 2 * i * RET_V_DIM:(2 * i + 2) * RET_V_DIM] for i in range(n_pairs)]
        q2s = [jnp.concatenate([sc["qrlo"][slot, rows, lsls[i]], sc["qrhi"][slot, rows, lsls[i]]], axis=0)
               for i in range(n_pairs)]
        a_s = [_dot_nt(q2s[i], kps[i].astype(BF16)) for i in range(n_pairs)]
        ocs = [_dot(q2s[i], state[i].astype(BF16)) for i in range(n_pairs)]
        us = [_dot_tn((kps[i] * zeta_ref[:, lsls[i]]).astype(BF16), vpairs[i]) for i in range(n_pairs)]
        unit_done()

        for kvh in range(N_KV_HEADS):
            s, vd, c0 = scores[kvh], vds[kvh], c0s[kvh]
            es, esink = [], []
            for g in range(KV_GROUP):
                h = kvh * KV_GROUP + g
                sg = s[g * BLK:(g + 1) * BLK] + sc["bias"][h]
                if j == 0 and first_mask is not None:
                    sg = sg + first_mask
                sink = sinks_ref[h]
                m = jnp.maximum(jnp.max(sg, axis=-1, keepdims=True), sink)
                es.append(jnp.exp(sg - m).astype(BF16))
                esink.append(jnp.exp(sink - m))
            o = _dot(jnp.concatenate(es, axis=0), vd)
            for pair in range(KV_GROUP // 2):
                oe = o[2 * pair * BLK:(2 * pair + 1) * BLK]
                oo = o[(2 * pair + 1) * BLK:(2 * pair + 2) * BLK]
                num = jnp.where(lowb, oe[:, :LANES], oo[:, LANES:])
                den = (jnp.where(lowb, oe[:, LANES:], oo[:, :LANES])
                       + jnp.where(lowb, esink[2 * pair], esink[2 * pair + 1]))
                cs = c0 + pair * LANES
                mix[slot, rows, cs:cs + LANES] = (num * (1.0 / den)).astype(BF16)
            unit_done()

        for i in range(n_pairs):
            a, oc, u, sp = a_s[i], ocs[i], us[i], state[i]
            inner = jnp.concatenate([a[:BLK] * decay_ref[2 * i], a[BLK:] * decay_ref[2 * i + 1]], axis=0)
            oi = _dot(inner.astype(BF16), vpairs[i])
            for half in range(2):
                h = 2 * i + half
                vsl = slice(h * RET_V_DIM, (h + 1) * RET_V_DIM)
                hr = slice(half * BLK, (half + 1) * BLK)
                o = oi[hr, half * RET_V_DIM:(half + 1) * RET_V_DIM] + oc[hr] * xi_ref[:, vsl]
                r = _group_norm(o, gng_ref[:, vsl], gnb_ref[:, vsl]) * sc["gate"][slot, rows, vsl]
                mix[slot, rows, ATT_Q_W + h * RET_V_DIM:ATT_Q_W + (h + 1) * RET_V_DIM] = r.astype(BF16)
            state[i] = jnp.concatenate(
                [_GL_PROMPT[2 * i] * sp[:RET_QK_DIM] + u[:RET_QK_DIM, :RET_V_DIM],
                 _GL_PROMPT[2 * i + 1] * sp[RET_QK_DIM:] + u[RET_QK_DIM:, RET_V_DIM:]], axis=0)

    assert not pending
    return state


def _pm_wout_pieces(slot, x_ref, rows, wout_ref, h_ref, sc):
    pw = 2 * LANES
    n = D_MODEL // pw
    parts = []

    def piece(k):
        parts.append(_dot(sc["mix"][slot], wout_ref[:, k * pw:(k + 1) * pw]))
        if k == n - 1:
            h_ref[rows, :] = x_ref[...] + jnp.concatenate(parts, axis=1)

    return [functools.partial(piece, k) for k in range(n)]


def _prompt_mixer_kernel(sinks_ref, xfirst_ref, xodd_ref, xnext_ref, gmix_ref, win_ref, wout_ref, gng_ref, gnb_ref,
                           dist_ref, mask_ref, decay_ref, xi_ref, zeta_ref,
                           wqf_ref, wof_ref,
                           h_ref, wk_ref, wv_ref, st_ref,
                           wqb_ref, wob_ref,
                           qlo_s, qhi_s, kd0_s, kd1_s, vd0_s, vd1_s,
                           qrlo_s, qrhi_s, kr_s, vr_s, gate_s, mix_s, bias_s, state_s, xkeep_s):
    u = pl.program_id(0)
    tm = xnext_ref.shape[0]
    wqb_ref[...] = wqf_ref[...].astype(BF16)
    wob_ref[...] = wof_ref[...].astype(BF16)
    sc = dict(qlo=qlo_s, qhi=qhi_s, kd0=kd0_s, kd1=kd1_s, vd0=vd0_s, vd1=vd1_s, qrlo=qrlo_s, qrhi=qrhi_s,
              kr=kr_s, vr=vr_s, gate=gate_s, mix=mix_s, bias=bias_s)
    n_pairs = N_RET_HEADS // 2
    blocks = functools.partial(_pm_blocks, tm=tm, sinks_ref=sinks_ref, gng_ref=gng_ref,
                               gnb_ref=gnb_ref, decay_ref=decay_ref, xi_ref=xi_ref, zeta_ref=zeta_ref, sc=sc)

    @pl.when(u == 0)
    def _():
        for h in range(N_ATT_HEADS):
            bias_s[h] = NEG_SLOPES[h] * dist_ref[...] + mask_ref[...]
        state_s[...] = jnp.zeros_like(state_s)
        kd0_s[1] = jnp.zeros(kd0_s.shape[1:], BF16)
        kd1_s[1] = jnp.zeros(kd1_s.shape[1:], BF16)
        vd0_s[1] = jnp.zeros(vd0_s.shape[1:], BF16)
        vd1_s[1] = jnp.zeros(vd1_s.shape[1:], BF16)
        xkeep_s[...] = xfirst_ref[...]
        for stage in _pm_project_stages(xfirst_ref[...], 0, gmix_ref, win_ref, sc):
            stage()

    seq_start = (u % 2) == 0
    state = [jnp.where(seq_start, 0.0, state_s[i * LANES:(i + 1) * LANES, :]) for i in range(n_pairs)]

    prev_kd, prev_vd = _pm_last_block(1, tm, sc)
    stages = _pm_project_stages(xodd_ref[...], 1, gmix_ref, win_ref, sc, kv_out=(wk_ref, wv_ref))
    state = blocks(0, prev_kd, prev_vd, seq_start, state, stages)
    wout0 = _pm_wout_pieces(0, xkeep_s, slice(0, tm), wout_ref, h_ref, sc)

    prev_kd, prev_vd = _pm_last_block(0, tm, sc)
    stages = _pm_project_stages(xnext_ref[...], 0, gmix_ref, win_ref, sc)
    state = blocks(1, prev_kd, prev_vd, False, state, wout0 + stages)
    for piece in _pm_wout_pieces(1, xodd_ref, slice(tm, 2 * tm), wout_ref, h_ref, sc):
        piece()
    xkeep_s[...] = xnext_ref[...]

    for i in range(n_pairs):
        state_s[i * LANES:(i + 1) * LANES, :] = state[i]
        st_ref[0, i * LANES:(i + 1) * LANES, :] = state[i]


def _prompt_mixer(x, g_mix, w_in, w_out, sinks, gn_g, gn_b, side_f32):
    b, s, d = x.shape
    tm = TM_MIX
    n_tiles = b * s // tm
    steps = n_tiles // 2
    seq_steps = s // (2 * tm)
    assert s % (2 * tm) == 0 and seq_steps == 2, "kernel assumes 4 tiles per sequence"
    x2d = x.reshape(b * s, d)
    const = lambda shape: pl.BlockSpec(shape, lambda i: (0,) * len(shape), pipeline_mode=pl.Buffered(1))
    slot2 = lambda rows, cols, dt: pltpu.VMEM((2, rows, cols), dt)
    side_specs = [pl.BlockSpec((w.shape[0] // steps, w.shape[1]), lambda i: (i, 0)) for w in side_f32]
    outs = pl.pallas_call(
        _prompt_mixer_kernel,
        grid=(steps,),
        in_specs=[
            pl.BlockSpec(memory_space=pltpu.SMEM),
            const((tm, d)),
            pl.BlockSpec((tm, d), lambda i: (2 * i + 1, 0)),
            pl.BlockSpec((tm, d), lambda i: (jnp.minimum(2 * i + 2, n_tiles - 1), 0)),
            const((1, d)), const((d, D_IN)), const((MIX_OUT, d)),
            const((1, RET_V_W)), const((1, RET_V_W)),
            const((BLK, 2 * BLK)), const((BLK, 2 * BLK)),
            const((N_RET_HEADS, BLK, BLK)), const((BLK, RET_V_W)), const((BLK, RET_QK_W)),
        ] + side_specs,
        out_specs=[
            pl.BlockSpec((2 * tm, d), lambda i: (i, 0)),
            pl.BlockSpec((1, WINDOW, ATT_KV_W), lambda i: (i // seq_steps, 0, 0)),
            pl.BlockSpec((1, WINDOW, ATT_KV_W), lambda i: (i // seq_steps, 0, 0)),
            pl.BlockSpec((1, RET_QK_W, RET_V_DIM), lambda i: (i // seq_steps, 0, 0)),
        ] + side_specs,
        out_shape=[
            jax.ShapeDtypeStruct((b * s, d), F32),
            jax.ShapeDtypeStruct((b, WINDOW, ATT_KV_W), F32),
            jax.ShapeDtypeStruct((b, WINDOW, ATT_KV_W), F32),
            jax.ShapeDtypeStruct((b, RET_QK_W, RET_V_DIM), F32),
        ] + [jax.ShapeDtypeStruct(w.shape, BF16) for w in side_f32],
        scratch_shapes=[
            slot2(tm, ATT_Q_W, BF16), slot2(tm, ATT_Q_W, BF16),
            slot2(tm, LANES, BF16), slot2(tm, LANES, BF16),
            slot2(tm, 2 * LANES, BF16), slot2(tm, 2 * LANES, BF16),
            slot2(tm, RET_QK_W, BF16), slot2(tm, RET_QK_W, BF16),
            slot2(tm, RET_QK_W, F32), slot2(tm, RET_V_W, BF16),
            slot2(tm, RET_V_W, F32), slot2(tm, MIX_OUT, BF16),
            pltpu.VMEM((N_ATT_HEADS, BLK, 2 * BLK), F32),
            pltpu.VMEM((RET_QK_W, RET_V_DIM), F32),
            pltpu.VMEM((tm, d), F32),
        ],
        compiler_params=pltpu.CompilerParams(
            dimension_semantics=("arbitrary",), vmem_limit_bytes=VMEM_LIMIT),
        name="prompt_mixer",
    )(sinks, x2d, x2d, x2d, g_mix, w_in, w_out, gn_g, gn_b,
      jnp.asarray(_P_DIST), jnp.asarray(_P_MASK), jnp.asarray(_P_DECAY), jnp.asarray(_P_XI),
      jnp.asarray(_P_ZETA), *side_f32)
    return (outs[0].reshape(b, s, d),) + tuple(outs[1:])


SC_BLOCK = (32, 1024)
SC_REG = (8, 16)


def _sc_cast_bf16(w):
    rows, cols = w.shape
    assert rows % SC_BLOCK[0] == 0 and cols % SC_BLOCK[1] == 0
    mesh = plsc.VectorSubcoreMesh(core_axis_name="core", subcore_axis_name="subcore")

    @pl.kernel(out_type=jax.ShapeDtypeStruct(w.shape, BF16), mesh=mesh, scratch_types=[])
    def cast(w_hbm, o_hbm):
        def body(in_v, out_v):
            @pl.loop(0, in_v.shape[0], step=SC_REG[0])
            def _(r0):
                @pl.loop(0, in_v.shape[1], step=SC_REG[1])
                def _(c0):
                    slc = (pl.ds(r0, SC_REG[0]), pl.ds(c0, SC_REG[1]))
                    out_v.at[*slc][...] = in_v.at[*slc][...].astype(BF16)

        pltpu.emit_pipeline(
            body,
            grid=(rows // SC_BLOCK[0], cols // SC_BLOCK[1]),
            in_specs=[pl.BlockSpec(block_shape=SC_BLOCK, index_map=lambda i, j: (i, j))],
            out_specs=[pl.BlockSpec(block_shape=SC_BLOCK, index_map=lambda i, j: (i, j))],
            core_axis_name=("core", "subcore"),
            dimension_semantics=(pltpu.PARALLEL, pltpu.PARALLEL),
        )(w_hbm, o_hbm)

    return cast(w)


def _memkv_kernel(mem_ref, g_ref, wk_ref, wv_ref, win_ref, wout_ref,
                  mk_ref, mv_ref, mkb_ref, mvb_ref, winb_ref, woutb_ref):
    winb_ref[...] = win_ref[...].astype(BF16)
    woutb_ref[...] = wout_ref[...].astype(BF16)
    mn = _rms(mem_ref[...], g_ref[...]).astype(BF16)
    mk = _dot(mn, wk_ref[...].astype(BF16))
    mv = _dot(mn, wv_ref[...].astype(BF16))
    tm = mem_ref.shape[0]
    group = X_D_HALVES * N_X_HEADS
    for hd in range(N_X_HEADS):
        for dh in range(X_D_HALVES):
            cols = slice(hd * X_HEAD_DIM + dh * LANES, hd * X_HEAD_DIM + (dh + 1) * LANES)
            rows = pl.ds(dh * N_X_HEADS + hd, tm, stride=group)
            mk_ref[rows, :] = mk[:, cols]
            mv_ref[rows, :] = mv[:, cols]
    mkb_ref[...] = mk.astype(BF16)
    mvb_ref[...] = mv.astype(BF16)


def _memory_kv(mem2d, g_mem, w_xk, w_xv, w_in, w_out):
    n, d = mem2d.shape
    tm = 512
    row = pl.BlockSpec((tm, d), lambda i: (i, 0))
    rows_out = pl.BlockSpec((tm * d // LANES, LANES), lambda i: (i, 0))
    const = lambda shape: pl.BlockSpec(shape, lambda i: (0,) * len(shape), pipeline_mode=pl.Buffered(1))
    steps = n // tm
    win_blk = pl.BlockSpec((w_in.shape[0] // steps, w_in.shape[1]), lambda i: (i, 0))
    wout_blk = pl.BlockSpec((w_out.shape[0] // steps, w_out.shape[1]), lambda i: (i, 0))
    return pl.pallas_call(
        _memkv_kernel,
        grid=(n // tm,),
        in_specs=[row, const((1, d)), const((d, d)), const((d, d)), win_blk, wout_blk],
        out_specs=[rows_out, rows_out, row, row, win_blk, wout_blk],
        out_shape=[jax.ShapeDtypeStruct((n * d // LANES, LANES), F32),
                   jax.ShapeDtypeStruct((n * d // LANES, LANES), F32),
                   jax.ShapeDtypeStruct((n, d), BF16), jax.ShapeDtypeStruct((n, d), BF16),
                   jax.ShapeDtypeStruct(w_in.shape, BF16), jax.ShapeDtypeStruct(w_out.shape, BF16)],
        compiler_params=pltpu.CompilerParams(
            dimension_semantics=("arbitrary",), vmem_limit_bytes=VMEM_LIMIT),
        name="memory_kv",
    )(mem2d, g_mem, w_xk, w_xv, w_in, w_out)


def _prompt_xattn_kernel(h_ref, g_ref, wq_ref, wo_ref, mk_ref, mv_ref, out_ref, o_s):
    def stages(r0):
        rows = slice(r0, r0 + SUB_ROWS)
        env = {}

        def project():
            env["h"] = h_ref[0, rows, :]
            xn = _rms(env["h"], g_ref[...]).astype(BF16)
            env["q"] = (_dot(xn, wq_ref[...]) * (X_HEAD_DIM ** -0.5)).astype(BF16)

        def scores(hd):
            sl = slice(hd * X_HEAD_DIM, (hd + 1) * X_HEAD_DIM)
            env[hd] = _dot_nt(env["q"][:, sl], mk_ref[0, :, sl])

        def head(hd):
            if hd + 1 < N_X_HEADS:
                scores(hd + 1)
            sl = slice(hd * X_HEAD_DIM, (hd + 1) * X_HEAD_DIM)
            s = env.pop(hd)
            m = jnp.max(s, axis=-1, keepdims=True)
            p = jnp.exp(s - m)
            p = p * (1.0 / jnp.sum(p, axis=-1, keepdims=True))
            o_s[rows, sl] = _dot(p.astype(BF16), mv_ref[0, :, sl]).astype(BF16)

        def output():
            out_ref[0, rows, :] = env["h"] + _dot(o_s[rows, :], wo_ref[...])

        def project_and_first_scores():
            project()
            scores(0)

        return ([project_and_first_scores] + [functools.partial(head, hd) for hd in range(N_X_HEADS)]
                + [output])

    chains = [stages(r0) for r0 in range(0, h_ref.shape[1], SUB_ROWS)]
    n_stage = len(chains[0])
    for step in range(n_stage + len(chains) - 1):
        for lag, chain in enumerate(chains):
            if 0 <= step - lag < n_stage:
                chain[step - lag]()


def _prompt_xattn(h, g, w_xq, w_xo, mkb, mvb):
    b, s, d = h.shape
    tm = TM_X
    const = lambda shape: pl.BlockSpec(shape, lambda i, j: (0,) * len(shape))
    tok = pl.BlockSpec((1, tm, d), lambda i, j: (i, j, 0))
    mem = pl.BlockSpec((1, N_MEM, d), lambda i, j: (i, 0, 0))
    return pl.pallas_call(
        _prompt_xattn_kernel,
        grid=(b, s // tm),
        in_specs=[tok, const((1, d)), const((d, d)), const((d, d)), mem, mem],
        out_specs=tok,
        out_shape=jax.ShapeDtypeStruct((b, s, d), F32),
        scratch_shapes=[pltpu.VMEM((tm, d), BF16)],
        compiler_params=pltpu.CompilerParams(
            dimension_semantics=("arbitrary", "arbitrary"), vmem_limit_bytes=VMEM_LIMIT),
        name="prompt_xattn",
    )(h, g, w_xq, w_xo, mkb, mvb)


def _sample_mixer_kernel(sinks_ref, x_ref, gmix_ref, win_ref, wout_ref, gng_ref, gnb_ref,
                         ck_ref, cv_ref, st_ref, bias_ref, dec_ref, xi_ref, zeta_ref,
                         h_ref, swk_ref, swv_ref, sst_ref):
    bb = ck_ref.shape[0]
    nt = bb // 2
    x = x_ref[...].reshape(bb * DEC_SEQ, D_MODEL)
    xn = _rms(x, gmix_ref[...]).astype(BF16)
    tile3 = lambda a: a.reshape(nt, SUBLANES, a.shape[-1])

    q = _dot(xn, win_ref[:, C_QA:C_QA + ATT_Q_W]) * (HEAD_DIM ** -0.5)
    kv = _dot(xn, win_ref[:, C_KV:C_KV + 2 * ATT_KV_W])
    qkr = _dot(xn, win_ref[:, C_QKR:C_QKR + 2 * RET_QK_W])
    vr = _dot(xn, win_ref[:, C_VR:C_VR + RET_V_W])
    gate3 = tile3(_silu(_dot(xn, win_ref[:, C_GR:C_GR + RET_V_W])))

    lo512, hi512 = _half_masks(ATT_Q_W)
    q_r = pltpu.roll(q, HALF, axis=1)
    q_nat3 = tile3(q)
    q_rot3 = tile3(q_r)
    lo3 = lo512.reshape(1, 1, ATT_Q_W)
    hi3 = hi512.reshape(1, 1, ATT_Q_W)
    qa3 = (q_nat3 * lo3).astype(BF16)
    qb3 = (q_rot3 * lo3).astype(BF16)
    qc3 = (q_rot3 * hi3).astype(BF16)
    qd3 = (q_nat3 * hi3).astype(BF16)
    t128 = lambda a, i: a[:, :, i * LANES:(i + 1) * LANES]
    qs = jnp.concatenate([t128(qa3, 0), t128(qb3, 1), t128(qa3, 1), t128(qb3, 2),
                          t128(qc3, 2), t128(qd3, 2), t128(qc3, 3), t128(qd3, 3)], axis=1)

    k3 = tile3(kv[:, :ATT_KV_W])
    v3 = tile3(kv[:, ATT_KV_W:])
    pad_kv = jnp.zeros((nt, BLK - SUBLANES, LANES), BF16)
    knew_pad = jnp.concatenate([k3.astype(BF16), pad_kv], axis=1)
    vnew_pad = jnp.concatenate([v3.astype(BF16), pad_kv], axis=1)
    to_lanes = lambda a3: jnp.swapaxes(
        jnp.concatenate([a3, jnp.zeros((nt, BLK - SUBLANES, LANES), F32)], axis=1), 1, 2)
    k3t, v3t = to_lanes(k3), to_lanes(v3)
    roll3 = lambda a, sh: pltpu.roll(a.reshape(nt * BLK, LANES), sh, axis=1).reshape(nt, BLK, LANES)

    lo256, _ = _half_masks(RET_QK_W)
    qr3 = tile3(qkr[:, :RET_QK_W])
    kr3 = tile3(qkr[:, RET_QK_W:] * (RET_QK_DIM ** -0.5))
    vr3 = tile3(vr)
    lane256 = lax.broadcasted_iota(jnp.int32, (1, 1, RET_QK_W), 2)
    qrs = jnp.concatenate(
        [(qr3 * ((lane256 >= h * RET_QK_DIM) & (lane256 < (h + 1) * RET_QK_DIM)).astype(F32)).astype(BF16)
         for h in range(N_RET_HEADS)],
        axis=1)
    kr_pad = jnp.concatenate([kr3.astype(BF16), jnp.zeros((nt, BLK - SUBLANES, RET_QK_W), BF16)], axis=1)
    vr_pad = jnp.concatenate([vr3.astype(BF16), jnp.zeros((nt, BLK - SUBLANES, RET_V_W), BF16)], axis=1)

    lane = lax.broadcasted_iota(jnp.int32, (1, 1, LANES), 2)
    row8 = lax.broadcasted_iota(jnp.int32, (1, SUBLANES, 1), 1)
    bmm_nt = lambda a, b: jnp.einsum('bqd,bkd->bqk', a, b, preferred_element_type=F32)
    bmm = lambda a, b: jnp.einsum('bqk,bkd->bqd', a, b, preferred_element_type=F32)

    att_par, ret_par = [], []
    for par in range(2):
        bsl = pl.ds(par, nt, stride=2)
        ckt = ck_ref[bsl]
        cvt = cv_ref[bsl]
        keep = lane < WINDOW - DEC_SEQ
        new_shift = WINDOW - DEC_SEQ - DEC_SEQ * par
        swk_ref[bsl] = jnp.where(keep, roll3(ckt, WINDOW - DEC_SEQ), roll3(k3t, new_shift))
        swv_ref[bsl] = jnp.where(keep, roll3(cvt, WINDOW - DEC_SEQ), roll3(v3t, new_shift))

        s = jnp.concatenate([bmm(qs, ckt.astype(BF16)), bmm_nt(qs, knew_pad)], axis=2) + bias_ref[par]
        ps = []
        for h in range(N_ATT_HEADS):
            ps.append(_sink_softmax(s[:, h * SUBLANES:(h + 1) * SUBLANES, :], sinks_ref[h]).astype(BF16))
        p_all = jnp.concatenate(ps, axis=1)
        o = bmm_nt(p_all[:, :, :BLK], cvt.astype(BF16)) + bmm(p_all[:, :, BLK:], vnew_pad)
        o_r = pltpu.roll(o.reshape(nt * N_ATT_HEADS * SUBLANES, LANES), HALF, axis=1).reshape(o.shape)
        hr = lambda a, h: a[:, h * SUBLANES:(h + 1) * SUBLANES, :]
        low = lane < HALF
        att_par.append(jnp.concatenate([
            jnp.where(low, hr(o, 0), hr(o_r, 1)), jnp.where(low, hr(o, 2), hr(o_r, 3)),
            jnp.where(low, hr(o_r, 4), hr(o, 5)), jnp.where(low, hr(o_r, 6), hr(o, 7))], axis=2))

        st = st_ref[bsl]
        oc = bmm(qrs, st.astype(BF16))
        inner = (bmm_nt(qrs, kr_pad) * dec_ref[par]).astype(BF16)
        oi = bmm(inner, vr_pad)
        rs = []
        for h in range(N_RET_HEADS):
            vsl = slice(h * RET_V_DIM, (h + 1) * RET_V_DIM)
            rsl = slice(h * SUBLANES, (h + 1) * SUBLANES)
            o_h = oi[:, rsl, vsl] + oc[:, rsl, :] * xi_ref[par, rsl, :]
            rs.append(_group_norm(o_h, gng_ref[:, vsl], gnb_ref[:, vsl]) * gate3[:, :, vsl])
        ret_par.append(jnp.concatenate(rs, axis=2))

        kz3 = (kr3 * zeta_ref[par]).astype(BF16)
        vr3_b = vr3.astype(BF16)
        for p in range(nt):
            for i in range(N_RET_HEADS // 2):
                u = _dot_tn(kz3[p][:, i * LANES:(i + 1) * LANES],
                            vr3_b[p][:, 2 * i * RET_V_DIM:(2 * i + 2) * RET_V_DIM])
                for half in range(2):
                    h = 2 * i + half
                    dsl = slice(h * RET_QK_DIM, (h + 1) * RET_QK_DIM)
                    sst_ref[2 * p + par, dsl, :] = (
                        _GL_SAMPLE[h] * st[p, dsl, :]
                        + u[half * RET_QK_DIM:(half + 1) * RET_QK_DIM, half * RET_V_DIM:(half + 1) * RET_V_DIM])

    own0 = row8 < DEC_SEQ
    att3 = jnp.where(own0, att_par[0], att_par[1])
    ret3 = jnp.where(own0, ret_par[0], ret_par[1])
    mix = jnp.concatenate([att3, ret3], axis=2).reshape(2 * nt * DEC_SEQ, MIX_OUT).astype(BF16)
    h_ref[...] = x + _dot(mix, wout_ref[...])


def _sample_mixer(x3d, g_mix, w_in, w_out, sinks, gn_g, gn_b, ck, cv, st):
    nb, ls, d = x3d.shape
    n = nb * ls
    bb = BB_MIX
    r = bb * DEC_SEQ
    const = lambda shape: pl.BlockSpec(shape, lambda i: (0,) * len(shape))
    row = pl.BlockSpec((r, d), lambda i: (i, 0))
    win = pl.BlockSpec((bb, WINDOW, ATT_KV_W), lambda i: (i, 0, 0))
    state = pl.BlockSpec((bb, RET_QK_W, RET_V_DIM), lambda i: (i, 0, 0))
    return pl.pallas_call(
        _sample_mixer_kernel,
        grid=(nb // bb,),
        in_specs=[
            pl.BlockSpec(memory_space=pltpu.SMEM),
            pl.BlockSpec((bb, ls, d), lambda i: (i, 0, 0)), const((1, d)), const((d, D_IN)), const((MIX_OUT, d)),
            const((1, RET_V_W)), const((1, RET_V_W)),
            win, win, state,
            const(_S_BIAS.shape), const(_S_DEC.shape), const(_S_XI.shape), const(_S_ZETA.shape),
        ],
        out_specs=[row, win, win, state],
        out_shape=[
            jax.ShapeDtypeStruct((n, d), F32),
            jax.ShapeDtypeStruct((nb, WINDOW, ATT_KV_W), F32),
            jax.ShapeDtypeStruct((nb, WINDOW, ATT_KV_W), F32),
            jax.ShapeDtypeStruct((nb, RET_QK_W, RET_V_DIM), F32),
        ],
        compiler_params=pltpu.CompilerParams(
            dimension_semantics=("arbitrary",), vmem_limit_bytes=VMEM_LIMIT),
        name="sample_mixer",
    )(sinks, x3d, g_mix, w_in, w_out, gn_g, gn_b, ck, cv, st,
      jnp.asarray(_S_BIAS), jnp.asarray(_S_DEC), jnp.asarray(_S_XI), jnp.asarray(_S_ZETA))


def _head_slab(x_ref, b, hd):
    group = X_D_HALVES * N_X_HEADS
    halves = [x_ref[b, pl.ds(dh * N_X_HEADS + hd, N_MEM, stride=group), :] for dh in range(X_D_HALVES)]
    return jnp.concatenate(halves, axis=1).astype(BF16)


def _mlp_value(h, g_ref, wup_ref, wdn_ref, gf_ref, fillers=None):
    xn = _rms(h, g_ref[...]).astype(BF16)
    piece = FF_CHUNK // N_X_HEADS
    opiece = D_MODEL // N_X_HEADS
    n_chunks = D_FF // FF_CHUNK
    nofill = (None, None, None)

    def up(c):
        qk, softmax, _ = fillers[c] if fillers is not None else nofill
        hid = []
        for k in range(N_X_HEADS):
            cols = slice(c * FF_CHUNK + k * piece, c * FF_CHUNK + (k + 1) * piece)
            u = jnp.maximum(_dot(xn, wup_ref[:, cols]), 0.0)
            hid.append((u * u).astype(BF16))
            if qk is not None:
                qk(k)
        if softmax is not None:
            softmax()
        return jnp.concatenate(hid, axis=1)

    def down(c, hid):
        pv = (fillers[c] if fillers is not None else nofill)[2]
        rows_c = slice(c * FF_CHUNK, (c + 1) * FF_CHUNK)
        out = []
        for k in range(N_X_HEADS):
            out.append(_dot(hid, wdn_ref[rows_c, k * opiece:(k + 1) * opiece]))
            if pv is not None:
                pv(k)
        return jnp.concatenate(out, axis=1)

    acc = h
    hid = up(0)
    for c in range(n_chunks):
        nxt = up(c + 1) if c + 1 < n_chunks else None
        acc = acc + down(c, hid)
        hid = nxt
    return _rms(acc, gf_ref[...])


def _mlp_xattn_kernel(hp_ref, hsm_ref, gx_ref, wq_ref, wo_ref, xk_ref, xv_ref, g_ref, wup_ref, wdn_ref, gf_ref,
                      yp_ref, ys_ref):
    i = pl.program_id(0)
    n = pl.num_programs(0) - 1
    bb = xk_ref.shape[0]
    assert bb == D_FF // FF_CHUNK and bb % 2 == 0

    @pl.when(i == 0)
    def _():
        xn = _rms(hsm_ref[...], gx_ref[...]).astype(BF16)
        ys_ref[...] = (_dot(xn, wq_ref[...]) * (X_HEAD_DIM ** -0.5)).reshape(ys_ref.shape)

    @pl.when(i < n)
    def _():
        own0 = lax.broadcasted_iota(jnp.int32, (SUBLANES, 1), 0) < DEC_SEQ
        o_rows = {}

        def attend(b):
            t = b // 2
            tile_b = pl.ds(i * bb + 2 * t, 2)
            env = dict(s=[], o=[])

            def qk(hd):
                if hd == 0:
                    env["q"] = ys_ref[tile_b].reshape(SUBLANES, D_MODEL).astype(BF16)
                env["s"].append(_dot_nt(env["q"][:, hd * X_HEAD_DIM:(hd + 1) * X_HEAD_DIM],
                                        _head_slab(xk_ref, b, hd)))

            def softmax():
                s = jnp.concatenate(env["s"], axis=0)
                m = jnp.max(s, axis=-1, keepdims=True)
                p = jnp.exp(s - m)
                env["p"] = p * (1.0 / jnp.sum(p, axis=-1, keepdims=True))

            def pv(hd):
                p = env["p"][hd * SUBLANES:(hd + 1) * SUBLANES].astype(BF16)
                env["o"].append(_dot(p, _head_slab(xv_ref, b, hd)))
                if hd == N_X_HEADS - 1:
                    o_rows[b] = jnp.concatenate(env["o"], axis=1)
                    if b % 2 == 1:
                        ys_ref[tile_b] = jnp.where(own0, o_rows[b - 1], o_rows[b]).reshape(2, DEC_SEQ, D_MODEL)

            return qk, softmax, pv

        fillers = [attend(b) for b in range(bb)]
        yp_ref[...] = _mlp_value(hp_ref[...], g_ref, wup_ref, wdn_ref, gf_ref, fillers)

    @pl.when(i == n)
    def _():
        o = ys_ref[...].reshape(hsm_ref.shape).astype(BF16)
        hs = hsm_ref[...] + _dot(o, wo_ref[...])
        ys_ref[...] = _mlp_value(hs, g_ref, wup_ref, wdn_ref, gf_ref).reshape(ys_ref.shape)


def _mlp_xattn(hp2d, hsm, g_xattn, w_xq, w_xo, xk, xv, g_mlp, w_up, w_down, g_final):
    n, d = hp2d.shape
    ns = hsm.shape[0]
    nb = xk.shape[0]
    bb = BB_X
    tm = n // (nb // bb)
    n_tiles = n // tm
    assert n_tiles * bb == nb and tm % SUBLANES == 0
    clip = lambda i: jnp.minimum(i, n_tiles - 1)
    prompt = pl.BlockSpec((tm, d), lambda i: (clip(i), 0))
    mem = pl.BlockSpec((bb,) + xk.shape[1:], lambda i: (clip(i), 0, 0))
    const = lambda shape: pl.BlockSpec(shape, lambda i: (0,) * len(shape), pipeline_mode=pl.Buffered(1))
    return pl.pallas_call(
        _mlp_xattn_kernel,
        grid=(n_tiles + 1,),
        in_specs=[prompt, const((ns, d)), const((1, d)), const((d, d)), const((d, d)), mem, mem,
                  const((1, d)), const((d, D_FF)), const((D_FF, d)), const((1, d))],
        out_specs=[prompt, pl.BlockSpec((ns // DEC_SEQ, DEC_SEQ, d), lambda i: (0, 0, 0))],
        out_shape=[jax.ShapeDtypeStruct((n, d), F32), jax.ShapeDtypeStruct((ns // DEC_SEQ, DEC_SEQ, d), F32)],
        compiler_params=pltpu.CompilerParams(
            dimension_semantics=("arbitrary",), vmem_limit_bytes=VMEM_LIMIT),
        name="mlp_xattn",
    )(hp2d, hsm, g_xattn, w_xq, w_xo, xk, xv, g_mlp, w_up, w_down, g_final)


def _mem_rows(c):
    nb = c.shape[0]
    c = c.reshape(nb, N_MEM, N_X_HEADS, X_D_HALVES, LANES)
    return jnp.transpose(c, (0, 1, 3, 2, 4)).reshape(nb, N_MEM * X_D_HALVES * N_X_HEADS, LANES)


def kernel(x_prompt, x_sample, mem_prompt, cache_win_k, cache_win_v, state_ret, cache_mem_k, cache_mem_v,
           g_mix, w_in, attn_sinks, ret_gn_g, ret_gn_b, w_out, g_xattn, g_mem, w_xq, w_xk, w_xv, w_xo,
           g_mlp, w_up, w_down, g_final):
    depth = w_in.shape[0]
    assert depth == 1, "single-layer trunk"
    b, s, d = x_prompt.shape
    nb, ls, _ = x_sample.shape
    row = lambda a: a.reshape(1, -1)
    sinks = attn_sinks[0]
    gn_g, gn_b = row(ret_gn_g[0]), row(ret_gn_b[0])
    g_fin = row(g_final)

    mk, mv, mkb, mvb, w_in_b, w_out_b = _memory_kv(
        mem_prompt.reshape(b * N_MEM, d), row(g_mem[0]), w_xk[0], w_xv[0], w_in[0], w_out[0])
    w_up_b, w_dn_b = _sc_cast_bf16(w_up[0]), _sc_cast_bf16(w_down[0])
    hp, p_wk, p_wv, p_rs, w_xq_b, w_xo_b = _prompt_mixer(
        x_prompt, row(g_mix[0]), w_in_b, w_out_b, sinks, gn_g, gn_b, (w_xq[0], w_xo[0]))
    hp = _prompt_xattn(hp, row(g_xattn[0]), w_xq_b, w_xo_b,
                       mkb.reshape(b, N_MEM, d), mvb.reshape(b, N_MEM, d))

    win_t = lambda c: jnp.transpose(c, (0, 2, 3, 1)).reshape(nb, ATT_KV_W, WINDOW)
    win_t_inv = lambda a: jnp.transpose(a.reshape(-1, N_KV_HEADS, HEAD_DIM, WINDOW),
                                        (0, 3, 1, 2)).reshape(1, -1, WINDOW, N_KV_HEADS, HEAD_DIM)
    hs, s_wk, s_wv, s_rs = _sample_mixer(
        x_sample, row(g_mix[0]), w_in_b, w_out_b, sinks, gn_g, gn_b,
        win_t(cache_win_k[0]), win_t(cache_win_v[0]), state_ret[0].reshape(nb, RET_QK_W, RET_V_DIM))

    y_prompt, y_sample = _mlp_xattn(
        hp.reshape(b * s, d), hs, row(g_xattn[0]), w_xq_b, w_xo_b,
        _mem_rows(cache_mem_k[0]), _mem_rows(cache_mem_v[0]), row(g_mlp[0]), w_up_b, w_dn_b, g_fin)
    y_prompt = y_prompt.reshape(b, s, d)

    ret5 = lambda a, n: a.reshape(1, n, N_RET_HEADS, RET_QK_DIM, RET_V_DIM)
    mem5 = lambda a: jnp.transpose(a.reshape(b, N_MEM, X_D_HALVES, N_X_HEADS, LANES),
                                   (0, 1, 3, 2, 4)).reshape(1, b, N_MEM, N_X_HEADS, X_HEAD_DIM)
    return (y_prompt, y_sample,
            win_t_inv(p_wk), win_t_inv(p_wv), ret5(p_rs, b), mem5(mk), mem5(mv),
            win_t_inv(s_wk), win_t_inv(s_wv), ret5(s_rs, nb))
```

```python
import functools

import jax
import jax.numpy as jnp
import numpy as np
from jax import lax
from jax.experimental import pallas as pl
from jax.experimental.pallas import tpu as pltpu

F32 = jnp.float32
BF16 = jnp.bfloat16

D_MODEL = 1024
BATCH = 8
SEQ = 2048
DEC_BATCH = 128
DEC_SEQ = 4
HEAD_DIM = 64
N_ATT_HEADS = 8
N_KV_HEADS = 2
KV_GROUP = N_ATT_HEADS // N_KV_HEADS
WINDOW = 128
BLK = 128
N_RET_HEADS = 4
RET_QK_DIM = 64
RET_V_DIM = 128
N_MEM = 256
N_X_HEADS = 4
X_HEAD_DIM = D_MODEL // N_X_HEADS
D_FF = 4 * D_MODEL
RMS_EPS = 1e-6
GN_EPS = 1e-5

ATT_Q_W = N_ATT_HEADS * HEAD_DIM
ATT_KV_W = N_KV_HEADS * HEAD_DIM
RET_QK_W = N_RET_HEADS * RET_QK_DIM
RET_V_W = N_RET_HEADS * RET_V_DIM
MIX_OUT = ATT_Q_W + RET_V_W
D_IN = ATT_Q_W + 2 * ATT_KV_W + 2 * RET_QK_W + 2 * RET_V_W
C_QA, C_KV, C_QKR, C_VR, C_GR = 0, 512, 768, 1280, 1792

LANES = 128
SUBLANES = 8
HALF = LANES // 2
X_D_HALVES = X_HEAD_DIM // LANES
NEG = -1e30
VMEM_LIMIT = 56 * 1024 * 1024

TM_MIX = 512
TM_X = 2048
SUB_ROWS = 512
FF_CHUNK = 1024
BB_MIX = 32
N_RING = 3
BB_X = 4

NEG_SLOPES = [-(2.0 ** (-8.0 * (i + 1) / N_ATT_HEADS)) for i in range(N_ATT_HEADS)]
_LOG_G = np.log(1.0 - 2.0 ** (-5.0 - np.arange(N_RET_HEADS))).astype(np.float32).astype(np.float64)


def _prompt_tables():
    qi = np.arange(BLK)[:, None]
    kj = np.arange(2 * BLK)[None, :]
    dist = (qi + BLK - kj).astype(np.float64)
    mask = np.where((dist >= 0) & (dist < WINDOW), 0.0, NEG)
    l = np.arange(BLK, dtype=np.float64)
    diff = l[:, None] - l[None, :]
    decay = np.where(diff >= 0, np.exp(_LOG_G[:, None, None] * np.maximum(diff, 0.0)), 0.0)
    xi = np.exp((l[:, None] + 1.0) * _LOG_G[None, :])
    zeta = np.exp((BLK - 1.0 - l)[:, None] * _LOG_G[None, :])
    xi_t = np.repeat(xi, RET_V_DIM, axis=1)
    zeta_t = np.repeat(zeta, RET_QK_DIM, axis=1)
    f = lambda a: np.asarray(a, np.float32)
    return f(dist), f(mask), f(decay), f(xi_t), f(zeta_t)


def _sample_tables():
    slopes = -np.asarray(NEG_SLOPES)
    bias = np.full((2, N_ATT_HEADS * SUBLANES, 2 * BLK), NEG, np.float64)
    dec = np.zeros((2, N_RET_HEADS * SUBLANES, BLK), np.float64)
    xi = np.zeros((2, N_RET_HEADS * SUBLANES, RET_V_DIM), np.float64)
    zeta = np.zeros((2, SUBLANES, RET_QK_W), np.float64)
    for par in range(2):
        for r in range(SUBLANES):
            own = DEC_SEQ * par <= r < DEC_SEQ * (par + 1)
            t = r - DEC_SEQ * par if own else r % DEC_SEQ
            for h in range(N_ATT_HEADS):
                row = h * SUBLANES + r
                for j in range(WINDOW):
                    d = t + WINDOW - j
                    if 0 <= d < WINDOW:
                        bias[par, row, j] = -slopes[h] * d
                for c in range(DEC_SEQ):
                    d = t - c
                    if d >= 0:
                        bias[par, row, WINDOW + DEC_SEQ * par + c] = -slopes[h] * d
            for h in range(N_RET_HEADS):
                row = h * SUBLANES + r
                if own:
                    xi[par, row, :] = np.exp((t + 1.0) * _LOG_G[h])
                    zeta[par, r, h * RET_QK_DIM:(h + 1) * RET_QK_DIM] = np.exp((DEC_SEQ - 1.0 - t) * _LOG_G[h])
                    for c in range(t + 1):
                        dec[par, row, DEC_SEQ * par + c] = np.exp(_LOG_G[h] * (t - c))
    f = lambda a: np.asarray(a, np.float32)
    return f(bias), f(dec), f(xi), f(zeta)


_P_DIST, _P_MASK, _P_DECAY, _P_XI, _P_ZETA = _prompt_tables()
_S_BIAS, _S_DEC, _S_XI, _S_ZETA = _sample_tables()
_GL_PROMPT = [float(np.exp(_LOG_G[h] * BLK)) for h in range(N_RET_HEADS)]
_GL_SAMPLE = [float(np.exp(_LOG_G[h] * DEC_SEQ)) for h in range(N_RET_HEADS)]


def _rms(x, g):
    return x * lax.rsqrt(jnp.mean(x * x, axis=-1, keepdims=True) + RMS_EPS) * g


def _dot(a, b):
    return jnp.dot(a, b, preferred_element_type=F32)


def _dot_nt(a, b):
    return lax.dot_general(a, b, (((1,), (1,)), ((), ())), preferred_element_type=F32)


def _dot_tn(a, b):
    return lax.dot_general(a, b, (((0,), (0,)), ((), ())), preferred_element_type=F32)


def _silu(g):
    return g * (1.0 / (1.0 + jnp.exp(-g)))


def _half_masks(width):
    lane = lax.broadcasted_iota(jnp.int32, (1, width), 1)
    lo = ((lane & (LANES - 1)) < HALF).astype(F32)
    return lo, 1.0 - lo


def _sink_softmax(s, sink):
    m = jnp.maximum(jnp.max(s, axis=-1, keepdims=True), sink)
    p = jnp.exp(s - m)
    den = jnp.sum(p, axis=-1, keepdims=True) + jnp.exp(sink - m)
    return p * (1.0 / den)


def _group_norm(o, g, b):
    mu = jnp.mean(o, axis=-1, keepdims=True)
    d = o - mu
    var = jnp.mean(d * d, axis=-1, keepdims=True)
    return d * lax.rsqrt(var + GN_EPS) * g + b


def _pm_project_stages(x, slot, gmix_ref, win_ref, sc, kv_out=None):
    tm = x.shape[0]
    xn = _rms(x, gmix_ref[...]).astype(BF16)

    pw = 2 * LANES
    lo, hi = _half_masks(pw)

    def stage_q(i):
        cols = slice(i * pw, (i + 1) * pw)
        q = _dot(xn, win_ref[:, C_QA + i * pw:C_QA + (i + 1) * pw])
        sc["qlo"][slot, :, cols] = (q * (lo * HEAD_DIM ** -0.5)).astype(BF16)
        sc["qhi"][slot, :, cols] = (q * (hi * HEAD_DIM ** -0.5)).astype(BF16)

    def stage_kv():
        z = _dot(xn, win_ref[:, C_KV:C_KV + pw])
        low = lax.broadcasted_iota(jnp.int32, (tm, LANES), 1) < HALF
        k = z[:, 0:ATT_KV_W]
        v = z[:, ATT_KV_W:2 * ATT_KV_W]
        if kv_out is not None:
            kv_out[0][0] = k[tm - WINDOW:, :].T
            kv_out[1][0] = v[tm - WINDOW:, :].T
        k_r = pltpu.roll(k, HALF, axis=1)
        v_r = pltpu.roll(v, HALF, axis=1)
        sc["kd0"][slot] = jnp.where(low, k, k_r).astype(BF16)
        sc["kd1"][slot] = jnp.where(low, k_r, k).astype(BF16)
        sc["vd0"][slot, :, 0:LANES] = jnp.where(low, v, 1.0).astype(BF16)
        sc["vd0"][slot, :, LANES:2 * LANES] = jnp.where(low, 1.0, v_r).astype(BF16)
        sc["vd1"][slot, :, 0:LANES] = jnp.where(low, v_r, 1.0).astype(BF16)
        sc["vd1"][slot, :, LANES:2 * LANES] = jnp.where(low, 1.0, v).astype(BF16)

    def stage_qr():
        qr = _dot(xn, win_ref[:, C_QKR:C_QKR + pw])
        sc["qrlo"][slot] = (qr * lo).astype(BF16)
        sc["qrhi"][slot] = (qr * hi).astype(BF16)

    def stage_kr():
        sc["kr"][slot] = _dot(xn, win_ref[:, C_QKR + pw:C_QKR + 2 * pw]) * (RET_QK_DIM ** -0.5)

    def stage_vr(i):
        cols = slice(i * pw, (i + 1) * pw)
        sc["vr"][slot, :, cols] = _dot(xn, win_ref[:, C_VR + i * pw:C_VR + (i + 1) * pw]).astype(BF16)

    def stage_gate(i):
        cols = slice(i * pw, (i + 1) * pw)
        sc["gate"][slot, :, cols] = _silu(_dot(xn, win_ref[:, C_GR + i * pw:C_GR + (i + 1) * pw]))

    part = functools.partial
    return [part(stage_q, 0), part(stage_q, 1), stage_kv, stage_qr, stage_kr,
            part(stage_vr, 0), part(stage_vr, 1), part(stage_gate, 0), part(stage_gate, 1)]


def _pm_last_block(slot, tm, sc):
    rows = slice(tm - BLK, tm)
    return ([sc["kd0"][slot, rows, :], sc["kd1"][slot, rows, :]],
            [sc["vd0"][slot, rows, :], sc["vd1"][slot, rows, :]])


def _pm_blocks(slot, prev_kd, prev_vd, is_first, state, fillers, tm, sinks_ref, gng_ref, gnb_ref,
               decay_ref, xi_ref, zeta_ref, sc):
    nblk = tm // BLK
    n_units = nblk * (N_KV_HEADS + N_RET_HEADS // 2)
    pending = list(fillers)
    done_units = [0]

    def unit_done():
        done_units[0] += 1
        while pending and (len(fillers) - len(pending)) * n_units < done_units[0] * len(fillers):
            pending.pop(0)()
    lowb = lax.broadcasted_iota(jnp.int32, (BLK, LANES), 1) < HALF
    col = lax.broadcasted_iota(jnp.int32, (BLK, 2 * BLK), 1)
    first_mask = None if is_first is False else jnp.where((col < BLK) & is_first, NEG, 0.0)
    kd_refs = (sc["kd0"], sc["kd1"])
    vd_refs = (sc["vd0"], sc["vd1"])
    qlo, qhi, mix = sc["qlo"], sc["qhi"], sc["mix"]
    n_pairs = N_RET_HEADS // 2

    for j in range(nblk):
        rows = slice(j * BLK, (j + 1) * BLK)
        c0s = [kvh * KV_GROUP * HEAD_DIM for kvh in range(N_KV_HEADS)]
        lsls = [slice(i * LANES, (i + 1) * LANES) for i in range(n_pairs)]
        vds, scores = [], []
        for kvh in range(N_KV_HEADS):
            if j == 0:
                kd = jnp.concatenate([prev_kd[kvh], kd_refs[kvh][slot, rows, :]], axis=0)
                vds.append(jnp.concatenate([prev_vd[kvh], vd_refs[kvh][slot, rows, :]], axis=0))
            else:
                krows = slice((j - 1) * BLK, (j + 1) * BLK)
                kd = kd_refs[kvh][slot, krows, :]
                vds.append(vd_refs[kvh][slot, krows, :])
            c0 = c0s[kvh]
            qst = jnp.concatenate([qlo[slot, rows, c0:c0 + LANES], qhi[slot, rows, c0:c0 + LANES],
                                   qlo[slot, rows, c0 + LANES:c0 + 2 * LANES],
                                   qhi[slot, rows, c0 + LANES:c0 + 2 * LANES]], axis=0)
            scores.append(_dot_nt(qst, kd))
        unit_done()

        kps = [sc["kr"][slot, rows, lsls[i]] for i in range(n_pairs)]
        vpairs = [sc["vr"][slot, rows, 2 * i * RET_V_DIM:(2 * i + 2) * RET_V_DIM] for i in range(n_pairs)]
        q2s = [jnp.concatenate([sc["qrlo"][slot, rows, lsls[i]], sc["qrhi"][slot, rows, lsls[i]]], axis=0)
               for i in range(n_pairs)]
        a_s = [_dot_nt(q2s[i], kps[i].astype(BF16)) for i in range(n_pairs)]
        ocs = [_dot(q2s[i], state[i].astype(BF16)) for i in range(n_pairs)]
        us = [_dot_tn((kps[i] * zeta_ref[:, lsls[i]]).astype(BF16), vpairs[i]) for i in range(n_pairs)]
        unit_done()

        for kvh in range(N_KV_HEADS):
            s, vd, c0 = scores[kvh], vds[kvh], c0s[kvh]
            es, esink = [], []
            for g in range(KV_GROUP):
                h = kvh * KV_GROUP + g
                sg = s[g * BLK:(g + 1) * BLK] + sc["bias"][h]
                if j == 0 and first_mask is not None:
                    sg = sg + first_mask
                sink = sinks_ref[h]
                m = jnp.maximum(jnp.max(sg, axis=-1, keepdims=True), sink)
                es.append(jnp.exp(sg - m).astype(BF16))
                esink.append(jnp.exp(sink - m))
            o = _dot(jnp.concatenate(es, axis=0), vd)
            for pair in range(KV_GROUP // 2):
                oe = o[2 * pair * BLK:(2 * pair + 1) * BLK]
                oo = o[(2 * pair + 1) * BLK:(2 * pair + 2) * BLK]
                num = jnp.where(lowb, oe[:, :LANES], oo[:, LANES:])
                den = (jnp.where(lowb, oe[:, LANES:], oo[:, :LANES])
                       + jnp.where(lowb, esink[2 * pair], esink[2 * pair + 1]))
                cs = c0 + pair * LANES
                mix[slot, rows, cs:cs + LANES] = (num * (1.0 / den)).astype(BF16)
            unit_done()

        for i in range(n_pairs):
            a, oc, u, sp = a_s[i], ocs[i], us[i], state[i]
            inner = jnp.concatenate([a[:BLK] * decay_ref[2 * i], a[BLK:] * decay_ref[2 * i + 1]], axis=0)
            oi = _dot(inner.astype(BF16), vpairs[i])
            for half in range(2):
                h = 2 * i + half
                vsl = slice(h * RET_V_DIM, (h + 1) * RET_V_DIM)
                hr = slice(half * BLK, (half + 1) * BLK)
                o = oi[hr, half * RET_V_DIM:(half + 1) * RET_V_DIM] + oc[hr] * xi_ref[:, vsl]
                r = _group_norm(o, gng_ref[:, vsl], gnb_ref[:, vsl]) * sc["gate"][slot, rows, vsl]
                mix[slot, rows, ATT_Q_W + h * RET_V_DIM:ATT_Q_W + (h + 1) * RET_V_DIM] = r.astype(BF16)
            state[i] = jnp.concatenate(
                [_GL_PROMPT[2 * i] * sp[:RET_QK_DIM] + u[:RET_QK_DIM, :RET_V_DIM],
                 _GL_PROMPT[2 * i + 1] * sp[RET_QK_DIM:] + u[RET_QK_DIM:, RET_V_DIM:]], axis=0)

    assert not pending
    return state


def _pm_wout_pieces(slot, x_ref, rows, wout_ref, h_ref, sc):
    pw = 2 * LANES
    n = D_MODEL // pw
    parts = []

    def piece(k):
        parts.append(_dot(sc["mix"][slot], wout_ref[:, k * pw:(k + 1) * pw]))
        if k == n - 1:
            h_ref[rows, :] = x_ref[...] + jnp.concatenate(parts, axis=1)

    return [functools.partial(piece, k) for k in range(n)]


def _prompt_mixer_kernel(sinks_ref, xfirst_ref, xodd_ref, xnext_ref, gmix_ref, win_ref, wout_ref, gng_ref, gnb_ref,
                           dist_ref, mask_ref, decay_ref, xi_ref, zeta_ref,
                           wupf_ref, wdnf_ref, wqf_ref, wof_ref,
                           h_ref, wk_ref, wv_ref, st_ref,
                           wupb_ref, wdnb_ref, wqb_ref, wob_ref,
                           qlo_s, qhi_s, kd0_s, kd1_s, vd0_s, vd1_s,
                           qrlo_s, qrhi_s, kr_s, vr_s, gate_s, mix_s, bias_s, state_s, xkeep_s):
    u = pl.program_id(0)
    tm = xnext_ref.shape[0]
    wupb_ref[...] = wupf_ref[...].astype(BF16)
    wdnb_ref[...] = wdnf_ref[...].astype(BF16)
    wqb_ref[...] = wqf_ref[...].astype(BF16)
    wob_ref[...] = wof_ref[...].astype(BF16)
    sc = dict(qlo=qlo_s, qhi=qhi_s, kd0=kd0_s, kd1=kd1_s, vd0=vd0_s, vd1=vd1_s, qrlo=qrlo_s, qrhi=qrhi_s,
              kr=kr_s, vr=vr_s, gate=gate_s, mix=mix_s, bias=bias_s)
    n_pairs = N_RET_HEADS // 2
    blocks = functools.partial(_pm_blocks, tm=tm, sinks_ref=sinks_ref, gng_ref=gng_ref,
                               gnb_ref=gnb_ref, decay_ref=decay_ref, xi_ref=xi_ref, zeta_ref=zeta_ref, sc=sc)

    @pl.when(u == 0)
    def _():
        for h in range(N_ATT_HEADS):
            bias_s[h] = NEG_SLOPES[h] * dist_ref[...] + mask_ref[...]
        state_s[...] = jnp.zeros_like(state_s)
        kd0_s[1] = jnp.zeros(kd0_s.shape[1:], BF16)
        kd1_s[1] = jnp.zeros(kd1_s.shape[1:], BF16)
        vd0_s[1] = jnp.zeros(vd0_s.shape[1:], BF16)
        vd1_s[1] = jnp.zeros(vd1_s.shape[1:], BF16)
        xkeep_s[...] = xfirst_ref[...]
        for stage in _pm_project_stages(xfirst_ref[...], 0, gmix_ref, win_ref, sc):
            stage()

    seq_start = (u % 2) == 0
    state = [jnp.where(seq_start, 0.0, state_s[i * LANES:(i + 1) * LANES, :]) for i in range(n_pairs)]

    prev_kd, prev_vd = _pm_last_block(1, tm, sc)
    stages = _pm_project_stages(xodd_ref[...], 1, gmix_ref, win_ref, sc, kv_out=(wk_ref, wv_ref))
    state = blocks(0, prev_kd, prev_vd, seq_start, state, stages)
    wout0 = _pm_wout_pieces(0, xkeep_s, slice(0, tm), wout_ref, h_ref, sc)

    prev_kd, prev_vd = _pm_last_block(0, tm, sc)
    stages = _pm_project_stages(xnext_ref[...], 0, gmix_ref, win_ref, sc)
    state = blocks(1, prev_kd, prev_vd, False, state, wout0 + stages)
    for piece in _pm_wout_pieces(1, xodd_ref, slice(tm, 2 * tm), wout_ref, h_ref, sc):
        piece()
    xkeep_s[...] = xnext_ref[...]

    for i in range(n_pairs):
        state_s[i * LANES:(i + 1) * LANES, :] = state[i]
        st_ref[0, i * LANES:(i + 1) * LANES, :] = state[i]


def _prompt_mixer(x, g_mix, w_in, w_out, sinks, gn_g, gn_b, side_f32):
    b, s, d = x.shape
    tm = TM_MIX
    n_tiles = b * s // tm
    steps = n_tiles // 2
    seq_steps = s // (2 * tm)
    assert s % (2 * tm) == 0 and seq_steps == 2, "kernel assumes 4 tiles per sequence"
    x2d = x.reshape(b * s, d)
    const = lambda shape: pl.BlockSpec(shape, lambda i: (0,) * len(shape), pipeline_mode=pl.Buffered(1))
    slot2 = lambda rows, cols, dt: pltpu.VMEM((2, rows, cols), dt)
    side_specs = [pl.BlockSpec((w.shape[0] // steps, w.shape[1]), lambda i: (i, 0)) for w in side_f32]
    outs = pl.pallas_call(
        _prompt_mixer_kernel,
        grid=(steps,),
        in_specs=[
            pl.BlockSpec(memory_space=pltpu.SMEM),
            const((tm, d)),
            pl.BlockSpec((tm, d), lambda i: (2 * i + 1, 0)),
            pl.BlockSpec((tm, d), lambda i: (jnp.minimum(2 * i + 2, n_tiles - 1), 0)),
            const((1, d)), const((d, D_IN)), const((MIX_OUT, d)),
            const((1, RET_V_W)), const((1, RET_V_W)),
            const((BLK, 2 * BLK)), const((BLK, 2 * BLK)),
            const((N_RET_HEADS, BLK, BLK)), const((BLK, RET_V_W)), const((BLK, RET_QK_W)),
        ] + side_specs,
        out_specs=[
            pl.BlockSpec((2 * tm, d), lambda i: (i, 0)),
            pl.BlockSpec((1, WINDOW, ATT_KV_W), lambda i: (i // seq_steps, 0, 0)),
            pl.BlockSpec((1, WINDOW, ATT_KV_W), lambda i: (i // seq_steps, 0, 0)),
            pl.BlockSpec((1, RET_QK_W, RET_V_DIM), lambda i: (i // seq_steps, 0, 0)),
        ] + side_specs,
        out_shape=[
            jax.ShapeDtypeStruct((b * s, d), F32),
            jax.ShapeDtypeStruct((b, WINDOW, ATT_KV_W), F32),
            jax.ShapeDtypeStruct((b, WINDOW, ATT_KV_W), F32),
            jax.ShapeDtypeStruct((b, RET_QK_W, RET_V_DIM), F32),
        ] + [jax.ShapeDtypeStruct(w.shape, BF16) for w in side_f32],
        scratch_shapes=[
            slot2(tm, ATT_Q_W, BF16), slot2(tm, ATT_Q_W, BF16),
            slot2(tm, LANES, BF16), slot2(tm, LANES, BF16),
            slot2(tm, 2 * LANES, BF16), slot2(tm, 2 * LANES, BF16),
            slot2(tm, RET_QK_W, BF16), slot2(tm, RET_QK_W, BF16),
            slot2(tm, RET_QK_W, F32), slot2(tm, RET_V_W, BF16),
            slot2(tm, RET_V_W, F32), slot2(tm, MIX_OUT, BF16),
            pltpu.VMEM((N_ATT_HEADS, BLK, 2 * BLK), F32),
            pltpu.VMEM((RET_QK_W, RET_V_DIM), F32),
            pltpu.VMEM((tm, d), F32),
        ],
        compiler_params=pltpu.CompilerParams(
            dimension_semantics=("arbitrary",), vmem_limit_bytes=VMEM_LIMIT),
        name="prompt_mixer",
    )(sinks, x2d, x2d, x2d, g_mix, w_in, w_out, gn_g, gn_b,
      jnp.asarray(_P_DIST), jnp.asarray(_P_MASK), jnp.asarray(_P_DECAY), jnp.asarray(_P_XI),
      jnp.asarray(_P_ZETA), *side_f32)
    return (outs[0].reshape(b, s, d),) + tuple(outs[1:])


def _memkv_kernel(mem_ref, g_ref, wk_ref, wv_ref, win_ref, wout_ref,
                  mk_ref, mv_ref, mkb_ref, mvb_ref, winb_ref, woutb_ref):
    winb_ref[...] = win_ref[...].astype(BF16)
    woutb_ref[...] = wout_ref[...].astype(BF16)
    mn = _rms(mem_ref[...], g_ref[...]).astype(BF16)
    mk = _dot(mn, wk_ref[...].astype(BF16))
    mv = _dot(mn, wv_ref[...].astype(BF16))
    tm = mem_ref.shape[0]
    group = X_D_HALVES * N_X_HEADS
    for hd in range(N_X_HEADS):
        for dh in range(X_D_HALVES):
            cols = slice(hd * X_HEAD_DIM + dh * LANES, hd * X_HEAD_DIM + (dh + 1) * LANES)
            rows = pl.ds(dh * N_X_HEADS + hd, tm, stride=group)
            mk_ref[rows, :] = mk[:, cols]
            mv_ref[rows, :] = mv[:, cols]
    mkb_ref[...] = mk.astype(BF16)
    mvb_ref[...] = mv.astype(BF16)


def _memory_kv(mem2d, g_mem, w_xk, w_xv, w_in, w_out):
    n, d = mem2d.shape
    tm = 512
    row = pl.BlockSpec((tm, d), lambda i: (i, 0))
    rows_out = pl.BlockSpec((tm * d // LANES, LANES), lambda i: (i, 0))
    const = lambda shape: pl.BlockSpec(shape, lambda i: (0,) * len(shape), pipeline_mode=pl.Buffered(1))
    steps = n // tm
    win_blk = pl.BlockSpec((w_in.shape[0] // steps, w_in.shape[1]), lambda i: (i, 0))
    wout_blk = pl.BlockSpec((w_out.shape[0] // steps, w_out.shape[1]), lambda i: (i, 0))
    return pl.pallas_call(
        _memkv_kernel,
        grid=(n // tm,),
        in_specs=[row, const((1, d)), const((d, d)), const((d, d)), win_blk, wout_blk],
        out_specs=[rows_out, rows_out, row, row, win_blk, wout_blk],
        out_shape=[jax.ShapeDtypeStruct((n * d // LANES, LANES), F32),
                   jax.ShapeDtypeStruct((n * d // LANES, LANES), F32),
                   jax.ShapeDtypeStruct((n, d), BF16), jax.ShapeDtypeStruct((n, d), BF16),
                   jax.ShapeDtypeStruct(w_in.shape, BF16), jax.ShapeDtypeStruct(w_out.shape, BF16)],
        compiler_params=pltpu.CompilerParams(
            dimension_semantics=("arbitrary",), vmem_limit_bytes=VMEM_LIMIT),
        name="memory_kv",
    )(mem2d, g_mem, w_xk, w_xv, w_in, w_out)


def _prompt_xattn_kernel(h_ref, g_ref, wq_ref, wo_ref, mk_ref, mv_ref, out_ref, o_s):
    def stages(r0):
        rows = slice(r0, r0 + SUB_ROWS)
        env = {}

        def project():
            env["h"] = h_ref[0, rows, :]
            xn = _rms(env["h"], g_ref[...]).astype(BF16)
            env["q"] = (_dot(xn, wq_ref[...]) * (X_HEAD_DIM ** -0.5)).astype(BF16)

        def scores(hd):
            sl = slice(hd * X_HEAD_DIM, (hd + 1) * X_HEAD_DIM)
            env[hd] = _dot_nt(env["q"][:, sl], mk_ref[0, :, sl])

        def head(hd):
            if hd + 1 < N_X_HEADS:
                scores(hd + 1)
            sl = slice(hd * X_HEAD_DIM, (hd + 1) * X_HEAD_DIM)
            s = env.pop(hd)
            m = jnp.max(s, axis=-1, keepdims=True)
            p = jnp.exp(s - m)
            p = p * (1.0 / jnp.sum(p, axis=-1, keepdims=True))
            o_s[rows, sl] = _dot(p.astype(BF16), mv_ref[0, :, sl]).astype(BF16)

        def output():
            out_ref[0, rows, :] = env["h"] + _dot(o_s[rows, :], wo_ref[...])

        def project_and_first_scores():
            project()
            scores(0)

        return ([project_and_first_scores] + [functools.partial(head, hd) for hd in range(N_X_HEADS)]
                + [output])

    chains = [stages(r0) for r0 in range(0, h_ref.shape[1], SUB_ROWS)]
    n_stage = len(chains[0])
    for step in range(n_stage + len(chains) - 1):
        for lag, chain in enumerate(chains):
            if 0 <= step - lag < n_stage:
                chain[step - lag]()


def _prompt_xattn(h, g, w_xq, w_xo, mkb, mvb):
    b, s, d = h.shape
    tm = TM_X
    const = lambda shape: pl.BlockSpec(shape, lambda i, j: (0,) * len(shape))
    tok = pl.BlockSpec((1, tm, d), lambda i, j: (i, j, 0))
    mem = pl.BlockSpec((1, N_MEM, d), lambda i, j: (i, 0, 0))
    return pl.pallas_call(
        _prompt_xattn_kernel,
        grid=(b, s // tm),
        in_specs=[tok, const((1, d)), const((d, d)), const((d, d)), mem, mem],
        out_specs=tok,
        out_shape=jax.ShapeDtypeStruct((b, s, d), F32),
        scratch_shapes=[pltpu.VMEM((tm, d), BF16)],
        compiler_params=pltpu.CompilerParams(
            dimension_semantics=("arbitrary", "arbitrary"), vmem_limit_bytes=VMEM_LIMIT),
        name="prompt_xattn",
    )(h, g, w_xq, w_xo, mkb, mvb)


def _sample_mixer_kernel(sinks_ref, x_ref, gmix_ref, win_ref, wout_ref, gng_ref, gnb_ref,
                         ck_hbm, cv_hbm, st_hbm, bias_ref, dec_ref, xi_ref, zeta_ref,
                         h_ref, swk_ref, swv_ref, sst_ref, ck_ring, cv_ring, st_ring, sem, *, n_steps):
    bb = swk_ref.shape[0]
    nt = bb // 2
    step = pl.program_id(0)
    rings = ((ck_hbm, ck_ring), (cv_hbm, cv_ring), (st_hbm, st_ring))

    def fetch(t, slot):
        return [pltpu.make_async_copy(src.at[pl.ds(t * bb, bb)], ring.at[slot], sem.at[j, slot])
                for j, (src, ring) in enumerate(rings)]

    @pl.when(step == 0)
    def _():
        for c in fetch(0, 0):
            c.start()

    x = x_ref[...].reshape(bb * DEC_SEQ, D_MODEL)
    xn = _rms(x, gmix_ref[...]).astype(BF16)
    tile3 = lambda a: a.reshape(nt, SUBLANES, a.shape[-1])

    q = _dot(xn, win_ref[:, C_QA:C_QA + ATT_Q_W]) * (HEAD_DIM ** -0.5)
    kv = _dot(xn, win_ref[:, C_KV:C_KV + 2 * ATT_KV_W])
    qkr = _dot(xn, win_ref[:, C_QKR:C_QKR + 2 * RET_QK_W])
    vr = _dot(xn, win_ref[:, C_VR:C_VR + RET_V_W])
    gate3 = tile3(_silu(_dot(xn, win_ref[:, C_GR:C_GR + RET_V_W])))

    slot = step % N_RING
    for c in fetch(step, slot):
        c.wait()
    if n_steps > 1:
        @pl.when(step == 0)
        def _():
            for c in fetch(1, 1):
                c.start()
    if n_steps > 2:
        @pl.when(step + 2 < n_steps)
        def _():
            for c in fetch(step + 2, (step + 2) % N_RING):
                c.start()
    ck_ref, cv_ref, st_ref = ck_ring.at[slot], cv_ring.at[slot], st_ring.at[slot]

    lo512, hi512 = _half_masks(ATT_Q_W)
    q_r = pltpu.roll(q, HALF, axis=1)
    q_nat3 = tile3(q)
    q_rot3 = tile3(q_r)
    lo3 = lo512.reshape(1, 1, ATT_Q_W)
    hi3 = hi512.reshape(1, 1, ATT_Q_W)
    qa3 = (q_nat3 * lo3).astype(BF16)
    qb3 = (q_rot3 * lo3).astype(BF16)
    qc3 = (q_rot3 * hi3).astype(BF16)
    qd3 = (q_nat3 * hi3).astype(BF16)
    t128 = lambda a, i: a[:, :, i * LANES:(i + 1) * LANES]
    qs = jnp.concatenate([t128(qa3, 0), t128(qb3, 1), t128(qa3, 1), t128(qb3, 2),
                          t128(qc3, 2), t128(qd3, 2), t128(qc3, 3), t128(qd3, 3)], axis=1)

    k3 = tile3(kv[:, :ATT_KV_W])
    v3 = tile3(kv[:, ATT_KV_W:])
    pad_kv = jnp.zeros((nt, BLK - SUBLANES, LANES), BF16)
    knew_pad = jnp.concatenate([k3.astype(BF16), pad_kv], axis=1)
    vnew_pad = jnp.concatenate([v3.astype(BF16), pad_kv], axis=1)
    to_lanes = lambda a3: jnp.swapaxes(
        jnp.concatenate([a3, jnp.zeros((nt, BLK - SUBLANES, LANES), F32)], axis=1), 1, 2)
    k3t, v3t = to_lanes(k3), to_lanes(v3)
    roll3 = lambda a, sh: pltpu.roll(a.reshape(nt * BLK, LANES), sh, axis=1).reshape(nt, BLK, LANES)

    lo256, _ = _half_masks(RET_QK_W)
    qr3 = tile3(qkr[:, :RET_QK_W])
    kr3 = tile3(qkr[:, RET_QK_W:] * (RET_QK_DIM ** -0.5))
    vr3 = tile3(vr)
    lane256 = lax.broadcasted_iota(jnp.int32, (1, 1, RET_QK_W), 2)
    qrs = jnp.concatenate(
        [(qr3 * ((lane256 >= h * RET_QK_DIM) & (lane256 < (h + 1) * RET_QK_DIM)).astype(F32)).astype(BF16)
         for h in range(N_RET_HEADS)],
        axis=1)
    kr_pad = jnp.concatenate([kr3.astype(BF16), jnp.zeros((nt, BLK - SUBLANES, RET_QK_W), BF16)], axis=1)
    vr_pad = jnp.concatenate([vr3.astype(BF16), jnp.zeros((nt, BLK - SUBLANES, RET_V_W), BF16)], axis=1)

    lane = lax.broadcasted_iota(jnp.int32, (1, 1, LANES), 2)
    row8 = lax.broadcasted_iota(jnp.int32, (1, SUBLANES, 1), 1)
    bmm_nt = lambda a, b: jnp.einsum('bqd,bkd->bqk', a, b, preferred_element_type=F32)
    bmm = lambda a, b: jnp.einsum('bqk,bkd->bqd', a, b, preferred_element_type=F32)

    att_par, ret_par = [], []
    for par in range(2):
        bsl = pl.ds(par, nt, stride=2)
        ckt = ck_ref[bsl]
        cvt = cv_ref[bsl]
        keep = lane < WINDOW - DEC_SEQ
        new_shift = WINDOW - DEC_SEQ - DEC_SEQ * par
        swk_ref[bsl] = jnp.where(keep, roll3(ckt, WINDOW - DEC_SEQ), roll3(k3t, new_shift))
        swv_ref[bsl] = jnp.where(keep, roll3(cvt, WINDOW - DEC_SEQ), roll3(v3t, new_shift))

        s = jnp.concatenate([bmm(qs, ckt.astype(BF16)), bmm_nt(qs, knew_pad)], axis=2) + bias_ref[par]
        ps = []
        for h in range(N_ATT_HEADS):
            ps.append(_sink_softmax(s[:, h * SUBLANES:(h + 1) * SUBLANES, :], sinks_ref[h]).astype(BF16))
        p_all = jnp.concatenate(ps, axis=1)
        o = bmm_nt(p_all[:, :, :BLK], cvt.astype(BF16)) + bmm(p_all[:, :, BLK:], vnew_pad)
        o_r = pltpu.roll(o.reshape(nt * N_ATT_HEADS * SUBLANES, LANES), HALF, axis=1).reshape(o.shape)
        hr = lambda a, h: a[:, h * SUBLANES:(h + 1) * SUBLANES, :]
        low = lane < HALF
        att_par.append(jnp.concatenate([
            jnp.where(low, hr(o, 0), hr(o_r, 1)), jnp.where(low, hr(o, 2), hr(o_r, 3)),
            jnp.where(low, hr(o_r, 4), hr(o, 5)), jnp.where(low, hr(o_r, 6), hr(o, 7))], axis=2))

        st = st_ref[bsl]
        oc = bmm(qrs, st.astype(BF16))
        inner = (bmm_nt(qrs, kr_pad) * dec_ref[par]).astype(BF16)
        oi = bmm(inner, vr_pad)
        rs = []
        for h in range(N_RET_HEADS):
            vsl = slice(h * RET_V_DIM, (h + 1) * RET_V_DIM)
            rsl = slice(h * SUBLANES, (h + 1) * SUBLANES)
            o_h = oi[:, rsl, vsl] + oc[:, rsl, :] * xi_ref[par, rsl, :]
            rs.append(_group_norm(o_h, gng_ref[:, vsl], gnb_ref[:, vsl]) * gate3[:, :, vsl])
        ret_par.append(jnp.concatenate(rs, axis=2))

        kz3 = (kr3 * zeta_ref[par]).astype(BF16)
        vr3_b = vr3.astype(BF16)
        for p in range(nt):
            for i in range(N_RET_HEADS // 2):
                u = _dot_tn(kz3[p][:, i * LANES:(i + 1) * LANES],
                            vr3_b[p][:, 2 * i * RET_V_DIM:(2 * i + 2) * RET_V_DIM])
                for half in range(2):
                    h = 2 * i + half
                    dsl = slice(h * RET_QK_DIM, (h + 1) * RET_QK_DIM)
                    sst_ref[2 * p + par, dsl, :] = (
                        _GL_SAMPLE[h] * st[p, dsl, :]
                        + u[half * RET_QK_DIM:(half + 1) * RET_QK_DIM, half * RET_V_DIM:(half + 1) * RET_V_DIM])

    own0 = row8 < DEC_SEQ
    att3 = jnp.where(own0, att_par[0], att_par[1])
    ret3 = jnp.where(own0, ret_par[0], ret_par[1])
    mix = jnp.concatenate([att3, ret3], axis=2).reshape(2 * nt * DEC_SEQ, MIX_OUT).astype(BF16)
    h_ref[...] = x + _dot(mix, wout_ref[...])


def _sample_mixer(x3d, g_mix, w_in, w_out, sinks, gn_g, gn_b, ck, cv, st):
    nb, ls, d = x3d.shape
    n = nb * ls
    bb = BB_MIX
    r = bb * DEC_SEQ
    const = lambda shape: pl.BlockSpec(shape, lambda i: (0,) * len(shape))
    row = pl.BlockSpec((r, d), lambda i: (i, 0))
    win = pl.BlockSpec((bb, WINDOW, ATT_KV_W), lambda i: (i, 0, 0))
    state = pl.BlockSpec((bb, RET_QK_W, RET_V_DIM), lambda i: (i, 0, 0))
    once = lambda shape: pl.BlockSpec(shape, lambda i: (0,) * len(shape), pipeline_mode=pl.Buffered(1))
    hbm = pl.BlockSpec(memory_space=pl.ANY)
    return pl.pallas_call(
        functools.partial(_sample_mixer_kernel, n_steps=nb // bb),
        grid=(nb // bb,),
        in_specs=[
            pl.BlockSpec(memory_space=pltpu.SMEM),
            pl.BlockSpec((bb, ls, d), lambda i: (i, 0, 0)), const((1, d)), once((d, D_IN)), once((MIX_OUT, d)),
            const((1, RET_V_W)), const((1, RET_V_W)),
            hbm, hbm, hbm,
            const(_S_BIAS.shape), const(_S_DEC.shape), const(_S_XI.shape), const(_S_ZETA.shape),
        ],
        out_specs=[row, win, win, state],
        scratch_shapes=[
            pltpu.VMEM((N_RING, bb, WINDOW, ATT_KV_W), F32), pltpu.VMEM((N_RING, bb, WINDOW, ATT_KV_W), F32),
            pltpu.VMEM((N_RING, bb, RET_QK_W, RET_V_DIM), F32), pltpu.SemaphoreType.DMA((3, N_RING)),
        ],
        out_shape=[
            jax.ShapeDtypeStruct((n, d), F32),
            jax.ShapeDtypeStruct((nb, WINDOW, ATT_KV_W), F32),
            jax.ShapeDtypeStruct((nb, WINDOW, ATT_KV_W), F32),
            jax.ShapeDtypeStruct((nb, RET_QK_W, RET_V_DIM), F32),
        ],
        compiler_params=pltpu.CompilerParams(
            dimension_semantics=("arbitrary",), vmem_limit_bytes=VMEM_LIMIT),
        name="sample_mixer",
    )(sinks, x3d, g_mix, w_in, w_out, gn_g, gn_b, ck, cv, st,
      jnp.asarray(_S_BIAS), jnp.asarray(_S_DEC), jnp.asarray(_S_XI), jnp.asarray(_S_ZETA))


def _head_slab(x_ref, b, hd):
    group = X_D_HALVES * N_X_HEADS
    halves = [x_ref[b, pl.ds(dh * N_X_HEADS + hd, N_MEM, stride=group), :] for dh in range(X_D_HALVES)]
    return jnp.concatenate(halves, axis=1).astype(BF16)


def _mlp_value(h, g_ref, wup_ref, wdn_ref, gf_ref, fillers=None):
    xn = _rms(h, g_ref[...]).astype(BF16)
    piece = FF_CHUNK // N_X_HEADS
    opiece = D_MODEL // N_X_HEADS
    n_chunks = D_FF // FF_CHUNK
    nofill = (None, None, None)

    def up(c):
        qk, softmax, _ = fillers[c] if fillers is not None else nofill
        hid = []
        for k in range(N_X_HEADS):
            cols = slice(c * FF_CHUNK + k * piece, c * FF_CHUNK + (k + 1) * piece)
            u = jnp.maximum(_dot(xn, wup_ref[:, cols]), 0.0)
            hid.append((u * u).astype(BF16))
            if qk is not None:
                qk(k)
        if softmax is not None:
            softmax()
        return jnp.concatenate(hid, axis=1)

    def down(c, hid):
        pv = (fillers[c] if fillers is not None else nofill)[2]
        rows_c = slice(c * FF_CHUNK, (c + 1) * FF_CHUNK)
        out = []
        for k in range(N_X_HEADS):
            out.append(_dot(hid, wdn_ref[rows_c, k * opiece:(k + 1) * opiece]))
            if pv is not None:
                pv(k)
        return jnp.concatenate(out, axis=1)

    acc = h
    hid = up(0)
    for c in range(n_chunks):
        nxt = up(c + 1) if c + 1 < n_chunks else None
        acc = acc + down(c, hid)
        hid = nxt
    return _rms(acc, gf_ref[...])


def _mlp_xattn_kernel(hp_ref, hsm_ref, gx_ref, wq_ref, wo_ref, xk_ref, xv_ref, g_ref, wup_ref, wdn_ref, gf_ref,
                      yp_ref, ys_ref):
    i = pl.program_id(0)
    n = pl.num_programs(0) - 1
    bb = xk_ref.shape[0]
    assert bb == D_FF // FF_CHUNK and bb % 2 == 0

    @pl.when(i == 0)
    def _():
        xn = _rms(hsm_ref[...], gx_ref[...]).astype(BF16)
        ys_ref[...] = (_dot(xn, wq_ref[...]) * (X_HEAD_DIM ** -0.5)).reshape(ys_ref.shape)

    @pl.when(i < n)
    def _():
        own0 = lax.broadcasted_iota(jnp.int32, (SUBLANES, 1), 0) < DEC_SEQ
        o_rows = {}

        def attend(b):
            t = b // 2
            tile_b = pl.ds(i * bb + 2 * t, 2)
            env = dict(s=[], o=[])

            def qk(hd):
                if hd == 0:
                    env["q"] = ys_ref[tile_b].reshape(SUBLANES, D_MODEL).astype(BF16)
                env["s"].append(_dot_nt(env["q"][:, hd * X_HEAD_DIM:(hd + 1) * X_HEAD_DIM],
                                        _head_slab(xk_ref, b, hd)))

            def softmax():
                s = jnp.concatenate(env["s"], axis=0)
                m = jnp.max(s, axis=-1, keepdims=True)
                p = jnp.exp(s - m)
                env["p"] = p * (1.0 / jnp.sum(p, axis=-1, keepdims=True))

            def pv(hd):
                p = env["p"][hd * SUBLANES:(hd + 1) * SUBLANES].astype(BF16)
                env["o"].append(_dot(p, _head_slab(xv_ref, b, hd)))
                if hd == N_X_HEADS - 1:
                    o_rows[b] = jnp.concatenate(env["o"], axis=1)
                    if b % 2 == 1:
                        ys_ref[tile_b] = jnp.where(own0, o_rows[b - 1], o_rows[b]).reshape(2, DEC_SEQ, D_MODEL)

            return qk, softmax, pv

        fillers = [attend(b) for b in range(bb)]
        yp_ref[...] = _mlp_value(hp_ref[...], g_ref, wup_ref, wdn_ref, gf_ref, fillers)

    @pl.when(i == n)
    def _():
        o = ys_ref[...].reshape(hsm_ref.shape).astype(BF16)
        hs = hsm_ref[...] + _dot(o, wo_ref[...])
        ys_ref[...] = _mlp_value(hs, g_ref, wup_ref, wdn_ref, gf_ref).reshape(ys_ref.shape)


def _mlp_xattn(hp2d, hsm, g_xattn, w_xq, w_xo, xk, xv, g_mlp, w_up, w_down, g_final):
    n, d = hp2d.shape
    ns = hsm.shape[0]
    nb = xk.shape[0]
    bb = BB_X
    tm = n // (nb // bb)
    n_tiles = n // tm
    assert n_tiles * bb == nb and tm % SUBLANES == 0
    clip = lambda i: jnp.minimum(i, n_tiles - 1)
    prompt = pl.BlockSpec((tm, d), lambda i: (clip(i), 0))
    mem = pl.BlockSpec((bb,) + xk.shape[1:], lambda i: (clip(i), 0, 0))
    const = lambda shape: pl.BlockSpec(shape, lambda i: (0,) * len(shape), pipeline_mode=pl.Buffered(1))
    return pl.pallas_call(
        _mlp_xattn_kernel,
        grid=(n_tiles + 1,),
        in_specs=[prompt, const((ns, d)), const((1, d)), const((d, d)), const((d, d)), mem, mem,
                  const((1, d)), const((d, D_FF)), const((D_FF, d)), const((1, d))],
        out_specs=[prompt, pl.BlockSpec((ns // DEC_SEQ, DEC_SEQ, d), lambda i: (0, 0, 0))],
        out_shape=[jax.ShapeDtypeStruct((n, d), F32), jax.ShapeDtypeStruct((ns // DEC_SEQ, DEC_SEQ, d), F32)],
        compiler_params=pltpu.CompilerParams(
            dimension_semantics=("arbitrary",), vmem_limit_bytes=VMEM_LIMIT),
        name="mlp_xattn",
    )(hp2d, hsm, g_xattn, w_xq, w_xo, xk, xv, g_mlp, w_up, w_down, g_final)


def _mem_rows(c):
    nb = c.shape[0]
    c = c.reshape(nb, N_MEM, N_X_HEADS, X_D_HALVES, LANES)
    return jnp.transpose(c, (0, 1, 3, 2, 4)).reshape(nb, N_MEM * X_D_HALVES * N_X_HEADS, LANES)


def kernel(x_prompt, x_sample, mem_prompt, cache_win_k, cache_win_v, state_ret, cache_mem_k, cache_mem_v,
           g_mix, w_in, attn_sinks, ret_gn_g, ret_gn_b, w_out, g_xattn, g_mem, w_xq, w_xk, w_xv, w_xo,
           g_mlp, w_up, w_down, g_final):
    depth = w_in.shape[0]
    assert depth == 1, "single-layer trunk"
    b, s, d = x_prompt.shape
    nb, ls, _ = x_sample.shape
    row = lambda a: a.reshape(1, -1)
    sinks = attn_sinks[0]
    gn_g, gn_b = row(ret_gn_g[0]), row(ret_gn_b[0])
    g_fin = row(g_final)

    mk, mv, mkb, mvb, w_in_b, w_out_b = _memory_kv(
        mem_prompt.reshape(b * N_MEM, d), row(g_mem[0]), w_xk[0], w_xv[0], w_in[0], w_out[0])
    hp, p_wk, p_wv, p_rs, w_up_b, w_dn_b, w_xq_b, w_xo_b = _prompt_mixer(
        x_prompt, row(g_mix[0]), w_in_b, w_out_b, sinks, gn_g, gn_b,
        (w_up[0], w_down[0], w_xq[0], w_xo[0]))
    hp = _prompt_xattn(hp, row(g_xattn[0]), w_xq_b, w_xo_b,
                       mkb.reshape(b, N_MEM, d), mvb.reshape(b, N_MEM, d))

    win_t = lambda c: jnp.transpose(c, (0, 2, 3, 1)).reshape(nb, ATT_KV_W, WINDOW)
    win_t_inv = lambda a: jnp.transpose(a.reshape(-1, N_KV_HEADS, HEAD_DIM, WINDOW),
                                        (0, 3, 1, 2)).reshape(1, -1, WINDOW, N_KV_HEADS, HEAD_DIM)
    hs, s_wk, s_wv, s_rs = _sample_mixer(
        x_sample, row(g_mix[0]), w_in_b, w_out_b, sinks, gn_g, gn_b,
        win_t(cache_win_k[0]), win_t(cache_win_v[0]), state_ret[0].reshape(nb, RET_QK_W, RET_V_DIM))

    y_prompt, y_sample = _mlp_xattn(
        hp.reshape(b * s, d), hs, row(g_xattn[0]), w_xq_b, w_xo_b,
        _mem_rows(cache_mem_k[0]), _mem_rows(cache_mem_v[0]), row(g_mlp[0]), w_up_b, w_dn_b, g_fin)
    y_prompt = y_prompt.reshape(b, s, d)

    ret5 = lambda a, n: a.reshape(1, n, N_RET_HEADS, RET_QK_DIM, RET_V_DIM)
    mem5 = lambda a: jnp.transpose(a.reshape(b, N_MEM, X_D_HALVES, N_X_HEADS, LANES),
                                   (0, 1, 3, 2, 4)).reshape(1, b, N_MEM, N_X_HEADS, X_HEAD_DIM)
    return (y_prompt, y_sample,
            win_t_inv(p_wk), win_t_inv(p_wv), ret5(p_rs, b), mem5(mk), mem5(mv),
            win_t_inv(s_wk), win_t_inv(s_wv), ret5(s_rs, nb))
```

```python
import functools

import jax
import jax.numpy as jnp
import numpy as np
from jax import lax
from jax.experimental import pallas as pl
from jax.experimental.pallas import tpu as pltpu

F32 = jnp.float32
BF16 = jnp.bfloat16

D_MODEL = 1024
BATCH = 8
SEQ = 2048
DEC_BATCH = 128
DEC_SEQ = 4
HEAD_DIM = 64
N_ATT_HEADS = 8
N_KV_HEADS = 2
KV_GROUP = N_ATT_HEADS // N_KV_HEADS
WINDOW = 128
BLK = 128
N_RET_HEADS = 4
RET_QK_DIM = 64
RET_V_DIM = 128
N_MEM = 256
N_X_HEADS = 4
X_HEAD_DIM = D_MODEL // N_X_HEADS
D_FF = 4 * D_MODEL
RMS_EPS = 1e-6
GN_EPS = 1e-5

ATT_Q_W = N_ATT_HEADS * HEAD_DIM
ATT_KV_W = N_KV_HEADS * HEAD_DIM
RET_QK_W = N_RET_HEADS * RET_QK_DIM
RET_V_W = N_RET_HEADS * RET_V_DIM
MIX_OUT = ATT_Q_W + RET_V_W
D_IN = ATT_Q_W + 2 * ATT_KV_W + 2 * RET_QK_W + 2 * RET_V_W
C_QA, C_KV, C_QKR, C_VR, C_GR = 0, 512, 768, 1280, 1792

LANES = 128
SUBLANES = 8
HALF = LANES // 2
X_D_HALVES = X_HEAD_DIM // LANES
NEG = -1e30
VMEM_LIMIT = 56 * 1024 * 1024

TM_MIX = 512
TM_X = 2048
SUB_ROWS = 512
FF_CHUNK = 1024
BB_MIX = 32
N_RING = 3
BB_X = 4

NEG_SLOPES = [-(2.0 ** (-8.0 * (i + 1) / N_ATT_HEADS)) for i in range(N_ATT_HEADS)]
_LOG_G = np.log(1.0 - 2.0 ** (-5.0 - np.arange(N_RET_HEADS))).astype(np.float32).astype(np.float64)


def _prompt_tables():
    qi = np.arange(BLK)[:, None]
    kj = np.arange(2 * BLK)[None, :]
    dist = (qi + BLK - kj).astype(np.float64)
    mask = np.where((dist >= 0) & (dist < WINDOW), 0.0, NEG)
    l = np.arange(BLK, dtype=np.float64)
    diff = l[:, None] - l[None, :]
    decay = np.where(diff >= 0, np.exp(_LOG_G[:, None, None] * np.maximum(diff, 0.0)), 0.0)
    xi = np.exp((l[:, None] + 1.0) * _LOG_G[None, :])
    zeta = np.exp((BLK - 1.0 - l)[:, None] * _LOG_G[None, :])
    xi_t = np.repeat(xi, RET_V_DIM, axis=1)
    zeta_t = np.repeat(zeta, RET_QK_DIM, axis=1)
    f = lambda a: np.asarray(a, np.float32)
    return f(dist), f(mask), f(decay), f(xi_t), f(zeta_t)


def _sample_tables():
    slopes = -np.asarray(NEG_SLOPES)
    bias = np.full((2, N_ATT_HEADS * SUBLANES, 2 * BLK), NEG, np.float64)
    dec = np.zeros((2, N_RET_HEADS * SUBLANES, BLK), np.float64)
    xi = np.zeros((2, N_RET_HEADS * SUBLANES, RET_V_DIM), np.float64)
    zeta = np.zeros((2, SUBLANES, RET_QK_W), np.float64)
    for par in range(2):
        for r in range(SUBLANES):
            own = DEC_SEQ * par <= r < DEC_SEQ * (par + 1)
            t = r - DEC_SEQ * par if own else r % DEC_SEQ
            for h in range(N_ATT_HEADS):
                row = h * SUBLANES + r
                for j in range(WINDOW):
                    d = t + WINDOW - j
                    if 0 <= d < WINDOW:
                        bias[par, row, j] = -slopes[h] * d
                for c in range(DEC_SEQ):
                    d = t - c
                    if d >= 0:
                        bias[par, row, WINDOW + DEC_SEQ * par + c] = -slopes[h] * d
            for h in range(N_RET_HEADS):
                row = h * SUBLANES + r
                if own:
                    xi[par, row, :] = np.exp((t + 1.0) * _LOG_G[h])
                    zeta[par, r, h * RET_QK_DIM:(h + 1) * RET_QK_DIM] = np.exp((DEC_SEQ - 1.0 - t) * _LOG_G[h])
                    for c in range(t + 1):
                        dec[par, row, DEC_SEQ * par + c] = np.exp(_LOG_G[h] * (t - c))
    f = lambda a: np.asarray(a, np.float32)
    return f(bias), f(dec), f(xi), f(zeta)


_P_DIST, _P_MASK, _P_DECAY, _P_XI, _P_ZETA = _prompt_tables()
_S_BIAS, _S_DEC, _S_XI, _S_ZETA = _sample_tables()
_GL_PROMPT = [float(np.exp(_LOG_G[h] * BLK)) for h in range(N_RET_HEADS)]
_GL_SAMPLE = [float(np.exp(_LOG_G[h] * DEC_SEQ)) for h in range(N_RET_HEADS)]


def _rms(x, g):
    return x * lax.rsqrt(jnp.mean(x * x, axis=-1, keepdims=True) + RMS_EPS) * g


def _dot(a, b):
    return jnp.dot(a, b, preferred_element_type=F32)


def _dot_nt(a, b):
    return lax.dot_general(a, b, (((1,), (1,)), ((), ())), preferred_element_type=F32)


def _dot_tn(a, b):
    return lax.dot_general(a, b, (((0,), (0,)), ((), ())), preferred_element_type=F32)


def _silu(g):
    return g * (1.0 / (1.0 + jnp.exp(-g)))


def _half_masks(width):
    lane = lax.broadcasted_iota(jnp.int32, (1, width), 1)
    lo = ((lane & (LANES - 1)) < HALF).astype(F32)
    return lo, 1.0 - lo


def _sink_softmax(s, sink):
    m = jnp.maximum(jnp.max(s, axis=-1, keepdims=True), sink)
    p = jnp.exp(s - m)
    den = jnp.sum(p, axis=-1, keepdims=True) + jnp.exp(sink - m)
    return p * (1.0 / den)


def _group_norm(o, g, b):
    mu = jnp.mean(o, axis=-1, keepdims=True)
    d = o - mu
    var = jnp.mean(d * d, axis=-1, keepdims=True)
    return d * lax.rsqrt(var + GN_EPS) * g + b


def _pm_project_stages(x, slot, gmix_ref, win_ref, sc, kv_out=None):
    tm = x.shape[0]
    xn = _rms(x, gmix_ref[...]).astype(BF16)

    pw = 2 * LANES
    lo, hi = _half_masks(pw)

    def stage_q(i):
        cols = slice(i * pw, (i + 1) * pw)
        q = _dot(xn, win_ref[:, C_QA + i * pw:C_QA + (i + 1) * pw])
        sc["qlo"][slot, :, cols] = (q * (lo * HEAD_DIM ** -0.5)).astype(BF16)
        sc["qhi"][slot, :, cols] = (q * (hi * HEAD_DIM ** -0.5)).astype(BF16)

    def stage_kv():
        z = _dot(xn, win_ref[:, C_KV:C_KV + pw])
        low = lax.broadcasted_iota(jnp.int32, (tm, LANES), 1) < HALF
        k = z[:, 0:ATT_KV_W]
        v = z[:, ATT_KV_W:2 * ATT_KV_W]
        if kv_out is not None:
            kv_out[0][0] = k[tm - WINDOW:, :].T
            kv_out[1][0] = v[tm - WINDOW:, :].T
        k_r = pltpu.roll(k, HALF, axis=1)
        v_r = pltpu.roll(v, HALF, axis=1)
        sc["kd0"][slot] = jnp.where(low, k, k_r).astype(BF16)
        sc["kd1"][slot] = jnp.where(low, k_r, k).astype(BF16)
        sc["vd0"][slot, :, 0:LANES] = jnp.where(low, v, 1.0).astype(BF16)
        sc["vd0"][slot, :, LANES:2 * LANES] = jnp.where(low, 1.0, v_r).astype(BF16)
        sc["vd1"][slot, :, 0:LANES] = jnp.where(low, v_r, 1.0).astype(BF16)
        sc["vd1"][slot, :, LANES:2 * LANES] = jnp.where(low, 1.0, v).astype(BF16)

    def stage_qr():
        qr = _dot(xn, win_ref[:, C_QKR:C_QKR + pw])
        sc["qrlo"][slot] = (qr * lo).astype(BF16)
        sc["qrhi"][slot] = (qr * hi).astype(BF16)

    def stage_kr():
        sc["kr"][slot] = _dot(xn, win_ref[:, C_QKR + pw:C_QKR + 2 * pw]) * (RET_QK_DIM ** -0.5)

    def stage_vr(i):
        cols = slice(i * pw, (i + 1) * pw)
        sc["vr"][slot, :, cols] = _dot(xn, win_ref[:, C_VR + i * pw:C_VR + (i + 1) * pw]).astype(BF16)

    def stage_gate(i):
        cols = slice(i * pw, (i + 1) * pw)
        sc["gate"][slot, :, cols] = _silu(_dot(xn, win_ref[:, C_GR + i * pw:C_GR + (i + 1) * pw]))

    part = functools.partial
    return [part(stage_q, 0), part(stage_q, 1), stage_kv, stage_qr, stage_kr,
            part(stage_vr, 0), part(stage_vr, 1), part(stage_gate, 0), part(stage_gate, 1)]


def _pm_last_block(slot, tm, sc):
    rows = slice(tm - BLK, tm)
    return ([sc["kd0"][slot, rows, :], sc["kd1"][slot, rows, :]],
            [sc["vd0"][slot, rows, :], sc["vd1"][slot, rows, :]])


def _pm_blocks(slot, prev_kd, prev_vd, is_first, state, fillers, tm, sinks_ref, gng_ref, gnb_ref,
               decay_ref, xi_ref, zeta_ref, sc):
    nblk = tm // BLK
    n_units = nblk * (N_KV_HEADS + N_RET_HEADS // 2)
    pending = list(fillers)
    done_units = [0]

    def unit_done():
        done_units[0] += 1
        while pending and (len(fillers) - len(pending)) * n_units < done_units[0] * len(fillers):
            pending.pop(0)()
    lowb = lax.broadcasted_iota(jnp.int32, (BLK, LANES), 1) < HALF
    col = lax.broadcasted_iota(jnp.int32, (BLK, 2 * BLK), 1)
    first_mask = None if is_first is False else jnp.where((col < BLK) & is_first, NEG, 0.0)
    kd_refs = (sc["kd0"], sc["kd1"])
    vd_refs = (sc["vd0"], sc["vd1"])
    qlo, qhi, mix = sc["qlo"], sc["qhi"], sc["mix"]
    n_pairs = N_RET_HEADS // 2

    for j in range(nblk):
        rows = slice(j * BLK, (j + 1) * BLK)
        c0s = [kvh * KV_GROUP * HEAD_DIM for kvh in range(N_KV_HEADS)]
        lsls = [slice(i * LANES, (i + 1) * LANES) for i in range(n_pairs)]
        vds, scores = [], []
        for kvh in range(N_KV_HEADS):
            if j == 0:
                kd = jnp.concatenate([prev_kd[kvh], kd_refs[kvh][slot, rows, :]], axis=0)
                vds.append(jnp.concatenate([prev_vd[kvh], vd_refs[kvh][slot, rows, :]], axis=0))
            else:
                krows = slice((j - 1) * BLK, (j + 1) * BLK)
                kd = kd_refs[kvh][slot, krows, :]
                vds.append(vd_refs[kvh][slot, krows, :])
            c0 = c0s[kvh]
            qst = jnp.concatenate([qlo[slot, rows, c0:c0 + LANES], qhi[slot, rows, c0:c0 + LANES],
                                   qlo[slot, rows, c0 + LANES:c0 + 2 * LANES],
                                   qhi[slot, rows, c0 + LANES:c0 + 2 * LANES]], axis=0)
            scores.append(_dot_nt(qst, kd))
        unit_done()

        kps = [sc["kr"][slot, rows, lsls[i]] for i in range(n_pairs)]
        vpairs = [sc["vr"][slot, rows, 2 * i * RET_V_DIM:(2 * i + 2) * RET_V_DIM] for i in range(n_pairs)]
        q2s = [jnp.concatenate([sc["qrlo"][slot, rows, lsls[i]], sc["qrhi"][slot, rows, lsls[i]]], axis=0)
               for i in range(n_pairs)]
        a_s = [_dot_nt(q2s[i], kps[i].astype(BF16)) for i in range(n_pairs)]
        ocs = [_dot(q2s[i], state[i].astype(BF16)) for i in range(n_pairs)]
        us = [_dot_tn((kps[i] * zeta_ref[:, lsls[i]]).astype(BF16), vpairs[i]) for i in range(n_pairs)]
        unit_done()

        for kvh in range(N_KV_HEADS):
            s, vd, c0 = scores[kvh], vds[kvh], c0s[kvh]
            es, esink = [], []
            for g in range(KV_GROUP):
                h = kvh * KV_GROUP + g
                sg = s[g * BLK:(g + 1) * BLK] + sc["bias"][h]
                if j == 0 and first_mask is not None:
                    sg = sg + first_mask
                sink = sinks_ref[h]
                m = jnp.maximum(jnp.max(sg, axis=-1, keepdims=True), sink)
                es.append(jnp.exp(sg - m).astype(BF16))
                esink.append(jnp.exp(sink - m))
            o = _dot(jnp.concatenate(es, axis=0), vd)
            for pair in range(KV_GROUP // 2):
                oe = o[2 * pair * BLK:(2 * pair + 1) * BLK]
                oo = o[(2 * pair + 1) * BLK:(2 * pair + 2) * BLK]
                num = jnp.where(lowb, oe[:, :LANES], oo[:, LANES:])
                den = (jnp.where(lowb, oe[:, LANES:], oo[:, :LANES])
                       + jnp.where(lowb, esink[2 * pair], esink[2 * pair + 1]))
                cs = c0 + pair * LANES
                mix[slot, rows, cs:cs + LANES] = (num * (1.0 / den)).astype(BF16)
            unit_done()

        for i in range(n_pairs):
            a, oc, u, sp = a_s[i], ocs[i], us[i], state[i]
            inner = jnp.concatenate([a[:BLK] * decay_ref[2 * i], a[BLK:] * decay_ref[2 * i + 1]], axis=0)
            oi = _dot(inner.astype(BF16), vpairs[i])
            for half in range(2):
                h = 2 * i + half
                vsl = slice(h * RET_V_DIM, (h + 1) * RET_V_DIM)
                hr = slice(half * BLK, (half + 1) * BLK)
                o = oi[hr, half * RET_V_DIM:(half + 1) * RET_V_DIM] + oc[hr] * xi_ref[:, vsl]
                r = _group_norm(o, gng_ref[:, vsl], gnb_ref[:, vsl]) * sc["gate"][slot, rows, vsl]
                mix[slot, rows, ATT_Q_W + h * RET_V_DIM:ATT_Q_W + (h + 1) * RET_V_DIM] = r.astype(BF16)
            state[i] = jnp.concatenate(
                [_GL_PROMPT[2 * i] * sp[:RET_QK_DIM] + u[:RET_QK_DIM, :RET_V_DIM],
                 _GL_PROMPT[2 * i + 1] * sp[RET_QK_DIM:] + u[RET_QK_DIM:, RET_V_DIM:]], axis=0)

    assert not pending
    return state


def _pm_wout_pieces(slot, x_ref, rows, wout_ref, h_ref, sc):
    pw = 2 * LANES
    n = D_MODEL // pw
    parts = []

    def piece(k):
        parts.append(_dot(sc["mix"][slot], wout_ref[:, k * pw:(k + 1) * pw]))
        if k == n - 1:
            h_ref[rows, :] = x_ref[...] + jnp.concatenate(parts, axis=1)

    return [functools.partial(piece, k) for k in range(n)]


def _prompt_mixer_kernel(sinks_ref, xfirst_ref, xodd_ref, xnext_ref, gmix_ref, win_ref, wout_ref, gng_ref, gnb_ref,
                           dist_ref, mask_ref, decay_ref, xi_ref, zeta_ref,
                           wupf_ref, wdnf_ref, wqf_ref, wof_ref,
                           h_ref, wk_ref, wv_ref, st_ref,
                           wupb_ref, wdnb_ref, wqb_ref, wob_ref,
                           qlo_s, qhi_s, kd0_s, kd1_s, vd0_s, vd1_s,
                           qrlo_s, qrhi_s, kr_s, vr_s, gate_s, mix_s, bias_s, state_s, xkeep_s):
    u = pl.program_id(0)
    tm = xnext_ref.shape[0]
    wupb_ref[...] = wupf_ref[...].astype(BF16)
    wdnb_ref[...] = wdnf_ref[...].astype(BF16)
    wqb_ref[...] = wqf_ref[...].astype(BF16)
    wob_ref[...] = wof_ref[...].astype(BF16)
    sc = dict(qlo=qlo_s, qhi=qhi_s, kd0=kd0_s, kd1=kd1_s, vd0=vd0_s, vd1=vd1_s, qrlo=qrlo_s, qrhi=qrhi_s,
              kr=kr_s, vr=vr_s, gate=gate_s, mix=mix_s, bias=bias_s)
    n_pairs = N_RET_HEADS // 2
    blocks = functools.partial(_pm_blocks, tm=tm, sinks_ref=sinks_ref, gng_ref=gng_ref,
                               gnb_ref=gnb_ref, decay_ref=decay_ref, xi_ref=xi_ref, zeta_ref=zeta_ref, sc=sc)

    @pl.when(u == 0)
    def _():
        for h in range(N_ATT_HEADS):
            bias_s[h] = NEG_SLOPES[h] * dist_ref[...] + mask_ref[...]
        state_s[...] = jnp.zeros_like(state_s)
        kd0_s[1] = jnp.zeros(kd0_s.shape[1:], BF16)
        kd1_s[1] = jnp.zeros(kd1_s.shape[1:], BF16)
        vd0_s[1] = jnp.zeros(vd0_s.shape[1:], BF16)
        vd1_s[1] = jnp.zeros(vd1_s.shape[1:], BF16)
        xkeep_s[...] = xfirst_ref[...]
        for stage in _pm_project_stages(xfirst_ref[...], 0, gmix_ref, win_ref, sc):
            stage()

    seq_start = (u % 2) == 0
    state = [jnp.where(seq_start, 0.0, state_s[i * LANES:(i + 1) * LANES, :]) for i in range(n_pairs)]

    prev_kd, prev_vd = _pm_last_block(1, tm, sc)
    stages = _pm_project_stages(xodd_ref[...], 1, gmix_ref, win_ref, sc, kv_out=(wk_ref, wv_ref))
    state = blocks(0, prev_kd, prev_vd, seq_start, state, stages)
    wout0 = _pm_wout_pieces(0, xkeep_s, slice(0, tm), wout_ref, h_ref, sc)

    prev_kd, prev_vd = _pm_last_block(0, tm, sc)
    stages = _pm_project_stages(xnext_ref[...], 0, gmix_ref, win_ref, sc)
    state = blocks(1, prev_kd, prev_vd, False, state, wout0 + stages)
    for piece in _pm_wout_pieces(1, xodd_ref, slice(tm, 2 * tm), wout_ref, h_ref, sc):
        piece()
    xkeep_s[...] = xnext_ref[...]

    for i in range(n_pairs):
        state_s[i * LANES:(i + 1) * LANES, :] = state[i]
        st_ref[0, i * LANES:(i + 1) * LANES, :] = state[i]


def _prompt_mixer(x, g_mix, w_in, w_out, sinks, gn_g, gn_b, side_f32):
    b, s, d = x.shape
    tm = TM_MIX
    n_tiles = b * s // tm
    steps = n_tiles // 2
    seq_steps = s // (2 * tm)
    assert s % (2 * tm) == 0 and seq_steps == 2, "kernel assumes 4 tiles per sequence"
    x2d = x.reshape(b * s, d)
    const = lambda shape: pl.BlockSpec(shape, lambda i: (0,) * len(shape), pipeline_mode=pl.Buffered(1))
    slot2 = lambda rows, cols, dt: pltpu.VMEM((2, rows, cols), dt)
    side_specs = [pl.BlockSpec((w.shape[0] // steps, w.shape[1]), lambda i: (i, 0)) for w in side_f32]
    outs = pl.pallas_call(
        _prompt_mixer_kernel,
        grid=(steps,),
        in_specs=[
            pl.BlockSpec(memory_space=pltpu.SMEM),
            const((tm, d)),
            pl.BlockSpec((tm, d), lambda i: (2 * i + 1, 0)),
            pl.BlockSpec((tm, d), lambda i: (jnp.minimum(2 * i + 2, n_tiles - 1), 0)),
            const((1, d)), const((d, D_IN)), const((MIX_OUT, d)),
            const((1, RET_V_W)), const((1, RET_V_W)),
            const((BLK, 2 * BLK)), const((BLK, 2 * BLK)),
            const((N_RET_HEADS, BLK, BLK)), const((BLK, RET_V_W)), const((BLK, RET_QK_W)),
        ] + side_specs,
        out_specs=[
            pl.BlockSpec((2 * tm, d), lambda i: (i, 0)),
            pl.BlockSpec((1, WINDOW, ATT_KV_W), lambda i: (i // seq_steps, 0, 0)),
            pl.BlockSpec((1, WINDOW, ATT_KV_W), lambda i: (i // seq_steps, 0, 0)),
            pl.BlockSpec((1, RET_QK_W, RET_V_DIM), lambda i: (i // seq_steps, 0, 0)),
        ] + side_specs,
        out_shape=[
            jax.ShapeDtypeStruct((b * s, d), F32),
            jax.ShapeDtypeStruct((b, WINDOW, ATT_KV_W), F32),
            jax.ShapeDtypeStruct((b, WINDOW, ATT_KV_W), F32),
            jax.ShapeDtypeStruct((b, RET_QK_W, RET_V_DIM), F32),
        ] + [jax.ShapeDtypeStruct(w.shape, BF16) for w in side_f32],
        scratch_shapes=[
            slot2(tm, ATT_Q_W, BF16), slot2(tm, ATT_Q_W, BF16),
            slot2(tm, LANES, BF16), slot2(tm, LANES, BF16),
            slot2(tm, 2 * LANES, BF16), slot2(tm, 2 * LANES, BF16),
            slot2(tm, RET_QK_W, BF16), slot2(tm, RET_QK_W, BF16),
            slot2(tm, RET_QK_W, F32), slot2(tm, RET_V_W, BF16),
            slot2(tm, RET_V_W, F32), slot2(tm, MIX_OUT, BF16),
            pltpu.VMEM((N_ATT_HEADS, BLK, 2 * BLK), F32),
            pltpu.VMEM((RET_QK_W, RET_V_DIM), F32),
            pltpu.VMEM((tm, d), F32),
        ],
        compiler_params=pltpu.CompilerParams(
            dimension_semantics=("arbitrary",), vmem_limit_bytes=VMEM_LIMIT),
        name="prompt_mixer",
    )(sinks, x2d, x2d, x2d, g_mix, w_in, w_out, gn_g, gn_b,
      jnp.asarray(_P_DIST), jnp.asarray(_P_MASK), jnp.asarray(_P_DECAY), jnp.asarray(_P_XI),
      jnp.asarray(_P_ZETA), *side_f32)
    return (outs[0].reshape(b, s, d),) + tuple(outs[1:])


def _memkv_kernel(mem_ref, g_ref, wk_ref, wv_ref, win_ref, wout_ref,
                  mk_ref, mv_ref, mkb_ref, mvb_ref, winb_ref, woutb_ref):
    winb_ref[...] = win_ref[...].astype(BF16)
    woutb_ref[...] = wout_ref[...].astype(BF16)
    mn = _rms(mem_ref[...], g_ref[...]).astype(BF16)
    mk = _dot(mn, wk_ref[...].astype(BF16))
    mv = _dot(mn, wv_ref[...].astype(BF16))
    tm = mem_ref.shape[0]
    group = X_D_HALVES * N_X_HEADS
    for hd in range(N_X_HEADS):
        for dh in range(X_D_HALVES):
            cols = slice(hd * X_HEAD_DIM + dh * LANES, hd * X_HEAD_DIM + (dh + 1) * LANES)
            rows = pl.ds(dh * N_X_HEADS + hd, tm, stride=group)
            mk_ref[rows, :] = mk[:, cols]
            mv_ref[rows, :] = mv[:, cols]
    mkb_ref[...] = mk.astype(BF16)
    mvb_ref[...] = mv.astype(BF16)


def _memory_kv(mem2d, g_mem, w_xk, w_xv, w_in, w_out):
    n, d = mem2d.shape
    tm = 512
    row = pl.BlockSpec((tm, d), lambda i: (i, 0))
    rows_out = pl.BlockSpec((tm * d // LANES, LANES), lambda i: (i, 0))
    const = lambda shape: pl.BlockSpec(shape, lambda i: (0,) * len(shape), pipeline_mode=pl.Buffered(1))
    steps = n // tm
    win_blk = pl.BlockSpec((w_in.shape[0] // steps, w_in.shape[1]), lambda i: (i, 0))
    wout_blk = pl.BlockSpec((w_out.shape[0] // steps, w_out.shape[1]), lambda i: (i, 0))
    return pl.pallas_call(
        _memkv_kernel,
        grid=(n // tm,),
        in_specs=[row, const((1, d)), const((d, d)), const((d, d)), win_blk, wout_blk],
        out_specs=[rows_out, rows_out, row, row, win_blk, wout_blk],
        out_shape=[jax.ShapeDtypeStruct((n * d // LANES, LANES), F32),
                   jax.ShapeDtypeStruct((n * d // LANES, LANES), F32),
                   jax.ShapeDtypeStruct((n, d), BF16), jax.ShapeDtypeStruct((n, d), BF16),
                   jax.ShapeDtypeStruct(w_in.shape, BF16), jax.ShapeDtypeStruct(w_out.shape, BF16)],
        compiler_params=pltpu.CompilerParams(
            dimension_semantics=("arbitrary",), vmem_limit_bytes=VMEM_LIMIT),
        name="memory_kv",
    )(mem2d, g_mem, w_xk, w_xv, w_in, w_out)


def _prompt_xattn_kernel(h_ref, g_ref, wq_ref, wo_ref, mk_ref, mv_ref, out_ref, o_s):
    def stages(r0):
        rows = slice(r0, r0 + SUB_ROWS)
        env = {}

        def project():
            env["h"] = h_ref[0, rows, :]
            xn = _rms(env["h"], g_ref[...]).astype(BF16)
            env["q"] = (_dot(xn, wq_ref[...]) * (X_HEAD_DIM ** -0.5)).astype(BF16)

        def scores(hd):
            sl = slice(hd * X_HEAD_DIM, (hd + 1) * X_HEAD_DIM)
            env[hd] = _dot_nt(env["q"][:, sl], mk_ref[0, :, sl])

        def head(hd):
            if hd + 1 < N_X_HEADS:
                scores(hd + 1)
            sl = slice(hd * X_HEAD_DIM, (hd + 1) * X_HEAD_DIM)
            s = env.pop(hd)
            m = jnp.max(s, axis=-1, keepdims=True)
            p = jnp.exp(s - m)
            p = p * (1.0 / jnp.sum(p, axis=-1, keepdims=True))
            o_s[rows, sl] = _dot(p.astype(BF16), mv_ref[0, :, sl]).astype(BF16)

        def output():
            out_ref[0, rows, :] = env["h"] + _dot(o_s[rows, :], wo_ref[...])

        def project_and_first_scores():
            project()
            scores(0)

        return ([project_and_first_scores] + [functools.partial(head, hd) for hd in range(N_X_HEADS)]
                + [output])

    chains = [stages(r0) for r0 in range(0, h_ref.shape[1], SUB_ROWS)]
    n_stage = len(chains[0])
    for step in range(n_stage + len(chains) - 1):
        for lag, chain in enumerate(chains):
            if 0 <= step - lag < n_stage:
                chain[step - lag]()


def _prompt_xattn(h, g, w_xq, w_xo, mkb, mvb):
    b, s, d = h.shape
    tm = TM_X
    const = lambda shape: pl.BlockSpec(shape, lambda i, j: (0,) * len(shape))
    tok = pl.BlockSpec((1, tm, d), lambda i, j: (i, j, 0))
    mem = pl.BlockSpec((1, N_MEM, d), lambda i, j: (i, 0, 0))
    return pl.pallas_call(
        _prompt_xattn_kernel,
        grid=(b, s // tm),
        in_specs=[tok, const((1, d)), const((d, d)), const((d, d)), mem, mem],
        out_specs=tok,
        out_shape=jax.ShapeDtypeStruct((b, s, d), F32),
        scratch_shapes=[pltpu.VMEM((tm, d), BF16)],
        compiler_params=pltpu.CompilerParams(
            dimension_semantics=("arbitrary", "arbitrary"), vmem_limit_bytes=VMEM_LIMIT),
        name="prompt_xattn",
    )(h, g, w_xq, w_xo, mkb, mvb)


def _sample_mixer_kernel(sinks_ref, x_ref, gmix_ref, win_ref, wout_ref, gng_ref, gnb_ref,
                         ck_ref, cv_ref, st_ref, bias_ref, dec_ref, xi_ref, zeta_ref,
                         h_ref, swk_ref, swv_ref, sst_ref):
    bb = ck_ref.shape[0]
    nt = bb // 2
    x = x_ref[...].reshape(bb * DEC_SEQ, D_MODEL)
    xn = _rms(x, gmix_ref[...]).astype(BF16)
    tile3 = lambda a: a.reshape(nt, SUBLANES, a.shape[-1])

    q = _dot(xn, win_ref[:, C_QA:C_QA + ATT_Q_W]) * (HEAD_DIM ** -0.5)
    kv = _dot(xn, win_ref[:, C_KV:C_KV + 2 * ATT_KV_W])
    qkr = _dot(xn, win_ref[:, C_QKR:C_QKR + 2 * RET_QK_W])
    vr = _dot(xn, win_ref[:, C_VR:C_VR + RET_V_W])
    gate3 = tile3(_silu(_dot(xn, win_ref[:, C_GR:C_GR + RET_V_W])))

    lo512, hi512 = _half_masks(ATT_Q_W)
    q_r = pltpu.roll(q, HALF, axis=1)
    q_nat3 = tile3(q)
    q_rot3 = tile3(q_r)
    lo3 = lo512.reshape(1, 1, ATT_Q_W)
    hi3 = hi512.reshape(1, 1, ATT_Q_W)
    qa3 = (q_nat3 * lo3).astype(BF16)
    qb3 = (q_rot3 * lo3).astype(BF16)
    qc3 = (q_rot3 * hi3).astype(BF16)
    qd3 = (q_nat3 * hi3).astype(BF16)
    t128 = lambda a, i: a[:, :, i * LANES:(i + 1) * LANES]
    qs = jnp.concatenate([t128(qa3, 0), t128(qb3, 1), t128(qa3, 1), t128(qb3, 2),
                          t128(qc3, 2), t128(qd3, 2), t128(qc3, 3), t128(qd3, 3)], axis=1)

    k3 = tile3(kv[:, :ATT_KV_W])
    v3 = tile3(kv[:, ATT_KV_W:])
    pad_kv = jnp.zeros((nt, BLK - SUBLANES, LANES), BF16)
    knew_pad = jnp.concatenate([k3.astype(BF16), pad_kv], axis=1)
    vnew_pad = jnp.concatenate([v3.astype(BF16), pad_kv], axis=1)
    to_lanes = lambda a3: jnp.swapaxes(
        jnp.concatenate([a3, jnp.zeros((nt, BLK - SUBLANES, LANES), F32)], axis=1), 1, 2)
    k3t, v3t = to_lanes(k3), to_lanes(v3)
    roll3 = lambda a, sh: pltpu.roll(a.reshape(nt * BLK, LANES), sh, axis=1).reshape(nt, BLK, LANES)

    lo256, _ = _half_masks(RET_QK_W)
    qr3 = tile3(qkr[:, :RET_QK_W])
    kr3 = tile3(qkr[:, RET_QK_W:] * (RET_QK_DIM ** -0.5))
    vr3 = tile3(vr)
    lane256 = lax.broadcasted_iota(jnp.int32, (1, 1, RET_QK_W), 2)
    qrs = jnp.concatenate(
        [(qr3 * ((lane256 >= h * RET_QK_DIM) & (lane256 < (h + 1) * RET_QK_DIM)).astype(F32)).astype(BF16)
         for h in range(N_RET_HEADS)],
        axis=1)
    kr_pad = jnp.concatenate([kr3.astype(BF16), jnp.zeros((nt, BLK - SUBLANES, RET_QK_W), BF16)], axis=1)
    vr_pad = jnp.concatenate([vr3.astype(BF16), jnp.zeros((nt, BLK - SUBLANES, RET_V_W), BF16)], axis=1)

    lane = lax.broadcasted_iota(jnp.int32, (1, 1, LANES), 2)
    row8 = lax.broadcasted_iota(jnp.int32, (1, SUBLANES, 1), 1)
    bmm_nt = lambda a, b: jnp.einsum('bqd,bkd->bqk', a, b, preferred_element_type=F32)
    bmm = lambda a, b: jnp.einsum('bqk,bkd->bqd', a, b, preferred_element_type=F32)

    att_par, ret_par = [], []
    for par in range(2):
        bsl = pl.ds(par, nt, stride=2)
        ckt = ck_ref[bsl]
        cvt = cv_ref[bsl]
        keep = lane < WINDOW - DEC_SEQ
        new_shift = WINDOW - DEC_SEQ - DEC_SEQ * par
        swk_ref[bsl] = jnp.where(keep, roll3(ckt, WINDOW - DEC_SEQ), roll3(k3t, new_shift))
        swv_ref[bsl] = jnp.where(keep, roll3(cvt, WINDOW - DEC_SEQ), roll3(v3t, new_shift))

        s = jnp.concatenate([bmm(qs, ckt.astype(BF16)), bmm_nt(qs, knew_pad)], axis=2) + bias_ref[par]
        ps = []
        for h in range(N_ATT_HEADS):
            ps.append(_sink_softmax(s[:, h * SUBLANES:(h + 1) * SUBLANES, :], sinks_ref[h]).astype(BF16))
        p_all = jnp.concatenate(ps, axis=1)
        o = bmm_nt(p_all[:, :, :BLK], cvt.astype(BF16)) + bmm(p_all[:, :, BLK:], vnew_pad)
        o_r = pltpu.roll(o.reshape(nt * N_ATT_HEADS * SUBLANES, LANES), HALF, axis=1).reshape(o.shape)
        hr = lambda a, h: a[:, h * SUBLANES:(h + 1) * SUBLANES, :]
        low = lane < HALF
        att_par.append(jnp.concatenate([
            jnp.where(low, hr(o, 0), hr(o_r, 1)), jnp.where(low, hr(o, 2), hr(o_r, 3)),
            jnp.where(low, hr(o_r, 4), hr(o, 5)), jnp.where(low, hr(o_r, 6), hr(o, 7))], axis=2))

        st = st_ref[bsl]
        oc = bmm(qrs, st.astype(BF16))
        inner = (bmm_nt(qrs, kr_pad) * dec_ref[par]).astype(BF16)
        oi = bmm(inner, vr_pad)
        rs = []
        for h in range(N_RET_HEADS):
            vsl = slice(h * RET_V_DIM, (h + 1) * RET_V_DIM)
            rsl = slice(h * SUBLANES, (h + 1) * SUBLANES)
            o_h = oi[:, rsl, vsl] + oc[:, rsl, :] * xi_ref[par, rsl, :]
            rs.append(_group_norm(o_h, gng_ref[:, vsl], gnb_ref[:, vsl]) * gate3[:, :, vsl])
        ret_par.append(jnp.concatenate(rs, axis=2))

        kz3 = (kr3 * zeta_ref[par]).astype(BF16)
        vr3_b = vr3.astype(BF16)
        for p in range(nt):
            for i in range(N_RET_HEADS // 2):
                u = _dot_tn(kz3[p][:, i * LANES:(i + 1) * LANES],
                            vr3_b[p][:, 2 * i * RET_V_DIM:(2 * i + 2) * RET_V_DIM])
                for half in range(2):
                    h = 2 * i + half
                    dsl = slice(h * RET_QK_DIM, (h + 1) * RET_QK_DIM)
                    sst_ref[2 * p + par, dsl, :] = (
                        _GL_SAMPLE[h] * st[p, dsl, :]
                        + u[half * RET_QK_DIM:(half + 1) * RET_QK_DIM, half * RET_V_DIM:(half + 1) * RET_V_DIM])

    own0 = row8 < DEC_SEQ
    att3 = jnp.where(own0, att_par[0], att_par[1])
    ret3 = jnp.where(own0, ret_par[0], ret_par[1])
    mix = jnp.concatenate([att3, ret3], axis=2).reshape(2 * nt * DEC_SEQ, MIX_OUT).astype(BF16)
    h_ref[...] = x + _dot(mix, wout_ref[...])


def _sample_mixer_ring_kernel(sinks_ref, x_ref, gmix_ref, win_ref, wout_ref, gng_ref, gnb_ref,
                              ck_hbm, cv_hbm, st_hbm, bias_ref, dec_ref, xi_ref, zeta_ref,
                              h_ref, swk_ref, swv_ref, sst_ref, ck_ring, cv_ring, st_ring, sem, *, n_steps):
    bb = swk_ref.shape[0]
    step = pl.program_id(0)
    rings = ((ck_hbm, ck_ring), (cv_hbm, cv_ring), (st_hbm, st_ring))

    def fetch(t, slot):
        return [pltpu.make_async_copy(src.at[pl.ds(t * bb, bb)], ring.at[slot], sem.at[j, slot])
                for j, (src, ring) in enumerate(rings)]

    @pl.when(step == 0)
    def _():
        for c in fetch(0, 0):
            c.start()

    slot = step % N_RING
    for c in fetch(step, slot):
        c.wait()
    if n_steps > 1:
        @pl.when(step == 0)
        def _():
            for c in fetch(1, 1):
                c.start()
    if n_steps > 2:
        @pl.when(step + 2 < n_steps)
        def _():
            for c in fetch(step + 2, (step + 2) % N_RING):
                c.start()

    for k in range(N_RING):
        @pl.when(slot == k)
        def _(k=k):
            _sample_mixer_kernel(sinks_ref, x_ref, gmix_ref, win_ref, wout_ref, gng_ref, gnb_ref,
                                 ck_ring.at[k], cv_ring.at[k], st_ring.at[k], bias_ref, dec_ref, xi_ref, zeta_ref,
                                 h_ref, swk_ref, swv_ref, sst_ref)


def _sample_mixer(x3d, g_mix, w_in, w_out, sinks, gn_g, gn_b, ck, cv, st):
    nb, ls, d = x3d.shape
    n = nb * ls
    bb = BB_MIX
    r = bb * DEC_SEQ
    const = lambda shape: pl.BlockSpec(shape, lambda i: (0,) * len(shape))
    row = pl.BlockSpec((r, d), lambda i: (i, 0))
    win = pl.BlockSpec((bb, WINDOW, ATT_KV_W), lambda i: (i, 0, 0))
    state = pl.BlockSpec((bb, RET_QK_W, RET_V_DIM), lambda i: (i, 0, 0))
    once = lambda shape: pl.BlockSpec(shape, lambda i: (0,) * len(shape), pipeline_mode=pl.Buffered(1))
    hbm = pl.BlockSpec(memory_space=pl.ANY)
    return pl.pallas_call(
        functools.partial(_sample_mixer_ring_kernel, n_steps=nb // bb),
        grid=(nb // bb,),
        in_specs=[
            pl.BlockSpec(memory_space=pltpu.SMEM),
            pl.BlockSpec((bb, ls, d), lambda i: (i, 0, 0)), const((1, d)), once((d, D_IN)), once((MIX_OUT, d)),
            const((1, RET_V_W)), const((1, RET_V_W)),
            hbm, hbm, hbm,
            const(_S_BIAS.shape), const(_S_DEC.shape), const(_S_XI.shape), const(_S_ZETA.shape),
        ],
        out_specs=[row, win, win, state],
        scratch_shapes=[
            pltpu.VMEM((N_RING, bb, WINDOW, ATT_KV_W), F32), pltpu.VMEM((N_RING, bb, WINDOW, ATT_KV_W), F32),
            pltpu.VMEM((N_RING, bb, RET_QK_W, RET_V_DIM), F32), pltpu.SemaphoreType.DMA((3, N_RING)),
        ],
        out_shape=[
            jax.ShapeDtypeStruct((n, d), F32),
            jax.ShapeDtypeStruct((nb, WINDOW, ATT_KV_W), F32),
            jax.ShapeDtypeStruct((nb, WINDOW, ATT_KV_W), F32),
            jax.ShapeDtypeStruct((nb, RET_QK_W, RET_V_DIM), F32),
        ],
        compiler_params=pltpu.CompilerParams(
            dimension_semantics=("arbitrary",), vmem_limit_bytes=VMEM_LIMIT),
        name="sample_mixer",
    )(sinks, x3d, g_mix, w_in, w_out, gn_g, gn_b, ck, cv, st,
      jnp.asarray(_S_BIAS), jnp.asarray(_S_DEC), jnp.asarray(_S_XI), jnp.asarray(_S_ZETA))


def _head_slab(x_ref, b, hd):
    group = X_D_HALVES * N_X_HEADS
    halves = [x_ref[b, pl.ds(dh * N_X_HEADS + hd, N_MEM, stride=group), :] for dh in range(X_D_HALVES)]
    return jnp.concatenate(halves, axis=1).astype(BF16)


def _mlp_value(h, g_ref, wup_ref, wdn_ref, gf_ref, fillers=None):
    xn = _rms(h, g_ref[...]).astype(BF16)
    piece = FF_CHUNK // N_X_HEADS
    opiece = D_MODEL // N_X_HEADS
    n_chunks = D_FF // FF_CHUNK
    nofill = (None, None, None)

    def up(c):
        qk, softmax, _ = fillers[c] if fillers is not None else nofill
        hid = []
        for k in range(N_X_HEADS):
            cols = slice(c * FF_CHUNK + k * piece, c * FF_CHUNK + (k + 1) * piece)
            u = jnp.maximum(_dot(xn, wup_ref[:, cols]), 0.0)
            hid.append((u * u).astype(BF16))
            if qk is not None:
                qk(k)
        if softmax is not None:
            softmax()
        return jnp.concatenate(hid, axis=1)

    def down(c, hid):
        pv = (fillers[c] if fillers is not None else nofill)[2]
        rows_c = slice(c * FF_CHUNK, (c + 1) * FF_CHUNK)
        out = []
        for k in range(N_X_HEADS):
            out.append(_dot(hid, wdn_ref[rows_c, k * opiece:(k + 1) * opiece]))
            if pv is not None:
                pv(k)
        return jnp.concatenate(out, axis=1)

    acc = h
    hid = up(0)
    for c in range(n_chunks):
        nxt = up(c + 1) if c + 1 < n_chunks else None
        acc = acc + down(c, hid)
        hid = nxt
    return _rms(acc, gf_ref[...])


def _mlp_xattn_kernel(hp_ref, hsm_ref, gx_ref, wq_ref, wo_ref, xk_ref, xv_ref, g_ref, wup_ref, wdn_ref, gf_ref,
                      yp_ref, ys_ref):
    i = pl.program_id(0)
    n = pl.num_programs(0) - 1
    bb = xk_ref.shape[0]
    assert bb == D_FF // FF_CHUNK and bb % 2 == 0

    @pl.when(i == 0)
    def _():
        xn = _rms(hsm_ref[...], gx_ref[...]).astype(BF16)
        ys_ref[...] = (_dot(xn, wq_ref[...]) * (X_HEAD_DIM ** -0.5)).reshape(ys_ref.shape)

    @pl.when(i < n)
    def _():
        own0 = lax.broadcasted_iota(jnp.int32, (SUBLANES, 1), 0) < DEC_SEQ
        o_rows = {}

        def attend(b):
            t = b // 2
            tile_b = pl.ds(i * bb + 2 * t, 2)
            env = dict(s=[], o=[])

            def qk(hd):
                if hd == 0:
                    env["q"] = ys_ref[tile_b].reshape(SUBLANES, D_MODEL).astype(BF16)
                env["s"].append(_dot_nt(env["q"][:, hd * X_HEAD_DIM:(hd + 1) * X_HEAD_DIM],
                                        _head_slab(xk_ref, b, hd)))

            def softmax():
                s = jnp.concatenate(env["s"], axis=0)
                m = jnp.max(s, axis=-1, keepdims=True)
                p = jnp.exp(s - m)
                env["p"] = p * (1.0 / jnp.sum(p, axis=-1, keepdims=True))

            def pv(hd):
                p = env["p"][hd * SUBLANES:(hd + 1) * SUBLANES].astype(BF16)
                env["o"].append(_dot(p, _head_slab(xv_ref, b, hd)))
                if hd == N_X_HEADS - 1:
                    o_rows[b] = jnp.concatenate(env["o"], axis=1)
                    if b % 2 == 1:
                        ys_ref[tile_b] = jnp.where(own0, o_rows[b - 1], o_rows[b]).reshape(2, DEC_SEQ, D_MODEL)

            return qk, softmax, pv

        fillers = [attend(b) for b in range(bb)]
        yp_ref[...] = _mlp_value(hp_ref[...], g_ref, wup_ref, wdn_ref, gf_ref, fillers)

    @pl.when(i == n)
    def _():
        o = ys_ref[...].reshape(hsm_ref.shape).astype(BF16)
        hs = hsm_ref[...] + _dot(o, wo_ref[...])
        ys_ref[...] = _mlp_value(hs, g_ref, wup_ref, wdn_ref, gf_ref).reshape(ys_ref.shape)


def _mlp_xattn(hp2d, hsm, g_xattn, w_xq, w_xo, xk, xv, g_mlp, w_up, w_down, g_final):
    n, d = hp2d.shape
    ns = hsm.shape[0]
    nb = xk.shape[0]
    bb = BB_X
    tm = n // (nb // bb)
    n_tiles = n // tm
    assert n_tiles * bb == nb and tm % SUBLANES == 0
    clip = lambda i: jnp.minimum(i, n_tiles - 1)
    prompt = pl.BlockSpec((tm, d), lambda i: (clip(i), 0))
    mem = pl.BlockSpec((bb,) + xk.shape[1:], lambda i: (clip(i), 0, 0))
    const = lambda shape: pl.BlockSpec(shape, lambda i: (0,) * len(shape), pipeline_mode=pl.Buffered(1))
    return pl.pallas_call(
        _mlp_xattn_kernel,
        grid=(n_tiles + 1,),
        in_specs=[prompt, const((ns, d)), const((1, d)), const((d, d)), const((d, d)), mem, mem,
                  const((1, d)), const((d, D_FF)), const((D_FF, d)), const((1, d))],
        out_specs=[prompt, pl.BlockSpec((ns // DEC_SEQ, DEC_SEQ, d), lambda i: (0, 0, 0))],
        out_shape=[jax.ShapeDtypeStruct((n, d), F32), jax.ShapeDtypeStruct((ns // DEC_SEQ, DEC_SEQ, d), F32)],
        compiler_params=pltpu.CompilerParams(
            dimension_semantics=("arbitrary",), vmem_limit_bytes=VMEM_LIMIT),
        name="mlp_xattn",
    )(hp2d, hsm, g_xattn, w_xq, w_xo, xk, xv, g_mlp, w_up, w_down, g_final)


def _mem_rows(c):
    nb = c.shape[0]
    c = c.reshape(nb, N_MEM, N_X_HEADS, X_D_HALVES, LANES)
    return jnp.transpose(c, (0, 1, 3, 2, 4)).reshape(nb, N_MEM * X_D_HALVES * N_X_HEADS, LANES)


def kernel(x_prompt, x_sample, mem_prompt, cache_win_k, cache_win_v, state_ret, cache_mem_k, cache_mem_v,
           g_mix, w_in, attn_sinks, ret_gn_g, ret_gn_b, w_out, g_xattn, g_mem, w_xq, w_xk, w_xv, w_xo,
           g_mlp, w_up, w_down, g_final):
    depth = w_in.shape[0]
    assert depth == 1, "single-layer trunk"
    b, s, d = x_prompt.shape
    nb, ls, _ = x_sample.shape
    row = lambda a: a.reshape(1, -1)
    sinks = attn_sinks[0]
    gn_g, gn_b = row(ret_gn_g[0]), row(ret_gn_b[0])
    g_fin = row(g_final)

    mk, mv, mkb, mvb, w_in_b, w_out_b = _memory_kv(
        mem_prompt.reshape(b * N_MEM, d), row(g_mem[0]), w_xk[0], w_xv[0], w_in[0], w_out[0])
    hp, p_wk, p_wv, p_rs, w_up_b, w_dn_b, w_xq_b, w_xo_b = _prompt_mixer(
        x_prompt, row(g_mix[0]), w_in_b, w_out_b, sinks, gn_g, gn_b,
        (w_up[0], w_down[0], w_xq[0], w_xo[0]))
    hp = _prompt_xattn(hp, row(g_xattn[0]), w_xq_b, w_xo_b,
                       mkb.reshape(b, N_MEM, d), mvb.reshape(b, N_MEM, d))

    win_t = lambda c: jnp.transpose(c, (0, 2, 3, 1)).reshape(nb, ATT_KV_W, WINDOW)
    win_t_inv = lambda a: jnp.transpose(a.reshape(-1, N_KV_HEADS, HEAD_DIM, WINDOW),
                                        (0, 3, 1, 2)).reshape(1, -1, WINDOW, N_KV_HEADS, HEAD_DIM)
    hs, s_wk, s_wv, s_rs = _sample_mixer(
        x_sample, row(g_mix[0]), w_in_b, w_out_b, sinks, gn_g, gn_b,
        win_t(cache_win_k[0]), win_t(cache_win_v[0]), state_ret[0].reshape(nb, RET_QK_W, RET_V_DIM))

    y_prompt, y_sample = _mlp_xattn(
        hp.reshape(b * s, d), hs, row(g_xattn[0]), w_xq_b, w_xo_b,
        _mem_rows(cache_mem_k[0]), _mem_rows(cache_mem_v[0]), row(g_mlp[0]), w_up_b, w_dn_b, g_fin)
    y_prompt = y_prompt.reshape(b, s, d)

    ret5 = lambda a, n: a.reshape(1, n, N_RET_HEADS, RET_QK_DIM, RET_V_DIM)
    mem5 = lambda a: jnp.transpose(a.reshape(b, N_MEM, X_D_HALVES, N_X_HEADS, LANES),
                                   (0, 1, 3, 2, 4)).reshape(1, b, N_MEM, N_X_HEADS, X_HEAD_DIM)
    return (y_prompt, y_sample,
            win_t_inv(p_wk), win_t_inv(p_wv), ret5(p_rs, b), mem5(mk), mem5(mv),
            win_t_inv(s_wk), win_t_inv(s_wv), ret5(s_rs, nb))
```

```python
import functools

import jax
import jax.numpy as jnp
import numpy as np
from jax import lax
from jax.experimental import pallas as pl
from jax.experimental.pallas import tpu as pltpu

F32 = jnp.float32
BF16 = jnp.bfloat16

D_MODEL = 1024
BATCH = 8
SEQ = 2048
DEC_BATCH = 128
DEC_SEQ = 4
HEAD_DIM = 64
N_ATT_HEADS = 8
N_KV_HEADS = 2
KV_GROUP = N_ATT_HEADS // N_KV_HEADS
WINDOW = 128
BLK = 128
N_RET_HEADS = 4
RET_QK_DIM = 64
RET_V_DIM = 128
N_MEM = 256
N_X_HEADS = 4
X_HEAD_DIM = D_MODEL // N_X_HEADS
D_FF = 4 * D_MODEL
RMS_EPS = 1e-6
GN_EPS = 1e-5

ATT_Q_W = N_ATT_HEADS * HEAD_DIM
ATT_KV_W = N_KV_HEADS * HEAD_DIM
RET_QK_W = N_RET_HEADS * RET_QK_DIM
RET_V_W = N_RET_HEADS * RET_V_DIM
MIX_OUT = ATT_Q_W + RET_V_W
D_IN = ATT_Q_W + 2 * ATT_KV_W + 2 * RET_QK_W + 2 * RET_V_W
C_QA, C_KV, C_QKR, C_VR, C_GR = 0, 512, 768, 1280, 1792

LANES = 128
SUBLANES = 8
HALF = LANES // 2
X_D_HALVES = X_HEAD_DIM // LANES
NEG = -1e30
VMEM_LIMIT = 56 * 1024 * 1024

TM_MIX = 512
TM_X = 2048
SUB_ROWS = 512
FF_CHUNK = 1024
BB_MIX = 32
BB_X = 4

NEG_SLOPES = [-(2.0 ** (-8.0 * (i + 1) / N_ATT_HEADS)) for i in range(N_ATT_HEADS)]
_LOG_G = np.log(1.0 - 2.0 ** (-5.0 - np.arange(N_RET_HEADS))).astype(np.float32).astype(np.float64)


def _prompt_tables():
    qi = np.arange(BLK)[:, None]
    kj = np.arange(2 * BLK)[None, :]
    dist = (qi + BLK - kj).astype(np.float64)
    mask = np.where((dist >= 0) & (dist < WINDOW), 0.0, NEG)
    l = np.arange(BLK, dtype=np.float64)
    diff = l[:, None] - l[None, :]
    decay = np.where(diff >= 0, np.exp(_LOG_G[:, None, None] * np.maximum(diff, 0.0)), 0.0)
    xi = np.exp((l[:, None] + 1.0) * _LOG_G[None, :])
    zeta = np.exp((BLK - 1.0 - l)[:, None] * _LOG_G[None, :])
    xi_t = np.repeat(xi, RET_V_DIM, axis=1)
    zeta_t = np.repeat(zeta, RET_QK_DIM, axis=1)
    f = lambda a: np.asarray(a, np.float32)
    return f(dist), f(mask), f(decay), f(xi_t), f(zeta_t)


def _sample_tables():
    slopes = -np.asarray(NEG_SLOPES)
    bias = np.full((2, N_ATT_HEADS * SUBLANES, 2 * BLK), NEG, np.float64)
    dec = np.zeros((2, N_RET_HEADS * SUBLANES, BLK), np.float64)
    xi = np.zeros((2, N_RET_HEADS * SUBLANES, RET_V_DIM), np.float64)
    zeta = np.zeros((2, SUBLANES, RET_QK_W), np.float64)
    for par in range(2):
        for r in range(SUBLANES):
            own = DEC_SEQ * par <= r < DEC_SEQ * (par + 1)
            t = r - DEC_SEQ * par if own else r % DEC_SEQ
            for h in range(N_ATT_HEADS):
                row = h * SUBLANES + r
                for j in range(WINDOW):
                    d = t + WINDOW - j
                    if 0 <= d < WINDOW:
                        bias[par, row, j] = -slopes[h] * d
                for c in range(DEC_SEQ):
                    d = t - c
                    if d >= 0:
                        bias[par, row, WINDOW + DEC_SEQ * par + c] = -slopes[h] * d
            for h in range(N_RET_HEADS):
                row = h * SUBLANES + r
                if own:
                    xi[par, row, :] = np.exp((t + 1.0) * _LOG_G[h])
                    zeta[par, r, h * RET_QK_DIM:(h + 1) * RET_QK_DIM] = np.exp((DEC_SEQ - 1.0 - t) * _LOG_G[h])
                    for c in range(t + 1):
                        dec[par, row, DEC_SEQ * par + c] = np.exp(_LOG_G[h] * (t - c))
    f = lambda a: np.asarray(a, np.float32)
    return f(bias), f(dec), f(xi), f(zeta)


_P_DIST, _P_MASK, _P_DECAY, _P_XI, _P_ZETA = _prompt_tables()
_S_BIAS, _S_DEC, _S_XI, _S_ZETA = _sample_tables()
_GL_PROMPT = [float(np.exp(_LOG_G[h] * BLK)) for h in range(N_RET_HEADS)]
_GL_SAMPLE = [float(np.exp(_LOG_G[h] * DEC_SEQ)) for h in range(N_RET_HEADS)]


def _rms(x, g):
    return x * lax.rsqrt(jnp.mean(x * x, axis=-1, keepdims=True) + RMS_EPS) * g


def _dot(a, b):
    return jnp.dot(a, b, preferred_element_type=F32)


def _dot_nt(a, b):
    return lax.dot_general(a, b, (((1,), (1,)), ((), ())), preferred_element_type=F32)


def _dot_tn(a, b):
    return lax.dot_general(a, b, (((0,), (0,)), ((), ())), preferred_element_type=F32)


def _silu(g):
    return g * (1.0 / (1.0 + jnp.exp(-g)))


def _half_masks(width):
    lane = lax.broadcasted_iota(jnp.int32, (1, width), 1)
    lo = ((lane & (LANES - 1)) < HALF).astype(F32)
    return lo, 1.0 - lo


def _sink_softmax(s, sink):
    m = jnp.maximum(jnp.max(s, axis=-1, keepdims=True), sink)
    p = jnp.exp(s - m)
    den = jnp.sum(p, axis=-1, keepdims=True) + jnp.exp(sink - m)
    return p * (1.0 / den)


def _group_norm(o, g, b):
    mu = jnp.mean(o, axis=-1, keepdims=True)
    d = o - mu
    var = jnp.mean(d * d, axis=-1, keepdims=True)
    return d * lax.rsqrt(var + GN_EPS) * g + b


def _pm_project_stages(x, slot, gmix_ref, win_ref, sc, kv_out=None):
    tm = x.shape[0]
    xn = _rms(x, gmix_ref[...]).astype(BF16)

    pw = 2 * LANES
    lo, hi = _half_masks(pw)

    def stage_q(i):
        cols = slice(i * pw, (i + 1) * pw)
        q = _dot(xn, win_ref[:, C_QA + i * pw:C_QA + (i + 1) * pw])
        sc["qlo"][slot, :, cols] = (q * (lo * HEAD_DIM ** -0.5)).astype(BF16)
        sc["qhi"][slot, :, cols] = (q * (hi * HEAD_DIM ** -0.5)).astype(BF16)

    def stage_kv():
        z = _dot(xn, win_ref[:, C_KV:C_KV + pw])
        low = lax.broadcasted_iota(jnp.int32, (tm, LANES), 1) < HALF
        k = z[:, 0:ATT_KV_W]
        v = z[:, ATT_KV_W:2 * ATT_KV_W]
        if kv_out is not None:
            kv_out[0][0] = k[tm - WINDOW:, :].T
            kv_out[1][0] = v[tm - WINDOW:, :].T
        k_r = pltpu.roll(k, HALF, axis=1)
        v_r = pltpu.roll(v, HALF, axis=1)
        sc["kd0"][slot] = jnp.where(low, k, k_r).astype(BF16)
        sc["kd1"][slot] = jnp.where(low, k_r, k).astype(BF16)
        sc["vd0"][slot, :, 0:LANES] = jnp.where(low, v, 1.0).astype(BF16)
        sc["vd0"][slot, :, LANES:2 * LANES] = jnp.where(low, 1.0, v_r).astype(BF16)
        sc["vd1"][slot, :, 0:LANES] = jnp.where(low, v_r, 1.0).astype(BF16)
        sc["vd1"][slot, :, LANES:2 * LANES] = jnp.where(low, 1.0, v).astype(BF16)

    def stage_qr():
        qr = _dot(xn, win_ref[:, C_QKR:C_QKR + pw])
        sc["qrlo"][slot] = (qr * lo).astype(BF16)
        sc["qrhi"][slot] = (qr * hi).astype(BF16)

    def stage_kr():
        sc["kr"][slot] = _dot(xn, win_ref[:, C_QKR + pw:C_QKR + 2 * pw]) * (RET_QK_DIM ** -0.5)

    def stage_vr(i):
        cols = slice(i * pw, (i + 1) * pw)
        sc["vr"][slot, :, cols] = _dot(xn, win_ref[:, C_VR + i * pw:C_VR + (i + 1) * pw]).astype(BF16)

    def stage_gate(i):
        cols = slice(i * pw, (i + 1) * pw)
        sc["gate"][slot, :, cols] = _silu(_dot(xn, win_ref[:, C_GR + i * pw:C_GR + (i + 1) * pw]))

    part = functools.partial
    return [part(stage_q, 0), part(stage_q, 1), stage_kv, stage_qr, stage_kr,
            part(stage_vr, 0), part(stage_vr, 1), part(stage_gate, 0), part(stage_gate, 1)]


def _pm_last_block(slot, tm, sc):
    rows = slice(tm - BLK, tm)
    return ([sc["kd0"][slot, rows, :], sc["kd1"][slot, rows, :]],
            [sc["vd0"][slot, rows, :], sc["vd1"][slot, rows, :]])


def _pm_blocks(slot, prev_kd, prev_vd, is_first, state, fillers, tm, sinks_ref, gng_ref, gnb_ref,
               decay_ref, xi_ref, zeta_ref, sc):
    nblk = tm // BLK
    n_units = nblk * (N_KV_HEADS + N_RET_HEADS // 2)
    pending = list(fillers)
    done_units = [0]

    def unit_done():
        done_units[0] += 1
        while pending and (len(fillers) - len(pending)) * n_units < done_units[0] * len(fillers):
            pending.pop(0)()
    lowb = lax.broadcasted_iota(jnp.int32, (BLK, LANES), 1) < HALF
    col = lax.broadcasted_iota(jnp.int32, (BLK, 2 * BLK), 1)
    first_mask = None if is_first is False else jnp.where((col < BLK) & is_first, NEG, 0.0)
    kd_refs = (sc["kd0"], sc["kd1"])
    vd_refs = (sc["vd0"], sc["vd1"])
    qlo, qhi, mix = sc["qlo"], sc["qhi"], sc["mix"]
    n_pairs = N_RET_HEADS // 2

    for j in range(nblk):
        rows = slice(j * BLK, (j + 1) * BLK)
        c0s = [kvh * KV_GROUP * HEAD_DIM for kvh in range(N_KV_HEADS)]
        lsls = [slice(i * LANES, (i + 1) * LANES) for i in range(n_pairs)]
        vds, scores = [], []
        for kvh in range(N_KV_HEADS):
            if j == 0:
                kd = jnp.concatenate([prev_kd[kvh], kd_refs[kvh][slot, rows, :]], axis=0)
                vds.append(jnp.concatenate([prev_vd[kvh], vd_refs[kvh][slot, rows, :]], axis=0))
            else:
                krows = slice((j - 1) * BLK, (j + 1) * BLK)
                kd = kd_refs[kvh][slot, krows, :]
                vds.append(vd_refs[kvh][slot, krows, :])
            c0 = c0s[kvh]
            qst = jnp.concatenate([qlo[slot, rows, c0:c0 + LANES], qhi[slot, rows, c0:c0 + LANES],
                                   qlo[slot, rows, c0 + LANES:c0 + 2 * LANES],
                                   qhi[slot, rows, c0 + LANES:c0 + 2 * LANES]], axis=0)
            scores.append(_dot_nt(qst, kd))
        unit_done()

        kps = [sc["kr"][slot, rows, lsls[i]] for i in range(n_pairs)]
        vpairs = [sc["vr"][slot, rows, 2 * i * RET_V_DIM:(2 * i + 2) * RET_V_DIM] for i in range(n_pairs)]
        q2s = [jnp.concatenate([sc["qrlo"][slot, rows, lsls[i]], sc["qrhi"][slot, rows, lsls[i]]], axis=0)
               for i in range(n_pairs)]
        a_s = [_dot_nt(q2s[i], kps[i].astype(BF16)) for i in range(n_pairs)]
        ocs = [_dot(q2s[i], state[i].astype(BF16)) for i in range(n_pairs)]
        us = [_dot_tn((kps[i] * zeta_ref[:, lsls[i]]).astype(BF16), vpairs[i]) for i in range(n_pairs)]
        unit_done()

        for kvh in range(N_KV_HEADS):
            s, vd, c0 = scores[kvh], vds[kvh], c0s[kvh]
            es, esink = [], []
            for g in range(KV_GROUP):
                h = kvh * KV_GROUP + g
                sg = s[g * BLK:(g + 1) * BLK] + sc["bias"][h]
                if j == 0 and first_mask is not None:
                    sg = sg + first_mask
                sink = sinks_ref[h]
                m = jnp.maximum(jnp.max(sg, axis=-1, keepdims=True), sink)
                es.append(jnp.exp(sg - m).astype(BF16))
                esink.append(jnp.exp(sink - m))
            o = _dot(jnp.concatenate(es, axis=0), vd)
            for pair in range(KV_GROUP // 2):
                oe = o[2 * pair * BLK:(2 * pair + 1) * BLK]
                oo = o[(2 * pair + 1) * BLK:(2 * pair + 2) * BLK]
                num = jnp.where(lowb, oe[:, :LANES], oo[:, LANES:])
                den = (jnp.where(lowb, oe[:, LANES:], oo[:, :LANES])
                       + jnp.where(lowb, esink[2 * pair], esink[2 * pair + 1]))
                cs = c0 + pair * LANES
                mix[slot, rows, cs:cs + LANES] = (num * (1.0 / den)).astype(BF16)
            unit_done()

        for i in range(n_pairs):
            a, oc, u, sp = a_s[i], ocs[i], us[i], state[i]
            inner = jnp.concatenate([a[:BLK] * decay_ref[2 * i], a[BLK:] * decay_ref[2 * i + 1]], axis=0)
            oi = _dot(inner.astype(BF16), vpairs[i])
            for half in range(2):
                h = 2 * i + half
                vsl = slice(h * RET_V_DIM, (h + 1) * RET_V_DIM)
                hr = slice(half * BLK, (half + 1) * BLK)
                o = oi[hr, half * RET_V_DIM:(half + 1) * RET_V_DIM] + oc[hr] * xi_ref[:, vsl]
                r = _group_norm(o, gng_ref[:, vsl], gnb_ref[:, vsl]) * sc["gate"][slot, rows, vsl]
                mix[slot, rows, ATT_Q_W + h * RET_V_DIM:ATT_Q_W + (h + 1) * RET_V_DIM] = r.astype(BF16)
            state[i] = jnp.concatenate(
                [_GL_PROMPT[2 * i] * sp[:RET_QK_DIM] + u[:RET_QK_DIM, :RET_V_DIM],
                 _GL_PROMPT[2 * i + 1] * sp[RET_QK_DIM:] + u[RET_QK_DIM:, RET_V_DIM:]], axis=0)

    assert not pending
    return state


def _pm_wout_pieces(slot, x_ref, rows, wout_ref, h_ref, sc):
    pw = 2 * LANES
    n = D_MODEL // pw
    parts = []

    def piece(k):
        parts.append(_dot(sc["mix"][slot], wout_ref[:, k * pw:(k + 1) * pw]))
        if k == n - 1:
            h_ref[rows, :] = x_ref[...] + jnp.concatenate(parts, axis=1)

    return [functools.partial(piece, k) for k in range(n)]


def _prompt_mixer_kernel(sinks_ref, xfirst_ref, xodd_ref, xnext_ref, gmix_ref, win_ref, wout_ref, gng_ref, gnb_ref,
                           dist_ref, mask_ref, decay_ref, xi_ref, zeta_ref,
                           wupf_ref, wdnf_ref, wqf_ref, wof_ref,
                           h_ref, wk_ref, wv_ref, st_ref,
                           wupb_ref, wdnb_ref, wqb_ref, wob_ref,
                           qlo_s, qhi_s, kd0_s, kd1_s, vd0_s, vd1_s,
                           qrlo_s, qrhi_s, kr_s, vr_s, gate_s, mix_s, bias_s, state_s, xkeep_s):
    u = pl.program_id(0)
    tm = xnext_ref.shape[0]
    wupb_ref[...] = wupf_ref[...].astype(BF16)
    wdnb_ref[...] = wdnf_ref[...].astype(BF16)
    wqb_ref[...] = wqf_ref[...].astype(BF16)
    wob_ref[...] = wof_ref[...].astype(BF16)
    sc = dict(qlo=qlo_s, qhi=qhi_s, kd0=kd0_s, kd1=kd1_s, vd0=vd0_s, vd1=vd1_s, qrlo=qrlo_s, qrhi=qrhi_s,
              kr=kr_s, vr=vr_s, gate=gate_s, mix=mix_s, bias=bias_s)
    n_pairs = N_RET_HEADS // 2
    blocks = functools.partial(_pm_blocks, tm=tm, sinks_ref=sinks_ref, gng_ref=gng_ref,
                               gnb_ref=gnb_ref, decay_ref=decay_ref, xi_ref=xi_ref, zeta_ref=zeta_ref, sc=sc)

    @pl.when(u == 0)
    def _():
        for h in range(N_ATT_HEADS):
            bias_s[h] = NEG_SLOPES[h] * dist_ref[...] + mask_ref[...]
        state_s[...] = jnp.zeros_like(state_s)
        kd0_s[1] = jnp.zeros(kd0_s.shape[1:], BF16)
        kd1_s[1] = jnp.zeros(kd1_s.shape[1:], BF16)
        vd0_s[1] = jnp.zeros(vd0_s.shape[1:], BF16)
        vd1_s[1] = jnp.zeros(vd1_s.shape[1:], BF16)
        xkeep_s[...] = xfirst_ref[...]
        for stage in _pm_project_stages(xfirst_ref[...], 0, gmix_ref, win_ref, sc):
            stage()

    seq_start = (u % 2) == 0
    state = [jnp.where(seq_start, 0.0, state_s[i * LANES:(i + 1) * LANES, :]) for i in range(n_pairs)]

    prev_kd, prev_vd = _pm_last_block(1, tm, sc)
    stages = _pm_project_stages(xodd_ref[...], 1, gmix_ref, win_ref, sc, kv_out=(wk_ref, wv_ref))
    state = blocks(0, prev_kd, prev_vd, seq_start, state, stages)
    wout0 = _pm_wout_pieces(0, xkeep_s, slice(0, tm), wout_ref, h_ref, sc)

    prev_kd, prev_vd = _pm_last_block(0, tm, sc)
    stages = _pm_project_stages(xnext_ref[...], 0, gmix_ref, win_ref, sc)
    state = blocks(1, prev_kd, prev_vd, False, state, wout0 + stages)
    for piece in _pm_wout_pieces(1, xodd_ref, slice(tm, 2 * tm), wout_ref, h_ref, sc):
        piece()
    xkeep_s[...] = xnext_ref[...]

    for i in range(n_pairs):
        state_s[i * LANES:(i + 1) * LANES, :] = state[i]
        st_ref[0, i * LANES:(i + 1) * LANES, :] = state[i]


def _prompt_mixer(x, g_mix, w_in, w_out, sinks, gn_g, gn_b, side_f32):
    b, s, d = x.shape
    tm = TM_MIX
    n_tiles = b * s // tm
    steps = n_tiles // 2
    seq_steps = s // (2 * tm)
    assert s % (2 * tm) == 0 and seq_steps == 2, "kernel assumes 4 tiles per sequence"
    x2d = x.reshape(b * s, d)
    const = lambda shape: pl.BlockSpec(shape, lambda i: (0,) * len(shape), pipeline_mode=pl.Buffered(1))
    slot2 = lambda rows, cols, dt: pltpu.VMEM((2, rows, cols), dt)
    side_specs = [pl.BlockSpec((w.shape[0] // steps, w.shape[1]), lambda i: (i, 0)) for w in side_f32]
    outs = pl.pallas_call(
        _prompt_mixer_kernel,
        grid=(steps,),
        in_specs=[
            pl.BlockSpec(memory_space=pltpu.SMEM),
            const((tm, d)),
            pl.BlockSpec((tm, d), lambda i: (2 * i + 1, 0)),
            pl.BlockSpec((tm, d), lambda i: (jnp.minimum(2 * i + 2, n_tiles - 1), 0)),
            const((1, d)), const((d, D_IN)), const((MIX_OUT, d)),
            const((1, RET_V_W)), const((1, RET_V_W)),
            const((BLK, 2 * BLK)), const((BLK, 2 * BLK)),
            const((N_RET_HEADS, BLK, BLK)), const((BLK, RET_V_W)), const((BLK, RET_QK_W)),
        ] + side_specs,
        out_specs=[
            pl.BlockSpec((2 * tm, d), lambda i: (i, 0)),
            pl.BlockSpec((1, WINDOW, ATT_KV_W), lambda i: (i // seq_steps, 0, 0)),
            pl.BlockSpec((1, WINDOW, ATT_KV_W), lambda i: (i // seq_steps, 0, 0)),
            pl.BlockSpec((1, RET_QK_W, RET_V_DIM), lambda i: (i // seq_steps, 0, 0)),
        ] + side_specs,
        out_shape=[
            jax.ShapeDtypeStruct((b * s, d), F32),
            jax.ShapeDtypeStruct((b, WINDOW, ATT_KV_W), F32),
            jax.ShapeDtypeStruct((b, WINDOW, ATT_KV_W), F32),
            jax.ShapeDtypeStruct((b, RET_QK_W, RET_V_DIM), F32),
        ] + [jax.ShapeDtypeStruct(w.shape, BF16) for w in side_f32],
        scratch_shapes=[
            slot2(tm, ATT_Q_W, BF16), slot2(tm, ATT_Q_W, BF16),
            slot2(tm, LANES, BF16), slot2(tm, LANES, BF16),
            slot2(tm, 2 * LANES, BF16), slot2(tm, 2 * LANES, BF16),
            slot2(tm, RET_QK_W, BF16), slot2(tm, RET_QK_W, BF16),
            slot2(tm, RET_QK_W, F32), slot2(tm, RET_V_W, BF16),
            slot2(tm, RET_V_W, F32), slot2(tm, MIX_OUT, BF16),
            pltpu.VMEM((N_ATT_HEADS, BLK, 2 * BLK), F32),
            pltpu.VMEM((RET_QK_W, RET_V_DIM), F32),
            pltpu.VMEM((tm, d), F32),
        ],
        compiler_params=pltpu.CompilerParams(
            dimension_semantics=("arbitrary",), vmem_limit_bytes=VMEM_LIMIT),
        name="prompt_mixer",
    )(sinks, x2d, x2d, x2d, g_mix, w_in, w_out, gn_g, gn_b,
      jnp.asarray(_P_DIST), jnp.asarray(_P_MASK), jnp.asarray(_P_DECAY), jnp.asarray(_P_XI),
      jnp.asarray(_P_ZETA), *side_f32)
    return (outs[0].reshape(b, s, d),) + tuple(outs[1:])


def _memkv_kernel(mem_ref, g_ref, wk_ref, wv_ref, win_ref, wout_ref,
                  mk_ref, mv_ref, mkb_ref, mvb_ref, winb_ref, woutb_ref):
    winb_ref[...] = win_ref[...].astype(BF16)
    woutb_ref[...] = wout_ref[...].astype(BF16)
    mn = _rms(mem_ref[...], g_ref[...]).astype(BF16)
    mk = _dot(mn, wk_ref[...].astype(BF16))
    mv = _dot(mn, wv_ref[...].astype(BF16))
    tm = mem_ref.shape[0]
    group = X_D_HALVES * N_X_HEADS
    for hd in range(N_X_HEADS):
        for dh in range(X_D_HALVES):
            cols = slice(hd * X_HEAD_DIM + dh * LANES, hd * X_HEAD_DIM + (dh + 1) * LANES)
            rows = pl.ds(dh * N_X_HEADS + hd, tm, stride=group)
            mk_ref[rows, :] = mk[:, cols]
            mv_ref[rows, :] = mv[:, cols]
    mkb_ref[...] = mk.astype(BF16)
    mvb_ref[...] = mv.astype(BF16)


def _memory_kv(mem2d, g_mem, w_xk, w_xv, w_in, w_out):
    n, d = mem2d.shape
    tm = 512
    row = pl.BlockSpec((tm, d), lambda i: (i, 0))
    rows_out = pl.BlockSpec((tm * d // LANES, LANES), lambda i: (i, 0))
    const = lambda shape: pl.BlockSpec(shape, lambda i: (0,) * len(shape))
    steps = n // tm
    win_blk = pl.BlockSpec((w_in.shape[0] // steps, w_in.shape[1]), lambda i: (i, 0))
    wout_blk = pl.BlockSpec((w_out.shape[0] // steps, w_out.shape[1]), lambda i: (i, 0))
    deep = lambda spec: pl.BlockSpec(spec.block_shape, spec.index_map, pipeline_mode=pl.Buffered(3))
    in_specs = [deep(row), const((1, d)), const((d, d)), const((d, d)), deep(win_blk), deep(wout_blk)]
    out_specs = [rows_out, rows_out, row, row, win_blk, wout_blk]

    def outer(*refs):
        pltpu.emit_pipeline(_memkv_kernel, grid=(steps,), in_specs=in_specs, out_specs=out_specs)(*refs)

    hbm = pl.BlockSpec(memory_space=pl.ANY)
    return pl.pallas_call(
        outer,
        in_specs=[hbm] * 6,
        out_specs=[hbm] * 6,
        out_shape=[jax.ShapeDtypeStruct((n * d // LANES, LANES), F32),
                   jax.ShapeDtypeStruct((n * d // LANES, LANES), F32),
                   jax.ShapeDtypeStruct((n, d), BF16), jax.ShapeDtypeStruct((n, d), BF16),
                   jax.ShapeDtypeStruct(w_in.shape, BF16), jax.ShapeDtypeStruct(w_out.shape, BF16)],
        compiler_params=pltpu.CompilerParams(vmem_limit_bytes=VMEM_LIMIT),
        name="memory_kv",
    )(mem2d, g_mem, w_xk, w_xv, w_in, w_out)


def _prompt_xattn_kernel(h_ref, g_ref, wq_ref, wo_ref, mk_ref, mv_ref, out_ref, o_s):
    def stages(r0):
        rows = slice(r0, r0 + SUB_ROWS)
        env = {}

        def project():
            env["h"] = h_ref[0, rows, :]
            xn = _rms(env["h"], g_ref[...]).astype(BF16)
            env["q"] = (_dot(xn, wq_ref[...]) * (X_HEAD_DIM ** -0.5)).astype(BF16)

        def scores(hd):
            sl = slice(hd * X_HEAD_DIM, (hd + 1) * X_HEAD_DIM)
            env[hd] = _dot_nt(env["q"][:, sl], mk_ref[0, :, sl])

        def head(hd):
            if hd + 1 < N_X_HEADS:
                scores(hd + 1)
            sl = slice(hd * X_HEAD_DIM, (hd + 1) * X_HEAD_DIM)
            s = env.pop(hd)
            m = jnp.max(s, axis=-1, keepdims=True)
            p = jnp.exp(s - m)
            p = p * (1.0 / jnp.sum(p, axis=-1, keepdims=True))
            o_s[rows, sl] = _dot(p.astype(BF16), mv_ref[0, :, sl]).astype(BF16)

        def output():
            out_ref[0, rows, :] = env["h"] + _dot(o_s[rows, :], wo_ref[...])

        def project_and_first_scores():
            project()
            scores(0)

        return ([project_and_first_scores] + [functools.partial(head, hd) for hd in range(N_X_HEADS)]
                + [output])

    chains = [stages(r0) for r0 in range(0, h_ref.shape[1], SUB_ROWS)]
    n_stage = len(chains[0])
    for step in range(n_stage + len(chains) - 1):
        for lag, chain in enumerate(chains):
            if 0 <= step - lag < n_stage:
                chain[step - lag]()


def _prompt_xattn(h, g, w_xq, w_xo, mkb, mvb):
    b, s, d = h.shape
    tm = TM_X
    const = lambda shape: pl.BlockSpec(shape, lambda i, j: (0,) * len(shape))
    tok = pl.BlockSpec((1, tm, d), lambda i, j: (i, j, 0))
    mem = pl.BlockSpec((1, N_MEM, d), lambda i, j: (i, 0, 0))
    return pl.pallas_call(
        _prompt_xattn_kernel,
        grid=(b, s // tm),
        in_specs=[tok, const((1, d)), const((d, d)), const((d, d)), mem, mem],
        out_specs=tok,
        out_shape=jax.ShapeDtypeStruct((b, s, d), F32),
        scratch_shapes=[pltpu.VMEM((tm, d), BF16)],
        compiler_params=pltpu.CompilerParams(
            dimension_semantics=("arbitrary", "arbitrary"), vmem_limit_bytes=VMEM_LIMIT),
        name="prompt_xattn",
    )(h, g, w_xq, w_xo, mkb, mvb)


def _sample_mixer_kernel(sinks_ref, x_ref, gmix_ref, win_ref, wout_ref, gng_ref, gnb_ref,
                         ck_ref, cv_ref, st_ref, bias_ref, dec_ref, xi_ref, zeta_ref,
                         h_ref, swk_ref, swv_ref, sst_ref):
    bb = ck_ref.shape[0]
    nt = bb // 2
    x = x_ref[...].reshape(bb * DEC_SEQ, D_MODEL)
    xn = _rms(x, gmix_ref[...]).astype(BF16)
    tile3 = lambda a: a.reshape(nt, SUBLANES, a.shape[-1])

    q = _dot(xn, win_ref[:, C_QA:C_QA + ATT_Q_W]) * (HEAD_DIM ** -0.5)
    kv = _dot(xn, win_ref[:, C_KV:C_KV + 2 * ATT_KV_W])
    qkr = _dot(xn, win_ref[:, C_QKR:C_QKR + 2 * RET_QK_W])
    vr = _dot(xn, win_ref[:, C_VR:C_VR + RET_V_W])
    gate3 = tile3(_silu(_dot(xn, win_ref[:, C_GR:C_GR + RET_V_W])))

    lo512, hi512 = _half_masks(ATT_Q_W)
    q_r = pltpu.roll(q, HALF, axis=1)
    q_nat3 = tile3(q)
    q_rot3 = tile3(q_r)
    lo3 = lo512.reshape(1, 1, ATT_Q_W)
    hi3 = hi512.reshape(1, 1, ATT_Q_W)
    qa3 = (q_nat3 * lo3).astype(BF16)
    qb3 = (q_rot3 * lo3).astype(BF16)
    qc3 = (q_rot3 * hi3).astype(BF16)
    qd3 = (q_nat3 * hi3).astype(BF16)
    t128 = lambda a, i: a[:, :, i * LANES:(i + 1) * LANES]
    qs = jnp.concatenate([t128(qa3, 0), t128(qb3, 1), t128(qa3, 1), t128(qb3, 2),
                          t128(qc3, 2), t128(qd3, 2), t128(qc3, 3), t128(qd3, 3)], axis=1)

    k3 = tile3(kv[:, :ATT_KV_W])
    v3 = tile3(kv[:, ATT_KV_W:])
    pad_kv = jnp.zeros((nt, BLK - SUBLANES, LANES), BF16)
    knew_pad = jnp.concatenate([k3.astype(BF16), pad_kv], axis=1)
    vnew_pad = jnp.concatenate([v3.astype(BF16), pad_kv], axis=1)
    to_lanes = lambda a3: jnp.swapaxes(
        jnp.concatenate([a3, jnp.zeros((nt, BLK - SUBLANES, LANES), F32)], axis=1), 1, 2)
    k3t, v3t = to_lanes(k3), to_lanes(v3)
    roll3 = lambda a, sh: pltpu.roll(a.reshape(nt * BLK, LANES), sh, axis=1).reshape(nt, BLK, LANES)

    lo256, _ = _half_masks(RET_QK_W)
    qr3 = tile3(qkr[:, :RET_QK_W])
    kr3 = tile3(qkr[:, RET_QK_W:] * (RET_QK_DIM ** -0.5))
    vr3 = tile3(vr)
    lane256 = lax.broadcasted_iota(jnp.int32, (1, 1, RET_QK_W), 2)
    qrs = jnp.concatenate(
        [(qr3 * ((lane256 >= h * RET_QK_DIM) & (lane256 < (h + 1) * RET_QK_DIM)).astype(F32)).astype(BF16)
         for h in range(N_RET_HEADS)],
        axis=1)
    kr_pad = jnp.concatenate([kr3.astype(BF16), jnp.zeros((nt, BLK - SUBLANES, RET_QK_W), BF16)], axis=1)
    vr_pad = jnp.concatenate([vr3.astype(BF16), jnp.zeros((nt, BLK - SUBLANES, RET_V_W), BF16)], axis=1)

    lane = lax.broadcasted_iota(jnp.int32, (1, 1, LANES), 2)
    row8 = lax.broadcasted_iota(jnp.int32, (1, SUBLANES, 1), 1)
    bmm_nt = lambda a, b: jnp.einsum('bqd,bkd->bqk', a, b, preferred_element_type=F32)
    bmm = lambda a, b: jnp.einsum('bqk,bkd->bqd', a, b, preferred_element_type=F32)

    att_par, ret_par = [], []
    for par in range(2):
        bsl = pl.ds(par, nt, stride=2)
        ckt = ck_ref[bsl]
        cvt = cv_ref[bsl]
        keep = lane < WINDOW - DEC_SEQ
        new_shift = WINDOW - DEC_SEQ - DEC_SEQ * par
        swk_ref[bsl] = jnp.where(keep, roll3(ckt, WINDOW - DEC_SEQ), roll3(k3t, new_shift))
        swv_ref[bsl] = jnp.where(keep, roll3(cvt, WINDOW - DEC_SEQ), roll3(v3t, new_shift))

        s = jnp.concatenate([bmm(qs, ckt.astype(BF16)), bmm_nt(qs, knew_pad)], axis=2) + bias_ref[par]
        ps = []
        for h in range(N_ATT_HEADS):
            ps.append(_sink_softmax(s[:, h * SUBLANES:(h + 1) * SUBLANES, :], sinks_ref[h]).astype(BF16))
        p_all = jnp.concatenate(ps, axis=1)
        o = bmm_nt(p_all[:, :, :BLK], cvt.astype(BF16)) + bmm(p_all[:, :, BLK:], vnew_pad)
        o_r = pltpu.roll(o.reshape(nt * N_ATT_HEADS * SUBLANES, LANES), HALF, axis=1).reshape(o.shape)
        hr = lambda a, h: a[:, h * SUBLANES:(h + 1) * SUBLANES, :]
        low = lane < HALF
        att_par.append(jnp.concatenate([
            jnp.where(low, hr(o, 0), hr(o_r, 1)), jnp.where(low, hr(o, 2), hr(o_r, 3)),
            jnp.where(low, hr(o_r, 4), hr(o, 5)), jnp.where(low, hr(o_r, 6), hr(o, 7))], axis=2))

        st = st_ref[bsl]
        oc = bmm(qrs, st.astype(BF16))
        inner = (bmm_nt(qrs, kr_pad) * dec_ref[par]).astype(BF16)
        oi = bmm(inner, vr_pad)
        rs = []
        for h in range(N_RET_HEADS):
            vsl = slice(h * RET_V_DIM, (h + 1) * RET_V_DIM)
            rsl = slice(h * SUBLANES, (h + 1) * SUBLANES)
            o_h = oi[:, rsl, vsl] + oc[:, rsl, :] * xi_ref[par, rsl, :]
            rs.append(_group_norm(o_h, gng_ref[:, vsl], gnb_ref[:, vsl]) * gate3[:, :, vsl])
        ret_par.append(jnp.concatenate(rs, axis=2))

        kz3 = (kr3 * zeta_ref[par]).astype(BF16)
        vr3_b = vr3.astype(BF16)
        for p in range(nt):
            for i in range(N_RET_HEADS // 2):
                u = _dot_tn(kz3[p][:, i * LANES:(i + 1) * LANES],
                            vr3_b[p][:, 2 * i * RET_V_DIM:(2 * i + 2) * RET_V_DIM])
                for half in range(2):
                    h = 2 * i + half
                    dsl = slice(h * RET_QK_DIM, (h + 1) * RET_QK_DIM)
                    sst_ref[2 * p + par, dsl, :] = (
                        _GL_SAMPLE[h] * st[p, dsl, :]
                        + u[half * RET_QK_DIM:(half + 1) * RET_QK_DIM, half * RET_V_DIM:(half + 1) * RET_V_DIM])

    own0 = row8 < DEC_SEQ
    att3 = jnp.where(own0, att_par[0], att_par[1])
    ret3 = jnp.where(own0, ret_par[0], ret_par[1])
    mix = jnp.concatenate([att3, ret3], axis=2).reshape(2 * nt * DEC_SEQ, MIX_OUT).astype(BF16)
    h_ref[...] = x + _dot(mix, wout_ref[...])


def _sample_mixer(x3d, g_mix, w_in, w_out, sinks, gn_g, gn_b, ck, cv, st):
    nb, ls, d = x3d.shape
    n = nb * ls
    bb = BB_MIX
    r = bb * DEC_SEQ
    const = lambda shape: pl.BlockSpec(shape, lambda i: (0,) * len(shape))
    row = pl.BlockSpec((r, d), lambda i: (i, 0))
    win = pl.BlockSpec((bb, WINDOW, ATT_KV_W), lambda i: (i, 0, 0))
    state = pl.BlockSpec((bb, RET_QK_W, RET_V_DIM), lambda i: (i, 0, 0))
    return pl.pallas_call(
        _sample_mixer_kernel,
        grid=(nb // bb,),
        in_specs=[
            pl.BlockSpec(memory_space=pltpu.SMEM),
            pl.BlockSpec((bb, ls, d), lambda i: (i, 0, 0)), const((1, d)), const((d, D_IN)), const((MIX_OUT, d)),
            const((1, RET_V_W)), const((1, RET_V_W)),
            win, win, state,
            const(_S_BIAS.shape), const(_S_DEC.shape), const(_S_XI.shape), const(_S_ZETA.shape),
        ],
        out_specs=[row, win, win, state],
        out_shape=[
            jax.ShapeDtypeStruct((n, d), F32),
            jax.ShapeDtypeStruct((nb, WINDOW, ATT_KV_W), F32),
            jax.ShapeDtypeStruct((nb, WINDOW, ATT_KV_W), F32),
            jax.ShapeDtypeStruct((nb, RET_QK_W, RET_V_DIM), F32),
        ],
        compiler_params=pltpu.CompilerParams(
            dimension_semantics=("arbitrary",), vmem_limit_bytes=VMEM_LIMIT),
        name="sample_mixer",
    )(sinks, x3d, g_mix, w_in, w_out, gn_g, gn_b, ck, cv, st,
      jnp.asarray(_S_BIAS), jnp.asarray(_S_DEC), jnp.asarray(_S_XI), jnp.asarray(_S_ZETA))


def _head_slab(x_ref, b, hd):
    group = X_D_HALVES * N_X_HEADS
    halves = [x_ref[b, pl.ds(dh * N_X_HEADS + hd, N_MEM, stride=group), :] for dh in range(X_D_HALVES)]
    return jnp.concatenate(halves, axis=1).astype(BF16)


def _mlp_value(h, g_ref, wup_ref, wdn_ref, gf_ref, fillers=None):
    xn = _rms(h, g_ref[...]).astype(BF16)
    piece = FF_CHUNK // N_X_HEADS
    opiece = D_MODEL // N_X_HEADS
    n_chunks = D_FF // FF_CHUNK
    nofill = (None, None, None)

    def up(c):
        qk, softmax, _ = fillers[c] if fillers is not None else nofill
        hid = []
        for k in range(N_X_HEADS):
            cols = slice(c * FF_CHUNK + k * piece, c * FF_CHUNK + (k + 1) * piece)
            u = jnp.maximum(_dot(xn, wup_ref[:, cols]), 0.0)
            hid.append((u * u).astype(BF16))
            if qk is not None:
                qk(k)
        if softmax is not None:
            softmax()
        return jnp.concatenate(hid, axis=1)

    def down(c, hid):
        pv = (fillers[c] if fillers is not None else nofill)[2]
        rows_c = slice(c * FF_CHUNK, (c + 1) * FF_CHUNK)
        out = []
        for k in range(N_X_HEADS):
            out.append(_dot(hid, wdn_ref[rows_c, k * opiece:(k + 1) * opiece]))
            if pv is not None:
                pv(k)
        return jnp.concatenate(out, axis=1)

    acc = h
    hid = up(0)
    for c in range(n_chunks):
        nxt = up(c + 1) if c + 1 < n_chunks else None
        acc = acc + down(c, hid)
        hid = nxt
    return _rms(acc, gf_ref[...])


def _mlp_xattn_kernel(hp_ref, hsm_ref, gx_ref, wq_ref, wo_ref, xk_ref, xv_ref, g_ref, wup_ref, wdn_ref, gf_ref,
                      yp_ref, ys_ref):
    i = pl.program_id(0)
    n = pl.num_programs(0) - 1
    bb = xk_ref.shape[0]
    assert bb == D_FF // FF_CHUNK and bb % 2 == 0

    @pl.when(i == 0)
    def _():
        xn = _rms(hsm_ref[...], gx_ref[...]).astype(BF16)
        ys_ref[...] = (_dot(xn, wq_ref[...]) * (X_HEAD_DIM ** -0.5)).reshape(ys_ref.shape)

    @pl.when(i < n)
    def _():
        own0 = lax.broadcasted_iota(jnp.int32, (SUBLANES, 1), 0) < DEC_SEQ
        o_rows = {}

        def attend(b):
            t = b // 2
            tile_b = pl.ds(i * bb + 2 * t, 2)
            env = dict(s=[], o=[])

            def qk(hd):
                if hd == 0:
                    env["q"] = ys_ref[tile_b].reshape(SUBLANES, D_MODEL).astype(BF16)
                env["s"].append(_dot_nt(env["q"][:, hd * X_HEAD_DIM:(hd + 1) * X_HEAD_DIM],
                                        _head_slab(xk_ref, b, hd)))

            def softmax():
                s = jnp.concatenate(env["s"], axis=0)
                m = jnp.max(s, axis=-1, keepdims=True)
                p = jnp.exp(s - m)
                env["p"] = p * (1.0 / jnp.sum(p, axis=-1, keepdims=True))

            def pv(hd):
                p = env["p"][hd * SUBLANES:(hd + 1) * SUBLANES].astype(BF16)
                env["o"].append(_dot(p, _head_slab(xv_ref, b, hd)))
                if hd == N_X_HEADS - 1:
                    o_rows[b] = jnp.concatenate(env["o"], axis=1)
                    if b % 2 == 1:
                        ys_ref[tile_b] = jnp.where(own0, o_rows[b - 1], o_rows[b]).reshape(2, DEC_SEQ, D_MODEL)

            return qk, softmax, pv

        fillers = [attend(b) for b in range(bb)]
        yp_ref[...] = _mlp_value(hp_ref[...], g_ref, wup_ref, wdn_ref, gf_ref, fillers)

    @pl.when(i == n)
    def _():
        o = ys_ref[...].reshape(hsm_ref.shape).astype(BF16)
        hs = hsm_ref[...] + _dot(o, wo_ref[...])
        ys_ref[...] = _mlp_value(hs, g_ref, wup_ref, wdn_ref, gf_ref).reshape(ys_ref.shape)


def _mlp_xattn(hp2d, hsm, g_xattn, w_xq, w_xo, xk, xv, g_mlp, w_up, w_down, g_final):
    n, d = hp2d.shape
    ns = hsm.shape[0]
    nb = xk.shape[0]
    bb = BB_X
    tm = n // (nb // bb)
    n_tiles = n // tm
    assert n_tiles * bb == nb and tm % SUBLANES == 0
    clip = lambda i: jnp.minimum(i, n_tiles - 1)
    prompt = pl.BlockSpec((tm, d), lambda i: (clip(i), 0))
    mem = pl.BlockSpec((bb,) + xk.shape[1:], lambda i: (clip(i), 0, 0))
    const = lambda shape: pl.BlockSpec(shape, lambda i: (0,) * len(shape), pipeline_mode=pl.Buffered(1))
    return pl.pallas_call(
        _mlp_xattn_kernel,
        grid=(n_tiles + 1,),
        in_specs=[prompt, const((ns, d)), const((1, d)), const((d, d)), const((d, d)), mem, mem,
                  const((1, d)), const((d, D_FF)), const((D_FF, d)), const((1, d))],
        out_specs=[prompt, pl.BlockSpec((ns // DEC_SEQ, DEC_SEQ, d), lambda i: (0, 0, 0))],
        out_shape=[jax.ShapeDtypeStruct((n, d), F32), jax.ShapeDtypeStruct((ns // DEC_SEQ, DEC_SEQ, d), F32)],
        compiler_params=pltpu.CompilerParams(
            dimension_semantics=("arbitrary",), vmem_limit_bytes=VMEM_LIMIT),
        name="mlp_xattn",
    )(hp2d, hsm, g_xattn, w_xq, w_xo, xk, xv, g_mlp, w_up, w_down, g_final)


def _mem_rows(c):
    nb = c.shape[0]
    c = c.reshape(nb, N_MEM, N_X_HEADS, X_D_HALVES, LANES)
    return jnp.transpose(c, (0, 1, 3, 2, 4)).reshape(nb, N_MEM * X_D_HALVES * N_X_HEADS, LANES)


def kernel(x_prompt, x_sample, mem_prompt, cache_win_k, cache_win_v, state_ret, cache_mem_k, cache_mem_v,
           g_mix, w_in, attn_sinks, ret_gn_g, ret_gn_b, w_out, g_xattn, g_mem, w_xq, w_xk, w_xv, w_xo,
           g_mlp, w_up, w_down, g_final):
    depth = w_in.shape[0]
    assert depth == 1, "single-layer trunk"
    b, s, d = x_prompt.shape
    nb, ls, _ = x_sample.shape
    row = lambda a: a.reshape(1, -1)
    sinks = attn_sinks[0]
    gn_g, gn_b = row(ret_gn_g[0]), row(ret_gn_b[0])
    g_fin = row(g_final)

    mk, mv, mkb, mvb, w_in_b, w_out_b = _memory_kv(
        mem_prompt.reshape(b * N_MEM, d), row(g_mem[0]), w_xk[0], w_xv[0], w_in[0], w_out[0])
    hp, p_wk, p_wv, p_rs, w_up_b, w_dn_b, w_xq_b, w_xo_b = _prompt_mixer(
        x_prompt, row(g_mix[0]), w_in_b, w_out_b, sinks, gn_g, gn_b,
        (w_up[0], w_down[0], w_xq[0], w_xo[0]))
    hp = _prompt_xattn(hp, row(g_xattn[0]), w_xq_b, w_xo_b,
                       mkb.reshape(b, N_MEM, d), mvb.reshape(b, N_MEM, d))

    win_t = lambda c: jnp.transpose(c, (0, 2, 3, 1)).reshape(nb, ATT_KV_W, WINDOW)
    win_t_inv = lambda a: jnp.transpose(a.reshape(-1, N_KV_HEADS, HEAD_DIM, WINDOW),
                                        (0, 3, 1, 2)).reshape(1, -1, WINDOW, N_KV_HEADS, HEAD_DIM)
    hs, s_wk, s_wv, s_rs = _sample_mixer(
        x_sample, row(g_mix[0]), w_in_b, w_out_b, sinks, gn_g, gn_b,
        win_t(cache_win_k[0]), win_t(cache_win_v[0]), state_ret[0].reshape(nb, RET_QK_W, RET_V_DIM))

    y_prompt, y_sample = _mlp_xattn(
        hp.reshape(b * s, d), hs, row(g_xattn[0]), w_xq_b, w_xo_b,
        _mem_rows(cache_mem_k[0]), _mem_rows(cache_mem_v[0]), row(g_mlp[0]), w_up_b, w_dn_b, g_fin)
    y_prompt = y_prompt.reshape(b, s, d)

    ret5 = lambda a, n: a.reshape(1, n, N_RET_HEADS, RET_QK_DIM, RET_V_DIM)
    mem5 = lambda a: jnp.transpose(a.reshape(b, N_MEM, X_D_HALVES, N_X_HEADS, LANES),
                                   (0, 1, 3, 2, 4)).reshape(1, b, N_MEM, N_X_HEADS, X_HEAD_DIM)
    return (y_prompt, y_sample,
            win_t_inv(p_wk), win_t_inv(p_wv), ret5(p_rs, b), mem5(mk), mem5(mv),
            win_t_inv(s_wk), win_t_inv(s_wv), ret5(s_rs, nb))
```

```python
import functools

import jax
import jax.numpy as jnp
import numpy as np
from jax import lax
from jax.experimental import pallas as pl
from jax.experimental.pallas import tpu as pltpu

F32 = jnp.float32
BF16 = jnp.bfloat16

D_MODEL = 1024
BATCH = 8
SEQ = 2048
DEC_BATCH = 128
DEC_SEQ = 4
HEAD_DIM = 64
N_ATT_HEADS = 8
N_KV_HEADS = 2
KV_GROUP = N_ATT_HEADS // N_KV_HEADS
WINDOW = 128
BLK = 128
N_RET_HEADS = 4
RET_QK_DIM = 64
RET_V_DIM = 128
N_MEM = 256
N_X_HEADS = 4
X_HEAD_DIM = D_MODEL // N_X_HEADS
D_FF = 4 * D_MODEL
RMS_EPS = 1e-6
GN_EPS = 1e-5

ATT_Q_W = N_ATT_HEADS * HEAD_DIM
ATT_KV_W = N_KV_HEADS * HEAD_DIM
RET_QK_W = N_RET_HEADS * RET_QK_DIM
RET_V_W = N_RET_HEADS * RET_V_DIM
MIX_OUT = ATT_Q_W + RET_V_W
D_IN = ATT_Q_W + 2 * ATT_KV_W + 2 * RET_QK_W + 2 * RET_V_W
C_QA, C_KV, C_QKR, C_VR, C_GR = 0, 512, 768, 1280, 1792

LANES = 128
SUBLANES = 8
HALF = LANES // 2
X_D_HALVES = X_HEAD_DIM // LANES
NEG = -1e30
VMEM_LIMIT = 56 * 1024 * 1024

TM_MIX = 512
TM_X = 2048
SUB_ROWS = 512
FF_CHUNK = 1024
BB_MIX = 32
BB_X = 4

NEG_SLOPES = [-(2.0 ** (-8.0 * (i + 1) / N_ATT_HEADS)) for i in range(N_ATT_HEADS)]
_LOG_G = np.log(1.0 - 2.0 ** (-5.0 - np.arange(N_RET_HEADS))).astype(np.float32).astype(np.float64)


def _prompt_tables():
    qi = np.arange(BLK)[:, None]
    kj = np.arange(2 * BLK)[None, :]
    dist = (qi + BLK - kj).astype(np.float64)
    mask = np.where((dist >= 0) & (dist < WINDOW), 0.0, NEG)
    l = np.arange(BLK, dtype=np.float64)
    diff = l[:, None] - l[None, :]
    decay = np.where(diff >= 0, np.exp(_LOG_G[:, None, None] * np.maximum(diff, 0.0)), 0.0)
    xi = np.exp((l[:, None] + 1.0) * _LOG_G[None, :])
    zeta = np.exp((BLK - 1.0 - l)[:, None] * _LOG_G[None, :])
    xi_t = np.repeat(xi, RET_V_DIM, axis=1)
    zeta_t = np.repeat(zeta, RET_QK_DIM, axis=1)
    f = lambda a: np.asarray(a, np.float32)
    return f(dist), f(mask), f(decay), f(xi_t), f(zeta_t)


def _sample_tables():
    slopes = -np.asarray(NEG_SLOPES)
    bias = np.full((2, N_ATT_HEADS * SUBLANES, 2 * BLK), NEG, np.float64)
    dec = np.zeros((2, N_RET_HEADS * SUBLANES, BLK), np.float64)
    xi = np.zeros((2, N_RET_HEADS * SUBLANES, RET_V_DIM), np.float64)
    zeta = np.zeros((2, SUBLANES, RET_QK_W), np.float64)
    for par in range(2):
        for r in range(SUBLANES):
            own = DEC_SEQ * par <= r < DEC_SEQ * (par + 1)
            t = r - DEC_SEQ * par if own else r % DEC_SEQ
            for h in range(N_ATT_HEADS):
                row = h * SUBLANES + r
                for j in range(WINDOW):
                    d = t + WINDOW - j
                    if 0 <= d < WINDOW:
                        bias[par, row, j] = -slopes[h] * d
                for c in range(DEC_SEQ):
                    d = t - c
                    if d >= 0:
                        bias[par, row, WINDOW + DEC_SEQ * par + c] = -slopes[h] * d
            for h in range(N_RET_HEADS):
                row = h * SUBLANES + r
                if own:
                    xi[par, row, :] = np.exp((t + 1.0) * _LOG_G[h])
                    zeta[par, r, h * RET_QK_DIM:(h + 1) * RET_QK_DIM] = np.exp((DEC_SEQ - 1.0 - t) * _LOG_G[h])
                    for c in range(t + 1):
                        dec[par, row, DEC_SEQ * par + c] = np.exp(_LOG_G[h] * (t - c))
    f = lambda a: np.asarray(a, np.float32)
    return f(bias), f(dec), f(xi), f(zeta)


_P_DIST, _P_MASK, _P_DECAY, _P_XI, _P_ZETA = _prompt_tables()
_S_BIAS, _S_DEC, _S_XI, _S_ZETA = _sample_tables()
_GL_PROMPT = [float(np.exp(_LOG_G[h] * BLK)) for h in range(N_RET_HEADS)]
_GL_SAMPLE = [float(np.exp(_LOG_G[h] * DEC_SEQ)) for h in range(N_RET_HEADS)]


def _rms(x, g):
    return x * lax.rsqrt(jnp.mean(x * x, axis=-1, keepdims=True) + RMS_EPS) * g


def _dot(a, b):
    return jnp.dot(a, b, preferred_element_type=F32)


def _dot_nt(a, b):
    return lax.dot_general(a, b, (((1,), (1,)), ((), ())), preferred_element_type=F32)


def _dot_tn(a, b):
    return lax.dot_general(a, b, (((0,), (0,)), ((), ())), preferred_element_type=F32)


def _silu(g):
    return g * (1.0 / (1.0 + jnp.exp(-g)))


def _half_masks(width):
    lane = lax.broadcasted_iota(jnp.int32, (1, width), 1)
    lo = ((lane & (LANES - 1)) < HALF).astype(F32)
    return lo, 1.0 - lo


def _sink_softmax(s, sink):
    m = jnp.maximum(jnp.max(s, axis=-1, keepdims=True), sink)
    p = jnp.exp(s - m)
    den = jnp.sum(p, axis=-1, keepdims=True) + jnp.exp(sink - m)
    return p * (1.0 / den)


def _group_norm(o, g, b):
    mu = jnp.mean(o, axis=-1, keepdims=True)
    d = o - mu
    var = jnp.mean(d * d, axis=-1, keepdims=True)
    return d * lax.rsqrt(var + GN_EPS) * g + b


def _pm_project_stages(x, slot, gmix_ref, win_ref, sc, kv_out=None):
    tm = x.shape[0]
    xn = _rms(x, gmix_ref[...]).astype(BF16)

    pw = 2 * LANES
    lo, hi = _half_masks(pw)

    def stage_q(i):
        cols = slice(i * pw, (i + 1) * pw)
        q = _dot(xn, win_ref[:, C_QA + i * pw:C_QA + (i + 1) * pw])
        sc["qlo"][slot, :, cols] = (q * (lo * HEAD_DIM ** -0.5)).astype(BF16)
        sc["qhi"][slot, :, cols] = (q * (hi * HEAD_DIM ** -0.5)).astype(BF16)

    def stage_kv():
        z = _dot(xn, win_ref[:, C_KV:C_KV + pw])
        low = lax.broadcasted_iota(jnp.int32, (tm, LANES), 1) < HALF
        k = z[:, 0:ATT_KV_W]
        v = z[:, ATT_KV_W:2 * ATT_KV_W]
        if kv_out is not None:
            kv_out[0][0] = k[tm - WINDOW:, :].T
            kv_out[1][0] = v[tm - WINDOW:, :].T
        k_r = pltpu.roll(k, HALF, axis=1)
        v_r = pltpu.roll(v, HALF, axis=1)
        sc["kd0"][slot] = jnp.where(low, k, k_r).astype(BF16)
        sc["kd1"][slot] = jnp.where(low, k_r, k).astype(BF16)
        sc["vd0"][slot, :, 0:LANES] = jnp.where(low, v, 1.0).astype(BF16)
        sc["vd0"][slot, :, LANES:2 * LANES] = jnp.where(low, 1.0, v_r).astype(BF16)
        sc["vd1"][slot, :, 0:LANES] = jnp.where(low, v_r, 1.0).astype(BF16)
        sc["vd1"][slot, :, LANES:2 * LANES] = jnp.where(low, 1.0, v).astype(BF16)

    def stage_qr():
        qr = _dot(xn, win_ref[:, C_QKR:C_QKR + pw])
        sc["qrlo"][slot] = (qr * lo).astype(BF16)
        sc["qrhi"][slot] = (qr * hi).astype(BF16)

    def stage_kr():
        sc["kr"][slot] = _dot(xn, win_ref[:, C_QKR + pw:C_QKR + 2 * pw]) * (RET_QK_DIM ** -0.5)

    def stage_vr(i):
        cols = slice(i * pw, (i + 1) * pw)
        sc["vr"][slot, :, cols] = _dot(xn, win_ref[:, C_VR + i * pw:C_VR + (i + 1) * pw]).astype(BF16)

    def stage_gate(i):
        cols = slice(i * pw, (i + 1) * pw)
        sc["gate"][slot, :, cols] = _silu(_dot(xn, win_ref[:, C_GR + i * pw:C_GR + (i + 1) * pw]))

    part = functools.partial
    return [part(stage_q, 0), part(stage_q, 1), stage_kv, stage_qr, stage_kr,
            part(stage_vr, 0), part(stage_vr, 1), part(stage_gate, 0), part(stage_gate, 1)]


def _pm_last_block(slot, tm, sc):
    rows = slice(tm - BLK, tm)
    return ([sc["kd0"][slot, rows, :], sc["kd1"][slot, rows, :]],
            [sc["vd0"][slot, rows, :], sc["vd1"][slot, rows, :]])


def _pm_blocks(slot, prev_kd, prev_vd, is_first, state, fillers, tm, sinks_ref, gng_ref, gnb_ref,
               decay_ref, xi_ref, zeta_ref, sc):
    nblk = tm // BLK
    n_units = nblk * (N_KV_HEADS + N_RET_HEADS // 2)
    pending = list(fillers)
    done_units = [0]

    def unit_done():
        done_units[0] += 1
        while pending and (len(fillers) - len(pending)) * n_units < done_units[0] * len(fillers):
            pending.pop(0)()
    lowb = lax.broadcasted_iota(jnp.int32, (BLK, LANES), 1) < HALF
    col = lax.broadcasted_iota(jnp.int32, (BLK, 2 * BLK), 1)
    first_mask = None if is_first is False else jnp.where((col < BLK) & is_first, NEG, 0.0)
    kd_refs = (sc["kd0"], sc["kd1"])
    vd_refs = (sc["vd0"], sc["vd1"])
    qlo, qhi, mix = sc["qlo"], sc["qhi"], sc["mix"]
    n_pairs = N_RET_HEADS // 2

    for j in range(nblk):
        rows = slice(j * BLK, (j + 1) * BLK)
        c0s = [kvh * KV_GROUP * HEAD_DIM for kvh in range(N_KV_HEADS)]
        lsls = [slice(i * LANES, (i + 1) * LANES) for i in range(n_pairs)]
        vds, scores = [], []
        for kvh in range(N_KV_HEADS):
            if j == 0:
                kd = jnp.concatenate([prev_kd[kvh], kd_refs[kvh][slot, rows, :]], axis=0)
                vds.append(jnp.concatenate([prev_vd[kvh], vd_refs[kvh][slot, rows, :]], axis=0))
            else:
                krows = slice((j - 1) * BLK, (j + 1) * BLK)
                kd = kd_refs[kvh][slot, krows, :]
                vds.append(vd_refs[kvh][slot, krows, :])
            c0 = c0s[kvh]
            qst = jnp.concatenate([qlo[slot, rows, c0:c0 + LANES], qhi[slot, rows, c0:c0 + LANES],
                                   qlo[slot, rows, c0 + LANES:c0 + 2 * LANES],
                                   qhi[slot, rows, c0 + LANES:c0 + 2 * LANES]], axis=0)
            scores.append(_dot_nt(qst, kd))
        unit_done()

        kps = [sc["kr"][slot, rows, lsls[i]] for i in range(n_pairs)]
        vpairs = [sc["vr"][slot, rows, 2 * i * RET_V_DIM:(2 * i + 2) * RET_V_DIM] for i in range(n_pairs)]
        q2s = [jnp.concatenate([sc["qrlo"][slot, rows, lsls[i]], sc["qrhi"][slot, rows, lsls[i]]], axis=0)
               for i in range(n_pairs)]
        a_s = [_dot_nt(q2s[i], kps[i].astype(BF16)) for i in range(n_pairs)]
        ocs = [_dot(q2s[i], state[i].astype(BF16)) for i in range(n_pairs)]
        us = [_dot_tn((kps[i] * zeta_ref[:, lsls[i]]).astype(BF16), vpairs[i]) for i in range(n_pairs)]
        unit_done()

        for kvh in range(N_KV_HEADS):
            s, vd, c0 = scores[kvh], vds[kvh], c0s[kvh]
            es, esink = [], []
            for g in range(KV_GROUP):
                h = kvh * KV_GROUP + g
                sg = s[g * BLK:(g + 1) * BLK] + sc["bias"][h]
                if j == 0 and first_mask is not None:
                    sg = sg + first_mask
                sink = sinks_ref[h]
                m = jnp.maximum(jnp.max(sg, axis=-1, keepdims=True), sink)
                es.append(jnp.exp(sg - m).astype(BF16))
                esink.append(jnp.exp(sink - m))
            o = _dot(jnp.concatenate(es, axis=0), vd)
            for pair in range(KV_GROUP // 2):
                oe = o[2 * pair * BLK:(2 * pair + 1) * BLK]
                oo = o[(2 * pair + 1) * BLK:(2 * pair + 2) * BLK]
                num = jnp.where(lowb, oe[:, :LANES], oo[:, LANES:])
                den = (jnp.where(lowb, oe[:, LANES:], oo[:, :LANES])
                       + jnp.where(lowb, esink[2 * pair], esink[2 * pair + 1]))
                cs = c0 + pair * LANES
                mix[slot, rows, cs:cs + LANES] = (num * (1.0 / den)).astype(BF16)
            unit_done()

        for i in range(n_pairs):
            a, oc, u, sp = a_s[i], ocs[i], us[i], state[i]
            inner = jnp.concatenate([a[:BLK] * decay_ref[2 * i], a[BLK:] * decay_ref[2 * i + 1]], axis=0)
            oi = _dot(inner.astype(BF16), vpairs[i])
            for half in range(2):
                h = 2 * i + half
                vsl = slice(h * RET_V_DIM, (h + 1) * RET_V_DIM)
                hr = slice(half * BLK, (half + 1) * BLK)
                o = oi[hr, half * RET_V_DIM:(half + 1) * RET_V_DIM] + oc[hr] * xi_ref[:, vsl]
                r = _group_norm(o, gng_ref[:, vsl], gnb_ref[:, vsl]) * sc["gate"][slot, rows, vsl]
                mix[slot, rows, ATT_Q_W + h * RET_V_DIM:ATT_Q_W + (h + 1) * RET_V_DIM] = r.astype(BF16)
            state[i] = jnp.concatenate(
                [_GL_PROMPT[2 * i] * sp[:RET_QK_DIM] + u[:RET_QK_DIM, :RET_V_DIM],
                 _GL_PROMPT[2 * i + 1] * sp[RET_QK_DIM:] + u[RET_QK_DIM:, RET_V_DIM:]], axis=0)

    assert not pending
    return state


def _pm_wout_pieces(slot, x_ref, rows, wout_ref, h_ref, sc):
    pw = 2 * LANES
    n = D_MODEL // pw
    parts = []

    def piece(k):
        parts.append(_dot(sc["mix"][slot], wout_ref[:, k * pw:(k + 1) * pw]))
        if k == n - 1:
            h_ref[rows, :] = x_ref[...] + jnp.concatenate(parts, axis=1)

    return [functools.partial(piece, k) for k in range(n)]


def _prompt_mixer_kernel(sinks_ref, xfirst_ref, xodd_ref, xnext_ref, gmix_ref, win_ref, wout_ref, gng_ref, gnb_ref,
                           dist_ref, mask_ref, decay_ref, xi_ref, zeta_ref,
                           wupf_ref, wdnf_ref, wqf_ref, wof_ref,
                           h_ref, wk_ref, wv_ref, st_ref,
                           wupb_ref, wdnb_ref, wqb_ref, wob_ref,
                           qlo_s, qhi_s, kd0_s, kd1_s, vd0_s, vd1_s,
                           qrlo_s, qrhi_s, kr_s, vr_s, gate_s, mix_s, bias_s, state_s, xkeep_s):
    u = pl.program_id(0)
    tm = xnext_ref.shape[0]
    wupb_ref[...] = wupf_ref[...].astype(BF16)
    wdnb_ref[...] = wdnf_ref[...].astype(BF16)
    wqb_ref[...] = wqf_ref[...].astype(BF16)
    wob_ref[...] = wof_ref[...].astype(BF16)
    sc = dict(qlo=qlo_s, qhi=qhi_s, kd0=kd0_s, kd1=kd1_s, vd0=vd0_s, vd1=vd1_s, qrlo=qrlo_s, qrhi=qrhi_s,
              kr=kr_s, vr=vr_s, gate=gate_s, mix=mix_s, bias=bias_s)
    n_pairs = N_RET_HEADS // 2
    blocks = functools.partial(_pm_blocks, tm=tm, sinks_ref=sinks_ref, gng_ref=gng_ref,
                               gnb_ref=gnb_ref, decay_ref=decay_ref, xi_ref=xi_ref, zeta_ref=zeta_ref, sc=sc)

    @pl.when(u == 0)
    def _():
        for h in range(N_ATT_HEADS):
            bias_s[h] = NEG_SLOPES[h] * dist_ref[...] + mask_ref[...]
        state_s[...] = jnp.zeros_like(state_s)
        kd0_s[1] = jnp.zeros(kd0_s.shape[1:], BF16)
        kd1_s[1] = jnp.zeros(kd1_s.shape[1:], BF16)
        vd0_s[1] = jnp.zeros(vd0_s.shape[1:], BF16)
        vd1_s[1] = jnp.zeros(vd1_s.shape[1:], BF16)
        xkeep_s[...] = xfirst_ref[...]
        for stage in _pm_project_stages(xfirst_ref[...], 0, gmix_ref, win_ref, sc):
            stage()

    seq_start = (u % 2) == 0
    state = [jnp.where(seq_start, 0.0, state_s[i * LANES:(i + 1) * LANES, :]) for i in range(n_pairs)]

    prev_kd, prev_vd = _pm_last_block(1, tm, sc)
    stages = _pm_project_stages(xodd_ref[...], 1, gmix_ref, win_ref, sc, kv_out=(wk_ref, wv_ref))
    state = blocks(0, prev_kd, prev_vd, seq_start, state, stages)
    wout0 = _pm_wout_pieces(0, xkeep_s, slice(0, tm), wout_ref, h_ref, sc)

    prev_kd, prev_vd = _pm_last_block(0, tm, sc)
    stages = _pm_project_stages(xnext_ref[...], 0, gmix_ref, win_ref, sc)
    state = blocks(1, prev_kd, prev_vd, False, state, wout0 + stages)
    for piece in _pm_wout_pieces(1, xodd_ref, slice(tm, 2 * tm), wout_ref, h_ref, sc):
        piece()
    xkeep_s[...] = xnext_ref[...]

    for i in range(n_pairs):
        state_s[i * LANES:(i + 1) * LANES, :] = state[i]
        st_ref[0, i * LANES:(i + 1) * LANES, :] = state[i]


def _prompt_mixer(x, g_mix, w_in, w_out, sinks, gn_g, gn_b, side_f32):
    b, s, d = x.shape
    tm = TM_MIX
    n_tiles = b * s // tm
    steps = n_tiles // 2
    seq_steps = s // (2 * tm)
    assert s % (2 * tm) == 0 and seq_steps == 2, "kernel assumes 4 tiles per sequence"
    x2d = x.reshape(b * s, d)
    const = lambda shape: pl.BlockSpec(shape, lambda i: (0,) * len(shape), pipeline_mode=pl.Buffered(1))
    slot2 = lambda rows, cols, dt: pltpu.VMEM((2, rows, cols), dt)
    side_specs = [pl.BlockSpec((w.shape[0] // steps, w.shape[1]), lambda i: (i, 0)) for w in side_f32]
    outs = pl.pallas_call(
        _prompt_mixer_kernel,
        grid=(steps,),
        in_specs=[
            pl.BlockSpec(memory_space=pltpu.SMEM),
            const((tm, d)),
            pl.BlockSpec((tm, d), lambda i: (2 * i + 1, 0)),
            pl.BlockSpec((tm, d), lambda i: (jnp.minimum(2 * i + 2, n_tiles - 1), 0)),
            const((1, d)), const((d, D_IN)), const((MIX_OUT, d)),
            const((1, RET_V_W)), const((1, RET_V_W)),
            const((BLK, 2 * BLK)), const((BLK, 2 * BLK)),
            const((N_RET_HEADS, BLK, BLK)), const((BLK, RET_V_W)), const((BLK, RET_QK_W)),
        ] + side_specs,
        out_specs=[
            pl.BlockSpec((2 * tm, d), lambda i: (i, 0)),
            pl.BlockSpec((1, WINDOW, ATT_KV_W), lambda i: (i // seq_steps, 0, 0)),
            pl.BlockSpec((1, WINDOW, ATT_KV_W), lambda i: (i // seq_steps, 0, 0)),
            pl.BlockSpec((1, RET_QK_W, RET_V_DIM), lambda i: (i // seq_steps, 0, 0)),
        ] + side_specs,
        out_shape=[
            jax.ShapeDtypeStruct((b * s, d), F32),
            jax.ShapeDtypeStruct((b, WINDOW, ATT_KV_W), F32),
            jax.ShapeDtypeStruct((b, WINDOW, ATT_KV_W), F32),
            jax.ShapeDtypeStruct((b, RET_QK_W, RET_V_DIM), F32),
        ] + [jax.ShapeDtypeStruct(w.shape, BF16) for w in side_f32],
        scratch_shapes=[
            slot2(tm, ATT_Q_W, BF16), slot2(tm, ATT_Q_W, BF16),
            slot2(tm, LANES, BF16), slot2(tm, LANES, BF16),
            slot2(tm, 2 * LANES, BF16), slot2(tm, 2 * LANES, BF16),
            slot2(tm, RET_QK_W, BF16), slot2(tm, RET_QK_W, BF16),
            slot2(tm, RET_QK_W, F32), slot2(tm, RET_V_W, BF16),
            slot2(tm, RET_V_W, F32), slot2(tm, MIX_OUT, BF16),
            pltpu.VMEM((N_ATT_HEADS, BLK, 2 * BLK), F32),
            pltpu.VMEM((RET_QK_W, RET_V_DIM), F32),
            pltpu.VMEM((tm, d), F32),
        ],
        compiler_params=pltpu.CompilerParams(
            dimension_semantics=("arbitrary",), vmem_limit_bytes=VMEM_LIMIT),
        name="prompt_mixer",
    )(sinks, x2d, x2d, x2d, g_mix, w_in, w_out, gn_g, gn_b,
      jnp.asarray(_P_DIST), jnp.asarray(_P_MASK), jnp.asarray(_P_DECAY), jnp.asarray(_P_XI),
      jnp.asarray(_P_ZETA), *side_f32)
    return (outs[0].reshape(b, s, d),) + tuple(outs[1:])


def _memkv_kernel(mem_ref, g_ref, wk_ref, wv_ref, win_ref, wout_ref,
                  mk_ref, mv_ref, mkb_ref, mvb_ref, winb_ref, woutb_ref):
    winb_ref[...] = win_ref[...].astype(BF16)
    woutb_ref[...] = wout_ref[...].astype(BF16)
    mn = _rms(mem_ref[...], g_ref[...]).astype(BF16)
    mk = _dot(mn, wk_ref[...].astype(BF16))
    mv = _dot(mn, wv_ref[...].astype(BF16))
    tm = mem_ref.shape[0]
    group = X_D_HALVES * N_X_HEADS
    for hd in range(N_X_HEADS):
        for dh in range(X_D_HALVES):
            cols = slice(hd * X_HEAD_DIM + dh * LANES, hd * X_HEAD_DIM + (dh + 1) * LANES)
            rows = pl.ds(dh * N_X_HEADS + hd, tm, stride=group)
            mk_ref[rows, :] = mk[:, cols]
            mv_ref[rows, :] = mv[:, cols]
    mkb_ref[...] = mk.astype(BF16)
    mvb_ref[...] = mv.astype(BF16)


def _memory_kv(mem2d, g_mem, w_xk, w_xv, w_in, w_out):
    n, d = mem2d.shape
    tm = 512
    row = pl.BlockSpec((tm, d), lambda i: (i, 0))
    rows_out = pl.BlockSpec((tm * d // LANES, LANES), lambda i: (i, 0))
    const = lambda shape: pl.BlockSpec(shape, lambda i: (0,) * len(shape))
    steps = n // tm
    win_blk = pl.BlockSpec((w_in.shape[0] // steps, w_in.shape[1]), lambda i: (i, 0))
    wout_blk = pl.BlockSpec((w_out.shape[0] // steps, w_out.shape[1]), lambda i: (i, 0))
    deep = lambda spec: pl.BlockSpec(spec.block_shape, spec.index_map, pipeline_mode=pl.Buffered(3))
    in_specs = [deep(row), const((1, d)), const((d, d)), const((d, d)), deep(win_blk), deep(wout_blk)]
    out_specs = [rows_out, rows_out, row, row, win_blk, wout_blk]

    def outer(*refs):
        pltpu.emit_pipeline(_memkv_kernel, grid=(steps,), in_specs=in_specs, out_specs=out_specs)(*refs)

    hbm = pl.BlockSpec(memory_space=pl.ANY)
    return pl.pallas_call(
        outer,
        in_specs=[hbm] * 6,
        out_specs=[hbm] * 6,
        out_shape=[jax.ShapeDtypeStruct((n * d // LANES, LANES), F32),
                   jax.ShapeDtypeStruct((n * d // LANES, LANES), F32),
                   jax.ShapeDtypeStruct((n, d), BF16), jax.ShapeDtypeStruct((n, d), BF16),
                   jax.ShapeDtypeStruct(w_in.shape, BF16), jax.ShapeDtypeStruct(w_out.shape, BF16)],
        compiler_params=pltpu.CompilerParams(vmem_limit_bytes=VMEM_LIMIT),
        name="memory_kv",
    )(mem2d, g_mem, w_xk, w_xv, w_in, w_out)


def _prompt_xattn_kernel(h_ref, g_ref, wq_ref, wo_ref, mk_ref, mv_ref, out_ref, o_s):
    def stages(r0):
        rows = slice(r0, r0 + SUB_ROWS)
        env = {}

        def project():
            env["h"] = h_ref[0, rows, :]
            xn = _rms(env["h"], g_ref[...]).astype(BF16)
            env["q"] = (_dot(xn, wq_ref[...]) * (X_HEAD_DIM ** -0.5)).astype(BF16)

        def scores(hd):
            sl = slice(hd * X_HEAD_DIM, (hd + 1) * X_HEAD_DIM)
            env[hd] = _dot_nt(env["q"][:, sl], mk_ref[0, :, sl])

        def head(hd):
            if hd + 1 < N_X_HEADS:
                scores(hd + 1)
            sl = slice(hd * X_HEAD_DIM, (hd + 1) * X_HEAD_DIM)
            s = env.pop(hd)
            m = jnp.max(s, axis=-1, keepdims=True)
            p = jnp.exp(s - m)
            p = p * (1.0 / jnp.sum(p, axis=-1, keepdims=True))
            o_s[rows, sl] = _dot(p.astype(BF16), mv_ref[0, :, sl]).astype(BF16)

        def output():
            out_ref[0, rows, :] = env["h"] + _dot(o_s[rows, :], wo_ref[...])

        def project_and_first_scores():
            project()
            scores(0)

        return ([project_and_first_scores] + [functools.partial(head, hd) for hd in range(N_X_HEADS)]
                + [output])

    chains = [stages(r0) for r0 in range(0, h_ref.shape[1], SUB_ROWS)]
    n_stage = len(chains[0])
    for step in range(n_stage + len(chains) - 1):
        for lag, chain in enumerate(chains):
            if 0 <= step - lag < n_stage:
                chain[step - lag]()


def _prompt_xattn(h, g, w_xq, w_xo, mkb, mvb):
    b, s, d = h.shape
    tm = TM_X
    const = lambda shape: pl.BlockSpec(shape, lambda i, j: (0,) * len(shape))
    tok = pl.BlockSpec((1, tm, d), lambda i, j: (i, j, 0))
    mem = pl.BlockSpec((1, N_MEM, d), lambda i, j: (i, 0, 0))
    return pl.pallas_call(
        _prompt_xattn_kernel,
        grid=(b, s // tm),
        in_specs=[tok, const((1, d)), const((d, d)), const((d, d)), mem, mem],
        out_specs=tok,
        out_shape=jax.ShapeDtypeStruct((b, s, d), F32),
        scratch_shapes=[pltpu.VMEM((tm, d), BF16)],
        compiler_params=pltpu.CompilerParams(
            dimension_semantics=("arbitrary", "arbitrary"), vmem_limit_bytes=VMEM_LIMIT),
        name="prompt_xattn",
    )(h, g, w_xq, w_xo, mkb, mvb)


def _sample_mixer_kernel(sinks_ref, x_ref, gmix_ref, win_ref, wout_ref, gng_ref, gnb_ref,
                         ck_ref, cv_ref, st_ref, bias_ref, dec_ref, xi_ref, zeta_ref,
                         h_ref, swk_ref, swv_ref, sst_ref):
    bb = ck_ref.shape[0]
    nt = bb // 2
    x = x_ref[...].reshape(bb * DEC_SEQ, D_MODEL)
    xn = _rms(x, gmix_ref[...]).astype(BF16)
    tile3 = lambda a: a.reshape(nt, SUBLANES, a.shape[-1])

    q = _dot(xn, win_ref[:, C_QA:C_QA + ATT_Q_W]) * (HEAD_DIM ** -0.5)
    kv = _dot(xn, win_ref[:, C_KV:C_KV + 2 * ATT_KV_W])
    qkr = _dot(xn, win_ref[:, C_QKR:C_QKR + 2 * RET_QK_W])
    vr = _dot(xn, win_ref[:, C_VR:C_VR + RET_V_W])
    gate3 = tile3(_silu(_dot(xn, win_ref[:, C_GR:C_GR + RET_V_W])))

    lo512, hi512 = _half_masks(ATT_Q_W)
    q_r = pltpu.roll(q, HALF, axis=1)
    q_nat3 = tile3(q)
    q_rot3 = tile3(q_r)
    lo3 = lo512.reshape(1, 1, ATT_Q_W)
    hi3 = hi512.reshape(1, 1, ATT_Q_W)
    qa3 = (q_nat3 * lo3).astype(BF16)
    qb3 = (q_rot3 * lo3).astype(BF16)
    qc3 = (q_rot3 * hi3).astype(BF16)
    qd3 = (q_nat3 * hi3).astype(BF16)
    t128 = lambda a, i: a[:, :, i * LANES:(i + 1) * LANES]
    qs = jnp.concatenate([t128(qa3, 0), t128(qb3, 1), t128(qa3, 1), t128(qb3, 2),
                          t128(qc3, 2), t128(qd3, 2), t128(qc3, 3), t128(qd3, 3)], axis=1)

    k3 = tile3(kv[:, :ATT_KV_W])
    v3 = tile3(kv[:, ATT_KV_W:])
    pad_kv = jnp.zeros((nt, BLK - SUBLANES, LANES), BF16)
    knew_pad = jnp.concatenate([k3.astype(BF16), pad_kv], axis=1)
    vnew_pad = jnp.concatenate([v3.astype(BF16), pad_kv], axis=1)
    to_lanes = lambda a3: jnp.swapaxes(
        jnp.concatenate([a3, jnp.zeros((nt, BLK - SUBLANES, LANES), F32)], axis=1), 1, 2)
    k3t, v3t = to_lanes(k3), to_lanes(v3)
    roll3 = lambda a, sh: pltpu.roll(a.reshape(nt * BLK, LANES), sh, axis=1).reshape(nt, BLK, LANES)

    lo256, _ = _half_masks(RET_QK_W)
    qr3 = tile3(qkr[:, :RET_QK_W])
    kr3 = tile3(qkr[:, RET_QK_W:] * (RET_QK_DIM ** -0.5))
    vr3 = tile3(vr)
    lane256 = lax.broadcasted_iota(jnp.int32, (1, 1, RET_QK_W), 2)
    qrs = jnp.concatenate(
        [(qr3 * ((lane256 >= h * RET_QK_DIM) & (lane256 < (h + 1) * RET_QK_DIM)).astype(F32)).astype(BF16)
         for h in range(N_RET_HEADS)],
        axis=1)
    kr_pad = jnp.concatenate([kr3.astype(BF16), jnp.zeros((nt, BLK - SUBLANES, RET_QK_W), BF16)], axis=1)
    vr_pad = jnp.concatenate([vr3.astype(BF16), jnp.zeros((nt, BLK - SUBLANES, RET_V_W), BF16)], axis=1)

    lane = lax.broadcasted_iota(jnp.int32, (1, 1, LANES), 2)
    row8 = lax.broadcasted_iota(jnp.int32, (1, SUBLANES, 1), 1)
    bmm_nt = lambda a, b: jnp.einsum('bqd,bkd->bqk', a, b, preferred_element_type=F32)
    bmm = lambda a, b: jnp.einsum('bqk,bkd->bqd', a, b, preferred_element_type=F32)

    att_par, ret_par = [], []
    for par in range(2):
        bsl = pl.ds(par, nt, stride=2)
        ckt = ck_ref[bsl]
        cvt = cv_ref[bsl]
        keep = lane < WINDOW - DEC_SEQ
        new_shift = WINDOW - DEC_SEQ - DEC_SEQ * par
        swk_ref[bsl] = jnp.where(keep, roll3(ckt, WINDOW - DEC_SEQ), roll3(k3t, new_shift))
        swv_ref[bsl] = jnp.where(keep, roll3(cvt, WINDOW - DEC_SEQ), roll3(v3t, new_shift))

        s = jnp.concatenate([bmm(qs, ckt.astype(BF16)), bmm_nt(qs, knew_pad)], axis=2) + bias_ref[par]
        ps = []
        for h in range(N_ATT_HEADS):
            ps.append(_sink_softmax(s[:, h * SUBLANES:(h + 1) * SUBLANES, :], sinks_ref[h]).astype(BF16))
        p_all = jnp.concatenate(ps, axis=1)
        o = bmm_nt(p_all[:, :, :BLK], cvt.astype(BF16)) + bmm(p_all[:, :, BLK:], vnew_pad)
        o_r = pltpu.roll(o.reshape(nt * N_ATT_HEADS * SUBLANES, LANES), HALF, axis=1).reshape(o.shape)
        hr = lambda a, h: a[:, h * SUBLANES:(h + 1) * SUBLANES, :]
        low = lane < HALF
        att_par.append(jnp.concatenate([
            jnp.where(low, hr(o, 0), hr(o_r, 1)), jnp.where(low, hr(o, 2), hr(o_r, 3)),
            jnp.where(low, hr(o_r, 4), hr(o, 5)), jnp.where(low, hr(o_r, 6), hr(o, 7))], axis=2))

        st = st_ref[bsl]
        oc = bmm(qrs, st.astype(BF16))
        inner = (bmm_nt(qrs, kr_pad) * dec_ref[par]).astype(BF16)
        oi = bmm(inner, vr_pad)
        rs = []
        for h in range(N_RET_HEADS):
            vsl = slice(h * RET_V_DIM, (h + 1) * RET_V_DIM)
            rsl = slice(h * SUBLANES, (h + 1) * SUBLANES)
            o_h = oi[:, rsl, vsl] + oc[:, rsl, :] * xi_ref[par, rsl, :]
            rs.append(_group_norm(o_h, gng_ref[:, vsl], gnb_ref[:, vsl]) * gate3[:, :, vsl])
        ret_par.append(jnp.concatenate(rs, axis=2))

        kz3 = (kr3 * zeta_ref[par]).astype(BF16)
        vr3_b = vr3.astype(BF16)
        for p in range(nt):
            for i in range(N_RET_HEADS // 2):
                u = _dot_tn(kz3[p][:, i * LANES:(i + 1) * LANES],
                            vr3_b[p][:, 2 * i * RET_V_DIM:(2 * i + 2) * RET_V_DIM])
                for half in range(2):
                    h = 2 * i + half
                    dsl = slice(h * RET_QK_DIM, (h + 1) * RET_QK_DIM)
                    sst_ref[2 * p + par, dsl, :] = (
                        _GL_SAMPLE[h] * st[p, dsl, :]
                        + u[half * RET_QK_DIM:(half + 1) * RET_QK_DIM, half * RET_V_DIM:(half + 1) * RET_V_DIM])

    own0 = row8 < DEC_SEQ
    att3 = jnp.where(own0, att_par[0], att_par[1])
    ret3 = jnp.where(own0, ret_par[0], ret_par[1])
    mix = jnp.concatenate([att3, ret3], axis=2).reshape(2 * nt * DEC_SEQ, MIX_OUT).astype(BF16)
    h_ref[...] = x + _dot(mix, wout_ref[...])


def _sample_mixer(x3d, g_mix, w_in, w_out, sinks, gn_g, gn_b, ck, cv, st):
    nb, ls, d = x3d.shape
    n = nb * ls
    bb = BB_MIX
    r = bb * DEC_SEQ
    const = lambda shape: pl.BlockSpec(shape, lambda i: (0,) * len(shape))
    row = pl.BlockSpec((r, d), lambda i: (i, 0))
    win = pl.BlockSpec((bb, WINDOW, ATT_KV_W), lambda i: (i, 0, 0))
    state = pl.BlockSpec((bb, RET_QK_W, RET_V_DIM), lambda i: (i, 0, 0))
    deep = lambda spec: pl.BlockSpec(spec.block_shape, spec.index_map, pipeline_mode=pl.Buffered(3))
    in_specs = [
        pl.BlockSpec((bb, ls, d), lambda i: (i, 0, 0)), const((1, d)), const((d, D_IN)), const((MIX_OUT, d)),
        const((1, RET_V_W)), const((1, RET_V_W)),
        win, win, deep(state),
        const(_S_BIAS.shape), const(_S_DEC.shape), const(_S_XI.shape), const(_S_ZETA.shape),
    ]
    out_specs = [row, win, win, state]

    def outer(sinks_ref, *refs):
        pltpu.emit_pipeline(functools.partial(_sample_mixer_kernel, sinks_ref), grid=(nb // bb,),
                            in_specs=in_specs, out_specs=out_specs)(*refs)

    hbm = pl.BlockSpec(memory_space=pl.ANY)
    return pl.pallas_call(
        outer,
        in_specs=[pl.BlockSpec(memory_space=pltpu.SMEM)] + [hbm] * 13,
        out_specs=[hbm] * 4,
        out_shape=[
            jax.ShapeDtypeStruct((n, d), F32),
            jax.ShapeDtypeStruct((nb, WINDOW, ATT_KV_W), F32),
            jax.ShapeDtypeStruct((nb, WINDOW, ATT_KV_W), F32),
            jax.ShapeDtypeStruct((nb, RET_QK_W, RET_V_DIM), F32),
        ],
        compiler_params=pltpu.CompilerParams(vmem_limit_bytes=VMEM_LIMIT),
        name="sample_mixer",
    )(sinks, x3d, g_mix, w_in, w_out, gn_g, gn_b, ck, cv, st,
      jnp.asarray(_S_BIAS), jnp.asarray(_S_DEC), jnp.asarray(_S_XI), jnp.asarray(_S_ZETA))


def _head_slab(x_ref, b, hd):
    group = X_D_HALVES * N_X_HEADS
    halves = [x_ref[b, pl.ds(dh * N_X_HEADS + hd, N_MEM, stride=group), :] for dh in range(X_D_HALVES)]
    return jnp.concatenate(halves, axis=1).astype(BF16)


def _mlp_value(h, g_ref, wup_ref, wdn_ref, gf_ref, fillers=None):
    xn = _rms(h, g_ref[...]).astype(BF16)
    piece = FF_CHUNK // N_X_HEADS
    opiece = D_MODEL // N_X_HEADS
    n_chunks = D_FF // FF_CHUNK
    nofill = (None, None, None)

    def up(c):
        qk, softmax, _ = fillers[c] if fillers is not None else nofill
        hid = []
        for k in range(N_X_HEADS):
            cols = slice(c * FF_CHUNK + k * piece, c * FF_CHUNK + (k + 1) * piece)
            u = jnp.maximum(_dot(xn, wup_ref[:, cols]), 0.0)
            hid.append((u * u).astype(BF16))
            if qk is not None:
                qk(k)
        if softmax is not None:
            softmax()
        return jnp.concatenate(hid, axis=1)

    def down(c, hid):
        pv = (fillers[c] if fillers is not None else nofill)[2]
        rows_c = slice(c * FF_CHUNK, (c + 1) * FF_CHUNK)
        out = []
        for k in range(N_X_HEADS):
            out.append(_dot(hid, wdn_ref[rows_c, k * opiece:(k + 1) * opiece]))
            if pv is not None:
                pv(k)
        return jnp.concatenate(out, axis=1)

    acc = h
    hid = up(0)
    for c in range(n_chunks):
        nxt = up(c + 1) if c + 1 < n_chunks else None
        acc = acc + down(c, hid)
        hid = nxt
    return _rms(acc, gf_ref[...])


def _mlp_xattn_kernel(hp_ref, hsm_ref, gx_ref, wq_ref, wo_ref, xk_ref, xv_ref, g_ref, wup_ref, wdn_ref, gf_ref,
                      yp_ref, ys_ref):
    i = pl.program_id(0)
    n = pl.num_programs(0) - 1
    bb = xk_ref.shape[0]
    assert bb == D_FF // FF_CHUNK and bb % 2 == 0

    @pl.when(i == 0)
    def _():
        xn = _rms(hsm_ref[...], gx_ref[...]).astype(BF16)
        ys_ref[...] = (_dot(xn, wq_ref[...]) * (X_HEAD_DIM ** -0.5)).reshape(ys_ref.shape)

    @pl.when(i < n)
    def _():
        own0 = lax.broadcasted_iota(jnp.int32, (SUBLANES, 1), 0) < DEC_SEQ
        o_rows = {}

        def attend(b):
            t = b // 2
            tile_b = pl.ds(i * bb + 2 * t, 2)
            env = dict(s=[], o=[])

            def qk(hd):
                if hd == 0:
                    env["q"] = ys_ref[tile_b].reshape(SUBLANES, D_MODEL).astype(BF16)
                env["s"].append(_dot_nt(env["q"][:, hd * X_HEAD_DIM:(hd + 1) * X_HEAD_DIM],
                                        _head_slab(xk_ref, b, hd)))

            def softmax():
                s = jnp.concatenate(env["s"], axis=0)
                m = jnp.max(s, axis=-1, keepdims=True)
                p = jnp.exp(s - m)
                env["p"] = p * (1.0 / jnp.sum(p, axis=-1, keepdims=True))

            def pv(hd):
                p = env["p"][hd * SUBLANES:(hd + 1) * SUBLANES].astype(BF16)
                env["o"].append(_dot(p, _head_slab(xv_ref, b, hd)))
                if hd == N_X_HEADS - 1:
                    o_rows[b] = jnp.concatenate(env["o"], axis=1)
                    if b % 2 == 1:
                        ys_ref[tile_b] = jnp.where(own0, o_rows[b - 1], o_rows[b]).reshape(2, DEC_SEQ, D_MODEL)

            return qk, softmax, pv

        fillers = [attend(b) for b in range(bb)]
        yp_ref[...] = _mlp_value(hp_ref[...], g_ref, wup_ref, wdn_ref, gf_ref, fillers)

    @pl.when(i == n)
    def _():
        o = ys_ref[...].reshape(hsm_ref.shape).astype(BF16)
        hs = hsm_ref[...] + _dot(o, wo_ref[...])
        ys_ref[...] = _mlp_value(hs, g_ref, wup_ref, wdn_ref, gf_ref).reshape(ys_ref.shape)


def _mlp_xattn(hp2d, hsm, g_xattn, w_xq, w_xo, xk, xv, g_mlp, w_up, w_down, g_final):
    n, d = hp2d.shape
    ns = hsm.shape[0]
    nb = xk.shape[0]
    bb = BB_X
    tm = n // (nb // bb)
    n_tiles = n // tm
    assert n_tiles * bb == nb and tm % SUBLANES == 0
    clip = lambda i: jnp.minimum(i, n_tiles - 1)
    prompt = pl.BlockSpec((tm, d), lambda i: (clip(i), 0))
    mem = pl.BlockSpec((bb,) + xk.shape[1:], lambda i: (clip(i), 0, 0))
    const = lambda shape: pl.BlockSpec(shape, lambda i: (0,) * len(shape), pipeline_mode=pl.Buffered(1))
    return pl.pallas_call(
        _mlp_xattn_kernel,
        grid=(n_tiles + 1,),
        in_specs=[prompt, const((ns, d)), const((1, d)), const((d, d)), const((d, d)), mem, mem,
                  const((1, d)), const((d, D_FF)), const((D_FF, d)), const((1, d))],
        out_specs=[prompt, pl.BlockSpec((ns // DEC_SEQ, DEC_SEQ, d), lambda i: (0, 0, 0))],
        out_shape=[jax.ShapeDtypeStruct((n, d), F32), jax.ShapeDtypeStruct((ns // DEC_SEQ, DEC_SEQ, d), F32)],
        compiler_params=pltpu.CompilerParams(
            dimension_semantics=("arbitrary",), vmem_limit_bytes=VMEM_LIMIT),
        name="mlp_xattn",
    )(hp2d, hsm, g_xattn, w_xq, w_xo, xk, xv, g_mlp, w_up, w_down, g_final)


def _mem_rows(c):
    nb = c.shape[0]
    c = c.reshape(nb, N_MEM, N_X_HEADS, X_D_HALVES, LANES)
    return jnp.transpose(c, (0, 1, 3, 2, 4)).reshape(nb, N_MEM * X_D_HALVES * N_X_HEADS, LANES)


def kernel(x_prompt, x_sample, mem_prompt, cache_win_k, cache_win_v, state_ret, cache_mem_k, cache_mem_v,
           g_mix, w_in, attn_sinks, ret_gn_g, ret_gn_b, w_out, g_xattn, g_mem, w_xq, w_xk, w_xv, w_xo,
           g_mlp, w_up, w_down, g_final):
    depth = w_in.shape[0]
    assert depth == 1, "single-layer trunk"
    b, s, d = x_prompt.shape
    nb, ls, _ = x_sample.shape
    row = lambda a: a.reshape(1, -1)
    sinks = attn_sinks[0]
    gn_g, gn_b = row(ret_gn_g[0]), row(ret_gn_b[0])
    g_fin = row(g_final)

    mk, mv, mkb, mvb, w_in_b, w_out_b = _memory_kv(
        mem_prompt.reshape(b * N_MEM, d), row(g_mem[0]), w_xk[0], w_xv[0], w_in[0], w_out[0])
    hp, p_wk, p_wv, p_rs, w_up_b, w_dn_b, w_xq_b, w_xo_b = _prompt_mixer(
        x_prompt, row(g_mix[0]), w_in_b, w_out_b, sinks, gn_g, gn_b,
        (w_up[0], w_down[0], w_xq[0], w_xo[0]))
    hp = _prompt_xattn(hp, row(g_xattn[0]), w_xq_b, w_xo_b,
                       mkb.reshape(b, N_MEM, d), mvb.reshape(b, N_MEM, d))

    win_t = lambda c: jnp.transpose(c, (0, 2, 3, 1)).reshape(nb, ATT_KV_W, WINDOW)
    win_t_inv = lambda a: jnp.transpose(a.reshape(-1, N_KV_HEADS, HEAD_DIM, WINDOW),
                                        (0, 3, 1, 2)).reshape(1, -1, WINDOW, N_KV_HEADS, HEAD_DIM)
    hs, s_wk, s_wv, s_rs = _sample_mixer(
        x_sample, row(g_mix[0]), w_in_b, w_out_b, sinks, gn_g, gn_b,
        win_t(cache_win_k[0]), win_t(cache_win_v[0]), state_ret[0].reshape(nb, RET_QK_W, RET_V_DIM))

    y_prompt, y_sample = _mlp_xattn(
        hp.reshape(b * s, d), hs, row(g_xattn[0]), w_xq_b, w_xo_b,
        _mem_rows(cache_mem_k[0]), _mem_rows(cache_mem_v[0]), row(g_mlp[0]), w_up_b, w_dn_b, g_fin)
    y_prompt = y_prompt.reshape(b, s, d)

    ret5 = lambda a, n: a.reshape(1, n, N_RET_HEADS, RET_QK_DIM, RET_V_DIM)
    mem5 = lambda a: jnp.transpose(a.reshape(b, N_MEM, X_D_HALVES, N_X_HEADS, LANES),
                                   (0, 1, 3, 2, 4)).reshape(1, b, N_MEM, N_X_HEADS, X_HEAD_DIM)
    return (y_prompt, y_sample,
            win_t_inv(p_wk), win_t_inv(p_wv), ret5(p_rs, b), mem5(mk), mem5(mv),
            win_t_inv(s_wk), win_t_inv(s_wv), ret5(s_rs, nb))
```
